```python
import jax, jax.numpy as jnp
from jax import lax
import numpy as np

D_MODEL = 1024
BATCH = 8
SEQ = 2048
DEPTH = 1
DEC_BATCH = 128
DEC_SEQ = 8
PAST_LEN = 16384
PAGE_SIZE = 128

N_META = 16
D_MIX = D_MODEL
D_A = D_MIX // 2
H_A = 4
DH_A = D_A // H_A
D_B = D_MIX - D_A
H_B = 4
DH_B = D_B // H_B
D_IN = 4 * D_A + 2 * H_A + 4 * D_B
D_FF = 2816
CONV_W = 3
CHUNK = 64
ALPHA = (2.0 * DEPTH) ** 0.25
BETA = (8.0 * DEPTH) ** -0.25
LN_EPS = 1e-5
RMS_EPS = 1e-6

kernel_name = 'hymba_mlstm_hgrn2_convffn_step'


def layer_norm(x, g, b):
    xf = x.astype(jnp.float32)
    mu = xf.mean(-1, keepdims=True)
    var = jnp.square(xf - mu).mean(-1, keepdims=True)
    return ((xf - mu) * lax.rsqrt(var + LN_EPS) * g + b).astype(x.dtype)


def head_rms(h, g):
    return h * lax.rsqrt(jnp.mean(h * h, -1, keepdims=True) + RMS_EPS) * g


def mlstm_chunk(carry, xs):
    C, n, m0 = carry
    q, k, v, ig, lf = xs
    T = q.shape[1]
    b = jnp.cumsum(lf, axis=1).transpose(0, 2, 1)
    igh = ig.transpose(0, 2, 1)
    causal = jnp.tril(jnp.ones((T, T), dtype=bool))
    D = jnp.where(causal, b[..., :, None] - b[..., None, :] + igh[..., None, :], -jnp.inf)
    m_t = jnp.maximum(b + m0[..., None], D.max(-1))
    dec = jnp.exp(b + m0[..., None] - m_t)
    W = jnp.exp(D - m_t[..., None])
    Sw = W * jnp.einsum('bthd,bshd->bhts', q, k)
    num = dec[..., None] * jnp.einsum('bthd,bhde->bhte', q, C) + jnp.einsum('bhts,bshe->bhte', Sw, v)
    den = dec * jnp.einsum('bthd,bhd->bht', q, n) + Sw.sum(-1)
    h = num / jnp.maximum(jnp.abs(den), jnp.exp(-m_t))[..., None]
    wT = W[:, :, -1, :]
    C_new = dec[..., -1, None, None] * C + jnp.einsum('bhs,bshd,bshe->bhde', wT, k, v)
    n_new = dec[..., -1, None] * n + jnp.einsum('bhs,bshd->bhd', wT, k)
    return (C_new, n_new, m_t[..., -1]), h.transpose(0, 2, 1, 3)


def hgrn_chunk(S, xs):
    q, lf, k, iv = xs
    T = q.shape[1]
    a = jnp.cumsum(lf, axis=1)
    causal = jnp.tril(jnp.ones((T, T), dtype=bool))[None, :, :, None, None]
    decay = jnp.exp(jnp.where(causal, a[:, :, None] - a[:, None, :], -jnp.inf))
    scores = jnp.einsum('bthc,btshc,bshc->bhts', q, decay, k)
    o = jnp.einsum('bthc,bhce->bthe', q * jnp.exp(a), S) + jnp.einsum('bhts,bshe->bthe', scores, iv)
    aT = a[:, -1]
    wk = jnp.exp(aT[:, None] - a) * k
    S_new = jnp.exp(aT)[..., None] * S + jnp.einsum('bshc,bshe->bhce', wk, iv)
    return S_new, o


def run_causal(fn, state, xs, lead):
    if lead is None:
        return fn(state, xs)
    state, out_head = fn(state, tuple(a[:, :lead] for a in xs))
    rest = tuple(a[:, lead:] for a in xs)
    Bn, L = rest[0].shape[0], rest[0].shape[1]
    n_c = L // CHUNK
    xs_c = tuple(jnp.moveaxis(a.reshape(Bn, n_c, CHUNK, *a.shape[2:]), 1, 0) for a in rest)
    state, outs = lax.scan(fn, state, xs_c)
    outs = jnp.moveaxis(outs, 0, 1).reshape(Bn, L, *outs.shape[3:])
    return state, jnp.concatenate([out_head, outs], axis=1)


def decoder_layer(x, lb, lead, mstate, hstate, conv_state, w_in, b_in, b_fgate_a, g_norm_a, g_norm_b,
                  w_out, b_out, ln1_g, ln1_b, w_up, b_up, w_conv, b_conv, w_down, b_down, ln2_g, ln2_b):
    Bn, T, _ = x.shape
    proj = (x @ w_in + b_in).astype(jnp.float32)
    splits = [int(s) for s in np.cumsum([D_A, D_A, D_A, D_A, H_A, H_A, D_B, D_B, D_B])]
    qa, ka, va, oa, ia, fa, qb, fb, ib, gb = jnp.split(proj, splits, axis=-1)
    hd = lambda t, H: t.reshape(Bn, T, H, -1)
    qa = hd(qa, H_A)
    ka = hd(ka, H_A) * (DH_A ** -0.5)
    va = hd(va, H_A)
    lfa = jax.nn.log_sigmoid(fa + b_fgate_a.astype(jnp.float32))
    mstate, h_a = run_causal(mlstm_chunk, mstate, (qa, ka, va, ia, lfa), lead)
    h_a = jax.nn.sigmoid(hd(oa, H_A)) * head_rms(h_a, g_norm_a.astype(jnp.float32))
    lbh = lb.reshape(H_B, DH_B)
    fb = hd(fb, H_B)
    lfb = jnp.log(lbh + (1.0 - lbh) * jax.nn.sigmoid(fb))
    kb = (1.0 - lbh) * jax.nn.sigmoid(-fb)
    qb = jax.nn.silu(hd(qb, H_B))
    ib = hd(ib, H_B)
    hstate, o_b = run_causal(hgrn_chunk, hstate, (qb, lfb, kb, ib), lead)
    h_b = jax.nn.sigmoid(hd(gb, H_B)) * head_rms(o_b, g_norm_b.astype(jnp.float32))
    mix = jnp.concatenate([h_a.reshape(Bn, T, D_A), h_b.reshape(Bn, T, D_B)], -1).astype(x.dtype)
    x = layer_norm(ALPHA * x + (mix @ w_out + b_out), ln1_g, ln1_b)
    up = x @ w_up + b_up
    u, gate = jnp.split(up, 2, axis=-1)
    full = jnp.concatenate([conv_state.astype(u.dtype), u], axis=1)
    conv = b_conv + full[:, 0:T] * w_conv[0]
    for j in range(1, CONV_W):
        conv = conv + full[:, j:j + T] * w_conv[j]
    new_conv = full[:, T:]
    ffn = (jax.nn.silu(conv) * gate) @ w_down + b_down
    x = layer_norm(ALPHA * x + ffn, ln2_g, ln2_b)
    return x, mstate, hstate, new_conv


def setup_inputs(seed: int = 0) -> dict:
    key = jax.random.key(seed)
    ks = jax.random.split(key, 32)
    nrm = lambda k, s, sc: jax.random.normal(k, s, jnp.float32) * sc
    f32 = jnp.float32
    return {
        'x_prompt': nrm(ks[0], (BATCH, SEQ, D_MODEL), 1.0),
        'x_sample': nrm(ks[1], (DEC_BATCH, DEC_SEQ, D_MODEL), 1.0),
        'state_mlstm_C': nrm(ks[2], (DEPTH, DEC_BATCH, H_A, DH_A, DH_A), 0.1),
        'state_mlstm_n': nrm(ks[3], (DEPTH, DEC_BATCH, H_A, DH_A), 0.5),
        'state_mlstm_m': nrm(ks[4], (DEPTH, DEC_BATCH, H_A), 1.0),
        'state_hgrn_S': nrm(ks[5], (DEPTH, DEC_BATCH, H_B, DH_B, DH_B), 0.1),
        'state_ffn_conv': nrm(ks[6], (DEPTH, DEC_BATCH, CONV_W - 1, D_FF), 1.0),
        'meta_tokens': nrm(ks[7], (N_META, D_MODEL), 1.0),
        'ln_emb_g': 1.0 + nrm(ks[8], (D_MODEL,), 0.02),
        'ln_emb_b': nrm(ks[9], (D_MODEL,), 0.02),
        'w_in': nrm(ks[10], (DEPTH, D_MODEL, D_IN), D_MODEL ** -0.5),
        'b_in': nrm(ks[11], (DEPTH, D_IN), 0.02),
        'b_fgate_a': jnp.broadcast_to(jnp.linspace(3.0, 6.0, H_A, dtype=f32), (DEPTH, H_A)) + nrm(ks[12], (DEPTH, H_A), 0.1),
        'g_norm_a': 1.0 + nrm(ks[13], (DEPTH, H_A, DH_A), 0.02),
        'g_norm_b': 1.0 + nrm(ks[14], (DEPTH, H_B, DH_B), 0.02),
        'hgrn_lb_logits': nrm(ks[15], (DEPTH + 1, D_B), 0.1),
        'w_out': nrm(ks[16], (DEPTH, D_MIX, D_MODEL), BETA * D_MIX ** -0.5),
        'b_out': nrm(ks[17], (DEPTH, D_MODEL), 0.02),
        'ln1_g': 1.0 + nrm(ks[18], (DEPTH, D_MODEL), 0.02),
        'ln1_b': nrm(ks[19], (DEPTH, D_MODEL), 0.02),
        'w_up': nrm(ks[20], (DEPTH, D_MODEL, 2 * D_FF), D_MODEL ** -0.5),
        'b_up': nrm(ks[21], (DEPTH, 2 * D_FF), 0.02),
        'w_conv': nrm(ks[22], (DEPTH, CONV_W, D_FF), CONV_W ** -0.5),
        'b_conv': nrm(ks[23], (DEPTH, D_FF), 0.02),
        'w_down': nrm(ks[24], (DEPTH, D_FF, D_MODEL), BETA * D_FF ** -0.5),
        'b_down': nrm(ks[25], (DEPTH, D_MODEL), 0.02),
        'ln2_g': 1.0 + nrm(ks[26], (DEPTH, D_MODEL), 0.02),
        'ln2_b': nrm(ks[27], (DEPTH, D_MODEL), 0.02),
    }


def reference(x_prompt, x_sample, state_mlstm_C, state_mlstm_n, state_mlstm_m, state_hgrn_S, state_ffn_conv,
              meta_tokens, ln_emb_g, ln_emb_b, w_in, b_in, b_fgate_a, g_norm_a, g_norm_b, hgrn_lb_logits,
              w_out, b_out, ln1_g, ln1_b, w_up, b_up, w_conv, b_conv, w_down, b_down, ln2_g, ln2_b):
    f32 = jnp.float32
    Bp = x_prompt.shape[0]
    lbs = jnp.cumsum(jax.nn.softmax(hgrn_lb_logits.astype(f32), axis=0), axis=0)
    meta = jnp.broadcast_to(meta_tokens[None].astype(x_prompt.dtype), (Bp, N_META, D_MODEL))
    xp = layer_norm(jnp.concatenate([meta, x_prompt], axis=1), ln_emb_g, ln_emb_b)
    xs = layer_norm(x_sample, ln_emb_g, ln_emb_b)
    Cp, np_, mp, Sp, cvp = [], [], [], [], []
    Cs, ns, ms, Ss, cvs = [], [], [], [], []
    for l in range(DEPTH):
        prm = (w_in[l], b_in[l], b_fgate_a[l], g_norm_a[l], g_norm_b[l], w_out[l], b_out[l], ln1_g[l], ln1_b[l],
               w_up[l], b_up[l], w_conv[l], b_conv[l], w_down[l], b_down[l], ln2_g[l], ln2_b[l])
        m0 = (jnp.zeros((Bp, H_A, DH_A, DH_A), f32), jnp.zeros((Bp, H_A, DH_A), f32), jnp.zeros((Bp, H_A), f32))
        h0 = jnp.zeros((Bp, H_B, DH_B, DH_B), f32)
        c0 = jnp.zeros((Bp, CONV_W - 1, D_FF), xp.dtype)
        xp, (c_p, n_p, m_p), s_p, cv_p = decoder_layer(xp, lbs[l], N_META, m0, h0, c0, *prm)
        ms0 = (state_mlstm_C[l].astype(f32), state_mlstm_n[l].astype(f32), state_mlstm_m[l].astype(f32))
        xs, (c_s, n_s, m_s), s_s, cv_s = decoder_layer(xs, lbs[l], None, ms0, state_hgrn_S[l].astype(f32),
                                                       state_ffn_conv[l], *prm)
        Cp.append(c_p); np_.append(n_p); mp.append(m_p); Sp.append(s_p); cvp.append(cv_p)
        Cs.append(c_s); ns.append(n_s); ms.append(m_s); Ss.append(s_s); cvs.append(cv_s)
    y_prompt = xp[:, N_META:]
    y_sample = xs
    return (y_prompt, y_sample,
            jnp.stack(Cp), jnp.stack(np_), jnp.stack(mp), jnp.stack(Sp), jnp.stack(cvp),
            jnp.stack(Cs), jnp.stack(ns), jnp.stack(ms), jnp.stack(Ss), jnp.stack(cvs))
```

```python
import functools

import jax
import jax.numpy as jnp
from jax import lax
from jax.experimental import pallas as pl
from jax.experimental.pallas import tpu as pltpu

D_MODEL = 1024
N_META = 16
N_HEADS = 4
D_HEAD = 128
D_GROUP = N_HEADS * D_HEAD
D_FF = 2816
CONV_W = 3
DEPTH = 1
ALPHA = (2.0 * DEPTH) ** 0.25
LN_EPS = 1e-5
RMS_EPS = 1e-6

LANES = 128
SUBLANES = 8
GATE_COL = 8 * D_GROUP
D_PROJ = GATE_COL + LANES
VMEM_LIMIT = 56 * 1024 * 1024

F32 = jnp.float32
BF16 = jnp.bfloat16
HIGHEST = lax.Precision.HIGHEST
NT_DIMS = (((1,), (1,)), ((), ()))
TN_DIMS = (((0,), (0,)), ((), ()))


def _layer_norm(x, g, b):
    mu = jnp.mean(x, axis=-1, keepdims=True)
    xc = x - mu
    var = jnp.mean(xc * xc, axis=-1, keepdims=True)
    return xc * lax.rsqrt(var + LN_EPS) * g + b


def _dot(a, b):
    return jnp.dot(a, b, preferred_element_type=F32)


def _dot_nt(a, b, precision=None):
    return lax.dot_general(a, b, NT_DIMS, precision=precision, preferred_element_type=F32)


def _dot_tn(a, b):
    return lax.dot_general(a, b, TN_DIMS, preferred_element_type=F32)


def _in_proj_kernel(x_ref, g_ref, b_ref, w_ref, bias_ref, o_ref):
    xn = _layer_norm(x_ref[...], g_ref[...], b_ref[...])
    o_ref[...] = _dot(xn.astype(BF16), w_ref[...]) + bias_ref[...]


def _in_proj(x, ln_g, ln_b, w, bias, *, tm):
    n = x.shape[0]
    const = lambda i: (0, 0)
    return pl.pallas_call(
        _in_proj_kernel,
        grid=(n // tm,),
        in_specs=[
            pl.BlockSpec((tm, D_MODEL), lambda i: (i, 0)),
            pl.BlockSpec((1, D_MODEL), const),
            pl.BlockSpec((1, D_MODEL), const),
            pl.BlockSpec((D_MODEL, D_PROJ), const),
            pl.BlockSpec((1, D_PROJ), const),
        ],
        out_specs=pl.BlockSpec((tm, D_PROJ), lambda i: (i, 0)),
        out_shape=jax.ShapeDtypeStruct((n, D_PROJ), F32),
        compiler_params=pltpu.CompilerParams(
            dimension_semantics=("arbitrary",), vmem_limit_bytes=VMEM_LIMIT),
        name="in_proj",
    )(x, ln_g, ln_b, w, bias)


def _hgrn_level_reference(a, level, t_len):
    h = 1 << level
    if 2 * h > SUBLANES:
        pieces = []
        for j in range(t_len // (2 * h)):
            row = a[2 * h * j + h - 1:2 * h * j + h, :]
            pieces.append(jnp.broadcast_to(row, (2 * h, a.shape[1])))
        return pieces[0] if len(pieces) == 1 else jnp.concatenate(pieces, axis=0)
    rows = lax.broadcasted_iota(jnp.int32, a.shape, 0)
    delta = (rows & (2 * h - 1)) - (h - 1)
    ref = a
    for d in range(1, h + 1):
        ref = jnp.where(delta == d, pltpu.roll(a, d, axis=0), ref)
    for d in range(1, h):
        ref = jnp.where(delta == -d, pltpu.roll(a, t_len - d, axis=0), ref)
    return ref


def _mixer_kernel(p_ref, c0_ref, n0_ref, m0_ref, s0_ref, bf_ref, ga_ref, gb_ref, lb_ref,
                  mix_ref, c_ref, n_ref, m_ref, s_ref, *, t_len, n_chunks):
    chunk = pl.program_id(1)

    @pl.when(chunk == 0)
    def _():
        c_ref[...] = c0_ref[...]
        n_ref[...] = n0_ref[...]
        m_ref[...] = m0_ref[...]
        for h in range(N_HEADS):
            s_ref[0, h] = s0_ref[0, h].T

    rows = lax.broadcasted_iota(jnp.int32, (t_len, t_len), 0)
    cols = lax.broadcasted_iota(jnp.int32, (t_len, t_len), 1)
    causal = cols <= rows
    tril = causal.astype(F32)
    n_levels = t_len.bit_length() - 1

    gates = p_ref[:, GATE_COL:GATE_COL + LANES]
    lane = lax.broadcasted_iota(jnp.int32, (t_len, LANES), 1)
    is_f = (lane >= N_HEADS) & (lane < 2 * N_HEADS)
    log_f = jnp.where(is_f, jax.nn.log_sigmoid(gates + bf_ref[...]), 0.0)
    cum_f = jnp.dot(tril, log_f, precision=HIGHEST, preferred_element_type=F32)
    z = jnp.where(is_f, cum_f, jnp.where(lane < N_HEADS, gates, 0.0))

    for h in range(N_HEADS):
        col = lambda j: p_ref[:, j * D_GROUP + h * D_HEAD:j * D_GROUP + (h + 1) * D_HEAD]
        q = col(0)
        k = col(1) * (D_HEAD ** -0.5)
        v = col(2)
        og = col(3)
        b_col = z[:, N_HEADS + h:N_HEADS + h + 1]
        i_col = z[:, h:h + 1]
        m0 = m_ref[0, :, h:h + 1]
        sel = jnp.where(lane == N_HEADS + h, 1.0, jnp.where(lane == h, -1.0, 0.0))
        bs_row = _dot_nt(sel, z, precision=HIGHEST)
        d = jnp.where(causal, b_col - bs_row, -jnp.inf)
        m_t = jnp.maximum(b_col + m0, jnp.max(d, axis=1, keepdims=True))
        dec = jnp.exp(b_col + m0 - m_t)
        w = jnp.exp(d - m_t)
        q16, k16, v16 = q.astype(BF16), k.astype(BF16), v.astype(BF16)
        sw = w * _dot_nt(q16, k16)
        c_old = c_ref[0, h]
        n_old = n_ref[0, h:h + 1, :]
        num = dec * _dot(q16, c_old.astype(BF16)) + _dot(sw.astype(BF16), v16)
        den = dec * jnp.sum(q * n_old, axis=1, keepdims=True) + jnp.sum(sw, axis=1, keepdims=True)
        hid = num / jnp.maximum(jnp.abs(den), jnp.exp(-m_t))
        hid = hid * lax.rsqrt(jnp.mean(hid * hid, axis=1, keepdims=True) + RMS_EPS)
        hid = hid * ga_ref[:, h * D_HEAD:(h + 1) * D_HEAD]
        mix_ref[:, h * D_HEAD:(h + 1) * D_HEAD] = jax.nn.sigmoid(og) * hid
        b_last = b_col[t_len - 1:t_len, :]
        m_last = m_t[t_len - 1:t_len, :]
        dec_last = dec[t_len - 1:t_len, :]
        w_last = jnp.exp(b_last - b_col + i_col - m_last)
        c_ref[0, h] = dec_last * c_old + _dot_tn(k16, (w_last * v).astype(BF16))
        n_ref[0, h:h + 1, :] = dec_last * n_old + jnp.sum(w_last * k, axis=0, keepdims=True)
        m_ref[0, :, h:h + 1] = m_last

    qb = p_ref[:, 4 * D_GROUP:5 * D_GROUP]
    fb = p_ref[:, 5 * D_GROUP:6 * D_GROUP]
    ib = p_ref[:, 6 * D_GROUP:7 * D_GROUP]
    gg = p_ref[:, 7 * D_GROUP:8 * D_GROUP]
    lb = lb_ref[...]
    log_fb = jnp.log(lb + (1.0 - lb) * jax.nn.sigmoid(fb))
    kb = (1.0 - lb) * jax.nn.sigmoid(-fb)
    qb = qb * jax.nn.sigmoid(qb)
    a = jnp.dot(tril, log_fb, precision=HIGHEST, preferred_element_type=F32)
    qb16, kb16, ib16 = qb.astype(BF16), kb.astype(BF16), ib.astype(BF16)

    hs = lambda x, h: x[:, h * D_HEAD:(h + 1) * D_HEAD]
    scores = [jnp.where(rows == cols, _dot_nt(hs(qb16, h), hs(kb16, h)), 0.0)
              for h in range(N_HEADS)]
    rows_g = lax.broadcasted_iota(jnp.int32, (t_len, D_GROUP), 0)
    for level in range(n_levels):
        half = 1 << level
        g = jnp.exp(-jnp.abs(a - _hgrn_level_reference(a, level, t_len)))
        upper = (rows_g & half) != 0
        q_l = jnp.where(upper, qb * g, 0.0).astype(BF16)
        k_l = jnp.where(upper, 0.0, kb * g).astype(BF16)
        same_block = (rows >> (level + 1)) == (cols >> (level + 1))
        for h in range(N_HEADS):
            scores[h] = scores[h] + jnp.where(same_block, _dot_nt(hs(q_l, h), hs(k_l, h)), 0.0)

    a_last = a[t_len - 1:t_len, :]
    q_in = (qb * jnp.exp(a)).astype(BF16)
    k_out = (kb * jnp.exp(a_last - a)).astype(BF16)
    carry = jnp.exp(a_last)
    for h in range(N_HEADS):
        s_old = s_ref[0, h]
        out = _dot_nt(hs(q_in, h), s_old.astype(BF16)) + _dot(scores[h].astype(BF16), hs(ib16, h))
        out = out * lax.rsqrt(jnp.mean(out * out, axis=1, keepdims=True) + RMS_EPS)
        out = out * gb_ref[:, h * D_HEAD:(h + 1) * D_HEAD]
        mix_ref[:, D_GROUP + h * D_HEAD:D_GROUP + (h + 1) * D_HEAD] = jax.nn.sigmoid(hs(gg, h)) * out
        s_ref[0, h] = hs(carry, h) * s_old + _dot_tn(hs(ib16, h), hs(k_out, h))

    @pl.when(chunk == n_chunks - 1)
    def _():
        for h in range(N_HEADS):
            s_ref[0, h] = s_ref[0, h].T


def _mixer(proj, c0, n0, m0, s0, bf_row, ga, gb, lb, *, n_seq, n_chunks, t_len, shared_init):
    init = (lambda b, c: (0, 0, 0, 0)) if shared_init else (lambda b, c: (b, 0, 0, 0))
    init3 = (lambda b, c: (0, 0, 0)) if shared_init else (lambda b, c: (b, 0, 0))
    const = lambda b, c: (0, 0)
    state4 = pl.BlockSpec((1, N_HEADS, D_HEAD, D_HEAD), lambda b, c: (b, 0, 0, 0))
    return pl.pallas_call(
        functools.partial(_mixer_kernel, t_len=t_len, n_chunks=n_chunks),
        grid=(n_seq, n_chunks),
        in_specs=[
            pl.BlockSpec((t_len, D_PROJ), lambda b, c: (b * n_chunks + c, 0)),
            pl.BlockSpec((1, N_HEADS, D_HEAD, D_HEAD), init),
            pl.BlockSpec((1, N_HEADS, D_HEAD), init3),
            pl.BlockSpec((1, 1, N_HEADS), init3),
            pl.BlockSpec((1, N_HEADS, D_HEAD, D_HEAD), init),
            pl.BlockSpec((1, LANES), const),
            pl.BlockSpec((1, D_GROUP), const),
            pl.BlockSpec((1, D_GROUP), const),
            pl.BlockSpec((1, D_GROUP), const),
        ],
        out_specs=[
            pl.BlockSpec((t_len, D_MODEL), lambda b, c: (b * n_chunks + c, 0)),
            state4,
            pl.BlockSpec((1, N_HEADS, D_HEAD), lambda b, c: (b, 0, 0)),
            pl.BlockSpec((1, 1, N_HEADS), lambda b, c: (b, 0, 0)),
            state4,
        ],
        out_shape=[
            jax.ShapeDtypeStruct((n_seq * n_chunks * t_len, D_MODEL), F32),
            jax.ShapeDtypeStruct((n_seq, N_HEADS, D_HEAD, D_HEAD), F32),
            jax.ShapeDtypeStruct((n_seq, N_HEADS, D_HEAD), F32),
            jax.ShapeDtypeStruct((n_seq, 1, N_HEADS), F32),
            jax.ShapeDtypeStruct((n_seq, N_HEADS, D_HEAD, D_HEAD), F32),
        ],
        compiler_params=pltpu.CompilerParams(
            dimension_semantics=("arbitrary", "arbitrary"), vmem_limit_bytes=VMEM_LIMIT),
        name=f"mixer_t{t_len}",
    )(proj, c0, n0, m0, s0, bf_row, ga, gb, lb)


def _out_proj_kernel(x_ref, mix_ref, ge_ref, be_ref, w_ref, bias_ref, g_ref, b_ref, o_ref):
    xn = _layer_norm(x_ref[...], ge_ref[...], be_ref[...])
    y = _dot(mix_ref[...].astype(BF16), w_ref[...]) + bias_ref[...]
    o_ref[...] = _layer_norm(ALPHA * xn + y, g_ref[...], b_ref[...])


def _out_proj(x, mix, ln_e_g, ln_e_b, w, bias, ln_g, ln_b, *, tm):
    n = x.shape[0]
    const = lambda i: (0, 0)
    row = pl.BlockSpec((tm, D_MODEL), lambda i: (i, 0))
    vec = pl.BlockSpec((1, D_MODEL), const)
    return pl.pallas_call(
        _out_proj_kernel,
        grid=(n // tm,),
        in_specs=[row, row, vec, vec, pl.BlockSpec((D_MODEL, D_MODEL), const), vec, vec, vec],
        out_specs=row,
        out_shape=jax.ShapeDtypeStruct((n, D_MODEL), F32),
        compiler_params=pltpu.CompilerParams(
            dimension_semantics=("arbitrary",), vmem_limit_bytes=VMEM_LIMIT),
        name="out_proj",
    )(x, mix, ln_e_g, ln_e_b, w, bias, ln_g, ln_b)


def _ffn_kernel(x_ref, cs_ref, wu_ref, bu_ref, wc_ref, bc_ref, wd_ref, bd_ref, g_ref, b_ref,
                y_ref, nc_ref, full_ref, *, n_seq_blk, t_len):
    hist = SUBLANES - (CONV_W - 1)

    @pl.when(pl.program_id(1) == 0)
    def _():
        full_ref[:, hist:SUBLANES, :] = cs_ref[...]

    x = x_ref[...]
    up = _dot(x.astype(BF16), wu_ref[...]) + bu_ref[...]
    u = up[:, :D_FF].reshape(n_seq_blk, t_len, D_FF)
    gate = up[:, D_FF:].reshape(n_seq_blk, t_len, D_FF)
    full_ref[:, SUBLANES:SUBLANES + t_len, :] = u
    conv = bc_ref[...] + u * wc_ref[CONV_W - 1:CONV_W, :]
    for j in range(CONV_W - 1):
        conv = conv + full_ref[:, hist + j:hist + j + t_len, :] * wc_ref[j:j + 1, :]
    last = full_ref[:, hist + t_len:SUBLANES + t_len, :]
    nc_ref[...] = last
    full_ref[:, hist:SUBLANES, :] = last
    act = (conv * jax.nn.sigmoid(conv) * gate).reshape(n_seq_blk * t_len, D_FF)
    ffn = _dot(act.astype(BF16), wd_ref[...]) + bd_ref[...]
    y_ref[...] = _layer_norm(ALPHA * x + ffn, g_ref[...], b_ref[...])


def _ffn(x, conv_state, w_up, b_up, w_conv, b_conv, w_down, b_down, ln_g, ln_b,
         *, n_seq, seq_len, n_seq_blk, t_len, shared_init):
    n_t = seq_len // t_len
    rows = n_seq_blk * t_len
    const = lambda s, t: (0, 0)
    cs_map = (lambda s, t: (0, 0, 0)) if shared_init else (lambda s, t: (s, 0, 0))
    row = pl.BlockSpec((rows, D_MODEL), lambda s, t: (s * n_t + t, 0))
    vec = pl.BlockSpec((1, D_MODEL), const)
    return pl.pallas_call(
        functools.partial(_ffn_kernel, n_seq_blk=n_seq_blk, t_len=t_len),
        grid=(n_seq // n_seq_blk, n_t),
        in_specs=[
            row,
            pl.BlockSpec((n_seq_blk, CONV_W - 1, D_FF), cs_map),
            pl.BlockSpec((D_MODEL, 2 * D_FF), const),
            pl.BlockSpec((1, 2 * D_FF), const),
            pl.BlockSpec((CONV_W, D_FF), const),
            pl.BlockSpec((1, D_FF), const),
            pl.BlockSpec((D_FF, D_MODEL), const),
            vec, vec, vec,
        ],
        out_specs=[row, pl.BlockSpec((n_seq_blk, CONV_W - 1, D_FF), lambda s, t: (s, 0, 0))],
        out_shape=[
            jax.ShapeDtypeStruct((n_seq * seq_len, D_MODEL), F32),
            jax.ShapeDtypeStruct((n_seq, CONV_W - 1, D_FF), F32),
        ],
        scratch_shapes=[pltpu.VMEM((n_seq_blk, SUBLANES + t_len, D_FF), F32)],
        compiler_params=pltpu.CompilerParams(
            dimension_semantics=("arbitrary", "arbitrary"), vmem_limit_bytes=VMEM_LIMIT),
        name=f"ffn_t{t_len}",
    )(x, conv_state, w_up, b_up, w_conv, b_conv, w_down, b_down, ln_g, ln_b)


def kernel(x_prompt, x_sample, state_mlstm_C, state_mlstm_n, state_mlstm_m, state_hgrn_S, state_ffn_conv, meta_tokens, ln_emb_g, ln_emb_b, w_in, b_in, b_fgate_a, g_norm_a, g_norm_b, hgrn_lb_logits, w_out, b_out, ln1_g, ln1_b, w_up, b_up, w_conv, b_conv, w_down, b_down, ln2_g, ln2_b):
    assert w_in.shape[0] == DEPTH == 1
    n_prompt, seq, _ = x_prompt.shape
    n_sample, dec_seq, _ = x_sample.shape
    row = lambda v: v.reshape(1, -1).astype(F32)

    gate0 = 4 * D_GROUP
    gate1 = gate0 + 2 * N_HEADS
    pad = D_PROJ - w_in.shape[2]
    w_in_p = jnp.concatenate(
        [w_in[0][:, :gate0], w_in[0][:, gate1:], w_in[0][:, gate0:gate1],
         jnp.zeros((D_MODEL, pad), w_in.dtype)], axis=1).astype(BF16)
    b_in_p = jnp.concatenate(
        [b_in[0][:gate0], b_in[0][gate1:], b_in[0][gate0:gate1], jnp.zeros((pad,), b_in.dtype)]
    ).reshape(1, D_PROJ).astype(F32)
    bf_row = jnp.zeros((1, LANES), F32).at[0, N_HEADS:2 * N_HEADS].set(b_fgate_a[0].astype(F32))
    lb = jnp.cumsum(jax.nn.softmax(hgrn_lb_logits.astype(F32), axis=0), axis=0)[0].reshape(1, D_GROUP)
    ga, gb = row(g_norm_a[0]), row(g_norm_b[0])
    ln_e = (row(ln_emb_g), row(ln_emb_b))
    out_p = (w_out[0].astype(BF16), row(b_out[0]), row(ln1_g[0]), row(ln1_b[0]))
    ffn_p = (w_up[0].astype(BF16), row(b_up[0]), w_conv[0].astype(F32), row(b_conv[0]),
             w_down[0].astype(BF16), row(b_down[0]), row(ln2_g[0]), row(ln2_b[0]))

    def layer(x_rows, mixer_state, conv_state, *, n_seq, seq_len, t_mix, tm, ffn_seq_blk, ffn_t,
              shared_init):
        proj = _in_proj(x_rows, *ln_e, w_in_p, b_in_p, tm=tm)
        mix, c_new, n_new, m_new, s_new = _mixer(
            proj, *mixer_state, bf_row, ga, gb, lb, n_seq=n_seq, n_chunks=seq_len // t_mix,
            t_len=t_mix, shared_init=shared_init)
        x1 = _out_proj(x_rows, mix, *ln_e, *out_p, tm=tm)
        y, conv_new = _ffn(x1, conv_state, *ffn_p, n_seq=n_seq, seq_len=seq_len,
                           n_seq_blk=ffn_seq_blk, t_len=ffn_t, shared_init=shared_init)
        return y, c_new, n_new, m_new, s_new, conv_new

    zero_state = (jnp.zeros((1, N_HEADS, D_HEAD, D_HEAD), F32), jnp.zeros((1, N_HEADS, D_HEAD), F32),
                  jnp.zeros((1, 1, N_HEADS), F32), jnp.zeros((1, N_HEADS, D_HEAD, D_HEAD), F32))
    _, c_m, n_m, m_m, s_m, conv_m = layer(
        meta_tokens.astype(F32), zero_state, jnp.zeros((1, CONV_W - 1, D_FF), F32),
        n_seq=1, seq_len=N_META, t_mix=N_META, tm=N_META, ffn_seq_blk=1, ffn_t=N_META,
        shared_init=False)

    y_p, c_p, n_p, m_p, s_p, conv_p = layer(
        x_prompt.reshape(n_prompt * seq, D_MODEL), (c_m, n_m, m_m, s_m), conv_m,
        n_seq=n_prompt, seq_len=seq, t_mix=128, tm=256, ffn_seq_blk=1, ffn_t=256,
        shared_init=True)

    sample_state = (state_mlstm_C[0].astype(F32), state_mlstm_n[0].astype(F32),
                    state_mlstm_m[0].astype(F32).reshape(n_sample, 1, N_HEADS),
                    state_hgrn_S[0].astype(F32))
    y_s, c_s, n_s, m_s, s_s, conv_s = layer(
        x_sample.reshape(n_sample * dec_seq, D_MODEL), sample_state, state_ffn_conv[0].astype(F32),
        n_seq=n_sample, seq_len=dec_seq, t_mix=dec_seq, tm=256, ffn_seq_blk=32, ffn_t=dec_seq,
        shared_init=False)

    lead = lambda v: v[None]
    return (y_p.reshape(n_prompt, seq, D_MODEL), y_s.reshape(n_sample, dec_seq, D_MODEL),
            lead(c_p), lead(n_p), lead(m_p.reshape(n_prompt, N_HEADS)), lead(s_p), lead(conv_p),
            lead(c_s), lead(n_s), lead(m_s.reshape(n_sample, N_HEADS)), lead(s_s), lead(conv_s))
```

```python
import functools

import jax
import jax.numpy as jnp
from jax import lax
from jax.experimental import pallas as pl
from jax.experimental.pallas import tpu as pltpu

D_MODEL = 1024
N_META = 16
N_HEADS = 4
D_HEAD = 128
D_GROUP = N_HEADS * D_HEAD
D_FF = 2816
CONV_W = 3
DEPTH = 1
ALPHA = (2.0 * DEPTH) ** 0.25
LN_EPS = 1e-5
RMS_EPS = 1e-6

LANES = 128
SUBLANES = 8
GATE_COL = 8 * D_GROUP
D_PROJ = GATE_COL + LANES
VMEM_LIMIT = 56 * 1024 * 1024

F32 = jnp.float32
BF16 = jnp.bfloat16
HIGHEST = lax.Precision.HIGHEST
NT_DIMS = (((1,), (1,)), ((), ()))
TN_DIMS = (((0,), (0,)), ((), ()))


def _layer_norm(x, g, b):
    mu = jnp.mean(x, axis=-1, keepdims=True)
    xc = x - mu
    var = jnp.mean(xc * xc, axis=-1, keepdims=True)
    return xc * lax.rsqrt(var + LN_EPS) * g + b


def _sigmoid(x):
    return 1.0 / (1.0 + jnp.exp(-x))


def _dot(a, b, precision=None):
    return jnp.dot(a, b, precision=precision, preferred_element_type=F32)


def _dot_nt(a, b, precision=None):
    return lax.dot_general(a, b, NT_DIMS, precision=precision, preferred_element_type=F32)


def _dot_tn(a, b):
    return lax.dot_general(a, b, TN_DIMS, preferred_element_type=F32)


def _in_proj_kernel(x_ref, g_ref, b_ref, w_ref, bias_ref, o_ref):
    xn = _layer_norm(x_ref[...], g_ref[...], b_ref[...])
    o_ref[...] = _dot(xn.astype(BF16), w_ref[...]) + bias_ref[...]


def _in_proj(x, ln_g, ln_b, w, bias, *, tm):
    n = x.shape[0]
    const = lambda i: (0, 0)
    return pl.pallas_call(
        _in_proj_kernel,
        grid=(n // tm,),
        in_specs=[
            pl.BlockSpec((tm, D_MODEL), lambda i: (i, 0)),
            pl.BlockSpec((1, D_MODEL), const),
            pl.BlockSpec((1, D_MODEL), const),
            pl.BlockSpec((D_MODEL, D_PROJ), const),
            pl.BlockSpec((1, D_PROJ), const),
        ],
        out_specs=pl.BlockSpec((tm, D_PROJ), lambda i: (i, 0)),
        out_shape=jax.ShapeDtypeStruct((n, D_PROJ), F32),
        compiler_params=pltpu.CompilerParams(
            dimension_semantics=("arbitrary",), vmem_limit_bytes=VMEM_LIMIT),
        name="in_proj",
    )(x, ln_g, ln_b, w, bias)


def _block_rows(x, level, t_len, row_in_block):
    size = 2 << level
    if size > SUBLANES:
        pieces = [jnp.broadcast_to(x[j * size + row_in_block:j * size + row_in_block + 1, :],
                                   (size, x.shape[1])) for j in range(t_len // size)]
        return pieces[0] if len(pieces) == 1 else jnp.concatenate(pieces, axis=0)
    x3 = x.reshape(t_len // SUBLANES, SUBLANES, x.shape[1])
    sub = lax.broadcasted_iota(jnp.int32, x3.shape, 1)
    out = None
    for j in range(SUBLANES // size):
        row = jnp.broadcast_to(x3[:, j * size + row_in_block:j * size + row_in_block + 1, :], x3.shape)
        out = row if out is None else jnp.where(sub >= j * size, row, out)
    return out.reshape(x.shape)


def _interleave_halves(lower, upper, level, t_len):
    half = 1 << level
    if half >= SUBLANES:
        pieces = []
        for j in range(t_len // (2 * half)):
            pieces.append(lower[2 * half * j:2 * half * j + half])
            pieces.append(upper[2 * half * j + half:2 * half * (j + 1)])
        return jnp.concatenate(pieces, axis=0)
    rows = lax.broadcasted_iota(jnp.int32, lower.shape, 0)
    return jnp.where((rows & half) != 0, upper, lower)


def _mlstm_units(*, q, k, v, og, z, head, c_old, n_old, m0, g_norm, lane, causal, t_len):
    idx = range(len(q))
    q16 = [q[i].astype(BF16) for i in idx]
    k16 = [k[i].astype(BF16) for i in idx]
    qk = [_dot_nt(q16[i], k16[i]) for i in idx]
    qc = [_dot(q16[i], c_old[i].astype(BF16)) for i in idx]
    bs_row = []
    for i in idx:
        sel = jnp.where(lane == N_HEADS + head[i], 1.0, jnp.where(lane == head[i], -1.0, 0.0))
        bs_row.append(_dot_nt(sel, z[i], precision=HIGHEST))
    b_col = [z[i][:, N_HEADS + head[i]:N_HEADS + head[i] + 1] for i in idx]
    i_col = [z[i][:, head[i]:head[i] + 1] for i in idx]
    d = [jnp.where(causal, b_col[i] - bs_row[i], -jnp.inf) for i in idx]
    m_t = [jnp.maximum(b_col[i] + m0[i], jnp.max(d[i], axis=1, keepdims=True)) for i in idx]
    dec = [jnp.exp(b_col[i] + m0[i] - m_t[i]) for i in idx]
    sw = [jnp.exp(d[i] - m_t[i]) * qk[i] for i in idx]
    swv = [_dot(sw[i].astype(BF16), v[i].astype(BF16)) for i in idx]
    last = slice(t_len - 1, t_len)
    w_last = [jnp.exp(b_col[i][last] - b_col[i] + i_col[i] - m_t[i][last]) for i in idx]
    kv = [_dot_tn(k16[i], (w_last[i] * v[i]).astype(BF16)) for i in idx]
    c_new = [dec[i][last] * c_old[i] + kv[i] for i in idx]
    n_new = [dec[i][last] * n_old[i] + jnp.sum(w_last[i] * k[i], axis=0, keepdims=True) for i in idx]
    m_new = [m_t[i][last] for i in idx]
    den = [dec[i] * jnp.sum(q[i] * n_old[i], axis=1, keepdims=True)
           + jnp.sum(sw[i], axis=1, keepdims=True) for i in idx]
    hid = [(dec[i] * qc[i] + swv[i]) / jnp.maximum(jnp.abs(den[i]), jnp.exp(-m_t[i])) for i in idx]
    rms = [lax.rsqrt(jnp.mean(hid[i] * hid[i], axis=1, keepdims=True) + RMS_EPS) for i in idx]
    out = [_sigmoid(og[i]) * (hid[i] * rms[i] * g_norm[i]) for i in idx]
    return out, c_new, n_new, m_new


def _hgrn_units(*, qr, fr, iv, gr, lb, s_old_t, g_norm, tril, level_of, t_len):
    idx = range(len(qr))
    n_levels = t_len.bit_length() - 1
    f = [lb[i] + (1.0 - lb[i]) * _sigmoid(fr[i]) for i in idx]
    a = [_dot(tril, jnp.log(f[i]), precision=HIGHEST) for i in idx]
    kb = [(1.0 - lb[i]) * _sigmoid(-fr[i]) for i in idx]
    qb = [qr[i] * _sigmoid(qr[i]) for i in idx]
    iv16 = [iv[i].astype(BF16) for i in idx]
    diag = [_dot_nt(qb[i].astype(BF16), kb[i].astype(BF16)) for i in idx]
    scores = [jnp.where(level_of == -2, diag[i], 0.0) for i in idx]
    for level in range(n_levels):
        x16 = []
        for i in idx:
            base = _interleave_halves(kb[i], qb[i], level, t_len)
            if level == 0:
                x = base * _interleave_halves(jnp.ones_like(f[i]), f[i], 0, t_len)
            else:
                ref = _block_rows(a[i], level, t_len, (1 << level) - 1)
                x = base * jnp.exp(-jnp.abs(a[i] - ref))
            x16.append(x.astype(BF16))
        part = [_dot_nt(x16[i], x16[i]) for i in idx]
        scores = [jnp.where(level_of == level, part[i], scores[i]) for i in idx]
    last = slice(t_len - 1, t_len)
    q_in = [(qb[i] * jnp.exp(a[i])).astype(BF16) for i in idx]
    k_out = [(kb[i] * jnp.exp(a[i][last] - a[i])).astype(BF16) for i in idx]
    inter = [_dot_nt(q_in[i], s_old_t[i].astype(BF16)) for i in idx]
    intra = [_dot(scores[i].astype(BF16), iv16[i]) for i in idx]
    kv = [_dot_tn(iv16[i], k_out[i]) for i in idx]
    s_new_t = [jnp.exp(a[i][last]) * s_old_t[i] + kv[i] for i in idx]
    o = [inter[i] + intra[i] for i in idx]
    rms = [lax.rsqrt(jnp.mean(o[i] * o[i], axis=1, keepdims=True) + RMS_EPS) for i in idx]
    out = [_sigmoid(gr[i]) * (o[i] * rms[i] * g_norm[i]) for i in idx]
    return out, s_new_t


def _mixer_kernel(p_ref, c0_ref, n0_ref, m0_ref, s0_ref, bf_ref, ga_ref, gb_ref, lb_ref,
                  mix_ref, c_ref, n_ref, m_ref, s_ref, *, t_len, n_chunks, n_seq_blk):
    chunk = pl.program_id(1)

    @pl.when(chunk == 0)
    def _():
        c_ref[...] = c0_ref[...]
        n_ref[...] = n0_ref[...]
        m_ref[...] = m0_ref[...]
        for s in range(n_seq_blk):
            for h in range(N_HEADS):
                s_ref[s, h] = s0_ref[s, h].T

    rows = lax.broadcasted_iota(jnp.int32, (t_len, t_len), 0)
    cols = lax.broadcasted_iota(jnp.int32, (t_len, t_len), 1)
    causal = cols <= rows
    tril = causal.astype(F32)
    level_of = jnp.where(rows > cols, 31 - lax.clz(rows ^ cols), jnp.where(rows == cols, -2, -1))
    lane = lax.broadcasted_iota(jnp.int32, (t_len, LANES), 1)
    is_f = (lane >= N_HEADS) & (lane < 2 * N_HEADS)
    hd = lambda j, h: slice(j * D_GROUP + h * D_HEAD, j * D_GROUP + (h + 1) * D_HEAD)

    units = [(s, h) for s in range(n_seq_blk) for h in range(N_HEADS)]
    rs = lambda s: slice(s * t_len, (s + 1) * t_len)

    z_seq, n_seq, m_seq = [], [], []
    for s in range(n_seq_blk):
        gates = p_ref[rs(s), GATE_COL:GATE_COL + LANES]
        log_f = jnp.where(is_f, jax.nn.log_sigmoid(gates + bf_ref[...]), 0.0)
        cum_f = _dot(tril, log_f, precision=HIGHEST)
        z_seq.append(jnp.where(is_f, cum_f, jnp.where(lane < N_HEADS, gates, 0.0)))
        n_seq.append(n_ref[s])
        m_seq.append(m_ref[s])
    outs, c_new, n_new, m_new = _mlstm_units(
        q=[p_ref[rs(s), hd(0, h)] for s, h in units],
        k=[p_ref[rs(s), hd(1, h)] * (D_HEAD ** -0.5) for s, h in units],
        v=[p_ref[rs(s), hd(2, h)] for s, h in units],
        og=[p_ref[rs(s), hd(3, h)] for s, h in units],
        z=[z_seq[s] for s, h in units], head=[h for s, h in units],
        c_old=[c_ref[s, h] for s, h in units], n_old=[n_seq[s][h:h + 1, :] for s, h in units],
        m0=[m_seq[s][:, h:h + 1] for s, h in units], g_norm=[ga_ref[:, hd(0, h)] for s, h in units],
        lane=lane, causal=causal, t_len=t_len)
    for i, (s, h) in enumerate(units):
        mix_ref[rs(s), hd(0, h)] = outs[i].astype(mix_ref.dtype)
        c_ref[s, h] = c_new[i]
    head_lane = lax.broadcasted_iota(jnp.int32, (1, N_HEADS), 1)
    for s in range(n_seq_blk):
        n_ref[s] = jnp.concatenate(n_new[s * N_HEADS:(s + 1) * N_HEADS], axis=0)
        m_row = m_seq[s]
        for h in range(N_HEADS):
            m_row = jnp.where(head_lane == h, m_new[s * N_HEADS + h], m_row)
        m_ref[s] = m_row

    outs, s_new_t = _hgrn_units(
        qr=[p_ref[rs(s), hd(4, h)] for s, h in units], fr=[p_ref[rs(s), hd(5, h)] for s, h in units],
        iv=[p_ref[rs(s), hd(6, h)] for s, h in units], gr=[p_ref[rs(s), hd(7, h)] for s, h in units],
        lb=[lb_ref[:, hd(0, h)] for s, h in units], s_old_t=[s_ref[s, h] for s, h in units],
        g_norm=[gb_ref[:, hd(0, h)] for s, h in units], tril=tril, level_of=level_of, t_len=t_len)
    for i, (s, h) in enumerate(units):
        mix_ref[rs(s), hd(1, h)] = outs[i].astype(mix_ref.dtype)
        s_ref[s, h] = s_new_t[i]

    @pl.when(chunk == n_chunks - 1)
    def _():
        for s in range(n_seq_blk):
            for h in range(N_HEADS):
                s_ref[s, h] = s_ref[s, h].T


def _mixer(proj, c0, n0, m0, s0, bf_row, ga, gb, lb, *, n_seq, n_chunks, t_len, n_seq_blk,
           shared_init):
    assert not shared_init or n_seq_blk == 1
    assert n_chunks == 1 or n_seq_blk == 1
    nb = n_seq_blk
    init = (lambda b, c: (0, 0, 0, 0)) if shared_init else (lambda b, c: (b, 0, 0, 0))
    init3 = (lambda b, c: (0, 0, 0)) if shared_init else (lambda b, c: (b, 0, 0))
    const = lambda b, c: (0, 0)
    state4 = pl.BlockSpec((nb, N_HEADS, D_HEAD, D_HEAD), lambda b, c: (b, 0, 0, 0))
    return pl.pallas_call(
        functools.partial(_mixer_kernel, t_len=t_len, n_chunks=n_chunks, n_seq_blk=nb),
        grid=(n_seq // nb, n_chunks),
        in_specs=[
            pl.BlockSpec((nb * t_len, D_PROJ), lambda b, c: (b * n_chunks + c, 0)),
            pl.BlockSpec((nb, N_HEADS, D_HEAD, D_HEAD), init),
            pl.BlockSpec((nb, N_HEADS, D_HEAD), init3),
            pl.BlockSpec((nb, 1, N_HEADS), init3),
            pl.BlockSpec((nb, N_HEADS, D_HEAD, D_HEAD), init),
            pl.BlockSpec((1, LANES), const),
            pl.BlockSpec((1, D_GROUP), const),
            pl.BlockSpec((1, D_GROUP), const),
            pl.BlockSpec((1, D_GROUP), const),
        ],
        out_specs=[
            pl.BlockSpec((nb * t_len, D_MODEL), lambda b, c: (b * n_chunks + c, 0)),
            state4,
            pl.BlockSpec((nb, N_HEADS, D_HEAD), lambda b, c: (b, 0, 0)),
            pl.BlockSpec((nb, 1, N_HEADS), lambda b, c: (b, 0, 0)),
            state4,
        ],
        out_shape=[
            jax.ShapeDtypeStruct((n_seq * n_chunks * t_len, D_MODEL), BF16),
            jax.ShapeDtypeStruct((n_seq, N_HEADS, D_HEAD, D_HEAD), F32),
            jax.ShapeDtypeStruct((n_seq, N_HEADS, D_HEAD), F32),
            jax.ShapeDtypeStruct((n_seq, 1, N_HEADS), F32),
            jax.ShapeDtypeStruct((n_seq, N_HEADS, D_HEAD, D_HEAD), F32),
        ],
        compiler_params=pltpu.CompilerParams(
            dimension_semantics=("arbitrary", "arbitrary"), vmem_limit_bytes=VMEM_LIMIT),
        name=f"mixer_t{t_len}",
    )(proj, c0, n0, m0, s0, bf_row, ga, gb, lb)


def _out_proj_kernel(x_ref, mix_ref, ge_ref, be_ref, w_ref, bias_ref, g_ref, b_ref, o_ref):
    xn = _layer_norm(x_ref[...], ge_ref[...], be_ref[...])
    y = _dot(mix_ref[...], w_ref[...]) + bias_ref[...]
    o_ref[...] = _layer_norm(ALPHA * xn + y, g_ref[...], b_ref[...])


def _out_proj(x, mix, ln_e_g, ln_e_b, w, bias, ln_g, ln_b, *, tm):
    n = x.shape[0]
    const = lambda i: (0, 0)
    row = pl.BlockSpec((tm, D_MODEL), lambda i: (i, 0))
    vec = pl.BlockSpec((1, D_MODEL), const)
    return pl.pallas_call(
        _out_proj_kernel,
        grid=(n // tm,),
        in_specs=[row, row, vec, vec, pl.BlockSpec((D_MODEL, D_MODEL), const), vec, vec, vec],
        out_specs=row,
        out_shape=jax.ShapeDtypeStruct((n, D_MODEL), F32),
        compiler_params=pltpu.CompilerParams(
            dimension_semantics=("arbitrary",), vmem_limit_bytes=VMEM_LIMIT),
        name="out_proj",
    )(x, mix, ln_e_g, ln_e_b, w, bias, ln_g, ln_b)


def _ffn_kernel(x_ref, cs_ref, wu_ref, bu_ref, wc_ref, bc_ref, wd_ref, bd_ref, g_ref, b_ref,
                y_ref, nc_ref, full_ref, *, n_seq_blk, t_len):
    hist = SUBLANES - (CONV_W - 1)

    @pl.when(pl.program_id(1) == 0)
    def _():
        full_ref[:, hist:SUBLANES, :] = cs_ref[...]

    x = x_ref[...]
    up = _dot(x.astype(BF16), wu_ref[...]) + bu_ref[...]
    u = up[:, :D_FF].reshape(n_seq_blk, t_len, D_FF)
    gate = up[:, D_FF:].reshape(n_seq_blk, t_len, D_FF)
    full_ref[:, SUBLANES:SUBLANES + t_len, :] = u
    conv = bc_ref[...] + u * wc_ref[CONV_W - 1:CONV_W, :]
    for j in range(CONV_W - 1):
        conv = conv + full_ref[:, hist + j:hist + j + t_len, :] * wc_ref[j:j + 1, :]
    last = full_ref[:, hist + t_len:SUBLANES + t_len, :]
    nc_ref[...] = last
    full_ref[:, hist:SUBLANES, :] = last
    act = (conv * _sigmoid(conv) * gate).reshape(n_seq_blk * t_len, D_FF)
    ffn = _dot(act.astype(BF16), wd_ref[...]) + bd_ref[...]
    y_ref[...] = _layer_norm(ALPHA * x + ffn, g_ref[...], b_ref[...])


def _ffn(x, conv_state, w_up, b_up, w_conv, b_conv, w_down, b_down, ln_g, ln_b,
         *, n_seq, seq_len, n_seq_blk, t_len, shared_init):
    n_t = seq_len // t_len
    rows = n_seq_blk * t_len
    const = lambda s, t: (0, 0)
    cs_map = (lambda s, t: (0, 0, 0)) if shared_init else (lambda s, t: (s, 0, 0))
    row = pl.BlockSpec((rows, D_MODEL), lambda s, t: (s * n_t + t, 0))
    vec = pl.BlockSpec((1, D_MODEL), const)
    return pl.pallas_call(
        functools.partial(_ffn_kernel, n_seq_blk=n_seq_blk, t_len=t_len),
        grid=(n_seq // n_seq_blk, n_t),
        in_specs=[
            row,
            pl.BlockSpec((n_seq_blk, CONV_W - 1, D_FF), cs_map),
            pl.BlockSpec((D_MODEL, 2 * D_FF), const),
            pl.BlockSpec((1, 2 * D_FF), const),
            pl.BlockSpec((CONV_W, D_FF), const),
            pl.BlockSpec((1, D_FF), const),
            pl.BlockSpec((D_FF, D_MODEL), const),
            vec, vec, vec,
        ],
        out_specs=[row, pl.BlockSpec((n_seq_blk, CONV_W - 1, D_FF), lambda s, t: (s, 0, 0))],
        out_shape=[
            jax.ShapeDtypeStruct((n_seq * seq_len, D_MODEL), F32),
            jax.ShapeDtypeStruct((n_seq, CONV_W - 1, D_FF), F32),
        ],
        scratch_shapes=[pltpu.VMEM((n_seq_blk, SUBLANES + t_len, D_FF), F32)],
        compiler_params=pltpu.CompilerParams(
            dimension_semantics=("arbitrary", "arbitrary"), vmem_limit_bytes=VMEM_LIMIT),
        name=f"ffn_t{t_len}",
    )(x, conv_state, w_up, b_up, w_conv, b_conv, w_down, b_down, ln_g, ln_b)


def kernel(x_prompt, x_sample, state_mlstm_C, state_mlstm_n, state_mlstm_m, state_hgrn_S, state_ffn_conv, meta_tokens, ln_emb_g, ln_emb_b, w_in, b_in, b_fgate_a, g_norm_a, g_norm_b, hgrn_lb_logits, w_out, b_out, ln1_g, ln1_b, w_up, b_up, w_conv, b_conv, w_down, b_down, ln2_g, ln2_b):
    assert w_in.shape[0] == DEPTH == 1
    n_prompt, seq, _ = x_prompt.shape
    n_sample, dec_seq, _ = x_sample.shape
    row = lambda v: v.reshape(1, -1).astype(F32)

    gate0 = 4 * D_GROUP
    gate1 = gate0 + 2 * N_HEADS
    pad = D_PROJ - w_in.shape[2]
    w_in16 = w_in[0].astype(BF16)
    w_in_p = jnp.concatenate(
        [w_in16[:, :gate0], w_in16[:, gate1:], w_in16[:, gate0:gate1],
         jnp.zeros((D_MODEL, pad), BF16)], axis=1)
    b_in_p = jnp.concatenate(
        [b_in[0][:gate0], b_in[0][gate1:], b_in[0][gate0:gate1], jnp.zeros((pad,), b_in.dtype)]
    ).reshape(1, D_PROJ).astype(F32)
    bf_row = jnp.zeros((1, LANES), F32).at[0, N_HEADS:2 * N_HEADS].set(b_fgate_a[0].astype(F32))
    lb = jnp.cumsum(jax.nn.softmax(hgrn_lb_logits.astype(F32), axis=0), axis=0)[0].reshape(1, D_GROUP)
    ga, gb = row(g_norm_a[0]), row(g_norm_b[0])
    ln_e = (row(ln_emb_g), row(ln_emb_b))
    out_p = (w_out[0].astype(BF16), row(b_out[0]), row(ln1_g[0]), row(ln1_b[0]))
    ffn_p = (w_up[0].astype(BF16), row(b_up[0]), w_conv[0].astype(F32), row(b_conv[0]),
             w_down[0].astype(BF16), row(b_down[0]), row(ln2_g[0]), row(ln2_b[0]))

    def layer(x_rows, mixer_state, conv_state, *, n_seq, seq_len, t_mix, mix_seq_blk, tm,
              ffn_seq_blk, ffn_t, shared_init):
        proj = _in_proj(x_rows, *ln_e, w_in_p, b_in_p, tm=tm)
        mix, c_new, n_new, m_new, s_new = _mixer(
            proj, *mixer_state, bf_row, ga, gb, lb, n_seq=n_seq, n_chunks=seq_len // t_mix,
            t_len=t_mix, n_seq_blk=mix_seq_blk, shared_init=shared_init)
        x1 = _out_proj(x_rows, mix, *ln_e, *out_p, tm=tm)
        y, conv_new = _ffn(x1, conv_state, *ffn_p, n_seq=n_seq, seq_len=seq_len,
                           n_seq_blk=ffn_seq_blk, t_len=ffn_t, shared_init=shared_init)
        return y, c_new, n_new, m_new, s_new, conv_new

    zero_state = (jnp.zeros((1, N_HEADS, D_HEAD, D_HEAD), F32), jnp.zeros((1, N_HEADS, D_HEAD), F32),
                  jnp.zeros((1, 1, N_HEADS), F32), jnp.zeros((1, N_HEADS, D_HEAD, D_HEAD), F32))
    _, c_m, n_m, m_m, s_m, conv_m = layer(
        meta_tokens.astype(F32), zero_state, jnp.zeros((1, CONV_W - 1, D_FF), F32),
        n_seq=1, seq_len=N_META, t_mix=N_META, mix_seq_blk=1, tm=N_META, ffn_seq_blk=1,
        ffn_t=N_META, shared_init=False)

    y_p, c_p, n_p, m_p, s_p, conv_p = layer(
        x_prompt.reshape(n_prompt * seq, D_MODEL), (c_m, n_m, m_m, s_m), conv_m,
        n_seq=n_prompt, seq_len=seq, t_mix=128, mix_seq_blk=1, tm=256, ffn_seq_blk=1, ffn_t=256,
        shared_init=True)

    sample_state = (state_mlstm_C[0].astype(F32), state_mlstm_n[0].astype(F32),
                    state_mlstm_m[0].astype(F32).reshape(n_sample, 1, N_HEADS),
                    state_hgrn_S[0].astype(F32))
    y_s, c_s, n_s, m_s, s_s, conv_s = layer(
        x_sample.reshape(n_sample * dec_seq, D_MODEL), sample_state, state_ffn_conv[0].astype(F32),
        n_seq=n_sample, seq_len=dec_seq, t_mix=dec_seq, mix_seq_blk=4, tm=256, ffn_seq_blk=32,
        ffn_t=dec_seq, shared_init=False)

    lead = lambda v: v[None]
    return (y_p.reshape(n_prompt, seq, D_MODEL), y_s.reshape(n_sample, dec_seq, D_MODEL),
            lead(c_p), lead(n_p), lead(m_p.reshape(n_prompt, N_HEADS)), lead(s_p), lead(conv_p),
            lead(c_s), lead(n_s), lead(m_s.reshape(n_sample, N_HEADS)), lead(s_s), lead(conv_s))
```

```python
import functools

import jax
import jax.numpy as jnp
from jax import lax
from jax.experimental import pallas as pl
from jax.experimental.pallas import tpu as pltpu

D_MODEL = 1024
N_META = 16
N_HEADS = 4
D_HEAD = 128
D_GROUP = N_HEADS * D_HEAD
D_FF = 2816
CONV_W = 3
DEPTH = 1
ALPHA = (2.0 * DEPTH) ** 0.25
LN_EPS = 1e-5
RMS_EPS = 1e-6

LANES = 128
SUBLANES = 8
GATE_COL = 8 * D_GROUP
D_PROJ = GATE_COL + LANES
VMEM_LIMIT = 56 * 1024 * 1024

F32 = jnp.float32
BF16 = jnp.bfloat16
HIGHEST = lax.Precision.HIGHEST
NT_DIMS = (((1,), (1,)), ((), ()))
TN_DIMS = (((0,), (0,)), ((), ()))


def _layer_norm(x, g, b):
    mu = jnp.mean(x, axis=-1, keepdims=True)
    xc = x - mu
    var = jnp.mean(xc * xc, axis=-1, keepdims=True)
    return xc * lax.rsqrt(var + LN_EPS) * g + b


def _sigmoid(x):
    return 1.0 / (1.0 + jnp.exp(-x))


def _dot(a, b, precision=None):
    return jnp.dot(a, b, precision=precision, preferred_element_type=F32)


def _dot_nt(a, b, precision=None):
    return lax.dot_general(a, b, NT_DIMS, precision=precision, preferred_element_type=F32)


def _dot_tn(a, b):
    return lax.dot_general(a, b, TN_DIMS, preferred_element_type=F32)


def _in_proj_kernel(x_ref, g_ref, b_ref, w_ref, bias_ref, o_ref):
    xn = _layer_norm(x_ref[...], g_ref[...], b_ref[...])
    o_ref[...] = _dot(xn.astype(BF16), w_ref[...]) + bias_ref[...]


def _in_proj(x, ln_g, ln_b, w, bias, *, tm):
    n = x.shape[0]
    const = lambda i: (0, 0)
    return pl.pallas_call(
        _in_proj_kernel,
        grid=(n // tm,),
        in_specs=[
            pl.BlockSpec((tm, D_MODEL), lambda i: (i, 0)),
            pl.BlockSpec((1, D_MODEL), const),
            pl.BlockSpec((1, D_MODEL), const),
            pl.BlockSpec((D_MODEL, D_PROJ), const),
            pl.BlockSpec((1, D_PROJ), const),
        ],
        out_specs=pl.BlockSpec((tm, D_PROJ), lambda i: (i, 0)),
        out_shape=jax.ShapeDtypeStruct((n, D_PROJ), F32),
        compiler_params=pltpu.CompilerParams(
            dimension_semantics=("arbitrary",), vmem_limit_bytes=VMEM_LIMIT),
        name="in_proj",
    )(x, ln_g, ln_b, w, bias)


def _block_rows(x, level, t_len, row_in_block):
    size = 2 << level
    if size > SUBLANES:
        pieces = [jnp.broadcast_to(x[j * size + row_in_block:j * size + row_in_block + 1, :],
                                   (size, x.shape[1])) for j in range(t_len // size)]
        return pieces[0] if len(pieces) == 1 else jnp.concatenate(pieces, axis=0)
    x3 = x.reshape(t_len // SUBLANES, SUBLANES, x.shape[1])
    sub = lax.broadcasted_iota(jnp.int32, x3.shape, 1)
    out = None
    for j in range(SUBLANES // size):
        row = jnp.broadcast_to(x3[:, j * size + row_in_block:j * size + row_in_block + 1, :], x3.shape)
        out = row if out is None else jnp.where(sub >= j * size, row, out)
    return out.reshape(x.shape)


def _interleave_halves(lower, upper, level, t_len):
    half = 1 << level
    if half >= SUBLANES:
        pieces = []
        for j in range(t_len // (2 * half)):
            pieces.append(lower[2 * half * j:2 * half * j + half])
            pieces.append(upper[2 * half * j + half:2 * half * (j + 1)])
        return jnp.concatenate(pieces, axis=0)
    rows = lax.broadcasted_iota(jnp.int32, lower.shape, 0)
    return jnp.where((rows & half) != 0, upper, lower)


def _run_interleaved(first, second, second_per_first):
    gens, results = [first, second], [None, None]
    live = [True, True]
    while any(live):
        for g, steps in ((0, 1), (1, second_per_first)):
            for _ in range(steps):
                if live[g]:
                    try:
                        next(gens[g])
                    except StopIteration as stop:
                        results[g], live[g] = stop.value, False
    return results


def _mlstm_units(*, q, k, v, og, z, head, c_old, n_old, m0, g_norm, lane, causal, t_len):
    idx = range(len(q))
    q16 = [q[i].astype(BF16) for i in idx]
    k16 = [k[i].astype(BF16) for i in idx]
    qk = [_dot_nt(q16[i], k16[i]) for i in idx]
    qc = [_dot(q16[i], c_old[i].astype(BF16)) for i in idx]
    bs_row = []
    for i in idx:
        sel = jnp.where(lane == N_HEADS + head[i], 1.0, jnp.where(lane == head[i], -1.0, 0.0))
        bs_row.append(_dot_nt(sel, z[i], precision=HIGHEST))
    yield
    b_col =[z[i][:, N_HEADS + head[i]:N_HEADS + head[i] + 1] for i in idx]
    i_col = [z[i][:, head[i]:head[i] + 1] for i in idx]
    d = [jnp.where(causal, b_col[i] - bs_row[i], -jnp.inf) for i in idx]
    m_t = [jnp.maximum(b_col[i] + m0[i], jnp.max(d[i], axis=1, keepdims=True)) for i in idx]
    dec = [jnp.exp(b_col[i] + m0[i] - m_t[i]) for i in idx]
    sw = [jnp.exp(d[i] - m_t[i]) * qk[i] for i in idx]
    yield
    swv = [_dot(sw[i].astype(BF16), v[i].astype(BF16)) for i in idx]
    last = slice(t_len - 1, t_len)
    w_last = [jnp.exp(b_col[i][last] - b_col[i] + i_col[i] - m_t[i][last]) for i in idx]
    kv = [_dot_tn(k16[i], (w_last[i] * v[i]).astype(BF16)) for i in idx]
    yield
    c_new =[dec[i][last] * c_old[i] + kv[i] for i in idx]
    n_new = [dec[i][last] * n_old[i] + jnp.sum(w_last[i] * k[i], axis=0, keepdims=True) for i in idx]
    m_new = [m_t[i][last] for i in idx]
    den = [dec[i] * jnp.sum(q[i] * n_old[i], axis=1, keepdims=True)
           + jnp.sum(sw[i], axis=1, keepdims=True) for i in idx]
    hid = [(dec[i] * qc[i] + swv[i]) / jnp.maximum(jnp.abs(den[i]), jnp.exp(-m_t[i])) for i in idx]
    yield
    rms = [lax.rsqrt(jnp.mean(hid[i] * hid[i], axis=1, keepdims=True) + RMS_EPS) for i in idx]
    out = [_sigmoid(og[i]) * (hid[i] * rms[i] * g_norm[i]) for i in idx]
    return out, c_new, n_new, m_new


def _hgrn_units(*, qr, fr, iv, gr, lb, s_old_t, g_norm, tril, level_of, t_len):
    idx = range(len(qr))
    n_levels = t_len.bit_length() - 1
    f = [lb[i] + (1.0 - lb[i]) * _sigmoid(fr[i]) for i in idx]
    a = [_dot(tril, jnp.log(f[i]), precision=HIGHEST) for i in idx]
    yield
    kb = [(1.0 - lb[i]) * _sigmoid(-fr[i]) for i in idx]
    qb = [qr[i] * _sigmoid(qr[i]) for i in idx]
    iv16 = [iv[i].astype(BF16) for i in idx]
    diag = [_dot_nt(qb[i].astype(BF16), kb[i].astype(BF16)) for i in idx]
    scores = [jnp.where(level_of == -2, diag[i], 0.0) for i in idx]
    for level in range(n_levels):
        yield
        x16 = []
        for i in idx:
            base = _interleave_halves(kb[i], qb[i], level, t_len)
            if level == 0:
                x = base * _interleave_halves(jnp.ones_like(f[i]), f[i], 0, t_len)
            else:
                ref = _block_rows(a[i], level, t_len, (1 << level) - 1)
                x = base * jnp.exp(-jnp.abs(a[i] - ref))
            x16.append(x.astype(BF16))
        part = [_dot_nt(x16[i], x16[i]) for i in idx]
        scores = [jnp.where(level_of == level, part[i], scores[i]) for i in idx]
    yield
    last = slice(t_len - 1, t_len)
    q_in = [(qb[i] * jnp.exp(a[i])).astype(BF16) for i in idx]
    k_out = [(kb[i] * jnp.exp(a[i][last] - a[i])).astype(BF16) for i in idx]
    inter = [_dot_nt(q_in[i], s_old_t[i].astype(BF16)) for i in idx]
    intra = [_dot(scores[i].astype(BF16), iv16[i]) for i in idx]
    kv = [_dot_tn(iv16[i], k_out[i]) for i in idx]
    yield
    s_new_t =[jnp.exp(a[i][last]) * s_old_t[i] + kv[i] for i in idx]
    o = [inter[i] + intra[i] for i in idx]
    rms = [lax.rsqrt(jnp.mean(o[i] * o[i], axis=1, keepdims=True) + RMS_EPS) for i in idx]
    out = [_sigmoid(gr[i]) * (o[i] * rms[i] * g_norm[i]) for i in idx]
    return out, s_new_t


def _mixer_kernel(p_ref, c0_ref, n0_ref, m0_ref, s0_ref, bf_ref, ga_ref, gb_ref, lb_ref,
                  mix_ref, c_ref, n_ref, m_ref, s_ref, *, t_len, n_chunks, n_seq_blk):
    chunk = pl.program_id(1)

    @pl.when(chunk == 0)
    def _():
        c_ref[...] = c0_ref[...]
        n_ref[...] = n0_ref[...]
        m_ref[...] = m0_ref[...]
        for s in range(n_seq_blk):
            for h in range(N_HEADS):
                s_ref[s, h] = s0_ref[s, h].T

    rows = lax.broadcasted_iota(jnp.int32, (t_len, t_len), 0)
    cols = lax.broadcasted_iota(jnp.int32, (t_len, t_len), 1)
    causal = cols <= rows
    tril = causal.astype(F32)
    level_of = jnp.where(rows > cols, 31 - lax.clz(rows ^ cols), jnp.where(rows == cols, -2, -1))
    lane = lax.broadcasted_iota(jnp.int32, (t_len, LANES), 1)
    is_f = (lane >= N_HEADS) & (lane < 2 * N_HEADS)
    hd = lambda j, h: slice(j * D_GROUP + h * D_HEAD, j * D_GROUP + (h + 1) * D_HEAD)

    units = [(s, h) for s in range(n_seq_blk) for h in range(N_HEADS)]
    rs = lambda s: slice(s * t_len, (s + 1) * t_len)

    z_seq, n_seq, m_seq = [], [], []
    for s in range(n_seq_blk):
        gates = p_ref[rs(s), GATE_COL:GATE_COL + LANES]
        log_f = jnp.where(is_f, jax.nn.log_sigmoid(gates + bf_ref[...]), 0.0)
        cum_f = _dot(tril, log_f, precision=HIGHEST)
        z_seq.append(jnp.where(is_f, cum_f, jnp.where(lane < N_HEADS, gates, 0.0)))
        n_seq.append(n_ref[s])
        m_seq.append(m_ref[s])
    mlstm = _mlstm_units(
        q=[p_ref[rs(s), hd(0, h)] for s, h in units],
        k=[p_ref[rs(s), hd(1, h)] * (D_HEAD ** -0.5) for s, h in units],
        v=[p_ref[rs(s), hd(2, h)] for s, h in units],
        og=[p_ref[rs(s), hd(3, h)] for s, h in units],
        z=[z_seq[s] for s, h in units], head=[h for s, h in units],
        c_old=[c_ref[s, h] for s, h in units], n_old=[n_seq[s][h:h + 1, :] for s, h in units],
        m0=[m_seq[s][:, h:h + 1] for s, h in units], g_norm=[ga_ref[:, hd(0, h)] for s, h in units],
        lane=lane, causal=causal, t_len=t_len)
    hgrn = _hgrn_units(
        qr=[p_ref[rs(s), hd(4, h)] for s, h in units], fr=[p_ref[rs(s), hd(5, h)] for s, h in units],
        iv=[p_ref[rs(s), hd(6, h)] for s, h in units], gr=[p_ref[rs(s), hd(7, h)] for s, h in units],
        lb=[lb_ref[:, hd(0, h)] for s, h in units], s_old_t=[s_ref[s, h] for s, h in units],
        g_norm=[gb_ref[:, hd(0, h)] for s, h in units], tril=tril, level_of=level_of, t_len=t_len)
    (outs, c_new, n_new, m_new), (outs_b, s_new_t) = _run_interleaved(mlstm, hgrn, 2)

    for i, (s, h) in enumerate(units):
        mix_ref[rs(s), hd(0, h)] = outs[i].astype(mix_ref.dtype)
        c_ref[s, h] = c_new[i]
    head_lane = lax.broadcasted_iota(jnp.int32, (1, N_HEADS), 1)
    for s in range(n_seq_blk):
        n_ref[s] = jnp.concatenate(n_new[s * N_HEADS:(s + 1) * N_HEADS], axis=0)
        m_row = m_seq[s]
        for h in range(N_HEADS):
            m_row = jnp.where(head_lane == h, m_new[s * N_HEADS + h], m_row)
        m_ref[s] = m_row

    for i, (s, h) in enumerate(units):
        mix_ref[rs(s), hd(1, h)] = outs_b[i].astype(mix_ref.dtype)
        s_ref[s, h] = s_new_t[i]

    @pl.when(chunk == n_chunks - 1)
    def _():
        for s in range(n_seq_blk):
            for h in range(N_HEADS):
                s_ref[s, h] = s_ref[s, h].T


def _mixer(proj, c0, n0, m0, s0, bf_row, ga, gb, lb, *, n_seq, n_chunks, t_len, n_seq_blk,
           shared_init):
    assert not shared_init or n_seq_blk == 1
    assert n_chunks == 1 or n_seq_blk == 1
    nb = n_seq_blk
    init = (lambda b, c: (0, 0, 0, 0)) if shared_init else (lambda b, c: (b, 0, 0, 0))
    init3 = (lambda b, c: (0, 0, 0)) if shared_init else (lambda b, c: (b, 0, 0))
    const = lambda b, c: (0, 0)
    state4 = pl.BlockSpec((nb, N_HEADS, D_HEAD, D_HEAD), lambda b, c: (b, 0, 0, 0))
    return pl.pallas_call(
        functools.partial(_mixer_kernel, t_len=t_len, n_chunks=n_chunks, n_seq_blk=nb),
        grid=(n_seq // nb, n_chunks),
        in_specs=[
            pl.BlockSpec((nb * t_len, D_PROJ), lambda b, c: (b * n_chunks + c, 0)),
            pl.BlockSpec((nb, N_HEADS, D_HEAD, D_HEAD), init),
            pl.BlockSpec((nb, N_HEADS, D_HEAD), init3),
            pl.BlockSpec((nb, 1, N_HEADS), init3),
            pl.BlockSpec((nb, N_HEADS, D_HEAD, D_HEAD), init),
            pl.BlockSpec((1, LANES), const),
            pl.BlockSpec((1, D_GROUP), const),
            pl.BlockSpec((1, D_GROUP), const),
            pl.BlockSpec((1, D_GROUP), const),
        ],
        out_specs=[
            pl.BlockSpec((nb * t_len, D_MODEL), lambda b, c: (b * n_chunks + c, 0)),
            state4,
            pl.BlockSpec((nb, N_HEADS, D_HEAD), lambda b, c: (b, 0, 0)),
            pl.BlockSpec((nb, 1, N_HEADS), lambda b, c: (b, 0, 0)),
            state4,
        ],
        out_shape=[
            jax.ShapeDtypeStruct((n_seq * n_chunks * t_len, D_MODEL), BF16),
            jax.ShapeDtypeStruct((n_seq, N_HEADS, D_HEAD, D_HEAD), F32),
            jax.ShapeDtypeStruct((n_seq, N_HEADS, D_HEAD), F32),
            jax.ShapeDtypeStruct((n_seq, 1, N_HEADS), F32),
            jax.ShapeDtypeStruct((n_seq, N_HEADS, D_HEAD, D_HEAD), F32),
        ],
        compiler_params=pltpu.CompilerParams(
            dimension_semantics=("arbitrary", "arbitrary"), vmem_limit_bytes=VMEM_LIMIT),
        name=f"mixer_t{t_len}",
    )(proj, c0, n0, m0, s0, bf_row, ga, gb, lb)


def _out_proj_kernel(x_ref, mix_ref, ge_ref, be_ref, w_ref, bias_ref, g_ref, b_ref, o_ref):
    xn = _layer_norm(x_ref[...], ge_ref[...], be_ref[...])
    y = _dot(mix_ref[...], w_ref[...]) + bias_ref[...]
    o_ref[...] = _layer_norm(ALPHA * xn + y, g_ref[...], b_ref[...])


def _out_proj(x, mix, ln_e_g, ln_e_b, w, bias, ln_g, ln_b, *, tm):
    n = x.shape[0]
    const = lambda i: (0, 0)
    row = pl.BlockSpec((tm, D_MODEL), lambda i: (i, 0))
    vec = pl.BlockSpec((1, D_MODEL), const)
    return pl.pallas_call(
        _out_proj_kernel,
        grid=(n // tm,),
        in_specs=[row, row, vec, vec, pl.BlockSpec((D_MODEL, D_MODEL), const), vec, vec, vec],
        out_specs=row,
        out_shape=jax.ShapeDtypeStruct((n, D_MODEL), F32),
        compiler_params=pltpu.CompilerParams(
            dimension_semantics=("arbitrary",), vmem_limit_bytes=VMEM_LIMIT),
        name="out_proj",
    )(x, mix, ln_e_g, ln_e_b, w, bias, ln_g, ln_b)


def _ffn_kernel(x_ref, cs_ref, wu_ref, bu_ref, wc_ref, bc_ref, wd_ref, bd_ref, g_ref, b_ref,
                y_ref, nc_ref, full_ref, *, n_seq_blk, t_len):
    hist = SUBLANES - (CONV_W - 1)

    @pl.when(pl.program_id(1) == 0)
    def _():
        full_ref[:, hist:SUBLANES, :] = cs_ref[...]

    x = x_ref[...]
    up = _dot(x.astype(BF16), wu_ref[...]) + bu_ref[...]
    u = up[:, :D_FF].reshape(n_seq_blk, t_len, D_FF)
    gate = up[:, D_FF:].reshape(n_seq_blk, t_len, D_FF)
    full_ref[:, SUBLANES:SUBLANES + t_len, :] = u
    conv = bc_ref[...] + u * wc_ref[CONV_W - 1:CONV_W, :]
    for j in range(CONV_W - 1):
        conv = conv + full_ref[:, hist + j:hist + j + t_len, :] * wc_ref[j:j + 1, :]
    last = full_ref[:, hist + t_len:SUBLANES + t_len, :]
    nc_ref[...] = last
    full_ref[:, hist:SUBLANES, :] = last
    act = (conv * _sigmoid(conv) * gate).reshape(n_seq_blk * t_len, D_FF)
    ffn = _dot(act.astype(BF16), wd_ref[...]) + bd_ref[...]
    y_ref[...] = _layer_norm(ALPHA * x + ffn, g_ref[...], b_ref[...])


def _ffn(x, conv_state, w_up, b_up, w_conv, b_conv, w_down, b_down, ln_g, ln_b,
         *, n_seq, seq_len, n_seq_blk, t_len, shared_init):
    n_t = seq_len // t_len
    rows = n_seq_blk * t_len
    const = lambda s, t: (0, 0)
    cs_map = (lambda s, t: (0, 0, 0)) if shared_init else (lambda s, t: (s, 0, 0))
    row = pl.BlockSpec((rows, D_MODEL), lambda s, t: (s * n_t + t, 0))
    vec = pl.BlockSpec((1, D_MODEL), const)
    return pl.pallas_call(
        functools.partial(_ffn_kernel, n_seq_blk=n_seq_blk, t_len=t_len),
        grid=(n_seq // n_seq_blk, n_t),
        in_specs=[
            row,
            pl.BlockSpec((n_seq_blk, CONV_W - 1, D_FF), cs_map),
            pl.BlockSpec((D_MODEL, 2 * D_FF), const),
            pl.BlockSpec((1, 2 * D_FF), const),
            pl.BlockSpec((CONV_W, D_FF), const),
            pl.BlockSpec((1, D_FF), const),
            pl.BlockSpec((D_FF, D_MODEL), const),
            vec, vec, vec,
        ],
        out_specs=[row, pl.BlockSpec((n_seq_blk, CONV_W - 1, D_FF), lambda s, t: (s, 0, 0))],
        out_shape=[
            jax.ShapeDtypeStruct((n_seq * seq_len, D_MODEL), F32),
            jax.ShapeDtypeStruct((n_seq, CONV_W - 1, D_FF), F32),
        ],
        scratch_shapes=[pltpu.VMEM((n_seq_blk, SUBLANES + t_len, D_FF), F32)],
        compiler_params=pltpu.CompilerParams(
            dimension_semantics=("arbitrary", "arbitrary"), vmem_limit_bytes=VMEM_LIMIT),
        name=f"ffn_t{t_len}",
    )(x, conv_state, w_up, b_up, w_conv, b_conv, w_down, b_down, ln_g, ln_b)


def kernel(x_prompt, x_sample, state_mlstm_C, state_mlstm_n, state_mlstm_m, state_hgrn_S, state_ffn_conv, meta_tokens, ln_emb_g, ln_emb_b, w_in, b_in, b_fgate_a, g_norm_a, g_norm_b, hgrn_lb_logits, w_out, b_out, ln1_g, ln1_b, w_up, b_up, w_conv, b_conv, w_down, b_down, ln2_g, ln2_b):
    assert w_in.shape[0] == DEPTH == 1
    n_prompt, seq, _ = x_prompt.shape
    n_sample, dec_seq, _ = x_sample.shape
    row = lambda v: v.reshape(1, -1).astype(F32)

    gate0 = 4 * D_GROUP
    gate1 = gate0 + 2 * N_HEADS
    pad = D_PROJ - w_in.shape[2]
    w_in16 = w_in[0].astype(BF16)
    w_in_p = jnp.concatenate(
        [w_in16[:, :gate0], w_in16[:, gate1:], w_in16[:, gate0:gate1],
         jnp.zeros((D_MODEL, pad), BF16)], axis=1)
    b_in_p = jnp.concatenate(
        [b_in[0][:gate0], b_in[0][gate1:], b_in[0][gate0:gate1], jnp.zeros((pad,), b_in.dtype)]
    ).reshape(1, D_PROJ).astype(F32)
    bf_row = jnp.zeros((1, LANES), F32).at[0, N_HEADS:2 * N_HEADS].set(b_fgate_a[0].astype(F32))
    lb = jnp.cumsum(jax.nn.softmax(hgrn_lb_logits.astype(F32), axis=0), axis=0)[0].reshape(1, D_GROUP)
    ga, gb = row(g_norm_a[0]), row(g_norm_b[0])
    ln_e = (row(ln_emb_g), row(ln_emb_b))
    out_p = (w_out[0].astype(BF16), row(b_out[0]), row(ln1_g[0]), row(ln1_b[0]))
    ffn_p = (w_up[0].astype(BF16), row(b_up[0]), w_conv[0].astype(F32), row(b_conv[0]),
             w_down[0].astype(BF16), row(b_down[0]), row(ln2_g[0]), row(ln2_b[0]))

    def layer(x_rows, mixer_state, conv_state, *, n_seq, seq_len, t_mix, mix_seq_blk, tm,
              ffn_seq_blk, ffn_t, shared_init):
        proj = _in_proj(x_rows, *ln_e, w_in_p, b_in_p, tm=tm)
        mix, c_new, n_new, m_new, s_new = _mixer(
            proj, *mixer_state, bf_row, ga, gb, lb, n_seq=n_seq, n_chunks=seq_len // t_mix,
            t_len=t_mix, n_seq_blk=mix_seq_blk, shared_init=shared_init)
        x1 = _out_proj(x_rows, mix, *ln_e, *out_p, tm=tm)
        y, conv_new = _ffn(x1, conv_state, *ffn_p, n_seq=n_seq, seq_len=seq_len,
                           n_seq_blk=ffn_seq_blk, t_len=ffn_t, shared_init=shared_init)
        return y, c_new, n_new, m_new, s_new, conv_new

    zero_state = (jnp.zeros((1, N_HEADS, D_HEAD, D_HEAD), F32), jnp.zeros((1, N_HEADS, D_HEAD), F32),
                  jnp.zeros((1, 1, N_HEADS), F32), jnp.zeros((1, N_HEADS, D_HEAD, D_HEAD), F32))
    _, c_m, n_m, m_m, s_m, conv_m = layer(
        meta_tokens.astype(F32), zero_state, jnp.zeros((1, CONV_W - 1, D_FF), F32),
        n_seq=1, seq_len=N_META, t_mix=N_META, mix_seq_blk=1, tm=N_META, ffn_seq_blk=1,
        ffn_t=N_META, shared_init=False)

    y_p, c_p, n_p, m_p, s_p, conv_p = layer(
        x_prompt.reshape(n_prompt * seq, D_MODEL), (c_m, n_m, m_m, s_m), conv_m,
        n_seq=n_prompt, seq_len=seq, t_mix=128, mix_seq_blk=1, tm=256, ffn_seq_blk=1, ffn_t=256,
        shared_init=True)

    sample_state = (state_mlstm_C[0].astype(F32), state_mlstm_n[0].astype(F32),
                    state_mlstm_m[0].astype(F32).reshape(n_sample, 1, N_HEADS),
                    state_hgrn_S[0].astype(F32))
    y_s, c_s, n_s, m_s, s_s, conv_s = layer(
        x_sample.reshape(n_sample * dec_seq, D_MODEL), sample_state, state_ffn_conv[0].astype(F32),
        n_seq=n_sample, seq_len=dec_seq, t_mix=dec_seq, mix_seq_blk=4, tm=256, ffn_seq_blk=32,
        ffn_t=dec_seq, shared_init=False)

    lead = lambda v: v[None]
    return (y_p.reshape(n_prompt, seq, D_MODEL), y_s.reshape(n_sample, dec_seq, D_MODEL),
            lead(c_p), lead(n_p), lead(m_p.reshape(n_prompt, N_HEADS)), lead(s_p), lead(conv_p),
            lead(c_s), lead(n_s), lead(m_s.reshape(n_sample, N_HEADS)), lead(s_s), lead(conv_s))
```

```python
import functools

import jax
import jax.numpy as jnp
from jax import lax
from jax.experimental import pallas as pl
from jax.experimental.pallas import tpu as pltpu

D_MODEL = 1024
N_META = 16
N_HEADS = 4
D_HEAD = 128
D_GROUP = N_HEADS * D_HEAD
D_FF = 2816
CONV_W = 3
DEPTH = 1
ALPHA = (2.0 * DEPTH) ** 0.25
LN_EPS = 1e-5
RMS_EPS = 1e-6

LANES = 128
SUBLANES = 8
GATE_COL = 8 * D_GROUP
D_PROJ = GATE_COL + LANES
IN_PROJ_STAGE_COLS = 512
PROMPT_CHUNK = 128
VMEM_LIMIT = 56 * 1024 * 1024

F32 = jnp.float32
BF16 = jnp.bfloat16
HIGHEST = lax.Precision.HIGHEST
NT_DIMS = (((1,), (1,)), ((), ()))
TN_DIMS = (((0,), (0,)), ((), ()))


def _layer_norm(x, g, b):
    mu = jnp.mean(x, axis=-1, keepdims=True)
    xc = x - mu
    var = jnp.mean(xc * xc, axis=-1, keepdims=True)
    return xc * lax.rsqrt(var + LN_EPS) * g + b


def _sigmoid(x):
    return 1.0 / (1.0 + jnp.exp(-x))


def _dot(a, b, precision=None):
    return jnp.dot(a, b, precision=precision, preferred_element_type=F32)


def _dot_nt(a, b, precision=None):
    return lax.dot_general(a, b, NT_DIMS, precision=precision, preferred_element_type=F32)


def _dot_tn(a, b):
    return lax.dot_general(a, b, TN_DIMS, preferred_element_type=F32)


def _in_proj_kernel(x_ref, g_ref, b_ref, w_ref, bias_ref, o_ref):
    xn = _layer_norm(x_ref[...], g_ref[...], b_ref[...])
    o_ref[...] = _dot(xn.astype(BF16), w_ref[...]) + bias_ref[...]


def _in_proj(x, ln_g, ln_b, w, bias, *, tm):
    n = x.shape[0]
    const = lambda i: (0, 0)
    return pl.pallas_call(
        _in_proj_kernel,
        grid=(n // tm,),
        in_specs=[
            pl.BlockSpec((tm, D_MODEL), lambda i: (i, 0)),
            pl.BlockSpec((1, D_MODEL), const),
            pl.BlockSpec((1, D_MODEL), const),
            pl.BlockSpec((D_MODEL, D_PROJ), const),
            pl.BlockSpec((1, D_PROJ), const),
        ],
        out_specs=pl.BlockSpec((tm, D_PROJ), lambda i: (i, 0)),
        out_shape=jax.ShapeDtypeStruct((n, D_PROJ), F32),
        compiler_params=pltpu.CompilerParams(
            dimension_semantics=("arbitrary",), vmem_limit_bytes=VMEM_LIMIT),
        name="in_proj",
    )(x, ln_g, ln_b, w, bias)


def _block_rows(x, level, t_len, row_in_block):
    size = 2 << level
    if size > SUBLANES:
        pieces = [jnp.broadcast_to(x[j * size + row_in_block:j * size + row_in_block + 1, :],
                                   (size, x.shape[1])) for j in range(t_len // size)]
        return pieces[0] if len(pieces) == 1 else jnp.concatenate(pieces, axis=0)
    x3 = x.reshape(t_len // SUBLANES, SUBLANES, x.shape[1])
    sub = lax.broadcasted_iota(jnp.int32, x3.shape, 1)
    out = None
    for j in range(SUBLANES // size):
        row = jnp.broadcast_to(x3[:, j * size + row_in_block:j * size + row_in_block + 1, :], x3.shape)
        out = row if out is None else jnp.where(sub >= j * size, row, out)
    return out.reshape(x.shape)


def _interleave_halves(lower, upper, level, t_len):
    half = 1 << level
    if half >= SUBLANES:
        pieces = []
        for j in range(t_len // (2 * half)):
            pieces.append(lower[2 * half * j:2 * half * j + half])
            pieces.append(upper[2 * half * j + half:2 * half * (j + 1)])
        return jnp.concatenate(pieces, axis=0)
    rows = lax.broadcasted_iota(jnp.int32, lower.shape, 0)
    return jnp.where((rows & half) != 0, upper, lower)


def _run_interleaved(gens, stages_per_round):
    results = [None] * len(gens)
    live = [True] * len(gens)
    while any(live):
        for g, steps in enumerate(stages_per_round):
            for _ in range(steps):
                if live[g]:
                    try:
                        next(gens[g])
                    except StopIteration as stop:
                        results[g], live[g] = stop.value, False
    return results


def _cumsum_rows(tril16, x):
    hi = x.astype(BF16)
    rest = x - hi.astype(F32)
    mid = rest.astype(BF16)
    lo = (rest - mid.astype(F32)).astype(BF16)
    return _dot(tril16, hi) + _dot(tril16, mid) + _dot(tril16, lo)


def _mlstm_units(*, q, k, v, og, z, zt, head, c_old, n_old, m0, g_norm, causal, t_len):
    idx = range(len(q))
    q16 = [q[i].astype(BF16) for i in idx]
    k16 = [k[i].astype(BF16) for i in idx]
    qk = [_dot_nt(q16[i], k16[i]) for i in idx]
    qc = [_dot(q16[i], c_old[i].astype(BF16)) for i in idx]
    bs_row = [zt[i][N_HEADS + head[i]:N_HEADS + head[i] + 1, :] - zt[i][head[i]:head[i] + 1, :]
              for i in idx]
    yield
    b_col = [z[i][:, N_HEADS + head[i]:N_HEADS + head[i] + 1] for i in idx]
    i_col = [z[i][:, head[i]:head[i] + 1] for i in idx]
    d = [jnp.where(causal, b_col[i] - bs_row[i], -jnp.inf) for i in idx]
    m_t = [jnp.maximum(b_col[i] + m0[i], jnp.max(d[i], axis=1, keepdims=True)) for i in idx]
    dec = [jnp.exp(b_col[i] + m0[i] - m_t[i]) for i in idx]
    sw = [jnp.exp(d[i] - m_t[i]) * qk[i] for i in idx]
    yield
    swv = [_dot(sw[i].astype(BF16), v[i].astype(BF16)) for i in idx]
    last = slice(t_len - 1, t_len)
    w_last = [jnp.exp(b_col[i][last] - b_col[i] + i_col[i] - m_t[i][last]) for i in idx]
    kv = [_dot_tn(k16[i], (w_last[i] * v[i]).astype(BF16)) for i in idx]
    yield
    c_new =[dec[i][last] * c_old[i] + kv[i] for i in idx]
    n_new = [dec[i][last] * n_old[i] + jnp.sum(w_last[i] * k[i], axis=0, keepdims=True) for i in idx]
    m_new = [m_t[i][last] for i in idx]
    den = [dec[i] * jnp.sum(q[i] * n_old[i], axis=1, keepdims=True)
           + jnp.sum(sw[i], axis=1, keepdims=True) for i in idx]
    hid = [(dec[i] * qc[i] + swv[i]) / jnp.maximum(jnp.abs(den[i]), jnp.exp(-m_t[i])) for i in idx]
    yield
    rms = [lax.rsqrt(jnp.mean(hid[i] * hid[i], axis=1, keepdims=True) + RMS_EPS) for i in idx]
    out = [_sigmoid(og[i]) * (hid[i] * rms[i] * g_norm[i]) for i in idx]
    return out, c_new, n_new, m_new


def _hgrn_units(*, qr, fr, iv, gr, lb, s_old_t, g_norm, tril, level_of, t_len):
    idx = range(len(qr))
    n_levels = t_len.bit_length() - 1
    f = [lb[i] + (1.0 - lb[i]) * _sigmoid(fr[i]) for i in idx]
    a = [_cumsum_rows(tril, jnp.log(f[i])) for i in idx]
    yield
    kb = [(1.0 - lb[i]) * _sigmoid(-fr[i]) for i in idx]
    qb = [qr[i] * _sigmoid(qr[i]) for i in idx]
    iv16 = [iv[i].astype(BF16) for i in idx]
    diag = [_dot_nt(qb[i].astype(BF16), kb[i].astype(BF16)) for i in idx]
    scores = [jnp.where(level_of == -2, diag[i], 0.0) for i in idx]
    for level in range(n_levels):
        yield
        x16 = []
        for i in idx:
            base = _interleave_halves(kb[i], qb[i], level, t_len)
            if level == 0:
                x = base * _interleave_halves(jnp.ones_like(f[i]), f[i], 0, t_len)
            else:
                ref = _block_rows(a[i], level, t_len, (1 << level) - 1)
                x = base * jnp.exp(-jnp.abs(a[i] - ref))
            x16.append(x.astype(BF16))
        part = [_dot_nt(x16[i], x16[i]) for i in idx]
        scores = [jnp.where(level_of == level, part[i], scores[i]) for i in idx]
    yield
    last = slice(t_len - 1, t_len)
    q_in = [(qb[i] * jnp.exp(a[i])).astype(BF16) for i in idx]
    k_out = [(kb[i] * jnp.exp(a[i][last] - a[i])).astype(BF16) for i in idx]
    inter = [_dot_nt(q_in[i], s_old_t[i].astype(BF16)) for i in idx]
    intra = [_dot(scores[i].astype(BF16), iv16[i]) for i in idx]
    kv = [_dot_tn(iv16[i], k_out[i]) for i in idx]
    yield
    s_new_t =[jnp.exp(a[i][last]) * s_old_t[i] + kv[i] for i in idx]
    o = [inter[i] + intra[i] for i in idx]
    rms = [lax.rsqrt(jnp.mean(o[i] * o[i], axis=1, keepdims=True) + RMS_EPS) for i in idx]
    out = [_sigmoid(gr[i]) * (o[i] * rms[i] * g_norm[i]) for i in idx]
    return out, s_new_t


def _load_state(c0_ref, n0_ref, m0_ref, s0_ref, c_ref, n_ref, m_ref, s_ref, n_seq_blk):
    c_ref[...] = c0_ref[...]
    n_ref[...] = n0_ref[...]
    m_ref[...] = m0_ref[...]
    for s in range(n_seq_blk):
        for h in range(N_HEADS):
            s_ref[s, h] = s0_ref[s, h].T


def _finish_state(s_ref, n_seq_blk):
    for s in range(n_seq_blk):
        for h in range(N_HEADS):
            s_ref[s, h] = s_ref[s, h].T


def _mixer_body(p_ref, bf_ref, ga_ref, gb_ref, lb_ref, c_ref, n_ref, m_ref, s_ref,
                *, t_len, n_seq_blk, side_stages=None):
    rows = lax.broadcasted_iota(jnp.int32, (t_len, t_len), 0)
    cols = lax.broadcasted_iota(jnp.int32, (t_len, t_len), 1)
    causal = cols <= rows
    tril = causal.astype(BF16)
    level_of = jnp.where(rows > cols, 31 - lax.clz(rows ^ cols), jnp.where(rows == cols, -2, -1))
    lane = lax.broadcasted_iota(jnp.int32, (t_len, LANES), 1)
    is_f = (lane >= N_HEADS) & (lane < 2 * N_HEADS)
    hd = lambda j, h: slice(j * D_GROUP + h * D_HEAD, j * D_GROUP + (h + 1) * D_HEAD)

    units = [(s, h) for s in range(n_seq_blk) for h in range(N_HEADS)]
    rs = lambda s: slice(s * t_len, (s + 1) * t_len)

    z_seq, zt_seq, n_seq, m_seq = [], [], [], []
    for s in range(n_seq_blk):
        gates = p_ref[rs(s), GATE_COL:GATE_COL + LANES]
        log_f = jnp.where(is_f, jax.nn.log_sigmoid(gates + bf_ref[...]), 0.0)
        cum_f = _cumsum_rows(tril, log_f)
        z_seq.append(jnp.where(is_f, cum_f, jnp.where(lane < N_HEADS, gates, 0.0)))
        zt_seq.append(z_seq[s].T)
        n_seq.append(n_ref[s])
        m_seq.append(m_ref[s])
    mlstm = _mlstm_units(
        q=[p_ref[rs(s), hd(0, h)] for s, h in units],
        k=[p_ref[rs(s), hd(1, h)] * (D_HEAD ** -0.5) for s, h in units],
        v=[p_ref[rs(s), hd(2, h)] for s, h in units],
        og=[p_ref[rs(s), hd(3, h)] for s, h in units],
        z=[z_seq[s] for s, h in units], zt=[zt_seq[s] for s, h in units], head=[h for s, h in units],
        c_old=[c_ref[s, h] for s, h in units], n_old=[n_seq[s][h:h + 1, :] for s, h in units],
        m0=[m_seq[s][:, h:h + 1] for s, h in units], g_norm=[ga_ref[:, hd(0, h)] for s, h in units],
        causal=causal, t_len=t_len)
    hgrn = _hgrn_units(
        qr=[p_ref[rs(s), hd(4, h)] for s, h in units], fr=[p_ref[rs(s), hd(5, h)] for s, h in units],
        iv=[p_ref[rs(s), hd(6, h)] for s, h in units], gr=[p_ref[rs(s), hd(7, h)] for s, h in units],
        lb=[lb_ref[:, hd(0, h)] for s, h in units], s_old_t=[s_ref[s, h] for s, h in units],
        g_norm=[gb_ref[:, hd(0, h)] for s, h in units], tril=tril, level_of=level_of, t_len=t_len)
    gens, per_round = [mlstm, hgrn], [1, 2]
    if side_stages is not None:
        gens, per_round = gens + [side_stages], per_round + [2]
    results = _run_interleaved(gens, per_round)
    (outs, c_new, n_new, m_new), (outs_b, s_new_t) = results[:2]
    side_result = results[2] if side_stages is not None else None

    for i, (s, h) in enumerate(units):
        c_ref[s, h] = c_new[i]
        s_ref[s, h] = s_new_t[i]
    head_lane = lax.broadcasted_iota(jnp.int32, (1, N_HEADS), 1)
    for s in range(n_seq_blk):
        n_ref[s] = jnp.concatenate(n_new[s * N_HEADS:(s + 1) * N_HEADS], axis=0)
        m_row = m_seq[s]
        for h in range(N_HEADS):
            m_row = jnp.where(head_lane == h, m_new[s * N_HEADS + h], m_row)
        m_ref[s] = m_row
    return outs, outs_b, side_result


def _mixer_kernel(p_ref, c0_ref, n0_ref, m0_ref, s0_ref, bf_ref, ga_ref, gb_ref, lb_ref,
                  mix_ref, c_ref, n_ref, m_ref, s_ref, *, t_len, n_chunks, n_seq_blk):
    chunk = pl.program_id(1)

    @pl.when(chunk == 0)
    def _():
        _load_state(c0_ref, n0_ref, m0_ref, s0_ref, c_ref, n_ref, m_ref, s_ref, n_seq_blk)

    outs_a, outs_b, _ = _mixer_body(p_ref, bf_ref, ga_ref, gb_ref, lb_ref, c_ref, n_ref, m_ref, s_ref,
                                    t_len=t_len, n_seq_blk=n_seq_blk)
    for i in range(n_seq_blk * N_HEADS):
        s, h = divmod(i, N_HEADS)
        rows = slice(s * t_len, (s + 1) * t_len)
        mix_ref[rows, h * D_HEAD:(h + 1) * D_HEAD] = outs_a[i].astype(mix_ref.dtype)
        mix_ref[rows, D_GROUP + h * D_HEAD:D_GROUP + (h + 1) * D_HEAD] = outs_b[i].astype(mix_ref.dtype)

    @pl.when(chunk == n_chunks - 1)
    def _():
        _finish_state(s_ref, n_seq_blk)


def _mixer(proj, c0, n0, m0, s0, bf_row, ga, gb, lb, *, n_seq, n_chunks, t_len, n_seq_blk,
           shared_init):
    assert not shared_init or n_seq_blk == 1
    assert n_chunks == 1 or n_seq_blk == 1
    nb = n_seq_blk
    init = (lambda b, c: (0, 0, 0, 0)) if shared_init else (lambda b, c: (b, 0, 0, 0))
    init3 = (lambda b, c: (0, 0, 0)) if shared_init else (lambda b, c: (b, 0, 0))
    const = lambda b, c: (0, 0)
    state4 = pl.BlockSpec((nb, N_HEADS, D_HEAD, D_HEAD), lambda b, c: (b, 0, 0, 0))
    return pl.pallas_call(
        functools.partial(_mixer_kernel, t_len=t_len, n_chunks=n_chunks, n_seq_blk=nb),
        grid=(n_seq // nb, n_chunks),
        in_specs=[
            pl.BlockSpec((nb * t_len, D_PROJ), lambda b, c: (b * n_chunks + c, 0)),
            pl.BlockSpec((nb, N_HEADS, D_HEAD, D_HEAD), init),
            pl.BlockSpec((nb, N_HEADS, D_HEAD), init3),
            pl.BlockSpec((nb, 1, N_HEADS), init3),
            pl.BlockSpec((nb, N_HEADS, D_HEAD, D_HEAD), init),
            pl.BlockSpec((1, LANES), const),
            pl.BlockSpec((1, D_GROUP), const),
            pl.BlockSpec((1, D_GROUP), const),
            pl.BlockSpec((1, D_GROUP), const),
        ],
        out_specs=[
            pl.BlockSpec((nb * t_len, D_MODEL), lambda b, c: (b * n_chunks + c, 0)),
            state4,
            pl.BlockSpec((nb, N_HEADS, D_HEAD), lambda b, c: (b, 0, 0)),
            pl.BlockSpec((nb, 1, N_HEADS), lambda b, c: (b, 0, 0)),
            state4,
        ],
        out_shape=[
            jax.ShapeDtypeStruct((n_seq * n_chunks * t_len, D_MODEL), BF16),
            jax.ShapeDtypeStruct((n_seq, N_HEADS, D_HEAD, D_HEAD), F32),
            jax.ShapeDtypeStruct((n_seq, N_HEADS, D_HEAD), F32),
            jax.ShapeDtypeStruct((n_seq, 1, N_HEADS), F32),
            jax.ShapeDtypeStruct((n_seq, N_HEADS, D_HEAD, D_HEAD), F32),
        ],
        compiler_params=pltpu.CompilerParams(
            dimension_semantics=("arbitrary", "arbitrary"), vmem_limit_bytes=VMEM_LIMIT),
        name=f"mixer_t{t_len}",
    )(proj, c0, n0, m0, s0, bf_row, ga, gb, lb)


def _in_proj_stages(x_ref, g_ref, b_ref, w_ref, bias_ref):
    xn = _layer_norm(x_ref[...], g_ref[...], b_ref[...])
    x16 = xn.astype(BF16)
    blocks = []
    for lo in range(0, D_PROJ, IN_PROJ_STAGE_COLS):
        hi = min(lo + IN_PROJ_STAGE_COLS, D_PROJ)
        yield
        blocks.append(_dot(x16, w_ref[:, lo:hi]) + bias_ref[:, lo:hi])
    return xn, blocks


def _store_proj(projected, proj_scr, xn_scr):
    xn, blocks = projected
    xn_scr[...] = xn
    for j, block in enumerate(blocks):
        proj_scr[:, j * IN_PROJ_STAGE_COLS:j * IN_PROJ_STAGE_COLS + block.shape[1]] = block


def _prompt_kernel(x0_ref, xnext_ref, ge_ref, be_ref, win_ref, bin_ref, c0_ref, n0_ref, m0_ref, s0_ref,
                   bf_ref, ga_ref, gb_ref, lb_ref, wout_ref, bout_ref, g1_ref, b1_ref,
                   x1_ref, c_ref, n_ref, m_ref, s_ref, proj_scr, xn_scr, *, t_len, n_chunks):
    chunk = pl.program_id(1)

    @pl.when((pl.program_id(0) == 0) & (chunk == 0))
    def _():
        first = _in_proj_stages(x0_ref, ge_ref, be_ref, win_ref, bin_ref)
        _store_proj(_run_interleaved([first], [1])[0], proj_scr, xn_scr)

    @pl.when(chunk == 0)
    def _():
        _load_state(c0_ref, n0_ref, m0_ref, s0_ref, c_ref, n_ref, m_ref, s_ref, 1)

    next_proj = _in_proj_stages(xnext_ref, ge_ref, be_ref, win_ref, bin_ref)
    outs_a, outs_b, projected = _mixer_body(
        proj_scr, bf_ref, ga_ref, gb_ref, lb_ref, c_ref, n_ref, m_ref, s_ref,
        t_len=t_len, n_seq_blk=1, side_stages=next_proj)
    mix = jnp.concatenate([o.astype(BF16) for o in outs_a + outs_b], axis=1)
    y = _dot(mix, wout_ref[...]) + bout_ref[...]
    x1_ref[...] = _layer_norm(ALPHA * xn_scr[...] + y, g1_ref[...], b1_ref[...])
    _store_proj(projected, proj_scr, xn_scr)

    @pl.when(chunk == n_chunks - 1)
    def _():
        _finish_state(s_ref, 1)


def _prompt_mixer(x, ln_e_g, ln_e_b, w_in, b_in, c0, n0, m0, s0, bf_row, ga, gb, lb,
                  w_out, b_out, ln_g, ln_b, *, n_seq, n_chunks, t_len):
    n_tiles = n_seq * n_chunks
    const = lambda b, c: (0, 0)
    init4 = lambda b, c: (0, 0, 0, 0)
    init3 = lambda b, c: (0, 0, 0)
    vec = pl.BlockSpec((1, D_MODEL), const)
    grp = pl.BlockSpec((1, D_GROUP), const)
    state4 = pl.BlockSpec((1, N_HEADS, D_HEAD, D_HEAD), lambda b, c: (b, 0, 0, 0))
    return pl.pallas_call(
        functools.partial(_prompt_kernel, t_len=t_len, n_chunks=n_chunks),
        grid=(n_seq, n_chunks),
        in_specs=[
            pl.BlockSpec((t_len, D_MODEL), const),
            pl.BlockSpec((t_len, D_MODEL),
                         lambda b, c: (jnp.minimum(b * n_chunks + c + 1, n_tiles - 1), 0)),
            vec, vec,
            pl.BlockSpec((D_MODEL, D_PROJ), const),
            pl.BlockSpec((1, D_PROJ), const),
            pl.BlockSpec((1, N_HEADS, D_HEAD, D_HEAD), init4),
            pl.BlockSpec((1, N_HEADS, D_HEAD), init3),
            pl.BlockSpec((1, 1, N_HEADS), init3),
            pl.BlockSpec((1, N_HEADS, D_HEAD, D_HEAD), init4),
            pl.BlockSpec((1, LANES), const),
            grp, grp, grp,
            pl.BlockSpec((D_MODEL, D_MODEL), const),
            vec, vec, vec,
        ],
        out_specs=[
            pl.BlockSpec((t_len, D_MODEL), lambda b, c: (b * n_chunks + c, 0)),
            state4,
            pl.BlockSpec((1, N_HEADS, D_HEAD), lambda b, c: (b, 0, 0)),
            pl.BlockSpec((1, 1, N_HEADS), lambda b, c: (b, 0, 0)),
            state4,
        ],
        out_shape=[
            jax.ShapeDtypeStruct((n_tiles * t_len, D_MODEL), F32),
            jax.ShapeDtypeStruct((n_seq, N_HEADS, D_HEAD, D_HEAD), F32),
            jax.ShapeDtypeStruct((n_seq, N_HEADS, D_HEAD), F32),
            jax.ShapeDtypeStruct((n_seq, 1, N_HEADS), F32),
            jax.ShapeDtypeStruct((n_seq, N_HEADS, D_HEAD, D_HEAD), F32),
        ],
        scratch_shapes=[pltpu.VMEM((t_len, D_PROJ), F32), pltpu.VMEM((t_len, D_MODEL), F32)],
        compiler_params=pltpu.CompilerParams(
            dimension_semantics=("arbitrary", "arbitrary"), vmem_limit_bytes=VMEM_LIMIT),
        name="prompt_mixer",
    )(x, x, ln_e_g, ln_e_b, w_in, b_in, c0, n0, m0, s0, bf_row, ga, gb, lb, w_out, b_out, ln_g, ln_b)


def _out_proj_kernel(x_ref, mix_ref, ge_ref, be_ref, w_ref, bias_ref, g_ref, b_ref, o_ref):
    xn = _layer_norm(x_ref[...], ge_ref[...], be_ref[...])
    y = _dot(mix_ref[...], w_ref[...]) + bias_ref[...]
    o_ref[...] = _layer_norm(ALPHA * xn + y, g_ref[...], b_ref[...])


def _out_proj(x, mix, ln_e_g, ln_e_b, w, bias, ln_g, ln_b, *, tm):
    n = x.shape[0]
    const = lambda i: (0, 0)
    row = pl.BlockSpec((tm, D_MODEL), lambda i: (i, 0))
    vec = pl.BlockSpec((1, D_MODEL), const)
    return pl.pallas_call(
        _out_proj_kernel,
        grid=(n // tm,),
        in_specs=[row, row, vec, vec, pl.BlockSpec((D_MODEL, D_MODEL), const), vec, vec, vec],
        out_specs=row,
        out_shape=jax.ShapeDtypeStruct((n, D_MODEL), F32),
        compiler_params=pltpu.CompilerParams(
            dimension_semantics=("arbitrary",), vmem_limit_bytes=VMEM_LIMIT),
        name="out_proj",
    )(x, mix, ln_e_g, ln_e_b, w, bias, ln_g, ln_b)


def _ffn_kernel(x_ref, cs_ref, wu_ref, bu_ref, wc_ref, bc_ref, wd_ref, bd_ref, g_ref, b_ref,
                y_ref, nc_ref, full_ref, *, n_seq_blk, t_len):
    hist = SUBLANES - (CONV_W - 1)

    @pl.when(pl.program_id(1) == 0)
    def _():
        full_ref[:, hist:SUBLANES, :] = cs_ref[...]

    x = x_ref[...]
    up = _dot(x.astype(BF16), wu_ref[...]) + bu_ref[...]
    u = up[:, :D_FF].reshape(n_seq_blk, t_len, D_FF)
    gate = up[:, D_FF:].reshape(n_seq_blk, t_len, D_FF)
    full_ref[:, SUBLANES:SUBLANES + t_len, :] = u
    conv = bc_ref[...] + u * wc_ref[CONV_W - 1:CONV_W, :]
    for j in range(CONV_W - 1):
        conv = conv + full_ref[:, hist + j:hist + j + t_len, :] * wc_ref[j:j + 1, :]
    last = full_ref[:, hist + t_len:SUBLANES + t_len, :]
    nc_ref[...] = last
    full_ref[:, hist:SUBLANES, :] = last
    act = (conv * _sigmoid(conv) * gate).reshape(n_seq_blk * t_len, D_FF)
    ffn = _dot(act.astype(BF16), wd_ref[...]) + bd_ref[...]
    y_ref[...] = _layer_norm(ALPHA * x + ffn, g_ref[...], b_ref[...])


def _ffn(x, conv_state, w_up, b_up, w_conv, b_conv, w_down, b_down, ln_g, ln_b,
         *, n_seq, seq_len, n_seq_blk, t_len, shared_init):
    n_t = seq_len // t_len
    rows = n_seq_blk * t_len
    const = lambda s, t: (0, 0)
    cs_map = (lambda s, t: (0, 0, 0)) if shared_init else (lambda s, t: (s, 0, 0))
    row = pl.BlockSpec((rows, D_MODEL), lambda s, t: (s * n_t + t, 0))
    vec = pl.BlockSpec((1, D_MODEL), const)
    return pl.pallas_call(
        functools.partial(_ffn_kernel, n_seq_blk=n_seq_blk, t_len=t_len),
        grid=(n_seq // n_seq_blk, n_t),
        in_specs=[
            row,
            pl.BlockSpec((n_seq_blk, CONV_W - 1, D_FF), cs_map),
            pl.BlockSpec((D_MODEL, 2 * D_FF), const),
            pl.BlockSpec((1, 2 * D_FF), const),
            pl.BlockSpec((CONV_W, D_FF), const),
            pl.BlockSpec((1, D_FF), const),
            pl.BlockSpec((D_FF, D_MODEL), const),
            vec, vec, vec,
        ],
        out_specs=[row, pl.BlockSpec((n_seq_blk, CONV_W - 1, D_FF), lambda s, t: (s, 0, 0))],
        out_shape=[
            jax.ShapeDtypeStruct((n_seq * seq_len, D_MODEL), F32),
            jax.ShapeDtypeStruct((n_seq, CONV_W - 1, D_FF), F32),
        ],
        scratch_shapes=[pltpu.VMEM((n_seq_blk, SUBLANES + t_len, D_FF), F32)],
        compiler_params=pltpu.CompilerParams(
            dimension_semantics=("arbitrary", "arbitrary"), vmem_limit_bytes=VMEM_LIMIT),
        name=f"ffn_t{t_len}",
    )(x, conv_state, w_up, b_up, w_conv, b_conv, w_down, b_down, ln_g, ln_b)


def kernel(x_prompt, x_sample, state_mlstm_C, state_mlstm_n, state_mlstm_m, state_hgrn_S, state_ffn_conv, meta_tokens, ln_emb_g, ln_emb_b, w_in, b_in, b_fgate_a, g_norm_a, g_norm_b, hgrn_lb_logits, w_out, b_out, ln1_g, ln1_b, w_up, b_up, w_conv, b_conv, w_down, b_down, ln2_g, ln2_b):
    assert w_in.shape[0] == DEPTH == 1
    n_prompt, seq, _ = x_prompt.shape
    n_sample, dec_seq, _ = x_sample.shape
    row = lambda v: v.reshape(1, -1).astype(F32)

    gate0 = 4 * D_GROUP
    gate1 = gate0 + 2 * N_HEADS
    pad = D_PROJ - w_in.shape[2]
    w_in16 = w_in[0].astype(BF16)
    w_in_p = jnp.concatenate(
        [w_in16[:, :gate0], w_in16[:, gate1:], w_in16[:, gate0:gate1],
         jnp.zeros((D_MODEL, pad), BF16)], axis=1)
    b_in_p = jnp.concatenate(
        [b_in[0][:gate0], b_in[0][gate1:], b_in[0][gate0:gate1], jnp.zeros((pad,), b_in.dtype)]
    ).reshape(1, D_PROJ).astype(F32)
    bf_row = jnp.zeros((1, LANES), F32).at[0, N_HEADS:2 * N_HEADS].set(b_fgate_a[0].astype(F32))
    lb = jnp.cumsum(jax.nn.softmax(hgrn_lb_logits.astype(F32), axis=0), axis=0)[0].reshape(1, D_GROUP)
    ga, gb = row(g_norm_a[0]), row(g_norm_b[0])
    ln_e = (row(ln_emb_g), row(ln_emb_b))
    out_p = (w_out[0].astype(BF16), row(b_out[0]), row(ln1_g[0]), row(ln1_b[0]))
    ffn_p = (w_up[0].astype(BF16), row(b_up[0]), w_conv[0].astype(F32), row(b_conv[0]),
             w_down[0].astype(BF16), row(b_down[0]), row(ln2_g[0]), row(ln2_b[0]))

    def layer(x_rows, mixer_state, conv_state, *, n_seq, seq_len, t_mix, mix_seq_blk, tm,
              ffn_seq_blk, ffn_t, shared_init):
        proj = _in_proj(x_rows, *ln_e, w_in_p, b_in_p, tm=tm)
        mix, c_new, n_new, m_new, s_new = _mixer(
            proj, *mixer_state, bf_row, ga, gb, lb, n_seq=n_seq, n_chunks=seq_len // t_mix,
            t_len=t_mix, n_seq_blk=mix_seq_blk, shared_init=shared_init)
        x1 = _out_proj(x_rows, mix, *ln_e, *out_p, tm=tm)
        y, conv_new = _ffn(x1, conv_state, *ffn_p, n_seq=n_seq, seq_len=seq_len,
                           n_seq_blk=ffn_seq_blk, t_len=ffn_t, shared_init=shared_init)
        return y, c_new, n_new, m_new, s_new, conv_new

    zero_state = (jnp.zeros((1, N_HEADS, D_HEAD, D_HEAD), F32), jnp.zeros((1, N_HEADS, D_HEAD), F32),
                  jnp.zeros((1, 1, N_HEADS), F32), jnp.zeros((1, N_HEADS, D_HEAD, D_HEAD), F32))
    _, c_m, n_m, m_m, s_m, conv_m = layer(
        meta_tokens.astype(F32), zero_state, jnp.zeros((1, CONV_W - 1, D_FF), F32),
        n_seq=1, seq_len=N_META, t_mix=N_META, mix_seq_blk=1, tm=N_META, ffn_seq_blk=1,
        ffn_t=N_META, shared_init=False)

    x1_p, c_p, n_p, m_p, s_p = _prompt_mixer(
        x_prompt.reshape(n_prompt * seq, D_MODEL), *ln_e, w_in_p, b_in_p, c_m, n_m, m_m, s_m,
        bf_row, ga, gb, lb, *out_p, n_seq=n_prompt, n_chunks=seq // PROMPT_CHUNK, t_len=PROMPT_CHUNK)
    y_p, conv_p = _ffn(x1_p, conv_m, *ffn_p, n_seq=n_prompt, seq_len=seq, n_seq_blk=1, t_len=256,
                       shared_init=True)

    sample_state = (state_mlstm_C[0].astype(F32), state_mlstm_n[0].astype(F32),
                    state_mlstm_m[0].astype(F32).reshape(n_sample, 1, N_HEADS),
                    state_hgrn_S[0].astype(F32))
    y_s, c_s, n_s, m_s, s_s, conv_s = layer(
        x_sample.reshape(n_sample * dec_seq, D_MODEL), sample_state, state_ffn_conv[0].astype(F32),
        n_seq=n_sample, seq_len=dec_seq, t_mix=dec_seq, mix_seq_blk=4, tm=256, ffn_seq_blk=32,
        ffn_t=dec_seq, shared_init=False)

    lead = lambda v: v[None]
    return (y_p.reshape(n_prompt, seq, D_MODEL), y_s.reshape(n_sample, dec_seq, D_MODEL),
            lead(c_p), lead(n_p), lead(m_p.reshape(n_prompt, N_HEADS)), lead(s_p), lead(conv_p),
            lead(c_s), lead(n_s), lead(m_s.reshape(n_sample, N_HEADS)), lead(s_s), lead(conv_s))
```

```python
import functools

import jax
import jax.numpy as jnp
from jax import lax
from jax.experimental import pallas as pl
from jax.experimental.pallas import tpu as pltpu

D_MODEL = 1024
N_META = 16
N_HEADS = 4
D_HEAD = 128
D_GROUP = N_HEADS * D_HEAD
D_FF = 2816
CONV_W = 3
DEPTH = 1
ALPHA = (2.0 * DEPTH) ** 0.25
LN_EPS = 1e-5
RMS_EPS = 1e-6

LANES = 128
SUBLANES = 8
GATE_COL = 8 * D_GROUP
D_PROJ = GATE_COL + LANES
IN_PROJ_STAGE_COLS = 256
PROMPT_CHUNK = 128
VMEM_LIMIT = 56 * 1024 * 1024

F32 = jnp.float32
BF16 = jnp.bfloat16
HIGHEST = lax.Precision.HIGHEST
NT_DIMS = (((1,), (1,)), ((), ()))
TN_DIMS = (((0,), (0,)), ((), ()))


def _layer_norm(x, g, b):
    mu = jnp.mean(x, axis=-1, keepdims=True)
    xc = x - mu
    var = jnp.mean(xc * xc, axis=-1, keepdims=True)
    return xc * lax.rsqrt(var + LN_EPS) * g + b


def _sigmoid(x):
    return 1.0 / (1.0 + jnp.exp(-x))


def _dot(a, b, precision=None):
    return jnp.dot(a, b, precision=precision, preferred_element_type=F32)


def _dot_nt(a, b, precision=None):
    return lax.dot_general(a, b, NT_DIMS, precision=precision, preferred_element_type=F32)


def _dot_tn(a, b):
    return lax.dot_general(a, b, TN_DIMS, preferred_element_type=F32)


def _in_proj_kernel(x_ref, g_ref, b_ref, w_ref, bias_ref, o_ref):
    xn = _layer_norm(x_ref[...], g_ref[...], b_ref[...])
    o_ref[...] = _dot(xn.astype(BF16), w_ref[...]) + bias_ref[...]


def _in_proj(x, ln_g, ln_b, w, bias, *, tm):
    n = x.shape[0]
    const = lambda i: (0, 0)
    return pl.pallas_call(
        _in_proj_kernel,
        grid=(n // tm,),
        in_specs=[
            pl.BlockSpec((tm, D_MODEL), lambda i: (i, 0)),
            pl.BlockSpec((1, D_MODEL), const),
            pl.BlockSpec((1, D_MODEL), const),
            pl.BlockSpec((D_MODEL, D_PROJ), const),
            pl.BlockSpec((1, D_PROJ), const),
        ],
        out_specs=pl.BlockSpec((tm, D_PROJ), lambda i: (i, 0)),
        out_shape=jax.ShapeDtypeStruct((n, D_PROJ), F32),
        compiler_params=pltpu.CompilerParams(
            dimension_semantics=("arbitrary",), vmem_limit_bytes=VMEM_LIMIT),
        name="in_proj",
    )(x, ln_g, ln_b, w, bias)


def _block_rows(x, level, t_len, row_in_block):
    size = 2 << level
    if size > SUBLANES:
        pieces = [jnp.broadcast_to(x[j * size + row_in_block:j * size + row_in_block + 1, :],
                                   (size, x.shape[1])) for j in range(t_len // size)]
        return pieces[0] if len(pieces) == 1 else jnp.concatenate(pieces, axis=0)
    x3 = x.reshape(t_len // SUBLANES, SUBLANES, x.shape[1])
    sub = lax.broadcasted_iota(jnp.int32, x3.shape, 1)
    out = None
    for j in range(SUBLANES // size):
        row = jnp.broadcast_to(x3[:, j * size + row_in_block:j * size + row_in_block + 1, :], x3.shape)
        out = row if out is None else jnp.where(sub >= j * size, row, out)
    return out.reshape(x.shape)


def _interleave_halves(lower, upper, level, t_len):
    half = 1 << level
    if half >= SUBLANES:
        pieces = []
        for j in range(t_len // (2 * half)):
            pieces.append(lower[2 * half * j:2 * half * j + half])
            pieces.append(upper[2 * half * j + half:2 * half * (j + 1)])
        return jnp.concatenate(pieces, axis=0)
    rows = lax.broadcasted_iota(jnp.int32, lower.shape, 0)
    return jnp.where((rows & half) != 0, upper, lower)


def _run_interleaved(gens, stages_per_round):
    results = [None] * len(gens)
    live = [True] * len(gens)
    while any(live):
        for g, steps in enumerate(stages_per_round):
            for _ in range(steps):
                if live[g]:
                    try:
                        next(gens[g])
                    except StopIteration as stop:
                        results[g], live[g] = stop.value, False
    return results


def _cumsum_rows(tril16, x):
    hi = x.astype(BF16)
    rest = x - hi.astype(F32)
    mid = rest.astype(BF16)
    lo = (rest - mid.astype(F32)).astype(BF16)
    return _dot(tril16, hi) + _dot(tril16, mid) + _dot(tril16, lo)


def _mlstm_units(*, q, k, v, og, z, zt, head, c_old, n_old, m0, g_norm, causal, t_len):
    idx = range(len(q))
    q16 = [q[i].astype(BF16) for i in idx]
    k16 = [k[i].astype(BF16) for i in idx]
    qk = [_dot_nt(q16[i], k16[i]) for i in idx]
    qc = [_dot(q16[i], c_old[i].astype(BF16)) for i in idx]
    bs_row = [zt[i][N_HEADS + head[i]:N_HEADS + head[i] + 1, :] - zt[i][head[i]:head[i] + 1, :]
              for i in idx]
    yield
    b_col = [z[i][:, N_HEADS + head[i]:N_HEADS + head[i] + 1] for i in idx]
    i_col = [z[i][:, head[i]:head[i] + 1] for i in idx]
    d = [jnp.where(causal, b_col[i] - bs_row[i], -jnp.inf) for i in idx]
    m_t = [jnp.maximum(b_col[i] + m0[i], jnp.max(d[i], axis=1, keepdims=True)) for i in idx]
    dec = [jnp.exp(b_col[i] + m0[i] - m_t[i]) for i in idx]
    sw = [jnp.exp(d[i] - m_t[i]) * qk[i] for i in idx]
    yield
    swv = [_dot(sw[i].astype(BF16), v[i].astype(BF16)) for i in idx]
    last = slice(t_len - 1, t_len)
    w_last = [jnp.exp(b_col[i][last] - b_col[i] + i_col[i] - m_t[i][last]) for i in idx]
    kv = [_dot_tn(k16[i], (w_last[i] * v[i]).astype(BF16)) for i in idx]
    yield
    c_new =[dec[i][last] * c_old[i] + kv[i] for i in idx]
    n_new = [dec[i][last] * n_old[i] + jnp.sum(w_last[i] * k[i], axis=0, keepdims=True) for i in idx]
    m_new = [m_t[i][last] for i in idx]
    den = [dec[i] * jnp.sum(q[i] * n_old[i], axis=1, keepdims=True)
           + jnp.sum(sw[i], axis=1, keepdims=True) for i in idx]
    hid = [(dec[i] * qc[i] + swv[i]) / jnp.maximum(jnp.abs(den[i]), jnp.exp(-m_t[i])) for i in idx]
    yield
    rms = [lax.rsqrt(jnp.mean(hid[i] * hid[i], axis=1, keepdims=True) + RMS_EPS) for i in idx]
    out = [_sigmoid(og[i]) * (hid[i] * rms[i] * g_norm[i]) for i in idx]
    return out, c_new, n_new, m_new


def _hgrn_units(*, qr, fr, iv, gr, lb, s_old_t, g_norm, tril, level_of, t_len):
    idx = range(len(qr))
    n_levels = t_len.bit_length() - 1
    f = [lb[i] + (1.0 - lb[i]) * _sigmoid(fr[i]) for i in idx]
    a = [_cumsum_rows(tril, jnp.log(f[i])) for i in idx]
    yield
    kb = [(1.0 - lb[i]) * _sigmoid(-fr[i]) for i in idx]
    qb = [qr[i] * _sigmoid(qr[i]) for i in idx]
    iv16 = [iv[i].astype(BF16) for i in idx]
    diag = [_dot_nt(qb[i].astype(BF16), kb[i].astype(BF16)) for i in idx]
    scores = [jnp.where(level_of == -2, diag[i], 0.0) for i in idx]
    for level in range(n_levels):
        yield
        x16 = []
        for i in idx:
            base = _interleave_halves(kb[i], qb[i], level, t_len)
            if level == 0:
                x = base * _interleave_halves(jnp.ones_like(f[i]), f[i], 0, t_len)
            else:
                ref = _block_rows(a[i], level, t_len, (1 << level) - 1)
                x = base * jnp.exp(-jnp.abs(a[i] - ref))
            x16.append(x.astype(BF16))
        part = [_dot_nt(x16[i], x16[i]) for i in idx]
        scores = [jnp.where(level_of == level, part[i], scores[i]) for i in idx]
    yield
    last = slice(t_len - 1, t_len)
    q_in = [(qb[i] * jnp.exp(a[i])).astype(BF16) for i in idx]
    k_out = [(kb[i] * jnp.exp(a[i][last] - a[i])).astype(BF16) for i in idx]
    inter = [_dot_nt(q_in[i], s_old_t[i].astype(BF16)) for i in idx]
    intra = [_dot(scores[i].astype(BF16), iv16[i]) for i in idx]
    kv = [_dot_tn(iv16[i], k_out[i]) for i in idx]
    yield
    s_new_t =[jnp.exp(a[i][last]) * s_old_t[i] + kv[i] for i in idx]
    o = [inter[i] + intra[i] for i in idx]
    rms = [lax.rsqrt(jnp.mean(o[i] * o[i], axis=1, keepdims=True) + RMS_EPS) for i in idx]
    out = [_sigmoid(gr[i]) * (o[i] * rms[i] * g_norm[i]) for i in idx]
    return out, s_new_t


def _load_state(c0_ref, n0_ref, m0_ref, s0_ref, c_ref, n_ref, m_ref, s_ref, n_seq_blk):
    c_ref[...] = c0_ref[...]
    n_ref[...] = n0_ref[...]
    m_ref[...] = m0_ref[...]
    for s in range(n_seq_blk):
        for h in range(N_HEADS):
            s_ref[s, h] = s0_ref[s, h].T


def _finish_state(s_ref, n_seq_blk):
    for s in range(n_seq_blk):
        for h in range(N_HEADS):
            s_ref[s, h] = s_ref[s, h].T


def _mixer_body(p_ref, bf_ref, ga_ref, gb_ref, lb_ref, c_ref, n_ref, m_ref, s_ref,
                *, t_len, n_seq_blk, side_stages=None):
    rows = lax.broadcasted_iota(jnp.int32, (t_len, t_len), 0)
    cols = lax.broadcasted_iota(jnp.int32, (t_len, t_len), 1)
    causal = cols <= rows
    tril = causal.astype(BF16)
    level_of = jnp.where(rows > cols, 31 - lax.clz(rows ^ cols), jnp.where(rows == cols, -2, -1))
    lane = lax.broadcasted_iota(jnp.int32, (t_len, LANES), 1)
    is_f = (lane >= N_HEADS) & (lane < 2 * N_HEADS)
    hd = lambda j, h: slice(j * D_GROUP + h * D_HEAD, j * D_GROUP + (h + 1) * D_HEAD)

    units = [(s, h) for s in range(n_seq_blk) for h in range(N_HEADS)]
    rs = lambda s: slice(s * t_len, (s + 1) * t_len)

    z_seq, zt_seq, n_seq, m_seq = [], [], [], []
    for s in range(n_seq_blk):
        gates = p_ref[rs(s), GATE_COL:GATE_COL + LANES]
        log_f = jnp.where(is_f, jax.nn.log_sigmoid(gates + bf_ref[...]), 0.0)
        cum_f = _cumsum_rows(tril, log_f)
        z_seq.append(jnp.where(is_f, cum_f, jnp.where(lane < N_HEADS, gates, 0.0)))
        zt_seq.append(z_seq[s].T)
        n_seq.append(n_ref[s])
        m_seq.append(m_ref[s])
    mlstm = _mlstm_units(
        q=[p_ref[rs(s), hd(0, h)] for s, h in units],
        k=[p_ref[rs(s), hd(1, h)] * (D_HEAD ** -0.5) for s, h in units],
        v=[p_ref[rs(s), hd(2, h)] for s, h in units],
        og=[p_ref[rs(s), hd(3, h)] for s, h in units],
        z=[z_seq[s] for s, h in units], zt=[zt_seq[s] for s, h in units], head=[h for s, h in units],
        c_old=[c_ref[s, h] for s, h in units], n_old=[n_seq[s][h:h + 1, :] for s, h in units],
        m0=[m_seq[s][:, h:h + 1] for s, h in units], g_norm=[ga_ref[:, hd(0, h)] for s, h in units],
        causal=causal, t_len=t_len)
    hgrn = _hgrn_units(
        qr=[p_ref[rs(s), hd(4, h)] for s, h in units], fr=[p_ref[rs(s), hd(5, h)] for s, h in units],
        iv=[p_ref[rs(s), hd(6, h)] for s, h in units], gr=[p_ref[rs(s), hd(7, h)] for s, h in units],
        lb=[lb_ref[:, hd(0, h)] for s, h in units], s_old_t=[s_ref[s, h] for s, h in units],
        g_norm=[gb_ref[:, hd(0, h)] for s, h in units], tril=tril, level_of=level_of, t_len=t_len)
    gens, per_round = [mlstm, hgrn], [1, 2]
    if side_stages is not None:
        gens, per_round = gens + [side_stages], per_round + [3]
    results = _run_interleaved(gens, per_round)
    (outs, c_new, n_new, m_new), (outs_b, s_new_t) = results[:2]
    side_result = results[2] if side_stages is not None else None

    for i, (s, h) in enumerate(units):
        c_ref[s, h] = c_new[i]
        s_ref[s, h] = s_new_t[i]
    head_lane = lax.broadcasted_iota(jnp.int32, (1, N_HEADS), 1)
    for s in range(n_seq_blk):
        n_ref[s] = jnp.concatenate(n_new[s * N_HEADS:(s + 1) * N_HEADS], axis=0)
        m_row = m_seq[s]
        for h in range(N_HEADS):
            m_row = jnp.where(head_lane == h, m_new[s * N_HEADS + h], m_row)
        m_ref[s] = m_row
    return outs, outs_b, side_result


def _mixer_kernel(p_ref, c0_ref, n0_ref, m0_ref, s0_ref, bf_ref, ga_ref, gb_ref, lb_ref,
                  mix_ref, c_ref, n_ref, m_ref, s_ref, *, t_len, n_chunks, n_seq_blk):
    chunk = pl.program_id(1)

    @pl.when(chunk == 0)
    def _():
        _load_state(c0_ref, n0_ref, m0_ref, s0_ref, c_ref, n_ref, m_ref, s_ref, n_seq_blk)

    outs_a, outs_b, _ = _mixer_body(p_ref, bf_ref, ga_ref, gb_ref, lb_ref, c_ref, n_ref, m_ref, s_ref,
                                    t_len=t_len, n_seq_blk=n_seq_blk)
    for i in range(n_seq_blk * N_HEADS):
        s, h = divmod(i, N_HEADS)
        rows = slice(s * t_len, (s + 1) * t_len)
        mix_ref[rows, h * D_HEAD:(h + 1) * D_HEAD] = outs_a[i].astype(mix_ref.dtype)
        mix_ref[rows, D_GROUP + h * D_HEAD:D_GROUP + (h + 1) * D_HEAD] = outs_b[i].astype(mix_ref.dtype)

    @pl.when(chunk == n_chunks - 1)
    def _():
        _finish_state(s_ref, n_seq_blk)


def _mixer(proj, c0, n0, m0, s0, bf_row, ga, gb, lb, *, n_seq, n_chunks, t_len, n_seq_blk,
           shared_init):
    assert not shared_init or n_seq_blk == 1
    assert n_chunks == 1 or n_seq_blk == 1
    nb = n_seq_blk
    init = (lambda b, c: (0, 0, 0, 0)) if shared_init else (lambda b, c: (b, 0, 0, 0))
    init3 = (lambda b, c: (0, 0, 0)) if shared_init else (lambda b, c: (b, 0, 0))
    const = lambda b, c: (0, 0)
    state4 = pl.BlockSpec((nb, N_HEADS, D_HEAD, D_HEAD), lambda b, c: (b, 0, 0, 0))
    return pl.pallas_call(
        functools.partial(_mixer_kernel, t_len=t_len, n_chunks=n_chunks, n_seq_blk=nb),
        grid=(n_seq // nb, n_chunks),
        in_specs=[
            pl.BlockSpec((nb * t_len, D_PROJ), lambda b, c: (b * n_chunks + c, 0)),
            pl.BlockSpec((nb, N_HEADS, D_HEAD, D_HEAD), init),
            pl.BlockSpec((nb, N_HEADS, D_HEAD), init3),
            pl.BlockSpec((nb, 1, N_HEADS), init3),
            pl.BlockSpec((nb, N_HEADS, D_HEAD, D_HEAD), init),
            pl.BlockSpec((1, LANES), const),
            pl.BlockSpec((1, D_GROUP), const),
            pl.BlockSpec((1, D_GROUP), const),
            pl.BlockSpec((1, D_GROUP), const),
        ],
        out_specs=[
            pl.BlockSpec((nb * t_len, D_MODEL), lambda b, c: (b * n_chunks + c, 0)),
            state4,
            pl.BlockSpec((nb, N_HEADS, D_HEAD), lambda b, c: (b, 0, 0)),
            pl.BlockSpec((nb, 1, N_HEADS), lambda b, c: (b, 0, 0)),
            state4,
        ],
        out_shape=[
            jax.ShapeDtypeStruct((n_seq * n_chunks * t_len, D_MODEL), BF16),
            jax.ShapeDtypeStruct((n_seq, N_HEADS, D_HEAD, D_HEAD), F32),
            jax.ShapeDtypeStruct((n_seq, N_HEADS, D_HEAD), F32),
            jax.ShapeDtypeStruct((n_seq, 1, N_HEADS), F32),
            jax.ShapeDtypeStruct((n_seq, N_HEADS, D_HEAD, D_HEAD), F32),
        ],
        compiler_params=pltpu.CompilerParams(
            dimension_semantics=("arbitrary", "arbitrary"), vmem_limit_bytes=VMEM_LIMIT),
        name=f"mixer_t{t_len}",
    )(proj, c0, n0, m0, s0, bf_row, ga, gb, lb)


def _in_proj_stages(x_ref, g_ref, b_ref, w_ref, bias_ref, proj_ref, xn_ref):
    xn = _layer_norm(x_ref[...], g_ref[...], b_ref[...])
    xn_ref[...] = xn
    x16 = xn.astype(BF16)
    for lo in range(0, D_PROJ, IN_PROJ_STAGE_COLS):
        hi = min(lo + IN_PROJ_STAGE_COLS, D_PROJ)
        yield
        proj_ref[:, lo:hi] = _dot(x16, w_ref[:, lo:hi]) + bias_ref[:, lo:hi]


def _prompt_kernel(x0_ref, xnext_ref, ge_ref, be_ref, win_ref, bin_ref, c0_ref, n0_ref, m0_ref, s0_ref,
                   bf_ref, ga_ref, gb_ref, lb_ref, wout_ref, bout_ref, g1_ref, b1_ref,
                   x1_ref, c_ref, n_ref, m_ref, s_ref, proj_scr, xn_scr, proj_alt, xn_alt,
                   *, t_len, n_chunks):
    chunk = pl.program_id(1)
    step = pl.program_id(0) * n_chunks + chunk

    @pl.when(step == 0)
    def _():
        first = _in_proj_stages(x0_ref, ge_ref, be_ref, win_ref, bin_ref, proj_scr, xn_scr)
        _run_interleaved([first], [1])

    @pl.when(chunk == 0)
    def _():
        _load_state(c0_ref, n0_ref, m0_ref, s0_ref, c_ref, n_ref, m_ref, s_ref, 1)

    def tile(proj_cur, xn_cur, proj_next, xn_next):
        next_proj = _in_proj_stages(xnext_ref, ge_ref, be_ref, win_ref, bin_ref, proj_next, xn_next)
        outs_a, outs_b, _ = _mixer_body(
            proj_cur, bf_ref, ga_ref, gb_ref, lb_ref, c_ref, n_ref, m_ref, s_ref,
            t_len=t_len, n_seq_blk=1, side_stages=next_proj)
        mix = jnp.concatenate([o.astype(BF16) for o in outs_a + outs_b], axis=1)
        y = _dot(mix, wout_ref[...]) + bout_ref[...]
        x1_ref[...] = _layer_norm(ALPHA * xn_cur[...] + y, g1_ref[...], b1_ref[...])

    @pl.when(step % 2 == 0)
    def _():
        tile(proj_scr, xn_scr, proj_alt, xn_alt)

    @pl.when(step % 2 == 1)
    def _():
        tile(proj_alt, xn_alt, proj_scr, xn_scr)

    @pl.when(chunk == n_chunks - 1)
    def _():
        _finish_state(s_ref, 1)


def _prompt_mixer(x, ln_e_g, ln_e_b, w_in, b_in, c0, n0, m0, s0, bf_row, ga, gb, lb,
                  w_out, b_out, ln_g, ln_b, *, n_seq, n_chunks, t_len):
    n_tiles = n_seq * n_chunks
    const = lambda b, c: (0, 0)
    init4 = lambda b, c: (0, 0, 0, 0)
    init3 = lambda b, c: (0, 0, 0)
    vec = pl.BlockSpec((1, D_MODEL), const)
    grp = pl.BlockSpec((1, D_GROUP), const)
    state4 = pl.BlockSpec((1, N_HEADS, D_HEAD, D_HEAD), lambda b, c: (b, 0, 0, 0))
    return pl.pallas_call(
        functools.partial(_prompt_kernel, t_len=t_len, n_chunks=n_chunks),
        grid=(n_seq, n_chunks),
        in_specs=[
            pl.BlockSpec((t_len, D_MODEL), const),
            pl.BlockSpec((t_len, D_MODEL),
                         lambda b, c: (jnp.minimum(b * n_chunks + c + 1, n_tiles - 1), 0)),
            vec, vec,
            pl.BlockSpec((D_MODEL, D_PROJ), const),
            pl.BlockSpec((1, D_PROJ), const),
            pl.BlockSpec((1, N_HEADS, D_HEAD, D_HEAD), init4),
            pl.BlockSpec((1, N_HEADS, D_HEAD), init3),
            pl.BlockSpec((1, 1, N_HEADS), init3),
            pl.BlockSpec((1, N_HEADS, D_HEAD, D_HEAD), init4),
            pl.BlockSpec((1, LANES), const),
            grp, grp, grp,
            pl.BlockSpec((D_MODEL, D_MODEL), const),
            vec, vec, vec,
        ],
        out_specs=[
            pl.BlockSpec((t_len, D_MODEL), lambda b, c: (b * n_chunks + c, 0)),
            state4,
            pl.BlockSpec((1, N_HEADS, D_HEAD), lambda b, c: (b, 0, 0)),
            pl.BlockSpec((1, 1, N_HEADS), lambda b, c: (b, 0, 0)),
            state4,
        ],
        out_shape=[
            jax.ShapeDtypeStruct((n_tiles * t_len, D_MODEL), F32),
            jax.ShapeDtypeStruct((n_seq, N_HEADS, D_HEAD, D_HEAD), F32),
            jax.ShapeDtypeStruct((n_seq, N_HEADS, D_HEAD), F32),
            jax.ShapeDtypeStruct((n_seq, 1, N_HEADS), F32),
            jax.ShapeDtypeStruct((n_seq, N_HEADS, D_HEAD, D_HEAD), F32),
        ],
        scratch_shapes=[pltpu.VMEM((t_len, D_PROJ), F32), pltpu.VMEM((t_len, D_MODEL), F32),
                        pltpu.VMEM((t_len, D_PROJ), F32), pltpu.VMEM((t_len, D_MODEL), F32)],
        compiler_params=pltpu.CompilerParams(
            dimension_semantics=("arbitrary", "arbitrary"), vmem_limit_bytes=VMEM_LIMIT),
        name="prompt_mixer",
    )(x, x, ln_e_g, ln_e_b, w_in, b_in, c0, n0, m0, s0, bf_row, ga, gb, lb, w_out, b_out, ln_g, ln_b)


def _out_proj_kernel(x_ref, mix_ref, ge_ref, be_ref, w_ref, bias_ref, g_ref, b_ref, o_ref):
    xn = _layer_norm(x_ref[...], ge_ref[...], be_ref[...])
    y = _dot(mix_ref[...], w_ref[...]) + bias_ref[...]
    o_ref[...] = _layer_norm(ALPHA * xn + y, g_ref[...], b_ref[...])


def _out_proj(x, mix, ln_e_g, ln_e_b, w, bias, ln_g, ln_b, *, tm):
    n = x.shape[0]
    const = lambda i: (0, 0)
    row = pl.BlockSpec((tm, D_MODEL), lambda i: (i, 0))
    vec = pl.BlockSpec((1, D_MODEL), const)
    return pl.pallas_call(
        _out_proj_kernel,
        grid=(n // tm,),
        in_specs=[row, row, vec, vec, pl.BlockSpec((D_MODEL, D_MODEL), const), vec, vec, vec],
        out_specs=row,
        out_shape=jax.ShapeDtypeStruct((n, D_MODEL), F32),
        compiler_params=pltpu.CompilerParams(
            dimension_semantics=("arbitrary",), vmem_limit_bytes=VMEM_LIMIT),
        name="out_proj",
    )(x, mix, ln_e_g, ln_e_b, w, bias, ln_g, ln_b)


def _ffn_kernel(x_ref, cs_ref, wu_ref, bu_ref, wc_ref, bc_ref, wd_ref, bd_ref, g_ref, b_ref,
                y_ref, nc_ref, full_ref, *, n_seq_blk, t_len):
    hist = SUBLANES - (CONV_W - 1)

    @pl.when(pl.program_id(1) == 0)
    def _():
        full_ref[:, hist:SUBLANES, :] = cs_ref[...]

    x = x_ref[...]
    up = _dot(x.astype(BF16), wu_ref[...]) + bu_ref[...]
    u = up[:, :D_FF].reshape(n_seq_blk, t_len, D_FF)
    gate = up[:, D_FF:].reshape(n_seq_blk, t_len, D_FF)
    full_ref[:, SUBLANES:SUBLANES + t_len, :] = u
    conv = bc_ref[...] + u * wc_ref[CONV_W - 1:CONV_W, :]
    for j in range(CONV_W - 1):
        conv = conv + full_ref[:, hist + j:hist + j + t_len, :] * wc_ref[j:j + 1, :]
    last = full_ref[:, hist + t_len:SUBLANES + t_len, :]
    nc_ref[...] = last
    full_ref[:, hist:SUBLANES, :] = last
    act = (conv * _sigmoid(conv) * gate).reshape(n_seq_blk * t_len, D_FF)
    ffn = _dot(act.astype(BF16), wd_ref[...]) + bd_ref[...]
    y_ref[...] = _layer_norm(ALPHA * x + ffn, g_ref[...], b_ref[...])


def _ffn(x, conv_state, w_up, b_up, w_conv, b_conv, w_down, b_down, ln_g, ln_b,
         *, n_seq, seq_len, n_seq_blk, t_len, shared_init):
    n_t = seq_len // t_len
    rows = n_seq_blk * t_len
    const = lambda s, t: (0, 0)
    cs_map = (lambda s, t: (0, 0, 0)) if shared_init else (lambda s, t: (s, 0, 0))
    row = pl.BlockSpec((rows, D_MODEL), lambda s, t: (s * n_t + t, 0))
    vec = pl.BlockSpec((1, D_MODEL), const)
    return pl.pallas_call(
        functools.partial(_ffn_kernel, n_seq_blk=n_seq_blk, t_len=t_len),
        grid=(n_seq // n_seq_blk, n_t),
        in_specs=[
            row,
            pl.BlockSpec((n_seq_blk, CONV_W - 1, D_FF), cs_map),
            pl.BlockSpec((D_MODEL, 2 * D_FF), const),
            pl.BlockSpec((1, 2 * D_FF), const),
            pl.BlockSpec((CONV_W, D_FF), const),
            pl.BlockSpec((1, D_FF), const),
            pl.BlockSpec((D_FF, D_MODEL), const),
            vec, vec, vec,
        ],
        out_specs=[row, pl.BlockSpec((n_seq_blk, CONV_W - 1, D_FF), lambda s, t: (s, 0, 0))],
        out_shape=[
            jax.ShapeDtypeStruct((n_seq * seq_len, D_MODEL), F32),
            jax.ShapeDtypeStruct((n_seq, CONV_W - 1, D_FF), F32),
        ],
        scratch_shapes=[pltpu.VMEM((n_seq_blk, SUBLANES + t_len, D_FF), F32)],
        compiler_params=pltpu.CompilerParams(
            dimension_semantics=("arbitrary", "arbitrary"), vmem_limit_bytes=VMEM_LIMIT),
        name=f"ffn_t{t_len}",
    )(x, conv_state, w_up, b_up, w_conv, b_conv, w_down, b_down, ln_g, ln_b)


def kernel(x_prompt, x_sample, state_mlstm_C, state_mlstm_n, state_mlstm_m, state_hgrn_S, state_ffn_conv, meta_tokens, ln_emb_g, ln_emb_b, w_in, b_in, b_fgate_a, g_norm_a, g_norm_b, hgrn_lb_logits, w_out, b_out, ln1_g, ln1_b, w_up, b_up, w_conv, b_conv, w_down, b_down, ln2_g, ln2_b):
    assert w_in.shape[0] == DEPTH == 1
    n_prompt, seq, _ = x_prompt.shape
    n_sample, dec_seq, _ = x_sample.shape
    row = lambda v: v.reshape(1, -1).astype(F32)

    gate0 = 4 * D_GROUP
    gate1 = gate0 + 2 * N_HEADS
    pad = D_PROJ - w_in.shape[2]
    w_in16 = w_in[0].astype(BF16)
    w_in_p = jnp.concatenate(
        [w_in16[:, :gate0], w_in16[:, gate1:], w_in16[:, gate0:gate1],
         jnp.zeros((D_MODEL, pad), BF16)], axis=1)
    b_in_p = jnp.concatenate(
        [b_in[0][:gate0], b_in[0][gate1:], b_in[0][gate0:gate1], jnp.zeros((pad,), b_in.dtype)]
    ).reshape(1, D_PROJ).astype(F32)
    bf_row = jnp.zeros((1, LANES), F32).at[0, N_HEADS:2 * N_HEADS].set(b_fgate_a[0].astype(F32))
    lb = jnp.cumsum(jax.nn.softmax(hgrn_lb_logits.astype(F32), axis=0), axis=0)[0].reshape(1, D_GROUP)
    ga, gb = row(g_norm_a[0]), row(g_norm_b[0])
    ln_e = (row(ln_emb_g), row(ln_emb_b))
    out_p = (w_out[0].astype(BF16), row(b_out[0]), row(ln1_g[0]), row(ln1_b[0]))
    ffn_p = (w_up[0].astype(BF16), row(b_up[0]), w_conv[0].astype(F32), row(b_conv[0]),
             w_down[0].astype(BF16), row(b_down[0]), row(ln2_g[0]), row(ln2_b[0]))

    def layer(x_rows, mixer_state, conv_state, *, n_seq, seq_len, t_mix, mix_seq_blk, tm,
              ffn_seq_blk, ffn_t, shared_init):
        proj = _in_proj(x_rows, *ln_e, w_in_p, b_in_p, tm=tm)
        mix, c_new, n_new, m_new, s_new = _mixer(
            proj, *mixer_state, bf_row, ga, gb, lb, n_seq=n_seq, n_chunks=seq_len // t_mix,
            t_len=t_mix, n_seq_blk=mix_seq_blk, shared_init=shared_init)
        x1 = _out_proj(x_rows, mix, *ln_e, *out_p, tm=tm)
        y, conv_new = _ffn(x1, conv_state, *ffn_p, n_seq=n_seq, seq_len=seq_len,
                           n_seq_blk=ffn_seq_blk, t_len=ffn_t, shared_init=shared_init)
        return y, c_new, n_new, m_new, s_new, conv_new

    zero_state = (jnp.zeros((1, N_HEADS, D_HEAD, D_HEAD), F32), jnp.zeros((1, N_HEADS, D_HEAD), F32),
                  jnp.zeros((1, 1, N_HEADS), F32), jnp.zeros((1, N_HEADS, D_HEAD, D_HEAD), F32))
    _, c_m, n_m, m_m, s_m, conv_m = layer(
        meta_tokens.astype(F32), zero_state, jnp.zeros((1, CONV_W - 1, D_FF), F32),
        n_seq=1, seq_len=N_META, t_mix=N_META, mix_seq_blk=1, tm=N_META, ffn_seq_blk=1,
        ffn_t=N_META, shared_init=False)

    x1_p, c_p, n_p, m_p, s_p = _prompt_mixer(
        x_prompt.reshape(n_prompt * seq, D_MODEL), *ln_e, w_in_p, b_in_p, c_m, n_m, m_m, s_m,
        bf_row, ga, gb, lb, *out_p, n_seq=n_prompt, n_chunks=seq // PROMPT_CHUNK, t_len=PROMPT_CHUNK)
    y_p, conv_p = _ffn(x1_p, conv_m, *ffn_p, n_seq=n_prompt, seq_len=seq, n_seq_blk=1, t_len=256,
                       shared_init=True)

    sample_state = (state_mlstm_C[0].astype(F32), state_mlstm_n[0].astype(F32),
                    state_mlstm_m[0].astype(F32).reshape(n_sample, 1, N_HEADS),
                    state_hgrn_S[0].astype(F32))
    y_s, c_s, n_s, m_s, s_s, conv_s = layer(
        x_sample.reshape(n_sample * dec_seq, D_MODEL), sample_state, state_ffn_conv[0].astype(F32),
        n_seq=n_sample, seq_len=dec_seq, t_mix=dec_seq, mix_seq_blk=4, tm=256, ffn_seq_blk=32,
        ffn_t=dec_seq, shared_init=False)

    lead = lambda v: v[None]
    return (y_p.reshape(n_prompt, seq, D_MODEL), y_s.reshape(n_sample, dec_seq, D_MODEL),
            lead(c_p), lead(n_p), lead(m_p.reshape(n_prompt, N_HEADS)), lead(s_p), lead(conv_p),
            lead(c_s), lead(n_s), lead(m_s.reshape(n_sample, N_HEADS)), lead(s_s), lead(conv_s))
```

```python
import functools

import jax
import jax.numpy as jnp
from jax import lax
from jax.experimental import pallas as pl
from jax.experimental.pallas import tpu as pltpu

D_MODEL = 1024
N_META = 16
N_HEADS = 4
D_HEAD = 128
D_GROUP = N_HEADS * D_HEAD
D_FF = 2816
CONV_W = 3
DEPTH = 1
ALPHA = (2.0 * DEPTH) ** 0.25
LN_EPS = 1e-5
RMS_EPS = 1e-6

LANES = 128
SUBLANES = 8
GATE_COL = 8 * D_GROUP
D_PROJ = GATE_COL + LANES
IN_PROJ_STAGE_COLS = 256
SIDE_STAGES_PER_ROUND = 3
PROMPT_CHUNK = 128
VMEM_LIMIT = 56 * 1024 * 1024

F32 = jnp.float32
BF16 = jnp.bfloat16
HIGHEST = lax.Precision.HIGHEST
NT_DIMS = (((1,), (1,)), ((), ()))
TN_DIMS = (((0,), (0,)), ((), ()))


def _layer_norm(x, g, b):
    mu = jnp.mean(x, axis=-1, keepdims=True)
    xc = x - mu
    var = jnp.mean(xc * xc, axis=-1, keepdims=True)
    return xc * lax.rsqrt(var + LN_EPS) * g + b


def _sigmoid(x):
    return 1.0 / (1.0 + jnp.exp(-x))


def _resident(block_shape, index_map):
    return pl.BlockSpec(block_shape, index_map, pipeline_mode=pl.Buffered(1))


def _dot(a, b, precision=None):
    return jnp.dot(a, b, precision=precision, preferred_element_type=F32)


def _dot_nt(a, b, precision=None):
    return lax.dot_general(a, b, NT_DIMS, precision=precision, preferred_element_type=F32)


def _dot_tn(a, b):
    return lax.dot_general(a, b, TN_DIMS, preferred_element_type=F32)


def _in_proj_kernel(x_ref, g_ref, b_ref, w_ref, bias_ref, o_ref):
    xn = _layer_norm(x_ref[...], g_ref[...], b_ref[...])
    o_ref[...] = _dot(xn.astype(BF16), w_ref[...]) + bias_ref[...]


def _in_proj(x, ln_g, ln_b, w, bias, *, tm):
    n = x.shape[0]
    const = lambda i: (0, 0)
    return pl.pallas_call(
        _in_proj_kernel,
        grid=(n // tm,),
        in_specs=[
            pl.BlockSpec((tm, D_MODEL), lambda i: (i, 0)),
            pl.BlockSpec((1, D_MODEL), const),
            pl.BlockSpec((1, D_MODEL), const),
            pl.BlockSpec((D_MODEL, D_PROJ), const),
            pl.BlockSpec((1, D_PROJ), const),
        ],
        out_specs=pl.BlockSpec((tm, D_PROJ), lambda i: (i, 0)),
        out_shape=jax.ShapeDtypeStruct((n, D_PROJ), F32),
        compiler_params=pltpu.CompilerParams(
            dimension_semantics=("arbitrary",), vmem_limit_bytes=VMEM_LIMIT),
        name="in_proj",
    )(x, ln_g, ln_b, w, bias)


def _block_rows(x, level, t_len, row_in_block):
    size = 2 << level
    if size > SUBLANES:
        pieces = [jnp.broadcast_to(x[j * size + row_in_block:j * size + row_in_block + 1, :],
                                   (size, x.shape[1])) for j in range(t_len // size)]
        return pieces[0] if len(pieces) == 1 else jnp.concatenate(pieces, axis=0)
    x3 = x.reshape(t_len // SUBLANES, SUBLANES, x.shape[1])
    sub = lax.broadcasted_iota(jnp.int32, x3.shape, 1)
    out = None
    for j in range(SUBLANES // size):
        row = jnp.broadcast_to(x3[:, j * size + row_in_block:j * size + row_in_block + 1, :], x3.shape)
        out = row if out is None else jnp.where(sub >= j * size, row, out)
    return out.reshape(x.shape)


def _interleave_halves(lower, upper, level, t_len):
    half = 1 << level
    if half >= SUBLANES:
        pieces = []
        for j in range(t_len // (2 * half)):
            pieces.append(lower[2 * half * j:2 * half * j + half])
            pieces.append(upper[2 * half * j + half:2 * half * (j + 1)])
        return jnp.concatenate(pieces, axis=0)
    rows = lax.broadcasted_iota(jnp.int32, lower.shape, 0)
    return jnp.where((rows & half) != 0, upper, lower)


def _run_interleaved(gens, stages_per_round):
    results = [None] * len(gens)
    live = [True] * len(gens)
    while any(live):
        for g, steps in enumerate(stages_per_round):
            for _ in range(steps):
                if live[g]:
                    try:
                        next(gens[g])
                    except StopIteration as stop:
                        results[g], live[g] = stop.value, False
    return results


def _cumsum_rows(tril16, x):
    hi = x.astype(BF16)
    rest = x - hi.astype(F32)
    mid = rest.astype(BF16)
    lo = (rest - mid.astype(F32)).astype(BF16)
    return _dot(tril16, hi) + _dot(tril16, mid) + _dot(tril16, lo)


def _mlstm_units(*, q, k, v, og, z, zt, head, c_old, n_old, m0, g_norm, causal, t_len):
    idx = range(len(q))
    q16 = [q[i].astype(BF16) for i in idx]
    k16 = [k[i].astype(BF16) for i in idx]
    qk = [_dot_nt(q16[i], k16[i]) for i in idx]
    qc = [_dot(q16[i], c_old[i].astype(BF16)) for i in idx]
    bs_row = [zt[i][N_HEADS + head[i]:N_HEADS + head[i] + 1, :] - zt[i][head[i]:head[i] + 1, :]
              for i in idx]
    yield
    b_col = [z[i][:, N_HEADS + head[i]:N_HEADS + head[i] + 1] for i in idx]
    i_col = [z[i][:, head[i]:head[i] + 1] for i in idx]
    d = [jnp.where(causal, b_col[i] - bs_row[i], -jnp.inf) for i in idx]
    m_t = [jnp.maximum(b_col[i] + m0[i], jnp.max(d[i], axis=1, keepdims=True)) for i in idx]
    dec = [jnp.exp(b_col[i] + m0[i] - m_t[i]) for i in idx]
    sw = [jnp.exp(d[i] - m_t[i]) * qk[i] for i in idx]
    yield
    swv = [_dot(sw[i].astype(BF16), v[i].astype(BF16)) for i in idx]
    last = slice(t_len - 1, t_len)
    w_last = [jnp.exp(b_col[i][last] - b_col[i] + i_col[i] - m_t[i][last]) for i in idx]
    kv = [_dot_tn(k16[i], (w_last[i] * v[i]).astype(BF16)) for i in idx]
    yield
    c_new =[dec[i][last] * c_old[i] + kv[i] for i in idx]
    n_new = [dec[i][last] * n_old[i] + jnp.sum(w_last[i] * k[i], axis=0, keepdims=True) for i in idx]
    m_new = [m_t[i][last] for i in idx]
    den = [dec[i] * jnp.sum(q[i] * n_old[i], axis=1, keepdims=True)
           + jnp.sum(sw[i], axis=1, keepdims=True) for i in idx]
    hid = [(dec[i] * qc[i] + swv[i]) / jnp.maximum(jnp.abs(den[i]), jnp.exp(-m_t[i])) for i in idx]
    yield
    rms = [lax.rsqrt(jnp.mean(hid[i] * hid[i], axis=1, keepdims=True) + RMS_EPS) for i in idx]
    out = [_sigmoid(og[i]) * (hid[i] * rms[i] * g_norm[i]) for i in idx]
    return out, c_new, n_new, m_new


def _hgrn_units(*, qr, fr, iv, gr, lb, s_old_t, g_norm, tril, level_of, t_len):
    idx = range(len(qr))
    n_levels = t_len.bit_length() - 1
    f = [lb[i] + (1.0 - lb[i]) * _sigmoid(fr[i]) for i in idx]
    a = [_cumsum_rows(tril, jnp.log(f[i])) for i in idx]
    yield
    kb = [(1.0 - lb[i]) * _sigmoid(-fr[i]) for i in idx]
    qb = [qr[i] * _sigmoid(qr[i]) for i in idx]
    iv16 = [iv[i].astype(BF16) for i in idx]
    diag = [_dot_nt(qb[i].astype(BF16), kb[i].astype(BF16)) for i in idx]
    scores = [jnp.where(level_of == -2, diag[i], 0.0) for i in idx]
    for level in range(n_levels):
        yield
        x16 = []
        for i in idx:
            base = _interleave_halves(kb[i], qb[i], level, t_len)
            if level == 0:
                x = base * _interleave_halves(jnp.ones_like(f[i]), f[i], 0, t_len)
            else:
                ref = _block_rows(a[i], level, t_len, (1 << level) - 1)
                x = base * jnp.exp(-jnp.abs(a[i] - ref))
            x16.append(x.astype(BF16))
        part = [_dot_nt(x16[i], x16[i]) for i in idx]
        scores = [jnp.where(level_of == level, part[i], scores[i]) for i in idx]
    yield
    last = slice(t_len - 1, t_len)
    q_in = [(qb[i] * jnp.exp(a[i])).astype(BF16) for i in idx]
    k_out = [(kb[i] * jnp.exp(a[i][last] - a[i])).astype(BF16) for i in idx]
    inter = [_dot_nt(q_in[i], s_old_t[i].astype(BF16)) for i in idx]
    intra = [_dot(scores[i].astype(BF16), iv16[i]) for i in idx]
    kv = [_dot_tn(iv16[i], k_out[i]) for i in idx]
    yield
    s_new_t =[jnp.exp(a[i][last]) * s_old_t[i] + kv[i] for i in idx]
    o = [inter[i] + intra[i] for i in idx]
    rms = [lax.rsqrt(jnp.mean(o[i] * o[i], axis=1, keepdims=True) + RMS_EPS) for i in idx]
    out = [_sigmoid(gr[i]) * (o[i] * rms[i] * g_norm[i]) for i in idx]
    return out, s_new_t


def _load_state(c0_ref, n0_ref, m0_ref, s0_ref, c_ref, n_ref, m_ref, s_ref, n_seq_blk):
    c_ref[...] = c0_ref[...]
    n_ref[...] = n0_ref[...]
    m_ref[...] = m0_ref[...]
    for s in range(n_seq_blk):
        for h in range(N_HEADS):
            s_ref[s, h] = s0_ref[s, h].T


def _finish_state(s_ref, n_seq_blk):
    for s in range(n_seq_blk):
        for h in range(N_HEADS):
            s_ref[s, h] = s_ref[s, h].T


def _mixer_body(p_ref, bf_ref, ga_ref, gb_ref, lb_ref, c_ref, n_ref, m_ref, s_ref,
                *, t_len, n_seq_blk, side_stages=None):
    rows = lax.broadcasted_iota(jnp.int32, (t_len, t_len), 0)
    cols = lax.broadcasted_iota(jnp.int32, (t_len, t_len), 1)
    causal = cols <= rows
    tril = causal.astype(BF16)
    level_of = jnp.where(rows > cols, 31 - lax.clz(rows ^ cols), jnp.where(rows == cols, -2, -1))
    lane = lax.broadcasted_iota(jnp.int32, (t_len, LANES), 1)
    is_f = (lane >= N_HEADS) & (lane < 2 * N_HEADS)
    hd = lambda j, h: slice(j * D_GROUP + h * D_HEAD, j * D_GROUP + (h + 1) * D_HEAD)

    units = [(s, h) for s in range(n_seq_blk) for h in range(N_HEADS)]
    rs = lambda s: slice(s * t_len, (s + 1) * t_len)

    z_seq, zt_seq, n_seq, m_seq = [], [], [], []
    for s in range(n_seq_blk):
        gates = p_ref[rs(s), GATE_COL:GATE_COL + LANES]
        log_f = jnp.where(is_f, jax.nn.log_sigmoid(gates + bf_ref[...]), 0.0)
        cum_f = _cumsum_rows(tril, log_f)
        z_seq.append(jnp.where(is_f, cum_f, jnp.where(lane < N_HEADS, gates, 0.0)))
        zt_seq.append(z_seq[s].T)
        n_seq.append(n_ref[s])
        m_seq.append(m_ref[s])
    mlstm = _mlstm_units(
        q=[p_ref[rs(s), hd(0, h)] for s, h in units],
        k=[p_ref[rs(s), hd(1, h)] * (D_HEAD ** -0.5) for s, h in units],
        v=[p_ref[rs(s), hd(2, h)] for s, h in units],
        og=[p_ref[rs(s), hd(3, h)] for s, h in units],
        z=[z_seq[s] for s, h in units], zt=[zt_seq[s] for s, h in units], head=[h for s, h in units],
        c_old=[c_ref[s, h] for s, h in units], n_old=[n_seq[s][h:h + 1, :] for s, h in units],
        m0=[m_seq[s][:, h:h + 1] for s, h in units], g_norm=[ga_ref[:, hd(0, h)] for s, h in units],
        causal=causal, t_len=t_len)
    hgrn = _hgrn_units(
        qr=[p_ref[rs(s), hd(4, h)] for s, h in units], fr=[p_ref[rs(s), hd(5, h)] for s, h in units],
        iv=[p_ref[rs(s), hd(6, h)] for s, h in units], gr=[p_ref[rs(s), hd(7, h)] for s, h in units],
        lb=[lb_ref[:, hd(0, h)] for s, h in units], s_old_t=[s_ref[s, h] for s, h in units],
        g_norm=[gb_ref[:, hd(0, h)] for s, h in units], tril=tril, level_of=level_of, t_len=t_len)
    gens, per_round = [mlstm, hgrn], [1, 2]
    if side_stages is not None:
        gens, per_round = [side_stages] + gens, [SIDE_STAGES_PER_ROUND] + per_round
    results = _run_interleaved(gens, per_round)
    (outs, c_new, n_new, m_new), (outs_b, s_new_t) = results[-2:]

    for i, (s, h) in enumerate(units):
        c_ref[s, h] = c_new[i]
        s_ref[s, h] = s_new_t[i]
    head_lane = lax.broadcasted_iota(jnp.int32, (1, N_HEADS), 1)
    for s in range(n_seq_blk):
        n_ref[s] = jnp.concatenate(n_new[s * N_HEADS:(s + 1) * N_HEADS], axis=0)
        m_row = m_seq[s]
        for h in range(N_HEADS):
            m_row = jnp.where(head_lane == h, m_new[s * N_HEADS + h], m_row)
        m_ref[s] = m_row
    return outs, outs_b


def _mixer_kernel(p_ref, c0_ref, n0_ref, m0_ref, s0_ref, bf_ref, ga_ref, gb_ref, lb_ref,
                  mix_ref, c_ref, n_ref, m_ref, s_ref, *, t_len, n_chunks, n_seq_blk):
    chunk = pl.program_id(1)

    @pl.when(chunk == 0)
    def _():
        _load_state(c0_ref, n0_ref, m0_ref, s0_ref, c_ref, n_ref, m_ref, s_ref, n_seq_blk)

    outs_a, outs_b = _mixer_body(p_ref, bf_ref, ga_ref, gb_ref, lb_ref, c_ref, n_ref, m_ref, s_ref,
                                 t_len=t_len, n_seq_blk=n_seq_blk)
    for i in range(n_seq_blk * N_HEADS):
        s, h = divmod(i, N_HEADS)
        rows = slice(s * t_len, (s + 1) * t_len)
        mix_ref[rows, h * D_HEAD:(h + 1) * D_HEAD] = outs_a[i].astype(mix_ref.dtype)
        mix_ref[rows, D_GROUP + h * D_HEAD:D_GROUP + (h + 1) * D_HEAD] = outs_b[i].astype(mix_ref.dtype)

    @pl.when(chunk == n_chunks - 1)
    def _():
        _finish_state(s_ref, n_seq_blk)


def _mixer(proj, c0, n0, m0, s0, bf_row, ga, gb, lb, *, n_seq, n_chunks, t_len, n_seq_blk,
           shared_init):
    assert not shared_init or n_seq_blk == 1
    assert n_chunks == 1 or n_seq_blk == 1
    nb = n_seq_blk
    init = (lambda b, c: (0, 0, 0, 0)) if shared_init else (lambda b, c: (b, 0, 0, 0))
    init3 = (lambda b, c: (0, 0, 0)) if shared_init else (lambda b, c: (b, 0, 0))
    const = lambda b, c: (0, 0)
    state4 = pl.BlockSpec((nb, N_HEADS, D_HEAD, D_HEAD), lambda b, c: (b, 0, 0, 0))
    return pl.pallas_call(
        functools.partial(_mixer_kernel, t_len=t_len, n_chunks=n_chunks, n_seq_blk=nb),
        grid=(n_seq // nb, n_chunks),
        in_specs=[
            pl.BlockSpec((nb * t_len, D_PROJ), lambda b, c: (b * n_chunks + c, 0)),
            pl.BlockSpec((nb, N_HEADS, D_HEAD, D_HEAD), init),
            pl.BlockSpec((nb, N_HEADS, D_HEAD), init3),
            pl.BlockSpec((nb, 1, N_HEADS), init3),
            pl.BlockSpec((nb, N_HEADS, D_HEAD, D_HEAD), init),
            pl.BlockSpec((1, LANES), const),
            pl.BlockSpec((1, D_GROUP), const),
            pl.BlockSpec((1, D_GROUP), const),
            pl.BlockSpec((1, D_GROUP), const),
        ],
        out_specs=[
            pl.BlockSpec((nb * t_len, D_MODEL), lambda b, c: (b * n_chunks + c, 0)),
            state4,
            pl.BlockSpec((nb, N_HEADS, D_HEAD), lambda b, c: (b, 0, 0)),
            pl.BlockSpec((nb, 1, N_HEADS), lambda b, c: (b, 0, 0)),
            state4,
        ],
        out_shape=[
            jax.ShapeDtypeStruct((n_seq * n_chunks * t_len, D_MODEL), BF16),
            jax.ShapeDtypeStruct((n_seq, N_HEADS, D_HEAD, D_HEAD), F32),
            jax.ShapeDtypeStruct((n_seq, N_HEADS, D_HEAD), F32),
            jax.ShapeDtypeStruct((n_seq, 1, N_HEADS), F32),
            jax.ShapeDtypeStruct((n_seq, N_HEADS, D_HEAD, D_HEAD), F32),
        ],
        compiler_params=pltpu.CompilerParams(
            dimension_semantics=("arbitrary", "arbitrary"), vmem_limit_bytes=VMEM_LIMIT),
        name=f"mixer_t{t_len}",
    )(proj, c0, n0, m0, s0, bf_row, ga, gb, lb)


def _in_proj_stages(x_ref, g_ref, b_ref, w_ref, bias_ref, proj_ref, xn_ref):
    xn = _layer_norm(x_ref[...], g_ref[...], b_ref[...])
    xn_ref[...] = xn
    x16 = xn.astype(BF16)
    for lo in range(0, D_PROJ, IN_PROJ_STAGE_COLS):
        hi = min(lo + IN_PROJ_STAGE_COLS, D_PROJ)
        yield
        proj_ref[:, lo:hi] = _dot(x16, w_ref[:, lo:hi]) + bias_ref[:, lo:hi]


def _prompt_kernel(x0_ref, xnext_ref, ge_ref, be_ref, win_ref, bin_ref, c0_ref, n0_ref, m0_ref, s0_ref,
                   bf_ref, ga_ref, gb_ref, lb_ref, wout_ref, bout_ref, g1_ref, b1_ref,
                   x1_ref, c_ref, n_ref, m_ref, s_ref, proj_scr, xn_scr, proj_alt, xn_alt,
                   *, t_len, n_chunks):
    chunk = pl.program_id(1)
    step = pl.program_id(0) * n_chunks + chunk

    @pl.when(step == 0)
    def _():
        first = _in_proj_stages(x0_ref, ge_ref, be_ref, win_ref, bin_ref, proj_scr, xn_scr)
        _run_interleaved([first], [1])

    @pl.when(chunk == 0)
    def _():
        _load_state(c0_ref, n0_ref, m0_ref, s0_ref, c_ref, n_ref, m_ref, s_ref, 1)

    def tile(proj_cur, xn_cur, proj_next, xn_next):
        next_proj = _in_proj_stages(xnext_ref, ge_ref, be_ref, win_ref, bin_ref, proj_next, xn_next)
        outs_a, outs_b = _mixer_body(
            proj_cur, bf_ref, ga_ref, gb_ref, lb_ref, c_ref, n_ref, m_ref, s_ref,
            t_len=t_len, n_seq_blk=1, side_stages=next_proj)
        mix = jnp.concatenate([o.astype(BF16) for o in outs_a + outs_b], axis=1)
        y = _dot(mix, wout_ref[...]) + bout_ref[...]
        x1_ref[...] = _layer_norm(ALPHA * xn_cur[...] + y, g1_ref[...], b1_ref[...])

    @pl.when(step % 2 == 0)
    def _():
        tile(proj_scr, xn_scr, proj_alt, xn_alt)

    @pl.when(step % 2 == 1)
    def _():
        tile(proj_alt, xn_alt, proj_scr, xn_scr)

    @pl.when(chunk == n_chunks - 1)
    def _():
        _finish_state(s_ref, 1)


def _prompt_mixer(x, ln_e_g, ln_e_b, w_in, b_in, c0, n0, m0, s0, bf_row, ga, gb, lb,
                  w_out, b_out, ln_g, ln_b, *, n_seq, n_chunks, t_len):
    n_tiles = n_seq * n_chunks
    const = lambda b, c: (0, 0)
    init4 = lambda b, c: (0, 0, 0, 0)
    init3 = lambda b, c: (0, 0, 0)
    vec = pl.BlockSpec((1, D_MODEL), const)
    grp = pl.BlockSpec((1, D_GROUP), const)
    state4 = pl.BlockSpec((1, N_HEADS, D_HEAD, D_HEAD), lambda b, c: (b, 0, 0, 0))
    return pl.pallas_call(
        functools.partial(_prompt_kernel, t_len=t_len, n_chunks=n_chunks),
        grid=(n_seq, n_chunks),
        in_specs=[
            pl.BlockSpec((t_len, D_MODEL), const),
            pl.BlockSpec((t_len, D_MODEL),
                         lambda b, c: (jnp.minimum(b * n_chunks + c + 1, n_tiles - 1), 0)),
            vec, vec,
            pl.BlockSpec((D_MODEL, D_PROJ), const),
            pl.BlockSpec((1, D_PROJ), const),
            pl.BlockSpec((1, N_HEADS, D_HEAD, D_HEAD), init4),
            pl.BlockSpec((1, N_HEADS, D_HEAD), init3),
            pl.BlockSpec((1, 1, N_HEADS), init3),
            pl.BlockSpec((1, N_HEADS, D_HEAD, D_HEAD), init4),
            pl.BlockSpec((1, LANES), const),
            grp, grp, grp,
            pl.BlockSpec((D_MODEL, D_MODEL), const),
            vec, vec, vec,
        ],
        out_specs=[
            pl.BlockSpec((t_len, D_MODEL), lambda b, c: (b * n_chunks + c, 0)),
            state4,
            pl.BlockSpec((1, N_HEADS, D_HEAD), lambda b, c: (b, 0, 0)),
            pl.BlockSpec((1, 1, N_HEADS), lambda b, c: (b, 0, 0)),
            state4,
        ],
        out_shape=[
            jax.ShapeDtypeStruct((n_tiles * t_len, D_MODEL), F32),
            jax.ShapeDtypeStruct((n_seq, N_HEADS, D_HEAD, D_HEAD), F32),
            jax.ShapeDtypeStruct((n_seq, N_HEADS, D_HEAD), F32),
            jax.ShapeDtypeStruct((n_seq, 1, N_HEADS), F32),
            jax.ShapeDtypeStruct((n_seq, N_HEADS, D_HEAD, D_HEAD), F32),
        ],
        scratch_shapes=[pltpu.VMEM((t_len, D_PROJ), F32), pltpu.VMEM((t_len, D_MODEL), F32),
                        pltpu.VMEM((t_len, D_PROJ), F32), pltpu.VMEM((t_len, D_MODEL), F32)],
        compiler_params=pltpu.CompilerParams(
            dimension_semantics=("arbitrary", "arbitrary"), vmem_limit_bytes=VMEM_LIMIT),
        name="prompt_mixer",
    )(x, x, ln_e_g, ln_e_b, w_in, b_in, c0, n0, m0, s0, bf_row, ga, gb, lb, w_out, b_out, ln_g, ln_b)


def _out_proj_kernel(x_ref, mix_ref, ge_ref, be_ref, w_ref, bias_ref, g_ref, b_ref, o_ref):
    xn = _layer_norm(x_ref[...], ge_ref[...], be_ref[...])
    y = _dot(mix_ref[...], w_ref[...]) + bias_ref[...]
    o_ref[...] = _layer_norm(ALPHA * xn + y, g_ref[...], b_ref[...])


def _out_proj(x, mix, ln_e_g, ln_e_b, w, bias, ln_g, ln_b, *, tm):
    n = x.shape[0]
    const = lambda i: (0, 0)
    row = pl.BlockSpec((tm, D_MODEL), lambda i: (i, 0))
    vec = pl.BlockSpec((1, D_MODEL), const)
    return pl.pallas_call(
        _out_proj_kernel,
        grid=(n // tm,),
        in_specs=[row, row, vec, vec, pl.BlockSpec((D_MODEL, D_MODEL), const), vec, vec, vec],
        out_specs=row,
        out_shape=jax.ShapeDtypeStruct((n, D_MODEL), F32),
        compiler_params=pltpu.CompilerParams(
            dimension_semantics=("arbitrary",), vmem_limit_bytes=VMEM_LIMIT),
        name="out_proj",
    )(x, mix, ln_e_g, ln_e_b, w, bias, ln_g, ln_b)


def _ffn_kernel(x_ref, cs_ref, wu_ref, bu_ref, wc_ref, bc_ref, wd_ref, bd_ref, g_ref, b_ref,
                y_ref, nc_ref, full_ref, *, n_seq_blk, t_len):
    hist = SUBLANES - (CONV_W - 1)

    @pl.when(pl.program_id(1) == 0)
    def _():
        full_ref[:, hist:SUBLANES, :] = cs_ref[...]

    x = x_ref[...]
    up = _dot(x.astype(BF16), wu_ref[...]) + bu_ref[...]
    u = up[:, :D_FF].reshape(n_seq_blk, t_len, D_FF)
    gate = up[:, D_FF:].reshape(n_seq_blk, t_len, D_FF)
    full_ref[:, SUBLANES:SUBLANES + t_len, :] = u
    conv = bc_ref[...] + u * wc_ref[CONV_W - 1:CONV_W, :]
    for j in range(CONV_W - 1):
        conv = conv + full_ref[:, hist + j:hist + j + t_len, :] * wc_ref[j:j + 1, :]
    last = full_ref[:, hist + t_len:SUBLANES + t_len, :]
    nc_ref[...] = last
    full_ref[:, hist:SUBLANES, :] = last
    act = (conv * _sigmoid(conv) * gate).reshape(n_seq_blk * t_len, D_FF)
    ffn = _dot(act.astype(BF16), wd_ref[...]) + bd_ref[...]
    y_ref[...] = _layer_norm(ALPHA * x + ffn, g_ref[...], b_ref[...])


def _ffn(x, conv_state, w_up, b_up, w_conv, b_conv, w_down, b_down, ln_g, ln_b,
         *, n_seq, seq_len, n_seq_blk, t_len, shared_init):
    n_t = seq_len // t_len
    rows = n_seq_blk * t_len
    const = lambda s, t: (0, 0)
    cs_map = (lambda s, t: (0, 0, 0)) if shared_init else (lambda s, t: (s, 0, 0))
    row = pl.BlockSpec((rows, D_MODEL), lambda s, t: (s * n_t + t, 0))
    vec = pl.BlockSpec((1, D_MODEL), const)
    return pl.pallas_call(
        functools.partial(_ffn_kernel, n_seq_blk=n_seq_blk, t_len=t_len),
        grid=(n_seq // n_seq_blk, n_t),
        in_specs=[
            row,
            pl.BlockSpec((n_seq_blk, CONV_W - 1, D_FF), cs_map),
            _resident((D_MODEL, 2 * D_FF), const),
            pl.BlockSpec((1, 2 * D_FF), const),
            pl.BlockSpec((CONV_W, D_FF), const),
            pl.BlockSpec((1, D_FF), const),
            _resident((D_FF, D_MODEL), const),
            vec, vec, vec,
        ],
        out_specs=[row, pl.BlockSpec((n_seq_blk, CONV_W - 1, D_FF), lambda s, t: (s, 0, 0))],
        out_shape=[
            jax.ShapeDtypeStruct((n_seq * seq_len, D_MODEL), F32),
            jax.ShapeDtypeStruct((n_seq, CONV_W - 1, D_FF), F32),
        ],
        scratch_shapes=[pltpu.VMEM((n_seq_blk, SUBLANES + t_len, D_FF), F32)],
        compiler_params=pltpu.CompilerParams(
            dimension_semantics=("arbitrary", "arbitrary"), vmem_limit_bytes=VMEM_LIMIT),
        name=f"ffn_t{t_len}",
    )(x, conv_state, w_up, b_up, w_conv, b_conv, w_down, b_down, ln_g, ln_b)


def kernel(x_prompt, x_sample, state_mlstm_C, state_mlstm_n, state_mlstm_m, state_hgrn_S, state_ffn_conv, meta_tokens, ln_emb_g, ln_emb_b, w_in, b_in, b_fgate_a, g_norm_a, g_norm_b, hgrn_lb_logits, w_out, b_out, ln1_g, ln1_b, w_up, b_up, w_conv, b_conv, w_down, b_down, ln2_g, ln2_b):
    assert w_in.shape[0] == DEPTH == 1
    n_prompt, seq, _ = x_prompt.shape
    n_sample, dec_seq, _ = x_sample.shape
    row = lambda v: v.reshape(1, -1).astype(F32)

    gate0 = 4 * D_GROUP
    gate1 = gate0 + 2 * N_HEADS
    pad = D_PROJ - w_in.shape[2]
    w_in16 = w_in[0].astype(BF16)
    w_in_p = jnp.concatenate(
        [w_in16[:, :gate0], w_in16[:, gate1:], w_in16[:, gate0:gate1],
         jnp.zeros((D_MODEL, pad), BF16)], axis=1)
    b_in_p = jnp.concatenate(
        [b_in[0][:gate0], b_in[0][gate1:], b_in[0][gate0:gate1], jnp.zeros((pad,), b_in.dtype)]
    ).reshape(1, D_PROJ).astype(F32)
    bf_row = jnp.zeros((1, LANES), F32).at[0, N_HEADS:2 * N_HEADS].set(b_fgate_a[0].astype(F32))
    lb = jnp.cumsum(jax.nn.softmax(hgrn_lb_logits.astype(F32), axis=0), axis=0)[0].reshape(1, D_GROUP)
    ga, gb = row(g_norm_a[0]), row(g_norm_b[0])
    ln_e = (row(ln_emb_g), row(ln_emb_b))
    out_p = (w_out[0].astype(BF16), row(b_out[0]), row(ln1_g[0]), row(ln1_b[0]))
    ffn_p = (w_up[0].astype(BF16), row(b_up[0]), w_conv[0].astype(F32), row(b_conv[0]),
             w_down[0].astype(BF16), row(b_down[0]), row(ln2_g[0]), row(ln2_b[0]))

    def layer(x_rows, mixer_state, conv_state, *, n_seq, seq_len, t_mix, mix_seq_blk, tm,
              ffn_seq_blk, ffn_t, shared_init):
        proj = _in_proj(x_rows, *ln_e, w_in_p, b_in_p, tm=tm)
        mix, c_new, n_new, m_new, s_new = _mixer(
            proj, *mixer_state, bf_row, ga, gb, lb, n_seq=n_seq, n_chunks=seq_len // t_mix,
            t_len=t_mix, n_seq_blk=mix_seq_blk, shared_init=shared_init)
        x1 = _out_proj(x_rows, mix, *ln_e, *out_p, tm=tm)
        y, conv_new = _ffn(x1, conv_state, *ffn_p, n_seq=n_seq, seq_len=seq_len,
                           n_seq_blk=ffn_seq_blk, t_len=ffn_t, shared_init=shared_init)
        return y, c_new, n_new, m_new, s_new, conv_new

    zero_state = (jnp.zeros((1, N_HEADS, D_HEAD, D_HEAD), F32), jnp.zeros((1, N_HEADS, D_HEAD), F32),
                  jnp.zeros((1, 1, N_HEADS), F32), jnp.zeros((1, N_HEADS, D_HEAD, D_HEAD), F32))
    _, c_m, n_m, m_m, s_m, conv_m = layer(
        meta_tokens.astype(F32), zero_state, jnp.zeros((1, CONV_W - 1, D_FF), F32),
        n_seq=1, seq_len=N_META, t_mix=N_META, mix_seq_blk=1, tm=N_META, ffn_seq_blk=1,
        ffn_t=N_META, shared_init=False)

    x1_p, c_p, n_p, m_p, s_p = _prompt_mixer(
        x_prompt.reshape(n_prompt * seq, D_MODEL), *ln_e, w_in_p, b_in_p, c_m, n_m, m_m, s_m,
        bf_row, ga, gb, lb, *out_p, n_seq=n_prompt, n_chunks=seq // PROMPT_CHUNK, t_len=PROMPT_CHUNK)
    y_p, conv_p = _ffn(x1_p, conv_m, *ffn_p, n_seq=n_prompt, seq_len=seq, n_seq_blk=1, t_len=512,
                       shared_init=True)

    sample_state = (state_mlstm_C[0].astype(F32), state_mlstm_n[0].astype(F32),
                    state_mlstm_m[0].astype(F32).reshape(n_sample, 1, N_HEADS),
                    state_hgrn_S[0].astype(F32))
    y_s, c_s, n_s, m_s, s_s, conv_s = layer(
        x_sample.reshape(n_sample * dec_seq, D_MODEL), sample_state, state_ffn_conv[0].astype(F32),
        n_seq=n_sample, seq_len=dec_seq, t_mix=dec_seq, mix_seq_blk=8, tm=256, ffn_seq_blk=32,
        ffn_t=dec_seq, shared_init=False)

    lead = lambda v: v[None]
    return (y_p.reshape(n_prompt, seq, D_MODEL), y_s.reshape(n_sample, dec_seq, D_MODEL),
            lead(c_p), lead(n_p), lead(m_p.reshape(n_prompt, N_HEADS)), lead(s_p), lead(conv_p),
            lead(c_s), lead(n_s), lead(m_s.reshape(n_sample, N_HEADS)), lead(s_s), lead(conv_s))
```

```python
import functools

import jax
import jax.numpy as jnp
from jax import lax
from jax.experimental import pallas as pl
from jax.experimental.pallas import tpu as pltpu

D_MODEL = 1024
N_META = 16
N_HEADS = 4
D_HEAD = 128
D_GROUP = N_HEADS * D_HEAD
D_FF = 2816
CONV_W = 3
DEPTH = 1
ALPHA = (2.0 * DEPTH) ** 0.25
LN_EPS = 1e-5
RMS_EPS = 1e-6
NEG_LOG2_E = -1.4426950408889634

LANES = 128
SUBLANES = 8
GATE_COL = 8 * D_GROUP
D_PROJ = GATE_COL + LANES
IN_PROJ_STAGE_COLS = 256
SIDE_STAGES_PER_ROUND = 3
PROMPT_CHUNK = 128
VMEM_LIMIT = 56 * 1024 * 1024

F32 = jnp.float32
BF16 = jnp.bfloat16
HIGHEST = lax.Precision.HIGHEST
NT_DIMS = (((1,), (1,)), ((), ()))
TN_DIMS = (((0,), (0,)), ((), ()))


def _layer_norm(x, g, b):
    mu = jnp.mean(x, axis=-1, keepdims=True)
    xc = x - mu
    var = jnp.mean(xc * xc, axis=-1, keepdims=True)
    return xc * lax.rsqrt(var + LN_EPS) * g + b


def _exp_neg(x):
    return jnp.exp2(x * NEG_LOG2_E)


def _sigmoid(x):
    return 1.0 / (1.0 + _exp_neg(x))


def _resident(block_shape, index_map):
    return pl.BlockSpec(block_shape, index_map, pipeline_mode=pl.Buffered(1))


def _dot(a, b, precision=None):
    return jnp.dot(a, b, precision=precision, preferred_element_type=F32)


def _dot_nt(a, b, precision=None):
    return lax.dot_general(a, b, NT_DIMS, precision=precision, preferred_element_type=F32)


def _dot_tn(a, b):
    return lax.dot_general(a, b, TN_DIMS, preferred_element_type=F32)


def _in_proj_kernel(x_ref, g_ref, b_ref, w_ref, bias_ref, o_ref):
    xn = _layer_norm(x_ref[...], g_ref[...], b_ref[...])
    o_ref[...] = _dot(xn.astype(BF16), w_ref[...]) + bias_ref[...]


def _in_proj(x, ln_g, ln_b, w, bias, *, tm):
    n = x.shape[0]
    const = lambda i: (0, 0)
    return pl.pallas_call(
        _in_proj_kernel,
        grid=(n // tm,),
        in_specs=[
            pl.BlockSpec((tm, D_MODEL), lambda i: (i, 0)),
            pl.BlockSpec((1, D_MODEL), const),
            pl.BlockSpec((1, D_MODEL), const),
            pl.BlockSpec((D_MODEL, D_PROJ), const),
            pl.BlockSpec((1, D_PROJ), const),
        ],
        out_specs=pl.BlockSpec((tm, D_PROJ), lambda i: (i, 0)),
        out_shape=jax.ShapeDtypeStruct((n, D_PROJ), F32),
        compiler_params=pltpu.CompilerParams(
            dimension_semantics=("arbitrary",), vmem_limit_bytes=VMEM_LIMIT),
        name="in_proj",
    )(x, ln_g, ln_b, w, bias)


def _block_rows(x, level, t_len, row_in_block):
    size = 2 << level
    if size > SUBLANES:
        pieces = [jnp.broadcast_to(x[j * size + row_in_block:j * size + row_in_block + 1, :],
                                   (size, x.shape[1])) for j in range(t_len // size)]
        return pieces[0] if len(pieces) == 1 else jnp.concatenate(pieces, axis=0)
    x3 = x.reshape(t_len // SUBLANES, SUBLANES, x.shape[1])
    sub = lax.broadcasted_iota(jnp.int32, x3.shape, 1)
    out = None
    for j in range(SUBLANES // size):
        row = jnp.broadcast_to(x3[:, j * size + row_in_block:j * size + row_in_block + 1, :], x3.shape)
        out = row if out is None else jnp.where(sub >= j * size, row, out)
    return out.reshape(x.shape)


def _interleave_halves(lower, upper, level, t_len):
    half = 1 << level
    if half >= SUBLANES:
        pieces = []
        for j in range(t_len // (2 * half)):
            pieces.append(lower[2 * half * j:2 * half * j + half])
            pieces.append(upper[2 * half * j + half:2 * half * (j + 1)])
        return jnp.concatenate(pieces, axis=0)
    rows = lax.broadcasted_iota(jnp.int32, lower.shape, 0)
    return jnp.where((rows & half) != 0, upper, lower)


def _run_interleaved(gens, stages_per_round):
    results = [None] * len(gens)
    live = [True] * len(gens)
    while any(live):
        for g, steps in enumerate(stages_per_round):
            for _ in range(steps):
                if live[g]:
                    try:
                        next(gens[g])
                    except StopIteration as stop:
                        results[g], live[g] = stop.value, False
    return results


def _cumsum_rows(tril16, x):
    hi = x.astype(BF16)
    rest = x - hi.astype(F32)
    mid = rest.astype(BF16)
    lo = (rest - mid.astype(F32)).astype(BF16)
    return _dot(tril16, hi) + _dot(tril16, mid) + _dot(tril16, lo)


def _mlstm_units(*, q, k, v, og, z, zt, head, c_old, n_old, m0, g_norm, causal, t_len):
    idx = range(len(q))
    q16 = [q[i].astype(BF16) for i in idx]
    k16 = [k[i].astype(BF16) for i in idx]
    qk = [_dot_nt(q16[i], k16[i]) for i in idx]
    qc = [_dot(q16[i], c_old[i].astype(BF16)) for i in idx]
    bs_row = [zt[i][N_HEADS + head[i]:N_HEADS + head[i] + 1, :] - zt[i][head[i]:head[i] + 1, :]
              for i in idx]
    yield
    b_col = [z[i][:, N_HEADS + head[i]:N_HEADS + head[i] + 1] for i in idx]
    i_col = [z[i][:, head[i]:head[i] + 1] for i in idx]
    d = [jnp.where(causal, b_col[i] - bs_row[i], -jnp.inf) for i in idx]
    m_t = [jnp.maximum(b_col[i] + m0[i], jnp.max(d[i], axis=1, keepdims=True)) for i in idx]
    dec = [jnp.exp(b_col[i] + m0[i] - m_t[i]) for i in idx]
    sw = [jnp.exp(d[i] - m_t[i]) * qk[i] for i in idx]
    yield
    swv = [_dot(sw[i].astype(BF16), v[i].astype(BF16)) for i in idx]
    last = slice(t_len - 1, t_len)
    w_last = [jnp.exp(b_col[i][last] - b_col[i] + i_col[i] - m_t[i][last]) for i in idx]
    kv = [_dot_tn(k16[i], (w_last[i] * v[i]).astype(BF16)) for i in idx]
    yield
    c_new =[dec[i][last] * c_old[i] + kv[i] for i in idx]
    n_new = [dec[i][last] * n_old[i] + jnp.sum(w_last[i] * k[i], axis=0, keepdims=True) for i in idx]
    m_new = [m_t[i][last] for i in idx]
    den = [dec[i] * jnp.sum(q[i] * n_old[i], axis=1, keepdims=True)
           + jnp.sum(sw[i], axis=1, keepdims=True) for i in idx]
    hid = [(dec[i] * qc[i] + swv[i]) / jnp.maximum(jnp.abs(den[i]), _exp_neg(m_t[i])) for i in idx]
    yield
    rms = [lax.rsqrt(jnp.mean(hid[i] * hid[i], axis=1, keepdims=True) + RMS_EPS) for i in idx]
    out = [_sigmoid(og[i]) * (hid[i] * rms[i] * g_norm[i]) for i in idx]
    return out, c_new, n_new, m_new


def _hgrn_units(*, qr, fr, iv, gr, lb, s_old_t, g_norm, tril, level_of, t_len):
    idx = range(len(qr))
    n_levels = t_len.bit_length() - 1
    f = [lb[i] + (1.0 - lb[i]) * _sigmoid(fr[i]) for i in idx]
    a = [_cumsum_rows(tril, jnp.log(f[i])) for i in idx]
    yield
    kb = [(1.0 - lb[i]) / (1.0 + jnp.exp(fr[i])) for i in idx]
    qb = [qr[i] * _sigmoid(qr[i]) for i in idx]
    iv16 = [iv[i].astype(BF16) for i in idx]
    diag = [_dot_nt(qb[i].astype(BF16), kb[i].astype(BF16)) for i in idx]
    scores = [jnp.where(level_of == -2, diag[i], 0.0) for i in idx]
    for level in range(n_levels):
        yield
        x16 = []
        for i in idx:
            base = _interleave_halves(kb[i], qb[i], level, t_len)
            if level == 0:
                x = base * _interleave_halves(jnp.ones_like(f[i]), f[i], 0, t_len)
            else:
                ref = _block_rows(a[i], level, t_len, (1 << level) - 1)
                x = base * _exp_neg(jnp.abs(a[i] - ref))
            x16.append(x.astype(BF16))
        part = [_dot_nt(x16[i], x16[i]) for i in idx]
        scores = [jnp.where(level_of == level, part[i], scores[i]) for i in idx]
    yield
    last = slice(t_len - 1, t_len)
    q_in = [(qb[i] * jnp.exp(a[i])).astype(BF16) for i in idx]
    k_out = [(kb[i] * jnp.exp(a[i][last] - a[i])).astype(BF16) for i in idx]
    inter = [_dot_nt(q_in[i], s_old_t[i].astype(BF16)) for i in idx]
    intra = [_dot(scores[i].astype(BF16), iv16[i]) for i in idx]
    kv = [_dot_tn(iv16[i], k_out[i]) for i in idx]
    yield
    s_new_t =[jnp.exp(a[i][last]) * s_old_t[i] + kv[i] for i in idx]
    o = [inter[i] + intra[i] for i in idx]
    rms = [lax.rsqrt(jnp.mean(o[i] * o[i], axis=1, keepdims=True) + RMS_EPS) for i in idx]
    out = [_sigmoid(gr[i]) * (o[i] * rms[i] * g_norm[i]) for i in idx]
    return out, s_new_t


def _load_state(c0_ref, n0_ref, m0_ref, s0_ref, c_ref, n_ref, m_ref, s_ref, n_seq_blk):
    c_ref[...] = c0_ref[...]
    n_ref[...] = n0_ref[...]
    m_ref[...] = m0_ref[...]
    for s in range(n_seq_blk):
        for h in range(N_HEADS):
            s_ref[s, h] = s0_ref[s, h].T


def _finish_state(s_ref, n_seq_blk):
    for s in range(n_seq_blk):
        for h in range(N_HEADS):
            s_ref[s, h] = s_ref[s, h].T


def _mixer_body(p_ref, bf_ref, ga_ref, gb_ref, lb_ref, c_ref, n_ref, m_ref, s_ref,
                *, t_len, n_seq_blk, side_stages=None):
    rows = lax.broadcasted_iota(jnp.int32, (t_len, t_len), 0)
    cols = lax.broadcasted_iota(jnp.int32, (t_len, t_len), 1)
    causal = cols <= rows
    tril = causal.astype(BF16)
    level_of = jnp.where(rows > cols, 31 - lax.clz(rows ^ cols), jnp.where(rows == cols, -2, -1))
    lane = lax.broadcasted_iota(jnp.int32, (t_len, LANES), 1)
    is_f = (lane >= N_HEADS) & (lane < 2 * N_HEADS)
    hd = lambda j, h: slice(j * D_GROUP + h * D_HEAD, j * D_GROUP + (h + 1) * D_HEAD)

    units = [(s, h) for s in range(n_seq_blk) for h in range(N_HEADS)]
    rs = lambda s: slice(s * t_len, (s + 1) * t_len)

    z_seq, zt_seq, n_seq, m_seq = [], [], [], []
    for s in range(n_seq_blk):
        gates = p_ref[rs(s), GATE_COL:GATE_COL + LANES]
        log_f = jnp.where(is_f, jax.nn.log_sigmoid(gates + bf_ref[...]), 0.0)
        cum_f = _cumsum_rows(tril, log_f)
        z_seq.append(jnp.where(is_f, cum_f, jnp.where(lane < N_HEADS, gates, 0.0)))
        zt_seq.append(z_seq[s].T)
        n_seq.append(n_ref[s])
        m_seq.append(m_ref[s])
    mlstm = _mlstm_units(
        q=[p_ref[rs(s), hd(0, h)] for s, h in units],
        k=[p_ref[rs(s), hd(1, h)] * (D_HEAD ** -0.5) for s, h in units],
        v=[p_ref[rs(s), hd(2, h)] for s, h in units],
        og=[p_ref[rs(s), hd(3, h)] for s, h in units],
        z=[z_seq[s] for s, h in units], zt=[zt_seq[s] for s, h in units], head=[h for s, h in units],
        c_old=[c_ref[s, h] for s, h in units], n_old=[n_seq[s][h:h + 1, :] for s, h in units],
        m0=[m_seq[s][:, h:h + 1] for s, h in units], g_norm=[ga_ref[:, hd(0, h)] for s, h in units],
        causal=causal, t_len=t_len)
    hgrn = _hgrn_units(
        qr=[p_ref[rs(s), hd(4, h)] for s, h in units], fr=[p_ref[rs(s), hd(5, h)] for s, h in units],
        iv=[p_ref[rs(s), hd(6, h)] for s, h in units], gr=[p_ref[rs(s), hd(7, h)] for s, h in units],
        lb=[lb_ref[:, hd(0, h)] for s, h in units], s_old_t=[s_ref[s, h] for s, h in units],
        g_norm=[gb_ref[:, hd(0, h)] for s, h in units], tril=tril, level_of=level_of, t_len=t_len)
    gens, per_round = [mlstm, hgrn], [1, 2]
    if side_stages is not None:
        gens, per_round = gens + [side_stages], per_round + [SIDE_STAGES_PER_ROUND]
    results = _run_interleaved(gens, per_round)
    (outs, c_new, n_new, m_new), (outs_b, s_new_t) = results[:2]

    for i, (s, h) in enumerate(units):
        c_ref[s, h] = c_new[i]
        s_ref[s, h] = s_new_t[i]
    head_lane = lax.broadcasted_iota(jnp.int32, (1, N_HEADS), 1)
    for s in range(n_seq_blk):
        n_ref[s] = jnp.concatenate(n_new[s * N_HEADS:(s + 1) * N_HEADS], axis=0)
        m_row = m_seq[s]
        for h in range(N_HEADS):
            m_row = jnp.where(head_lane == h, m_new[s * N_HEADS + h], m_row)
        m_ref[s] = m_row
    return outs, outs_b


def _mixer_kernel(p_ref, c0_ref, n0_ref, m0_ref, s0_ref, bf_ref, ga_ref, gb_ref, lb_ref,
                  mix_ref, c_ref, n_ref, m_ref, s_ref, *, t_len, n_chunks, n_seq_blk):
    chunk = pl.program_id(1)

    @pl.when(chunk == 0)
    def _():
        _load_state(c0_ref, n0_ref, m0_ref, s0_ref, c_ref, n_ref, m_ref, s_ref, n_seq_blk)

    outs_a, outs_b = _mixer_body(p_ref, bf_ref, ga_ref, gb_ref, lb_ref, c_ref, n_ref, m_ref, s_ref,
                                 t_len=t_len, n_seq_blk=n_seq_blk)
    for i in range(n_seq_blk * N_HEADS):
        s, h = divmod(i, N_HEADS)
        rows = slice(s * t_len, (s + 1) * t_len)
        mix_ref[rows, h * D_HEAD:(h + 1) * D_HEAD] = outs_a[i].astype(mix_ref.dtype)
        mix_ref[rows, D_GROUP + h * D_HEAD:D_GROUP + (h + 1) * D_HEAD] = outs_b[i].astype(mix_ref.dtype)

    @pl.when(chunk == n_chunks - 1)
    def _():
        _finish_state(s_ref, n_seq_blk)


def _mixer(proj, c0, n0, m0, s0, bf_row, ga, gb, lb, *, n_seq, n_chunks, t_len, n_seq_blk,
           shared_init):
    assert not shared_init or n_seq_blk == 1
    assert n_chunks == 1 or n_seq_blk == 1
    nb = n_seq_blk
    init = (lambda b, c: (0, 0, 0, 0)) if shared_init else (lambda b, c: (b, 0, 0, 0))
    init3 = (lambda b, c: (0, 0, 0)) if shared_init else (lambda b, c: (b, 0, 0))
    const = lambda b, c: (0, 0)
    state4 = pl.BlockSpec((nb, N_HEADS, D_HEAD, D_HEAD), lambda b, c: (b, 0, 0, 0))
    return pl.pallas_call(
        functools.partial(_mixer_kernel, t_len=t_len, n_chunks=n_chunks, n_seq_blk=nb),
        grid=(n_seq // nb, n_chunks),
        in_specs=[
            pl.BlockSpec((nb * t_len, D_PROJ), lambda b, c: (b * n_chunks + c, 0)),
            pl.BlockSpec((nb, N_HEADS, D_HEAD, D_HEAD), init),
            pl.BlockSpec((nb, N_HEADS, D_HEAD), init3),
            pl.BlockSpec((nb, 1, N_HEADS), init3),
            pl.BlockSpec((nb, N_HEADS, D_HEAD, D_HEAD), init),
            pl.BlockSpec((1, LANES), const),
            pl.BlockSpec((1, D_GROUP), const),
            pl.BlockSpec((1, D_GROUP), const),
            pl.BlockSpec((1, D_GROUP), const),
        ],
        out_specs=[
            pl.BlockSpec((nb * t_len, D_MODEL), lambda b, c: (b * n_chunks + c, 0)),
            state4,
            pl.BlockSpec((nb, N_HEADS, D_HEAD), lambda b, c: (b, 0, 0)),
            pl.BlockSpec((nb, 1, N_HEADS), lambda b, c: (b, 0, 0)),
            state4,
        ],
        out_shape=[
            jax.ShapeDtypeStruct((n_seq * n_chunks * t_len, D_MODEL), BF16),
            jax.ShapeDtypeStruct((n_seq, N_HEADS, D_HEAD, D_HEAD), F32),
            jax.ShapeDtypeStruct((n_seq, N_HEADS, D_HEAD), F32),
            jax.ShapeDtypeStruct((n_seq, 1, N_HEADS), F32),
            jax.ShapeDtypeStruct((n_seq, N_HEADS, D_HEAD, D_HEAD), F32),
        ],
        compiler_params=pltpu.CompilerParams(
            dimension_semantics=("arbitrary", "arbitrary"), vmem_limit_bytes=VMEM_LIMIT),
        name=f"mixer_t{t_len}",
    )(proj, c0, n0, m0, s0, bf_row, ga, gb, lb)


def _in_proj_stages(x_ref, g_ref, b_ref, w_ref, bias_ref, proj_ref, xn_ref):
    xn = _layer_norm(x_ref[...], g_ref[...], b_ref[...])
    xn_ref[...] = xn
    x16 = xn.astype(BF16)
    for lo in range(0, D_PROJ, IN_PROJ_STAGE_COLS):
        hi = min(lo + IN_PROJ_STAGE_COLS, D_PROJ)
        yield
        proj_ref[:, lo:hi] = _dot(x16, w_ref[:, lo:hi]) + bias_ref[:, lo:hi]


def _prompt_kernel(x0_ref, xnext_ref, ge_ref, be_ref, win_ref, bin_ref, c0_ref, n0_ref, m0_ref, s0_ref,
                   bf_ref, ga_ref, gb_ref, lb_ref, wout_ref, bout_ref, g1_ref, b1_ref,
                   x1_ref, c_ref, n_ref, m_ref, s_ref, proj_scr, xn_scr, proj_alt, xn_alt,
                   *, t_len, n_chunks):
    chunk = pl.program_id(1)
    step = pl.program_id(0) * n_chunks + chunk

    @pl.when(step == 0)
    def _():
        first = _in_proj_stages(x0_ref, ge_ref, be_ref, win_ref, bin_ref, proj_scr, xn_scr)
        _run_interleaved([first], [1])

    @pl.when(chunk == 0)
    def _():
        _load_state(c0_ref, n0_ref, m0_ref, s0_ref, c_ref, n_ref, m_ref, s_ref, 1)

    def tile(proj_cur, xn_cur, proj_next, xn_next):
        next_proj = _in_proj_stages(xnext_ref, ge_ref, be_ref, win_ref, bin_ref, proj_next, xn_next)
        outs_a, outs_b = _mixer_body(
            proj_cur, bf_ref, ga_ref, gb_ref, lb_ref, c_ref, n_ref, m_ref, s_ref,
            t_len=t_len, n_seq_blk=1, side_stages=next_proj)
        mix = jnp.concatenate([o.astype(BF16) for o in outs_a + outs_b], axis=1)
        y = _dot(mix, wout_ref[...]) + bout_ref[...]
        x1_ref[...] = _layer_norm(ALPHA * xn_cur[...] + y, g1_ref[...], b1_ref[...])

    @pl.when(step % 2 == 0)
    def _():
        tile(proj_scr, xn_scr, proj_alt, xn_alt)

    @pl.when(step % 2 == 1)
    def _():
        tile(proj_alt, xn_alt, proj_scr, xn_scr)

    @pl.when(chunk == n_chunks - 1)
    def _():
        _finish_state(s_ref, 1)


def _prompt_mixer(x, ln_e_g, ln_e_b, w_in, b_in, c0, n0, m0, s0, bf_row, ga, gb, lb,
                  w_out, b_out, ln_g, ln_b, *, n_seq, n_chunks, t_len):
    n_tiles = n_seq * n_chunks
    const = lambda b, c: (0, 0)
    init4 = lambda b, c: (0, 0, 0, 0)
    init3 = lambda b, c: (0, 0, 0)
    vec = pl.BlockSpec((1, D_MODEL), const)
    grp = pl.BlockSpec((1, D_GROUP), const)
    state4 = pl.BlockSpec((1, N_HEADS, D_HEAD, D_HEAD), lambda b, c: (b, 0, 0, 0))
    return pl.pallas_call(
        functools.partial(_prompt_kernel, t_len=t_len, n_chunks=n_chunks),
        grid=(n_seq, n_chunks),
        in_specs=[
            pl.BlockSpec((t_len, D_MODEL), const),
            pl.BlockSpec((t_len, D_MODEL),
                         lambda b, c: (jnp.minimum(b * n_chunks + c + 1, n_tiles - 1), 0)),
            vec, vec,
            pl.BlockSpec((D_MODEL, D_PROJ), const),
            pl.BlockSpec((1, D_PROJ), const),
            pl.BlockSpec((1, N_HEADS, D_HEAD, D_HEAD), init4),
            pl.BlockSpec((1, N_HEADS, D_HEAD), init3),
            pl.BlockSpec((1, 1, N_HEADS), init3),
            pl.BlockSpec((1, N_HEADS, D_HEAD, D_HEAD), init4),
            pl.BlockSpec((1, LANES), const),
            grp, grp, grp,
            pl.BlockSpec((D_MODEL, D_MODEL), const),
            vec, vec, vec,
        ],
        out_specs=[
            pl.BlockSpec((t_len, D_MODEL), lambda b, c: (b * n_chunks + c, 0)),
            state4,
            pl.BlockSpec((1, N_HEADS, D_HEAD), lambda b, c: (b, 0, 0)),
            pl.BlockSpec((1, 1, N_HEADS), lambda b, c: (b, 0, 0)),
            state4,
        ],
        out_shape=[
            jax.ShapeDtypeStruct((n_tiles * t_len, D_MODEL), F32),
            jax.ShapeDtypeStruct((n_seq, N_HEADS, D_HEAD, D_HEAD), F32),
            jax.ShapeDtypeStruct((n_seq, N_HEADS, D_HEAD), F32),
            jax.ShapeDtypeStruct((n_seq, 1, N_HEADS), F32),
            jax.ShapeDtypeStruct((n_seq, N_HEADS, D_HEAD, D_HEAD), F32),
        ],
        scratch_shapes=[pltpu.VMEM((t_len, D_PROJ), F32), pltpu.VMEM((t_len, D_MODEL), F32),
                        pltpu.VMEM((t_len, D_PROJ), F32), pltpu.VMEM((t_len, D_MODEL), F32)],
        compiler_params=pltpu.CompilerParams(
            dimension_semantics=("arbitrary", "arbitrary"), vmem_limit_bytes=VMEM_LIMIT),
        name="prompt_mixer",
    )(x, x, ln_e_g, ln_e_b, w_in, b_in, c0, n0, m0, s0, bf_row, ga, gb, lb, w_out, b_out, ln_g, ln_b)


def _out_proj_kernel(x_ref, mix_ref, ge_ref, be_ref, w_ref, bias_ref, g_ref, b_ref, o_ref):
    xn = _layer_norm(x_ref[...], ge_ref[...], be_ref[...])
    y = _dot(mix_ref[...], w_ref[...]) + bias_ref[...]
    o_ref[...] = _layer_norm(ALPHA * xn + y, g_ref[...], b_ref[...])


def _out_proj(x, mix, ln_e_g, ln_e_b, w, bias, ln_g, ln_b, *, tm):
    n = x.shape[0]
    const = lambda i: (0, 0)
    row = pl.BlockSpec((tm, D_MODEL), lambda i: (i, 0))
    vec = pl.BlockSpec((1, D_MODEL), const)
    return pl.pallas_call(
        _out_proj_kernel,
        grid=(n // tm,),
        in_specs=[row, row, vec, vec, pl.BlockSpec((D_MODEL, D_MODEL), const), vec, vec, vec],
        out_specs=row,
        out_shape=jax.ShapeDtypeStruct((n, D_MODEL), F32),
        compiler_params=pltpu.CompilerParams(
            dimension_semantics=("arbitrary",), vmem_limit_bytes=VMEM_LIMIT),
        name="out_proj",
    )(x, mix, ln_e_g, ln_e_b, w, bias, ln_g, ln_b)


def _ffn_kernel(x_ref, cs_ref, wu_ref, bu_ref, wc_ref, bc_ref, wd_ref, bd_ref, g_ref, b_ref,
                y_ref, nc_ref, full_ref, *, n_seq_blk, t_len):
    hist = SUBLANES - (CONV_W - 1)

    @pl.when(pl.program_id(1) == 0)
    def _():
        full_ref[:, hist:SUBLANES, :] = cs_ref[...]

    x = x_ref[...]
    up = _dot(x.astype(BF16), wu_ref[...]) + bu_ref[...]
    u = up[:, :D_FF].reshape(n_seq_blk, t_len, D_FF)
    gate = up[:, D_FF:].reshape(n_seq_blk, t_len, D_FF)
    full_ref[:, SUBLANES:SUBLANES + t_len, :] = u
    conv = bc_ref[...] + u * wc_ref[CONV_W - 1:CONV_W, :]
    for j in range(CONV_W - 1):
        conv = conv + full_ref[:, hist + j:hist + j + t_len, :] * wc_ref[j:j + 1, :]
    last = full_ref[:, hist + t_len:SUBLANES + t_len, :]
    nc_ref[...] = last
    full_ref[:, hist:SUBLANES, :] = last
    act = (conv * _sigmoid(conv) * gate).reshape(n_seq_blk * t_len, D_FF)
    ffn = _dot(act.astype(BF16), wd_ref[...]) + bd_ref[...]
    y_ref[...] = _layer_norm(ALPHA * x + ffn, g_ref[...], b_ref[...])


def _ffn(x, conv_state, w_up, b_up, w_conv, b_conv, w_down, b_down, ln_g, ln_b,
         *, n_seq, seq_len, n_seq_blk, t_len, shared_init):
    n_t = seq_len // t_len
    rows = n_seq_blk * t_len
    const = lambda s, t: (0, 0)
    cs_map = (lambda s, t: (0, 0, 0)) if shared_init else (lambda s, t: (s, 0, 0))
    row = pl.BlockSpec((rows, D_MODEL), lambda s, t: (s * n_t + t, 0))
    vec = pl.BlockSpec((1, D_MODEL), const)
    return pl.pallas_call(
        functools.partial(_ffn_kernel, n_seq_blk=n_seq_blk, t_len=t_len),
        grid=(n_seq // n_seq_blk, n_t),
        in_specs=[
            row,
            pl.BlockSpec((n_seq_blk, CONV_W - 1, D_FF), cs_map),
            _resident((D_MODEL, 2 * D_FF), const),
            pl.BlockSpec((1, 2 * D_FF), const),
            pl.BlockSpec((CONV_W, D_FF), const),
            pl.BlockSpec((1, D_FF), const),
            _resident((D_FF, D_MODEL), const),
            vec, vec, vec,
        ],
        out_specs=[row, pl.BlockSpec((n_seq_blk, CONV_W - 1, D_FF), lambda s, t: (s, 0, 0))],
        out_shape=[
            jax.ShapeDtypeStruct((n_seq * seq_len, D_MODEL), F32),
            jax.ShapeDtypeStruct((n_seq, CONV_W - 1, D_FF), F32),
        ],
        scratch_shapes=[pltpu.VMEM((n_seq_blk, SUBLANES + t_len, D_FF), F32)],
        compiler_params=pltpu.CompilerParams(
            dimension_semantics=("arbitrary", "arbitrary"), vmem_limit_bytes=VMEM_LIMIT),
        name=f"ffn_t{t_len}",
    )(x, conv_state, w_up, b_up, w_conv, b_conv, w_down, b_down, ln_g, ln_b)


def kernel(x_prompt, x_sample, state_mlstm_C, state_mlstm_n, state_mlstm_m, state_hgrn_S, state_ffn_conv, meta_tokens, ln_emb_g, ln_emb_b, w_in, b_in, b_fgate_a, g_norm_a, g_norm_b, hgrn_lb_logits, w_out, b_out, ln1_g, ln1_b, w_up, b_up, w_conv, b_conv, w_down, b_down, ln2_g, ln2_b):
    assert w_in.shape[0] == DEPTH == 1
    n_prompt, seq, _ = x_prompt.shape
    n_sample, dec_seq, _ = x_sample.shape
    row = lambda v: v.reshape(1, -1).astype(F32)

    gate0 = 4 * D_GROUP
    gate1 = gate0 + 2 * N_HEADS
    pad = D_PROJ - w_in.shape[2]
    w_in16 = w_in[0].astype(BF16)
    w_in_p = jnp.concatenate(
        [w_in16[:, :gate0], w_in16[:, gate1:], w_in16[:, gate0:gate1],
         jnp.zeros((D_MODEL, pad), BF16)], axis=1)
    b_in_p = jnp.concatenate(
        [b_in[0][:gate0], b_in[0][gate1:], b_in[0][gate0:gate1], jnp.zeros((pad,), b_in.dtype)]
    ).reshape(1, D_PROJ).astype(F32)
    bf_row = jnp.zeros((1, LANES), F32).at[0, N_HEADS:2 * N_HEADS].set(b_fgate_a[0].astype(F32))
    lb = jnp.cumsum(jax.nn.softmax(hgrn_lb_logits.astype(F32), axis=0), axis=0)[0].reshape(1, D_GROUP)
    ga, gb = row(g_norm_a[0]), row(g_norm_b[0])
    ln_e = (row(ln_emb_g), row(ln_emb_b))
    out_p = (w_out[0].astype(BF16), row(b_out[0]), row(ln1_g[0]), row(ln1_b[0]))
    ffn_p = (w_up[0].astype(BF16), row(b_up[0]), w_conv[0].astype(F32), row(b_conv[0]),
             w_down[0].astype(BF16), row(b_down[0]), row(ln2_g[0]), row(ln2_b[0]))

    def layer(x_rows, mixer_state, conv_state, *, n_seq, seq_len, t_mix, mix_seq_blk, tm,
              ffn_seq_blk, ffn_t, shared_init):
        proj = _in_proj(x_rows, *ln_e, w_in_p, b_in_p, tm=tm)
        mix, c_new, n_new, m_new, s_new = _mixer(
            proj, *mixer_state, bf_row, ga, gb, lb, n_seq=n_seq, n_chunks=seq_len // t_mix,
            t_len=t_mix, n_seq_blk=mix_seq_blk, shared_init=shared_init)
        x1 = _out_proj(x_rows, mix, *ln_e, *out_p, tm=tm)
        y, conv_new = _ffn(x1, conv_state, *ffn_p, n_seq=n_seq, seq_len=seq_len,
                           n_seq_blk=ffn_seq_blk, t_len=ffn_t, shared_init=shared_init)
        return y, c_new, n_new, m_new, s_new, conv_new

    zero_state = (jnp.zeros((1, N_HEADS, D_HEAD, D_HEAD), F32), jnp.zeros((1, N_HEADS, D_HEAD), F32),
                  jnp.zeros((1, 1, N_HEADS), F32), jnp.zeros((1, N_HEADS, D_HEAD, D_HEAD), F32))
    _, c_m, n_m, m_m, s_m, conv_m = layer(
        meta_tokens.astype(F32), zero_state, jnp.zeros((1, CONV_W - 1, D_FF), F32),
        n_seq=1, seq_len=N_META, t_mix=N_META, mix_seq_blk=1, tm=N_META, ffn_seq_blk=1,
        ffn_t=N_META, shared_init=False)

    x1_p, c_p, n_p, m_p, s_p = _prompt_mixer(
        x_prompt.reshape(n_prompt * seq, D_MODEL), *ln_e, w_in_p, b_in_p, c_m, n_m, m_m, s_m,
        bf_row, ga, gb, lb, *out_p, n_seq=n_prompt, n_chunks=seq // PROMPT_CHUNK, t_len=PROMPT_CHUNK)
    y_p, conv_p = _ffn(x1_p, conv_m, *ffn_p, n_seq=n_prompt, seq_len=seq, n_seq_blk=1, t_len=512,
                       shared_init=True)

    sample_state = (state_mlstm_C[0].astype(F32), state_mlstm_n[0].astype(F32),
                    state_mlstm_m[0].astype(F32).reshape(n_sample, 1, N_HEADS),
                    state_hgrn_S[0].astype(F32))
    y_s, c_s, n_s, m_s, s_s, conv_s = layer(
        x_sample.reshape(n_sample * dec_seq, D_MODEL), sample_state, state_ffn_conv[0].astype(F32),
        n_seq=n_sample, seq_len=dec_seq, t_mix=dec_seq, mix_seq_blk=8, tm=256, ffn_seq_blk=32,
        ffn_t=dec_seq, shared_init=False)

    lead = lambda v: v[None]
    return (y_p.reshape(n_prompt, seq, D_MODEL), y_s.reshape(n_sample, dec_seq, D_MODEL),
            lead(c_p), lead(n_p), lead(m_p.reshape(n_prompt, N_HEADS)), lead(s_p), lead(conv_p),
            lead(c_s), lead(n_s), lead(m_s.reshape(n_sample, N_HEADS)), lead(s_s), lead(conv_s))
```

```python
import functools

import jax
import jax.numpy as jnp
from jax import lax
from jax.experimental import pallas as pl
from jax.experimental.pallas import tpu as pltpu

D_MODEL = 1024
N_META = 16
N_HEADS = 4
D_HEAD = 128
D_GROUP = N_HEADS * D_HEAD
D_FF = 2816
CONV_W = 3
DEPTH = 1
ALPHA = (2.0 * DEPTH) ** 0.25
LN_EPS = 1e-5
RMS_EPS = 1e-6
NEG_LOG2_E = -1.4426950408889634

LANES = 128
SUBLANES = 8
GATE_COL = 8 * D_GROUP
D_PROJ = GATE_COL + LANES
KB_GROUP, F_GROUP = 8, 9
D_ACT = D_PROJ + 2 * D_GROUP
IN_PROJ_STAGE_COLS = 256
SIDE_STAGES_PER_ROUND = 3
PROMPT_CHUNK = 128
VMEM_LIMIT = 56 * 1024 * 1024

F32 = jnp.float32
BF16 = jnp.bfloat16
HIGHEST = lax.Precision.HIGHEST
NT_DIMS = (((1,), (1,)), ((), ()))
TN_DIMS = (((0,), (0,)), ((), ()))


def _layer_norm(x, g, b):
    mu = jnp.mean(x, axis=-1, keepdims=True)
    xc = x - mu
    var = jnp.mean(xc * xc, axis=-1, keepdims=True)
    return xc * lax.rsqrt(var + LN_EPS) * g + b


def _exp_neg(x):
    return jnp.exp2(x * NEG_LOG2_E)


def _sigmoid(x):
    return 1.0 / (1.0 + _exp_neg(x))


def _resident(block_shape, index_map):
    return pl.BlockSpec(block_shape, index_map, pipeline_mode=pl.Buffered(1))


def _dot(a, b, precision=None):
    return jnp.dot(a, b, precision=precision, preferred_element_type=F32)


def _dot_nt(a, b, precision=None):
    return lax.dot_general(a, b, NT_DIMS, precision=precision, preferred_element_type=F32)


def _dot_tn(a, b):
    return lax.dot_general(a, b, TN_DIMS, preferred_element_type=F32)


def _in_proj_kernel(x_ref, g_ref, b_ref, w_ref, bias_ref, o_ref):
    xn = _layer_norm(x_ref[...], g_ref[...], b_ref[...])
    o_ref[...] = _dot(xn.astype(BF16), w_ref[...]) + bias_ref[...]


def _in_proj(x, ln_g, ln_b, w, bias, *, tm):
    n = x.shape[0]
    const = lambda i: (0, 0)
    return pl.pallas_call(
        _in_proj_kernel,
        grid=(n // tm,),
        in_specs=[
            pl.BlockSpec((tm, D_MODEL), lambda i: (i, 0)),
            pl.BlockSpec((1, D_MODEL), const),
            pl.BlockSpec((1, D_MODEL), const),
            pl.BlockSpec((D_MODEL, D_PROJ), const),
            pl.BlockSpec((1, D_PROJ), const),
        ],
        out_specs=pl.BlockSpec((tm, D_PROJ), lambda i: (i, 0)),
        out_shape=jax.ShapeDtypeStruct((n, D_PROJ), F32),
        compiler_params=pltpu.CompilerParams(
            dimension_semantics=("arbitrary",), vmem_limit_bytes=VMEM_LIMIT),
        name="in_proj",
    )(x, ln_g, ln_b, w, bias)


def _block_rows(x, level, t_len, row_in_block):
    size = 2 << level
    if size > SUBLANES:
        pieces = [jnp.broadcast_to(x[j * size + row_in_block:j * size + row_in_block + 1, :],
                                   (size, x.shape[1])) for j in range(t_len // size)]
        return pieces[0] if len(pieces) == 1 else jnp.concatenate(pieces, axis=0)
    x3 = x.reshape(t_len // SUBLANES, SUBLANES, x.shape[1])
    sub = lax.broadcasted_iota(jnp.int32, x3.shape, 1)
    out = None
    for j in range(SUBLANES // size):
        row = jnp.broadcast_to(x3[:, j * size + row_in_block:j * size + row_in_block + 1, :], x3.shape)
        out = row if out is None else jnp.where(sub >= j * size, row, out)
    return out.reshape(x.shape)


def _interleave_halves(lower, upper, level, t_len):
    half = 1 << level
    if half >= SUBLANES:
        pieces = []
        for j in range(t_len // (2 * half)):
            pieces.append(lower[2 * half * j:2 * half * j + half])
            pieces.append(upper[2 * half * j + half:2 * half * (j + 1)])
        return jnp.concatenate(pieces, axis=0)
    rows = lax.broadcasted_iota(jnp.int32, lower.shape, 0)
    return jnp.where((rows & half) != 0, upper, lower)


def _run_interleaved(gens, stages_per_round):
    results = [None] * len(gens)
    live = [True] * len(gens)
    while any(live):
        for g, steps in enumerate(stages_per_round):
            for _ in range(steps):
                if live[g]:
                    try:
                        next(gens[g])
                    except StopIteration as stop:
                        results[g], live[g] = stop.value, False
    return results


def _cumsum_rows(tril16, x):
    hi = x.astype(BF16)
    rest = x - hi.astype(F32)
    mid = rest.astype(BF16)
    lo = (rest - mid.astype(F32)).astype(BF16)
    return _dot(tril16, hi) + _dot(tril16, mid) + _dot(tril16, lo)


def _activate(group, x, lb=None):
    if group == 1:
        return {1: x * (D_HEAD ** -0.5)}
    if group in (3, 7):
        return {group: _sigmoid(x)}
    if group == 4:
        return {4: x * _sigmoid(x)}
    if group == 5:
        f = lb + (1.0 - lb) * _sigmoid(x)
        return {5: jnp.log(f), KB_GROUP: (1.0 - lb) / (1.0 + jnp.exp(x)), F_GROUP: f}
    return {group: x}


def _mlstm_units(*, q, k, v, gate, z, zt, head, c_old, n_old, m0, g_norm, causal, t_len):
    idx = range(len(q))
    q16 = [q[i].astype(BF16) for i in idx]
    k16 = [k[i].astype(BF16) for i in idx]
    qk = [_dot_nt(q16[i], k16[i]) for i in idx]
    qc = [_dot(q16[i], c_old[i].astype(BF16)) for i in idx]
    bs_row = [zt[i][N_HEADS + head[i]:N_HEADS + head[i] + 1, :] - zt[i][head[i]:head[i] + 1, :]
              for i in idx]
    yield
    b_col = [z[i][:, N_HEADS + head[i]:N_HEADS + head[i] + 1] for i in idx]
    i_col = [z[i][:, head[i]:head[i] + 1] for i in idx]
    d = [jnp.where(causal, b_col[i] - bs_row[i], -jnp.inf) for i in idx]
    m_t = [jnp.maximum(b_col[i] + m0[i], jnp.max(d[i], axis=1, keepdims=True)) for i in idx]
    dec = [jnp.exp(b_col[i] + m0[i] - m_t[i]) for i in idx]
    sw = [jnp.exp(d[i] - m_t[i]) * qk[i] for i in idx]
    yield
    swv = [_dot(sw[i].astype(BF16), v[i].astype(BF16)) for i in idx]
    last = slice(t_len - 1, t_len)
    w_last = [jnp.exp(b_col[i][last] - b_col[i] + i_col[i] - m_t[i][last]) for i in idx]
    kv = [_dot_tn(k16[i], (w_last[i] * v[i]).astype(BF16)) for i in idx]
    yield
    c_new =[dec[i][last] * c_old[i] + kv[i] for i in idx]
    n_new = [dec[i][last] * n_old[i] + jnp.sum(w_last[i] * k[i], axis=0, keepdims=True) for i in idx]
    m_new = [m_t[i][last] for i in idx]
    den = [dec[i] * jnp.sum(q[i] * n_old[i], axis=1, keepdims=True)
           + jnp.sum(sw[i], axis=1, keepdims=True) for i in idx]
    hid = [(dec[i] * qc[i] + swv[i]) / jnp.maximum(jnp.abs(den[i]), _exp_neg(m_t[i])) for i in idx]
    yield
    rms = [lax.rsqrt(jnp.mean(hid[i] * hid[i], axis=1, keepdims=True) + RMS_EPS) for i in idx]
    out = [gate[i] * (hid[i] * rms[i] * g_norm[i]) for i in idx]
    return out, c_new, n_new, m_new


def _hgrn_units(*, qb, log_f, f, kb, iv, gate, s_old_t, g_norm, tril, level_of, t_len):
    idx = range(len(qb))
    n_levels = t_len.bit_length() - 1
    a = [_cumsum_rows(tril, log_f[i]) for i in idx]
    yield
    iv16 = [iv[i].astype(BF16) for i in idx]
    diag = [_dot_nt(qb[i].astype(BF16), kb[i].astype(BF16)) for i in idx]
    scores = [jnp.where(level_of == -2, diag[i], 0.0) for i in idx]
    for level in range(n_levels):
        yield
        x16 = []
        for i in idx:
            base = _interleave_halves(kb[i], qb[i], level, t_len)
            if level == 0:
                x = base * _interleave_halves(jnp.ones_like(f[i]), f[i], 0, t_len)
            else:
                ref = _block_rows(a[i], level, t_len, (1 << level) - 1)
                x = base * _exp_neg(jnp.abs(a[i] - ref))
            x16.append(x.astype(BF16))
        part = [_dot_nt(x16[i], x16[i]) for i in idx]
        scores = [jnp.where(level_of == level, part[i], scores[i]) for i in idx]
    yield
    last = slice(t_len - 1, t_len)
    q_in = [(qb[i] * jnp.exp(a[i])).astype(BF16) for i in idx]
    k_out = [(kb[i] * jnp.exp(a[i][last] - a[i])).astype(BF16) for i in idx]
    inter = [_dot_nt(q_in[i], s_old_t[i].astype(BF16)) for i in idx]
    intra = [_dot(scores[i].astype(BF16), iv16[i]) for i in idx]
    kv = [_dot_tn(iv16[i], k_out[i]) for i in idx]
    yield
    s_new_t =[jnp.exp(a[i][last]) * s_old_t[i] + kv[i] for i in idx]
    o = [inter[i] + intra[i] for i in idx]
    rms = [lax.rsqrt(jnp.mean(o[i] * o[i], axis=1, keepdims=True) + RMS_EPS) for i in idx]
    out = [gate[i] * (o[i] * rms[i] * g_norm[i]) for i in idx]
    return out, s_new_t


def _load_state(c0_ref, n0_ref, m0_ref, s0_ref, c_ref, n_ref, m_ref, s_ref, n_seq_blk):
    c_ref[...] = c0_ref[...]
    n_ref[...] = n0_ref[...]
    m_ref[...] = m0_ref[...]
    for s in range(n_seq_blk):
        for h in range(N_HEADS):
            s_ref[s, h] = s0_ref[s, h].T


def _finish_state(s_ref, n_seq_blk):
    for s in range(n_seq_blk):
        for h in range(N_HEADS):
            s_ref[s, h] = s_ref[s, h].T


def _act_cols(group, h):
    base = group * D_GROUP if group < KB_GROUP else D_PROJ + (group - KB_GROUP) * D_GROUP
    return slice(base + h * D_HEAD, base + (h + 1) * D_HEAD)


def _mixer_body(p_ref, bf_ref, ga_ref, gb_ref, lb_ref, c_ref, n_ref, m_ref, s_ref,
                *, t_len, n_seq_blk, activated, side_stages=None):
    rows = lax.broadcasted_iota(jnp.int32, (t_len, t_len), 0)
    cols = lax.broadcasted_iota(jnp.int32, (t_len, t_len), 1)
    causal = cols <= rows
    tril = causal.astype(BF16)
    level_of = jnp.where(rows > cols, 31 - lax.clz(rows ^ cols), jnp.where(rows == cols, -2, -1))
    lane = lax.broadcasted_iota(jnp.int32, (t_len, LANES), 1)
    is_f = (lane >= N_HEADS) & (lane < 2 * N_HEADS)
    hd = lambda j, h: slice(j * D_GROUP + h * D_HEAD, j * D_GROUP + (h + 1) * D_HEAD)

    units = [(s, h) for s in range(n_seq_blk) for h in range(N_HEADS)]
    rs = lambda s: slice(s * t_len, (s + 1) * t_len)

    z_seq, zt_seq, n_seq, m_seq = [], [], [], []
    for s in range(n_seq_blk):
        gates = p_ref[rs(s), GATE_COL:GATE_COL + LANES]
        log_f = jnp.where(is_f, jax.nn.log_sigmoid(gates + bf_ref[...]), 0.0)
        cum_f = _cumsum_rows(tril, log_f)
        z_seq.append(jnp.where(is_f, cum_f, jnp.where(lane < N_HEADS, gates, 0.0)))
        zt_seq.append(z_seq[s].T)
        n_seq.append(n_ref[s])
        m_seq.append(m_ref[s])
    computed = {}

    def act(group):
        if activated:
            return [p_ref[rs(s), _act_cols(group, h)] for s, h in units]
        source = 5 if group >= KB_GROUP else group
        for s, h in units:
            if (source, s, h) not in computed:
                computed[source, s, h] = _activate(source, p_ref[rs(s), hd(source, h)],
                                                   lb_ref[:, hd(0, h)])
        return [computed[source, s, h][group] for s, h in units]

    mlstm = _mlstm_units(
        q=act(0), k=act(1), v=act(2), gate=act(3),
        z=[z_seq[s] for s, h in units], zt=[zt_seq[s] for s, h in units], head=[h for s, h in units],
        c_old=[c_ref[s, h] for s, h in units], n_old=[n_seq[s][h:h + 1, :] for s, h in units],
        m0=[m_seq[s][:, h:h + 1] for s, h in units], g_norm=[ga_ref[:, hd(0, h)] for s, h in units],
        causal=causal, t_len=t_len)
    hgrn = _hgrn_units(
        qb=act(4), log_f=act(5), f=act(F_GROUP), kb=act(KB_GROUP), iv=act(6), gate=act(7),
        s_old_t=[s_ref[s, h] for s, h in units],
        g_norm=[gb_ref[:, hd(0, h)] for s, h in units], tril=tril, level_of=level_of, t_len=t_len)
    gens, per_round = [mlstm, hgrn], [1, 2]
    if side_stages is not None:
        gens, per_round = gens + [side_stages], per_round + [SIDE_STAGES_PER_ROUND]
    results = _run_interleaved(gens, per_round)
    (outs, c_new, n_new, m_new), (outs_b, s_new_t) = results[:2]

    for i, (s, h) in enumerate(units):
        c_ref[s, h] = c_new[i]
        s_ref[s, h] = s_new_t[i]
    head_lane = lax.broadcasted_iota(jnp.int32, (1, N_HEADS), 1)
    for s in range(n_seq_blk):
        n_ref[s] = jnp.concatenate(n_new[s * N_HEADS:(s + 1) * N_HEADS], axis=0)
        m_row = m_seq[s]
        for h in range(N_HEADS):
            m_row = jnp.where(head_lane == h, m_new[s * N_HEADS + h], m_row)
        m_ref[s] = m_row
    return outs, outs_b


def _mixer_kernel(p_ref, c0_ref, n0_ref, m0_ref, s0_ref, bf_ref, ga_ref, gb_ref, lb_ref,
                  mix_ref, c_ref, n_ref, m_ref, s_ref, *, t_len, n_chunks, n_seq_blk):
    chunk = pl.program_id(1)

    @pl.when(chunk == 0)
    def _():
        _load_state(c0_ref, n0_ref, m0_ref, s0_ref, c_ref, n_ref, m_ref, s_ref, n_seq_blk)

    outs_a, outs_b = _mixer_body(p_ref, bf_ref, ga_ref, gb_ref, lb_ref, c_ref, n_ref, m_ref, s_ref,
                                 t_len=t_len, n_seq_blk=n_seq_blk, activated=False)
    for i in range(n_seq_blk * N_HEADS):
        s, h = divmod(i, N_HEADS)
        rows = slice(s * t_len, (s + 1) * t_len)
        mix_ref[rows, h * D_HEAD:(h + 1) * D_HEAD] = outs_a[i].astype(mix_ref.dtype)
        mix_ref[rows, D_GROUP + h * D_HEAD:D_GROUP + (h + 1) * D_HEAD] = outs_b[i].astype(mix_ref.dtype)

    @pl.when(chunk == n_chunks - 1)
    def _():
        _finish_state(s_ref, n_seq_blk)


def _mixer(proj, c0, n0, m0, s0, bf_row, ga, gb, lb, *, n_seq, n_chunks, t_len, n_seq_blk,
           shared_init):
    assert not shared_init or n_seq_blk == 1
    assert n_chunks == 1 or n_seq_blk == 1
    nb = n_seq_blk
    init = (lambda b, c: (0, 0, 0, 0)) if shared_init else (lambda b, c: (b, 0, 0, 0))
    init3 = (lambda b, c: (0, 0, 0)) if shared_init else (lambda b, c: (b, 0, 0))
    const = lambda b, c: (0, 0)
    state4 = pl.BlockSpec((nb, N_HEADS, D_HEAD, D_HEAD), lambda b, c: (b, 0, 0, 0))
    return pl.pallas_call(
        functools.partial(_mixer_kernel, t_len=t_len, n_chunks=n_chunks, n_seq_blk=nb),
        grid=(n_seq // nb, n_chunks),
        in_specs=[
            pl.BlockSpec((nb * t_len, D_PROJ), lambda b, c: (b * n_chunks + c, 0)),
            pl.BlockSpec((nb, N_HEADS, D_HEAD, D_HEAD), init),
            pl.BlockSpec((nb, N_HEADS, D_HEAD), init3),
            pl.BlockSpec((nb, 1, N_HEADS), init3),
            pl.BlockSpec((nb, N_HEADS, D_HEAD, D_HEAD), init),
            pl.BlockSpec((1, LANES), const),
            pl.BlockSpec((1, D_GROUP), const),
            pl.BlockSpec((1, D_GROUP), const),
            pl.BlockSpec((1, D_GROUP), const),
        ],
        out_specs=[
            pl.BlockSpec((nb * t_len, D_MODEL), lambda b, c: (b * n_chunks + c, 0)),
            state4,
            pl.BlockSpec((nb, N_HEADS, D_HEAD), lambda b, c: (b, 0, 0)),
            pl.BlockSpec((nb, 1, N_HEADS), lambda b, c: (b, 0, 0)),
            state4,
        ],
        out_shape=[
            jax.ShapeDtypeStruct((n_seq * n_chunks * t_len, D_MODEL), BF16),
            jax.ShapeDtypeStruct((n_seq, N_HEADS, D_HEAD, D_HEAD), F32),
            jax.ShapeDtypeStruct((n_seq, N_HEADS, D_HEAD), F32),
            jax.ShapeDtypeStruct((n_seq, 1, N_HEADS), F32),
            jax.ShapeDtypeStruct((n_seq, N_HEADS, D_HEAD, D_HEAD), F32),
        ],
        compiler_params=pltpu.CompilerParams(
            dimension_semantics=("arbitrary", "arbitrary"), vmem_limit_bytes=VMEM_LIMIT),
        name=f"mixer_t{t_len}",
    )(proj, c0, n0, m0, s0, bf_row, ga, gb, lb)


def _in_proj_stages(x_ref, g_ref, b_ref, w_ref, bias_ref, lb_ref, act_ref, xn_ref):
    assert D_GROUP % IN_PROJ_STAGE_COLS == 0
    xn = _layer_norm(x_ref[...], g_ref[...], b_ref[...])
    xn_ref[...] = xn
    x16 = xn.astype(BF16)
    for lo in range(0, D_PROJ, IN_PROJ_STAGE_COLS):
        hi = min(lo + IN_PROJ_STAGE_COLS, D_PROJ)
        yield
        block = _dot(x16, w_ref[:, lo:hi]) + bias_ref[:, lo:hi]
        group, off = divmod(lo, D_GROUP)
        if group >= KB_GROUP:
            act_ref[:, lo:hi] = block
            continue
        for dst, val in _activate(group, block, lb_ref[:, off:off + hi - lo]).items():
            base = _act_cols(dst, 0).start + off
            act_ref[:, base:base + hi - lo] = val


def _prompt_kernel(x0_ref, xnext_ref, ge_ref, be_ref, win_ref, bin_ref, c0_ref, n0_ref, m0_ref, s0_ref,
                   bf_ref, ga_ref, gb_ref, lb_ref, wout_ref, bout_ref, g1_ref, b1_ref,
                   x1_ref, c_ref, n_ref, m_ref, s_ref, proj_scr, xn_scr, proj_alt, xn_alt,
                   *, t_len, n_chunks):
    chunk = pl.program_id(1)
    step = pl.program_id(0) * n_chunks + chunk

    @pl.when(step == 0)
    def _():
        first = _in_proj_stages(x0_ref, ge_ref, be_ref, win_ref, bin_ref, lb_ref, proj_scr, xn_scr)
        _run_interleaved([first], [1])

    @pl.when(chunk == 0)
    def _():
        _load_state(c0_ref, n0_ref, m0_ref, s0_ref, c_ref, n_ref, m_ref, s_ref, 1)

    def tile(proj_cur, xn_cur, proj_next, xn_next):
        next_proj = _in_proj_stages(xnext_ref, ge_ref, be_ref, win_ref, bin_ref, lb_ref,
                                    proj_next, xn_next)
        outs_a, outs_b = _mixer_body(
            proj_cur, bf_ref, ga_ref, gb_ref, lb_ref, c_ref, n_ref, m_ref, s_ref,
            t_len=t_len, n_seq_blk=1, activated=True, side_stages=next_proj)
        mix = jnp.concatenate([o.astype(BF16) for o in outs_a + outs_b], axis=1)
        y = _dot(mix, wout_ref[...]) + bout_ref[...]
        x1_ref[...] = _layer_norm(ALPHA * xn_cur[...] + y, g1_ref[...], b1_ref[...])

    @pl.when(step % 2 == 0)
    def _():
        tile(proj_scr, xn_scr, proj_alt, xn_alt)

    @pl.when(step % 2 == 1)
    def _():
        tile(proj_alt, xn_alt, proj_scr, xn_scr)

    @pl.when(chunk == n_chunks - 1)
    def _():
        _finish_state(s_ref, 1)


def _prompt_mixer(x, ln_e_g, ln_e_b, w_in, b_in, c0, n0, m0, s0, bf_row, ga, gb, lb,
                  w_out, b_out, ln_g, ln_b, *, n_seq, n_chunks, t_len):
    n_tiles = n_seq * n_chunks
    const = lambda b, c: (0, 0)
    init4 = lambda b, c: (0, 0, 0, 0)
    init3 = lambda b, c: (0, 0, 0)
    vec = pl.BlockSpec((1, D_MODEL), const)
    grp = pl.BlockSpec((1, D_GROUP), const)
    state4 = pl.BlockSpec((1, N_HEADS, D_HEAD, D_HEAD), lambda b, c: (b, 0, 0, 0))
    return pl.pallas_call(
        functools.partial(_prompt_kernel, t_len=t_len, n_chunks=n_chunks),
        grid=(n_seq, n_chunks),
        in_specs=[
            pl.BlockSpec((t_len, D_MODEL), const),
            pl.BlockSpec((t_len, D_MODEL),
                         lambda b, c: (jnp.minimum(b * n_chunks + c + 1, n_tiles - 1), 0)),
            vec, vec,
            pl.BlockSpec((D_MODEL, D_PROJ), const),
            pl.BlockSpec((1, D_PROJ), const),
            pl.BlockSpec((1, N_HEADS, D_HEAD, D_HEAD), init4),
            pl.BlockSpec((1, N_HEADS, D_HEAD), init3),
            pl.BlockSpec((1, 1, N_HEADS), init3),
            pl.BlockSpec((1, N_HEADS, D_HEAD, D_HEAD), init4),
            pl.BlockSpec((1, LANES), const),
            grp, grp, grp,
            pl.BlockSpec((D_MODEL, D_MODEL), const),
            vec, vec, vec,
        ],
        out_specs=[
            pl.BlockSpec((t_len, D_MODEL), lambda b, c: (b * n_chunks + c, 0)),
            state4,
            pl.BlockSpec((1, N_HEADS, D_HEAD), lambda b, c: (b, 0, 0)),
            pl.BlockSpec((1, 1, N_HEADS), lambda b, c: (b, 0, 0)),
            state4,
        ],
        out_shape=[
            jax.ShapeDtypeStruct((n_tiles * t_len, D_MODEL), F32),
            jax.ShapeDtypeStruct((n_seq, N_HEADS, D_HEAD, D_HEAD), F32),
            jax.ShapeDtypeStruct((n_seq, N_HEADS, D_HEAD), F32),
            jax.ShapeDtypeStruct((n_seq, 1, N_HEADS), F32),
            jax.ShapeDtypeStruct((n_seq, N_HEADS, D_HEAD, D_HEAD), F32),
        ],
        scratch_shapes=[pltpu.VMEM((t_len, D_ACT), F32), pltpu.VMEM((t_len, D_MODEL), F32),
                        pltpu.VMEM((t_len, D_ACT), F32), pltpu.VMEM((t_len, D_MODEL), F32)],
        compiler_params=pltpu.CompilerParams(
            dimension_semantics=("arbitrary", "arbitrary"), vmem_limit_bytes=VMEM_LIMIT),
        name="prompt_mixer",
    )(x, x, ln_e_g, ln_e_b, w_in, b_in, c0, n0, m0, s0, bf_row, ga, gb, lb, w_out, b_out, ln_g, ln_b)


def _out_proj_kernel(x_ref, mix_ref, ge_ref, be_ref, w_ref, bias_ref, g_ref, b_ref, o_ref):
    xn = _layer_norm(x_ref[...], ge_ref[...], be_ref[...])
    y = _dot(mix_ref[...], w_ref[...]) + bias_ref[...]
    o_ref[...] = _layer_norm(ALPHA * xn + y, g_ref[...], b_ref[...])


def _out_proj(x, mix, ln_e_g, ln_e_b, w, bias, ln_g, ln_b, *, tm):
    n = x.shape[0]
    const = lambda i: (0, 0)
    row = pl.BlockSpec((tm, D_MODEL), lambda i: (i, 0))
    vec = pl.BlockSpec((1, D_MODEL), const)
    return pl.pallas_call(
        _out_proj_kernel,
        grid=(n // tm,),
        in_specs=[row, row, vec, vec, pl.BlockSpec((D_MODEL, D_MODEL), const), vec, vec, vec],
        out_specs=row,
        out_shape=jax.ShapeDtypeStruct((n, D_MODEL), F32),
        compiler_params=pltpu.CompilerParams(
            dimension_semantics=("arbitrary",), vmem_limit_bytes=VMEM_LIMIT),
        name="out_proj",
    )(x, mix, ln_e_g, ln_e_b, w, bias, ln_g, ln_b)


def _ffn_kernel(x_ref, cs_ref, wu_ref, bu_ref, wc_ref, bc_ref, wd_ref, bd_ref, g_ref, b_ref,
                y_ref, nc_ref, full_ref, *, n_seq_blk, t_len):
    hist = SUBLANES - (CONV_W - 1)

    @pl.when(pl.program_id(1) == 0)
    def _():
        full_ref[:, hist:SUBLANES, :] = cs_ref[...]

    x = x_ref[...]
    up = _dot(x.astype(BF16), wu_ref[...]) + bu_ref[...]
    u = up[:, :D_FF].reshape(n_seq_blk, t_len, D_FF)
    gate = up[:, D_FF:].reshape(n_seq_blk, t_len, D_FF)
    full_ref[:, SUBLANES:SUBLANES + t_len, :] = u
    conv = bc_ref[...] + u * wc_ref[CONV_W - 1:CONV_W, :]
    for j in range(CONV_W - 1):
        conv = conv + full_ref[:, hist + j:hist + j + t_len, :] * wc_ref[j:j + 1, :]
    last = full_ref[:, hist + t_len:SUBLANES + t_len, :]
    nc_ref[...] = last
    full_ref[:, hist:SUBLANES, :] = last
    act = (conv * _sigmoid(conv) * gate).reshape(n_seq_blk * t_len, D_FF)
    ffn = _dot(act.astype(BF16), wd_ref[...]) + bd_ref[...]
    y_ref[...] = _layer_norm(ALPHA * x + ffn, g_ref[...], b_ref[...])


def _ffn(x, conv_state, w_up, b_up, w_conv, b_conv, w_down, b_down, ln_g, ln_b,
         *, n_seq, seq_len, n_seq_blk, t_len, shared_init):
    n_t = seq_len // t_len
    rows = n_seq_blk * t_len
    const = lambda s, t: (0, 0)
    cs_map = (lambda s, t: (0, 0, 0)) if shared_init else (lambda s, t: (s, 0, 0))
    row = pl.BlockSpec((rows, D_MODEL), lambda s, t: (s * n_t + t, 0))
    vec = pl.BlockSpec((1, D_MODEL), const)
    return pl.pallas_call(
        functools.partial(_ffn_kernel, n_seq_blk=n_seq_blk, t_len=t_len),
        grid=(n_seq // n_seq_blk, n_t),
        in_specs=[
            row,
            pl.BlockSpec((n_seq_blk, CONV_W - 1, D_FF), cs_map),
            _resident((D_MODEL, 2 * D_FF), const),
            pl.BlockSpec((1, 2 * D_FF), const),
            pl.BlockSpec((CONV_W, D_FF), const),
            pl.BlockSpec((1, D_FF), const),
            _resident((D_FF, D_MODEL), const),
            vec, vec, vec,
        ],
        out_specs=[row, pl.BlockSpec((n_seq_blk, CONV_W - 1, D_FF), lambda s, t: (s, 0, 0))],
        out_shape=[
            jax.ShapeDtypeStruct((n_seq * seq_len, D_MODEL), F32),
            jax.ShapeDtypeStruct((n_seq, CONV_W - 1, D_FF), F32),
        ],
        scratch_shapes=[pltpu.VMEM((n_seq_blk, SUBLANES + t_len, D_FF), F32)],
        compiler_params=pltpu.CompilerParams(
            dimension_semantics=("arbitrary", "arbitrary"), vmem_limit_bytes=VMEM_LIMIT),
        name=f"ffn_t{t_len}",
    )(x, conv_state, w_up, b_up, w_conv, b_conv, w_down, b_down, ln_g, ln_b)


def kernel(x_prompt, x_sample, state_mlstm_C, state_mlstm_n, state_mlstm_m, state_hgrn_S, state_ffn_conv, meta_tokens, ln_emb_g, ln_emb_b, w_in, b_in, b_fgate_a, g_norm_a, g_norm_b, hgrn_lb_logits, w_out, b_out, ln1_g, ln1_b, w_up, b_up, w_conv, b_conv, w_down, b_down, ln2_g, ln2_b):
    assert w_in.shape[0] == DEPTH == 1
    n_prompt, seq, _ = x_prompt.shape
    n_sample, dec_seq, _ = x_sample.shape
    row = lambda v: v.reshape(1, -1).astype(F32)

    gate0 = 4 * D_GROUP
    gate1 = gate0 + 2 * N_HEADS
    pad = D_PROJ - w_in.shape[2]
    w_in16 = w_in[0].astype(BF16)
    w_in_p = jnp.concatenate(
        [w_in16[:, :gate0], w_in16[:, gate1:], w_in16[:, gate0:gate1],
         jnp.zeros((D_MODEL, pad), BF16)], axis=1)
    b_in_p = jnp.concatenate(
        [b_in[0][:gate0], b_in[0][gate1:], b_in[0][gate0:gate1], jnp.zeros((pad,), b_in.dtype)]
    ).reshape(1, D_PROJ).astype(F32)
    bf_row = jnp.zeros((1, LANES), F32).at[0, N_HEADS:2 * N_HEADS].set(b_fgate_a[0].astype(F32))
    lb = jnp.cumsum(jax.nn.softmax(hgrn_lb_logits.astype(F32), axis=0), axis=0)[0].reshape(1, D_GROUP)
    ga, gb = row(g_norm_a[0]), row(g_norm_b[0])
    ln_e = (row(ln_emb_g), row(ln_emb_b))
    out_p = (w_out[0].astype(BF16), row(b_out[0]), row(ln1_g[0]), row(ln1_b[0]))
    ffn_p = (w_up[0].astype(BF16), row(b_up[0]), w_conv[0].astype(F32), row(b_conv[0]),
             w_down[0].astype(BF16), row(b_down[0]), row(ln2_g[0]), row(ln2_b[0]))

    def layer(x_rows, mixer_state, conv_state, *, n_seq, seq_len, t_mix, mix_seq_blk, tm,
              ffn_seq_blk, ffn_t, shared_init):
        proj = _in_proj(x_rows, *ln_e, w_in_p, b_in_p, tm=tm)
        mix, c_new, n_new, m_new, s_new = _mixer(
            proj, *mixer_state, bf_row, ga, gb, lb, n_seq=n_seq, n_chunks=seq_len // t_mix,
            t_len=t_mix, n_seq_blk=mix_seq_blk, shared_init=shared_init)
        x1 = _out_proj(x_rows, mix, *ln_e, *out_p, tm=tm)
        y, conv_new = _ffn(x1, conv_state, *ffn_p, n_seq=n_seq, seq_len=seq_len,
                           n_seq_blk=ffn_seq_blk, t_len=ffn_t, shared_init=shared_init)
        return y, c_new, n_new, m_new, s_new, conv_new

    zero_state = (jnp.zeros((1, N_HEADS, D_HEAD, D_HEAD), F32), jnp.zeros((1, N_HEADS, D_HEAD), F32),
                  jnp.zeros((1, 1, N_HEADS), F32), jnp.zeros((1, N_HEADS, D_HEAD, D_HEAD), F32))
    _, c_m, n_m, m_m, s_m, conv_m = layer(
        meta_tokens.astype(F32), zero_state, jnp.zeros((1, CONV_W - 1, D_FF), F32),
        n_seq=1, seq_len=N_META, t_mix=N_META, mix_seq_blk=1, tm=N_META, ffn_seq_blk=1,
        ffn_t=N_META, shared_init=False)

    x1_p, c_p, n_p, m_p, s_p = _prompt_mixer(
        x_prompt.reshape(n_prompt * seq, D_MODEL), *ln_e, w_in_p, b_in_p, c_m, n_m, m_m, s_m,
        bf_row, ga, gb, lb, *out_p, n_seq=n_prompt, n_chunks=seq // PROMPT_CHUNK, t_len=PROMPT_CHUNK)
    y_p, conv_p = _ffn(x1_p, conv_m, *ffn_p, n_seq=n_prompt, seq_len=seq, n_seq_blk=1, t_len=512,
                       shared_init=True)

    sample_state = (state_mlstm_C[0].astype(F32), state_mlstm_n[0].astype(F32),
                    state_mlstm_m[0].astype(F32).reshape(n_sample, 1, N_HEADS),
                    state_hgrn_S[0].astype(F32))
    y_s, c_s, n_s, m_s, s_s, conv_s = layer(
        x_sample.reshape(n_sample * dec_seq, D_MODEL), sample_state, state_ffn_conv[0].astype(F32),
        n_seq=n_sample, seq_len=dec_seq, t_mix=dec_seq, mix_seq_blk=8, tm=256, ffn_seq_blk=32,
        ffn_t=dec_seq, shared_init=False)

    lead = lambda v: v[None]
    return (y_p.reshape(n_prompt, seq, D_MODEL), y_s.reshape(n_sample, dec_seq, D_MODEL),
            lead(c_p), lead(n_p), lead(m_p.reshape(n_prompt, N_HEADS)), lead(s_p), lead(conv_p),
            lead(c_s), lead(n_s), lead(m_s.reshape(n_sample, N_HEADS)), lead(s_s), lead(conv_s))
```

```python
import functools

import jax
import jax.numpy as jnp
from jax import lax
from jax.experimental import pallas as pl
from jax.experimental.pallas import tpu as pltpu

D_MODEL = 1024
N_META = 16
N_HEADS = 4
D_HEAD = 128
D_GROUP = N_HEADS * D_HEAD
D_FF = 2816
CONV_W = 3
DEPTH = 1
ALPHA = (2.0 * DEPTH) ** 0.25
LN_EPS = 1e-5
RMS_EPS = 1e-6
NEG_LOG2_E = -1.4426950408889634

LANES = 128
SUBLANES = 8
GATE_COL = 8 * D_GROUP
D_PROJ = GATE_COL + LANES
KB_GROUP, F_GROUP = 8, 9
D_ACT = D_PROJ + 2 * D_GROUP
IN_PROJ_STAGE_COLS = 256
SIDE_STAGES_PER_ROUND = 3
PROMPT_SEQS_PER_STEP = 2
PROMPT_CHUNK = 128
VMEM_LIMIT = 56 * 1024 * 1024

F32 = jnp.float32
BF16 = jnp.bfloat16
HIGHEST = lax.Precision.HIGHEST
NT_DIMS = (((1,), (1,)), ((), ()))
TN_DIMS = (((0,), (0,)), ((), ()))


def _layer_norm(x, g, b):
    mu = jnp.mean(x, axis=-1, keepdims=True)
    xc = x - mu
    var = jnp.mean(xc * xc, axis=-1, keepdims=True)
    return xc * lax.rsqrt(var + LN_EPS) * g + b


def _exp_neg(x):
    return jnp.exp2(x * NEG_LOG2_E)


def _sigmoid(x):
    return 1.0 / (1.0 + _exp_neg(x))


def _resident(block_shape, index_map):
    return pl.BlockSpec(block_shape, index_map, pipeline_mode=pl.Buffered(1))


def _dot(a, b, precision=None):
    return jnp.dot(a, b, precision=precision, preferred_element_type=F32)


def _dot_nt(a, b, precision=None):
    return lax.dot_general(a, b, NT_DIMS, precision=precision, preferred_element_type=F32)


def _dot_tn(a, b):
    return lax.dot_general(a, b, TN_DIMS, preferred_element_type=F32)


def _in_proj_kernel(x_ref, g_ref, b_ref, w_ref, bias_ref, o_ref):
    xn = _layer_norm(x_ref[...], g_ref[...], b_ref[...])
    o_ref[...] = _dot(xn.astype(BF16), w_ref[...]) + bias_ref[...]


def _in_proj(x, ln_g, ln_b, w, bias, *, tm):
    n = x.shape[0]
    const = lambda i: (0, 0)
    return pl.pallas_call(
        _in_proj_kernel,
        grid=(n // tm,),
        in_specs=[
            pl.BlockSpec((tm, D_MODEL), lambda i: (i, 0)),
            pl.BlockSpec((1, D_MODEL), const),
            pl.BlockSpec((1, D_MODEL), const),
            pl.BlockSpec((D_MODEL, D_PROJ), const),
            pl.BlockSpec((1, D_PROJ), const),
        ],
        out_specs=pl.BlockSpec((tm, D_PROJ), lambda i: (i, 0)),
        out_shape=jax.ShapeDtypeStruct((n, D_PROJ), F32),
        compiler_params=pltpu.CompilerParams(
            dimension_semantics=("arbitrary",), vmem_limit_bytes=VMEM_LIMIT),
        name="in_proj",
    )(x, ln_g, ln_b, w, bias)


def _block_rows(x, level, t_len, row_in_block):
    size = 2 << level
    if size > SUBLANES:
        pieces = [jnp.broadcast_to(x[j * size + row_in_block:j * size + row_in_block + 1, :],
                                   (size, x.shape[1])) for j in range(t_len // size)]
        return pieces[0] if len(pieces) == 1 else jnp.concatenate(pieces, axis=0)
    x3 = x.reshape(t_len // SUBLANES, SUBLANES, x.shape[1])
    sub = lax.broadcasted_iota(jnp.int32, x3.shape, 1)
    out = None
    for j in range(SUBLANES // size):
        row = jnp.broadcast_to(x3[:, j * size + row_in_block:j * size + row_in_block + 1, :], x3.shape)
        out = row if out is None else jnp.where(sub >= j * size, row, out)
    return out.reshape(x.shape)


def _interleave_halves(lower, upper, level, t_len):
    half = 1 << level
    if half >= SUBLANES:
        pieces = []
        for j in range(t_len // (2 * half)):
            pieces.append(lower[2 * half * j:2 * half * j + half])
            pieces.append(upper[2 * half * j + half:2 * half * (j + 1)])
        return jnp.concatenate(pieces, axis=0)
    rows = lax.broadcasted_iota(jnp.int32, lower.shape, 0)
    return jnp.where((rows & half) != 0, upper, lower)


def _run_interleaved(gens, stages_per_round):
    results = [None] * len(gens)
    live = [True] * len(gens)
    while any(live):
        for g, steps in enumerate(stages_per_round):
            for _ in range(steps):
                if live[g]:
                    try:
                        next(gens[g])
                    except StopIteration as stop:
                        results[g], live[g] = stop.value, False
    return results


def _cumsum_rows(tril16, x):
    hi = x.astype(BF16)
    rest = x - hi.astype(F32)
    mid = rest.astype(BF16)
    lo = (rest - mid.astype(F32)).astype(BF16)
    return _dot(tril16, hi) + _dot(tril16, mid) + _dot(tril16, lo)


def _activate(group, x, lb=None):
    if group == 1:
        return {1: x * (D_HEAD ** -0.5)}
    if group in (3, 7):
        return {group: _sigmoid(x)}
    if group == 4:
        return {4: x * _sigmoid(x)}
    if group == 5:
        f = lb + (1.0 - lb) * _sigmoid(x)
        return {5: jnp.log(f), KB_GROUP: (1.0 - lb) / (1.0 + jnp.exp(x)), F_GROUP: f}
    return {group: x}


def _mlstm_units(*, q, k, v, gate, z, zt, head, c_old, n_old, m0, g_norm, causal, t_len):
    idx = range(len(q))
    q16 = [q[i].astype(BF16) for i in idx]
    k16 = [k[i].astype(BF16) for i in idx]
    qk = [_dot_nt(q16[i], k16[i]) for i in idx]
    qc = [_dot(q16[i], c_old[i].astype(BF16)) for i in idx]
    bs_row = [zt[i][N_HEADS + head[i]:N_HEADS + head[i] + 1, :] - zt[i][head[i]:head[i] + 1, :]
              for i in idx]
    yield
    b_col = [z[i][:, N_HEADS + head[i]:N_HEADS + head[i] + 1] for i in idx]
    i_col = [z[i][:, head[i]:head[i] + 1] for i in idx]
    d = [jnp.where(causal, b_col[i] - bs_row[i], -jnp.inf) for i in idx]
    m_t = [jnp.maximum(b_col[i] + m0[i], jnp.max(d[i], axis=1, keepdims=True)) for i in idx]
    dec = [jnp.exp(b_col[i] + m0[i] - m_t[i]) for i in idx]
    sw = [jnp.exp(d[i] - m_t[i]) * qk[i] for i in idx]
    yield
    swv = [_dot(sw[i].astype(BF16), v[i].astype(BF16)) for i in idx]
    last = slice(t_len - 1, t_len)
    w_last = [jnp.exp(b_col[i][last] - b_col[i] + i_col[i] - m_t[i][last]) for i in idx]
    kv = [_dot_tn(k16[i], (w_last[i] * v[i]).astype(BF16)) for i in idx]
    yield
    c_new =[dec[i][last] * c_old[i] + kv[i] for i in idx]
    n_new = [dec[i][last] * n_old[i] + jnp.sum(w_last[i] * k[i], axis=0, keepdims=True) for i in idx]
    m_new = [m_t[i][last] for i in idx]
    den = [dec[i] * jnp.sum(q[i] * n_old[i], axis=1, keepdims=True)
           + jnp.sum(sw[i], axis=1, keepdims=True) for i in idx]
    hid = [(dec[i] * qc[i] + swv[i]) / jnp.maximum(jnp.abs(den[i]), _exp_neg(m_t[i])) for i in idx]
    yield
    rms = [lax.rsqrt(jnp.mean(hid[i] * hid[i], axis=1, keepdims=True) + RMS_EPS) for i in idx]
    out = [gate[i] * (hid[i] * rms[i] * g_norm[i]) for i in idx]
    return out, c_new, n_new, m_new


def _hgrn_units(*, qb, log_f, f, kb, iv, gate, s_old_t, g_norm, tril, level_of, t_len):
    idx = range(len(qb))
    n_levels = t_len.bit_length() - 1
    a = [_cumsum_rows(tril, log_f[i]) for i in idx]
    yield
    iv16 = [iv[i].astype(BF16) for i in idx]
    diag = [_dot_nt(qb[i].astype(BF16), kb[i].astype(BF16)) for i in idx]
    scores = [jnp.where(level_of == -2, diag[i], 0.0) for i in idx]
    for level in range(n_levels):
        yield
        x16 = []
        for i in idx:
            base = _interleave_halves(kb[i], qb[i], level, t_len)
            if level == 0:
                x = base * _interleave_halves(jnp.ones_like(f[i]), f[i], 0, t_len)
            else:
                ref = _block_rows(a[i], level, t_len, (1 << level) - 1)
                x = base * _exp_neg(jnp.abs(a[i] - ref))
            x16.append(x.astype(BF16))
        part = [_dot_nt(x16[i], x16[i]) for i in idx]
        scores = [jnp.where(level_of == level, part[i], scores[i]) for i in idx]
    yield
    last = slice(t_len - 1, t_len)
    q_in = [(qb[i] * jnp.exp(a[i])).astype(BF16) for i in idx]
    k_out = [(kb[i] * jnp.exp(a[i][last] - a[i])).astype(BF16) for i in idx]
    inter = [_dot_nt(q_in[i], s_old_t[i].astype(BF16)) for i in idx]
    intra = [_dot(scores[i].astype(BF16), iv16[i]) for i in idx]
    kv = [_dot_tn(iv16[i], k_out[i]) for i in idx]
    yield
    s_new_t =[jnp.exp(a[i][last]) * s_old_t[i] + kv[i] for i in idx]
    o = [inter[i] + intra[i] for i in idx]
    rms = [lax.rsqrt(jnp.mean(o[i] * o[i], axis=1, keepdims=True) + RMS_EPS) for i in idx]
    out = [gate[i] * (o[i] * rms[i] * g_norm[i]) for i in idx]
    return out, s_new_t


def _load_state(c0_ref, n0_ref, m0_ref, s0_ref, c_ref, n_ref, m_ref, s_ref, n_seq_blk):
    shared = c0_ref.shape[0] == 1 and n_seq_blk > 1
    for s in range(n_seq_blk):
        src = 0 if shared else s
        c_ref[s] = c0_ref[src]
        n_ref[s] = n0_ref[src]
        m_ref[s] = m0_ref[src]
        for h in range(N_HEADS):
            s_ref[s, h] = s0_ref[src, h].T


def _finish_state(s_ref, n_seq_blk):
    for s in range(n_seq_blk):
        for h in range(N_HEADS):
            s_ref[s, h] = s_ref[s, h].T


def _act_cols(group, h):
    base = group * D_GROUP if group < KB_GROUP else D_PROJ + (group - KB_GROUP) * D_GROUP
    return slice(base + h * D_HEAD, base + (h + 1) * D_HEAD)


def _mixer_body(p_ref, bf_ref, ga_ref, gb_ref, lb_ref, c_ref, n_ref, m_ref, s_ref,
                *, t_len, n_seq_blk, activated, side_stages=None):
    rows = lax.broadcasted_iota(jnp.int32, (t_len, t_len), 0)
    cols = lax.broadcasted_iota(jnp.int32, (t_len, t_len), 1)
    causal = cols <= rows
    tril = causal.astype(BF16)
    level_of = jnp.where(rows > cols, 31 - lax.clz(rows ^ cols), jnp.where(rows == cols, -2, -1))
    lane = lax.broadcasted_iota(jnp.int32, (t_len, LANES), 1)
    is_f = (lane >= N_HEADS) & (lane < 2 * N_HEADS)
    hd = lambda j, h: slice(j * D_GROUP + h * D_HEAD, j * D_GROUP + (h + 1) * D_HEAD)

    units = [(s, h) for s in range(n_seq_blk) for h in range(N_HEADS)]
    rs = lambda s: slice(s * t_len, (s + 1) * t_len)

    z_seq, zt_seq, n_seq, m_seq = [], [], [], []
    for s in range(n_seq_blk):
        gates = p_ref[rs(s), GATE_COL:GATE_COL + LANES]
        log_f = jnp.where(is_f, jax.nn.log_sigmoid(gates + bf_ref[...]), 0.0)
        cum_f = _cumsum_rows(tril, log_f)
        z_seq.append(jnp.where(is_f, cum_f, jnp.where(lane < N_HEADS, gates, 0.0)))
        zt_seq.append(z_seq[s].T)
        n_seq.append(n_ref[s])
        m_seq.append(m_ref[s])
    computed = {}

    def act(group):
        if activated:
            return [p_ref[rs(s), _act_cols(group, h)] for s, h in units]
        source = 5 if group >= KB_GROUP else group
        for s, h in units:
            if (source, s, h) not in computed:
                computed[source, s, h] = _activate(source, p_ref[rs(s), hd(source, h)],
                                                   lb_ref[:, hd(0, h)])
        return [computed[source, s, h][group] for s, h in units]

    mlstm = _mlstm_units(
        q=act(0), k=act(1), v=act(2), gate=act(3),
        z=[z_seq[s] for s, h in units], zt=[zt_seq[s] for s, h in units], head=[h for s, h in units],
        c_old=[c_ref[s, h] for s, h in units], n_old=[n_seq[s][h:h + 1, :] for s, h in units],
        m0=[m_seq[s][:, h:h + 1] for s, h in units], g_norm=[ga_ref[:, hd(0, h)] for s, h in units],
        causal=causal, t_len=t_len)
    hgrn = _hgrn_units(
        qb=act(4), log_f=act(5), f=act(F_GROUP), kb=act(KB_GROUP), iv=act(6), gate=act(7),
        s_old_t=[s_ref[s, h] for s, h in units],
        g_norm=[gb_ref[:, hd(0, h)] for s, h in units], tril=tril, level_of=level_of, t_len=t_len)
    gens, per_round = [mlstm, hgrn], [1, 2]
    if side_stages is not None:
        gens, per_round = gens + [side_stages], per_round + [SIDE_STAGES_PER_ROUND]
    results = _run_interleaved(gens, per_round)
    (outs, c_new, n_new, m_new), (outs_b, s_new_t) = results[:2]

    for i, (s, h) in enumerate(units):
        c_ref[s, h] = c_new[i]
        s_ref[s, h] = s_new_t[i]
    head_lane = lax.broadcasted_iota(jnp.int32, (1, N_HEADS), 1)
    for s in range(n_seq_blk):
        n_ref[s] = jnp.concatenate(n_new[s * N_HEADS:(s + 1) * N_HEADS], axis=0)
        m_row = m_seq[s]
        for h in range(N_HEADS):
            m_row = jnp.where(head_lane == h, m_new[s * N_HEADS + h], m_row)
        m_ref[s] = m_row
    return outs, outs_b


def _mixer_kernel(p_ref, c0_ref, n0_ref, m0_ref, s0_ref, bf_ref, ga_ref, gb_ref, lb_ref,
                  mix_ref, c_ref, n_ref, m_ref, s_ref, *, t_len, n_chunks, n_seq_blk):
    chunk = pl.program_id(1)

    @pl.when(chunk == 0)
    def _():
        _load_state(c0_ref, n0_ref, m0_ref, s0_ref, c_ref, n_ref, m_ref, s_ref, n_seq_blk)

    outs_a, outs_b = _mixer_body(p_ref, bf_ref, ga_ref, gb_ref, lb_ref, c_ref, n_ref, m_ref, s_ref,
                                 t_len=t_len, n_seq_blk=n_seq_blk, activated=False)
    for i in range(n_seq_blk * N_HEADS):
        s, h = divmod(i, N_HEADS)
        rows = slice(s * t_len, (s + 1) * t_len)
        mix_ref[rows, h * D_HEAD:(h + 1) * D_HEAD] = outs_a[i].astype(mix_ref.dtype)
        mix_ref[rows, D_GROUP + h * D_HEAD:D_GROUP + (h + 1) * D_HEAD] = outs_b[i].astype(mix_ref.dtype)

    @pl.when(chunk == n_chunks - 1)
    def _():
        _finish_state(s_ref, n_seq_blk)


def _mixer(proj, c0, n0, m0, s0, bf_row, ga, gb, lb, *, n_seq, n_chunks, t_len, n_seq_blk,
           shared_init):
    assert not shared_init or n_seq_blk == 1
    assert n_chunks == 1 or n_seq_blk == 1
    nb = n_seq_blk
    init = (lambda b, c: (0, 0, 0, 0)) if shared_init else (lambda b, c: (b, 0, 0, 0))
    init3 = (lambda b, c: (0, 0, 0)) if shared_init else (lambda b, c: (b, 0, 0))
    const = lambda b, c: (0, 0)
    state4 = pl.BlockSpec((nb, N_HEADS, D_HEAD, D_HEAD), lambda b, c: (b, 0, 0, 0))
    return pl.pallas_call(
        functools.partial(_mixer_kernel, t_len=t_len, n_chunks=n_chunks, n_seq_blk=nb),
        grid=(n_seq // nb, n_chunks),
        in_specs=[
            pl.BlockSpec((nb * t_len, D_PROJ), lambda b, c: (b * n_chunks + c, 0)),
            pl.BlockSpec((nb, N_HEADS, D_HEAD, D_HEAD), init),
            pl.BlockSpec((nb, N_HEADS, D_HEAD), init3),
            pl.BlockSpec((nb, 1, N_HEADS), init3),
            pl.BlockSpec((nb, N_HEADS, D_HEAD, D_HEAD), init),
            pl.BlockSpec((1, LANES), const),
            pl.BlockSpec((1, D_GROUP), const),
            pl.BlockSpec((1, D_GROUP), const),
            pl.BlockSpec((1, D_GROUP), const),
        ],
        out_specs=[
            pl.BlockSpec((nb * t_len, D_MODEL), lambda b, c: (b * n_chunks + c, 0)),
            state4,
            pl.BlockSpec((nb, N_HEADS, D_HEAD), lambda b, c: (b, 0, 0)),
            pl.BlockSpec((nb, 1, N_HEADS), lambda b, c: (b, 0, 0)),
            state4,
        ],
        out_shape=[
            jax.ShapeDtypeStruct((n_seq * n_chunks * t_len, D_MODEL), BF16),
            jax.ShapeDtypeStruct((n_seq, N_HEADS, D_HEAD, D_HEAD), F32),
            jax.ShapeDtypeStruct((n_seq, N_HEADS, D_HEAD), F32),
            jax.ShapeDtypeStruct((n_seq, 1, N_HEADS), F32),
            jax.ShapeDtypeStruct((n_seq, N_HEADS, D_HEAD, D_HEAD), F32),
        ],
        compiler_params=pltpu.CompilerParams(
            dimension_semantics=("arbitrary", "arbitrary"), vmem_limit_bytes=VMEM_LIMIT),
        name=f"mixer_t{t_len}",
    )(proj, c0, n0, m0, s0, bf_row, ga, gb, lb)


def _in_proj_stages(x_ref, g_ref, b_ref, w_ref, bias_ref, lb_ref, act_ref, xn_ref):
    assert D_GROUP % IN_PROJ_STAGE_COLS == 0
    xn = _layer_norm(x_ref[...].reshape(xn_ref.shape), g_ref[...], b_ref[...])
    xn_ref[...] = xn
    x16 = xn.astype(BF16)
    for lo in range(0, D_PROJ, IN_PROJ_STAGE_COLS):
        hi = min(lo + IN_PROJ_STAGE_COLS, D_PROJ)
        yield
        block = _dot(x16, w_ref[:, lo:hi]) + bias_ref[:, lo:hi]
        group, off = divmod(lo, D_GROUP)
        if group >= KB_GROUP:
            act_ref[:, lo:hi] = block
            continue
        for dst, val in _activate(group, block, lb_ref[:, off:off + hi - lo]).items():
            base = _act_cols(dst, 0).start + off
            act_ref[:, base:base + hi - lo] = val


def _prompt_kernel(x0_ref, xnext_ref, ge_ref, be_ref, win_ref, bin_ref, c0_ref, n0_ref, m0_ref, s0_ref,
                   bf_ref, ga_ref, gb_ref, lb_ref, wout_ref, bout_ref, g1_ref, b1_ref,
                   x1_ref, c_ref, n_ref, m_ref, s_ref, proj_scr, xn_scr, proj_alt, xn_alt,
                   *, t_len, n_chunks, n_seq_blk):
    chunk = pl.program_id(1)
    step = pl.program_id(0) * n_chunks + chunk

    @pl.when(step == 0)
    def _():
        first = _in_proj_stages(x0_ref, ge_ref, be_ref, win_ref, bin_ref, lb_ref, proj_scr, xn_scr)
        _run_interleaved([first], [1])

    @pl.when(chunk == 0)
    def _():
        _load_state(c0_ref, n0_ref, m0_ref, s0_ref, c_ref, n_ref, m_ref, s_ref, n_seq_blk)

    def tile(proj_cur, xn_cur, proj_next, xn_next):
        next_proj = _in_proj_stages(xnext_ref, ge_ref, be_ref, win_ref, bin_ref, lb_ref,
                                    proj_next, xn_next)
        outs_a, outs_b = _mixer_body(
            proj_cur, bf_ref, ga_ref, gb_ref, lb_ref, c_ref, n_ref, m_ref, s_ref,
            t_len=t_len, n_seq_blk=n_seq_blk, activated=True, side_stages=next_proj)
        mix = jnp.concatenate(
            [jnp.concatenate([o.astype(BF16) for o in outs_a[s * N_HEADS:(s + 1) * N_HEADS]
                              + outs_b[s * N_HEADS:(s + 1) * N_HEADS]], axis=1)
             for s in range(n_seq_blk)], axis=0)
        y = _dot(mix, wout_ref[...]) + bout_ref[...]
        x1 = _layer_norm(ALPHA * xn_cur[...] + y, g1_ref[...], b1_ref[...])
        x1_ref[...] = x1.reshape(x1_ref.shape)

    @pl.when(step % 2 == 0)
    def _():
        tile(proj_scr, xn_scr, proj_alt, xn_alt)

    @pl.when(step % 2 == 1)
    def _():
        tile(proj_alt, xn_alt, proj_scr, xn_scr)

    @pl.when(chunk == n_chunks - 1)
    def _():
        _finish_state(s_ref, n_seq_blk)


def _prompt_mixer(x, ln_e_g, ln_e_b, w_in, b_in, c0, n0, m0, s0, bf_row, ga, gb, lb,
                  w_out, b_out, ln_g, ln_b, *, n_chunks, t_len, n_seq_blk):
    n_seq = x.shape[0]
    nb = n_seq_blk
    n_tiles = (n_seq // nb) * n_chunks
    const = lambda b, c: (0, 0)
    init4 = lambda b, c: (0, 0, 0, 0)
    init3 = lambda b, c: (0, 0, 0)
    vec = pl.BlockSpec((1, D_MODEL), const)
    grp = pl.BlockSpec((1, D_GROUP), const)
    state4 = pl.BlockSpec((nb, N_HEADS, D_HEAD, D_HEAD), lambda b, c: (b, 0, 0, 0))

    def next_tile(b, c):
        nxt = jnp.minimum(b * n_chunks + c + 1, n_tiles - 1)
        return (nxt // n_chunks, nxt % n_chunks, 0)

    return pl.pallas_call(
        functools.partial(_prompt_kernel, t_len=t_len, n_chunks=n_chunks, n_seq_blk=nb),
        grid=(n_seq // nb, n_chunks),
        in_specs=[
            pl.BlockSpec((nb, t_len, D_MODEL), init3),
            pl.BlockSpec((nb, t_len, D_MODEL), next_tile),
            vec, vec,
            pl.BlockSpec((D_MODEL, D_PROJ), const),
            pl.BlockSpec((1, D_PROJ), const),
            pl.BlockSpec((1, N_HEADS, D_HEAD, D_HEAD), init4),
            pl.BlockSpec((1, N_HEADS, D_HEAD), init3),
            pl.BlockSpec((1, 1, N_HEADS), init3),
            pl.BlockSpec((1, N_HEADS, D_HEAD, D_HEAD), init4),
            pl.BlockSpec((1, LANES), const),
            grp, grp, grp,
            pl.BlockSpec((D_MODEL, D_MODEL), const),
            vec, vec, vec,
        ],
        out_specs=[
            pl.BlockSpec((nb, t_len, D_MODEL), lambda b, c: (b, c, 0)),
            state4,
            pl.BlockSpec((nb, N_HEADS, D_HEAD), lambda b, c: (b, 0, 0)),
            pl.BlockSpec((nb, 1, N_HEADS), lambda b, c: (b, 0, 0)),
            state4,
        ],
        out_shape=[
            jax.ShapeDtypeStruct(x.shape, F32),
            jax.ShapeDtypeStruct((n_seq, N_HEADS, D_HEAD, D_HEAD), F32),
            jax.ShapeDtypeStruct((n_seq, N_HEADS, D_HEAD), F32),
            jax.ShapeDtypeStruct((n_seq, 1, N_HEADS), F32),
            jax.ShapeDtypeStruct((n_seq, N_HEADS, D_HEAD, D_HEAD), F32),
        ],
        scratch_shapes=[pltpu.VMEM((nb * t_len, D_ACT), F32), pltpu.VMEM((nb * t_len, D_MODEL), F32),
                        pltpu.VMEM((nb * t_len, D_ACT), F32), pltpu.VMEM((nb * t_len, D_MODEL), F32)],
        compiler_params=pltpu.CompilerParams(
            dimension_semantics=("arbitrary", "arbitrary"), vmem_limit_bytes=VMEM_LIMIT),
        name="prompt_mixer",
    )(x, x, ln_e_g, ln_e_b, w_in, b_in, c0, n0, m0, s0, bf_row, ga, gb, lb, w_out, b_out, ln_g, ln_b)


def _out_proj_kernel(x_ref, mix_ref, ge_ref, be_ref, w_ref, bias_ref, g_ref, b_ref, o_ref):
    xn = _layer_norm(x_ref[...], ge_ref[...], be_ref[...])
    y = _dot(mix_ref[...], w_ref[...]) + bias_ref[...]
    o_ref[...] = _layer_norm(ALPHA * xn + y, g_ref[...], b_ref[...])


def _out_proj(x, mix, ln_e_g, ln_e_b, w, bias, ln_g, ln_b, *, tm):
    n = x.shape[0]
    const = lambda i: (0, 0)
    row = pl.BlockSpec((tm, D_MODEL), lambda i: (i, 0))
    vec = pl.BlockSpec((1, D_MODEL), const)
    return pl.pallas_call(
        _out_proj_kernel,
        grid=(n // tm,),
        in_specs=[row, row, vec, vec, pl.BlockSpec((D_MODEL, D_MODEL), const), vec, vec, vec],
        out_specs=row,
        out_shape=jax.ShapeDtypeStruct((n, D_MODEL), F32),
        compiler_params=pltpu.CompilerParams(
            dimension_semantics=("arbitrary",), vmem_limit_bytes=VMEM_LIMIT),
        name="out_proj",
    )(x, mix, ln_e_g, ln_e_b, w, bias, ln_g, ln_b)


def _ffn_kernel(x_ref, cs_ref, wu_ref, bu_ref, wc_ref, bc_ref, wd_ref, bd_ref, g_ref, b_ref,
                y_ref, nc_ref, full_ref, *, n_seq_blk, t_len):
    hist = SUBLANES - (CONV_W - 1)

    @pl.when(pl.program_id(1) == 0)
    def _():
        full_ref[:, hist:SUBLANES, :] = cs_ref[...]

    x = x_ref[...]
    up = _dot(x.astype(BF16), wu_ref[...]) + bu_ref[...]
    u = up[:, :D_FF].reshape(n_seq_blk, t_len, D_FF)
    gate = up[:, D_FF:].reshape(n_seq_blk, t_len, D_FF)
    full_ref[:, SUBLANES:SUBLANES + t_len, :] = u
    conv = bc_ref[...] + u * wc_ref[CONV_W - 1:CONV_W, :]
    for j in range(CONV_W - 1):
        conv = conv + full_ref[:, hist + j:hist + j + t_len, :] * wc_ref[j:j + 1, :]
    last = full_ref[:, hist + t_len:SUBLANES + t_len, :]
    nc_ref[...] = last
    full_ref[:, hist:SUBLANES, :] = last
    act = (conv * _sigmoid(conv) * gate).reshape(n_seq_blk * t_len, D_FF)
    ffn = _dot(act.astype(BF16), wd_ref[...]) + bd_ref[...]
    y_ref[...] = _layer_norm(ALPHA * x + ffn, g_ref[...], b_ref[...])


def _ffn(x, conv_state, w_up, b_up, w_conv, b_conv, w_down, b_down, ln_g, ln_b,
         *, n_seq, seq_len, n_seq_blk, t_len, shared_init):
    n_t = seq_len // t_len
    rows = n_seq_blk * t_len
    const = lambda s, t: (0, 0)
    cs_map = (lambda s, t: (0, 0, 0)) if shared_init else (lambda s, t: (s, 0, 0))
    row = pl.BlockSpec((rows, D_MODEL), lambda s, t: (s * n_t + t, 0))
    vec = pl.BlockSpec((1, D_MODEL), const)
    return pl.pallas_call(
        functools.partial(_ffn_kernel, n_seq_blk=n_seq_blk, t_len=t_len),
        grid=(n_seq // n_seq_blk, n_t),
        in_specs=[
            row,
            pl.BlockSpec((n_seq_blk, CONV_W - 1, D_FF), cs_map),
            _resident((D_MODEL, 2 * D_FF), const),
            pl.BlockSpec((1, 2 * D_FF), const),
            pl.BlockSpec((CONV_W, D_FF), const),
            pl.BlockSpec((1, D_FF), const),
            _resident((D_FF, D_MODEL), const),
            vec, vec, vec,
        ],
        out_specs=[row, pl.BlockSpec((n_seq_blk, CONV_W - 1, D_FF), lambda s, t: (s, 0, 0))],
        out_shape=[
            jax.ShapeDtypeStruct((n_seq * seq_len, D_MODEL), F32),
            jax.ShapeDtypeStruct((n_seq, CONV_W - 1, D_FF), F32),
        ],
        scratch_shapes=[pltpu.VMEM((n_seq_blk, SUBLANES + t_len, D_FF), F32)],
        compiler_params=pltpu.CompilerParams(
            dimension_semantics=("arbitrary", "arbitrary"), vmem_limit_bytes=VMEM_LIMIT),
        name=f"ffn_t{t_len}",
    )(x, conv_state, w_up, b_up, w_conv, b_conv, w_down, b_down, ln_g, ln_b)


def kernel(x_prompt, x_sample, state_mlstm_C, state_mlstm_n, state_mlstm_m, state_hgrn_S, state_ffn_conv, meta_tokens, ln_emb_g, ln_emb_b, w_in, b_in, b_fgate_a, g_norm_a, g_norm_b, hgrn_lb_logits, w_out, b_out, ln1_g, ln1_b, w_up, b_up, w_conv, b_conv, w_down, b_down, ln2_g, ln2_b):
    assert w_in.shape[0] == DEPTH == 1
    n_prompt, seq, _ = x_prompt.shape
    n_sample, dec_seq, _ = x_sample.shape
    row = lambda v: v.reshape(1, -1).astype(F32)

    gate0 = 4 * D_GROUP
    gate1 = gate0 + 2 * N_HEADS
    pad = D_PROJ - w_in.shape[2]
    w_in16 = w_in[0].astype(BF16)
    w_in_p = jnp.concatenate(
        [w_in16[:, :gate0], w_in16[:, gate1:], w_in16[:, gate0:gate1],
         jnp.zeros((D_MODEL, pad), BF16)], axis=1)
    b_in_p = jnp.concatenate(
        [b_in[0][:gate0], b_in[0][gate1:], b_in[0][gate0:gate1], jnp.zeros((pad,), b_in.dtype)]
    ).reshape(1, D_PROJ).astype(F32)
    bf_row = jnp.zeros((1, LANES), F32).at[0, N_HEADS:2 * N_HEADS].set(b_fgate_a[0].astype(F32))
    lb = jnp.cumsum(jax.nn.softmax(hgrn_lb_logits.astype(F32), axis=0), axis=0)[0].reshape(1, D_GROUP)
    ga, gb = row(g_norm_a[0]), row(g_norm_b[0])
    ln_e = (row(ln_emb_g), row(ln_emb_b))
    out_p = (w_out[0].astype(BF16), row(b_out[0]), row(ln1_g[0]), row(ln1_b[0]))
    ffn_p = (w_up[0].astype(BF16), row(b_up[0]), w_conv[0].astype(F32), row(b_conv[0]),
             w_down[0].astype(BF16), row(b_down[0]), row(ln2_g[0]), row(ln2_b[0]))

    def layer(x_rows, mixer_state, conv_state, *, n_seq, seq_len, t_mix, mix_seq_blk, tm,
              ffn_seq_blk, ffn_t, shared_init):
        proj = _in_proj(x_rows, *ln_e, w_in_p, b_in_p, tm=tm)
        mix, c_new, n_new, m_new, s_new = _mixer(
            proj, *mixer_state, bf_row, ga, gb, lb, n_seq=n_seq, n_chunks=seq_len // t_mix,
            t_len=t_mix, n_seq_blk=mix_seq_blk, shared_init=shared_init)
        x1 = _out_proj(x_rows, mix, *ln_e, *out_p, tm=tm)
        y, conv_new = _ffn(x1, conv_state, *ffn_p, n_seq=n_seq, seq_len=seq_len,
                           n_seq_blk=ffn_seq_blk, t_len=ffn_t, shared_init=shared_init)
        return y, c_new, n_new, m_new, s_new, conv_new

    zero_state = (jnp.zeros((1, N_HEADS, D_HEAD, D_HEAD), F32), jnp.zeros((1, N_HEADS, D_HEAD), F32),
                  jnp.zeros((1, 1, N_HEADS), F32), jnp.zeros((1, N_HEADS, D_HEAD, D_HEAD), F32))
    _, c_m, n_m, m_m, s_m, conv_m = layer(
        meta_tokens.astype(F32), zero_state, jnp.zeros((1, CONV_W - 1, D_FF), F32),
        n_seq=1, seq_len=N_META, t_mix=N_META, mix_seq_blk=1, tm=N_META, ffn_seq_blk=1,
        ffn_t=N_META, shared_init=False)

    x1_p, c_p, n_p, m_p, s_p = _prompt_mixer(
        x_prompt.astype(F32), *ln_e, w_in_p, b_in_p, c_m, n_m, m_m, s_m, bf_row, ga, gb, lb, *out_p,
        n_chunks=seq // PROMPT_CHUNK, t_len=PROMPT_CHUNK, n_seq_blk=PROMPT_SEQS_PER_STEP)
    y_p, conv_p = _ffn(x1_p.reshape(n_prompt * seq, D_MODEL), conv_m, *ffn_p, n_seq=n_prompt,
                       seq_len=seq, n_seq_blk=1, t_len=512, shared_init=True)

    sample_state = (state_mlstm_C[0].astype(F32), state_mlstm_n[0].astype(F32),
                    state_mlstm_m[0].astype(F32).reshape(n_sample, 1, N_HEADS),
                    state_hgrn_S[0].astype(F32))
    y_s, c_s, n_s, m_s, s_s, conv_s = layer(
        x_sample.reshape(n_sample * dec_seq, D_MODEL), sample_state, state_ffn_conv[0].astype(F32),
        n_seq=n_sample, seq_len=dec_seq, t_mix=dec_seq, mix_seq_blk=8, tm=256, ffn_seq_blk=32,
        ffn_t=dec_seq, shared_init=False)

    lead = lambda v: v[None]
    return (y_p.reshape(n_prompt, seq, D_MODEL), y_s.reshape(n_sample, dec_seq, D_MODEL),
            lead(c_p), lead(n_p), lead(m_p.reshape(n_prompt, N_HEADS)), lead(s_p), lead(conv_p),
            lead(c_s), lead(n_s), lead(m_s.reshape(n_sample, N_HEADS)), lead(s_s), lead(conv_s))
```

```python
import functools

import jax
import jax.numpy as jnp
from jax import lax
from jax.experimental import pallas as pl
from jax.experimental.pallas import tpu as pltpu

D_MODEL = 1024
N_META = 16
N_HEADS = 4
D_HEAD = 128
D_GROUP = N_HEADS * D_HEAD
D_FF = 2816
CONV_W = 3
DEPTH = 1
ALPHA = (2.0 * DEPTH) ** 0.25
LN_EPS = 1e-5
RMS_EPS = 1e-6
NEG_LOG2_E = -1.4426950408889634

LANES = 128
SUBLANES = 8
GATE_COL = 8 * D_GROUP
D_PROJ = GATE_COL + LANES
KB_GROUP, F_GROUP = 8, 9
D_ACT = D_PROJ + 2 * D_GROUP
IN_PROJ_STAGE_COLS = 256
SIDE_STAGES_PER_ROUND = 3
PROMPT_SEQS_PER_STEP = 2
PROMPT_CHUNK = 128
VMEM_LIMIT = 56 * 1024 * 1024

F32 = jnp.float32
BF16 = jnp.bfloat16
NT_DIMS = (((1,), (1,)), ((), ()))
TN_DIMS = (((0,), (0,)), ((), ()))


def _layer_norm(x, g, b):
    mu = jnp.mean(x, axis=-1, keepdims=True)
    xc = x - mu
    var = jnp.mean(xc * xc, axis=-1, keepdims=True)
    return xc * lax.rsqrt(var + LN_EPS) * g + b


def _exp_neg(x):
    return jnp.exp2(x * NEG_LOG2_E)


def _sigmoid(x):
    return 1.0 / (1.0 + _exp_neg(x))


def _resident(block_shape, index_map):
    return pl.BlockSpec(block_shape, index_map, pipeline_mode=pl.Buffered(1))


def _dot(a, b):
    return jnp.dot(a, b, preferred_element_type=F32)


def _dot_nt(a, b):
    return lax.dot_general(a, b, NT_DIMS, preferred_element_type=F32)


def _dot_tn(a, b):
    return lax.dot_general(a, b, TN_DIMS, preferred_element_type=F32)


def _regroup_kernel(w_ref, o_ref):
    gate0 = 4 * D_GROUP
    gate1 = gate0 + 2 * N_HEADS
    rows = w_ref.shape[0]
    o_ref[:, :gate0] = w_ref[:, :gate0]
    o_ref[:, gate0:GATE_COL] = w_ref[:, gate1:gate1 + gate0]
    o_ref[:, GATE_COL:] = jnp.concatenate(
        [w_ref[:, gate0:gate1], jnp.zeros((rows, LANES - (gate1 - gate0)), w_ref.dtype)], axis=1)


def _regroup_in_proj_weight(w, *, tm):
    n, cols = w.shape
    return pl.pallas_call(
        _regroup_kernel,
        grid=(n // tm,),
        in_specs=[pl.BlockSpec((tm, cols), lambda i: (i, 0))],
        out_specs=pl.BlockSpec((tm, D_PROJ), lambda i: (i, 0)),
        out_shape=jax.ShapeDtypeStruct((n, D_PROJ), w.dtype),
        compiler_params=pltpu.CompilerParams(dimension_semantics=("arbitrary",)),
        name="regroup_w_in",
    )(w)


def _in_proj_kernel(x_ref, g_ref, b_ref, w_ref, bias_ref, o_ref):
    xn = _layer_norm(x_ref[...], g_ref[...], b_ref[...])
    o_ref[...] = _dot(xn.astype(BF16), w_ref[...]) + bias_ref[...]


def _in_proj(x, ln_g, ln_b, w, bias, *, tm):
    n = x.shape[0]
    const = lambda i: (0, 0)
    return pl.pallas_call(
        _in_proj_kernel,
        grid=(n // tm,),
        in_specs=[
            pl.BlockSpec((tm, D_MODEL), lambda i: (i, 0)),
            pl.BlockSpec((1, D_MODEL), const),
            pl.BlockSpec((1, D_MODEL), const),
            pl.BlockSpec((D_MODEL, D_PROJ), const),
            pl.BlockSpec((1, D_PROJ), const),
        ],
        out_specs=pl.BlockSpec((tm, D_PROJ), lambda i: (i, 0)),
        out_shape=jax.ShapeDtypeStruct((n, D_PROJ), F32),
        compiler_params=pltpu.CompilerParams(
            dimension_semantics=("arbitrary",), vmem_limit_bytes=VMEM_LIMIT),
        name="in_proj",
    )(x, ln_g, ln_b, w, bias)


def _block_rows(x, level, t_len, row_in_block):
    size = 2 << level
    if size > SUBLANES:
        pieces = [jnp.broadcast_to(x[j * size + row_in_block:j * size + row_in_block + 1, :],
                                   (size, x.shape[1])) for j in range(t_len // size)]
        return pieces[0] if len(pieces) == 1 else jnp.concatenate(pieces, axis=0)
    x3 = x.reshape(t_len // SUBLANES, SUBLANES, x.shape[1])
    sub = lax.broadcasted_iota(jnp.int32, x3.shape, 1)
    out = None
    for j in range(SUBLANES // size):
        row = jnp.broadcast_to(x3[:, j * size + row_in_block:j * size + row_in_block + 1, :], x3.shape)
        out = row if out is None else jnp.where(sub >= j * size, row, out)
    return out.reshape(x.shape)


def _interleave_halves(lower, upper, level, t_len):
    half = 1 << level
    if half >= SUBLANES:
        pieces = []
        for j in range(t_len // (2 * half)):
            pieces.append(lower[2 * half * j:2 * half * j + half])
            pieces.append(upper[2 * half * j + half:2 * half * (j + 1)])
        return jnp.concatenate(pieces, axis=0)
    rows = lax.broadcasted_iota(jnp.int32, lower.shape, 0)
    return jnp.where((rows & half) != 0, upper, lower)


def _run_interleaved(gens, stages_per_round):
    results = [None] * len(gens)
    live = [True] * len(gens)
    while any(live):
        for g, steps in enumerate(stages_per_round):
            for _ in range(steps):
                if live[g]:
                    try:
                        next(gens[g])
                    except StopIteration as stop:
                        results[g], live[g] = stop.value, False
    return results


def _cumsum_rows(tril16, x):
    hi = x.astype(BF16)
    rest = x - hi.astype(F32)
    mid = rest.astype(BF16)
    lo = (rest - mid.astype(F32)).astype(BF16)
    return _dot(tril16, hi) + _dot(tril16, mid) + _dot(tril16, lo)


def _activate(group, x, lb=None):
    if group == 1:
        return {1: x * (D_HEAD ** -0.5)}
    if group in (3, 7):
        return {group: _sigmoid(x)}
    if group == 4:
        return {4: x * _sigmoid(x)}
    if group == 5:
        f = lb + (1.0 - lb) * _sigmoid(x)
        return {5: jnp.log(f), KB_GROUP: (1.0 - lb) / (1.0 + jnp.exp(x)), F_GROUP: f}
    return {group: x}


def _mlstm_units(*, q, k, v, gate, z, zt, head, c_old, n_old, m0, g_norm, causal, t_len):
    idx = range(len(q))
    q16 = [q[i].astype(BF16) for i in idx]
    k16 = [k[i].astype(BF16) for i in idx]
    qk = [_dot_nt(q16[i], k16[i]) for i in idx]
    qc = [_dot(q16[i], c_old[i].astype(BF16)) for i in idx]
    bs_row = [zt[i][N_HEADS + head[i]:N_HEADS + head[i] + 1, :] - zt[i][head[i]:head[i] + 1, :]
              for i in idx]
    yield
    b_col = [z[i][:, N_HEADS + head[i]:N_HEADS + head[i] + 1] for i in idx]
    i_col = [z[i][:, head[i]:head[i] + 1] for i in idx]
    d = [jnp.where(causal, b_col[i] - bs_row[i], -jnp.inf) for i in idx]
    m_t = [jnp.maximum(b_col[i] + m0[i], jnp.max(d[i], axis=1, keepdims=True)) for i in idx]
    dec = [jnp.exp(b_col[i] + m0[i] - m_t[i]) for i in idx]
    sw = [jnp.exp(d[i] - m_t[i]) * qk[i] for i in idx]
    yield
    swv = [_dot(sw[i].astype(BF16), v[i].astype(BF16)) for i in idx]
    last = slice(t_len - 1, t_len)
    w_last = [jnp.exp(b_col[i][last] - b_col[i] + i_col[i] - m_t[i][last]) for i in idx]
    kv = [_dot_tn(k16[i], (w_last[i] * v[i]).astype(BF16)) for i in idx]
    yield
    c_new = [dec[i][last] * c_old[i] + kv[i] for i in idx]
    n_new = [dec[i][last] * n_old[i] + jnp.sum(w_last[i] * k[i], axis=0, keepdims=True) for i in idx]
    m_new = [m_t[i][last] for i in idx]
    den = [dec[i] * jnp.sum(q[i] * n_old[i], axis=1, keepdims=True)
           + jnp.sum(sw[i], axis=1, keepdims=True) for i in idx]
    hid = [(dec[i] * qc[i] + swv[i]) / jnp.maximum(jnp.abs(den[i]), _exp_neg(m_t[i])) for i in idx]
    yield
    rms = [lax.rsqrt(jnp.mean(hid[i] * hid[i], axis=1, keepdims=True) + RMS_EPS) for i in idx]
    out = [gate[i] * (hid[i] * rms[i] * g_norm[i]) for i in idx]
    return out, c_new, n_new, m_new


def _hgrn_units(*, qb, log_f, f, kb, iv, gate, s_old_t, g_norm, tril, level_of, t_len):
    idx = range(len(qb))
    n_levels = t_len.bit_length() - 1
    a = [_cumsum_rows(tril, log_f[i]) for i in idx]
    yield
    iv16 = [iv[i].astype(BF16) for i in idx]
    diag = [_dot_nt(qb[i].astype(BF16), kb[i].astype(BF16)) for i in idx]
    scores = [jnp.where(level_of == -2, diag[i], 0.0) for i in idx]
    for level in range(n_levels):
        yield
        x16 = []
        for i in idx:
            base = _interleave_halves(kb[i], qb[i], level, t_len)
            if level == 0:
                x = base * _interleave_halves(jnp.ones_like(f[i]), f[i], 0, t_len)
            else:
                ref = _block_rows(a[i], level, t_len, (1 << level) - 1)
                x = base * _exp_neg(jnp.abs(a[i] - ref))
            x16.append(x.astype(BF16))
        part = [_dot_nt(x16[i], x16[i]) for i in idx]
        scores = [jnp.where(level_of == level, part[i], scores[i]) for i in idx]
    yield
    last = slice(t_len - 1, t_len)
    q_in = [(qb[i] * jnp.exp(a[i])).astype(BF16) for i in idx]
    k_out = [(kb[i] * jnp.exp(a[i][last] - a[i])).astype(BF16) for i in idx]
    inter = [_dot_nt(q_in[i], s_old_t[i].astype(BF16)) for i in idx]
    intra = [_dot(scores[i].astype(BF16), iv16[i]) for i in idx]
    kv = [_dot_tn(iv16[i], k_out[i]) for i in idx]
    yield
    s_new_t = [jnp.exp(a[i][last]) * s_old_t[i] + kv[i] for i in idx]
    o = [inter[i] + intra[i] for i in idx]
    rms = [lax.rsqrt(jnp.mean(o[i] * o[i], axis=1, keepdims=True) + RMS_EPS) for i in idx]
    out = [gate[i] * (o[i] * rms[i] * g_norm[i]) for i in idx]
    return out, s_new_t


def _load_state(c0_ref, n0_ref, m0_ref, s0_ref, c_ref, n_ref, m_ref, s_ref, n_seq_blk):
    shared = c0_ref.shape[0] == 1 and n_seq_blk > 1
    for s in range(n_seq_blk):
        src = 0 if shared else s
        c_ref[s] = c0_ref[src]
        n_ref[s] = n0_ref[src]
        m_ref[s] = m0_ref[src]
        for h in range(N_HEADS):
            s_ref[s, h] = s0_ref[src, h].T


def _finish_state(s_ref, n_seq_blk):
    for s in range(n_seq_blk):
        for h in range(N_HEADS):
            s_ref[s, h] = s_ref[s, h].T


def _act_cols(group, h):
    base = group * D_GROUP if group < KB_GROUP else D_PROJ + (group - KB_GROUP) * D_GROUP
    return slice(base + h * D_HEAD, base + (h + 1) * D_HEAD)


def _mixer_body(p_ref, bf_ref, ga_ref, gb_ref, lb_ref, c_ref, n_ref, m_ref, s_ref,
                *, t_len, n_seq_blk, activated, side_stages=None):
    rows = lax.broadcasted_iota(jnp.int32, (t_len, t_len), 0)
    cols = lax.broadcasted_iota(jnp.int32, (t_len, t_len), 1)
    causal = cols <= rows
    tril = causal.astype(BF16)
    level_of = jnp.where(rows > cols, 31 - lax.clz(rows ^ cols), jnp.where(rows == cols, -2, -1))
    lane = lax.broadcasted_iota(jnp.int32, (t_len, LANES), 1)
    is_f = (lane >= N_HEADS) & (lane < 2 * N_HEADS)
    hd = lambda j, h: slice(j * D_GROUP + h * D_HEAD, j * D_GROUP + (h + 1) * D_HEAD)

    units = [(s, h) for s in range(n_seq_blk) for h in range(N_HEADS)]
    rs = lambda s: slice(s * t_len, (s + 1) * t_len)

    z_seq, zt_seq, n_seq, m_seq = [], [], [], []
    for s in range(n_seq_blk):
        gates = p_ref[rs(s), GATE_COL:GATE_COL + LANES]
        log_f = jnp.where(is_f, jax.nn.log_sigmoid(gates + bf_ref[...]), 0.0)
        cum_f = _cumsum_rows(tril, log_f)
        z_seq.append(jnp.where(is_f, cum_f, jnp.where(lane < N_HEADS, gates, 0.0)))
        zt_seq.append(z_seq[s].T)
        n_seq.append(n_ref[s])
        m_seq.append(m_ref[s])
    computed = {}

    def act(group):
        if activated:
            return [p_ref[rs(s), _act_cols(group, h)] for s, h in units]
        source = 5 if group >= KB_GROUP else group
        for s, h in units:
            if (source, s, h) not in computed:
                computed[source, s, h] = _activate(source, p_ref[rs(s), hd(source, h)],
                                                   lb_ref[:, hd(0, h)])
        return [computed[source, s, h][group] for s, h in units]

    mlstm = _mlstm_units(
        q=act(0), k=act(1), v=act(2), gate=act(3),
        z=[z_seq[s] for s, h in units], zt=[zt_seq[s] for s, h in units], head=[h for s, h in units],
        c_old=[c_ref[s, h] for s, h in units], n_old=[n_seq[s][h:h + 1, :] for s, h in units],
        m0=[m_seq[s][:, h:h + 1] for s, h in units], g_norm=[ga_ref[:, hd(0, h)] for s, h in units],
        causal=causal, t_len=t_len)
    hgrn = _hgrn_units(
        qb=act(4), log_f=act(5), f=act(F_GROUP), kb=act(KB_GROUP), iv=act(6), gate=act(7),
        s_old_t=[s_ref[s, h] for s, h in units],
        g_norm=[gb_ref[:, hd(0, h)] for s, h in units], tril=tril, level_of=level_of, t_len=t_len)
    gens, per_round = [mlstm, hgrn], [1, 2]
    if side_stages is not None:
        gens, per_round = gens + [side_stages], per_round + [SIDE_STAGES_PER_ROUND]
    results = _run_interleaved(gens, per_round)
    (outs, c_new, n_new, m_new), (outs_b, s_new_t) = results[:2]

    for i, (s, h) in enumerate(units):
        c_ref[s, h] = c_new[i]
        s_ref[s, h] = s_new_t[i]
    head_lane = lax.broadcasted_iota(jnp.int32, (1, N_HEADS), 1)
    for s in range(n_seq_blk):
        n_ref[s] = jnp.concatenate(n_new[s * N_HEADS:(s + 1) * N_HEADS], axis=0)
        m_row = m_seq[s]
        for h in range(N_HEADS):
            m_row = jnp.where(head_lane == h, m_new[s * N_HEADS + h], m_row)
        m_ref[s] = m_row
    return outs, outs_b


def _mixer_kernel(p_ref, c0_ref, n0_ref, m0_ref, s0_ref, bf_ref, ga_ref, gb_ref, lb_ref,
                  mix_ref, c_ref, n_ref, m_ref, s_ref, *, t_len, n_chunks, n_seq_blk):
    chunk = pl.program_id(1)

    @pl.when(chunk == 0)
    def _():
        _load_state(c0_ref, n0_ref, m0_ref, s0_ref, c_ref, n_ref, m_ref, s_ref, n_seq_blk)

    outs_a, outs_b = _mixer_body(p_ref, bf_ref, ga_ref, gb_ref, lb_ref, c_ref, n_ref, m_ref, s_ref,
                                 t_len=t_len, n_seq_blk=n_seq_blk, activated=False)
    for i in range(n_seq_blk * N_HEADS):
        s, h = divmod(i, N_HEADS)
        rows = slice(s * t_len, (s + 1) * t_len)
        mix_ref[rows, h * D_HEAD:(h + 1) * D_HEAD] = outs_a[i].astype(mix_ref.dtype)
        mix_ref[rows, D_GROUP + h * D_HEAD:D_GROUP + (h + 1) * D_HEAD] = outs_b[i].astype(mix_ref.dtype)

    @pl.when(chunk == n_chunks - 1)
    def _():
        _finish_state(s_ref, n_seq_blk)


def _mixer(proj, c0, n0, m0, s0, bf_row, ga, gb, lb, *, n_seq, n_chunks, t_len, n_seq_blk,
           shared_init):
    assert not shared_init or n_seq_blk == 1
    assert n_chunks == 1 or n_seq_blk == 1
    nb = n_seq_blk
    init = (lambda b, c: (0, 0, 0, 0)) if shared_init else (lambda b, c: (b, 0, 0, 0))
    init3 = (lambda b, c: (0, 0, 0)) if shared_init else (lambda b, c: (b, 0, 0))
    const = lambda b, c: (0, 0)
    state4 = pl.BlockSpec((nb, N_HEADS, D_HEAD, D_HEAD), lambda b, c: (b, 0, 0, 0))
    return pl.pallas_call(
        functools.partial(_mixer_kernel, t_len=t_len, n_chunks=n_chunks, n_seq_blk=nb),
        grid=(n_seq // nb, n_chunks),
        in_specs=[
            pl.BlockSpec((nb * t_len, D_PROJ), lambda b, c: (b * n_chunks + c, 0)),
            pl.BlockSpec((nb, N_HEADS, D_HEAD, D_HEAD), init),
            pl.BlockSpec((nb, N_HEADS, D_HEAD), init3),
            pl.BlockSpec((nb, 1, N_HEADS), init3),
            pl.BlockSpec((nb, N_HEADS, D_HEAD, D_HEAD), init),
            pl.BlockSpec((1, LANES), const),
            pl.BlockSpec((1, D_GROUP), const),
            pl.BlockSpec((1, D_GROUP), const),
            pl.BlockSpec((1, D_GROUP), const),
        ],
        out_specs=[
            pl.BlockSpec((nb * t_len, D_MODEL), lambda b, c: (b * n_chunks + c, 0)),
            state4,
            pl.BlockSpec((nb, N_HEADS, D_HEAD), lambda b, c: (b, 0, 0)),
            pl.BlockSpec((nb, 1, N_HEADS), lambda b, c: (b, 0, 0)),
            state4,
        ],
        out_shape=[
            jax.ShapeDtypeStruct((n_seq * n_chunks * t_len, D_MODEL), BF16),
            jax.ShapeDtypeStruct((n_seq, N_HEADS, D_HEAD, D_HEAD), F32),
            jax.ShapeDtypeStruct((n_seq, N_HEADS, D_HEAD), F32),
            jax.ShapeDtypeStruct((n_seq, 1, N_HEADS), F32),
            jax.ShapeDtypeStruct((n_seq, N_HEADS, D_HEAD, D_HEAD), F32),
        ],
        compiler_params=pltpu.CompilerParams(
            dimension_semantics=("arbitrary", "arbitrary"), vmem_limit_bytes=VMEM_LIMIT),
        name=f"mixer_t{t_len}",
    )(proj, c0, n0, m0, s0, bf_row, ga, gb, lb)


def _in_proj_stages(x_ref, g_ref, b_ref, w_ref, bias_ref, lb_ref, act_ref, xn_ref):
    assert D_GROUP % IN_PROJ_STAGE_COLS == 0
    xn = _layer_norm(x_ref[...].reshape(xn_ref.shape), g_ref[...], b_ref[...])
    xn_ref[...] = xn
    x16 = xn.astype(BF16)
    for lo in range(0, D_PROJ, IN_PROJ_STAGE_COLS):
        hi = min(lo + IN_PROJ_STAGE_COLS, D_PROJ)
        yield
        block = _dot(x16, w_ref[:, lo:hi]) + bias_ref[:, lo:hi]
        group, off = divmod(lo, D_GROUP)
        if lo >= GATE_COL:
            act_ref[:, lo:hi] = block
            continue
        for dst, val in _activate(group, block, lb_ref[:, off:off + hi - lo]).items():
            base = _act_cols(dst, 0).start + off
            act_ref[:, base:base + hi - lo] = val


def _prompt_kernel(x0_ref, xnext_ref, ge_ref, be_ref, win_ref, bin_ref, c0_ref, n0_ref, m0_ref, s0_ref,
                   bf_ref, ga_ref, gb_ref, lb_ref, wout_ref, bout_ref, g1_ref, b1_ref,
                   x1_ref, c_ref, n_ref, m_ref, s_ref, proj_scr, xn_scr, proj_alt, xn_alt,
                   *, t_len, n_chunks, n_seq_blk):
    chunk = pl.program_id(1)
    step = pl.program_id(0) * n_chunks + chunk

    @pl.when(step == 0)
    def _():
        first = _in_proj_stages(x0_ref, ge_ref, be_ref, win_ref, bin_ref, lb_ref, proj_scr, xn_scr)
        _run_interleaved([first], [1])

    @pl.when(chunk == 0)
    def _():
        _load_state(c0_ref, n0_ref, m0_ref, s0_ref, c_ref, n_ref, m_ref, s_ref, n_seq_blk)

    def tile(proj_cur, xn_cur, proj_next, xn_next):
        next_proj = _in_proj_stages(xnext_ref, ge_ref, be_ref, win_ref, bin_ref, lb_ref,
                                    proj_next, xn_next)
        outs_a, outs_b = _mixer_body(
            proj_cur, bf_ref, ga_ref, gb_ref, lb_ref, c_ref, n_ref, m_ref, s_ref,
            t_len=t_len, n_seq_blk=n_seq_blk, activated=True, side_stages=next_proj)
        mix = jnp.concatenate(
            [jnp.concatenate([o.astype(BF16) for o in outs_a[s * N_HEADS:(s + 1) * N_HEADS]
                              + outs_b[s * N_HEADS:(s + 1) * N_HEADS]], axis=1)
             for s in range(n_seq_blk)], axis=0)
        y = _dot(mix, wout_ref[...]) + bout_ref[...]
        x1 = _layer_norm(ALPHA * xn_cur[...] + y, g1_ref[...], b1_ref[...])
        x1_ref[...] = x1.reshape(x1_ref.shape)

    @pl.when(step % 2 == 0)
    def _():
        tile(proj_scr, xn_scr, proj_alt, xn_alt)

    @pl.when(step % 2 == 1)
    def _():
        tile(proj_alt, xn_alt, proj_scr, xn_scr)

    @pl.when(chunk == n_chunks - 1)
    def _():
        _finish_state(s_ref, n_seq_blk)


def _prompt_mixer(x, ln_e_g, ln_e_b, w_in, b_in, c0, n0, m0, s0, bf_row, ga, gb, lb,
                  w_out, b_out, ln_g, ln_b, *, n_chunks, t_len, n_seq_blk):
    n_seq = x.shape[0]
    nb = n_seq_blk
    n_tiles = (n_seq // nb) * n_chunks
    const = lambda b, c: (0, 0)
    init4 = lambda b, c: (0, 0, 0, 0)
    init3 = lambda b, c: (0, 0, 0)
    vec = pl.BlockSpec((1, D_MODEL), const)
    grp = pl.BlockSpec((1, D_GROUP), const)
    state4 = pl.BlockSpec((nb, N_HEADS, D_HEAD, D_HEAD), lambda b, c: (b, 0, 0, 0))

    def next_tile(b, c):
        nxt = jnp.minimum(b * n_chunks + c + 1, n_tiles - 1)
        return (nxt // n_chunks, nxt % n_chunks, 0)

    return pl.pallas_call(
        functools.partial(_prompt_kernel, t_len=t_len, n_chunks=n_chunks, n_seq_blk=nb),
        grid=(n_seq // nb, n_chunks),
        in_specs=[
            _resident((nb, t_len, D_MODEL), init3),
            pl.BlockSpec((nb, t_len, D_MODEL), next_tile),
            vec, vec,
            _resident((D_MODEL, D_PROJ), const),
            pl.BlockSpec((1, D_PROJ), const),
            pl.BlockSpec((1, N_HEADS, D_HEAD, D_HEAD), init4),
            pl.BlockSpec((1, N_HEADS, D_HEAD), init3),
            pl.BlockSpec((1, 1, N_HEADS), init3),
            pl.BlockSpec((1, N_HEADS, D_HEAD, D_HEAD), init4),
            pl.BlockSpec((1, LANES), const),
            grp, grp, grp,
            _resident((D_MODEL, D_MODEL), const),
            vec, vec, vec,
        ],
        out_specs=[
            pl.BlockSpec((nb, t_len, D_MODEL), lambda b, c: (b, c, 0)),
            state4,
            pl.BlockSpec((nb, N_HEADS, D_HEAD), lambda b, c: (b, 0, 0)),
            pl.BlockSpec((nb, 1, N_HEADS), lambda b, c: (b, 0, 0)),
            state4,
        ],
        out_shape=[
            jax.ShapeDtypeStruct(x.shape, F32),
            jax.ShapeDtypeStruct((n_seq, N_HEADS, D_HEAD, D_HEAD), F32),
            jax.ShapeDtypeStruct((n_seq, N_HEADS, D_HEAD), F32),
            jax.ShapeDtypeStruct((n_seq, 1, N_HEADS), F32),
            jax.ShapeDtypeStruct((n_seq, N_HEADS, D_HEAD, D_HEAD), F32),
        ],
        scratch_shapes=[pltpu.VMEM((nb * t_len, D_ACT), F32), pltpu.VMEM((nb * t_len, D_MODEL), F32),
                        pltpu.VMEM((nb * t_len, D_ACT), F32), pltpu.VMEM((nb * t_len, D_MODEL), F32)],
        compiler_params=pltpu.CompilerParams(
            dimension_semantics=("arbitrary", "arbitrary"), vmem_limit_bytes=VMEM_LIMIT),
        name="prompt_mixer",
    )(x, x, ln_e_g, ln_e_b, w_in, b_in, c0, n0, m0, s0, bf_row, ga, gb, lb, w_out, b_out, ln_g, ln_b)


def _out_proj_kernel(x_ref, mix_ref, ge_ref, be_ref, w_ref, bias_ref, g_ref, b_ref, o_ref):
    xn = _layer_norm(x_ref[...], ge_ref[...], be_ref[...])
    y = _dot(mix_ref[...], w_ref[...]) + bias_ref[...]
    o_ref[...] = _layer_norm(ALPHA * xn + y, g_ref[...], b_ref[...])


def _out_proj(x, mix, ln_e_g, ln_e_b, w, bias, ln_g, ln_b, *, tm):
    n = x.shape[0]
    const = lambda i: (0, 0)
    row = pl.BlockSpec((tm, D_MODEL), lambda i: (i, 0))
    vec = pl.BlockSpec((1, D_MODEL), const)
    return pl.pallas_call(
        _out_proj_kernel,
        grid=(n // tm,),
        in_specs=[row, row, vec, vec, pl.BlockSpec((D_MODEL, D_MODEL), const), vec, vec, vec],
        out_specs=row,
        out_shape=jax.ShapeDtypeStruct((n, D_MODEL), F32),
        compiler_params=pltpu.CompilerParams(
            dimension_semantics=("arbitrary",), vmem_limit_bytes=VMEM_LIMIT),
        name="out_proj",
    )(x, mix, ln_e_g, ln_e_b, w, bias, ln_g, ln_b)


def _ffn_kernel(x_ref, cs_ref, wu_ref, bu_ref, wc_ref, bc_ref, wd_ref, bd_ref, g_ref, b_ref,
                y_ref, nc_ref, full_ref, *, n_seq_blk, t_len):
    hist = SUBLANES - (CONV_W - 1)

    @pl.when(pl.program_id(1) == 0)
    def _():
        full_ref[:, hist:SUBLANES, :] = cs_ref[...]

    x = x_ref[...]
    up = _dot(x.astype(BF16), wu_ref[...]) + bu_ref[...]
    u = up[:, :D_FF].reshape(n_seq_blk, t_len, D_FF)
    gate = up[:, D_FF:].reshape(n_seq_blk, t_len, D_FF)
    full_ref[:, SUBLANES:SUBLANES + t_len, :] = u
    conv = bc_ref[...] + u * wc_ref[CONV_W - 1:CONV_W, :]
    for j in range(CONV_W - 1):
        conv = conv + full_ref[:, hist + j:hist + j + t_len, :] * wc_ref[j:j + 1, :]
    last = full_ref[:, hist + t_len:SUBLANES + t_len, :]
    nc_ref[...] = last
    full_ref[:, hist:SUBLANES, :] = last
    act = (conv * _sigmoid(conv) * gate).reshape(n_seq_blk * t_len, D_FF)
    ffn = _dot(act.astype(BF16), wd_ref[...]) + bd_ref[...]
    y_ref[...] = _layer_norm(ALPHA * x + ffn, g_ref[...], b_ref[...])


def _ffn(x, conv_state, w_up, b_up, w_conv, b_conv, w_down, b_down, ln_g, ln_b,
         *, n_seq, seq_len, n_seq_blk, t_len, shared_init):
    n_t = seq_len // t_len
    rows = n_seq_blk * t_len
    const = lambda s, t: (0, 0)
    cs_map = (lambda s, t: (0, 0, 0)) if shared_init else (lambda s, t: (s, 0, 0))
    row = pl.BlockSpec((rows, D_MODEL), lambda s, t: (s * n_t + t, 0))
    vec = pl.BlockSpec((1, D_MODEL), const)
    return pl.pallas_call(
        functools.partial(_ffn_kernel, n_seq_blk=n_seq_blk, t_len=t_len),
        grid=(n_seq // n_seq_blk, n_t),
        in_specs=[
            row,
            pl.BlockSpec((n_seq_blk, CONV_W - 1, D_FF), cs_map),
            _resident((D_MODEL, 2 * D_FF), const),
            pl.BlockSpec((1, 2 * D_FF), const),
            pl.BlockSpec((CONV_W, D_FF), const),
            pl.BlockSpec((1, D_FF), const),
            _resident((D_FF, D_MODEL), const),
            vec, vec, vec,
        ],
        out_specs=[row, pl.BlockSpec((n_seq_blk, CONV_W - 1, D_FF), lambda s, t: (s, 0, 0))],
        out_shape=[
            jax.ShapeDtypeStruct((n_seq * seq_len, D_MODEL), F32),
            jax.ShapeDtypeStruct((n_seq, CONV_W - 1, D_FF), F32),
        ],
        scratch_shapes=[pltpu.VMEM((n_seq_blk, SUBLANES + t_len, D_FF), F32)],
        compiler_params=pltpu.CompilerParams(
            dimension_semantics=("arbitrary", "arbitrary"), vmem_limit_bytes=VMEM_LIMIT),
        name=f"ffn_t{t_len}",
    )(x, conv_state, w_up, b_up, w_conv, b_conv, w_down, b_down, ln_g, ln_b)


def kernel(x_prompt, x_sample, state_mlstm_C, state_mlstm_n, state_mlstm_m, state_hgrn_S, state_ffn_conv, meta_tokens, ln_emb_g, ln_emb_b, w_in, b_in, b_fgate_a, g_norm_a, g_norm_b, hgrn_lb_logits, w_out, b_out, ln1_g, ln1_b, w_up, b_up, w_conv, b_conv, w_down, b_down, ln2_g, ln2_b):
    assert w_in.shape[0] == DEPTH == 1
    n_prompt, seq, _ = x_prompt.shape
    n_sample, dec_seq, _ = x_sample.shape
    row = lambda v: v.reshape(1, -1).astype(F32)

    gate0 = 4 * D_GROUP
    gate1 = gate0 + 2 * N_HEADS
    pad = D_PROJ - w_in.shape[2]
    w_in_p = _regroup_in_proj_weight(w_in[0].astype(BF16), tm=256)
    b_in_p = jnp.concatenate(
        [b_in[0][:gate0], b_in[0][gate1:], b_in[0][gate0:gate1], jnp.zeros((pad,), b_in.dtype)]
    ).reshape(1, D_PROJ).astype(F32)
    bf_row = jnp.zeros((1, LANES), F32).at[0, N_HEADS:2 * N_HEADS].set(b_fgate_a[0].astype(F32))
    lb = jnp.cumsum(jax.nn.softmax(hgrn_lb_logits.astype(F32), axis=0), axis=0)[0].reshape(1, D_GROUP)
    ga, gb = row(g_norm_a[0]), row(g_norm_b[0])
    ln_e = (row(ln_emb_g), row(ln_emb_b))
    out_p = (w_out[0].astype(BF16), row(b_out[0]), row(ln1_g[0]), row(ln1_b[0]))
    ffn_p = (w_up[0].astype(BF16), row(b_up[0]), w_conv[0].astype(F32), row(b_conv[0]),
             w_down[0].astype(BF16), row(b_down[0]), row(ln2_g[0]), row(ln2_b[0]))

    def layer(x_rows, mixer_state, conv_state, *, n_seq, seq_len, t_mix, mix_seq_blk, tm,
              ffn_seq_blk, ffn_t, shared_init):
        proj = _in_proj(x_rows, *ln_e, w_in_p, b_in_p, tm=tm)
        mix, c_new, n_new, m_new, s_new = _mixer(
            proj, *mixer_state, bf_row, ga, gb, lb, n_seq=n_seq, n_chunks=seq_len // t_mix,
            t_len=t_mix, n_seq_blk=mix_seq_blk, shared_init=shared_init)
        x1 = _out_proj(x_rows, mix, *ln_e, *out_p, tm=tm)
        y, conv_new = _ffn(x1, conv_state, *ffn_p, n_seq=n_seq, seq_len=seq_len,
                           n_seq_blk=ffn_seq_blk, t_len=ffn_t, shared_init=shared_init)
        return y, c_new, n_new, m_new, s_new, conv_new

    zero_state = (jnp.zeros((1, N_HEADS, D_HEAD, D_HEAD), F32), jnp.zeros((1, N_HEADS, D_HEAD), F32),
                  jnp.zeros((1, 1, N_HEADS), F32), jnp.zeros((1, N_HEADS, D_HEAD, D_HEAD), F32))
    _, c_m, n_m, m_m, s_m, conv_m = layer(
        meta_tokens.astype(F32), zero_state, jnp.zeros((1, CONV_W - 1, D_FF), F32),
        n_seq=1, seq_len=N_META, t_mix=N_META, mix_seq_blk=1, tm=N_META, ffn_seq_blk=1,
        ffn_t=N_META, shared_init=False)

    x1_p, c_p, n_p, m_p, s_p = _prompt_mixer(
        x_prompt.astype(F32), *ln_e, w_in_p, b_in_p, c_m, n_m, m_m, s_m, bf_row, ga, gb, lb, *out_p,
        n_chunks=seq // PROMPT_CHUNK, t_len=PROMPT_CHUNK, n_seq_blk=PROMPT_SEQS_PER_STEP)
    y_p, conv_p = _ffn(x1_p.reshape(n_prompt * seq, D_MODEL), conv_m, *ffn_p, n_seq=n_prompt,
                       seq_len=seq, n_seq_blk=1, t_len=512, shared_init=True)

    sample_state = (state_mlstm_C[0].astype(F32), state_mlstm_n[0].astype(F32),
                    state_mlstm_m[0].astype(F32).reshape(n_sample, 1, N_HEADS),
                    state_hgrn_S[0].astype(F32))
    y_s, c_s, n_s, m_s, s_s, conv_s = layer(
        x_sample.reshape(n_sample * dec_seq, D_MODEL), sample_state, state_ffn_conv[0].astype(F32),
        n_seq=n_sample, seq_len=dec_seq, t_mix=dec_seq, mix_seq_blk=8, tm=256, ffn_seq_blk=32,
        ffn_t=dec_seq, shared_init=False)

    lead = lambda v: v[None]
    return (y_p.reshape(n_prompt, seq, D_MODEL), y_s.reshape(n_sample, dec_seq, D_MODEL),
            lead(c_p), lead(n_p), lead(m_p.reshape(n_prompt, N_HEADS)), lead(s_p), lead(conv_p),
            lead(c_s), lead(n_s), lead(m_s.reshape(n_sample, N_HEADS)), lead(s_s), lead(conv_s))
```

```python
import functools

import jax
import jax.numpy as jnp
from jax import lax
from jax.experimental import pallas as pl
from jax.experimental.pallas import tpu as pltpu

D_MODEL = 1024
N_META = 16
N_HEADS = 4
D_HEAD = 128
D_GROUP = N_HEADS * D_HEAD
D_FF = 2816
CONV_W = 3
DEPTH = 1
ALPHA = (2.0 * DEPTH) ** 0.25
LN_EPS = 1e-5
RMS_EPS = 1e-6
NEG_LOG2_E = -1.4426950408889634

LANES = 128
SUBLANES = 8
GATE_COL = 8 * D_GROUP
D_PROJ = GATE_COL + LANES
KB_GROUP, F_GROUP = 8, 9
D_ACT = D_PROJ + 2 * D_GROUP
IN_PROJ_STAGE_COLS = 256
SIDE_STAGES_PER_ROUND = 3
PROMPT_SEQS_PER_STEP = 2
PROMPT_CHUNK = 128
VMEM_LIMIT = 56 * 1024 * 1024

F32 = jnp.float32
BF16 = jnp.bfloat16
NT_DIMS = (((1,), (1,)), ((), ()))
TN_DIMS = (((0,), (0,)), ((), ()))


def _layer_norm(x, g, b):
    mu = jnp.mean(x, axis=-1, keepdims=True)
    xc = x - mu
    var = jnp.mean(xc * xc, axis=-1, keepdims=True)
    return xc * lax.rsqrt(var + LN_EPS) * g + b


def _exp_neg(x):
    return jnp.exp2(x * NEG_LOG2_E)


def _sigmoid(x):
    return 1.0 / (1.0 + _exp_neg(x))


def _resident(block_shape, index_map):
    return pl.BlockSpec(block_shape, index_map, pipeline_mode=pl.Buffered(1))


def _dot(a, b):
    return jnp.dot(a, b, preferred_element_type=F32)


def _dot_nt(a, b):
    return lax.dot_general(a, b, NT_DIMS, preferred_element_type=F32)


def _dot_tn(a, b):
    return lax.dot_general(a, b, TN_DIMS, preferred_element_type=F32)


def _regroup_kernel(w_ref, o_ref):
    gate0 = 4 * D_GROUP
    gate1 = gate0 + 2 * N_HEADS
    rows = w_ref.shape[0]
    o_ref[:, :gate0] = w_ref[:, :gate0]
    o_ref[:, gate0:GATE_COL] = w_ref[:, gate1:gate1 + gate0]
    o_ref[:, GATE_COL:] = jnp.concatenate(
        [w_ref[:, gate0:gate1], jnp.zeros((rows, LANES - (gate1 - gate0)), w_ref.dtype)], axis=1)


def _regroup_in_proj_weight(w, *, tm):
    n, cols = w.shape
    return pl.pallas_call(
        _regroup_kernel,
        grid=(n // tm,),
        in_specs=[pl.BlockSpec((tm, cols), lambda i: (i, 0))],
        out_specs=pl.BlockSpec((tm, D_PROJ), lambda i: (i, 0)),
        out_shape=jax.ShapeDtypeStruct((n, D_PROJ), w.dtype),
        compiler_params=pltpu.CompilerParams(dimension_semantics=("arbitrary",)),
        name="regroup_w_in",
    )(w)


def _in_proj_kernel(x_ref, g_ref, b_ref, w_ref, bias_ref, o_ref):
    xn = _layer_norm(x_ref[...], g_ref[...], b_ref[...])
    o_ref[...] = _dot(xn.astype(BF16), w_ref[...]) + bias_ref[...]


def _in_proj(x, ln_g, ln_b, w, bias, *, tm):
    n = x.shape[0]
    const = lambda i: (0, 0)
    return pl.pallas_call(
        _in_proj_kernel,
        grid=(n // tm,),
        in_specs=[
            pl.BlockSpec((tm, D_MODEL), lambda i: (i, 0)),
            pl.BlockSpec((1, D_MODEL), const),
            pl.BlockSpec((1, D_MODEL), const),
            pl.BlockSpec((D_MODEL, D_PROJ), const),
            pl.BlockSpec((1, D_PROJ), const),
        ],
        out_specs=pl.BlockSpec((tm, D_PROJ), lambda i: (i, 0)),
        out_shape=jax.ShapeDtypeStruct((n, D_PROJ), F32),
        compiler_params=pltpu.CompilerParams(
            dimension_semantics=("arbitrary",), vmem_limit_bytes=VMEM_LIMIT),
        name="in_proj",
    )(x, ln_g, ln_b, w, bias)


def _block_rows(x, level, t_len, row_in_block):
    size = 2 << level
    if size > SUBLANES:
        pieces = [jnp.broadcast_to(x[j * size + row_in_block:j * size + row_in_block + 1, :],
                                   (size, x.shape[1])) for j in range(t_len // size)]
        return pieces[0] if len(pieces) == 1 else jnp.concatenate(pieces, axis=0)
    x3 = x.reshape(t_len // SUBLANES, SUBLANES, x.shape[1])
    sub = lax.broadcasted_iota(jnp.int32, x3.shape, 1)
    out = None
    for j in range(SUBLANES // size):
        row = jnp.broadcast_to(x3[:, j * size + row_in_block:j * size + row_in_block + 1, :], x3.shape)
        out = row if out is None else jnp.where(sub >= j * size, row, out)
    return out.reshape(x.shape)


def _interleave_halves(lower, upper, level, t_len):
    half = 1 << level
    if half >= SUBLANES:
        pieces = []
        for j in range(t_len // (2 * half)):
            pieces.append(lower[2 * half * j:2 * half * j + half])
            pieces.append(upper[2 * half * j + half:2 * half * (j + 1)])
        return jnp.concatenate(pieces, axis=0)
    rows = lax.broadcasted_iota(jnp.int32, lower.shape, 0)
    return jnp.where((rows & half) != 0, upper, lower)


def _run_interleaved(gens, stages_per_round):
    results = [None] * len(gens)
    live = [True] * len(gens)
    while any(live):
        for g, steps in enumerate(stages_per_round):
            for _ in range(steps):
                if live[g]:
                    try:
                        next(gens[g])
                    except StopIteration as stop:
                        results[g], live[g] = stop.value, False
    return results


def _cumsum_rows(tril16, x):
    hi = x.astype(BF16)
    rest = x - hi.astype(F32)
    mid = rest.astype(BF16)
    lo = (rest - mid.astype(F32)).astype(BF16)
    return _dot(tril16, hi) + _dot(tril16, mid) + _dot(tril16, lo)


def _activate(group, x, lb=None):
    if group == 1:
        return {1: x * (D_HEAD ** -0.5)}
    if group in (3, 7):
        return {group: _sigmoid(x)}
    if group == 4:
        return {4: x * _sigmoid(x)}
    if group == 5:
        f = lb + (1.0 - lb) * _sigmoid(x)
        return {5: jnp.log(f), KB_GROUP: (1.0 - lb) / (1.0 + jnp.exp(x)), F_GROUP: f}
    return {group: x}


def _mlstm_units(*, q, k, v, gate, z, zt, head, c_old, n_old, m0, g_norm, causal, t_len):
    idx = range(len(q))
    q16 = [q[i].astype(BF16) for i in idx]
    k16 = [k[i].astype(BF16) for i in idx]
    qk = [_dot_nt(q16[i], k16[i]) for i in idx]
    qc = [_dot(q16[i], c_old[i].astype(BF16)) for i in idx]
    bs_row = [zt[i][N_HEADS + head[i]:N_HEADS + head[i] + 1, :] - zt[i][head[i]:head[i] + 1, :]
              for i in idx]
    yield
    b_col = [z[i][:, N_HEADS + head[i]:N_HEADS + head[i] + 1] for i in idx]
    i_col = [z[i][:, head[i]:head[i] + 1] for i in idx]
    d = [jnp.where(causal, b_col[i] - bs_row[i], -jnp.inf) for i in idx]
    m_t = [jnp.maximum(b_col[i] + m0[i], jnp.max(d[i], axis=1, keepdims=True)) for i in idx]
    dec = [jnp.exp(b_col[i] + m0[i] - m_t[i]) for i in idx]
    sw = [jnp.exp(d[i] - m_t[i]) * qk[i] for i in idx]
    yield
    swv = [_dot(sw[i].astype(BF16), v[i].astype(BF16)) for i in idx]
    last = slice(t_len - 1, t_len)
    w_last = [jnp.exp(b_col[i][last] - b_col[i] + i_col[i] - m_t[i][last]) for i in idx]
    kv = [_dot_tn(k16[i], (w_last[i] * v[i]).astype(BF16)) for i in idx]
    yield
    c_new = [dec[i][last] * c_old[i] + kv[i] for i in idx]
    n_new = [dec[i][last] * n_old[i] + jnp.sum(w_last[i] * k[i], axis=0, keepdims=True) for i in idx]
    m_new = [m_t[i][last] for i in idx]
    den = [dec[i] * jnp.sum(q[i] * n_old[i], axis=1, keepdims=True)
           + jnp.sum(sw[i], axis=1, keepdims=True) for i in idx]
    hid = [(dec[i] * qc[i] + swv[i]) / jnp.maximum(jnp.abs(den[i]), _exp_neg(m_t[i])) for i in idx]
    yield
    rms = [lax.rsqrt(jnp.mean(hid[i] * hid[i], axis=1, keepdims=True) + RMS_EPS) for i in idx]
    out = [gate[i] * (hid[i] * rms[i] * g_norm[i]) for i in idx]
    return out, c_new, n_new, m_new


def _hgrn_units(*, qb, log_f, f, kb, iv, gate, s_old_t, g_norm, tril, level_of, t_len):
    idx = range(len(qb))
    n_levels = t_len.bit_length() - 1
    a = [_cumsum_rows(tril, log_f[i]) for i in idx]
    yield
    iv16 = [iv[i].astype(BF16) for i in idx]
    diag = [_dot_nt(qb[i].astype(BF16), kb[i].astype(BF16)) for i in idx]
    scores = [jnp.where(level_of == -2, diag[i], 0.0) for i in idx]
    for level in range(n_levels):
        yield
        x16 = []
        for i in idx:
            base = _interleave_halves(kb[i], qb[i], level, t_len)
            if level == 0:
                x = base * _interleave_halves(jnp.ones_like(f[i]), f[i], 0, t_len)
            else:
                ref = _block_rows(a[i], level, t_len, (1 << level) - 1)
                x = base * _exp_neg(jnp.abs(a[i] - ref))
            x16.append(x.astype(BF16))
        part = [_dot_nt(x16[i], x16[i]) for i in idx]
        scores = [jnp.where(level_of == level, part[i], scores[i]) for i in idx]
    yield
    last = slice(t_len - 1, t_len)
    q_in = [(qb[i] * jnp.exp(a[i])).astype(BF16) for i in idx]
    k_out = [(kb[i] * jnp.exp(a[i][last] - a[i])).astype(BF16) for i in idx]
    inter = [_dot_nt(q_in[i], s_old_t[i].astype(BF16)) for i in idx]
    intra = [_dot(scores[i].astype(BF16), iv16[i]) for i in idx]
    kv = [_dot_tn(iv16[i], k_out[i]) for i in idx]
    yield
    s_new_t = [jnp.exp(a[i][last]) * s_old_t[i] + kv[i] for i in idx]
    o = [inter[i] + intra[i] for i in idx]
    rms = [lax.rsqrt(jnp.mean(o[i] * o[i], axis=1, keepdims=True) + RMS_EPS) for i in idx]
    out = [gate[i] * (o[i] * rms[i] * g_norm[i]) for i in idx]
    return out, s_new_t


def _load_state(c0_ref, n0_ref, m0_ref, s0_ref, c_ref, n_ref, m_ref, s_ref, n_seq_blk):
    shared = c0_ref.shape[0] == 1 and n_seq_blk > 1
    for s in range(n_seq_blk):
        src = 0 if shared else s
        c_ref[s] = c0_ref[src]
        n_ref[s] = n0_ref[src]
        m_ref[s] = m0_ref[src]
        for h in range(N_HEADS):
            s_ref[s, h] = s0_ref[src, h].T


def _finish_state(s_ref, n_seq_blk):
    for s in range(n_seq_blk):
        for h in range(N_HEADS):
            s_ref[s, h] = s_ref[s, h].T


def _act_cols(group, h):
    base = group * D_GROUP if group < KB_GROUP else D_PROJ + (group - KB_GROUP) * D_GROUP
    return slice(base + h * D_HEAD, base + (h + 1) * D_HEAD)


def _mixer_body(p_ref, bf_ref, ga_ref, gb_ref, lb_ref, c_ref, n_ref, m_ref, s_ref,
                *, t_len, n_seq_blk, activated, side_stages=None):
    rows = lax.broadcasted_iota(jnp.int32, (t_len, t_len), 0)
    cols = lax.broadcasted_iota(jnp.int32, (t_len, t_len), 1)
    causal = cols <= rows
    tril = causal.astype(BF16)
    level_of = jnp.where(rows > cols, 31 - lax.clz(rows ^ cols), jnp.where(rows == cols, -2, -1))
    lane = lax.broadcasted_iota(jnp.int32, (t_len, LANES), 1)
    is_f = (lane >= N_HEADS) & (lane < 2 * N_HEADS)
    hd = lambda j, h: slice(j * D_GROUP + h * D_HEAD, j * D_GROUP + (h + 1) * D_HEAD)

    units = [(s, h) for s in range(n_seq_blk) for h in range(N_HEADS)]
    rs = lambda s: slice(s * t_len, (s + 1) * t_len)

    z_seq, zt_seq, n_seq, m_seq = [], [], [], []
    for s in range(n_seq_blk):
        gates = p_ref[rs(s), GATE_COL:GATE_COL + LANES]
        log_f = jnp.where(is_f, jax.nn.log_sigmoid(gates + bf_ref[...]), 0.0)
        cum_f = _cumsum_rows(tril, log_f)
        z_seq.append(jnp.where(is_f, cum_f, jnp.where(lane < N_HEADS, gates, 0.0)))
        zt_seq.append(z_seq[s].T)
        n_seq.append(n_ref[s])
        m_seq.append(m_ref[s])
    computed = {}

    def act(group):
        if activated:
            return [p_ref[rs(s), _act_cols(group, h)] for s, h in units]
        source = 5 if group >= KB_GROUP else group
        for s, h in units:
            if (source, s, h) not in computed:
                computed[source, s, h] = _activate(source, p_ref[rs(s), hd(source, h)],
                                                   lb_ref[:, hd(0, h)])
        return [computed[source, s, h][group] for s, h in units]

    mlstm = _mlstm_units(
        q=act(0), k=act(1), v=act(2), gate=act(3),
        z=[z_seq[s] for s, h in units], zt=[zt_seq[s] for s, h in units], head=[h for s, h in units],
        c_old=[c_ref[s, h] for s, h in units], n_old=[n_seq[s][h:h + 1, :] for s, h in units],
        m0=[m_seq[s][:, h:h + 1] for s, h in units], g_norm=[ga_ref[:, hd(0, h)] for s, h in units],
        causal=causal, t_len=t_len)
    hgrn = _hgrn_units(
        qb=act(4), log_f=act(5), f=act(F_GROUP), kb=act(KB_GROUP), iv=act(6), gate=act(7),
        s_old_t=[s_ref[s, h] for s, h in units],
        g_norm=[gb_ref[:, hd(0, h)] for s, h in units], tril=tril, level_of=level_of, t_len=t_len)
    gens, per_round = [mlstm, hgrn], [1, 2]
    if side_stages is not None:
        gens, per_round = gens + [side_stages], per_round + [SIDE_STAGES_PER_ROUND]
    results = _run_interleaved(gens, per_round)
    (outs, c_new, n_new, m_new), (outs_b, s_new_t) = results[:2]

    for i, (s, h) in enumerate(units):
        c_ref[s, h] = c_new[i]
        s_ref[s, h] = s_new_t[i]
    head_lane = lax.broadcasted_iota(jnp.int32, (1, N_HEADS), 1)
    for s in range(n_seq_blk):
        n_ref[s] = jnp.concatenate(n_new[s * N_HEADS:(s + 1) * N_HEADS], axis=0)
        m_row = m_seq[s]
        for h in range(N_HEADS):
            m_row = jnp.where(head_lane == h, m_new[s * N_HEADS + h], m_row)
        m_ref[s] = m_row
    return outs, outs_b


def _mixer_kernel(p_ref, c0_ref, n0_ref, m0_ref, s0_ref, bf_ref, ga_ref, gb_ref, lb_ref,
                  mix_ref, c_ref, n_ref, m_ref, s_ref, *, t_len, n_chunks, n_seq_blk):
    chunk = pl.program_id(1)

    @pl.when(chunk == 0)
    def _():
        _load_state(c0_ref, n0_ref, m0_ref, s0_ref, c_ref, n_ref, m_ref, s_ref, n_seq_blk)

    outs_a, outs_b = _mixer_body(p_ref, bf_ref, ga_ref, gb_ref, lb_ref, c_ref, n_ref, m_ref, s_ref,
                                 t_len=t_len, n_seq_blk=n_seq_blk, activated=False)
    for i in range(n_seq_blk * N_HEADS):
        s, h = divmod(i, N_HEADS)
        rows = slice(s * t_len, (s + 1) * t_len)
        mix_ref[rows, h * D_HEAD:(h + 1) * D_HEAD] = outs_a[i].astype(mix_ref.dtype)
        mix_ref[rows, D_GROUP + h * D_HEAD:D_GROUP + (h + 1) * D_HEAD] = outs_b[i].astype(mix_ref.dtype)

    @pl.when(chunk == n_chunks - 1)
    def _():
        _finish_state(s_ref, n_seq_blk)


def _mixer(proj, c0, n0, m0, s0, bf_row, ga, gb, lb, *, n_seq, n_chunks, t_len, n_seq_blk,
           shared_init):
    assert not shared_init or n_seq_blk == 1
    assert n_chunks == 1 or n_seq_blk == 1
    nb = n_seq_blk
    init = (lambda b, c: (0, 0, 0, 0)) if shared_init else (lambda b, c: (b, 0, 0, 0))
    init3 = (lambda b, c: (0, 0, 0)) if shared_init else (lambda b, c: (b, 0, 0))
    const = lambda b, c: (0, 0)
    state4 = pl.BlockSpec((nb, N_HEADS, D_HEAD, D_HEAD), lambda b, c: (b, 0, 0, 0))
    return pl.pallas_call(
        functools.partial(_mixer_kernel, t_len=t_len, n_chunks=n_chunks, n_seq_blk=nb),
        grid=(n_seq // nb, n_chunks),
        in_specs=[
            pl.BlockSpec((nb * t_len, D_PROJ), lambda b, c: (b * n_chunks + c, 0)),
            pl.BlockSpec((nb, N_HEADS, D_HEAD, D_HEAD), init),
            pl.BlockSpec((nb, N_HEADS, D_HEAD), init3),
            pl.BlockSpec((nb, 1, N_HEADS), init3),
            pl.BlockSpec((nb, N_HEADS, D_HEAD, D_HEAD), init),
            pl.BlockSpec((1, LANES), const),
            pl.BlockSpec((1, D_GROUP), const),
            pl.BlockSpec((1, D_GROUP), const),
            pl.BlockSpec((1, D_GROUP), const),
        ],
        out_specs=[
            pl.BlockSpec((nb * t_len, D_MODEL), lambda b, c: (b * n_chunks + c, 0)),
            state4,
            pl.BlockSpec((nb, N_HEADS, D_HEAD), lambda b, c: (b, 0, 0)),
            pl.BlockSpec((nb, 1, N_HEADS), lambda b, c: (b, 0, 0)),
            state4,
        ],
        out_shape=[
            jax.ShapeDtypeStruct((n_seq * n_chunks * t_len, D_MODEL), BF16),
            jax.ShapeDtypeStruct((n_seq, N_HEADS, D_HEAD, D_HEAD), F32),
            jax.ShapeDtypeStruct((n_seq, N_HEADS, D_HEAD), F32),
            jax.ShapeDtypeStruct((n_seq, 1, N_HEADS), F32),
            jax.ShapeDtypeStruct((n_seq, N_HEADS, D_HEAD, D_HEAD), F32),
        ],
        compiler_params=pltpu.CompilerParams(
            dimension_semantics=("arbitrary", "arbitrary"), vmem_limit_bytes=VMEM_LIMIT),
        name=f"mixer_t{t_len}",
    )(proj, c0, n0, m0, s0, bf_row, ga, gb, lb)


def _meta_kernel(x_ref, ge_ref, be_ref, win_ref, bin_ref, bf_ref, ga_ref, gb_ref, lb_ref,
                 wout_ref, bout_ref, g1_ref, b1_ref, wu_ref, bu_ref,
                 c_ref, n_ref, m_ref, s_ref, conv_ref, proj_scr):
    t_len = x_ref.shape[0]
    xn = _layer_norm(x_ref[...], ge_ref[...], be_ref[...])
    proj_scr[...] = _dot(xn.astype(BF16), win_ref[...]) + bin_ref[...]
    for ref in (c_ref, n_ref, m_ref, s_ref):
        ref[...] = jnp.zeros(ref.shape, ref.dtype)
    outs_a, outs_b = _mixer_body(proj_scr, bf_ref, ga_ref, gb_ref, lb_ref, c_ref, n_ref, m_ref, s_ref,
                                 t_len=t_len, n_seq_blk=1, activated=False)
    _finish_state(s_ref, 1)
    mix = jnp.concatenate([o.astype(BF16) for o in outs_a + outs_b], axis=1)
    y = _dot(mix, wout_ref[...]) + bout_ref[...]
    x1 = _layer_norm(ALPHA * xn + y, g1_ref[...], b1_ref[...])
    u = _dot(x1.astype(BF16), wu_ref[...]) + bu_ref[...]
    conv_ref[0] = u[t_len - (CONV_W - 1):, :]


def _meta_state(x, ln_e_g, ln_e_b, w_in, b_in, bf_row, ga, gb, lb, w_out, b_out, ln_g, ln_b,
                w_up, b_up):
    t_len = x.shape[0]
    full = lambda shape: _resident(shape, lambda i: (0,) * len(shape))
    vec, grp = full((1, D_MODEL)), full((1, D_GROUP))
    state4 = (1, N_HEADS, D_HEAD, D_HEAD)
    return pl.pallas_call(
        _meta_kernel,
        grid=(1,),
        in_specs=[full((t_len, D_MODEL)), vec, vec, full((D_MODEL, D_PROJ)), full((1, D_PROJ)),
                  full((1, LANES)), grp, grp, grp, full((D_MODEL, D_MODEL)), vec, vec, vec,
                  full((D_MODEL, D_FF)), full((1, D_FF))],
        out_specs=[pl.BlockSpec(shape, lambda i, rank=len(shape): (0,) * rank)
                   for shape in (state4, (1, N_HEADS, D_HEAD), (1, 1, N_HEADS), state4,
                                 (1, CONV_W - 1, D_FF))],
        out_shape=[
            jax.ShapeDtypeStruct(state4, F32),
            jax.ShapeDtypeStruct((1, N_HEADS, D_HEAD), F32),
            jax.ShapeDtypeStruct((1, 1, N_HEADS), F32),
            jax.ShapeDtypeStruct(state4, F32),
            jax.ShapeDtypeStruct((1, CONV_W - 1, D_FF), F32),
        ],
        scratch_shapes=[pltpu.VMEM((t_len, D_PROJ), F32)],
        compiler_params=pltpu.CompilerParams(
            dimension_semantics=("arbitrary",), vmem_limit_bytes=VMEM_LIMIT),
        name="meta_state",
    )(x, ln_e_g, ln_e_b, w_in, b_in, bf_row, ga, gb, lb, w_out, b_out, ln_g, ln_b, w_up, b_up)


def _in_proj_stages(x_ref, g_ref, b_ref, w_ref, bias_ref, lb_ref, act_ref, xn_ref):
    assert D_GROUP % IN_PROJ_STAGE_COLS == 0
    xn = _layer_norm(x_ref[...].reshape(xn_ref.shape), g_ref[...], b_ref[...])
    xn_ref[...] = xn
    x16 = xn.astype(BF16)
    for lo in range(0, D_PROJ, IN_PROJ_STAGE_COLS):
        hi = min(lo + IN_PROJ_STAGE_COLS, D_PROJ)
        yield
        block = _dot(x16, w_ref[:, lo:hi]) + bias_ref[:, lo:hi]
        group, off = divmod(lo, D_GROUP)
        if lo >= GATE_COL:
            act_ref[:, lo:hi] = block
            continue
        for dst, val in _activate(group, block, lb_ref[:, off:off + hi - lo]).items():
            base = _act_cols(dst, 0).start + off
            act_ref[:, base:base + hi - lo] = val


def _prompt_kernel(x0_ref, xnext_ref, ge_ref, be_ref, win_ref, bin_ref, c0_ref, n0_ref, m0_ref, s0_ref,
                   bf_ref, ga_ref, gb_ref, lb_ref, wout_ref, bout_ref, g1_ref, b1_ref,
                   x1_ref, c_ref, n_ref, m_ref, s_ref, proj_scr, xn_scr, proj_alt, xn_alt,
                   *, t_len, n_chunks, n_seq_blk):
    chunk = pl.program_id(1)
    step = pl.program_id(0) * n_chunks + chunk

    @pl.when(step == 0)
    def _():
        first = _in_proj_stages(x0_ref, ge_ref, be_ref, win_ref, bin_ref, lb_ref, proj_scr, xn_scr)
        _run_interleaved([first], [1])

    @pl.when(chunk == 0)
    def _():
        _load_state(c0_ref, n0_ref, m0_ref, s0_ref, c_ref, n_ref, m_ref, s_ref, n_seq_blk)

    def tile(proj_cur, xn_cur, proj_next, xn_next):
        next_proj = _in_proj_stages(xnext_ref, ge_ref, be_ref, win_ref, bin_ref, lb_ref,
                                    proj_next, xn_next)
        outs_a, outs_b = _mixer_body(
            proj_cur, bf_ref, ga_ref, gb_ref, lb_ref, c_ref, n_ref, m_ref, s_ref,
            t_len=t_len, n_seq_blk=n_seq_blk, activated=True, side_stages=next_proj)
        mix = jnp.concatenate(
            [jnp.concatenate([o.astype(BF16) for o in outs_a[s * N_HEADS:(s + 1) * N_HEADS]
                              + outs_b[s * N_HEADS:(s + 1) * N_HEADS]], axis=1)
             for s in range(n_seq_blk)], axis=0)
        y = _dot(mix, wout_ref[...]) + bout_ref[...]
        x1 = _layer_norm(ALPHA * xn_cur[...] + y, g1_ref[...], b1_ref[...])
        x1_ref[...] = x1.reshape(x1_ref.shape)

    @pl.when(step % 2 == 0)
    def _():
        tile(proj_scr, xn_scr, proj_alt, xn_alt)

    @pl.when(step % 2 == 1)
    def _():
        tile(proj_alt, xn_alt, proj_scr, xn_scr)

    @pl.when(chunk == n_chunks - 1)
    def _():
        _finish_state(s_ref, n_seq_blk)


def _prompt_mixer(x, ln_e_g, ln_e_b, w_in, b_in, c0, n0, m0, s0, bf_row, ga, gb, lb,
                  w_out, b_out, ln_g, ln_b, *, n_chunks, t_len, n_seq_blk):
    n_seq = x.shape[0]
    nb = n_seq_blk
    n_tiles = (n_seq // nb) * n_chunks
    const = lambda b, c: (0, 0)
    init4 = lambda b, c: (0, 0, 0, 0)
    init3 = lambda b, c: (0, 0, 0)
    vec = pl.BlockSpec((1, D_MODEL), const)
    grp = pl.BlockSpec((1, D_GROUP), const)
    state4 = pl.BlockSpec((nb, N_HEADS, D_HEAD, D_HEAD), lambda b, c: (b, 0, 0, 0))

    def next_tile(b, c):
        nxt = jnp.minimum(b * n_chunks + c + 1, n_tiles - 1)
        return (nxt // n_chunks, nxt % n_chunks, 0)

    return pl.pallas_call(
        functools.partial(_prompt_kernel, t_len=t_len, n_chunks=n_chunks, n_seq_blk=nb),
        grid=(n_seq // nb, n_chunks),
        in_specs=[
            _resident((nb, t_len, D_MODEL), init3),
            pl.BlockSpec((nb, t_len, D_MODEL), next_tile),
            vec, vec,
            _resident((D_MODEL, D_PROJ), const),
            pl.BlockSpec((1, D_PROJ), const),
            pl.BlockSpec((1, N_HEADS, D_HEAD, D_HEAD), init4),
            pl.BlockSpec((1, N_HEADS, D_HEAD), init3),
            pl.BlockSpec((1, 1, N_HEADS), init3),
            pl.BlockSpec((1, N_HEADS, D_HEAD, D_HEAD), init4),
            pl.BlockSpec((1, LANES), const),
            grp, grp, grp,
            _resident((D_MODEL, D_MODEL), const),
            vec, vec, vec,
        ],
        out_specs=[
            pl.BlockSpec((nb, t_len, D_MODEL), lambda b, c: (b, c, 0)),
            state4,
            pl.BlockSpec((nb, N_HEADS, D_HEAD), lambda b, c: (b, 0, 0)),
            pl.BlockSpec((nb, 1, N_HEADS), lambda b, c: (b, 0, 0)),
            state4,
        ],
        out_shape=[
            jax.ShapeDtypeStruct(x.shape, F32),
            jax.ShapeDtypeStruct((n_seq, N_HEADS, D_HEAD, D_HEAD), F32),
            jax.ShapeDtypeStruct((n_seq, N_HEADS, D_HEAD), F32),
            jax.ShapeDtypeStruct((n_seq, 1, N_HEADS), F32),
            jax.ShapeDtypeStruct((n_seq, N_HEADS, D_HEAD, D_HEAD), F32),
        ],
        scratch_shapes=[pltpu.VMEM((nb * t_len, D_ACT), F32), pltpu.VMEM((nb * t_len, D_MODEL), F32),
                        pltpu.VMEM((nb * t_len, D_ACT), F32), pltpu.VMEM((nb * t_len, D_MODEL), F32)],
        compiler_params=pltpu.CompilerParams(
            dimension_semantics=("arbitrary", "arbitrary"), vmem_limit_bytes=VMEM_LIMIT),
        name="prompt_mixer",
    )(x, x, ln_e_g, ln_e_b, w_in, b_in, c0, n0, m0, s0, bf_row, ga, gb, lb, w_out, b_out, ln_g, ln_b)


def _out_proj_kernel(x_ref, mix_ref, ge_ref, be_ref, w_ref, bias_ref, g_ref, b_ref, o_ref):
    xn = _layer_norm(x_ref[...], ge_ref[...], be_ref[...])
    y = _dot(mix_ref[...], w_ref[...]) + bias_ref[...]
    o_ref[...] = _layer_norm(ALPHA * xn + y, g_ref[...], b_ref[...])


def _out_proj(x, mix, ln_e_g, ln_e_b, w, bias, ln_g, ln_b, *, tm):
    n = x.shape[0]
    const = lambda i: (0, 0)
    row = pl.BlockSpec((tm, D_MODEL), lambda i: (i, 0))
    vec = pl.BlockSpec((1, D_MODEL), const)
    return pl.pallas_call(
        _out_proj_kernel,
        grid=(n // tm,),
        in_specs=[row, row, vec, vec, pl.BlockSpec((D_MODEL, D_MODEL), const), vec, vec, vec],
        out_specs=row,
        out_shape=jax.ShapeDtypeStruct((n, D_MODEL), F32),
        compiler_params=pltpu.CompilerParams(
            dimension_semantics=("arbitrary",), vmem_limit_bytes=VMEM_LIMIT),
        name="out_proj",
    )(x, mix, ln_e_g, ln_e_b, w, bias, ln_g, ln_b)


def _ffn_kernel(x_ref, cs_ref, wu_ref, bu_ref, wc_ref, bc_ref, wd_ref, bd_ref, g_ref, b_ref,
                y_ref, nc_ref, full_ref, *, n_seq_blk, t_len):
    hist = SUBLANES - (CONV_W - 1)

    @pl.when(pl.program_id(1) == 0)
    def _():
        full_ref[:, hist:SUBLANES, :] = cs_ref[...]

    x = x_ref[...]
    up = _dot(x.astype(BF16), wu_ref[...]) + bu_ref[...]
    u = up[:, :D_FF].reshape(n_seq_blk, t_len, D_FF)
    gate = up[:, D_FF:].reshape(n_seq_blk, t_len, D_FF)
    full_ref[:, SUBLANES:SUBLANES + t_len, :] = u
    conv = bc_ref[...] + u * wc_ref[CONV_W - 1:CONV_W, :]
    for j in range(CONV_W - 1):
        conv = conv + full_ref[:, hist + j:hist + j + t_len, :] * wc_ref[j:j + 1, :]
    last = full_ref[:, hist + t_len:SUBLANES + t_len, :]
    nc_ref[...] = last
    full_ref[:, hist:SUBLANES, :] = last
    act = (conv * _sigmoid(conv) * gate).reshape(n_seq_blk * t_len, D_FF)
    ffn = _dot(act.astype(BF16), wd_ref[...]) + bd_ref[...]
    y_ref[...] = _layer_norm(ALPHA * x + ffn, g_ref[...], b_ref[...])


def _ffn(x, conv_state, w_up, b_up, w_conv, b_conv, w_down, b_down, ln_g, ln_b,
         *, n_seq, seq_len, n_seq_blk, t_len, shared_init):
    n_t = seq_len // t_len
    rows = n_seq_blk * t_len
    const = lambda s, t: (0, 0)
    cs_map = (lambda s, t: (0, 0, 0)) if shared_init else (lambda s, t: (s, 0, 0))
    row = pl.BlockSpec((rows, D_MODEL), lambda s, t: (s * n_t + t, 0))
    vec = pl.BlockSpec((1, D_MODEL), const)
    return pl.pallas_call(
        functools.partial(_ffn_kernel, n_seq_blk=n_seq_blk, t_len=t_len),
        grid=(n_seq // n_seq_blk, n_t),
        in_specs=[
            row,
            pl.BlockSpec((n_seq_blk, CONV_W - 1, D_FF), cs_map),
            _resident((D_MODEL, 2 * D_FF), const),
            pl.BlockSpec((1, 2 * D_FF), const),
            pl.BlockSpec((CONV_W, D_FF), const),
            pl.BlockSpec((1, D_FF), const),
            _resident((D_FF, D_MODEL), const),
            vec, vec, vec,
        ],
        out_specs=[row, pl.BlockSpec((n_seq_blk, CONV_W - 1, D_FF), lambda s, t: (s, 0, 0))],
        out_shape=[
            jax.ShapeDtypeStruct((n_seq * seq_len, D_MODEL), F32),
            jax.ShapeDtypeStruct((n_seq, CONV_W - 1, D_FF), F32),
        ],
        scratch_shapes=[pltpu.VMEM((n_seq_blk, SUBLANES + t_len, D_FF), F32)],
        compiler_params=pltpu.CompilerParams(
            dimension_semantics=("arbitrary", "arbitrary"), vmem_limit_bytes=VMEM_LIMIT),
        name=f"ffn_t{t_len}",
    )(x, conv_state, w_up, b_up, w_conv, b_conv, w_down, b_down, ln_g, ln_b)


def kernel(x_prompt, x_sample, state_mlstm_C, state_mlstm_n, state_mlstm_m, state_hgrn_S, state_ffn_conv, meta_tokens, ln_emb_g, ln_emb_b, w_in, b_in, b_fgate_a, g_norm_a, g_norm_b, hgrn_lb_logits, w_out, b_out, ln1_g, ln1_b, w_up, b_up, w_conv, b_conv, w_down, b_down, ln2_g, ln2_b):
    assert w_in.shape[0] == DEPTH == 1
    n_prompt, seq, _ = x_prompt.shape
    n_sample, dec_seq, _ = x_sample.shape
    row = lambda v: v.reshape(1, -1).astype(F32)

    gate0 = 4 * D_GROUP
    gate1 = gate0 + 2 * N_HEADS
    pad = D_PROJ - w_in.shape[2]
    w_in_p = _regroup_in_proj_weight(w_in[0].astype(BF16), tm=256)
    b_in_p = jnp.concatenate(
        [b_in[0][:gate0], b_in[0][gate1:], b_in[0][gate0:gate1], jnp.zeros((pad,), b_in.dtype)]
    ).reshape(1, D_PROJ).astype(F32)
    bf_row = jnp.zeros((1, LANES), F32).at[0, N_HEADS:2 * N_HEADS].set(b_fgate_a[0].astype(F32))
    lb = jnp.cumsum(jax.nn.softmax(hgrn_lb_logits.astype(F32), axis=0), axis=0)[0].reshape(1, D_GROUP)
    ga, gb = row(g_norm_a[0]), row(g_norm_b[0])
    ln_e = (row(ln_emb_g), row(ln_emb_b))
    out_p = (w_out[0].astype(BF16), row(b_out[0]), row(ln1_g[0]), row(ln1_b[0]))
    ffn_p = (w_up[0].astype(BF16), row(b_up[0]), w_conv[0].astype(F32), row(b_conv[0]),
             w_down[0].astype(BF16), row(b_down[0]), row(ln2_g[0]), row(ln2_b[0]))

    def layer(x_rows, mixer_state, conv_state, *, n_seq, seq_len, t_mix, mix_seq_blk, tm,
              ffn_seq_blk, ffn_t, shared_init):
        proj = _in_proj(x_rows, *ln_e, w_in_p, b_in_p, tm=tm)
        mix, c_new, n_new, m_new, s_new = _mixer(
            proj, *mixer_state, bf_row, ga, gb, lb, n_seq=n_seq, n_chunks=seq_len // t_mix,
            t_len=t_mix, n_seq_blk=mix_seq_blk, shared_init=shared_init)
        x1 = _out_proj(x_rows, mix, *ln_e, *out_p, tm=tm)
        y, conv_new = _ffn(x1, conv_state, *ffn_p, n_seq=n_seq, seq_len=seq_len,
                           n_seq_blk=ffn_seq_blk, t_len=ffn_t, shared_init=shared_init)
        return y, c_new, n_new, m_new, s_new, conv_new

    c_m, n_m, m_m, s_m, conv_m = _meta_state(
        meta_tokens.astype(F32), *ln_e, w_in_p, b_in_p, bf_row, ga, gb, lb, *out_p, ffn_p[0], ffn_p[1])

    x1_p, c_p, n_p, m_p, s_p = _prompt_mixer(
        x_prompt.astype(F32), *ln_e, w_in_p, b_in_p, c_m, n_m, m_m, s_m, bf_row, ga, gb, lb, *out_p,
        n_chunks=seq // PROMPT_CHUNK, t_len=PROMPT_CHUNK, n_seq_blk=PROMPT_SEQS_PER_STEP)
    y_p, conv_p = _ffn(x1_p.reshape(n_prompt * seq, D_MODEL), conv_m, *ffn_p, n_seq=n_prompt,
                       seq_len=seq, n_seq_blk=1, t_len=512, shared_init=True)

    sample_state = (state_mlstm_C[0].astype(F32), state_mlstm_n[0].astype(F32),
                    state_mlstm_m[0].astype(F32).reshape(n_sample, 1, N_HEADS),
                    state_hgrn_S[0].astype(F32))
    y_s, c_s, n_s, m_s, s_s, conv_s = layer(
        x_sample.reshape(n_sample * dec_seq, D_MODEL), sample_state, state_ffn_conv[0].astype(F32),
        n_seq=n_sample, seq_len=dec_seq, t_mix=dec_seq, mix_seq_blk=8, tm=256, ffn_seq_blk=32,
        ffn_t=dec_seq, shared_init=False)

    lead = lambda v: v[None]
    return (y_p.reshape(n_prompt, seq, D_MODEL), y_s.reshape(n_sample, dec_seq, D_MODEL),
            lead(c_p), lead(n_p), lead(m_p.reshape(n_prompt, N_HEADS)), lead(s_p), lead(conv_p),
            lead(c_s), lead(n_s), lead(m_s.reshape(n_sample, N_HEADS)), lead(s_s), lead(conv_s))
```

```python
import functools

import jax
import jax.numpy as jnp
from jax import lax
from jax.experimental import pallas as pl
from jax.experimental.pallas import tpu as pltpu

D_MODEL = 1024
N_META = 16
N_HEADS = 4
D_HEAD = 128
D_GROUP = N_HEADS * D_HEAD
D_FF = 2816
CONV_W = 3
DEPTH = 1
ALPHA = (2.0 * DEPTH) ** 0.25
LN_EPS = 1e-5
RMS_EPS = 1e-6
NEG_LOG2_E = -1.4426950408889634

LANES = 128
SUBLANES = 8
GATE_COL = 8 * D_GROUP
D_PROJ = GATE_COL + LANES
KB_GROUP, F_GROUP = 8, 9
D_ACT = D_PROJ + 2 * D_GROUP
IN_PROJ_STAGE_COLS = 256
SIDE_STAGES_PER_ROUND = 3
PROMPT_SEQS_PER_STEP = 2
PROMPT_CHUNK = 128
VMEM_LIMIT = 56 * 1024 * 1024

F32 = jnp.float32
BF16 = jnp.bfloat16
NT_DIMS = (((1,), (1,)), ((), ()))
TN_DIMS = (((0,), (0,)), ((), ()))


def _layer_norm(x, g, b):
    mu = jnp.mean(x, axis=-1, keepdims=True)
    xc = x - mu
    var = jnp.mean(xc * xc, axis=-1, keepdims=True)
    return xc * lax.rsqrt(var + LN_EPS) * g + b


def _exp_neg(x):
    return jnp.exp2(x * NEG_LOG2_E)


def _sigmoid(x):
    return 1.0 / (1.0 + _exp_neg(x))


def _resident(block_shape, index_map):
    return pl.BlockSpec(block_shape, index_map, pipeline_mode=pl.Buffered(1))


def _dot(a, b):
    return jnp.dot(a, b, preferred_element_type=F32)


def _dot_nt(a, b):
    return lax.dot_general(a, b, NT_DIMS, preferred_element_type=F32)


def _dot_tn(a, b):
    return lax.dot_general(a, b, TN_DIMS, preferred_element_type=F32)


def _regroup_kernel(w_ref, o_ref):
    gate0 = 4 * D_GROUP
    gate1 = gate0 + 2 * N_HEADS
    rows = w_ref.shape[0]
    o_ref[:, :gate0] = w_ref[:, :gate0]
    o_ref[:, gate0:GATE_COL] = w_ref[:, gate1:gate1 + gate0]
    o_ref[:, GATE_COL:] = jnp.concatenate(
        [w_ref[:, gate0:gate1], jnp.zeros((rows, LANES - (gate1 - gate0)), w_ref.dtype)], axis=1)


def _regroup_in_proj_weight(w, *, tm):
    n, cols = w.shape
    return pl.pallas_call(
        _regroup_kernel,
        grid=(n // tm,),
        in_specs=[pl.BlockSpec((tm, cols), lambda i: (i, 0))],
        out_specs=pl.BlockSpec((tm, D_PROJ), lambda i: (i, 0)),
        out_shape=jax.ShapeDtypeStruct((n, D_PROJ), w.dtype),
        compiler_params=pltpu.CompilerParams(dimension_semantics=("arbitrary",)),
        name="regroup_w_in",
    )(w)


def _in_proj_kernel(x_ref, g_ref, b_ref, w_ref, bias_ref, o_ref):
    xn = _layer_norm(x_ref[...], g_ref[...], b_ref[...])
    o_ref[...] = _dot(xn.astype(BF16), w_ref[...]) + bias_ref[...]


def _in_proj(x, ln_g, ln_b, w, bias, *, tm):
    n = x.shape[0]
    const = lambda i: (0, 0)
    return pl.pallas_call(
        _in_proj_kernel,
        grid=(n // tm,),
        in_specs=[
            pl.BlockSpec((tm, D_MODEL), lambda i: (i, 0)),
            pl.BlockSpec((1, D_MODEL), const),
            pl.BlockSpec((1, D_MODEL), const),
            pl.BlockSpec((D_MODEL, D_PROJ), const),
            pl.BlockSpec((1, D_PROJ), const),
        ],
        out_specs=pl.BlockSpec((tm, D_PROJ), lambda i: (i, 0)),
        out_shape=jax.ShapeDtypeStruct((n, D_PROJ), F32),
        compiler_params=pltpu.CompilerParams(
            dimension_semantics=("arbitrary",), vmem_limit_bytes=VMEM_LIMIT),
        name="in_proj",
    )(x, ln_g, ln_b, w, bias)


def _block_rows(x, level, t_len, row_in_block):
    size = 2 << level
    if size > SUBLANES:
        pieces = [jnp.broadcast_to(x[j * size + row_in_block:j * size + row_in_block + 1, :],
                                   (size, x.shape[1])) for j in range(t_len // size)]
        return pieces[0] if len(pieces) == 1 else jnp.concatenate(pieces, axis=0)
    x3 = x.reshape(t_len // SUBLANES, SUBLANES, x.shape[1])
    sub = lax.broadcasted_iota(jnp.int32, x3.shape, 1)
    out = None
    for j in range(SUBLANES // size):
        row = jnp.broadcast_to(x3[:, j * size + row_in_block:j * size + row_in_block + 1, :], x3.shape)
        out = row if out is None else jnp.where(sub >= j * size, row, out)
    return out.reshape(x.shape)


def _interleave_halves(lower, upper, level, t_len):
    half = 1 << level
    if half >= SUBLANES:
        pieces = []
        for j in range(t_len // (2 * half)):
            pieces.append(lower[2 * half * j:2 * half * j + half])
            pieces.append(upper[2 * half * j + half:2 * half * (j + 1)])
        return jnp.concatenate(pieces, axis=0)
    rows = lax.broadcasted_iota(jnp.int32, lower.shape, 0)
    return jnp.where((rows & half) != 0, upper, lower)


def _run_interleaved(gens, stages_per_round):
    results = [None] * len(gens)
    live = [True] * len(gens)
    while any(live):
        for g, steps in enumerate(stages_per_round):
            for _ in range(steps):
                if live[g]:
                    try:
                        next(gens[g])
                    except StopIteration as stop:
                        results[g], live[g] = stop.value, False
    return results


def _cumsum_rows(tril16, x):
    hi = x.astype(BF16)
    rest = x - hi.astype(F32)
    mid = rest.astype(BF16)
    lo = (rest - mid.astype(F32)).astype(BF16)
    return _dot(tril16, hi) + _dot(tril16, mid) + _dot(tril16, lo)


def _activate(group, x, lb=None):
    if group == 1:
        return {1: x * (D_HEAD ** -0.5)}
    if group in (3, 7):
        return {group: _sigmoid(x)}
    if group == 4:
        return {4: x * _sigmoid(x)}
    if group == 5:
        f = lb + (1.0 - lb) * _sigmoid(x)
        return {5: jnp.log(f), KB_GROUP: (1.0 - lb) / (1.0 + jnp.exp(x)), F_GROUP: f}
    return {group: x}


def _mlstm_units(*, q, k, v, gate, z, zt, m0_row, c_old, n_old, g_norm, causal, lane, t_len):
    idx = range(len(q))
    seq = [i // N_HEADS for i in idx]
    b_lane = [N_HEADS + i % N_HEADS for i in idx]
    q16 = [q[i].astype(BF16) for i in idx]
    k16 = [k[i].astype(BF16) for i in idx]
    qk = [_dot_nt(q16[i], k16[i]) for i in idx]
    qc = [_dot(q16[i], c_old[i].astype(BF16)) for i in idx]
    bs_row = [zt[seq[i]][b_lane[i]:b_lane[i] + 1, :] - zt[seq[i]][i % N_HEADS:i % N_HEADS + 1, :]
              for i in idx]
    yield
    col = lambda per_seq, i: per_seq[seq[i]][:, b_lane[i]:b_lane[i] + 1]
    d = [jnp.where(causal, col(z, i) - bs_row[i], -jnp.inf) for i in idx]
    row_max = [jnp.max(d[i], axis=1, keepdims=True) for i in idx]
    last = slice(t_len - 1, t_len)
    m_t_seq, dec_seq, floor_seq, w_last_seq = [], [], [], []
    for s in range(len(z)):
        r = z[s] + m0_row[s]
        d_max = jnp.full(r.shape, -jnp.inf, F32)
        for h in range(N_HEADS):
            d_max = jnp.where(lane == N_HEADS + h, row_max[s * N_HEADS + h], d_max)
        m_t = jnp.maximum(r, d_max)
        i_gate = pltpu.roll(z[s], N_HEADS, axis=1)
        m_t_seq.append(m_t)
        dec_seq.append(jnp.exp(r - m_t))
        floor_seq.append(_exp_neg(m_t))
        w_last_seq.append(jnp.exp(z[s][last] - z[s] + i_gate - m_t[last]))
    m_t = [col(m_t_seq, i) for i in idx]
    dec = [col(dec_seq, i) for i in idx]
    w_last = [col(w_last_seq, i) for i in idx]
    sw = [jnp.exp(d[i] - m_t[i]) * qk[i] for i in idx]
    yield
    swv = [_dot(sw[i].astype(BF16), v[i].astype(BF16)) for i in idx]
    kv = [_dot_tn(k16[i], (w_last[i] * v[i]).astype(BF16)) for i in idx]
    yield
    c_new = [dec[i][last] * c_old[i] + kv[i] for i in idx]
    n_new = [dec[i][last] * n_old[i] + jnp.sum(w_last[i] * k[i], axis=0, keepdims=True) for i in idx]
    m_new = [m_t[i][last] for i in idx]
    den = [dec[i] * jnp.sum(q[i] * n_old[i], axis=1, keepdims=True)
           + jnp.sum(sw[i], axis=1, keepdims=True) for i in idx]
    hid = [(dec[i] * qc[i] + swv[i]) / jnp.maximum(jnp.abs(den[i]), col(floor_seq, i)) for i in idx]
    yield
    rms = [lax.rsqrt(jnp.mean(hid[i] * hid[i], axis=1, keepdims=True) + RMS_EPS) for i in idx]
    out = [gate[i] * (hid[i] * rms[i] * g_norm[i]) for i in idx]
    return out, c_new, n_new, m_new


def _hgrn_units(*, qb, log_f, f, kb, iv, gate, s_old_t, g_norm, tril, level_of, t_len):
    idx = range(len(qb))
    n_levels = t_len.bit_length() - 1
    a = [_cumsum_rows(tril, log_f[i]) for i in idx]
    yield
    iv16 = [iv[i].astype(BF16) for i in idx]
    diag = [_dot_nt(qb[i].astype(BF16), kb[i].astype(BF16)) for i in idx]
    scores = [jnp.where(level_of == -2, diag[i], 0.0) for i in idx]
    for level in range(n_levels):
        yield
        x16 = []
        for i in idx:
            base = _interleave_halves(kb[i], qb[i], level, t_len)
            if level == 0:
                x = base * _interleave_halves(jnp.ones_like(f[i]), f[i], 0, t_len)
            else:
                ref = _block_rows(a[i], level, t_len, (1 << level) - 1)
                x = base * _exp_neg(jnp.abs(a[i] - ref))
            x16.append(x.astype(BF16))
        part = [_dot_nt(x16[i], x16[i]) for i in idx]
        scores = [jnp.where(level_of == level, part[i], scores[i]) for i in idx]
    yield
    last = slice(t_len - 1, t_len)
    q_in = [(qb[i] * jnp.exp(a[i])).astype(BF16) for i in idx]
    k_out = [(kb[i] * jnp.exp(a[i][last] - a[i])).astype(BF16) for i in idx]
    inter = [_dot_nt(q_in[i], s_old_t[i].astype(BF16)) for i in idx]
    intra = [_dot(scores[i].astype(BF16), iv16[i]) for i in idx]
    kv = [_dot_tn(iv16[i], k_out[i]) for i in idx]
    yield
    s_new_t = [jnp.exp(a[i][last]) * s_old_t[i] + kv[i] for i in idx]
    o = [inter[i] + intra[i] for i in idx]
    rms = [lax.rsqrt(jnp.mean(o[i] * o[i], axis=1, keepdims=True) + RMS_EPS) for i in idx]
    out = [gate[i] * (o[i] * rms[i] * g_norm[i]) for i in idx]
    return out, s_new_t


def _load_state(c0_ref, n0_ref, m0_ref, s0_ref, c_ref, n_ref, m_ref, s_ref, n_seq_blk):
    shared = c0_ref.shape[0] == 1 and n_seq_blk > 1
    for s in range(n_seq_blk):
        src = 0 if shared else s
        c_ref[s] = c0_ref[src]
        n_ref[s] = n0_ref[src]
        m_ref[s] = m0_ref[src]
        for h in range(N_HEADS):
            s_ref[s, h] = s0_ref[src, h].T


def _finish_state(s_ref, n_seq_blk):
    for s in range(n_seq_blk):
        for h in range(N_HEADS):
            s_ref[s, h] = s_ref[s, h].T


def _act_cols(group, h):
    base = group * D_GROUP if group < KB_GROUP else D_PROJ + (group - KB_GROUP) * D_GROUP
    return slice(base + h * D_HEAD, base + (h + 1) * D_HEAD)


def _mixer_body(p_ref, bf_ref, ga_ref, gb_ref, lb_ref, c_ref, n_ref, m_ref, s_ref,
                *, t_len, n_seq_blk, activated, side_stages=None):
    rows = lax.broadcasted_iota(jnp.int32, (t_len, t_len), 0)
    cols = lax.broadcasted_iota(jnp.int32, (t_len, t_len), 1)
    causal = cols <= rows
    tril = causal.astype(BF16)
    level_of = jnp.where(rows > cols, 31 - lax.clz(rows ^ cols), jnp.where(rows == cols, -2, -1))
    lane = lax.broadcasted_iota(jnp.int32, (t_len, LANES), 1)
    is_f = (lane >= N_HEADS) & (lane < 2 * N_HEADS)
    hd = lambda j, h: slice(j * D_GROUP + h * D_HEAD, j * D_GROUP + (h + 1) * D_HEAD)

    units = [(s, h) for s in range(n_seq_blk) for h in range(N_HEADS)]
    rs = lambda s: slice(s * t_len, (s + 1) * t_len)

    z_seq, zt_seq, n_seq, m_seq = [], [], [], []
    for s in range(n_seq_blk):
        gates = p_ref[rs(s), GATE_COL:GATE_COL + LANES]
        log_f = jnp.where(is_f, jax.nn.log_sigmoid(gates + bf_ref[...]), 0.0)
        cum_f = _cumsum_rows(tril, log_f)
        z_seq.append(jnp.where(is_f, cum_f, jnp.where(lane < N_HEADS, gates, 0.0)))
        zt_seq.append(z_seq[s].T)
        n_seq.append(n_ref[s])
        m_seq.append(m_ref[s])
    computed = {}

    def act(group):
        if activated:
            return [p_ref[rs(s), _act_cols(group, h)] for s, h in units]
        source = 5 if group >= KB_GROUP else group
        for s, h in units:
            if (source, s, h) not in computed:
                computed[source, s, h] = _activate(source, p_ref[rs(s), hd(source, h)],
                                                   lb_ref[:, hd(0, h)])
        return [computed[source, s, h][group] for s, h in units]

    lane_row = lane[:1, :]
    m0_rows = []
    for s in range(n_seq_blk):
        m0_row = jnp.zeros((1, LANES), F32)
        for h in range(N_HEADS):
            m0_row = jnp.where(lane_row == N_HEADS + h, m_seq[s][:, h:h + 1], m0_row)
        m0_rows.append(m0_row)
    mlstm = _mlstm_units(
        q=act(0), k=act(1), v=act(2), gate=act(3), z=z_seq, zt=zt_seq, m0_row=m0_rows,
        c_old=[c_ref[s, h] for s, h in units], n_old=[n_seq[s][h:h + 1, :] for s, h in units],
        g_norm=[ga_ref[:, hd(0, h)] for s, h in units], causal=causal, lane=lane, t_len=t_len)
    hgrn = _hgrn_units(
        qb=act(4), log_f=act(5), f=act(F_GROUP), kb=act(KB_GROUP), iv=act(6), gate=act(7),
        s_old_t=[s_ref[s, h] for s, h in units],
        g_norm=[gb_ref[:, hd(0, h)] for s, h in units], tril=tril, level_of=level_of, t_len=t_len)
    gens, per_round = [mlstm, hgrn], [1, 2]
    if side_stages is not None:
        gens, per_round = gens + [side_stages], per_round + [SIDE_STAGES_PER_ROUND]
    results = _run_interleaved(gens, per_round)
    (outs, c_new, n_new, m_new), (outs_b, s_new_t) = results[:2]

    for i, (s, h) in enumerate(units):
        c_ref[s, h] = c_new[i]
        s_ref[s, h] = s_new_t[i]
    head_lane = lax.broadcasted_iota(jnp.int32, (1, N_HEADS), 1)
    for s in range(n_seq_blk):
        n_ref[s] = jnp.concatenate(n_new[s * N_HEADS:(s + 1) * N_HEADS], axis=0)
        m_row = m_seq[s]
        for h in range(N_HEADS):
            m_row = jnp.where(head_lane == h, m_new[s * N_HEADS + h], m_row)
        m_ref[s] = m_row
    return outs, outs_b


def _mixer_kernel(p_ref, c0_ref, n0_ref, m0_ref, s0_ref, bf_ref, ga_ref, gb_ref, lb_ref,
                  mix_ref, c_ref, n_ref, m_ref, s_ref, *, t_len, n_chunks, n_seq_blk):
    chunk = pl.program_id(1)

    @pl.when(chunk == 0)
    def _():
        _load_state(c0_ref, n0_ref, m0_ref, s0_ref, c_ref, n_ref, m_ref, s_ref, n_seq_blk)

    outs_a, outs_b = _mixer_body(p_ref, bf_ref, ga_ref, gb_ref, lb_ref, c_ref, n_ref, m_ref, s_ref,
                                 t_len=t_len, n_seq_blk=n_seq_blk, activated=False)
    for i in range(n_seq_blk * N_HEADS):
        s, h = divmod(i, N_HEADS)
        rows = slice(s * t_len, (s + 1) * t_len)
        mix_ref[rows, h * D_HEAD:(h + 1) * D_HEAD] = outs_a[i].astype(mix_ref.dtype)
        mix_ref[rows, D_GROUP + h * D_HEAD:D_GROUP + (h + 1) * D_HEAD] = outs_b[i].astype(mix_ref.dtype)

    @pl.when(chunk == n_chunks - 1)
    def _():
        _finish_state(s_ref, n_seq_blk)


def _mixer(proj, c0, n0, m0, s0, bf_row, ga, gb, lb, *, n_seq, n_chunks, t_len, n_seq_blk,
           shared_init):
    assert not shared_init or n_seq_blk == 1
    assert n_chunks == 1 or n_seq_blk == 1
    nb = n_seq_blk
    init = (lambda b, c: (0, 0, 0, 0)) if shared_init else (lambda b, c: (b, 0, 0, 0))
    init3 = (lambda b, c: (0, 0, 0)) if shared_init else (lambda b, c: (b, 0, 0))
    const = lambda b, c: (0, 0)
    state4 = pl.BlockSpec((nb, N_HEADS, D_HEAD, D_HEAD), lambda b, c: (b, 0, 0, 0))
    return pl.pallas_call(
        functools.partial(_mixer_kernel, t_len=t_len, n_chunks=n_chunks, n_seq_blk=nb),
        grid=(n_seq // nb, n_chunks),
        in_specs=[
            pl.BlockSpec((nb * t_len, D_PROJ), lambda b, c: (b * n_chunks + c, 0)),
            pl.BlockSpec((nb, N_HEADS, D_HEAD, D_HEAD), init),
            pl.BlockSpec((nb, N_HEADS, D_HEAD), init3),
            pl.BlockSpec((nb, 1, N_HEADS), init3),
            pl.BlockSpec((nb, N_HEADS, D_HEAD, D_HEAD), init),
            pl.BlockSpec((1, LANES), const),
            pl.BlockSpec((1, D_GROUP), const),
            pl.BlockSpec((1, D_GROUP), const),
            pl.BlockSpec((1, D_GROUP), const),
        ],
        out_specs=[
            pl.BlockSpec((nb * t_len, D_MODEL), lambda b, c: (b * n_chunks + c, 0)),
            state4,
            pl.BlockSpec((nb, N_HEADS, D_HEAD), lambda b, c: (b, 0, 0)),
            pl.BlockSpec((nb, 1, N_HEADS), lambda b, c: (b, 0, 0)),
            state4,
        ],
        out_shape=[
            jax.ShapeDtypeStruct((n_seq * n_chunks * t_len, D_MODEL), BF16),
            jax.ShapeDtypeStruct((n_seq, N_HEADS, D_HEAD, D_HEAD), F32),
            jax.ShapeDtypeStruct((n_seq, N_HEADS, D_HEAD), F32),
            jax.ShapeDtypeStruct((n_seq, 1, N_HEADS), F32),
            jax.ShapeDtypeStruct((n_seq, N_HEADS, D_HEAD, D_HEAD), F32),
        ],
        compiler_params=pltpu.CompilerParams(
            dimension_semantics=("arbitrary", "arbitrary"), vmem_limit_bytes=VMEM_LIMIT),
        name=f"mixer_t{t_len}",
    )(proj, c0, n0, m0, s0, bf_row, ga, gb, lb)


def _meta_kernel(x_ref, ge_ref, be_ref, win_ref, bin_ref, bf_ref, ga_ref, gb_ref, lb_ref,
                 wout_ref, bout_ref, g1_ref, b1_ref, wu_ref, bu_ref,
                 c_ref, n_ref, m_ref, s_ref, conv_ref, proj_scr):
    t_len = x_ref.shape[0]
    xn = _layer_norm(x_ref[...], ge_ref[...], be_ref[...])
    proj_scr[...] = _dot(xn.astype(BF16), win_ref[...]) + bin_ref[...]
    for ref in (c_ref, n_ref, m_ref, s_ref):
        ref[...] = jnp.zeros(ref.shape, ref.dtype)
    outs_a, outs_b = _mixer_body(proj_scr, bf_ref, ga_ref, gb_ref, lb_ref, c_ref, n_ref, m_ref, s_ref,
                                 t_len=t_len, n_seq_blk=1, activated=False)
    _finish_state(s_ref, 1)
    mix = jnp.concatenate([o.astype(BF16) for o in outs_a + outs_b], axis=1)
    y = _dot(mix, wout_ref[...]) + bout_ref[...]
    x1 = _layer_norm(ALPHA * xn + y, g1_ref[...], b1_ref[...])
    u = _dot(x1.astype(BF16), wu_ref[...]) + bu_ref[...]
    conv_ref[0] = u[t_len - (CONV_W - 1):, :]


def _meta_state(x, ln_e_g, ln_e_b, w_in, b_in, bf_row, ga, gb, lb, w_out, b_out, ln_g, ln_b,
                w_up, b_up):
    t_len = x.shape[0]
    full = lambda shape: _resident(shape, lambda i: (0,) * len(shape))
    vec, grp = full((1, D_MODEL)), full((1, D_GROUP))
    state4 = (1, N_HEADS, D_HEAD, D_HEAD)
    return pl.pallas_call(
        _meta_kernel,
        grid=(1,),
        in_specs=[full((t_len, D_MODEL)), vec, vec, full((D_MODEL, D_PROJ)), full((1, D_PROJ)),
                  full((1, LANES)), grp, grp, grp, full((D_MODEL, D_MODEL)), vec, vec, vec,
                  full((D_MODEL, D_FF)), full((1, D_FF))],
        out_specs=[pl.BlockSpec(shape, lambda i, rank=len(shape): (0,) * rank)
                   for shape in (state4, (1, N_HEADS, D_HEAD), (1, 1, N_HEADS), state4,
                                 (1, CONV_W - 1, D_FF))],
        out_shape=[
            jax.ShapeDtypeStruct(state4, F32),
            jax.ShapeDtypeStruct((1, N_HEADS, D_HEAD), F32),
            jax.ShapeDtypeStruct((1, 1, N_HEADS), F32),
            jax.ShapeDtypeStruct(state4, F32),
            jax.ShapeDtypeStruct((1, CONV_W - 1, D_FF), F32),
        ],
        scratch_shapes=[pltpu.VMEM((t_len, D_PROJ), F32)],
        compiler_params=pltpu.CompilerParams(
            dimension_semantics=("arbitrary",), vmem_limit_bytes=VMEM_LIMIT),
        name="meta_state",
    )(x, ln_e_g, ln_e_b, w_in, b_in, bf_row, ga, gb, lb, w_out, b_out, ln_g, ln_b, w_up, b_up)


def _in_proj_stages(x_ref, g_ref, b_ref, w_ref, bias_ref, lb_ref, act_ref, xn_ref):
    assert D_GROUP % IN_PROJ_STAGE_COLS == 0
    xn = _layer_norm(x_ref[...].reshape(xn_ref.shape), g_ref[...], b_ref[...])
    xn_ref[...] = xn
    x16 = xn.astype(BF16)
    for lo in range(0, D_PROJ, IN_PROJ_STAGE_COLS):
        hi = min(lo + IN_PROJ_STAGE_COLS, D_PROJ)
        yield
        block = _dot(x16, w_ref[:, lo:hi]) + bias_ref[:, lo:hi]
        group, off = divmod(lo, D_GROUP)
        if lo >= GATE_COL:
            act_ref[:, lo:hi] = block
            continue
        for dst, val in _activate(group, block, lb_ref[:, off:off + hi - lo]).items():
            base = _act_cols(dst, 0).start + off
            act_ref[:, base:base + hi - lo] = val


def _prompt_kernel(x0_ref, xnext_ref, ge_ref, be_ref, win_ref, bin_ref, c0_ref, n0_ref, m0_ref, s0_ref,
                   bf_ref, ga_ref, gb_ref, lb_ref, wout_ref, bout_ref, g1_ref, b1_ref,
                   x1_ref, c_ref, n_ref, m_ref, s_ref, proj_scr, xn_scr, proj_alt, xn_alt,
                   *, t_len, n_chunks, n_seq_blk):
    chunk = pl.program_id(1)
    step = pl.program_id(0) * n_chunks + chunk

    @pl.when(step == 0)
    def _():
        first = _in_proj_stages(x0_ref, ge_ref, be_ref, win_ref, bin_ref, lb_ref, proj_scr, xn_scr)
        _run_interleaved([first], [1])

    @pl.when(chunk == 0)
    def _():
        _load_state(c0_ref, n0_ref, m0_ref, s0_ref, c_ref, n_ref, m_ref, s_ref, n_seq_blk)

    def tile(proj_cur, xn_cur, proj_next, xn_next):
        next_proj = _in_proj_stages(xnext_ref, ge_ref, be_ref, win_ref, bin_ref, lb_ref,
                                    proj_next, xn_next)
        outs_a, outs_b = _mixer_body(
            proj_cur, bf_ref, ga_ref, gb_ref, lb_ref, c_ref, n_ref, m_ref, s_ref,
            t_len=t_len, n_seq_blk=n_seq_blk, activated=True, side_stages=next_proj)
        mix = jnp.concatenate(
            [jnp.concatenate([o.astype(BF16) for o in outs_a[s * N_HEADS:(s + 1) * N_HEADS]
                              + outs_b[s * N_HEADS:(s + 1) * N_HEADS]], axis=1)
             for s in range(n_seq_blk)], axis=0)
        y = _dot(mix, wout_ref[...]) + bout_ref[...]
        x1 = _layer_norm(ALPHA * xn_cur[...] + y, g1_ref[...], b1_ref[...])
        x1_ref[...] = x1.reshape(x1_ref.shape)

    @pl.when(step % 2 == 0)
    def _():
        tile(proj_scr, xn_scr, proj_alt, xn_alt)

    @pl.when(step % 2 == 1)
    def _():
        tile(proj_alt, xn_alt, proj_scr, xn_scr)

    @pl.when(chunk == n_chunks - 1)
    def _():
        _finish_state(s_ref, n_seq_blk)


def _prompt_mixer(x, ln_e_g, ln_e_b, w_in, b_in, c0, n0, m0, s0, bf_row, ga, gb, lb,
                  w_out, b_out, ln_g, ln_b, *, n_chunks, t_len, n_seq_blk):
    n_seq = x.shape[0]
    nb = n_seq_blk
    n_tiles = (n_seq // nb) * n_chunks
    const = lambda b, c: (0, 0)
    init4 = lambda b, c: (0, 0, 0, 0)
    init3 = lambda b, c: (0, 0, 0)
    vec = pl.BlockSpec((1, D_MODEL), const)
    grp = pl.BlockSpec((1, D_GROUP), const)
    state4 = pl.BlockSpec((nb, N_HEADS, D_HEAD, D_HEAD), lambda b, c: (b, 0, 0, 0))

    def next_tile(b, c):
        nxt = jnp.minimum(b * n_chunks + c + 1, n_tiles - 1)
        return (nxt // n_chunks, nxt % n_chunks, 0)

    return pl.pallas_call(
        functools.partial(_prompt_kernel, t_len=t_len, n_chunks=n_chunks, n_seq_blk=nb),
        grid=(n_seq // nb, n_chunks),
        in_specs=[
            _resident((nb, t_len, D_MODEL), init3),
            pl.BlockSpec((nb, t_len, D_MODEL), next_tile),
            vec, vec,
            _resident((D_MODEL, D_PROJ), const),
            pl.BlockSpec((1, D_PROJ), const),
            pl.BlockSpec((1, N_HEADS, D_HEAD, D_HEAD), init4),
            pl.BlockSpec((1, N_HEADS, D_HEAD), init3),
            pl.BlockSpec((1, 1, N_HEADS), init3),
            pl.BlockSpec((1, N_HEADS, D_HEAD, D_HEAD), init4),
            pl.BlockSpec((1, LANES), const),
            grp, grp, grp,
            _resident((D_MODEL, D_MODEL), const),
            vec, vec, vec,
        ],
        out_specs=[
            pl.BlockSpec((nb, t_len, D_MODEL), lambda b, c: (b, c, 0)),
            state4,
            pl.BlockSpec((nb, N_HEADS, D_HEAD), lambda b, c: (b, 0, 0)),
            pl.BlockSpec((nb, 1, N_HEADS), lambda b, c: (b, 0, 0)),
            state4,
        ],
        out_shape=[
            jax.ShapeDtypeStruct(x.shape, F32),
            jax.ShapeDtypeStruct((n_seq, N_HEADS, D_HEAD, D_HEAD), F32),
            jax.ShapeDtypeStruct((n_seq, N_HEADS, D_HEAD), F32),
            jax.ShapeDtypeStruct((n_seq, 1, N_HEADS), F32),
            jax.ShapeDtypeStruct((n_seq, N_HEADS, D_HEAD, D_HEAD), F32),
        ],
        scratch_shapes=[pltpu.VMEM((nb * t_len, D_ACT), F32), pltpu.VMEM((nb * t_len, D_MODEL), F32),
                        pltpu.VMEM((nb * t_len, D_ACT), F32), pltpu.VMEM((nb * t_len, D_MODEL), F32)],
        compiler_params=pltpu.CompilerParams(
            dimension_semantics=("arbitrary", "arbitrary"), vmem_limit_bytes=VMEM_LIMIT),
        name="prompt_mixer",
    )(x, x, ln_e_g, ln_e_b, w_in, b_in, c0, n0, m0, s0, bf_row, ga, gb, lb, w_out, b_out, ln_g, ln_b)


def _out_proj_kernel(x_ref, mix_ref, ge_ref, be_ref, w_ref, bias_ref, g_ref, b_ref, o_ref):
    xn = _layer_norm(x_ref[...], ge_ref[...], be_ref[...])
    y = _dot(mix_ref[...], w_ref[...]) + bias_ref[...]
    o_ref[...] = _layer_norm(ALPHA * xn + y, g_ref[...], b_ref[...])


def _out_proj(x, mix, ln_e_g, ln_e_b, w, bias, ln_g, ln_b, *, tm):
    n = x.shape[0]
    const = lambda i: (0, 0)
    row = pl.BlockSpec((tm, D_MODEL), lambda i: (i, 0))
    vec = pl.BlockSpec((1, D_MODEL), const)
    return pl.pallas_call(
        _out_proj_kernel,
        grid=(n // tm,),
        in_specs=[row, row, vec, vec, pl.BlockSpec((D_MODEL, D_MODEL), const), vec, vec, vec],
        out_specs=row,
        out_shape=jax.ShapeDtypeStruct((n, D_MODEL), F32),
        compiler_params=pltpu.CompilerParams(
            dimension_semantics=("arbitrary",), vmem_limit_bytes=VMEM_LIMIT),
        name="out_proj",
    )(x, mix, ln_e_g, ln_e_b, w, bias, ln_g, ln_b)


def _ffn_kernel(x_ref, cs_ref, wu_ref, bu_ref, wc_ref, bc_ref, wd_ref, bd_ref, g_ref, b_ref,
                y_ref, nc_ref, full_ref, *, n_seq_blk, t_len):
    hist = SUBLANES - (CONV_W - 1)

    @pl.when(pl.program_id(1) == 0)
    def _():
        full_ref[:, hist:SUBLANES, :] = cs_ref[...]

    x = x_ref[...]
    up = _dot(x.astype(BF16), wu_ref[...]) + bu_ref[...]
    u = up[:, :D_FF].reshape(n_seq_blk, t_len, D_FF)
    gate = up[:, D_FF:].reshape(n_seq_blk, t_len, D_FF)
    full_ref[:, SUBLANES:SUBLANES + t_len, :] = u
    conv = bc_ref[...] + u * wc_ref[CONV_W - 1:CONV_W, :]
    for j in range(CONV_W - 1):
        conv = conv + full_ref[:, hist + j:hist + j + t_len, :] * wc_ref[j:j + 1, :]
    last = full_ref[:, hist + t_len:SUBLANES + t_len, :]
    nc_ref[...] = last
    full_ref[:, hist:SUBLANES, :] = last
    act = (conv * _sigmoid(conv) * gate).reshape(n_seq_blk * t_len, D_FF)
    ffn = _dot(act.astype(BF16), wd_ref[...]) + bd_ref[...]
    y_ref[...] = _layer_norm(ALPHA * x + ffn, g_ref[...], b_ref[...])


def _ffn(x, conv_state, w_up, b_up, w_conv, b_conv, w_down, b_down, ln_g, ln_b,
         *, n_seq, seq_len, n_seq_blk, t_len, shared_init):
    n_t = seq_len // t_len
    rows = n_seq_blk * t_len
    const = lambda s, t: (0, 0)
    cs_map = (lambda s, t: (0, 0, 0)) if shared_init else (lambda s, t: (s, 0, 0))
    row = pl.BlockSpec((rows, D_MODEL), lambda s, t: (s * n_t + t, 0))
    vec = pl.BlockSpec((1, D_MODEL), const)
    return pl.pallas_call(
        functools.partial(_ffn_kernel, n_seq_blk=n_seq_blk, t_len=t_len),
        grid=(n_seq // n_seq_blk, n_t),
        in_specs=[
            row,
            pl.BlockSpec((n_seq_blk, CONV_W - 1, D_FF), cs_map),
            _resident((D_MODEL, 2 * D_FF), const),
            pl.BlockSpec((1, 2 * D_FF), const),
            pl.BlockSpec((CONV_W, D_FF), const),
            pl.BlockSpec((1, D_FF), const),
            _resident((D_FF, D_MODEL), const),
            vec, vec, vec,
        ],
        out_specs=[row, pl.BlockSpec((n_seq_blk, CONV_W - 1, D_FF), lambda s, t: (s, 0, 0))],
        out_shape=[
            jax.ShapeDtypeStruct((n_seq * seq_len, D_MODEL), F32),
            jax.ShapeDtypeStruct((n_seq, CONV_W - 1, D_FF), F32),
        ],
        scratch_shapes=[pltpu.VMEM((n_seq_blk, SUBLANES + t_len, D_FF), F32)],
        compiler_params=pltpu.CompilerParams(
            dimension_semantics=("arbitrary", "arbitrary"), vmem_limit_bytes=VMEM_LIMIT),
        name=f"ffn_t{t_len}",
    )(x, conv_state, w_up, b_up, w_conv, b_conv, w_down, b_down, ln_g, ln_b)


def kernel(x_prompt, x_sample, state_mlstm_C, state_mlstm_n, state_mlstm_m, state_hgrn_S, state_ffn_conv, meta_tokens, ln_emb_g, ln_emb_b, w_in, b_in, b_fgate_a, g_norm_a, g_norm_b, hgrn_lb_logits, w_out, b_out, ln1_g, ln1_b, w_up, b_up, w_conv, b_conv, w_down, b_down, ln2_g, ln2_b):
    assert w_in.shape[0] == DEPTH == 1
    n_prompt, seq, _ = x_prompt.shape
    n_sample, dec_seq, _ = x_sample.shape
    row = lambda v: v.reshape(1, -1).astype(F32)

    gate0 = 4 * D_GROUP
    gate1 = gate0 + 2 * N_HEADS
    pad = D_PROJ - w_in.shape[2]
    w_in_p = _regroup_in_proj_weight(w_in[0].astype(BF16), tm=256)
    b_in_p = jnp.concatenate(
        [b_in[0][:gate0], b_in[0][gate1:], b_in[0][gate0:gate1], jnp.zeros((pad,), b_in.dtype)]
    ).reshape(1, D_PROJ).astype(F32)
    bf_row = jnp.zeros((1, LANES), F32).at[0, N_HEADS:2 * N_HEADS].set(b_fgate_a[0].astype(F32))
    lb = jnp.cumsum(jax.nn.softmax(hgrn_lb_logits.astype(F32), axis=0), axis=0)[0].reshape(1, D_GROUP)
    ga, gb = row(g_norm_a[0]), row(g_norm_b[0])
    ln_e = (row(ln_emb_g), row(ln_emb_b))
    out_p = (w_out[0].astype(BF16), row(b_out[0]), row(ln1_g[0]), row(ln1_b[0]))
    ffn_p = (w_up[0].astype(BF16), row(b_up[0]), w_conv[0].astype(F32), row(b_conv[0]),
             w_down[0].astype(BF16), row(b_down[0]), row(ln2_g[0]), row(ln2_b[0]))

    def layer(x_rows, mixer_state, conv_state, *, n_seq, seq_len, t_mix, mix_seq_blk, tm,
              ffn_seq_blk, ffn_t, shared_init):
        proj = _in_proj(x_rows, *ln_e, w_in_p, b_in_p, tm=tm)
        mix, c_new, n_new, m_new, s_new = _mixer(
            proj, *mixer_state, bf_row, ga, gb, lb, n_seq=n_seq, n_chunks=seq_len // t_mix,
            t_len=t_mix, n_seq_blk=mix_seq_blk, shared_init=shared_init)
        x1 = _out_proj(x_rows, mix, *ln_e, *out_p, tm=tm)
        y, conv_new = _ffn(x1, conv_state, *ffn_p, n_seq=n_seq, seq_len=seq_len,
                           n_seq_blk=ffn_seq_blk, t_len=ffn_t, shared_init=shared_init)
        return y, c_new, n_new, m_new, s_new, conv_new

    c_m, n_m, m_m, s_m, conv_m = _meta_state(
        meta_tokens.astype(F32), *ln_e, w_in_p, b_in_p, bf_row, ga, gb, lb, *out_p, ffn_p[0], ffn_p[1])

    x1_p, c_p, n_p, m_p, s_p = _prompt_mixer(
        x_prompt.astype(F32), *ln_e, w_in_p, b_in_p, c_m, n_m, m_m, s_m, bf_row, ga, gb, lb, *out_p,
        n_chunks=seq // PROMPT_CHUNK, t_len=PROMPT_CHUNK, n_seq_blk=PROMPT_SEQS_PER_STEP)
    y_p, conv_p = _ffn(x1_p.reshape(n_prompt * seq, D_MODEL), conv_m, *ffn_p, n_seq=n_prompt,
                       seq_len=seq, n_seq_blk=1, t_len=512, shared_init=True)

    sample_state = (state_mlstm_C[0].astype(F32), state_mlstm_n[0].astype(F32),
                    state_mlstm_m[0].astype(F32).reshape(n_sample, 1, N_HEADS),
                    state_hgrn_S[0].astype(F32))
    y_s, c_s, n_s, m_s, s_s, conv_s = layer(
        x_sample.reshape(n_sample * dec_seq, D_MODEL), sample_state, state_ffn_conv[0].astype(F32),
        n_seq=n_sample, seq_len=dec_seq, t_mix=dec_seq, mix_seq_blk=8, tm=256, ffn_seq_blk=32,
        ffn_t=dec_seq, shared_init=False)

    lead = lambda v: v[None]
    return (y_p.reshape(n_prompt, seq, D_MODEL), y_s.reshape(n_sample, dec_seq, D_MODEL),
            lead(c_p), lead(n_p), lead(m_p.reshape(n_prompt, N_HEADS)), lead(s_p), lead(conv_p),
            lead(c_s), lead(n_s), lead(m_s.reshape(n_sample, N_HEADS)), lead(s_s), lead(conv_s))
```

```python
import functools

import jax
import jax.numpy as jnp
from jax import lax
from jax.experimental import pallas as pl
from jax.experimental.pallas import tpu as pltpu

D_MODEL = 1024
N_META = 16
N_HEADS = 4
D_HEAD = 128
D_GROUP = N_HEADS * D_HEAD
D_FF = 2816
CONV_W = 3
DEPTH = 1
ALPHA = (2.0 * DEPTH) ** 0.25
LN_EPS = 1e-5
RMS_EPS = 1e-6
NEG_LOG2_E = -1.4426950408889634

LANES = 128
SUBLANES = 8
GATE_COL = 8 * D_GROUP
D_PROJ = GATE_COL + LANES
KB_GROUP, F_GROUP = 8, 9
D_ACT = D_PROJ + 2 * D_GROUP
IN_PROJ_STAGE_COLS = 256
SIDE_STAGES_PER_ROUND = 3
PROMPT_SEQS_PER_STEP = 4
PROMPT_CHUNK = 128
VMEM_LIMIT = 56 * 1024 * 1024
PROMPT_VMEM_LIMIT = 62 * 1024 * 1024

F32 = jnp.float32
BF16 = jnp.bfloat16
NT_DIMS = (((1,), (1,)), ((), ()))
TN_DIMS = (((0,), (0,)), ((), ()))


def _layer_norm(x, g, b):
    mu = jnp.mean(x, axis=-1, keepdims=True)
    xc = x - mu
    var = jnp.mean(xc * xc, axis=-1, keepdims=True)
    return xc * lax.rsqrt(var + LN_EPS) * g + b


def _exp_neg(x):
    return jnp.exp2(x * NEG_LOG2_E)


def _sigmoid(x):
    return 1.0 / (1.0 + _exp_neg(x))


def _resident(block_shape, index_map):
    return pl.BlockSpec(block_shape, index_map, pipeline_mode=pl.Buffered(1))


def _dot(a, b):
    return jnp.dot(a, b, preferred_element_type=F32)


def _dot_nt(a, b):
    return lax.dot_general(a, b, NT_DIMS, preferred_element_type=F32)


def _dot_tn(a, b):
    return lax.dot_general(a, b, TN_DIMS, preferred_element_type=F32)


def _regroup_kernel(w_ref, o_ref):
    gate0 = 4 * D_GROUP
    gate1 = gate0 + 2 * N_HEADS
    rows = w_ref.shape[0]
    o_ref[:, :gate0] = w_ref[:, :gate0]
    o_ref[:, gate0:GATE_COL] = w_ref[:, gate1:gate1 + gate0]
    o_ref[:, GATE_COL:] = jnp.concatenate(
        [w_ref[:, gate0:gate1], jnp.zeros((rows, LANES - (gate1 - gate0)), w_ref.dtype)], axis=1)


def _regroup_in_proj_weight(w, *, tm):
    n, cols = w.shape
    return pl.pallas_call(
        _regroup_kernel,
        grid=(n // tm,),
        in_specs=[pl.BlockSpec((tm, cols), lambda i: (i, 0))],
        out_specs=pl.BlockSpec((tm, D_PROJ), lambda i: (i, 0)),
        out_shape=jax.ShapeDtypeStruct((n, D_PROJ), w.dtype),
        compiler_params=pltpu.CompilerParams(dimension_semantics=("arbitrary",)),
        name="regroup_w_in",
    )(w)


def _in_proj_kernel(x_ref, g_ref, b_ref, w_ref, bias_ref, o_ref):
    xn = _layer_norm(x_ref[...], g_ref[...], b_ref[...])
    o_ref[...] = _dot(xn.astype(BF16), w_ref[...]) + bias_ref[...]


def _in_proj(x, ln_g, ln_b, w, bias, *, tm):
    n = x.shape[0]
    const = lambda i: (0, 0)
    return pl.pallas_call(
        _in_proj_kernel,
        grid=(n // tm,),
        in_specs=[
            pl.BlockSpec((tm, D_MODEL), lambda i: (i, 0)),
            pl.BlockSpec((1, D_MODEL), const),
            pl.BlockSpec((1, D_MODEL), const),
            pl.BlockSpec((D_MODEL, D_PROJ), const),
            pl.BlockSpec((1, D_PROJ), const),
        ],
        out_specs=pl.BlockSpec((tm, D_PROJ), lambda i: (i, 0)),
        out_shape=jax.ShapeDtypeStruct((n, D_PROJ), F32),
        compiler_params=pltpu.CompilerParams(
            dimension_semantics=("arbitrary",), vmem_limit_bytes=VMEM_LIMIT),
        name="in_proj",
    )(x, ln_g, ln_b, w, bias)


def _block_rows(x, level, t_len, row_in_block):
    size = 2 << level
    if size > SUBLANES:
        pieces = [jnp.broadcast_to(x[j * size + row_in_block:j * size + row_in_block + 1, :],
                                   (size, x.shape[1])) for j in range(t_len // size)]
        return pieces[0] if len(pieces) == 1 else jnp.concatenate(pieces, axis=0)
    x3 = x.reshape(t_len // SUBLANES, SUBLANES, x.shape[1])
    sub = lax.broadcasted_iota(jnp.int32, x3.shape, 1)
    out = None
    for j in range(SUBLANES // size):
        row = jnp.broadcast_to(x3[:, j * size + row_in_block:j * size + row_in_block + 1, :], x3.shape)
        out = row if out is None else jnp.where(sub >= j * size, row, out)
    return out.reshape(x.shape)


def _interleave_halves(lower, upper, level, t_len):
    half = 1 << level
    if half >= SUBLANES:
        pieces = []
        for j in range(t_len // (2 * half)):
            pieces.append(lower[2 * half * j:2 * half * j + half])
            pieces.append(upper[2 * half * j + half:2 * half * (j + 1)])
        return jnp.concatenate(pieces, axis=0)
    rows = lax.broadcasted_iota(jnp.int32, lower.shape, 0)
    return jnp.where((rows & half) != 0, upper, lower)


def _run_interleaved(gens, stages_per_round):
    results = [None] * len(gens)
    live = [True] * len(gens)
    while any(live):
        for g, steps in enumerate(stages_per_round):
            for _ in range(steps):
                if live[g]:
                    try:
                        next(gens[g])
                    except StopIteration as stop:
                        results[g], live[g] = stop.value, False
    return results


def _cumsum_rows(tril16, x):
    hi = x.astype(BF16)
    rest = x - hi.astype(F32)
    mid = rest.astype(BF16)
    lo = (rest - mid.astype(F32)).astype(BF16)
    return _dot(tril16, hi) + _dot(tril16, mid) + _dot(tril16, lo)


def _activate(group, x, lb=None):
    if group == 1:
        return {1: x * (D_HEAD ** -0.5)}
    if group in (3, 7):
        return {group: _sigmoid(x)}
    if group == 4:
        return {4: x * _sigmoid(x)}
    if group == 5:
        f = lb + (1.0 - lb) * _sigmoid(x)
        return {5: jnp.log(f), KB_GROUP: (1.0 - lb) / (1.0 + jnp.exp(x)), F_GROUP: f}
    return {group: x}


def _mlstm_units(*, q, k, v, gate, z, zt, m0_row, c_old, n_old, g_norm, causal, lane, t_len):
    idx = range(len(q))
    seq = [i // N_HEADS for i in idx]
    b_lane = [N_HEADS + i % N_HEADS for i in idx]
    q16 = [q[i].astype(BF16) for i in idx]
    k16 = [k[i].astype(BF16) for i in idx]
    qk = [_dot_nt(q16[i], k16[i]) for i in idx]
    qc = [_dot(q16[i], c_old[i].astype(BF16)) for i in idx]
    bs_row = [zt[seq[i]][b_lane[i]:b_lane[i] + 1, :] - zt[seq[i]][i % N_HEADS:i % N_HEADS + 1, :]
              for i in idx]
    yield
    col = lambda per_seq, i: per_seq[seq[i]][:, b_lane[i]:b_lane[i] + 1]
    d = [jnp.where(causal, col(z, i) - bs_row[i], -jnp.inf) for i in idx]
    row_max = [jnp.max(d[i], axis=1, keepdims=True) for i in idx]
    last = slice(t_len - 1, t_len)
    m_t_seq, dec_seq, floor_seq, w_last_seq = [], [], [], []
    for s in range(len(z)):
        r = z[s] + m0_row[s]
        d_max = jnp.full(r.shape, -jnp.inf, F32)
        for h in range(N_HEADS):
            d_max = jnp.where(lane == N_HEADS + h, row_max[s * N_HEADS + h], d_max)
        m_t = jnp.maximum(r, d_max)
        i_gate = pltpu.roll(z[s], N_HEADS, axis=1)
        m_t_seq.append(m_t)
        dec_seq.append(jnp.exp(r - m_t))
        floor_seq.append(_exp_neg(m_t))
        w_last_seq.append(jnp.exp(z[s][last] - z[s] + i_gate - m_t[last]))
    m_t = [col(m_t_seq, i) for i in idx]
    dec = [col(dec_seq, i) for i in idx]
    w_last = [col(w_last_seq, i) for i in idx]
    sw = [jnp.exp(d[i] - m_t[i]) * qk[i] for i in idx]
    yield
    swv = [_dot(sw[i].astype(BF16), v[i].astype(BF16)) for i in idx]
    kv = [_dot_tn(k16[i], (w_last[i] * v[i]).astype(BF16)) for i in idx]
    yield
    c_new = [dec[i][last] * c_old[i] + kv[i] for i in idx]
    n_new = [dec[i][last] * n_old[i] + jnp.sum(w_last[i] * k[i], axis=0, keepdims=True) for i in idx]
    m_new = [m_t[i][last] for i in idx]
    den = [dec[i] * jnp.sum(q[i] * n_old[i], axis=1, keepdims=True)
           + jnp.sum(sw[i], axis=1, keepdims=True) for i in idx]
    hid = [(dec[i] * qc[i] + swv[i]) / jnp.maximum(jnp.abs(den[i]), col(floor_seq, i)) for i in idx]
    yield
    rms = [lax.rsqrt(jnp.mean(hid[i] * hid[i], axis=1, keepdims=True) + RMS_EPS) for i in idx]
    out = [gate[i] * (hid[i] * rms[i] * g_norm[i]) for i in idx]
    return out, c_new, n_new, m_new


def _hgrn_units(*, qb, log_f, f, kb, iv, gate, s_old_t, g_norm, tril, level_of, t_len):
    idx = range(len(qb))
    n_levels = t_len.bit_length() - 1
    a = [_cumsum_rows(tril, log_f[i]) for i in idx]
    yield
    iv16 = [iv[i].astype(BF16) for i in idx]
    diag = [_dot_nt(qb[i].astype(BF16), kb[i].astype(BF16)) for i in idx]
    scores = [jnp.where(level_of == -2, diag[i], 0.0) for i in idx]
    for level in range(n_levels):
        yield
        x16 = []
        for i in idx:
            base = _interleave_halves(kb[i], qb[i], level, t_len)
            if level == 0:
                x = base * _interleave_halves(jnp.ones_like(f[i]), f[i], 0, t_len)
            else:
                ref = _block_rows(a[i], level, t_len, (1 << level) - 1)
                x = base * _exp_neg(jnp.abs(a[i] - ref))
            x16.append(x.astype(BF16))
        part = [_dot_nt(x16[i], x16[i]) for i in idx]
        scores = [jnp.where(level_of == level, part[i], scores[i]) for i in idx]
    yield
    last = slice(t_len - 1, t_len)
    q_in = [(qb[i] * jnp.exp(a[i])).astype(BF16) for i in idx]
    k_out = [(kb[i] * jnp.exp(a[i][last] - a[i])).astype(BF16) for i in idx]
    inter = [_dot_nt(q_in[i], s_old_t[i].astype(BF16)) for i in idx]
    intra = [_dot(scores[i].astype(BF16), iv16[i]) for i in idx]
    kv = [_dot_tn(iv16[i], k_out[i]) for i in idx]
    yield
    s_new_t = [jnp.exp(a[i][last]) * s_old_t[i] + kv[i] for i in idx]
    o = [inter[i] + intra[i] for i in idx]
    rms = [lax.rsqrt(jnp.mean(o[i] * o[i], axis=1, keepdims=True) + RMS_EPS) for i in idx]
    out = [gate[i] * (o[i] * rms[i] * g_norm[i]) for i in idx]
    return out, s_new_t


def _load_state(c0_ref, n0_ref, m0_ref, s0_ref, c_ref, n_ref, m_ref, s_ref, n_seq_blk):
    shared = c0_ref.shape[0] == 1 and n_seq_blk > 1
    for s in range(n_seq_blk):
        src = 0 if shared else s
        c_ref[s] = c0_ref[src]
        n_ref[s] = n0_ref[src]
        m_ref[s] = m0_ref[src]
        for h in range(N_HEADS):
            s_ref[s, h] = s0_ref[src, h].T


def _finish_state(s_ref, n_seq_blk):
    for s in range(n_seq_blk):
        for h in range(N_HEADS):
            s_ref[s, h] = s_ref[s, h].T


def _act_cols(group, h):
    base = group * D_GROUP if group < KB_GROUP else D_PROJ + (group - KB_GROUP) * D_GROUP
    return slice(base + h * D_HEAD, base + (h + 1) * D_HEAD)


def _mixer_body(p_ref, bf_ref, ga_ref, gb_ref, lb_ref, c_ref, n_ref, m_ref, s_ref,
                *, t_len, n_seq_blk, activated, side_stages=None):
    rows = lax.broadcasted_iota(jnp.int32, (t_len, t_len), 0)
    cols = lax.broadcasted_iota(jnp.int32, (t_len, t_len), 1)
    causal = cols <= rows
    tril = causal.astype(BF16)
    level_of = jnp.where(rows > cols, 31 - lax.clz(rows ^ cols), jnp.where(rows == cols, -2, -1))
    lane = lax.broadcasted_iota(jnp.int32, (t_len, LANES), 1)
    is_f = (lane >= N_HEADS) & (lane < 2 * N_HEADS)
    hd = lambda j, h: slice(j * D_GROUP + h * D_HEAD, j * D_GROUP + (h + 1) * D_HEAD)

    units = [(s, h) for s in range(n_seq_blk) for h in range(N_HEADS)]
    rs = lambda s: slice(s * t_len, (s + 1) * t_len)

    z_seq, zt_seq, n_seq, m_seq = [], [], [], []
    for s in range(n_seq_blk):
        gates = p_ref[rs(s), GATE_COL:GATE_COL + LANES]
        log_f = jnp.where(is_f, jax.nn.log_sigmoid(gates + bf_ref[...]), 0.0)
        cum_f = _cumsum_rows(tril, log_f)
        z_seq.append(jnp.where(is_f, cum_f, jnp.where(lane < N_HEADS, gates, 0.0)))
        zt_seq.append(z_seq[s].T)
        n_seq.append(n_ref[s])
        m_seq.append(m_ref[s])
    computed = {}

    def act(group):
        if activated:
            return [p_ref[rs(s), _act_cols(group, h)] for s, h in units]
        source = 5 if group >= KB_GROUP else group
        for s, h in units:
            if (source, s, h) not in computed:
                computed[source, s, h] = _activate(source, p_ref[rs(s), hd(source, h)],
                                                   lb_ref[:, hd(0, h)])
        return [computed[source, s, h][group] for s, h in units]

    lane_row = lane[:1, :]
    m0_rows = []
    for s in range(n_seq_blk):
        m0_row = jnp.zeros((1, LANES), F32)
        for h in range(N_HEADS):
            m0_row = jnp.where(lane_row == N_HEADS + h, m_seq[s][:, h:h + 1], m0_row)
        m0_rows.append(m0_row)
    mlstm = _mlstm_units(
        q=act(0), k=act(1), v=act(2), gate=act(3), z=z_seq, zt=zt_seq, m0_row=m0_rows,
        c_old=[c_ref[s, h] for s, h in units], n_old=[n_seq[s][h:h + 1, :] for s, h in units],
        g_norm=[ga_ref[:, hd(0, h)] for s, h in units], causal=causal, lane=lane, t_len=t_len)
    hgrn = _hgrn_units(
        qb=act(4), log_f=act(5), f=act(F_GROUP), kb=act(KB_GROUP), iv=act(6), gate=act(7),
        s_old_t=[s_ref[s, h] for s, h in units],
        g_norm=[gb_ref[:, hd(0, h)] for s, h in units], tril=tril, level_of=level_of, t_len=t_len)
    gens, per_round = [mlstm, hgrn], [1, 2]
    if side_stages is not None:
        gens, per_round = gens + [side_stages], per_round + [SIDE_STAGES_PER_ROUND]
    results = _run_interleaved(gens, per_round)
    (outs, c_new, n_new, m_new), (outs_b, s_new_t) = results[:2]

    for i, (s, h) in enumerate(units):
        c_ref[s, h] = c_new[i]
        s_ref[s, h] = s_new_t[i]
    head_lane = lax.broadcasted_iota(jnp.int32, (1, N_HEADS), 1)
    for s in range(n_seq_blk):
        n_ref[s] = jnp.concatenate(n_new[s * N_HEADS:(s + 1) * N_HEADS], axis=0)
        m_row = m_seq[s]
        for h in range(N_HEADS):
            m_row = jnp.where(head_lane == h, m_new[s * N_HEADS + h], m_row)
        m_ref[s] = m_row
    return outs, outs_b


def _mixer_kernel(p_ref, c0_ref, n0_ref, m0_ref, s0_ref, bf_ref, ga_ref, gb_ref, lb_ref,
                  mix_ref, c_ref, n_ref, m_ref, s_ref, *, t_len, n_chunks, n_seq_blk):
    chunk = pl.program_id(1)

    @pl.when(chunk == 0)
    def _():
        _load_state(c0_ref, n0_ref, m0_ref, s0_ref, c_ref, n_ref, m_ref, s_ref, n_seq_blk)

    outs_a, outs_b = _mixer_body(p_ref, bf_ref, ga_ref, gb_ref, lb_ref, c_ref, n_ref, m_ref, s_ref,
                                 t_len=t_len, n_seq_blk=n_seq_blk, activated=False)
    for i in range(n_seq_blk * N_HEADS):
        s, h = divmod(i, N_HEADS)
        rows = slice(s * t_len, (s + 1) * t_len)
        mix_ref[rows, h * D_HEAD:(h + 1) * D_HEAD] = outs_a[i].astype(mix_ref.dtype)
        mix_ref[rows, D_GROUP + h * D_HEAD:D_GROUP + (h + 1) * D_HEAD] = outs_b[i].astype(mix_ref.dtype)

    @pl.when(chunk == n_chunks - 1)
    def _():
        _finish_state(s_ref, n_seq_blk)


def _mixer(proj, c0, n0, m0, s0, bf_row, ga, gb, lb, *, n_seq, n_chunks, t_len, n_seq_blk,
           shared_init):
    assert not shared_init or n_seq_blk == 1
    assert n_chunks == 1 or n_seq_blk == 1
    nb = n_seq_blk
    init = (lambda b, c: (0, 0, 0, 0)) if shared_init else (lambda b, c: (b, 0, 0, 0))
    init3 = (lambda b, c: (0, 0, 0)) if shared_init else (lambda b, c: (b, 0, 0))
    const = lambda b, c: (0, 0)
    state4 = pl.BlockSpec((nb, N_HEADS, D_HEAD, D_HEAD), lambda b, c: (b, 0, 0, 0))
    return pl.pallas_call(
        functools.partial(_mixer_kernel, t_len=t_len, n_chunks=n_chunks, n_seq_blk=nb),
        grid=(n_seq // nb, n_chunks),
        in_specs=[
            pl.BlockSpec((nb * t_len, D_PROJ), lambda b, c: (b * n_chunks + c, 0)),
            pl.BlockSpec((nb, N_HEADS, D_HEAD, D_HEAD), init),
            pl.BlockSpec((nb, N_HEADS, D_HEAD), init3),
            pl.BlockSpec((nb, 1, N_HEADS), init3),
            pl.BlockSpec((nb, N_HEADS, D_HEAD, D_HEAD), init),
            pl.BlockSpec((1, LANES), const),
            pl.BlockSpec((1, D_GROUP), const),
            pl.BlockSpec((1, D_GROUP), const),
            pl.BlockSpec((1, D_GROUP), const),
        ],
        out_specs=[
            pl.BlockSpec((nb * t_len, D_MODEL), lambda b, c: (b * n_chunks + c, 0)),
            state4,
            pl.BlockSpec((nb, N_HEADS, D_HEAD), lambda b, c: (b, 0, 0)),
            pl.BlockSpec((nb, 1, N_HEADS), lambda b, c: (b, 0, 0)),
            state4,
        ],
        out_shape=[
            jax.ShapeDtypeStruct((n_seq * n_chunks * t_len, D_MODEL), BF16),
            jax.ShapeDtypeStruct((n_seq, N_HEADS, D_HEAD, D_HEAD), F32),
            jax.ShapeDtypeStruct((n_seq, N_HEADS, D_HEAD), F32),
            jax.ShapeDtypeStruct((n_seq, 1, N_HEADS), F32),
            jax.ShapeDtypeStruct((n_seq, N_HEADS, D_HEAD, D_HEAD), F32),
        ],
        compiler_params=pltpu.CompilerParams(
            dimension_semantics=("arbitrary", "arbitrary"), vmem_limit_bytes=VMEM_LIMIT),
        name=f"mixer_t{t_len}",
    )(proj, c0, n0, m0, s0, bf_row, ga, gb, lb)


def _meta_kernel(x_ref, ge_ref, be_ref, win_ref, bin_ref, bf_ref, ga_ref, gb_ref, lb_ref,
                 wout_ref, bout_ref, g1_ref, b1_ref, wu_ref, bu_ref,
                 c_ref, n_ref, m_ref, s_ref, conv_ref, proj_scr):
    t_len = x_ref.shape[0]
    xn = _layer_norm(x_ref[...], ge_ref[...], be_ref[...])
    proj_scr[...] = _dot(xn.astype(BF16), win_ref[...]) + bin_ref[...]
    for ref in (c_ref, n_ref, m_ref, s_ref):
        ref[...] = jnp.zeros(ref.shape, ref.dtype)
    outs_a, outs_b = _mixer_body(proj_scr, bf_ref, ga_ref, gb_ref, lb_ref, c_ref, n_ref, m_ref, s_ref,
                                 t_len=t_len, n_seq_blk=1, activated=False)
    _finish_state(s_ref, 1)
    mix = jnp.concatenate([o.astype(BF16) for o in outs_a + outs_b], axis=1)
    y = _dot(mix, wout_ref[...]) + bout_ref[...]
    x1 = _layer_norm(ALPHA * xn + y, g1_ref[...], b1_ref[...])
    u = _dot(x1.astype(BF16), wu_ref[...]) + bu_ref[...]
    conv_ref[0] = u[t_len - (CONV_W - 1):, :]


def _meta_state(x, ln_e_g, ln_e_b, w_in, b_in, bf_row, ga, gb, lb, w_out, b_out, ln_g, ln_b,
                w_up, b_up):
    t_len = x.shape[0]
    full = lambda shape: _resident(shape, lambda i: (0,) * len(shape))
    vec, grp = full((1, D_MODEL)), full((1, D_GROUP))
    state4 = (1, N_HEADS, D_HEAD, D_HEAD)
    return pl.pallas_call(
        _meta_kernel,
        grid=(1,),
        in_specs=[full((t_len, D_MODEL)), vec, vec, full((D_MODEL, D_PROJ)), full((1, D_PROJ)),
                  full((1, LANES)), grp, grp, grp, full((D_MODEL, D_MODEL)), vec, vec, vec,
                  full((D_MODEL, D_FF)), full((1, D_FF))],
        out_specs=[pl.BlockSpec(shape, lambda i, rank=len(shape): (0,) * rank)
                   for shape in (state4, (1, N_HEADS, D_HEAD), (1, 1, N_HEADS), state4,
                                 (1, CONV_W - 1, D_FF))],
        out_shape=[
            jax.ShapeDtypeStruct(state4, F32),
            jax.ShapeDtypeStruct((1, N_HEADS, D_HEAD), F32),
            jax.ShapeDtypeStruct((1, 1, N_HEADS), F32),
            jax.ShapeDtypeStruct(state4, F32),
            jax.ShapeDtypeStruct((1, CONV_W - 1, D_FF), F32),
        ],
        scratch_shapes=[pltpu.VMEM((t_len, D_PROJ), F32)],
        compiler_params=pltpu.CompilerParams(
            dimension_semantics=("arbitrary",), vmem_limit_bytes=VMEM_LIMIT),
        name="meta_state",
    )(x, ln_e_g, ln_e_b, w_in, b_in, bf_row, ga, gb, lb, w_out, b_out, ln_g, ln_b, w_up, b_up)


def _in_proj_stages(x_ref, g_ref, b_ref, w_ref, bias_ref, lb_ref, act_ref, xn_ref):
    assert D_GROUP % IN_PROJ_STAGE_COLS == 0
    xn = _layer_norm(x_ref[...].reshape(xn_ref.shape), g_ref[...], b_ref[...])
    xn_ref[...] = xn
    x16 = xn.astype(BF16)
    for lo in range(0, D_PROJ, IN_PROJ_STAGE_COLS):
        hi = min(lo + IN_PROJ_STAGE_COLS, D_PROJ)
        yield
        block = _dot(x16, w_ref[:, lo:hi]) + bias_ref[:, lo:hi]
        group, off = divmod(lo, D_GROUP)
        if lo >= GATE_COL:
            act_ref[:, lo:hi] = block
            continue
        for dst, val in _activate(group, block, lb_ref[:, off:off + hi - lo]).items():
            base = _act_cols(dst, 0).start + off
            act_ref[:, base:base + hi - lo] = val


def _prompt_kernel(x0_ref, xnext_ref, ge_ref, be_ref, win_ref, bin_ref, c0_ref, n0_ref, m0_ref, s0_ref,
                   bf_ref, ga_ref, gb_ref, lb_ref, wout_ref, bout_ref, g1_ref, b1_ref,
                   x1_ref, c_ref, n_ref, m_ref, s_ref, proj_scr, xn_scr, proj_alt, xn_alt,
                   *, t_len, n_chunks, n_seq_blk):
    chunk = pl.program_id(1)
    step = pl.program_id(0) * n_chunks + chunk

    @pl.when(step == 0)
    def _():
        first = _in_proj_stages(x0_ref, ge_ref, be_ref, win_ref, bin_ref, lb_ref, proj_scr, xn_scr)
        _run_interleaved([first], [1])

    @pl.when(chunk == 0)
    def _():
        _load_state(c0_ref, n0_ref, m0_ref, s0_ref, c_ref, n_ref, m_ref, s_ref, n_seq_blk)

    def tile(proj_cur, xn_cur, proj_next, xn_next):
        next_proj = _in_proj_stages(xnext_ref, ge_ref, be_ref, win_ref, bin_ref, lb_ref,
                                    proj_next, xn_next)
        outs_a, outs_b = _mixer_body(
            proj_cur, bf_ref, ga_ref, gb_ref, lb_ref, c_ref, n_ref, m_ref, s_ref,
            t_len=t_len, n_seq_blk=n_seq_blk, activated=True, side_stages=next_proj)
        mix = jnp.concatenate(
            [jnp.concatenate([o.astype(BF16) for o in outs_a[s * N_HEADS:(s + 1) * N_HEADS]
                              + outs_b[s * N_HEADS:(s + 1) * N_HEADS]], axis=1)
             for s in range(n_seq_blk)], axis=0)
        y = _dot(mix, wout_ref[...]) + bout_ref[...]
        x1 = _layer_norm(ALPHA * xn_cur[...] + y, g1_ref[...], b1_ref[...])
        x1_ref[...] = x1.reshape(x1_ref.shape)

    @pl.when(step % 2 == 0)
    def _():
        tile(proj_scr, xn_scr, proj_alt, xn_alt)

    @pl.when(step % 2 == 1)
    def _():
        tile(proj_alt, xn_alt, proj_scr, xn_scr)

    @pl.when(chunk == n_chunks - 1)
    def _():
        _finish_state(s_ref, n_seq_blk)


def _prompt_mixer(x, ln_e_g, ln_e_b, w_in, b_in, c0, n0, m0, s0, bf_row, ga, gb, lb,
                  w_out, b_out, ln_g, ln_b, *, n_chunks, t_len, n_seq_blk):
    n_seq = x.shape[0]
    nb = n_seq_blk
    n_tiles = (n_seq // nb) * n_chunks
    const = lambda b, c: (0, 0)
    init4 = lambda b, c: (0, 0, 0, 0)
    init3 = lambda b, c: (0, 0, 0)
    vec = pl.BlockSpec((1, D_MODEL), const)
    grp = pl.BlockSpec((1, D_GROUP), const)
    state4 = pl.BlockSpec((nb, N_HEADS, D_HEAD, D_HEAD), lambda b, c: (b, 0, 0, 0))

    def next_tile(b, c):
        nxt = jnp.minimum(b * n_chunks + c + 1, n_tiles - 1)
        return (nxt // n_chunks, nxt % n_chunks, 0)

    return pl.pallas_call(
        functools.partial(_prompt_kernel, t_len=t_len, n_chunks=n_chunks, n_seq_blk=nb),
        grid=(n_seq // nb, n_chunks),
        in_specs=[
            _resident((nb, t_len, D_MODEL), init3),
            pl.BlockSpec((nb, t_len, D_MODEL), next_tile),
            vec, vec,
            _resident((D_MODEL, D_PROJ), const),
            pl.BlockSpec((1, D_PROJ), const),
            pl.BlockSpec((1, N_HEADS, D_HEAD, D_HEAD), init4),
            pl.BlockSpec((1, N_HEADS, D_HEAD), init3),
            pl.BlockSpec((1, 1, N_HEADS), init3),
            pl.BlockSpec((1, N_HEADS, D_HEAD, D_HEAD), init4),
            pl.BlockSpec((1, LANES), const),
            grp, grp, grp,
            _resident((D_MODEL, D_MODEL), const),
            vec, vec, vec,
        ],
        out_specs=[
            pl.BlockSpec((nb, t_len, D_MODEL), lambda b, c: (b, c, 0)),
            state4,
            pl.BlockSpec((nb, N_HEADS, D_HEAD), lambda b, c: (b, 0, 0)),
            pl.BlockSpec((nb, 1, N_HEADS), lambda b, c: (b, 0, 0)),
            state4,
        ],
        out_shape=[
            jax.ShapeDtypeStruct(x.shape, F32),
            jax.ShapeDtypeStruct((n_seq, N_HEADS, D_HEAD, D_HEAD), F32),
            jax.ShapeDtypeStruct((n_seq, N_HEADS, D_HEAD), F32),
            jax.ShapeDtypeStruct((n_seq, 1, N_HEADS), F32),
            jax.ShapeDtypeStruct((n_seq, N_HEADS, D_HEAD, D_HEAD), F32),
        ],
        scratch_shapes=[pltpu.VMEM((nb * t_len, D_ACT), F32), pltpu.VMEM((nb * t_len, D_MODEL), F32),
                        pltpu.VMEM((nb * t_len, D_ACT), F32), pltpu.VMEM((nb * t_len, D_MODEL), F32)],
        compiler_params=pltpu.CompilerParams(
            dimension_semantics=("arbitrary", "arbitrary"), vmem_limit_bytes=PROMPT_VMEM_LIMIT),
        name="prompt_mixer",
    )(x, x, ln_e_g, ln_e_b, w_in, b_in, c0, n0, m0, s0, bf_row, ga, gb, lb, w_out, b_out, ln_g, ln_b)


def _out_proj_kernel(x_ref, mix_ref, ge_ref, be_ref, w_ref, bias_ref, g_ref, b_ref, o_ref):
    xn = _layer_norm(x_ref[...], ge_ref[...], be_ref[...])
    y = _dot(mix_ref[...], w_ref[...]) + bias_ref[...]
    o_ref[...] = _layer_norm(ALPHA * xn + y, g_ref[...], b_ref[...])


def _out_proj(x, mix, ln_e_g, ln_e_b, w, bias, ln_g, ln_b, *, tm):
    n = x.shape[0]
    const = lambda i: (0, 0)
    row = pl.BlockSpec((tm, D_MODEL), lambda i: (i, 0))
    vec = pl.BlockSpec((1, D_MODEL), const)
    return pl.pallas_call(
        _out_proj_kernel,
        grid=(n // tm,),
        in_specs=[row, row, vec, vec, pl.BlockSpec((D_MODEL, D_MODEL), const), vec, vec, vec],
        out_specs=row,
        out_shape=jax.ShapeDtypeStruct((n, D_MODEL), F32),
        compiler_params=pltpu.CompilerParams(
            dimension_semantics=("arbitrary",), vmem_limit_bytes=VMEM_LIMIT),
        name="out_proj",
    )(x, mix, ln_e_g, ln_e_b, w, bias, ln_g, ln_b)


def _ffn_kernel(x_ref, cs_ref, wu_ref, bu_ref, wc_ref, bc_ref, wd_ref, bd_ref, g_ref, b_ref,
                y_ref, nc_ref, full_ref, *, n_seq_blk, t_len):
    hist = SUBLANES - (CONV_W - 1)

    @pl.when(pl.program_id(1) == 0)
    def _():
        full_ref[:, hist:SUBLANES, :] = cs_ref[...]

    x = x_ref[...]
    up = _dot(x.astype(BF16), wu_ref[...]) + bu_ref[...]
    u = up[:, :D_FF].reshape(n_seq_blk, t_len, D_FF)
    gate = up[:, D_FF:].reshape(n_seq_blk, t_len, D_FF)
    full_ref[:, SUBLANES:SUBLANES + t_len, :] = u
    conv = bc_ref[...] + u * wc_ref[CONV_W - 1:CONV_W, :]
    for j in range(CONV_W - 1):
        conv = conv + full_ref[:, hist + j:hist + j + t_len, :] * wc_ref[j:j + 1, :]
    last = full_ref[:, hist + t_len:SUBLANES + t_len, :]
    nc_ref[...] = last
    full_ref[:, hist:SUBLANES, :] = last
    act = (conv * _sigmoid(conv) * gate).reshape(n_seq_blk * t_len, D_FF)
    ffn = _dot(act.astype(BF16), wd_ref[...]) + bd_ref[...]
    y_ref[...] = _layer_norm(ALPHA * x + ffn, g_ref[...], b_ref[...])


def _ffn(x, conv_state, w_up, b_up, w_conv, b_conv, w_down, b_down, ln_g, ln_b,
         *, n_seq, seq_len, n_seq_blk, t_len, shared_init):
    n_t = seq_len // t_len
    rows = n_seq_blk * t_len
    const = lambda s, t: (0, 0)
    cs_map = (lambda s, t: (0, 0, 0)) if shared_init else (lambda s, t: (s, 0, 0))
    row = pl.BlockSpec((rows, D_MODEL), lambda s, t: (s * n_t + t, 0))
    vec = pl.BlockSpec((1, D_MODEL), const)
    return pl.pallas_call(
        functools.partial(_ffn_kernel, n_seq_blk=n_seq_blk, t_len=t_len),
        grid=(n_seq // n_seq_blk, n_t),
        in_specs=[
            row,
            pl.BlockSpec((n_seq_blk, CONV_W - 1, D_FF), cs_map),
            _resident((D_MODEL, 2 * D_FF), const),
            pl.BlockSpec((1, 2 * D_FF), const),
            pl.BlockSpec((CONV_W, D_FF), const),
            pl.BlockSpec((1, D_FF), const),
            _resident((D_FF, D_MODEL), const),
            vec, vec, vec,
        ],
        out_specs=[row, pl.BlockSpec((n_seq_blk, CONV_W - 1, D_FF), lambda s, t: (s, 0, 0))],
        out_shape=[
            jax.ShapeDtypeStruct((n_seq * seq_len, D_MODEL), F32),
            jax.ShapeDtypeStruct((n_seq, CONV_W - 1, D_FF), F32),
        ],
        scratch_shapes=[pltpu.VMEM((n_seq_blk, SUBLANES + t_len, D_FF), F32)],
        compiler_params=pltpu.CompilerParams(
            dimension_semantics=("arbitrary", "arbitrary"), vmem_limit_bytes=VMEM_LIMIT),
        name=f"ffn_t{t_len}",
    )(x, conv_state, w_up, b_up, w_conv, b_conv, w_down, b_down, ln_g, ln_b)


def kernel(x_prompt, x_sample, state_mlstm_C, state_mlstm_n, state_mlstm_m, state_hgrn_S, state_ffn_conv, meta_tokens, ln_emb_g, ln_emb_b, w_in, b_in, b_fgate_a, g_norm_a, g_norm_b, hgrn_lb_logits, w_out, b_out, ln1_g, ln1_b, w_up, b_up, w_conv, b_conv, w_down, b_down, ln2_g, ln2_b):
    assert w_in.shape[0] == DEPTH == 1
    n_prompt, seq, _ = x_prompt.shape
    n_sample, dec_seq, _ = x_sample.shape
    row = lambda v: v.reshape(1, -1).astype(F32)

    gate0 = 4 * D_GROUP
    gate1 = gate0 + 2 * N_HEADS
    pad = D_PROJ - w_in.shape[2]
    w_in_p = _regroup_in_proj_weight(w_in[0].astype(BF16), tm=256)
    b_in_p = jnp.concatenate(
        [b_in[0][:gate0], b_in[0][gate1:], b_in[0][gate0:gate1], jnp.zeros((pad,), b_in.dtype)]
    ).reshape(1, D_PROJ).astype(F32)
    bf_row = jnp.zeros((1, LANES), F32).at[0, N_HEADS:2 * N_HEADS].set(b_fgate_a[0].astype(F32))
    lb = jnp.cumsum(jax.nn.softmax(hgrn_lb_logits.astype(F32), axis=0), axis=0)[0].reshape(1, D_GROUP)
    ga, gb = row(g_norm_a[0]), row(g_norm_b[0])
    ln_e = (row(ln_emb_g), row(ln_emb_b))
    out_p = (w_out[0].astype(BF16), row(b_out[0]), row(ln1_g[0]), row(ln1_b[0]))
    ffn_p = (w_up[0].astype(BF16), row(b_up[0]), w_conv[0].astype(F32), row(b_conv[0]),
             w_down[0].astype(BF16), row(b_down[0]), row(ln2_g[0]), row(ln2_b[0]))

    def layer(x_rows, mixer_state, conv_state, *, n_seq, seq_len, t_mix, mix_seq_blk, tm,
              ffn_seq_blk, ffn_t, shared_init):
        proj = _in_proj(x_rows, *ln_e, w_in_p, b_in_p, tm=tm)
        mix, c_new, n_new, m_new, s_new = _mixer(
            proj, *mixer_state, bf_row, ga, gb, lb, n_seq=n_seq, n_chunks=seq_len // t_mix,
            t_len=t_mix, n_seq_blk=mix_seq_blk, shared_init=shared_init)
        x1 = _out_proj(x_rows, mix, *ln_e, *out_p, tm=tm)
        y, conv_new = _ffn(x1, conv_state, *ffn_p, n_seq=n_seq, seq_len=seq_len,
                           n_seq_blk=ffn_seq_blk, t_len=ffn_t, shared_init=shared_init)
        return y, c_new, n_new, m_new, s_new, conv_new

    c_m, n_m, m_m, s_m, conv_m = _meta_state(
        meta_tokens.astype(F32), *ln_e, w_in_p, b_in_p, bf_row, ga, gb, lb, *out_p, ffn_p[0], ffn_p[1])

    x1_p, c_p, n_p, m_p, s_p = _prompt_mixer(
        x_prompt.astype(F32), *ln_e, w_in_p, b_in_p, c_m, n_m, m_m, s_m, bf_row, ga, gb, lb, *out_p,
        n_chunks=seq // PROMPT_CHUNK, t_len=PROMPT_CHUNK, n_seq_blk=PROMPT_SEQS_PER_STEP)
    y_p, conv_p = _ffn(x1_p.reshape(n_prompt * seq, D_MODEL), conv_m, *ffn_p, n_seq=n_prompt,
                       seq_len=seq, n_seq_blk=1, t_len=512, shared_init=True)

    sample_state = (state_mlstm_C[0].astype(F32), state_mlstm_n[0].astype(F32),
                    state_mlstm_m[0].astype(F32).reshape(n_sample, 1, N_HEADS),
                    state_hgrn_S[0].astype(F32))
    y_s, c_s, n_s, m_s, s_s, conv_s = layer(
        x_sample.reshape(n_sample * dec_seq, D_MODEL), sample_state, state_ffn_conv[0].astype(F32),
        n_seq=n_sample, seq_len=dec_seq, t_mix=dec_seq, mix_seq_blk=8, tm=256, ffn_seq_blk=32,
        ffn_t=dec_seq, shared_init=False)

    lead = lambda v: v[None]
    return (y_p.reshape(n_prompt, seq, D_MODEL), y_s.reshape(n_sample, dec_seq, D_MODEL),
            lead(c_p), lead(n_p), lead(m_p.reshape(n_prompt, N_HEADS)), lead(s_p), lead(conv_p),
            lead(c_s), lead(n_s), lead(m_s.reshape(n_sample, N_HEADS)), lead(s_s), lead(conv_s))
```

```python
import functools

import jax
import jax.numpy as jnp
from jax import lax
from jax.experimental import pallas as pl
from jax.experimental.pallas import tpu as pltpu

D_MODEL = 1024
N_META = 16
N_HEADS = 4
D_HEAD = 128
D_GROUP = N_HEADS * D_HEAD
D_FF = 2816
CONV_W = 3
DEPTH = 1
ALPHA = (2.0 * DEPTH) ** 0.25
LN_EPS = 1e-5
RMS_EPS = 1e-6
NEG_LOG2_E = -1.4426950408889634

LANES = 128
SUBLANES = 8
GATE_COL = 8 * D_GROUP
D_PROJ = GATE_COL + LANES
KB_GROUP, F_GROUP = 8, 9
D_ACT = D_PROJ + 2 * D_GROUP
IN_PROJ_STAGE_COLS = 256
SIDE_STAGES_PER_ROUND = 3
PROMPT_SEQS_PER_STEP = 2
PROMPT_CHUNK = 128
VMEM_LIMIT = 56 * 1024 * 1024

F32 = jnp.float32
BF16 = jnp.bfloat16
NT_DIMS = (((1,), (1,)), ((), ()))
TN_DIMS = (((0,), (0,)), ((), ()))


def _layer_norm(x, g, b):
    mu = jnp.mean(x, axis=-1, keepdims=True)
    xc = x - mu
    var = jnp.mean(xc * xc, axis=-1, keepdims=True)
    return xc * lax.rsqrt(var + LN_EPS) * g + b


def _exp_neg(x):
    return jnp.exp2(x * NEG_LOG2_E)


def _sigmoid(x):
    return 1.0 / (1.0 + _exp_neg(x))


def _resident(block_shape, index_map):
    return pl.BlockSpec(block_shape, index_map, pipeline_mode=pl.Buffered(1))


def _dot(a, b):
    return jnp.dot(a, b, preferred_element_type=F32)


def _dot_nt(a, b):
    return lax.dot_general(a, b, NT_DIMS, preferred_element_type=F32)


def _dot_tn(a, b):
    return lax.dot_general(a, b, TN_DIMS, preferred_element_type=F32)


def _regroup_kernel(w_ref, o_ref):
    gate0 = 4 * D_GROUP
    gate1 = gate0 + 2 * N_HEADS
    rows = w_ref.shape[0]
    o_ref[:, :gate0] = w_ref[:, :gate0]
    o_ref[:, gate0:GATE_COL] = w_ref[:, gate1:gate1 + gate0]
    o_ref[:, GATE_COL:] = jnp.concatenate(
        [w_ref[:, gate0:gate1], jnp.zeros((rows, LANES - (gate1 - gate0)), w_ref.dtype)], axis=1)


def _regroup_in_proj_weight(w, *, tm):
    n, cols = w.shape
    return pl.pallas_call(
        _regroup_kernel,
        grid=(n // tm,),
        in_specs=[pl.BlockSpec((tm, cols), lambda i: (i, 0))],
        out_specs=pl.BlockSpec((tm, D_PROJ), lambda i: (i, 0)),
        out_shape=jax.ShapeDtypeStruct((n, D_PROJ), w.dtype),
        compiler_params=pltpu.CompilerParams(dimension_semantics=("arbitrary",)),
        name="regroup_w_in",
    )(w)


def _in_proj_kernel(x_ref, g_ref, b_ref, w_ref, bias_ref, o_ref):
    xn = _layer_norm(x_ref[...], g_ref[...], b_ref[...])
    o_ref[...] = _dot(xn.astype(BF16), w_ref[...]) + bias_ref[...]


def _in_proj(x, ln_g, ln_b, w, bias, *, tm):
    n = x.shape[0]
    const = lambda i: (0, 0)
    return pl.pallas_call(
        _in_proj_kernel,
        grid=(n // tm,),
        in_specs=[
            pl.BlockSpec((tm, D_MODEL), lambda i: (i, 0)),
            pl.BlockSpec((1, D_MODEL), const),
            pl.BlockSpec((1, D_MODEL), const),
            pl.BlockSpec((D_MODEL, D_PROJ), const),
            pl.BlockSpec((1, D_PROJ), const),
        ],
        out_specs=pl.BlockSpec((tm, D_PROJ), lambda i: (i, 0)),
        out_shape=jax.ShapeDtypeStruct((n, D_PROJ), F32),
        compiler_params=pltpu.CompilerParams(
            dimension_semantics=("arbitrary",), vmem_limit_bytes=VMEM_LIMIT),
        name="in_proj",
    )(x, ln_g, ln_b, w, bias)


def _block_rows(x, level, t_len, row_in_block):
    size = 2 << level
    if size > SUBLANES:
        pieces = [jnp.broadcast_to(x[j * size + row_in_block:j * size + row_in_block + 1, :],
                                   (size, x.shape[1])) for j in range(t_len // size)]
        return pieces[0] if len(pieces) == 1 else jnp.concatenate(pieces, axis=0)
    x3 = x.reshape(t_len // SUBLANES, SUBLANES, x.shape[1])
    sub = lax.broadcasted_iota(jnp.int32, x3.shape, 1)
    out = None
    for j in range(SUBLANES // size):
        row = jnp.broadcast_to(x3[:, j * size + row_in_block:j * size + row_in_block + 1, :], x3.shape)
        out = row if out is None else jnp.where(sub >= j * size, row, out)
    return out.reshape(x.shape)


def _interleave_halves(lower, upper, level, t_len):
    half = 1 << level
    if half >= SUBLANES:
        pieces = []
        for j in range(t_len // (2 * half)):
            pieces.append(lower[2 * half * j:2 * half * j + half])
            pieces.append(upper[2 * half * j + half:2 * half * (j + 1)])
        return jnp.concatenate(pieces, axis=0)
    rows = lax.broadcasted_iota(jnp.int32, lower.shape, 0)
    return jnp.where((rows & half) != 0, upper, lower)


def _run_interleaved(gens, stages_per_round):
    results = [None] * len(gens)
    live = [True] * len(gens)
    while any(live):
        for g, steps in enumerate(stages_per_round):
            for _ in range(steps):
                if live[g]:
                    try:
                        next(gens[g])
                    except StopIteration as stop:
                        results[g], live[g] = stop.value, False
    return results


def _cumsum_rows(tril16, x):
    hi = x.astype(BF16)
    rest = x - hi.astype(F32)
    mid = rest.astype(BF16)
    lo = (rest - mid.astype(F32)).astype(BF16)
    return _dot(tril16, hi) + _dot(tril16, mid) + _dot(tril16, lo)


def _activate(group, x, lb=None):
    if group == 1:
        return {1: x * (D_HEAD ** -0.5)}
    if group in (3, 7):
        return {group: _sigmoid(x)}
    if group == 4:
        return {4: x * _sigmoid(x)}
    if group == 5:
        f = lb + (1.0 - lb) * _sigmoid(x)
        return {5: jnp.log(f), KB_GROUP: (1.0 - lb) / (1.0 + jnp.exp(x)), F_GROUP: f}
    return {group: x}


def _mlstm_units(*, q, k, v, gate, z, zt, m0_row, c_old, n_old, g_norm, causal, lane, t_len):
    idx = range(len(q))
    seq = [i // N_HEADS for i in idx]
    b_lane = [N_HEADS + i % N_HEADS for i in idx]
    q16 = [q[i].astype(BF16) for i in idx]
    k16 = [k[i].astype(BF16) for i in idx]
    qk = [_dot_nt(q16[i], k16[i]) for i in idx]
    qc = [_dot(q16[i], c_old[i].astype(BF16)) for i in idx]
    bs_row = [zt[seq[i]][b_lane[i]:b_lane[i] + 1, :] - zt[seq[i]][i % N_HEADS:i % N_HEADS + 1, :]
              for i in idx]
    yield
    col = lambda per_seq, i: per_seq[seq[i]][:, b_lane[i]:b_lane[i] + 1]
    d = [jnp.where(causal, col(z, i) - bs_row[i], -jnp.inf) for i in idx]
    row_max = [jnp.max(d[i], axis=1, keepdims=True) for i in idx]
    last = slice(t_len - 1, t_len)
    m_t_seq, dec_seq, floor_seq, w_last_seq = [], [], [], []
    for s in range(len(z)):
        r = z[s] + m0_row[s]
        d_max = jnp.full(r.shape, -jnp.inf, F32)
        for h in range(N_HEADS):
            d_max = jnp.where(lane == N_HEADS + h, row_max[s * N_HEADS + h], d_max)
        m_t = jnp.maximum(r, d_max)
        i_gate = pltpu.roll(z[s], N_HEADS, axis=1)
        m_t_seq.append(m_t)
        dec_seq.append(jnp.exp(r - m_t))
        floor_seq.append(_exp_neg(m_t))
        w_last_seq.append(jnp.exp(z[s][last] - z[s] + i_gate - m_t[last]))
    m_t = [col(m_t_seq, i) for i in idx]
    dec = [col(dec_seq, i) for i in idx]
    w_last = [col(w_last_seq, i) for i in idx]
    sw = [jnp.exp(d[i] - m_t[i]) * qk[i] for i in idx]
    yield
    swv = [_dot(sw[i].astype(BF16), v[i].astype(BF16)) for i in idx]
    kv = [_dot_tn(k16[i], (w_last[i] * v[i]).astype(BF16)) for i in idx]
    yield
    c_new = [dec[i][last] * c_old[i] + kv[i] for i in idx]
    n_new = [dec[i][last] * n_old[i] + jnp.sum(w_last[i] * k[i], axis=0, keepdims=True) for i in idx]
    m_new = [m_t[i][last] for i in idx]
    den = [dec[i] * jnp.sum(q[i] * n_old[i], axis=1, keepdims=True)
           + jnp.sum(sw[i], axis=1, keepdims=True) for i in idx]
    hid = [(dec[i] * qc[i] + swv[i]) / jnp.maximum(jnp.abs(den[i]), col(floor_seq, i)) for i in idx]
    yield
    rms = [lax.rsqrt(jnp.mean(hid[i] * hid[i], axis=1, keepdims=True) + RMS_EPS) for i in idx]
    out = [gate[i] * (hid[i] * rms[i] * g_norm[i]) for i in idx]
    return out, c_new, n_new, m_new


def _hgrn_units(*, qb, log_f, f, kb, iv, gate, s_old_t, g_norm, tril, level_of, t_len):
    idx = range(len(qb))
    n_levels = t_len.bit_length() - 1
    a = [_cumsum_rows(tril, log_f[i]) for i in idx]
    yield
    iv16 = [iv[i].astype(BF16) for i in idx]
    diag = [_dot_nt(qb[i].astype(BF16), kb[i].astype(BF16)) for i in idx]
    scores = [jnp.where(level_of == -2, diag[i], 0.0) for i in idx]
    for level in range(n_levels):
        yield
        x16 = []
        for i in idx:
            base = _interleave_halves(kb[i], qb[i], level, t_len)
            if level == 0:
                x = base * _interleave_halves(jnp.ones_like(f[i]), f[i], 0, t_len)
            else:
                ref = _block_rows(a[i], level, t_len, (1 << level) - 1)
                x = base * _exp_neg(jnp.abs(a[i] - ref))
            x16.append(x.astype(BF16))
        part = [_dot_nt(x16[i], x16[i]) for i in idx]
        scores = [jnp.where(level_of == level, part[i], scores[i]) for i in idx]
    yield
    last = slice(t_len - 1, t_len)
    q_in = [(qb[i] * jnp.exp(a[i])).astype(BF16) for i in idx]
    k_out = [(kb[i] * jnp.exp(a[i][last] - a[i])).astype(BF16) for i in idx]
    inter = [_dot_nt(q_in[i], s_old_t[i].astype(BF16)) for i in idx]
    intra = [_dot(scores[i].astype(BF16), iv16[i]) for i in idx]
    kv = [_dot_tn(iv16[i], k_out[i]) for i in idx]
    yield
    s_new_t = [jnp.exp(a[i][last]) * s_old_t[i] + kv[i] for i in idx]
    o = [inter[i] + intra[i] for i in idx]
    rms = [lax.rsqrt(jnp.mean(o[i] * o[i], axis=1, keepdims=True) + RMS_EPS) for i in idx]
    out = [gate[i] * (o[i] * rms[i] * g_norm[i]) for i in idx]
    return out, s_new_t


def _load_state(c0_ref, n0_ref, m0_ref, s0_ref, c_ref, n_ref, m_ref, s_ref, n_seq_blk):
    shared = c0_ref.shape[0] == 1 and n_seq_blk > 1
    for s in range(n_seq_blk):
        src = 0 if shared else s
        c_ref[s] = c0_ref[src]
        n_ref[s] = n0_ref[src]
        m_ref[s] = m0_ref[src]
        for h in range(N_HEADS):
            s_ref[s, h] = s0_ref[src, h].T


def _finish_state(s_ref, n_seq_blk):
    for s in range(n_seq_blk):
        for h in range(N_HEADS):
            s_ref[s, h] = s_ref[s, h].T


def _act_cols(group, h):
    base = group * D_GROUP if group < KB_GROUP else D_PROJ + (group - KB_GROUP) * D_GROUP
    return slice(base + h * D_HEAD, base + (h + 1) * D_HEAD)


def _mixer_body(p_ref, bf_ref, ga_ref, gb_ref, lb_ref, c_ref, n_ref, m_ref, s_ref,
                *, t_len, n_seq_blk, activated, side_stages=None):
    rows = lax.broadcasted_iota(jnp.int32, (t_len, t_len), 0)
    cols = lax.broadcasted_iota(jnp.int32, (t_len, t_len), 1)
    causal = cols <= rows
    tril = causal.astype(BF16)
    level_of = jnp.where(rows > cols, 31 - lax.clz(rows ^ cols), jnp.where(rows == cols, -2, -1))
    lane = lax.broadcasted_iota(jnp.int32, (t_len, LANES), 1)
    is_f = (lane >= N_HEADS) & (lane < 2 * N_HEADS)
    hd = lambda j, h: slice(j * D_GROUP + h * D_HEAD, j * D_GROUP + (h + 1) * D_HEAD)

    units = [(s, h) for s in range(n_seq_blk) for h in range(N_HEADS)]
    rs = lambda s: slice(s * t_len, (s + 1) * t_len)

    z_seq, zt_seq, n_seq, m_seq = [], [], [], []
    for s in range(n_seq_blk):
        gates = p_ref[rs(s), GATE_COL:GATE_COL + LANES]
        log_f = jnp.where(is_f, jax.nn.log_sigmoid(gates + bf_ref[...]), 0.0)
        cum_f = _cumsum_rows(tril, log_f)
        z_seq.append(jnp.where(is_f, cum_f, jnp.where(lane < N_HEADS, gates, 0.0)))
        zt_seq.append(z_seq[s].T)
        n_seq.append(n_ref[s])
        m_seq.append(m_ref[s])
    computed = {}

    def act(group):
        if activated:
            return [p_ref[rs(s), _act_cols(group, h)] for s, h in units]
        source = 5 if group >= KB_GROUP else group
        for s, h in units:
            if (source, s, h) not in computed:
                computed[source, s, h] = _activate(source, p_ref[rs(s), hd(source, h)],
                                                   lb_ref[:, hd(0, h)])
        return [computed[source, s, h][group] for s, h in units]

    lane_row = lane[:1, :]
    m0_rows = []
    for s in range(n_seq_blk):
        m0_row = jnp.zeros((1, LANES), F32)
        for h in range(N_HEADS):
            m0_row = jnp.where(lane_row == N_HEADS + h, m_seq[s][:, h:h + 1], m0_row)
        m0_rows.append(m0_row)
    mlstm = _mlstm_units(
        q=act(0), k=act(1), v=act(2), gate=act(3), z=z_seq, zt=zt_seq, m0_row=m0_rows,
        c_old=[c_ref[s, h] for s, h in units], n_old=[n_seq[s][h:h + 1, :] for s, h in units],
        g_norm=[ga_ref[:, hd(0, h)] for s, h in units], causal=causal, lane=lane, t_len=t_len)
    hgrn = _hgrn_units(
        qb=act(4), log_f=act(5), f=act(F_GROUP), kb=act(KB_GROUP), iv=act(6), gate=act(7),
        s_old_t=[s_ref[s, h] for s, h in units],
        g_norm=[gb_ref[:, hd(0, h)] for s, h in units], tril=tril, level_of=level_of, t_len=t_len)
    gens, per_round = [mlstm, hgrn], [1, 2]
    if side_stages is not None:
        gens, per_round = gens + [side_stages], per_round + [SIDE_STAGES_PER_ROUND]
    results = _run_interleaved(gens, per_round)
    (outs, c_new, n_new, m_new), (outs_b, s_new_t) = results[:2]

    for i, (s, h) in enumerate(units):
        c_ref[s, h] = c_new[i]
        s_ref[s, h] = s_new_t[i]
    head_lane = lax.broadcasted_iota(jnp.int32, (1, N_HEADS), 1)
    for s in range(n_seq_blk):
        n_ref[s] = jnp.concatenate(n_new[s * N_HEADS:(s + 1) * N_HEADS], axis=0)
        m_row = m_seq[s]
        for h in range(N_HEADS):
            m_row = jnp.where(head_lane == h, m_new[s * N_HEADS + h], m_row)
        m_ref[s] = m_row
    return outs, outs_b


def _mixer_kernel(p_ref, c0_ref, n0_ref, m0_ref, s0_ref, bf_ref, ga_ref, gb_ref, lb_ref,
                  mix_ref, c_ref, n_ref, m_ref, s_ref, *, t_len, n_chunks, n_seq_blk):
    chunk = pl.program_id(1)

    @pl.when(chunk == 0)
    def _():
        _load_state(c0_ref, n0_ref, m0_ref, s0_ref, c_ref, n_ref, m_ref, s_ref, n_seq_blk)

    outs_a, outs_b = _mixer_body(p_ref, bf_ref, ga_ref, gb_ref, lb_ref, c_ref, n_ref, m_ref, s_ref,
                                 t_len=t_len, n_seq_blk=n_seq_blk, activated=False)
    for i in range(n_seq_blk * N_HEADS):
        s, h = divmod(i, N_HEADS)
        rows = slice(s * t_len, (s + 1) * t_len)
        mix_ref[rows, h * D_HEAD:(h + 1) * D_HEAD] = outs_a[i].astype(mix_ref.dtype)
        mix_ref[rows, D_GROUP + h * D_HEAD:D_GROUP + (h + 1) * D_HEAD] = outs_b[i].astype(mix_ref.dtype)

    @pl.when(chunk == n_chunks - 1)
    def _():
        _finish_state(s_ref, n_seq_blk)


def _mixer(proj, c0, n0, m0, s0, bf_row, ga, gb, lb, *, n_seq, n_chunks, t_len, n_seq_blk,
           shared_init):
    assert not shared_init or n_seq_blk == 1
    assert n_chunks == 1 or n_seq_blk == 1
    nb = n_seq_blk
    init = (lambda b, c: (0, 0, 0, 0)) if shared_init else (lambda b, c: (b, 0, 0, 0))
    init3 = (lambda b, c: (0, 0, 0)) if shared_init else (lambda b, c: (b, 0, 0))
    const = lambda b, c: (0, 0)
    state4 = pl.BlockSpec((nb, N_HEADS, D_HEAD, D_HEAD), lambda b, c: (b, 0, 0, 0))
    return pl.pallas_call(
        functools.partial(_mixer_kernel, t_len=t_len, n_chunks=n_chunks, n_seq_blk=nb),
        grid=(n_seq // nb, n_chunks),
        in_specs=[
            pl.BlockSpec((nb * t_len, D_PROJ), lambda b, c: (b * n_chunks + c, 0)),
            pl.BlockSpec((nb, N_HEADS, D_HEAD, D_HEAD), init),
            pl.BlockSpec((nb, N_HEADS, D_HEAD), init3),
            pl.BlockSpec((nb, 1, N_HEADS), init3),
            pl.BlockSpec((nb, N_HEADS, D_HEAD, D_HEAD), init),
            pl.BlockSpec((1, LANES), const),
            pl.BlockSpec((1, D_GROUP), const),
            pl.BlockSpec((1, D_GROUP), const),
            pl.BlockSpec((1, D_GROUP), const),
        ],
        out_specs=[
            pl.BlockSpec((nb * t_len, D_MODEL), lambda b, c: (b * n_chunks + c, 0)),
            state4,
            pl.BlockSpec((nb, N_HEADS, D_HEAD), lambda b, c: (b, 0, 0)),
            pl.BlockSpec((nb, 1, N_HEADS), lambda b, c: (b, 0, 0)),
            state4,
        ],
        out_shape=[
            jax.ShapeDtypeStruct((n_seq * n_chunks * t_len, D_MODEL), BF16),
            jax.ShapeDtypeStruct((n_seq, N_HEADS, D_HEAD, D_HEAD), F32),
            jax.ShapeDtypeStruct((n_seq, N_HEADS, D_HEAD), F32),
            jax.ShapeDtypeStruct((n_seq, 1, N_HEADS), F32),
            jax.ShapeDtypeStruct((n_seq, N_HEADS, D_HEAD, D_HEAD), F32),
        ],
        compiler_params=pltpu.CompilerParams(
            dimension_semantics=("arbitrary", "arbitrary"), vmem_limit_bytes=VMEM_LIMIT),
        name=f"mixer_t{t_len}",
    )(proj, c0, n0, m0, s0, bf_row, ga, gb, lb)


def _meta_kernel(x_ref, ge_ref, be_ref, win_ref, bin_ref, bf_ref, ga_ref, gb_ref, lb_ref,
                 wout_ref, bout_ref, g1_ref, b1_ref, wu_ref, bu_ref,
                 c_ref, n_ref, m_ref, s_ref, conv_ref, proj_scr):
    t_len = x_ref.shape[0]
    xn = _layer_norm(x_ref[...], ge_ref[...], be_ref[...])
    proj_scr[...] = _dot(xn.astype(BF16), win_ref[...]) + bin_ref[...]
    for ref in (c_ref, n_ref, m_ref, s_ref):
        ref[...] = jnp.zeros(ref.shape, ref.dtype)
    outs_a, outs_b = _mixer_body(proj_scr, bf_ref, ga_ref, gb_ref, lb_ref, c_ref, n_ref, m_ref, s_ref,
                                 t_len=t_len, n_seq_blk=1, activated=False)
    _finish_state(s_ref, 1)
    mix = jnp.concatenate([o.astype(BF16) for o in outs_a + outs_b], axis=1)
    y = _dot(mix, wout_ref[...]) + bout_ref[...]
    x1 = _layer_norm(ALPHA * xn + y, g1_ref[...], b1_ref[...])
    u = _dot(x1.astype(BF16), wu_ref[...]) + bu_ref[...]
    conv_ref[0] = u[t_len - (CONV_W - 1):, :]


def _meta_state(x, ln_e_g, ln_e_b, w_in, b_in, bf_row, ga, gb, lb, w_out, b_out, ln_g, ln_b,
                w_up, b_up):
    t_len = x.shape[0]
    full = lambda shape: _resident(shape, lambda i: (0,) * len(shape))
    vec, grp = full((1, D_MODEL)), full((1, D_GROUP))
    state4 = (1, N_HEADS, D_HEAD, D_HEAD)
    return pl.pallas_call(
        _meta_kernel,
        grid=(1,),
        in_specs=[full((t_len, D_MODEL)), vec, vec, full((D_MODEL, D_PROJ)), full((1, D_PROJ)),
                  full((1, LANES)), grp, grp, grp, full((D_MODEL, D_MODEL)), vec, vec, vec,
                  full((D_MODEL, D_FF)), full((1, D_FF))],
        out_specs=[pl.BlockSpec(shape, lambda i, rank=len(shape): (0,) * rank)
                   for shape in (state4, (1, N_HEADS, D_HEAD), (1, 1, N_HEADS), state4,
                                 (1, CONV_W - 1, D_FF))],
        out_shape=[
            jax.ShapeDtypeStruct(state4, F32),
            jax.ShapeDtypeStruct((1, N_HEADS, D_HEAD), F32),
            jax.ShapeDtypeStruct((1, 1, N_HEADS), F32),
            jax.ShapeDtypeStruct(state4, F32),
            jax.ShapeDtypeStruct((1, CONV_W - 1, D_FF), F32),
        ],
        scratch_shapes=[pltpu.VMEM((t_len, D_PROJ), F32)],
        compiler_params=pltpu.CompilerParams(
            dimension_semantics=("arbitrary",), vmem_limit_bytes=VMEM_LIMIT),
        name="meta_state",
    )(x, ln_e_g, ln_e_b, w_in, b_in, bf_row, ga, gb, lb, w_out, b_out, ln_g, ln_b, w_up, b_up)


def _in_proj_stages(x_ref, g_ref, b_ref, w_ref, bias_ref, lb_ref, act_ref, xn_ref):
    assert D_GROUP % IN_PROJ_STAGE_COLS == 0
    xn = _layer_norm(x_ref[...].reshape(xn_ref.shape), g_ref[...], b_ref[...])
    xn_ref[...] = xn
    x16 = xn.astype(BF16)
    for lo in range(0, D_PROJ, IN_PROJ_STAGE_COLS):
        hi = min(lo + IN_PROJ_STAGE_COLS, D_PROJ)
        yield
        block = _dot(x16, w_ref[:, lo:hi]) + bias_ref[:, lo:hi]
        group, off = divmod(lo, D_GROUP)
        if lo >= GATE_COL:
            act_ref[:, lo:hi] = block
            continue
        for dst, val in _activate(group, block, lb_ref[:, off:off + hi - lo]).items():
            base = _act_cols(dst, 0).start + off
            act_ref[:, base:base + hi - lo] = val


def _prompt_kernel(x0_ref, xnext_ref, ge_ref, be_ref, win_ref, bin_ref, c0_ref, n0_ref, m0_ref, s0_ref,
                   bf_ref, ga_ref, gb_ref, lb_ref, wout_ref, bout_ref, g1_ref, b1_ref,
                   x1_ref, c_ref, n_ref, m_ref, s_ref, proj_scr, xn_scr, proj_alt, xn_alt,
                   *, t_len, n_chunks, n_seq_blk):
    chunk = pl.program_id(1)
    step = pl.program_id(0) * n_chunks + chunk

    @pl.when(step == 0)
    def _():
        first = _in_proj_stages(x0_ref, ge_ref, be_ref, win_ref, bin_ref, lb_ref, proj_scr, xn_scr)
        _run_interleaved([first], [1])

    @pl.when(chunk == 0)
    def _():
        _load_state(c0_ref, n0_ref, m0_ref, s0_ref, c_ref, n_ref, m_ref, s_ref, n_seq_blk)

    def tile(proj_cur, xn_cur, proj_next, xn_next):
        next_proj = _in_proj_stages(xnext_ref, ge_ref, be_ref, win_ref, bin_ref, lb_ref,
                                    proj_next, xn_next)
        outs_a, outs_b = _mixer_body(
            proj_cur, bf_ref, ga_ref, gb_ref, lb_ref, c_ref, n_ref, m_ref, s_ref,
            t_len=t_len, n_seq_blk=n_seq_blk, activated=True, side_stages=next_proj)
        mix = jnp.concatenate(
            [jnp.concatenate([o.astype(BF16) for o in outs_a[s * N_HEADS:(s + 1) * N_HEADS]
                              + outs_b[s * N_HEADS:(s + 1) * N_HEADS]], axis=1)
             for s in range(n_seq_blk)], axis=0)
        y = _dot(mix, wout_ref[...]) + bout_ref[...]
        x1 = _layer_norm(ALPHA * xn_cur[...] + y, g1_ref[...], b1_ref[...])
        x1_ref[...] = x1.reshape(x1_ref.shape)

    @pl.when(step % 2 == 0)
    def _():
        tile(proj_scr, xn_scr, proj_alt, xn_alt)

    @pl.when(step % 2 == 1)
    def _():
        tile(proj_alt, xn_alt, proj_scr, xn_scr)

    @pl.when(chunk == n_chunks - 1)
    def _():
        _finish_state(s_ref, n_seq_blk)


def _prompt_mixer(x, ln_e_g, ln_e_b, w_in, b_in, c0, n0, m0, s0, bf_row, ga, gb, lb,
                  w_out, b_out, ln_g, ln_b, *, n_chunks, t_len, n_seq_blk):
    n_seq = x.shape[0]
    nb = n_seq_blk
    n_tiles = (n_seq // nb) * n_chunks
    const = lambda b, c: (0, 0)
    init4 = lambda b, c: (0, 0, 0, 0)
    init3 = lambda b, c: (0, 0, 0)
    vec = pl.BlockSpec((1, D_MODEL), const)
    grp = pl.BlockSpec((1, D_GROUP), const)
    state4 = pl.BlockSpec((nb, N_HEADS, D_HEAD, D_HEAD), lambda b, c: (b, 0, 0, 0))

    def next_tile(b, c):
        nxt = jnp.minimum(b * n_chunks + c + 1, n_tiles - 1)
        return (nxt // n_chunks, nxt % n_chunks, 0)

    return pl.pallas_call(
        functools.partial(_prompt_kernel, t_len=t_len, n_chunks=n_chunks, n_seq_blk=nb),
        grid=(n_seq // nb, n_chunks),
        in_specs=[
            _resident((nb, t_len, D_MODEL), init3),
            pl.BlockSpec((nb, t_len, D_MODEL), next_tile),
            vec, vec,
            _resident((D_MODEL, D_PROJ), const),
            pl.BlockSpec((1, D_PROJ), const),
            pl.BlockSpec((1, N_HEADS, D_HEAD, D_HEAD), init4),
            pl.BlockSpec((1, N_HEADS, D_HEAD), init3),
            pl.BlockSpec((1, 1, N_HEADS), init3),
            pl.BlockSpec((1, N_HEADS, D_HEAD, D_HEAD), init4),
            pl.BlockSpec((1, LANES), const),
            grp, grp, grp,
            _resident((D_MODEL, D_MODEL), const),
            vec, vec, vec,
        ],
        out_specs=[
            pl.BlockSpec((nb, t_len, D_MODEL), lambda b, c: (b, c, 0)),
            state4,
            pl.BlockSpec((nb, N_HEADS, D_HEAD), lambda b, c: (b, 0, 0)),
            pl.BlockSpec((nb, 1, N_HEADS), lambda b, c: (b, 0, 0)),
            state4,
        ],
        out_shape=[
            jax.ShapeDtypeStruct(x.shape, F32),
            jax.ShapeDtypeStruct((n_seq, N_HEADS, D_HEAD, D_HEAD), F32),
            jax.ShapeDtypeStruct((n_seq, N_HEADS, D_HEAD), F32),
            jax.ShapeDtypeStruct((n_seq, 1, N_HEADS), F32),
            jax.ShapeDtypeStruct((n_seq, N_HEADS, D_HEAD, D_HEAD), F32),
        ],
        scratch_shapes=[pltpu.VMEM((nb * t_len, D_ACT), F32), pltpu.VMEM((nb * t_len, D_MODEL), F32),
                        pltpu.VMEM((nb * t_len, D_ACT), F32), pltpu.VMEM((nb * t_len, D_MODEL), F32)],
        compiler_params=pltpu.CompilerParams(
            dimension_semantics=("arbitrary", "arbitrary"), vmem_limit_bytes=VMEM_LIMIT),
        name="prompt_mixer",
    )(x, x, ln_e_g, ln_e_b, w_in, b_in, c0, n0, m0, s0, bf_row, ga, gb, lb, w_out, b_out, ln_g, ln_b)


def _ffn_kernel(x_ref, *refs, n_seq_blk, t_len):
    _ffn_tile(x_ref[...], *refs, n_seq_blk=n_seq_blk, t_len=t_len)


def _out_proj_ffn_kernel(x_ref, mix_ref, ge_ref, be_ref, wo_ref, bo_ref, g1_ref, b1_ref, *refs,
                         n_seq_blk, t_len):
    xn = _layer_norm(x_ref[...], ge_ref[...], be_ref[...])
    y = _dot(mix_ref[...], wo_ref[...]) + bo_ref[...]
    x1 = _layer_norm(ALPHA * xn + y, g1_ref[...], b1_ref[...])
    _ffn_tile(x1, *refs, n_seq_blk=n_seq_blk, t_len=t_len)


def _ffn_tile(x, cs_ref, wu_ref, bu_ref, wc_ref, bc_ref, wd_ref, bd_ref, g_ref, b_ref,
              y_ref, nc_ref, full_ref, *, n_seq_blk, t_len):
    hist = SUBLANES - (CONV_W - 1)

    @pl.when(pl.program_id(1) == 0)
    def _():
        full_ref[:, hist:SUBLANES, :] = cs_ref[...]

    up = _dot(x.astype(BF16), wu_ref[...]) + bu_ref[...]
    u = up[:, :D_FF].reshape(n_seq_blk, t_len, D_FF)
    gate = up[:, D_FF:].reshape(n_seq_blk, t_len, D_FF)
    full_ref[:, SUBLANES:SUBLANES + t_len, :] = u
    conv = bc_ref[...] + u * wc_ref[CONV_W - 1:CONV_W, :]
    for j in range(CONV_W - 1):
        conv = conv + full_ref[:, hist + j:hist + j + t_len, :] * wc_ref[j:j + 1, :]
    last = full_ref[:, hist + t_len:SUBLANES + t_len, :]
    nc_ref[...] = last
    full_ref[:, hist:SUBLANES, :] = last
    act = (conv * _sigmoid(conv) * gate).reshape(n_seq_blk * t_len, D_FF)
    ffn = _dot(act.astype(BF16), wd_ref[...]) + bd_ref[...]
    y_ref[...] = _layer_norm(ALPHA * x + ffn, g_ref[...], b_ref[...])


def _ffn(x, conv_state, w_up, b_up, w_conv, b_conv, w_down, b_down, ln_g, ln_b,
         *, n_seq, seq_len, n_seq_blk, t_len, shared_init, out_proj=None):
    n_t = seq_len // t_len
    rows = n_seq_blk * t_len
    const = lambda s, t: (0, 0)
    cs_map = (lambda s, t: (0, 0, 0)) if shared_init else (lambda s, t: (s, 0, 0))
    row = pl.BlockSpec((rows, D_MODEL), lambda s, t: (s * n_t + t, 0))
    vec = pl.BlockSpec((1, D_MODEL), const)
    body, lead_specs, lead_args = _ffn_kernel, [row], (x,)
    if out_proj is not None:
        body = _out_proj_ffn_kernel
        lead_specs = [row, row, vec, vec, _resident((D_MODEL, D_MODEL), const), vec, vec, vec]
        lead_args = (x,) + tuple(out_proj)
    return pl.pallas_call(
        functools.partial(body, n_seq_blk=n_seq_blk, t_len=t_len),
        grid=(n_seq // n_seq_blk, n_t),
        in_specs=lead_specs + [
            pl.BlockSpec((n_seq_blk, CONV_W - 1, D_FF), cs_map),
            _resident((D_MODEL, 2 * D_FF), const),
            pl.BlockSpec((1, 2 * D_FF), const),
            pl.BlockSpec((CONV_W, D_FF), const),
            pl.BlockSpec((1, D_FF), const),
            _resident((D_FF, D_MODEL), const),
            vec, vec, vec,
        ],
        out_specs=[row, pl.BlockSpec((n_seq_blk, CONV_W - 1, D_FF), lambda s, t: (s, 0, 0))],
        out_shape=[
            jax.ShapeDtypeStruct((n_seq * seq_len, D_MODEL), F32),
            jax.ShapeDtypeStruct((n_seq, CONV_W - 1, D_FF), F32),
        ],
        scratch_shapes=[pltpu.VMEM((n_seq_blk, SUBLANES + t_len, D_FF), F32)],
        compiler_params=pltpu.CompilerParams(
            dimension_semantics=("arbitrary", "arbitrary"), vmem_limit_bytes=VMEM_LIMIT),
        name=f"ffn_t{t_len}",
    )(*lead_args, conv_state, w_up, b_up, w_conv, b_conv, w_down, b_down, ln_g, ln_b)


def kernel(x_prompt, x_sample, state_mlstm_C, state_mlstm_n, state_mlstm_m, state_hgrn_S, state_ffn_conv, meta_tokens, ln_emb_g, ln_emb_b, w_in, b_in, b_fgate_a, g_norm_a, g_norm_b, hgrn_lb_logits, w_out, b_out, ln1_g, ln1_b, w_up, b_up, w_conv, b_conv, w_down, b_down, ln2_g, ln2_b):
    assert w_in.shape[0] == DEPTH == 1
    n_prompt, seq, _ = x_prompt.shape
    n_sample, dec_seq, _ = x_sample.shape
    row = lambda v: v.reshape(1, -1).astype(F32)

    gate0 = 4 * D_GROUP
    gate1 = gate0 + 2 * N_HEADS
    pad = D_PROJ - w_in.shape[2]
    w_in_p = _regroup_in_proj_weight(w_in[0].astype(BF16), tm=256)
    b_in_p = jnp.concatenate(
        [b_in[0][:gate0], b_in[0][gate1:], b_in[0][gate0:gate1], jnp.zeros((pad,), b_in.dtype)]
    ).reshape(1, D_PROJ).astype(F32)
    bf_row = jnp.zeros((1, LANES), F32).at[0, N_HEADS:2 * N_HEADS].set(b_fgate_a[0].astype(F32))
    lb = jnp.cumsum(jax.nn.softmax(hgrn_lb_logits.astype(F32), axis=0), axis=0)[0].reshape(1, D_GROUP)
    ga, gb = row(g_norm_a[0]), row(g_norm_b[0])
    ln_e = (row(ln_emb_g), row(ln_emb_b))
    out_p = (w_out[0].astype(BF16), row(b_out[0]), row(ln1_g[0]), row(ln1_b[0]))
    ffn_p = (w_up[0].astype(BF16), row(b_up[0]), w_conv[0].astype(F32), row(b_conv[0]),
             w_down[0].astype(BF16), row(b_down[0]), row(ln2_g[0]), row(ln2_b[0]))

    c_m, n_m, m_m, s_m, conv_m = _meta_state(
        meta_tokens.astype(F32), *ln_e, w_in_p, b_in_p, bf_row, ga, gb, lb, *out_p, ffn_p[0], ffn_p[1])

    x1_p, c_p, n_p, m_p, s_p = _prompt_mixer(
        x_prompt.astype(F32), *ln_e, w_in_p, b_in_p, c_m, n_m, m_m, s_m, bf_row, ga, gb, lb, *out_p,
        n_chunks=seq // PROMPT_CHUNK, t_len=PROMPT_CHUNK, n_seq_blk=PROMPT_SEQS_PER_STEP)
    y_p, conv_p = _ffn(x1_p.reshape(n_prompt * seq, D_MODEL), conv_m, *ffn_p, n_seq=n_prompt,
                       seq_len=seq, n_seq_blk=1, t_len=512, shared_init=True)

    sample_state = (state_mlstm_C[0].astype(F32), state_mlstm_n[0].astype(F32),
                    state_mlstm_m[0].astype(F32).reshape(n_sample, 1, N_HEADS),
                    state_hgrn_S[0].astype(F32))
    xs_rows = x_sample.reshape(n_sample * dec_seq, D_MODEL).astype(F32)
    proj_s = _in_proj(xs_rows, *ln_e, w_in_p, b_in_p, tm=256)
    mix_s, c_s, n_s, m_s, s_s = _mixer(
        proj_s, *sample_state, bf_row, ga, gb, lb, n_seq=n_sample, n_chunks=1, t_len=dec_seq,
        n_seq_blk=8, shared_init=False)
    y_s, conv_s = _ffn(xs_rows, state_ffn_conv[0].astype(F32), *ffn_p, n_seq=n_sample,
                       seq_len=dec_seq, n_seq_blk=32, t_len=dec_seq, shared_init=False,
                       out_proj=(mix_s, *ln_e, *out_p))

    lead = lambda v: v[None]
    return (y_p.reshape(n_prompt, seq, D_MODEL), y_s.reshape(n_sample, dec_seq, D_MODEL),
            lead(c_p), lead(n_p), lead(m_p.reshape(n_prompt, N_HEADS)), lead(s_p), lead(conv_p),
            lead(c_s), lead(n_s), lead(m_s.reshape(n_sample, N_HEADS)), lead(s_s), lead(conv_s))
```

```python
import functools

import jax
import jax.numpy as jnp
from jax import lax
from jax.experimental import pallas as pl
from jax.experimental.pallas import tpu as pltpu

D_MODEL = 1024
N_META = 16
N_HEADS = 4
D_HEAD = 128
D_GROUP = N_HEADS * D_HEAD
D_FF = 2816
CONV_W = 3
DEPTH = 1
ALPHA = (2.0 * DEPTH) ** 0.25
LN_EPS = 1e-5
RMS_EPS = 1e-6
NEG_LOG2_E = -1.4426950408889634

LANES = 128
SUBLANES = 8
GATE_COL = 8 * D_GROUP
D_PROJ = GATE_COL + LANES
KB_GROUP, F_GROUP = 8, 9
D_ACT = D_PROJ + 2 * D_GROUP
IN_PROJ_STAGE_COLS = 256
SIDE_STAGES_PER_ROUND = 3
PROMPT_SEQS_PER_STEP = 2
PROMPT_CHUNK = 128
VMEM_LIMIT = 56 * 1024 * 1024

F32 = jnp.float32
BF16 = jnp.bfloat16
NT_DIMS = (((1,), (1,)), ((), ()))
TN_DIMS = (((0,), (0,)), ((), ()))


def _layer_norm(x, g, b):
    mu = jnp.mean(x, axis=-1, keepdims=True)
    xc = x - mu
    var = jnp.mean(xc * xc, axis=-1, keepdims=True)
    return xc * lax.rsqrt(var + LN_EPS) * g + b


def _exp_neg(x):
    return jnp.exp2(x * NEG_LOG2_E)


def _sigmoid(x):
    return 1.0 / (1.0 + _exp_neg(x))


def _resident(block_shape, index_map):
    return pl.BlockSpec(block_shape, index_map, pipeline_mode=pl.Buffered(1))


def _dot(a, b):
    return jnp.dot(a, b, preferred_element_type=F32)


def _dot_nt(a, b):
    return lax.dot_general(a, b, NT_DIMS, preferred_element_type=F32)


def _dot_tn(a, b):
    return lax.dot_general(a, b, TN_DIMS, preferred_element_type=F32)


def _regroup_kernel(wt_ref, o_ref):
    gate0 = 4 * D_GROUP
    gate1 = gate0 + 2 * N_HEADS
    for j in range(GATE_COL // LANES):
        src = j * LANES if j * LANES < gate0 else j * LANES + (gate1 - gate0)
        o_ref[:, j * LANES:(j + 1) * LANES] = wt_ref[src:src + LANES, :].T.astype(o_ref.dtype)
    gates = wt_ref[gate0:gate1, :].T.astype(o_ref.dtype)
    o_ref[:, GATE_COL:] = jnp.concatenate(
        [gates, jnp.zeros((gates.shape[0], LANES - gates.shape[1]), o_ref.dtype)], axis=1)


def _regroup_in_proj_weight(w_t, *, tm):
    cols, n = w_t.shape
    return pl.pallas_call(
        _regroup_kernel,
        grid=(n // tm,),
        in_specs=[pl.BlockSpec((cols, tm), lambda i: (0, i))],
        out_specs=pl.BlockSpec((tm, D_PROJ), lambda i: (i, 0)),
        out_shape=jax.ShapeDtypeStruct((n, D_PROJ), BF16),
        compiler_params=pltpu.CompilerParams(dimension_semantics=("arbitrary",)),
        name="regroup_w_in",
    )(w_t)


def _in_proj_kernel(x_ref, g_ref, b_ref, w_ref, bias_ref, o_ref):
    xn = _layer_norm(x_ref[...], g_ref[...], b_ref[...])
    o_ref[...] = _dot(xn.astype(BF16), w_ref[...]) + bias_ref[...]


def _in_proj(x, ln_g, ln_b, w, bias, *, tm):
    n = x.shape[0]
    const = lambda i: (0, 0)
    return pl.pallas_call(
        _in_proj_kernel,
        grid=(n // tm,),
        in_specs=[
            pl.BlockSpec((tm, D_MODEL), lambda i: (i, 0)),
            pl.BlockSpec((1, D_MODEL), const),
            pl.BlockSpec((1, D_MODEL), const),
            pl.BlockSpec((D_MODEL, D_PROJ), const),
            pl.BlockSpec((1, D_PROJ), const),
        ],
        out_specs=pl.BlockSpec((tm, D_PROJ), lambda i: (i, 0)),
        out_shape=jax.ShapeDtypeStruct((n, D_PROJ), F32),
        compiler_params=pltpu.CompilerParams(
            dimension_semantics=("arbitrary",), vmem_limit_bytes=VMEM_LIMIT),
        name="in_proj",
    )(x, ln_g, ln_b, w, bias)


def _block_rows(x, level, t_len, row_in_block):
    size = 2 << level
    if size > SUBLANES:
        pieces = [jnp.broadcast_to(x[j * size + row_in_block:j * size + row_in_block + 1, :],
                                   (size, x.shape[1])) for j in range(t_len // size)]
        return pieces[0] if len(pieces) == 1 else jnp.concatenate(pieces, axis=0)
    x3 = x.reshape(t_len // SUBLANES, SUBLANES, x.shape[1])
    sub = lax.broadcasted_iota(jnp.int32, x3.shape, 1)
    out = None
    for j in range(SUBLANES // size):
        row = jnp.broadcast_to(x3[:, j * size + row_in_block:j * size + row_in_block + 1, :], x3.shape)
        out = row if out is None else jnp.where(sub >= j * size, row, out)
    return out.reshape(x.shape)


def _interleave_halves(lower, upper, level, t_len):
    half = 1 << level
    if half >= SUBLANES:
        pieces = []
        for j in range(t_len // (2 * half)):
            pieces.append(lower[2 * half * j:2 * half * j + half])
            pieces.append(upper[2 * half * j + half:2 * half * (j + 1)])
        return jnp.concatenate(pieces, axis=0)
    rows = lax.broadcasted_iota(jnp.int32, lower.shape, 0)
    return jnp.where((rows & half) != 0, upper, lower)


def _run_interleaved(gens, stages_per_round):
    results = [None] * len(gens)
    live = [True] * len(gens)
    while any(live):
        for g, steps in enumerate(stages_per_round):
            for _ in range(steps):
                if live[g]:
                    try:
                        next(gens[g])
                    except StopIteration as stop:
                        results[g], live[g] = stop.value, False
    return results


def _cumsum_rows(tril16, x):
    hi = x.astype(BF16)
    rest = x - hi.astype(F32)
    mid = rest.astype(BF16)
    lo = (rest - mid.astype(F32)).astype(BF16)
    return _dot(tril16, hi) + _dot(tril16, mid) + _dot(tril16, lo)


def _activate(group, x, lb=None):
    if group == 1:
        return {1: x * (D_HEAD ** -0.5)}
    if group in (3, 7):
        return {group: _sigmoid(x)}
    if group == 4:
        return {4: x * _sigmoid(x)}
    if group == 5:
        f = lb + (1.0 - lb) * _sigmoid(x)
        return {5: jnp.log(f), KB_GROUP: (1.0 - lb) / (1.0 + jnp.exp(x)), F_GROUP: f}
    return {group: x}


def _mlstm_units(*, q, k, v, gate, z, zt, m0_row, c_old, n_old, g_norm, causal, lane, t_len):
    idx = range(len(q))
    seq = [i // N_HEADS for i in idx]
    b_lane = [N_HEADS + i % N_HEADS for i in idx]
    q16 = [q[i].astype(BF16) for i in idx]
    k16 = [k[i].astype(BF16) for i in idx]
    qk = [_dot_nt(q16[i], k16[i]) for i in idx]
    qc = [_dot(q16[i], c_old[i].astype(BF16)) for i in idx]
    bs_row = [zt[seq[i]][b_lane[i]:b_lane[i] + 1, :] - zt[seq[i]][i % N_HEADS:i % N_HEADS + 1, :]
              for i in idx]
    yield
    col = lambda per_seq, i: per_seq[seq[i]][:, b_lane[i]:b_lane[i] + 1]
    d = [jnp.where(causal, col(z, i) - bs_row[i], -jnp.inf) for i in idx]
    row_max = [jnp.max(d[i], axis=1, keepdims=True) for i in idx]
    last = slice(t_len - 1, t_len)
    m_t_seq, dec_seq, floor_seq, w_last_seq = [], [], [], []
    for s in range(len(z)):
        r = z[s] + m0_row[s]
        d_max = jnp.full(r.shape, -jnp.inf, F32)
        for h in range(N_HEADS):
            d_max = jnp.where(lane == N_HEADS + h, row_max[s * N_HEADS + h], d_max)
        m_t = jnp.maximum(r, d_max)
        i_gate = pltpu.roll(z[s], N_HEADS, axis=1)
        m_t_seq.append(m_t)
        dec_seq.append(jnp.exp(r - m_t))
        floor_seq.append(_exp_neg(m_t))
        w_last_seq.append(jnp.exp(z[s][last] - z[s] + i_gate - m_t[last]))
    m_t = [col(m_t_seq, i) for i in idx]
    dec = [col(dec_seq, i) for i in idx]
    w_last = [col(w_last_seq, i) for i in idx]
    sw = [jnp.exp(d[i] - m_t[i]) * qk[i] for i in idx]
    yield
    swv = [_dot(sw[i].astype(BF16), v[i].astype(BF16)) for i in idx]
    kv = [_dot_tn(k16[i], (w_last[i] * v[i]).astype(BF16)) for i in idx]
    yield
    c_new = [dec[i][last] * c_old[i] + kv[i] for i in idx]
    n_new = [dec[i][last] * n_old[i] + jnp.sum(w_last[i] * k[i], axis=0, keepdims=True) for i in idx]
    m_new = [m_t[i][last] for i in idx]
    den = [dec[i] * jnp.sum(q[i] * n_old[i], axis=1, keepdims=True)
           + jnp.sum(sw[i], axis=1, keepdims=True) for i in idx]
    hid = [(dec[i] * qc[i] + swv[i]) / jnp.maximum(jnp.abs(den[i]), col(floor_seq, i)) for i in idx]
    yield
    rms = [lax.rsqrt(jnp.mean(hid[i] * hid[i], axis=1, keepdims=True) + RMS_EPS) for i in idx]
    out = [gate[i] * (hid[i] * rms[i] * g_norm[i]) for i in idx]
    return out, c_new, n_new, m_new


def _hgrn_units(*, qb, log_f, f, kb, iv, gate, s_old_t, g_norm, tril, level_of, t_len):
    idx = range(len(qb))
    n_levels = t_len.bit_length() - 1
    a = [_cumsum_rows(tril, log_f[i]) for i in idx]
    yield
    iv16 = [iv[i].astype(BF16) for i in idx]
    diag = [_dot_nt(qb[i].astype(BF16), kb[i].astype(BF16)) for i in idx]
    scores = [jnp.where(level_of == -2, diag[i], 0.0) for i in idx]
    for level in range(n_levels):
        yield
        x16 = []
        for i in idx:
            base = _interleave_halves(kb[i], qb[i], level, t_len)
            if level == 0:
                x = base * _interleave_halves(jnp.ones_like(f[i]), f[i], 0, t_len)
            else:
                ref = _block_rows(a[i], level, t_len, (1 << level) - 1)
                x = base * _exp_neg(jnp.abs(a[i] - ref))
            x16.append(x.astype(BF16))
        part = [_dot_nt(x16[i], x16[i]) for i in idx]
        scores = [jnp.where(level_of == level, part[i], scores[i]) for i in idx]
    yield
    last = slice(t_len - 1, t_len)
    q_in = [(qb[i] * jnp.exp(a[i])).astype(BF16) for i in idx]
    k_out = [(kb[i] * jnp.exp(a[i][last] - a[i])).astype(BF16) for i in idx]
    inter = [_dot_nt(q_in[i], s_old_t[i].astype(BF16)) for i in idx]
    intra = [_dot(scores[i].astype(BF16), iv16[i]) for i in idx]
    kv = [_dot_tn(iv16[i], k_out[i]) for i in idx]
    yield
    s_new_t = [jnp.exp(a[i][last]) * s_old_t[i] + kv[i] for i in idx]
    o = [inter[i] + intra[i] for i in idx]
    rms = [lax.rsqrt(jnp.mean(o[i] * o[i], axis=1, keepdims=True) + RMS_EPS) for i in idx]
    out = [gate[i] * (o[i] * rms[i] * g_norm[i]) for i in idx]
    return out, s_new_t


def _load_state(c0_ref, n0_ref, m0_ref, s0_ref, c_ref, n_ref, m_ref, s_ref, n_seq_blk):
    shared = c0_ref.shape[0] == 1 and n_seq_blk > 1
    for s in range(n_seq_blk):
        src = 0 if shared else s
        c_ref[s] = c0_ref[src]
        n_ref[s] = n0_ref[src]
        m_ref[s] = m0_ref[src]
        for h in range(N_HEADS):
            s_ref[s, h] = s0_ref[src, h].T


def _finish_state(s_ref, n_seq_blk):
    for s in range(n_seq_blk):
        for h in range(N_HEADS):
            s_ref[s, h] = s_ref[s, h].T


def _act_cols(group, h):
    base = group * D_GROUP if group < KB_GROUP else D_PROJ + (group - KB_GROUP) * D_GROUP
    return slice(base + h * D_HEAD, base + (h + 1) * D_HEAD)


def _mixer_body(p_ref, bf_ref, ga_ref, gb_ref, lb_ref, c_ref, n_ref, m_ref, s_ref,
                *, t_len, n_seq_blk, activated, side_stages=None):
    rows = lax.broadcasted_iota(jnp.int32, (t_len, t_len), 0)
    cols = lax.broadcasted_iota(jnp.int32, (t_len, t_len), 1)
    causal = cols <= rows
    tril = causal.astype(BF16)
    level_of = jnp.where(rows > cols, 31 - lax.clz(rows ^ cols), jnp.where(rows == cols, -2, -1))
    lane = lax.broadcasted_iota(jnp.int32, (t_len, LANES), 1)
    is_f = (lane >= N_HEADS) & (lane < 2 * N_HEADS)
    hd = lambda j, h: slice(j * D_GROUP + h * D_HEAD, j * D_GROUP + (h + 1) * D_HEAD)

    units = [(s, h) for s in range(n_seq_blk) for h in range(N_HEADS)]
    rs = lambda s: slice(s * t_len, (s + 1) * t_len)

    z_seq, zt_seq, n_seq, m_seq = [], [], [], []
    for s in range(n_seq_blk):
        gates = p_ref[rs(s), GATE_COL:GATE_COL + LANES]
        log_f = jnp.where(is_f, jax.nn.log_sigmoid(gates + bf_ref[...]), 0.0)
        cum_f = _cumsum_rows(tril, log_f)
        z_seq.append(jnp.where(is_f, cum_f, jnp.where(lane < N_HEADS, gates, 0.0)))
        zt_seq.append(z_seq[s].T)
        n_seq.append(n_ref[s])
        m_seq.append(m_ref[s])
    computed = {}

    def act(group):
        if activated:
            return [p_ref[rs(s), _act_cols(group, h)] for s, h in units]
        source = 5 if group >= KB_GROUP else group
        for s, h in units:
            if (source, s, h) not in computed:
                computed[source, s, h] = _activate(source, p_ref[rs(s), hd(source, h)],
                                                   lb_ref[:, hd(0, h)])
        return [computed[source, s, h][group] for s, h in units]

    lane_row = lane[:1, :]
    m0_rows = []
    for s in range(n_seq_blk):
        m0_row = jnp.zeros((1, LANES), F32)
        for h in range(N_HEADS):
            m0_row = jnp.where(lane_row == N_HEADS + h, m_seq[s][:, h:h + 1], m0_row)
        m0_rows.append(m0_row)
    mlstm = _mlstm_units(
        q=act(0), k=act(1), v=act(2), gate=act(3), z=z_seq, zt=zt_seq, m0_row=m0_rows,
        c_old=[c_ref[s, h] for s, h in units], n_old=[n_seq[s][h:h + 1, :] for s, h in units],
        g_norm=[ga_ref[:, hd(0, h)] for s, h in units], causal=causal, lane=lane, t_len=t_len)
    hgrn = _hgrn_units(
        qb=act(4), log_f=act(5), f=act(F_GROUP), kb=act(KB_GROUP), iv=act(6), gate=act(7),
        s_old_t=[s_ref[s, h] for s, h in units],
        g_norm=[gb_ref[:, hd(0, h)] for s, h in units], tril=tril, level_of=level_of, t_len=t_len)
    gens, per_round = [mlstm, hgrn], [1, 2]
    if side_stages is not None:
        gens, per_round = gens + [side_stages], per_round + [SIDE_STAGES_PER_ROUND]
    results = _run_interleaved(gens, per_round)
    (outs, c_new, n_new, m_new), (outs_b, s_new_t) = results[:2]

    for i, (s, h) in enumerate(units):
        c_ref[s, h] = c_new[i]
        s_ref[s, h] = s_new_t[i]
    head_lane = lax.broadcasted_iota(jnp.int32, (1, N_HEADS), 1)
    for s in range(n_seq_blk):
        n_ref[s] = jnp.concatenate(n_new[s * N_HEADS:(s + 1) * N_HEADS], axis=0)
        m_row = m_seq[s]
        for h in range(N_HEADS):
            m_row = jnp.where(head_lane == h, m_new[s * N_HEADS + h], m_row)
        m_ref[s] = m_row
    return outs, outs_b


def _mixer_kernel(p_ref, c0_ref, n0_ref, m0_ref, s0_ref, bf_ref, ga_ref, gb_ref, lb_ref,
                  mix_ref, c_ref, n_ref, m_ref, s_ref, *, t_len, n_chunks, n_seq_blk):
    chunk = pl.program_id(1)

    @pl.when(chunk == 0)
    def _():
        _load_state(c0_ref, n0_ref, m0_ref, s0_ref, c_ref, n_ref, m_ref, s_ref, n_seq_blk)

    outs_a, outs_b = _mixer_body(p_ref, bf_ref, ga_ref, gb_ref, lb_ref, c_ref, n_ref, m_ref, s_ref,
                                 t_len=t_len, n_seq_blk=n_seq_blk, activated=False)
    for i in range(n_seq_blk * N_HEADS):
        s, h = divmod(i, N_HEADS)
        rows = slice(s * t_len, (s + 1) * t_len)
        mix_ref[rows, h * D_HEAD:(h + 1) * D_HEAD] = outs_a[i].astype(mix_ref.dtype)
        mix_ref[rows, D_GROUP + h * D_HEAD:D_GROUP + (h + 1) * D_HEAD] = outs_b[i].astype(mix_ref.dtype)

    @pl.when(chunk == n_chunks - 1)
    def _():
        _finish_state(s_ref, n_seq_blk)


def _mixer(proj, c0, n0, m0, s0, bf_row, ga, gb, lb, *, n_seq, n_chunks, t_len, n_seq_blk,
           shared_init):
    assert not shared_init or n_seq_blk == 1
    assert n_chunks == 1 or n_seq_blk == 1
    nb = n_seq_blk
    init = (lambda b, c: (0, 0, 0, 0)) if shared_init else (lambda b, c: (b, 0, 0, 0))
    init3 = (lambda b, c: (0, 0, 0)) if shared_init else (lambda b, c: (b, 0, 0))
    const = lambda b, c: (0, 0)
    state4 = pl.BlockSpec((nb, N_HEADS, D_HEAD, D_HEAD), lambda b, c: (b, 0, 0, 0))
    return pl.pallas_call(
        functools.partial(_mixer_kernel, t_len=t_len, n_chunks=n_chunks, n_seq_blk=nb),
        grid=(n_seq // nb, n_chunks),
        in_specs=[
            pl.BlockSpec((nb * t_len, D_PROJ), lambda b, c: (b * n_chunks + c, 0)),
            pl.BlockSpec((nb, N_HEADS, D_HEAD, D_HEAD), init),
            pl.BlockSpec((nb, N_HEADS, D_HEAD), init3),
            pl.BlockSpec((nb, 1, N_HEADS), init3),
            pl.BlockSpec((nb, N_HEADS, D_HEAD, D_HEAD), init),
            pl.BlockSpec((1, LANES), const),
            pl.BlockSpec((1, D_GROUP), const),
            pl.BlockSpec((1, D_GROUP), const),
            pl.BlockSpec((1, D_GROUP), const),
        ],
        out_specs=[
            pl.BlockSpec((nb * t_len, D_MODEL), lambda b, c: (b * n_chunks + c, 0)),
            state4,
            pl.BlockSpec((nb, N_HEADS, D_HEAD), lambda b, c: (b, 0, 0)),
            pl.BlockSpec((nb, 1, N_HEADS), lambda b, c: (b, 0, 0)),
            state4,
        ],
        out_shape=[
            jax.ShapeDtypeStruct((n_seq * n_chunks * t_len, D_MODEL), BF16),
            jax.ShapeDtypeStruct((n_seq, N_HEADS, D_HEAD, D_HEAD), F32),
            jax.ShapeDtypeStruct((n_seq, N_HEADS, D_HEAD), F32),
            jax.ShapeDtypeStruct((n_seq, 1, N_HEADS), F32),
            jax.ShapeDtypeStruct((n_seq, N_HEADS, D_HEAD, D_HEAD), F32),
        ],
        compiler_params=pltpu.CompilerParams(
            dimension_semantics=("arbitrary", "arbitrary"), vmem_limit_bytes=VMEM_LIMIT),
        name=f"mixer_t{t_len}",
    )(proj, c0, n0, m0, s0, bf_row, ga, gb, lb)


def _meta_kernel(x_ref, ge_ref, be_ref, win_ref, bin_ref, bf_ref, ga_ref, gb_ref, lb_ref,
                 wout_ref, bout_ref, g1_ref, b1_ref, wu_ref, bu_ref,
                 c_ref, n_ref, m_ref, s_ref, conv_ref, proj_scr):
    t_len = x_ref.shape[0]
    xn = _layer_norm(x_ref[...], ge_ref[...], be_ref[...])
    proj_scr[...] = _dot(xn.astype(BF16), win_ref[...]) + bin_ref[...]
    for ref in (c_ref, n_ref, m_ref, s_ref):
        ref[...] = jnp.zeros(ref.shape, ref.dtype)
    outs_a, outs_b = _mixer_body(proj_scr, bf_ref, ga_ref, gb_ref, lb_ref, c_ref, n_ref, m_ref, s_ref,
                                 t_len=t_len, n_seq_blk=1, activated=False)
    _finish_state(s_ref, 1)
    mix = jnp.concatenate([o.astype(BF16) for o in outs_a + outs_b], axis=1)
    y = _dot(mix, wout_ref[...]) + bout_ref[...]
    x1 = _layer_norm(ALPHA * xn + y, g1_ref[...], b1_ref[...])
    u = _dot(x1.astype(BF16), wu_ref[...]) + bu_ref[...]
    conv_ref[0] = u[t_len - (CONV_W - 1):, :]


def _meta_state(x, ln_e_g, ln_e_b, w_in, b_in, bf_row, ga, gb, lb, w_out, b_out, ln_g, ln_b,
                w_up, b_up):
    t_len = x.shape[0]
    full = lambda shape: _resident(shape, lambda i: (0,) * len(shape))
    vec, grp = full((1, D_MODEL)), full((1, D_GROUP))
    state4 = (1, N_HEADS, D_HEAD, D_HEAD)
    return pl.pallas_call(
        _meta_kernel,
        grid=(1,),
        in_specs=[full((t_len, D_MODEL)), vec, vec, full((D_MODEL, D_PROJ)), full((1, D_PROJ)),
                  full((1, LANES)), grp, grp, grp, full((D_MODEL, D_MODEL)), vec, vec, vec,
                  full((D_MODEL, D_FF)), full((1, D_FF))],
        out_specs=[pl.BlockSpec(shape, lambda i, rank=len(shape): (0,) * rank)
                   for shape in (state4, (1, N_HEADS, D_HEAD), (1, 1, N_HEADS), state4,
                                 (1, CONV_W - 1, D_FF))],
        out_shape=[
            jax.ShapeDtypeStruct(state4, F32),
            jax.ShapeDtypeStruct((1, N_HEADS, D_HEAD), F32),
            jax.ShapeDtypeStruct((1, 1, N_HEADS), F32),
            jax.ShapeDtypeStruct(state4, F32),
            jax.ShapeDtypeStruct((1, CONV_W - 1, D_FF), F32),
        ],
        scratch_shapes=[pltpu.VMEM((t_len, D_PROJ), F32)],
        compiler_params=pltpu.CompilerParams(
            dimension_semantics=("arbitrary",), vmem_limit_bytes=VMEM_LIMIT),
        name="meta_state",
    )(x, ln_e_g, ln_e_b, w_in, b_in, bf_row, ga, gb, lb, w_out, b_out, ln_g, ln_b, w_up, b_up)


def _in_proj_stages(x_ref, g_ref, b_ref, w_ref, bias_ref, lb_ref, act_ref, xn_ref):
    assert D_GROUP % IN_PROJ_STAGE_COLS == 0
    xn = _layer_norm(x_ref[...].reshape(xn_ref.shape), g_ref[...], b_ref[...])
    xn_ref[...] = xn
    x16 = xn.astype(BF16)
    for lo in range(0, D_PROJ, IN_PROJ_STAGE_COLS):
        hi = min(lo + IN_PROJ_STAGE_COLS, D_PROJ)
        yield
        block = _dot(x16, w_ref[:, lo:hi]) + bias_ref[:, lo:hi]
        group, off = divmod(lo, D_GROUP)
        if lo >= GATE_COL:
            act_ref[:, lo:hi] = block
            continue
        for dst, val in _activate(group, block, lb_ref[:, off:off + hi - lo]).items():
            base = _act_cols(dst, 0).start + off
            act_ref[:, base:base + hi - lo] = val


def _prompt_kernel(x0_ref, xnext_ref, ge_ref, be_ref, win_ref, bin_ref, c0_ref, n0_ref, m0_ref, s0_ref,
                   bf_ref, ga_ref, gb_ref, lb_ref, wout_ref, bout_ref, g1_ref, b1_ref,
                   x1_ref, c_ref, n_ref, m_ref, s_ref, proj_scr, xn_scr, proj_alt, xn_alt,
                   *, t_len, n_chunks, n_seq_blk):
    chunk = pl.program_id(1)
    step = pl.program_id(0) * n_chunks + chunk

    @pl.when(step == 0)
    def _():
        first = _in_proj_stages(x0_ref, ge_ref, be_ref, win_ref, bin_ref, lb_ref, proj_scr, xn_scr)
        _run_interleaved([first], [1])

    @pl.when(chunk == 0)
    def _():
        _load_state(c0_ref, n0_ref, m0_ref, s0_ref, c_ref, n_ref, m_ref, s_ref, n_seq_blk)

    def tile(proj_cur, xn_cur, proj_next, xn_next):
        next_proj = _in_proj_stages(xnext_ref, ge_ref, be_ref, win_ref, bin_ref, lb_ref,
                                    proj_next, xn_next)
        outs_a, outs_b = _mixer_body(
            proj_cur, bf_ref, ga_ref, gb_ref, lb_ref, c_ref, n_ref, m_ref, s_ref,
            t_len=t_len, n_seq_blk=n_seq_blk, activated=True, side_stages=next_proj)
        mix = jnp.concatenate(
            [jnp.concatenate([o.astype(BF16) for o in outs_a[s * N_HEADS:(s + 1) * N_HEADS]
                              + outs_b[s * N_HEADS:(s + 1) * N_HEADS]], axis=1)
             for s in range(n_seq_blk)], axis=0)
        y = _dot(mix, wout_ref[...]) + bout_ref[...]
        x1 = _layer_norm(ALPHA * xn_cur[...] + y, g1_ref[...], b1_ref[...])
        x1_ref[...] = x1.reshape(x1_ref.shape)

    @pl.when(step % 2 == 0)
    def _():
        tile(proj_scr, xn_scr, proj_alt, xn_alt)

    @pl.when(step % 2 == 1)
    def _():
        tile(proj_alt, xn_alt, proj_scr, xn_scr)

    @pl.when(chunk == n_chunks - 1)
    def _():
        _finish_state(s_ref, n_seq_blk)


def _prompt_mixer(x, ln_e_g, ln_e_b, w_in, b_in, c0, n0, m0, s0, bf_row, ga, gb, lb,
                  w_out, b_out, ln_g, ln_b, *, n_chunks, t_len, n_seq_blk):
    n_seq = x.shape[0]
    nb = n_seq_blk
    n_tiles = (n_seq // nb) * n_chunks
    const = lambda b, c: (0, 0)
    init4 = lambda b, c: (0, 0, 0, 0)
    init3 = lambda b, c: (0, 0, 0)
    vec = pl.BlockSpec((1, D_MODEL), const)
    grp = pl.BlockSpec((1, D_GROUP), const)
    state4 = pl.BlockSpec((nb, N_HEADS, D_HEAD, D_HEAD), lambda b, c: (b, 0, 0, 0))

    def next_tile(b, c):
        nxt = jnp.minimum(b * n_chunks + c + 1, n_tiles - 1)
        return (nxt // n_chunks, nxt % n_chunks, 0)

    return pl.pallas_call(
        functools.partial(_prompt_kernel, t_len=t_len, n_chunks=n_chunks, n_seq_blk=nb),
        grid=(n_seq // nb, n_chunks),
        in_specs=[
            _resident((nb, t_len, D_MODEL), init3),
            pl.BlockSpec((nb, t_len, D_MODEL), next_tile),
            vec, vec,
            _resident((D_MODEL, D_PROJ), const),
            pl.BlockSpec((1, D_PROJ), const),
            pl.BlockSpec((1, N_HEADS, D_HEAD, D_HEAD), init4),
            pl.BlockSpec((1, N_HEADS, D_HEAD), init3),
            pl.BlockSpec((1, 1, N_HEADS), init3),
            pl.BlockSpec((1, N_HEADS, D_HEAD, D_HEAD), init4),
            pl.BlockSpec((1, LANES), const),
            grp, grp, grp,
            _resident((D_MODEL, D_MODEL), const),
            vec, vec, vec,
        ],
        out_specs=[
            pl.BlockSpec((nb, t_len, D_MODEL), lambda b, c: (b, c, 0)),
            state4,
            pl.BlockSpec((nb, N_HEADS, D_HEAD), lambda b, c: (b, 0, 0)),
            pl.BlockSpec((nb, 1, N_HEADS), lambda b, c: (b, 0, 0)),
            state4,
        ],
        out_shape=[
            jax.ShapeDtypeStruct(x.shape, F32),
            jax.ShapeDtypeStruct((n_seq, N_HEADS, D_HEAD, D_HEAD), F32),
            jax.ShapeDtypeStruct((n_seq, N_HEADS, D_HEAD), F32),
            jax.ShapeDtypeStruct((n_seq, 1, N_HEADS), F32),
            jax.ShapeDtypeStruct((n_seq, N_HEADS, D_HEAD, D_HEAD), F32),
        ],
        scratch_shapes=[pltpu.VMEM((nb * t_len, D_ACT), F32), pltpu.VMEM((nb * t_len, D_MODEL), F32),
                        pltpu.VMEM((nb * t_len, D_ACT), F32), pltpu.VMEM((nb * t_len, D_MODEL), F32)],
        compiler_params=pltpu.CompilerParams(
            dimension_semantics=("arbitrary", "arbitrary"), vmem_limit_bytes=VMEM_LIMIT),
        name="prompt_mixer",
    )(x, x, ln_e_g, ln_e_b, w_in, b_in, c0, n0, m0, s0, bf_row, ga, gb, lb, w_out, b_out, ln_g, ln_b)


def _ffn_kernel(x_ref, *refs, n_seq_blk, t_len):
    _ffn_tile(x_ref[...], *refs, n_seq_blk=n_seq_blk, t_len=t_len)


def _out_proj_ffn_kernel(x_ref, mix_ref, ge_ref, be_ref, wo_ref, bo_ref, g1_ref, b1_ref, *refs,
                         n_seq_blk, t_len):
    xn = _layer_norm(x_ref[...], ge_ref[...], be_ref[...])
    y = _dot(mix_ref[...], wo_ref[...]) + bo_ref[...]
    x1 = _layer_norm(ALPHA * xn + y, g1_ref[...], b1_ref[...])
    _ffn_tile(x1, *refs, n_seq_blk=n_seq_blk, t_len=t_len)


def _ffn_tile(x, cs_ref, wu_ref, bu_ref, wc_ref, bc_ref, wd_ref, bd_ref, g_ref, b_ref,
              y_ref, nc_ref, full_ref, *, n_seq_blk, t_len):
    hist = SUBLANES - (CONV_W - 1)

    @pl.when(pl.program_id(1) == 0)
    def _():
        full_ref[:, hist:SUBLANES, :] = cs_ref[...]

    up = _dot(x.astype(BF16), wu_ref[...]) + bu_ref[...]
    u = up[:, :D_FF].reshape(n_seq_blk, t_len, D_FF)
    gate = up[:, D_FF:].reshape(n_seq_blk, t_len, D_FF)
    full_ref[:, SUBLANES:SUBLANES + t_len, :] = u
    conv = bc_ref[...] + u * wc_ref[CONV_W - 1:CONV_W, :]
    for j in range(CONV_W - 1):
        conv = conv + full_ref[:, hist + j:hist + j + t_len, :] * wc_ref[j:j + 1, :]
    last = full_ref[:, hist + t_len:SUBLANES + t_len, :]
    nc_ref[...] = last
    full_ref[:, hist:SUBLANES, :] = last
    act = (conv * _sigmoid(conv) * gate).reshape(n_seq_blk * t_len, D_FF)
    ffn = _dot(act.astype(BF16), wd_ref[...]) + bd_ref[...]
    y_ref[...] = _layer_norm(ALPHA * x + ffn, g_ref[...], b_ref[...])


def _ffn(x, conv_state, w_up, b_up, w_conv, b_conv, w_down, b_down, ln_g, ln_b,
         *, n_seq, seq_len, n_seq_blk, t_len, shared_init, out_proj=None):
    n_t = seq_len // t_len
    rows = n_seq_blk * t_len
    const = lambda s, t: (0, 0)
    cs_map = (lambda s, t: (0, 0, 0)) if shared_init else (lambda s, t: (s, 0, 0))
    row = pl.BlockSpec((rows, D_MODEL), lambda s, t: (s * n_t + t, 0))
    vec = pl.BlockSpec((1, D_MODEL), const)
    body, lead_specs, lead_args = _ffn_kernel, [row], (x,)
    if out_proj is not None:
        body = _out_proj_ffn_kernel
        lead_specs = [row, row, vec, vec, _resident((D_MODEL, D_MODEL), const), vec, vec, vec]
        lead_args = (x,) + tuple(out_proj)
    return pl.pallas_call(
        functools.partial(body, n_seq_blk=n_seq_blk, t_len=t_len),
        grid=(n_seq // n_seq_blk, n_t),
        in_specs=lead_specs + [
            pl.BlockSpec((n_seq_blk, CONV_W - 1, D_FF), cs_map),
            _resident((D_MODEL, 2 * D_FF), const),
            pl.BlockSpec((1, 2 * D_FF), const),
            pl.BlockSpec((CONV_W, D_FF), const),
            pl.BlockSpec((1, D_FF), const),
            _resident((D_FF, D_MODEL), const),
            vec, vec, vec,
        ],
        out_specs=[row, pl.BlockSpec((n_seq_blk, CONV_W - 1, D_FF), lambda s, t: (s, 0, 0))],
        out_shape=[
            jax.ShapeDtypeStruct((n_seq * seq_len, D_MODEL), F32),
            jax.ShapeDtypeStruct((n_seq, CONV_W - 1, D_FF), F32),
        ],
        scratch_shapes=[pltpu.VMEM((n_seq_blk, SUBLANES + t_len, D_FF), F32)],
        compiler_params=pltpu.CompilerParams(
            dimension_semantics=("arbitrary", "arbitrary"), vmem_limit_bytes=VMEM_LIMIT),
        name=f"ffn_t{t_len}",
    )(*lead_args, conv_state, w_up, b_up, w_conv, b_conv, w_down, b_down, ln_g, ln_b)


def kernel(x_prompt, x_sample, state_mlstm_C, state_mlstm_n, state_mlstm_m, state_hgrn_S, state_ffn_conv, meta_tokens, ln_emb_g, ln_emb_b, w_in, b_in, b_fgate_a, g_norm_a, g_norm_b, hgrn_lb_logits, w_out, b_out, ln1_g, ln1_b, w_up, b_up, w_conv, b_conv, w_down, b_down, ln2_g, ln2_b):
    assert w_in.shape[0] == DEPTH == 1
    n_prompt, seq, _ = x_prompt.shape
    n_sample, dec_seq, _ = x_sample.shape
    row = lambda v: v.reshape(1, -1).astype(F32)

    gate0 = 4 * D_GROUP
    gate1 = gate0 + 2 * N_HEADS
    pad = D_PROJ - w_in.shape[2]
    w_in_p = _regroup_in_proj_weight(w_in[0].T, tm=256)
    b_in_p = jnp.concatenate(
        [b_in[0][:gate0], b_in[0][gate1:], b_in[0][gate0:gate1], jnp.zeros((pad,), b_in.dtype)]
    ).reshape(1, D_PROJ).astype(F32)
    bf_row = jnp.zeros((1, LANES), F32).at[0, N_HEADS:2 * N_HEADS].set(b_fgate_a[0].astype(F32))
    lb = jnp.cumsum(jax.nn.softmax(hgrn_lb_logits.astype(F32), axis=0), axis=0)[0].reshape(1, D_GROUP)
    ga, gb = row(g_norm_a[0]), row(g_norm_b[0])
    ln_e = (row(ln_emb_g), row(ln_emb_b))
    out_p = (w_out[0].astype(BF16), row(b_out[0]), row(ln1_g[0]), row(ln1_b[0]))
    ffn_p = (w_up[0].astype(BF16), row(b_up[0]), w_conv[0].astype(F32), row(b_conv[0]),
             w_down[0].astype(BF16), row(b_down[0]), row(ln2_g[0]), row(ln2_b[0]))

    c_m, n_m, m_m, s_m, conv_m = _meta_state(
        meta_tokens.astype(F32), *ln_e, w_in_p, b_in_p, bf_row, ga, gb, lb, *out_p, ffn_p[0], ffn_p[1])

    x1_p, c_p, n_p, m_p, s_p = _prompt_mixer(
        x_prompt.astype(F32), *ln_e, w_in_p, b_in_p, c_m, n_m, m_m, s_m, bf_row, ga, gb, lb, *out_p,
        n_chunks=seq // PROMPT_CHUNK, t_len=PROMPT_CHUNK, n_seq_blk=PROMPT_SEQS_PER_STEP)
    y_p, conv_p = _ffn(x1_p.reshape(n_prompt * seq, D_MODEL), conv_m, *ffn_p, n_seq=n_prompt,
                       seq_len=seq, n_seq_blk=1, t_len=512, shared_init=True)

    sample_state = (state_mlstm_C[0].astype(F32), state_mlstm_n[0].astype(F32),
                    state_mlstm_m[0].astype(F32).reshape(n_sample, 1, N_HEADS),
                    state_hgrn_S[0].astype(F32))
    xs_rows = x_sample.reshape(n_sample * dec_seq, D_MODEL).astype(F32)
    proj_s = _in_proj(xs_rows, *ln_e, w_in_p, b_in_p, tm=256)
    mix_s, c_s, n_s, m_s, s_s = _mixer(
        proj_s, *sample_state, bf_row, ga, gb, lb, n_seq=n_sample, n_chunks=1, t_len=dec_seq,
        n_seq_blk=8, shared_init=False)
    y_s, conv_s = _ffn(xs_rows, state_ffn_conv[0].astype(F32), *ffn_p, n_seq=n_sample,
                       seq_len=dec_seq, n_seq_blk=32, t_len=dec_seq, shared_init=False,
                       out_proj=(mix_s, *ln_e, *out_p))

    lead = lambda v: v[None]
    return (y_p.reshape(n_prompt, seq, D_MODEL), y_s.reshape(n_sample, dec_seq, D_MODEL),
            lead(c_p), lead(n_p), lead(m_p.reshape(n_prompt, N_HEADS)), lead(s_p), lead(conv_p),
            lead(c_s), lead(n_s), lead(m_s.reshape(n_sample, N_HEADS)), lead(s_s), lead(conv_s))
```

```python
import functools

import jax
import jax.numpy as jnp
from jax import lax
from jax.experimental import pallas as pl
from jax.experimental.pallas import tpu as pltpu

D_MODEL = 1024
N_META = 16
N_HEADS = 4
D_HEAD = 128
D_GROUP = N_HEADS * D_HEAD
D_FF = 2816
CONV_W = 3
DEPTH = 1
ALPHA = (2.0 * DEPTH) ** 0.25
LN_EPS = 1e-5
RMS_EPS = 1e-6
NEG_LOG2_E = -1.4426950408889634

LANES = 128
SUBLANES = 8
GATE_COL = 8 * D_GROUP
D_PROJ = GATE_COL + LANES
KB_GROUP, F_GROUP = 8, 9
D_ACT = D_PROJ + 2 * D_GROUP
IN_PROJ_STAGE_COLS = 256
SIDE_STAGES_PER_ROUND = 2
PROMPT_SEQS_PER_STEP = 2
PROMPT_CHUNK = 128
VMEM_LIMIT = 56 * 1024 * 1024

F32 = jnp.float32
BF16 = jnp.bfloat16
NT_DIMS = (((1,), (1,)), ((), ()))
TN_DIMS = (((0,), (0,)), ((), ()))


def _layer_norm(x, g, b):
    mu = jnp.mean(x, axis=-1, keepdims=True)
    xc = x - mu
    var = jnp.mean(xc * xc, axis=-1, keepdims=True)
    return xc * lax.rsqrt(var + LN_EPS) * g + b


def _exp_neg(x):
    return jnp.exp2(x * NEG_LOG2_E)


def _sigmoid(x):
    return 1.0 / (1.0 + _exp_neg(x))


def _resident(block_shape, index_map):
    return pl.BlockSpec(block_shape, index_map, pipeline_mode=pl.Buffered(1))


def _dot(a, b):
    return jnp.dot(a, b, preferred_element_type=F32)


def _dot_nt(a, b):
    return lax.dot_general(a, b, NT_DIMS, preferred_element_type=F32)


def _dot_tn(a, b):
    return lax.dot_general(a, b, TN_DIMS, preferred_element_type=F32)


def _regroup_kernel(wt_ref, o_ref):
    gate0 = 4 * D_GROUP
    gate1 = gate0 + 2 * N_HEADS
    for j in range(GATE_COL // LANES):
        src = j * LANES if j * LANES < gate0 else j * LANES + (gate1 - gate0)
        o_ref[:, j * LANES:(j + 1) * LANES] = wt_ref[src:src + LANES, :].T.astype(o_ref.dtype)
    gates = wt_ref[gate0:gate1, :].T.astype(o_ref.dtype)
    o_ref[:, GATE_COL:] = jnp.concatenate(
        [gates, jnp.zeros((gates.shape[0], LANES - gates.shape[1]), o_ref.dtype)], axis=1)


def _regroup_in_proj_weight(w_t, *, tm):
    cols, n = w_t.shape
    return pl.pallas_call(
        _regroup_kernel,
        grid=(n // tm,),
        in_specs=[pl.BlockSpec((cols, tm), lambda i: (0, i))],
        out_specs=pl.BlockSpec((tm, D_PROJ), lambda i: (i, 0)),
        out_shape=jax.ShapeDtypeStruct((n, D_PROJ), BF16),
        compiler_params=pltpu.CompilerParams(dimension_semantics=("arbitrary",)),
        name="regroup_w_in",
    )(w_t)


def _in_proj_kernel(x_ref, g_ref, b_ref, w_ref, bias_ref, o_ref):
    xn = _layer_norm(x_ref[...], g_ref[...], b_ref[...])
    o_ref[...] = _dot(xn.astype(BF16), w_ref[...]) + bias_ref[...]


def _in_proj(x, ln_g, ln_b, w, bias, *, tm):
    n = x.shape[0]
    const = lambda i: (0, 0)
    return pl.pallas_call(
        _in_proj_kernel,
        grid=(n // tm,),
        in_specs=[
            pl.BlockSpec((tm, D_MODEL), lambda i: (i, 0)),
            pl.BlockSpec((1, D_MODEL), const),
            pl.BlockSpec((1, D_MODEL), const),
            pl.BlockSpec((D_MODEL, D_PROJ), const),
            pl.BlockSpec((1, D_PROJ), const),
        ],
        out_specs=pl.BlockSpec((tm, D_PROJ), lambda i: (i, 0)),
        out_shape=jax.ShapeDtypeStruct((n, D_PROJ), F32),
        compiler_params=pltpu.CompilerParams(
            dimension_semantics=("arbitrary",), vmem_limit_bytes=VMEM_LIMIT),
        name="in_proj",
    )(x, ln_g, ln_b, w, bias)


def _block_rows(x, level, t_len, row_in_block):
    size = 2 << level
    if size > SUBLANES:
        pieces = [jnp.broadcast_to(x[j * size + row_in_block:j * size + row_in_block + 1, :],
                                   (size, x.shape[1])) for j in range(t_len // size)]
        return pieces[0] if len(pieces) == 1 else jnp.concatenate(pieces, axis=0)
    x3 = x.reshape(t_len // SUBLANES, SUBLANES, x.shape[1])
    sub = lax.broadcasted_iota(jnp.int32, x3.shape, 1)
    out = None
    for j in range(SUBLANES // size):
        row = jnp.broadcast_to(x3[:, j * size + row_in_block:j * size + row_in_block + 1, :], x3.shape)
        out = row if out is None else jnp.where(sub >= j * size, row, out)
    return out.reshape(x.shape)


def _interleave_halves(lower, upper, level, t_len):
    half = 1 << level
    if half >= SUBLANES:
        pieces = []
        for j in range(t_len // (2 * half)):
            pieces.append(lower[2 * half * j:2 * half * j + half])
            pieces.append(upper[2 * half * j + half:2 * half * (j + 1)])
        return jnp.concatenate(pieces, axis=0)
    rows = lax.broadcasted_iota(jnp.int32, lower.shape, 0)
    return jnp.where((rows & half) != 0, upper, lower)


def _run_interleaved(gens, stages_per_round, n_main=None):
    results = [None] * len(gens)
    live = [True] * len(gens)
    while any(live[:n_main]):
        for g, steps in enumerate(stages_per_round):
            for _ in range(steps):
                if live[g]:
                    try:
                        next(gens[g])
                    except StopIteration as stop:
                        results[g], live[g] = stop.value, False
    return results


def _cumsum_rows(tril16, x):
    hi = x.astype(BF16)
    rest = x - hi.astype(F32)
    mid = rest.astype(BF16)
    lo = (rest - mid.astype(F32)).astype(BF16)
    return _dot(tril16, hi) + _dot(tril16, mid) + _dot(tril16, lo)


def _activate(group, x, lb=None):
    if group == 1:
        return {1: x * (D_HEAD ** -0.5)}
    if group in (3, 7):
        return {group: _sigmoid(x)}
    if group == 4:
        return {4: x * _sigmoid(x)}
    if group == 5:
        f = lb + (1.0 - lb) * _sigmoid(x)
        return {5: jnp.log(f), KB_GROUP: (1.0 - lb) / (1.0 + jnp.exp(x)), F_GROUP: f}
    return {group: x}


def _mlstm_units(*, q, k, v, gate, z, zt, m0_row, c_old, n_old, g_norm, causal, lane, t_len):
    idx = range(len(q))
    seq = [i // N_HEADS for i in idx]
    b_lane = [N_HEADS + i % N_HEADS for i in idx]
    q16 = [q[i].astype(BF16) for i in idx]
    k16 = [k[i].astype(BF16) for i in idx]
    qk = [_dot_nt(q16[i], k16[i]) for i in idx]
    qc = [_dot(q16[i], c_old[i].astype(BF16)) for i in idx]
    bs_row = [zt[seq[i]][b_lane[i]:b_lane[i] + 1, :] - zt[seq[i]][i % N_HEADS:i % N_HEADS + 1, :]
              for i in idx]
    yield
    col = lambda per_seq, i: per_seq[seq[i]][:, b_lane[i]:b_lane[i] + 1]
    d = [jnp.where(causal, col(z, i) - bs_row[i], -jnp.inf) for i in idx]
    row_max = [jnp.max(d[i], axis=1, keepdims=True) for i in idx]
    last = slice(t_len - 1, t_len)
    m_t_seq, dec_seq, floor_seq, w_last_seq = [], [], [], []
    for s in range(len(z)):
        r = z[s] + m0_row[s]
        d_max = jnp.full(r.shape, -jnp.inf, F32)
        for h in range(N_HEADS):
            d_max = jnp.where(lane == N_HEADS + h, row_max[s * N_HEADS + h], d_max)
        m_t = jnp.maximum(r, d_max)
        i_gate = pltpu.roll(z[s], N_HEADS, axis=1)
        m_t_seq.append(m_t)
        dec_seq.append(jnp.exp(r - m_t))
        floor_seq.append(_exp_neg(m_t))
        w_last_seq.append(jnp.exp(z[s][last] - z[s] + i_gate - m_t[last]))
    m_t = [col(m_t_seq, i) for i in idx]
    dec = [col(dec_seq, i) for i in idx]
    w_last = [col(w_last_seq, i) for i in idx]
    sw = [jnp.exp(d[i] - m_t[i]) * qk[i] for i in idx]
    yield
    swv = [_dot(sw[i].astype(BF16), v[i].astype(BF16)) for i in idx]
    kv = [_dot_tn(k16[i], (w_last[i] * v[i]).astype(BF16)) for i in idx]
    yield
    c_new = [dec[i][last] * c_old[i] + kv[i] for i in idx]
    n_new = [dec[i][last] * n_old[i] + jnp.sum(w_last[i] * k[i], axis=0, keepdims=True) for i in idx]
    m_new = [m_t[i][last] for i in idx]
    den = [dec[i] * jnp.sum(q[i] * n_old[i], axis=1, keepdims=True)
           + jnp.sum(sw[i], axis=1, keepdims=True) for i in idx]
    hid = [(dec[i] * qc[i] + swv[i]) / jnp.maximum(jnp.abs(den[i]), col(floor_seq, i)) for i in idx]
    yield
    rms = [lax.rsqrt(jnp.mean(hid[i] * hid[i], axis=1, keepdims=True) + RMS_EPS) for i in idx]
    out = [gate[i] * (hid[i] * rms[i] * g_norm[i]) for i in idx]
    return out, c_new, n_new, m_new


def _hgrn_units(*, qb, log_f, f, kb, iv, gate, s_old_t, g_norm, tril, level_of, t_len):
    idx = range(len(qb))
    n_levels = t_len.bit_length() - 1
    a = [_cumsum_rows(tril, log_f[i]) for i in idx]
    yield
    iv16 = [iv[i].astype(BF16) for i in idx]
    diag = [_dot_nt(qb[i].astype(BF16), kb[i].astype(BF16)) for i in idx]
    scores = [jnp.where(level_of == -2, diag[i], 0.0) for i in idx]
    for level in range(n_levels):
        yield
        x16 = []
        for i in idx:
            base = _interleave_halves(kb[i], qb[i], level, t_len)
            if level == 0:
                x = base * _interleave_halves(jnp.ones_like(f[i]), f[i], 0, t_len)
            else:
                ref = _block_rows(a[i], level, t_len, (1 << level) - 1)
                x = base * _exp_neg(jnp.abs(a[i] - ref))
            x16.append(x.astype(BF16))
        part = [_dot_nt(x16[i], x16[i]) for i in idx]
        scores = [jnp.where(level_of == level, part[i], scores[i]) for i in idx]
    yield
    last = slice(t_len - 1, t_len)
    q_in = [(qb[i] * jnp.exp(a[i])).astype(BF16) for i in idx]
    k_out = [(kb[i] * jnp.exp(a[i][last] - a[i])).astype(BF16) for i in idx]
    inter = [_dot_nt(q_in[i], s_old_t[i].astype(BF16)) for i in idx]
    intra = [_dot(scores[i].astype(BF16), iv16[i]) for i in idx]
    kv = [_dot_tn(iv16[i], k_out[i]) for i in idx]
    yield
    s_new_t = [jnp.exp(a[i][last]) * s_old_t[i] + kv[i] for i in idx]
    o = [inter[i] + intra[i] for i in idx]
    rms = [lax.rsqrt(jnp.mean(o[i] * o[i], axis=1, keepdims=True) + RMS_EPS) for i in idx]
    out = [gate[i] * (o[i] * rms[i] * g_norm[i]) for i in idx]
    return out, s_new_t


def _load_state(c0_ref, n0_ref, m0_ref, s0_ref, c_ref, n_ref, m_ref, s_ref, n_seq_blk):
    shared = c0_ref.shape[0] == 1 and n_seq_blk > 1
    for s in range(n_seq_blk):
        src = 0 if shared else s
        c_ref[s] = c0_ref[src]
        n_ref[s] = n0_ref[src]
        m_ref[s] = m0_ref[src]
        for h in range(N_HEADS):
            s_ref[s, h] = s0_ref[src, h].T


def _finish_state(s_ref, n_seq_blk):
    for s in range(n_seq_blk):
        for h in range(N_HEADS):
            s_ref[s, h] = s_ref[s, h].T


def _act_cols(group, h):
    base = group * D_GROUP if group < KB_GROUP else D_PROJ + (group - KB_GROUP) * D_GROUP
    return slice(base + h * D_HEAD, base + (h + 1) * D_HEAD)


def _mixer_body(p_ref, bf_ref, ga_ref, gb_ref, lb_ref, c_ref, n_ref, m_ref, s_ref,
                *, t_len, n_seq_blk, activated, side_stages=None):
    rows = lax.broadcasted_iota(jnp.int32, (t_len, t_len), 0)
    cols = lax.broadcasted_iota(jnp.int32, (t_len, t_len), 1)
    causal = cols <= rows
    tril = causal.astype(BF16)
    level_of = jnp.where(rows > cols, 31 - lax.clz(rows ^ cols), jnp.where(rows == cols, -2, -1))
    lane = lax.broadcasted_iota(jnp.int32, (t_len, LANES), 1)
    is_f = (lane >= N_HEADS) & (lane < 2 * N_HEADS)
    hd = lambda j, h: slice(j * D_GROUP + h * D_HEAD, j * D_GROUP + (h + 1) * D_HEAD)

    units = [(s, h) for s in range(n_seq_blk) for h in range(N_HEADS)]
    rs = lambda s: slice(s * t_len, (s + 1) * t_len)

    z_seq, zt_seq, n_seq, m_seq = [], [], [], []
    for s in range(n_seq_blk):
        gates = p_ref[rs(s), GATE_COL:GATE_COL + LANES]
        log_f = jnp.where(is_f, jax.nn.log_sigmoid(gates + bf_ref[...]), 0.0)
        cum_f = _cumsum_rows(tril, log_f)
        z_seq.append(jnp.where(is_f, cum_f, jnp.where(lane < N_HEADS, gates, 0.0)))
        zt_seq.append(z_seq[s].T)
        n_seq.append(n_ref[s])
        m_seq.append(m_ref[s])
    computed = {}

    def act(group):
        if activated:
            return [p_ref[rs(s), _act_cols(group, h)] for s, h in units]
        source = 5 if group >= KB_GROUP else group
        for s, h in units:
            if (source, s, h) not in computed:
                computed[source, s, h] = _activate(source, p_ref[rs(s), hd(source, h)],
                                                   lb_ref[:, hd(0, h)])
        return [computed[source, s, h][group] for s, h in units]

    lane_row = lane[:1, :]
    m0_rows = []
    for s in range(n_seq_blk):
        m0_row = jnp.zeros((1, LANES), F32)
        for h in range(N_HEADS):
            m0_row = jnp.where(lane_row == N_HEADS + h, m_seq[s][:, h:h + 1], m0_row)
        m0_rows.append(m0_row)
    mlstm = _mlstm_units(
        q=act(0), k=act(1), v=act(2), gate=act(3), z=z_seq, zt=zt_seq, m0_row=m0_rows,
        c_old=[c_ref[s, h] for s, h in units], n_old=[n_seq[s][h:h + 1, :] for s, h in units],
        g_norm=[ga_ref[:, hd(0, h)] for s, h in units], causal=causal, lane=lane, t_len=t_len)
    hgrn = _hgrn_units(
        qb=act(4), log_f=act(5), f=act(F_GROUP), kb=act(KB_GROUP), iv=act(6), gate=act(7),
        s_old_t=[s_ref[s, h] for s, h in units],
        g_norm=[gb_ref[:, hd(0, h)] for s, h in units], tril=tril, level_of=level_of, t_len=t_len)
    gens, per_round = [mlstm, hgrn], [1, 2]
    if side_stages is not None:
        gens, per_round = gens + [side_stages], per_round + [SIDE_STAGES_PER_ROUND]
    results = _run_interleaved(gens, per_round, n_main=2)
    (outs, c_new, n_new, m_new), (outs_b, s_new_t) = results[:2]

    for i, (s, h) in enumerate(units):
        c_ref[s, h] = c_new[i]
        s_ref[s, h] = s_new_t[i]
    head_lane = lax.broadcasted_iota(jnp.int32, (1, N_HEADS), 1)
    for s in range(n_seq_blk):
        n_ref[s] = jnp.concatenate(n_new[s * N_HEADS:(s + 1) * N_HEADS], axis=0)
        m_row = m_seq[s]
        for h in range(N_HEADS):
            m_row = jnp.where(head_lane == h, m_new[s * N_HEADS + h], m_row)
        m_ref[s] = m_row
    return outs, outs_b


def _mixer_kernel(p_ref, c0_ref, n0_ref, m0_ref, s0_ref, bf_ref, ga_ref, gb_ref, lb_ref,
                  mix_ref, c_ref, n_ref, m_ref, s_ref, *, t_len, n_chunks, n_seq_blk):
    chunk = pl.program_id(1)

    @pl.when(chunk == 0)
    def _():
        _load_state(c0_ref, n0_ref, m0_ref, s0_ref, c_ref, n_ref, m_ref, s_ref, n_seq_blk)

    outs_a, outs_b = _mixer_body(p_ref, bf_ref, ga_ref, gb_ref, lb_ref, c_ref, n_ref, m_ref, s_ref,
                                 t_len=t_len, n_seq_blk=n_seq_blk, activated=False)
    for i in range(n_seq_blk * N_HEADS):
        s, h = divmod(i, N_HEADS)
        rows = slice(s * t_len, (s + 1) * t_len)
        mix_ref[rows, h * D_HEAD:(h + 1) * D_HEAD] = outs_a[i].astype(mix_ref.dtype)
        mix_ref[rows, D_GROUP + h * D_HEAD:D_GROUP + (h + 1) * D_HEAD] = outs_b[i].astype(mix_ref.dtype)

    @pl.when(chunk == n_chunks - 1)
    def _():
        _finish_state(s_ref, n_seq_blk)


def _mixer(proj, c0, n0, m0, s0, bf_row, ga, gb, lb, *, n_seq, n_chunks, t_len, n_seq_blk,
           shared_init):
    assert not shared_init or n_seq_blk == 1
    assert n_chunks == 1 or n_seq_blk == 1
    nb = n_seq_blk
    init = (lambda b, c: (0, 0, 0, 0)) if shared_init else (lambda b, c: (b, 0, 0, 0))
    init3 = (lambda b, c: (0, 0, 0)) if shared_init else (lambda b, c: (b, 0, 0))
    const = lambda b, c: (0, 0)
    state4 = pl.BlockSpec((nb, N_HEADS, D_HEAD, D_HEAD), lambda b, c: (b, 0, 0, 0))
    return pl.pallas_call(
        functools.partial(_mixer_kernel, t_len=t_len, n_chunks=n_chunks, n_seq_blk=nb),
        grid=(n_seq // nb, n_chunks),
        in_specs=[
            pl.BlockSpec((nb * t_len, D_PROJ), lambda b, c: (b * n_chunks + c, 0)),
            pl.BlockSpec((nb, N_HEADS, D_HEAD, D_HEAD), init),
            pl.BlockSpec((nb, N_HEADS, D_HEAD), init3),
            pl.BlockSpec((nb, 1, N_HEADS), init3),
            pl.BlockSpec((nb, N_HEADS, D_HEAD, D_HEAD), init),
            pl.BlockSpec((1, LANES), const),
            pl.BlockSpec((1, D_GROUP), const),
            pl.BlockSpec((1, D_GROUP), const),
            pl.BlockSpec((1, D_GROUP), const),
        ],
        out_specs=[
            pl.BlockSpec((nb * t_len, D_MODEL), lambda b, c: (b * n_chunks + c, 0)),
            state4,
            pl.BlockSpec((nb, N_HEADS, D_HEAD), lambda b, c: (b, 0, 0)),
            pl.BlockSpec((nb, 1, N_HEADS), lambda b, c: (b, 0, 0)),
            state4,
        ],
        out_shape=[
            jax.ShapeDtypeStruct((n_seq * n_chunks * t_len, D_MODEL), BF16),
            jax.ShapeDtypeStruct((n_seq, N_HEADS, D_HEAD, D_HEAD), F32),
            jax.ShapeDtypeStruct((n_seq, N_HEADS, D_HEAD), F32),
            jax.ShapeDtypeStruct((n_seq, 1, N_HEADS), F32),
            jax.ShapeDtypeStruct((n_seq, N_HEADS, D_HEAD, D_HEAD), F32),
        ],
        compiler_params=pltpu.CompilerParams(
            dimension_semantics=("arbitrary", "arbitrary"), vmem_limit_bytes=VMEM_LIMIT),
        name=f"mixer_t{t_len}",
    )(proj, c0, n0, m0, s0, bf_row, ga, gb, lb)


def _meta_kernel(x_ref, ge_ref, be_ref, win_ref, bin_ref, bf_ref, ga_ref, gb_ref, lb_ref,
                 wout_ref, bout_ref, g1_ref, b1_ref, wu_ref, bu_ref,
                 c_ref, n_ref, m_ref, s_ref, conv_ref, proj_scr):
    t_len = x_ref.shape[0]
    xn = _layer_norm(x_ref[...], ge_ref[...], be_ref[...])
    proj_scr[...] = _dot(xn.astype(BF16), win_ref[...]) + bin_ref[...]
    for ref in (c_ref, n_ref, m_ref, s_ref):
        ref[...] = jnp.zeros(ref.shape, ref.dtype)
    outs_a, outs_b = _mixer_body(proj_scr, bf_ref, ga_ref, gb_ref, lb_ref, c_ref, n_ref, m_ref, s_ref,
                                 t_len=t_len, n_seq_blk=1, activated=False)
    _finish_state(s_ref, 1)
    mix = jnp.concatenate([o.astype(BF16) for o in outs_a + outs_b], axis=1)
    y = _dot(mix, wout_ref[...]) + bout_ref[...]
    x1 = _layer_norm(ALPHA * xn + y, g1_ref[...], b1_ref[...])
    u = _dot(x1.astype(BF16), wu_ref[...]) + bu_ref[...]
    conv_ref[0] = u[t_len - (CONV_W - 1):, :]


def _meta_state(x, ln_e_g, ln_e_b, w_in, b_in, bf_row, ga, gb, lb, w_out, b_out, ln_g, ln_b,
                w_up, b_up):
    t_len = x.shape[0]
    full = lambda shape: _resident(shape, lambda i: (0,) * len(shape))
    vec, grp = full((1, D_MODEL)), full((1, D_GROUP))
    state4 = (1, N_HEADS, D_HEAD, D_HEAD)
    return pl.pallas_call(
        _meta_kernel,
        grid=(1,),
        in_specs=[full((t_len, D_MODEL)), vec, vec, full((D_MODEL, D_PROJ)), full((1, D_PROJ)),
                  full((1, LANES)), grp, grp, grp, full((D_MODEL, D_MODEL)), vec, vec, vec,
                  full((D_MODEL, D_FF)), full((1, D_FF))],
        out_specs=[pl.BlockSpec(shape, lambda i, rank=len(shape): (0,) * rank)
                   for shape in (state4, (1, N_HEADS, D_HEAD), (1, 1, N_HEADS), state4,
                                 (1, CONV_W - 1, D_FF))],
        out_shape=[
            jax.ShapeDtypeStruct(state4, F32),
            jax.ShapeDtypeStruct((1, N_HEADS, D_HEAD), F32),
            jax.ShapeDtypeStruct((1, 1, N_HEADS), F32),
            jax.ShapeDtypeStruct(state4, F32),
            jax.ShapeDtypeStruct((1, CONV_W - 1, D_FF), F32),
        ],
        scratch_shapes=[pltpu.VMEM((t_len, D_PROJ), F32)],
        compiler_params=pltpu.CompilerParams(
            dimension_semantics=("arbitrary",), vmem_limit_bytes=VMEM_LIMIT),
        name="meta_state",
    )(x, ln_e_g, ln_e_b, w_in, b_in, bf_row, ga, gb, lb, w_out, b_out, ln_g, ln_b, w_up, b_up)


def _in_proj_stages(x_ref, g_ref, b_ref, w_ref, bias_ref, lb_ref, act_ref, xn_ref):
    assert D_GROUP % IN_PROJ_STAGE_COLS == 0
    xn = _layer_norm(x_ref[...].reshape(xn_ref.shape), g_ref[...], b_ref[...])
    xn_ref[...] = xn
    x16 = xn.astype(BF16)
    for lo in range(0, D_PROJ, IN_PROJ_STAGE_COLS):
        hi = min(lo + IN_PROJ_STAGE_COLS, D_PROJ)
        yield
        block = _dot(x16, w_ref[:, lo:hi]) + bias_ref[:, lo:hi]
        group, off = divmod(lo, D_GROUP)
        if lo >= GATE_COL:
            act_ref[:, lo:hi] = block
            continue
        for dst, val in _activate(group, block, lb_ref[:, off:off + hi - lo]).items():
            base = _act_cols(dst, 0).start + off
            act_ref[:, base:base + hi - lo] = val


def _prompt_kernel(x0_ref, xnext_ref, ge_ref, be_ref, win_ref, bin_ref, c0_ref, n0_ref, m0_ref, s0_ref,
                   bf_ref, ga_ref, gb_ref, lb_ref, wout_ref, bout_ref, g1_ref, b1_ref,
                   x1_ref, c_ref, n_ref, m_ref, s_ref, proj_scr, xn_scr, proj_alt, xn_alt,
                   *, t_len, n_chunks, n_seq_blk):
    chunk = pl.program_id(1)
    step = pl.program_id(0) * n_chunks + chunk

    @pl.when(step == 0)
    def _():
        first = _in_proj_stages(x0_ref, ge_ref, be_ref, win_ref, bin_ref, lb_ref, proj_scr, xn_scr)
        _run_interleaved([first], [1])

    @pl.when(chunk == 0)
    def _():
        _load_state(c0_ref, n0_ref, m0_ref, s0_ref, c_ref, n_ref, m_ref, s_ref, n_seq_blk)

    def tile(proj_cur, xn_cur, proj_next, xn_next):
        next_proj = _in_proj_stages(xnext_ref, ge_ref, be_ref, win_ref, bin_ref, lb_ref,
                                    proj_next, xn_next)
        outs_a, outs_b = [], []
        for s in range(n_seq_blk):
            one = slice(s, s + 1)
            a_s, b_s = _mixer_body(
                proj_cur.at[s * t_len:(s + 1) * t_len], bf_ref, ga_ref, gb_ref, lb_ref,
                c_ref.at[one], n_ref.at[one], m_ref.at[one], s_ref.at[one],
                t_len=t_len, n_seq_blk=1, activated=True, side_stages=next_proj)
            outs_a, outs_b = outs_a + a_s, outs_b + b_s
        _run_interleaved([next_proj], [1])
        mix = jnp.concatenate(
            [jnp.concatenate([o.astype(BF16) for o in outs_a[s * N_HEADS:(s + 1) * N_HEADS]
                              + outs_b[s * N_HEADS:(s + 1) * N_HEADS]], axis=1)
             for s in range(n_seq_blk)], axis=0)
        y = _dot(mix, wout_ref[...]) + bout_ref[...]
        x1 = _layer_norm(ALPHA * xn_cur[...] + y, g1_ref[...], b1_ref[...])
        x1_ref[...] = x1.reshape(x1_ref.shape)

    @pl.when(step % 2 == 0)
    def _():
        tile(proj_scr, xn_scr, proj_alt, xn_alt)

    @pl.when(step % 2 == 1)
    def _():
        tile(proj_alt, xn_alt, proj_scr, xn_scr)

    @pl.when(chunk == n_chunks - 1)
    def _():
        _finish_state(s_ref, n_seq_blk)


def _prompt_mixer(x, ln_e_g, ln_e_b, w_in, b_in, c0, n0, m0, s0, bf_row, ga, gb, lb,
                  w_out, b_out, ln_g, ln_b, *, n_chunks, t_len, n_seq_blk):
    n_seq = x.shape[0]
    nb = n_seq_blk
    n_tiles = (n_seq // nb) * n_chunks
    const = lambda b, c: (0, 0)
    init4 = lambda b, c: (0, 0, 0, 0)
    init3 = lambda b, c: (0, 0, 0)
    vec = pl.BlockSpec((1, D_MODEL), const)
    grp = pl.BlockSpec((1, D_GROUP), const)
    state4 = pl.BlockSpec((nb, N_HEADS, D_HEAD, D_HEAD), lambda b, c: (b, 0, 0, 0))

    def next_tile(b, c):
        nxt = jnp.minimum(b * n_chunks + c + 1, n_tiles - 1)
        return (nxt // n_chunks, nxt % n_chunks, 0)

    return pl.pallas_call(
        functools.partial(_prompt_kernel, t_len=t_len, n_chunks=n_chunks, n_seq_blk=nb),
        grid=(n_seq // nb, n_chunks),
        in_specs=[
            _resident((nb, t_len, D_MODEL), init3),
            pl.BlockSpec((nb, t_len, D_MODEL), next_tile),
            vec, vec,
            _resident((D_MODEL, D_PROJ), const),
            pl.BlockSpec((1, D_PROJ), const),
            pl.BlockSpec((1, N_HEADS, D_HEAD, D_HEAD), init4),
            pl.BlockSpec((1, N_HEADS, D_HEAD), init3),
            pl.BlockSpec((1, 1, N_HEADS), init3),
            pl.BlockSpec((1, N_HEADS, D_HEAD, D_HEAD), init4),
            pl.BlockSpec((1, LANES), const),
            grp, grp, grp,
            _resident((D_MODEL, D_MODEL), const),
            vec, vec, vec,
        ],
        out_specs=[
            pl.BlockSpec((nb, t_len, D_MODEL), lambda b, c: (b, c, 0)),
            state4,
            pl.BlockSpec((nb, N_HEADS, D_HEAD), lambda b, c: (b, 0, 0)),
            pl.BlockSpec((nb, 1, N_HEADS), lambda b, c: (b, 0, 0)),
            state4,
        ],
        out_shape=[
            jax.ShapeDtypeStruct(x.shape, F32),
            jax.ShapeDtypeStruct((n_seq, N_HEADS, D_HEAD, D_HEAD), F32),
            jax.ShapeDtypeStruct((n_seq, N_HEADS, D_HEAD), F32),
            jax.ShapeDtypeStruct((n_seq, 1, N_HEADS), F32),
            jax.ShapeDtypeStruct((n_seq, N_HEADS, D_HEAD, D_HEAD), F32),
        ],
        scratch_shapes=[pltpu.VMEM((nb * t_len, D_ACT), F32), pltpu.VMEM((nb * t_len, D_MODEL), F32),
                        pltpu.VMEM((nb * t_len, D_ACT), F32), pltpu.VMEM((nb * t_len, D_MODEL), F32)],
        compiler_params=pltpu.CompilerParams(
            dimension_semantics=("arbitrary", "arbitrary"), vmem_limit_bytes=VMEM_LIMIT),
        name="prompt_mixer",
    )(x, x, ln_e_g, ln_e_b, w_in, b_in, c0, n0, m0, s0, bf_row, ga, gb, lb, w_out, b_out, ln_g, ln_b)


def _ffn_kernel(x_ref, *refs, n_seq_blk, t_len):
    _ffn_tile(x_ref[...], *refs, n_seq_blk=n_seq_blk, t_len=t_len)


def _out_proj_ffn_kernel(x_ref, mix_ref, ge_ref, be_ref, wo_ref, bo_ref, g1_ref, b1_ref, *refs,
                         n_seq_blk, t_len):
    xn = _layer_norm(x_ref[...], ge_ref[...], be_ref[...])
    y = _dot(mix_ref[...], wo_ref[...]) + bo_ref[...]
    x1 = _layer_norm(ALPHA * xn + y, g1_ref[...], b1_ref[...])
    _ffn_tile(x1, *refs, n_seq_blk=n_seq_blk, t_len=t_len)


def _ffn_tile(x, cs_ref, wu_ref, bu_ref, wc_ref, bc_ref, wd_ref, bd_ref, g_ref, b_ref,
              y_ref, nc_ref, full_ref, *, n_seq_blk, t_len):
    hist = SUBLANES - (CONV_W - 1)

    @pl.when(pl.program_id(1) == 0)
    def _():
        full_ref[:, hist:SUBLANES, :] = cs_ref[...]

    up = _dot(x.astype(BF16), wu_ref[...]) + bu_ref[...]
    u = up[:, :D_FF].reshape(n_seq_blk, t_len, D_FF)
    gate = up[:, D_FF:].reshape(n_seq_blk, t_len, D_FF)
    full_ref[:, SUBLANES:SUBLANES + t_len, :] = u
    conv = bc_ref[...] + u * wc_ref[CONV_W - 1:CONV_W, :]
    for j in range(CONV_W - 1):
        conv = conv + full_ref[:, hist + j:hist + j + t_len, :] * wc_ref[j:j + 1, :]
    last = full_ref[:, hist + t_len:SUBLANES + t_len, :]
    nc_ref[...] = last
    full_ref[:, hist:SUBLANES, :] = last
    act = (conv * _sigmoid(conv) * gate).reshape(n_seq_blk * t_len, D_FF)
    ffn = _dot(act.astype(BF16), wd_ref[...]) + bd_ref[...]
    y_ref[...] = _layer_norm(ALPHA * x + ffn, g_ref[...], b_ref[...])


def _ffn(x, conv_state, w_up, b_up, w_conv, b_conv, w_down, b_down, ln_g, ln_b,
         *, n_seq, seq_len, n_seq_blk, t_len, shared_init, out_proj=None):
    n_t = seq_len // t_len
    rows = n_seq_blk * t_len
    const = lambda s, t: (0, 0)
    cs_map = (lambda s, t: (0, 0, 0)) if shared_init else (lambda s, t: (s, 0, 0))
    row = pl.BlockSpec((rows, D_MODEL), lambda s, t: (s * n_t + t, 0))
    vec = pl.BlockSpec((1, D_MODEL), const)
    body, lead_specs, lead_args = _ffn_kernel, [row], (x,)
    if out_proj is not None:
        body = _out_proj_ffn_kernel
        lead_specs = [row, row, vec, vec, _resident((D_MODEL, D_MODEL), const), vec, vec, vec]
        lead_args = (x,) + tuple(out_proj)
    return pl.pallas_call(
        functools.partial(body, n_seq_blk=n_seq_blk, t_len=t_len),
        grid=(n_seq // n_seq_blk, n_t),
        in_specs=lead_specs + [
            pl.BlockSpec((n_seq_blk, CONV_W - 1, D_FF), cs_map),
            _resident((D_MODEL, 2 * D_FF), const),
            pl.BlockSpec((1, 2 * D_FF), const),
            pl.BlockSpec((CONV_W, D_FF), const),
            pl.BlockSpec((1, D_FF), const),
            _resident((D_FF, D_MODEL), const),
            vec, vec, vec,
        ],
        out_specs=[row, pl.BlockSpec((n_seq_blk, CONV_W - 1, D_FF), lambda s, t: (s, 0, 0))],
        out_shape=[
            jax.ShapeDtypeStruct((n_seq * seq_len, D_MODEL), F32),
            jax.ShapeDtypeStruct((n_seq, CONV_W - 1, D_FF), F32),
        ],
        scratch_shapes=[pltpu.VMEM((n_seq_blk, SUBLANES + t_len, D_FF), F32)],
        compiler_params=pltpu.CompilerParams(
            dimension_semantics=("arbitrary", "arbitrary"), vmem_limit_bytes=VMEM_LIMIT),
        name=f"ffn_t{t_len}",
    )(*lead_args, conv_state, w_up, b_up, w_conv, b_conv, w_down, b_down, ln_g, ln_b)


def kernel(x_prompt, x_sample, state_mlstm_C, state_mlstm_n, state_mlstm_m, state_hgrn_S, state_ffn_conv, meta_tokens, ln_emb_g, ln_emb_b, w_in, b_in, b_fgate_a, g_norm_a, g_norm_b, hgrn_lb_logits, w_out, b_out, ln1_g, ln1_b, w_up, b_up, w_conv, b_conv, w_down, b_down, ln2_g, ln2_b):
    assert w_in.shape[0] == DEPTH == 1
    n_prompt, seq, _ = x_prompt.shape
    n_sample, dec_seq, _ = x_sample.shape
    row = lambda v: v.reshape(1, -1).astype(F32)

    gate0 = 4 * D_GROUP
    gate1 = gate0 + 2 * N_HEADS
    pad = D_PROJ - w_in.shape[2]
    w_in_p = _regroup_in_proj_weight(w_in[0].T, tm=256)
    b_in_p = jnp.concatenate(
        [b_in[0][:gate0], b_in[0][gate1:], b_in[0][gate0:gate1], jnp.zeros((pad,), b_in.dtype)]
    ).reshape(1, D_PROJ).astype(F32)
    bf_row = jnp.zeros((1, LANES), F32).at[0, N_HEADS:2 * N_HEADS].set(b_fgate_a[0].astype(F32))
    lb = jnp.cumsum(jax.nn.softmax(hgrn_lb_logits.astype(F32), axis=0), axis=0)[0].reshape(1, D_GROUP)
    ga, gb = row(g_norm_a[0]), row(g_norm_b[0])
    ln_e = (row(ln_emb_g), row(ln_emb_b))
    out_p = (w_out[0].astype(BF16), row(b_out[0]), row(ln1_g[0]), row(ln1_b[0]))
    ffn_p = (w_up[0].astype(BF16), row(b_up[0]), w_conv[0].astype(F32), row(b_conv[0]),
             w_down[0].astype(BF16), row(b_down[0]), row(ln2_g[0]), row(ln2_b[0]))

    c_m, n_m, m_m, s_m, conv_m = _meta_state(
        meta_tokens.astype(F32), *ln_e, w_in_p, b_in_p, bf_row, ga, gb, lb, *out_p, ffn_p[0], ffn_p[1])

    x1_p, c_p, n_p, m_p, s_p = _prompt_mixer(
        x_prompt.astype(F32), *ln_e, w_in_p, b_in_p, c_m, n_m, m_m, s_m, bf_row, ga, gb, lb, *out_p,
        n_chunks=seq // PROMPT_CHUNK, t_len=PROMPT_CHUNK, n_seq_blk=PROMPT_SEQS_PER_STEP)
    y_p, conv_p = _ffn(x1_p.reshape(n_prompt * seq, D_MODEL), conv_m, *ffn_p, n_seq=n_prompt,
                       seq_len=seq, n_seq_blk=1, t_len=512, shared_init=True)

    sample_state = (state_mlstm_C[0].astype(F32), state_mlstm_n[0].astype(F32),
                    state_mlstm_m[0].astype(F32).reshape(n_sample, 1, N_HEADS),
                    state_hgrn_S[0].astype(F32))
    xs_rows = x_sample.reshape(n_sample * dec_seq, D_MODEL).astype(F32)
    proj_s = _in_proj(xs_rows, *ln_e, w_in_p, b_in_p, tm=256)
    mix_s, c_s, n_s, m_s, s_s = _mixer(
        proj_s, *sample_state, bf_row, ga, gb, lb, n_seq=n_sample, n_chunks=1, t_len=dec_seq,
        n_seq_blk=8, shared_init=False)
    y_s, conv_s = _ffn(xs_rows, state_ffn_conv[0].astype(F32), *ffn_p, n_seq=n_sample,
                       seq_len=dec_seq, n_seq_blk=32, t_len=dec_seq, shared_init=False,
                       out_proj=(mix_s, *ln_e, *out_p))

    lead = lambda v: v[None]
    return (y_p.reshape(n_prompt, seq, D_MODEL), y_s.reshape(n_sample, dec_seq, D_MODEL),
            lead(c_p), lead(n_p), lead(m_p.reshape(n_prompt, N_HEADS)), lead(s_p), lead(conv_p),
            lead(c_s), lead(n_s), lead(m_s.reshape(n_sample, N_HEADS)), lead(s_s), lead(conv_s))
```

```python
import functools

import jax
import jax.numpy as jnp
from jax import lax
from jax.experimental import pallas as pl
from jax.experimental.pallas import tpu as pltpu

D_MODEL = 1024
N_META = 16
N_HEADS = 4
D_HEAD = 128
D_GROUP = N_HEADS * D_HEAD
D_FF = 2816
CONV_W = 3
DEPTH = 1
ALPHA = (2.0 * DEPTH) ** 0.25
LN_EPS = 1e-5
RMS_EPS = 1e-6
NEG_LOG2_E = -1.4426950408889634

LANES = 128
SUBLANES = 8
GATE_COL = 8 * D_GROUP
D_PROJ = GATE_COL + LANES
KB_GROUP, F_GROUP = 8, 9
D_ACT = D_PROJ + 2 * D_GROUP
IN_PROJ_STAGE_COLS = 256
SIDE_STAGES_PER_ROUND = 3
PROMPT_SEQS_PER_STEP = 2
PROMPT_CHUNK = 128
VMEM_LIMIT = 56 * 1024 * 1024

F32 = jnp.float32
BF16 = jnp.bfloat16
NT_DIMS = (((1,), (1,)), ((), ()))
TN_DIMS = (((0,), (0,)), ((), ()))


def _layer_norm(x, g, b):
    mu = jnp.mean(x, axis=-1, keepdims=True)
    xc = x - mu
    var = jnp.mean(xc * xc, axis=-1, keepdims=True)
    return xc * lax.rsqrt(var + LN_EPS) * g + b


def _exp_neg(x):
    return jnp.exp2(x * NEG_LOG2_E)


def _sigmoid(x):
    return 1.0 / (1.0 + _exp_neg(x))


def _resident(block_shape, index_map):
    return pl.BlockSpec(block_shape, index_map, pipeline_mode=pl.Buffered(1))


def _dot(a, b):
    return jnp.dot(a, b, preferred_element_type=F32)


def _dot_nt(a, b):
    return lax.dot_general(a, b, NT_DIMS, preferred_element_type=F32)


def _dot_tn(a, b):
    return lax.dot_general(a, b, TN_DIMS, preferred_element_type=F32)


def _regroup_kernel(wt_ref, o_ref):
    gate0 = 4 * D_GROUP
    gate1 = gate0 + 2 * N_HEADS
    for j in range(GATE_COL // LANES):
        src = j * LANES if j * LANES < gate0 else j * LANES + (gate1 - gate0)
        o_ref[:, j * LANES:(j + 1) * LANES] = wt_ref[src:src + LANES, :].T.astype(o_ref.dtype)
    gates = wt_ref[gate0:gate1, :].T.astype(o_ref.dtype)
    o_ref[:, GATE_COL:] = jnp.concatenate(
        [gates, jnp.zeros((gates.shape[0], LANES - gates.shape[1]), o_ref.dtype)], axis=1)


def _regroup_in_proj_weight(w_t, *, tm):
    cols, n = w_t.shape
    return pl.pallas_call(
        _regroup_kernel,
        grid=(n // tm,),
        in_specs=[pl.BlockSpec((cols, tm), lambda i: (0, i))],
        out_specs=pl.BlockSpec((tm, D_PROJ), lambda i: (i, 0)),
        out_shape=jax.ShapeDtypeStruct((n, D_PROJ), BF16),
        compiler_params=pltpu.CompilerParams(dimension_semantics=("arbitrary",)),
        name="regroup_w_in",
    )(w_t)


def _in_proj_kernel(x_ref, g_ref, b_ref, w_ref, bias_ref, o_ref):
    xn = _layer_norm(x_ref[...], g_ref[...], b_ref[...])
    o_ref[...] = _dot(xn.astype(BF16), w_ref[...]) + bias_ref[...]


def _in_proj(x, ln_g, ln_b, w, bias, *, tm):
    n = x.shape[0]
    const = lambda i: (0, 0)
    return pl.pallas_call(
        _in_proj_kernel,
        grid=(n // tm,),
        in_specs=[
            pl.BlockSpec((tm, D_MODEL), lambda i: (i, 0)),
            pl.BlockSpec((1, D_MODEL), const),
            pl.BlockSpec((1, D_MODEL), const),
            pl.BlockSpec((D_MODEL, D_PROJ), const),
            pl.BlockSpec((1, D_PROJ), const),
        ],
        out_specs=pl.BlockSpec((tm, D_PROJ), lambda i: (i, 0)),
        out_shape=jax.ShapeDtypeStruct((n, D_PROJ), F32),
        compiler_params=pltpu.CompilerParams(
            dimension_semantics=("arbitrary",), vmem_limit_bytes=VMEM_LIMIT),
        name="in_proj",
    )(x, ln_g, ln_b, w, bias)


def _block_rows(x, level, t_len, row_in_block):
    size = 2 << level
    if size > SUBLANES:
        pieces = [jnp.broadcast_to(x[j * size + row_in_block:j * size + row_in_block + 1, :],
                                   (size, x.shape[1])) for j in range(t_len // size)]
        return pieces[0] if len(pieces) == 1 else jnp.concatenate(pieces, axis=0)
    x3 = x.reshape(t_len // SUBLANES, SUBLANES, x.shape[1])
    sub = lax.broadcasted_iota(jnp.int32, x3.shape, 1)
    out = None
    for j in range(SUBLANES // size):
        row = jnp.broadcast_to(x3[:, j * size + row_in_block:j * size + row_in_block + 1, :], x3.shape)
        out = row if out is None else jnp.where(sub >= j * size, row, out)
    return out.reshape(x.shape)


def _interleave_halves(lower, upper, level, t_len):
    half = 1 << level
    if half >= SUBLANES:
        pieces = []
        for j in range(t_len // (2 * half)):
            pieces.append(lower[2 * half * j:2 * half * j + half])
            pieces.append(upper[2 * half * j + half:2 * half * (j + 1)])
        return jnp.concatenate(pieces, axis=0)
    rows = lax.broadcasted_iota(jnp.int32, lower.shape, 0)
    return jnp.where((rows & half) != 0, upper, lower)


def _run_interleaved(gens, stages_per_round):
    results = [None] * len(gens)
    live = [True] * len(gens)
    while any(live):
        for g, steps in enumerate(stages_per_round):
            for _ in range(steps):
                if live[g]:
                    try:
                        next(gens[g])
                    except StopIteration as stop:
                        results[g], live[g] = stop.value, False
    return results


def _cumsum_rows(tril16, x):
    hi = x.astype(BF16)
    rest = x - hi.astype(F32)
    mid = rest.astype(BF16)
    lo = (rest - mid.astype(F32)).astype(BF16)
    return _dot(tril16, hi) + _dot(tril16, mid) + _dot(tril16, lo)


def _activate(group, x, lb=None):
    if group == 1:
        return {1: x * (D_HEAD ** -0.5)}
    if group in (3, 7):
        return {group: _sigmoid(x)}
    if group == 4:
        return {4: x * _sigmoid(x)}
    if group == 5:
        f = lb + (1.0 - lb) * _sigmoid(x)
        return {5: jnp.log(f), KB_GROUP: (1.0 - lb) / (1.0 + jnp.exp(x)), F_GROUP: f}
    return {group: x}


def _mlstm_units(*, q, k, v, gate, z, zt, m0_row, c_old, n_old, g_norm, causal, lane, t_len):
    idx = range(len(q))
    seq = [i // N_HEADS for i in idx]
    b_lane = [N_HEADS + i % N_HEADS for i in idx]
    q16 = [q[i].astype(BF16) for i in idx]
    k16 = [k[i].astype(BF16) for i in idx]
    qk = [_dot_nt(q16[i], k16[i]) for i in idx]
    qc = [_dot(q16[i], c_old[i].astype(BF16)) for i in idx]
    bs_row = [zt[seq[i]][b_lane[i]:b_lane[i] + 1, :] - zt[seq[i]][i % N_HEADS:i % N_HEADS + 1, :]
              for i in idx]
    yield
    col = lambda per_seq, i: per_seq[seq[i]][:, b_lane[i]:b_lane[i] + 1]
    d = [jnp.where(causal, col(z, i) - bs_row[i], -jnp.inf) for i in idx]
    row_max = [jnp.max(d[i], axis=1, keepdims=True) for i in idx]
    last = slice(t_len - 1, t_len)
    m_t_seq, dec_seq, floor_seq, w_last_seq = [], [], [], []
    for s in range(len(z)):
        r = z[s] + m0_row[s]
        d_max = jnp.full(r.shape, -jnp.inf, F32)
        for h in range(N_HEADS):
            d_max = jnp.where(lane == N_HEADS + h, row_max[s * N_HEADS + h], d_max)
        m_t = jnp.maximum(r, d_max)
        i_gate = pltpu.roll(z[s], N_HEADS, axis=1)
        m_t_seq.append(m_t)
        dec_seq.append(jnp.exp(r - m_t))
        floor_seq.append(_exp_neg(m_t))
        w_last_seq.append(jnp.exp(z[s][last] - z[s] + i_gate - m_t[last]))
    m_t = [col(m_t_seq, i) for i in idx]
    dec = [col(dec_seq, i) for i in idx]
    w_last = [col(w_last_seq, i) for i in idx]
    sw = [jnp.exp(d[i] - m_t[i]) * qk[i] for i in idx]
    yield
    swv = [_dot(sw[i].astype(BF16), v[i].astype(BF16)) for i in idx]
    kv = [_dot_tn(k16[i], (w_last[i] * v[i]).astype(BF16)) for i in idx]
    yield
    c_new = [dec[i][last] * c_old[i] + kv[i] for i in idx]
    n_new = [dec[i][last] * n_old[i] + jnp.sum(w_last[i] * k[i], axis=0, keepdims=True) for i in idx]
    m_new = [m_t[i][last] for i in idx]
    den = [dec[i] * jnp.sum(q[i] * n_old[i], axis=1, keepdims=True)
           + jnp.sum(sw[i], axis=1, keepdims=True) for i in idx]
    hid = [(dec[i] * qc[i] + swv[i]) / jnp.maximum(jnp.abs(den[i]), col(floor_seq, i)) for i in idx]
    yield
    rms = [lax.rsqrt(jnp.mean(hid[i] * hid[i], axis=1, keepdims=True) + RMS_EPS) for i in idx]
    out = [gate[i] * (hid[i] * rms[i] * g_norm[i]) for i in idx]
    return out, c_new, n_new, m_new


def _hgrn_units(*, qb, log_f, f, kb, iv, gate, s_old, transposed, g_norm, tril, level_of, t_len):
    idx = range(len(qb))
    n_levels = t_len.bit_length() - 1
    a = [_cumsum_rows(tril, log_f[i]) for i in idx]
    yield
    iv16 = [iv[i].astype(BF16) for i in idx]
    diag = [_dot_nt(qb[i].astype(BF16), kb[i].astype(BF16)) for i in idx]
    scores = [jnp.where(level_of == -2, diag[i], 0.0) for i in idx]
    for level in range(n_levels):
        yield
        x16 = []
        for i in idx:
            base = _interleave_halves(kb[i], qb[i], level, t_len)
            if level == 0:
                x = base * _interleave_halves(jnp.ones_like(f[i]), f[i], 0, t_len)
            else:
                ref = _block_rows(a[i], level, t_len, (1 << level) - 1)
                x = base * _exp_neg(jnp.abs(a[i] - ref))
            x16.append(x.astype(BF16))
        part = [_dot_nt(x16[i], x16[i]) for i in idx]
        scores = [jnp.where(level_of == level, part[i], scores[i]) for i in idx]
    yield
    last = slice(t_len - 1, t_len)
    q_in = [(qb[i] * jnp.exp(a[i])).astype(BF16) for i in idx]
    k_out = [(kb[i] * jnp.exp(a[i][last] - a[i])).astype(BF16) for i in idx]
    intra = [_dot(scores[i].astype(BF16), iv16[i]) for i in idx]
    if transposed:
        inter = [_dot_nt(q_in[i], s_old[i].astype(BF16)) for i in idx]
        kv = [_dot_tn(iv16[i], k_out[i]) for i in idx]
        carry = [jnp.exp(a[i][last]) for i in idx]
    else:
        inter = [_dot(q_in[i], s_old[i].astype(BF16)) for i in idx]
        kv = [_dot_tn(k_out[i], iv16[i]) for i in idx]
        carry = [jnp.exp(a[i].T[:, t_len - 1:t_len]) for i in idx]
    yield
    s_new = [carry[i] * s_old[i] + kv[i] for i in idx]
    o = [inter[i] + intra[i] for i in idx]
    rms = [lax.rsqrt(jnp.mean(o[i] * o[i], axis=1, keepdims=True) + RMS_EPS) for i in idx]
    out = [gate[i] * (o[i] * rms[i] * g_norm[i]) for i in idx]
    return out, s_new


def _load_state(c0_ref, n0_ref, m0_ref, s0_ref, c_ref, n_ref, m_ref, s_ref, n_seq_blk):
    shared = c0_ref.shape[0] == 1 and n_seq_blk > 1
    for s in range(n_seq_blk):
        src = 0 if shared else s
        c_ref[s] = c0_ref[src]
        n_ref[s] = n0_ref[src]
        m_ref[s] = m0_ref[src]
        for h in range(N_HEADS):
            s_ref[s, h] = s0_ref[src, h].T


def _finish_state(s_ref, n_seq_blk):
    for s in range(n_seq_blk):
        for h in range(N_HEADS):
            s_ref[s, h] = s_ref[s, h].T


def _act_cols(group, h):
    base = group * D_GROUP if group < KB_GROUP else D_PROJ + (group - KB_GROUP) * D_GROUP
    return slice(base + h * D_HEAD, base + (h + 1) * D_HEAD)


def _mixer_body(p_ref, bf_ref, ga_ref, gb_ref, lb_ref, c_ref, n_ref, m_ref, s_ref,
                *, t_len, n_seq_blk, activated, side_stages=None, state_in=None):
    c_in, n_in, m_in, s_in = (c_ref, n_ref, m_ref, s_ref) if state_in is None else state_in
    rows = lax.broadcasted_iota(jnp.int32, (t_len, t_len), 0)
    cols = lax.broadcasted_iota(jnp.int32, (t_len, t_len), 1)
    causal = cols <= rows
    tril = causal.astype(BF16)
    level_of = jnp.where(rows > cols, 31 - lax.clz(rows ^ cols), jnp.where(rows == cols, -2, -1))
    lane = lax.broadcasted_iota(jnp.int32, (t_len, LANES), 1)
    is_f = (lane >= N_HEADS) & (lane < 2 * N_HEADS)
    hd = lambda j, h: slice(j * D_GROUP + h * D_HEAD, j * D_GROUP + (h + 1) * D_HEAD)

    units = [(s, h) for s in range(n_seq_blk) for h in range(N_HEADS)]
    rs = lambda s: slice(s * t_len, (s + 1) * t_len)

    z_seq, zt_seq, n_seq, m_seq = [], [], [], []
    for s in range(n_seq_blk):
        gates = p_ref[rs(s), GATE_COL:GATE_COL + LANES]
        log_f = jnp.where(is_f, jax.nn.log_sigmoid(gates + bf_ref[...]), 0.0)
        cum_f = _cumsum_rows(tril, log_f)
        z_seq.append(jnp.where(is_f, cum_f, jnp.where(lane < N_HEADS, gates, 0.0)))
        zt_seq.append(z_seq[s].T)
        n_seq.append(n_in[s])
        m_seq.append(m_in[s])
    computed = {}

    def act(group):
        if activated:
            return [p_ref[rs(s), _act_cols(group, h)] for s, h in units]
        source = 5 if group >= KB_GROUP else group
        for s, h in units:
            if (source, s, h) not in computed:
                computed[source, s, h] = _activate(source, p_ref[rs(s), hd(source, h)],
                                                   lb_ref[:, hd(0, h)])
        return [computed[source, s, h][group] for s, h in units]

    lane_row = lane[:1, :]
    m0_rows = []
    for s in range(n_seq_blk):
        m0_row = jnp.zeros((1, LANES), F32)
        for h in range(N_HEADS):
            m0_row = jnp.where(lane_row == N_HEADS + h, m_seq[s][:, h:h + 1], m0_row)
        m0_rows.append(m0_row)
    mlstm = _mlstm_units(
        q=act(0), k=act(1), v=act(2), gate=act(3), z=z_seq, zt=zt_seq, m0_row=m0_rows,
        c_old=[c_in[s, h] for s, h in units], n_old=[n_seq[s][h:h + 1, :] for s, h in units],
        g_norm=[ga_ref[:, hd(0, h)] for s, h in units], causal=causal, lane=lane, t_len=t_len)
    hgrn = _hgrn_units(
        qb=act(4), log_f=act(5), f=act(F_GROUP), kb=act(KB_GROUP), iv=act(6), gate=act(7),
        s_old=[s_in[s, h] for s, h in units], transposed=state_in is None,
        g_norm=[gb_ref[:, hd(0, h)] for s, h in units], tril=tril, level_of=level_of, t_len=t_len)
    gens, per_round = [mlstm, hgrn], [1, 2]
    if side_stages is not None:
        gens, per_round = gens + [side_stages], per_round + [SIDE_STAGES_PER_ROUND]
    results = _run_interleaved(gens, per_round)
    (outs, c_new, n_new, m_new), (outs_b, s_new_t) = results[:2]

    for i, (s, h) in enumerate(units):
        c_ref[s, h] = c_new[i]
        s_ref[s, h] = s_new_t[i]
    head_lane = lax.broadcasted_iota(jnp.int32, (1, N_HEADS), 1)
    for s in range(n_seq_blk):
        n_ref[s] = jnp.concatenate(n_new[s * N_HEADS:(s + 1) * N_HEADS], axis=0)
        m_row = m_seq[s]
        for h in range(N_HEADS):
            m_row = jnp.where(head_lane == h, m_new[s * N_HEADS + h], m_row)
        m_ref[s] = m_row
    return outs, outs_b


def _mixer_kernel(p_ref, c0_ref, n0_ref, m0_ref, s0_ref, bf_ref, ga_ref, gb_ref, lb_ref,
                  mix_ref, c_ref, n_ref, m_ref, s_ref, *, t_len, n_chunks, n_seq_blk):
    chunk = pl.program_id(1)
    single_block = n_chunks == 1 and c0_ref.shape == c_ref.shape

    if not single_block:
        @pl.when(chunk == 0)
        def _():
            _load_state(c0_ref, n0_ref, m0_ref, s0_ref, c_ref, n_ref, m_ref, s_ref, n_seq_blk)

    outs_a, outs_b = _mixer_body(
        p_ref, bf_ref, ga_ref, gb_ref, lb_ref, c_ref, n_ref, m_ref, s_ref, t_len=t_len,
        n_seq_blk=n_seq_blk, activated=False,
        state_in=(c0_ref, n0_ref, m0_ref, s0_ref) if single_block else None)
    for i in range(n_seq_blk * N_HEADS):
        s, h = divmod(i, N_HEADS)
        rows = slice(s * t_len, (s + 1) * t_len)
        mix_ref[rows, h * D_HEAD:(h + 1) * D_HEAD] = outs_a[i].astype(mix_ref.dtype)
        mix_ref[rows, D_GROUP + h * D_HEAD:D_GROUP + (h + 1) * D_HEAD] = outs_b[i].astype(mix_ref.dtype)

    if not single_block:
        @pl.when(chunk == n_chunks - 1)
        def _():
            _finish_state(s_ref, n_seq_blk)


def _mixer(proj, c0, n0, m0, s0, bf_row, ga, gb, lb, *, n_seq, n_chunks, t_len, n_seq_blk,
           shared_init):
    assert not shared_init or n_seq_blk == 1
    assert n_chunks == 1 or n_seq_blk == 1
    nb = n_seq_blk
    init = (lambda b, c: (0, 0, 0, 0)) if shared_init else (lambda b, c: (b, 0, 0, 0))
    init3 = (lambda b, c: (0, 0, 0)) if shared_init else (lambda b, c: (b, 0, 0))
    const = lambda b, c: (0, 0)
    state4 = pl.BlockSpec((nb, N_HEADS, D_HEAD, D_HEAD), lambda b, c: (b, 0, 0, 0))
    return pl.pallas_call(
        functools.partial(_mixer_kernel, t_len=t_len, n_chunks=n_chunks, n_seq_blk=nb),
        grid=(n_seq // nb, n_chunks),
        in_specs=[
            pl.BlockSpec((nb * t_len, D_PROJ), lambda b, c: (b * n_chunks + c, 0)),
            pl.BlockSpec((nb, N_HEADS, D_HEAD, D_HEAD), init),
            pl.BlockSpec((nb, N_HEADS, D_HEAD), init3),
            pl.BlockSpec((nb, 1, N_HEADS), init3),
            pl.BlockSpec((nb, N_HEADS, D_HEAD, D_HEAD), init),
            pl.BlockSpec((1, LANES), const),
            pl.BlockSpec((1, D_GROUP), const),
            pl.BlockSpec((1, D_GROUP), const),
            pl.BlockSpec((1, D_GROUP), const),
        ],
        out_specs=[
            pl.BlockSpec((nb * t_len, D_MODEL), lambda b, c: (b * n_chunks + c, 0)),
            state4,
            pl.BlockSpec((nb, N_HEADS, D_HEAD), lambda b, c: (b, 0, 0)),
            pl.BlockSpec((nb, 1, N_HEADS), lambda b, c: (b, 0, 0)),
            state4,
        ],
        out_shape=[
            jax.ShapeDtypeStruct((n_seq * n_chunks * t_len, D_MODEL), BF16),
            jax.ShapeDtypeStruct((n_seq, N_HEADS, D_HEAD, D_HEAD), F32),
            jax.ShapeDtypeStruct((n_seq, N_HEADS, D_HEAD), F32),
            jax.ShapeDtypeStruct((n_seq, 1, N_HEADS), F32),
            jax.ShapeDtypeStruct((n_seq, N_HEADS, D_HEAD, D_HEAD), F32),
        ],
        compiler_params=pltpu.CompilerParams(
            dimension_semantics=("arbitrary", "arbitrary"), vmem_limit_bytes=VMEM_LIMIT),
        name=f"mixer_t{t_len}",
    )(proj, c0, n0, m0, s0, bf_row, ga, gb, lb)


def _meta_kernel(x_ref, ge_ref, be_ref, win_ref, bin_ref, bf_ref, ga_ref, gb_ref, lb_ref,
                 wout_ref, bout_ref, g1_ref, b1_ref, wu_ref, bu_ref,
                 c_ref, n_ref, m_ref, s_ref, conv_ref, proj_scr):
    t_len = x_ref.shape[0]
    xn = _layer_norm(x_ref[...], ge_ref[...], be_ref[...])
    proj_scr[...] = _dot(xn.astype(BF16), win_ref[...]) + bin_ref[...]
    for ref in (c_ref, n_ref, m_ref, s_ref):
        ref[...] = jnp.zeros(ref.shape, ref.dtype)
    outs_a, outs_b = _mixer_body(proj_scr, bf_ref, ga_ref, gb_ref, lb_ref, c_ref, n_ref, m_ref, s_ref,
                                 t_len=t_len, n_seq_blk=1, activated=False)
    _finish_state(s_ref, 1)
    mix = jnp.concatenate([o.astype(BF16) for o in outs_a + outs_b], axis=1)
    y = _dot(mix, wout_ref[...]) + bout_ref[...]
    x1 = _layer_norm(ALPHA * xn + y, g1_ref[...], b1_ref[...])
    u = _dot(x1.astype(BF16), wu_ref[...]) + bu_ref[...]
    conv_ref[0] = u[t_len - (CONV_W - 1):, :]


def _meta_state(x, ln_e_g, ln_e_b, w_in, b_in, bf_row, ga, gb, lb, w_out, b_out, ln_g, ln_b,
                w_up, b_up):
    t_len = x.shape[0]
    full = lambda shape: _resident(shape, lambda i: (0,) * len(shape))
    vec, grp = full((1, D_MODEL)), full((1, D_GROUP))
    state4 = (1, N_HEADS, D_HEAD, D_HEAD)
    return pl.pallas_call(
        _meta_kernel,
        grid=(1,),
        in_specs=[full((t_len, D_MODEL)), vec, vec, full((D_MODEL, D_PROJ)), full((1, D_PROJ)),
                  full((1, LANES)), grp, grp, grp, full((D_MODEL, D_MODEL)), vec, vec, vec,
                  full((D_MODEL, D_FF)), full((1, D_FF))],
        out_specs=[pl.BlockSpec(shape, lambda i, rank=len(shape): (0,) * rank)
                   for shape in (state4, (1, N_HEADS, D_HEAD), (1, 1, N_HEADS), state4,
                                 (1, CONV_W - 1, D_FF))],
        out_shape=[
            jax.ShapeDtypeStruct(state4, F32),
            jax.ShapeDtypeStruct((1, N_HEADS, D_HEAD), F32),
            jax.ShapeDtypeStruct((1, 1, N_HEADS), F32),
            jax.ShapeDtypeStruct(state4, F32),
            jax.ShapeDtypeStruct((1, CONV_W - 1, D_FF), F32),
        ],
        scratch_shapes=[pltpu.VMEM((t_len, D_PROJ), F32)],
        compiler_params=pltpu.CompilerParams(
            dimension_semantics=("arbitrary",), vmem_limit_bytes=VMEM_LIMIT),
        name="meta_state",
    )(x, ln_e_g, ln_e_b, w_in, b_in, bf_row, ga, gb, lb, w_out, b_out, ln_g, ln_b, w_up, b_up)


def _in_proj_stages(x_ref, g_ref, b_ref, w_ref, bias_ref, lb_ref, act_ref, xn_ref):
    assert D_GROUP % IN_PROJ_STAGE_COLS == 0
    xn = _layer_norm(x_ref[...].reshape(xn_ref.shape), g_ref[...], b_ref[...])
    xn_ref[...] = xn
    x16 = xn.astype(BF16)
    for lo in range(0, D_PROJ, IN_PROJ_STAGE_COLS):
        hi = min(lo + IN_PROJ_STAGE_COLS, D_PROJ)
        yield
        block = _dot(x16, w_ref[:, lo:hi]) + bias_ref[:, lo:hi]
        group, off = divmod(lo, D_GROUP)
        if lo >= GATE_COL:
            act_ref[:, lo:hi] = block
            continue
        for dst, val in _activate(group, block, lb_ref[:, off:off + hi - lo]).items():
            base = _act_cols(dst, 0).start + off
            act_ref[:, base:base + hi - lo] = val


def _prompt_kernel(x0_ref, xnext_ref, ge_ref, be_ref, win_ref, bin_ref, c0_ref, n0_ref, m0_ref, s0_ref,
                   bf_ref, ga_ref, gb_ref, lb_ref, wout_ref, bout_ref, g1_ref, b1_ref,
                   x1_ref, c_ref, n_ref, m_ref, s_ref, proj_scr, xn_scr, proj_alt, xn_alt,
                   *, t_len, n_chunks, n_seq_blk):
    chunk = pl.program_id(1)
    step = pl.program_id(0) * n_chunks + chunk

    @pl.when(step == 0)
    def _():
        first = _in_proj_stages(x0_ref, ge_ref, be_ref, win_ref, bin_ref, lb_ref, proj_scr, xn_scr)
        _run_interleaved([first], [1])

    @pl.when(chunk == 0)
    def _():
        _load_state(c0_ref, n0_ref, m0_ref, s0_ref, c_ref, n_ref, m_ref, s_ref, n_seq_blk)

    def tile(proj_cur, xn_cur, proj_next, xn_next):
        next_proj = _in_proj_stages(xnext_ref, ge_ref, be_ref, win_ref, bin_ref, lb_ref,
                                    proj_next, xn_next)
        outs_a, outs_b = _mixer_body(
            proj_cur, bf_ref, ga_ref, gb_ref, lb_ref, c_ref, n_ref, m_ref, s_ref,
            t_len=t_len, n_seq_blk=n_seq_blk, activated=True, side_stages=next_proj)
        mix = jnp.concatenate(
            [jnp.concatenate([o.astype(BF16) for o in outs_a[s * N_HEADS:(s + 1) * N_HEADS]
                              + outs_b[s * N_HEADS:(s + 1) * N_HEADS]], axis=1)
             for s in range(n_seq_blk)], axis=0)
        y = _dot(mix, wout_ref[...]) + bout_ref[...]
        x1 = _layer_norm(ALPHA * xn_cur[...] + y, g1_ref[...], b1_ref[...])
        x1_ref[...] = x1.reshape(x1_ref.shape)

    @pl.when(step % 2 == 0)
    def _():
        tile(proj_scr, xn_scr, proj_alt, xn_alt)

    @pl.when(step % 2 == 1)
    def _():
        tile(proj_alt, xn_alt, proj_scr, xn_scr)

    @pl.when(chunk == n_chunks - 1)
    def _():
        _finish_state(s_ref, n_seq_blk)


def _prompt_mixer(x, ln_e_g, ln_e_b, w_in, b_in, c0, n0, m0, s0, bf_row, ga, gb, lb,
                  w_out, b_out, ln_g, ln_b, *, n_chunks, t_len, n_seq_blk):
    n_seq = x.shape[0]
    nb = n_seq_blk
    n_tiles = (n_seq // nb) * n_chunks
    const = lambda b, c: (0, 0)
    init4 = lambda b, c: (0, 0, 0, 0)
    init3 = lambda b, c: (0, 0, 0)
    vec = pl.BlockSpec((1, D_MODEL), const)
    grp = pl.BlockSpec((1, D_GROUP), const)
    state4 = pl.BlockSpec((nb, N_HEADS, D_HEAD, D_HEAD), lambda b, c: (b, 0, 0, 0))

    def next_tile(b, c):
        nxt = jnp.minimum(b * n_chunks + c + 1, n_tiles - 1)
        return (nxt // n_chunks, nxt % n_chunks, 0)

    return pl.pallas_call(
        functools.partial(_prompt_kernel, t_len=t_len, n_chunks=n_chunks, n_seq_blk=nb),
        grid=(n_seq // nb, n_chunks),
        in_specs=[
            _resident((nb, t_len, D_MODEL), init3),
            pl.BlockSpec((nb, t_len, D_MODEL), next_tile),
            vec, vec,
            _resident((D_MODEL, D_PROJ), const),
            pl.BlockSpec((1, D_PROJ), const),
            pl.BlockSpec((1, N_HEADS, D_HEAD, D_HEAD), init4),
            pl.BlockSpec((1, N_HEADS, D_HEAD), init3),
            pl.BlockSpec((1, 1, N_HEADS), init3),
            pl.BlockSpec((1, N_HEADS, D_HEAD, D_HEAD), init4),
            pl.BlockSpec((1, LANES), const),
            grp, grp, grp,
            _resident((D_MODEL, D_MODEL), const),
            vec, vec, vec,
        ],
        out_specs=[
            pl.BlockSpec((nb, t_len, D_MODEL), lambda b, c: (b, c, 0)),
            state4,
            pl.BlockSpec((nb, N_HEADS, D_HEAD), lambda b, c: (b, 0, 0)),
            pl.BlockSpec((nb, 1, N_HEADS), lambda b, c: (b, 0, 0)),
            state4,
        ],
        out_shape=[
            jax.ShapeDtypeStruct(x.shape, F32),
            jax.ShapeDtypeStruct((n_seq, N_HEADS, D_HEAD, D_HEAD), F32),
            jax.ShapeDtypeStruct((n_seq, N_HEADS, D_HEAD), F32),
            jax.ShapeDtypeStruct((n_seq, 1, N_HEADS), F32),
            jax.ShapeDtypeStruct((n_seq, N_HEADS, D_HEAD, D_HEAD), F32),
        ],
        scratch_shapes=[pltpu.VMEM((nb * t_len, D_ACT), F32), pltpu.VMEM((nb * t_len, D_MODEL), F32),
                        pltpu.VMEM((nb * t_len, D_ACT), F32), pltpu.VMEM((nb * t_len, D_MODEL), F32)],
        compiler_params=pltpu.CompilerParams(
            dimension_semantics=("arbitrary", "arbitrary"), vmem_limit_bytes=VMEM_LIMIT),
        name="prompt_mixer",
    )(x, x, ln_e_g, ln_e_b, w_in, b_in, c0, n0, m0, s0, bf_row, ga, gb, lb, w_out, b_out, ln_g, ln_b)


def _ffn_kernel(x_ref, *refs, n_seq_blk, t_len):
    _ffn_tile(x_ref[...], *refs, n_seq_blk=n_seq_blk, t_len=t_len)


def _out_proj_ffn_kernel(x_ref, mix_ref, ge_ref, be_ref, wo_ref, bo_ref, g1_ref, b1_ref, *refs,
                         n_seq_blk, t_len):
    xn = _layer_norm(x_ref[...], ge_ref[...], be_ref[...])
    y = _dot(mix_ref[...], wo_ref[...]) + bo_ref[...]
    x1 = _layer_norm(ALPHA * xn + y, g1_ref[...], b1_ref[...])
    _ffn_tile(x1, *refs, n_seq_blk=n_seq_blk, t_len=t_len)


def _ffn_tile(x, cs_ref, wu_ref, bu_ref, wc_ref, bc_ref, wd_ref, bd_ref, g_ref, b_ref,
              y_ref, nc_ref, full_ref, *, n_seq_blk, t_len):
    hist = SUBLANES - (CONV_W - 1)

    @pl.when(pl.program_id(1) == 0)
    def _():
        full_ref[:, hist:SUBLANES, :] = cs_ref[...]

    up = _dot(x.astype(BF16), wu_ref[...]) + bu_ref[...]
    u = up[:, :D_FF].reshape(n_seq_blk, t_len, D_FF)
    gate = up[:, D_FF:].reshape(n_seq_blk, t_len, D_FF)
    full_ref[:, SUBLANES:SUBLANES + t_len, :] = u
    conv = bc_ref[...] + u * wc_ref[CONV_W - 1:CONV_W, :]
    for j in range(CONV_W - 1):
        conv = conv + full_ref[:, hist + j:hist + j + t_len, :] * wc_ref[j:j + 1, :]
    last = full_ref[:, hist + t_len:SUBLANES + t_len, :]
    nc_ref[...] = last
    full_ref[:, hist:SUBLANES, :] = last
    act = (conv * _sigmoid(conv) * gate).reshape(n_seq_blk * t_len, D_FF)
    ffn = _dot(act.astype(BF16), wd_ref[...]) + bd_ref[...]
    y_ref[...] = _layer_norm(ALPHA * x + ffn, g_ref[...], b_ref[...])


def _ffn(x, conv_state, w_up, b_up, w_conv, b_conv, w_down, b_down, ln_g, ln_b,
         *, n_seq, seq_len, n_seq_blk, t_len, shared_init, out_proj=None):
    n_t = seq_len // t_len
    rows = n_seq_blk * t_len
    const = lambda s, t: (0, 0)
    cs_map = (lambda s, t: (0, 0, 0)) if shared_init else (lambda s, t: (s, 0, 0))
    row = pl.BlockSpec((rows, D_MODEL), lambda s, t: (s * n_t + t, 0))
    vec = pl.BlockSpec((1, D_MODEL), const)
    body, lead_specs, lead_args = _ffn_kernel, [row], (x,)
    if out_proj is not None:
        body = _out_proj_ffn_kernel
        lead_specs = [row, row, vec, vec, _resident((D_MODEL, D_MODEL), const), vec, vec, vec]
        lead_args = (x,) + tuple(out_proj)
    return pl.pallas_call(
        functools.partial(body, n_seq_blk=n_seq_blk, t_len=t_len),
        grid=(n_seq // n_seq_blk, n_t),
        in_specs=lead_specs + [
            pl.BlockSpec((n_seq_blk, CONV_W - 1, D_FF), cs_map),
            _resident((D_MODEL, 2 * D_FF), const),
            pl.BlockSpec((1, 2 * D_FF), const),
            pl.BlockSpec((CONV_W, D_FF), const),
            pl.BlockSpec((1, D_FF), const),
            _resident((D_FF, D_MODEL), const),
            vec, vec, vec,
        ],
        out_specs=[row, pl.BlockSpec((n_seq_blk, CONV_W - 1, D_FF), lambda s, t: (s, 0, 0))],
        out_shape=[
            jax.ShapeDtypeStruct((n_seq * seq_len, D_MODEL), F32),
            jax.ShapeDtypeStruct((n_seq, CONV_W - 1, D_FF), F32),
        ],
        scratch_shapes=[pltpu.VMEM((n_seq_blk, SUBLANES + t_len, D_FF), F32)],
        compiler_params=pltpu.CompilerParams(
            dimension_semantics=("arbitrary", "arbitrary"), vmem_limit_bytes=VMEM_LIMIT),
        name=f"ffn_t{t_len}",
    )(*lead_args, conv_state, w_up, b_up, w_conv, b_conv, w_down, b_down, ln_g, ln_b)


def kernel(x_prompt, x_sample, state_mlstm_C, state_mlstm_n, state_mlstm_m, state_hgrn_S, state_ffn_conv, meta_tokens, ln_emb_g, ln_emb_b, w_in, b_in, b_fgate_a, g_norm_a, g_norm_b, hgrn_lb_logits, w_out, b_out, ln1_g, ln1_b, w_up, b_up, w_conv, b_conv, w_down, b_down, ln2_g, ln2_b):
    assert w_in.shape[0] == DEPTH == 1
    n_prompt, seq, _ = x_prompt.shape
    n_sample, dec_seq, _ = x_sample.shape
    row = lambda v: v.reshape(1, -1).astype(F32)

    gate0 = 4 * D_GROUP
    gate1 = gate0 + 2 * N_HEADS
    pad = D_PROJ - w_in.shape[2]
    w_in_p = _regroup_in_proj_weight(w_in[0].T, tm=256)
    b_in_p = jnp.concatenate(
        [b_in[0][:gate0], b_in[0][gate1:], b_in[0][gate0:gate1], jnp.zeros((pad,), b_in.dtype)]
    ).reshape(1, D_PROJ).astype(F32)
    bf_row = jnp.zeros((1, LANES), F32).at[0, N_HEADS:2 * N_HEADS].set(b_fgate_a[0].astype(F32))
    lb = jnp.cumsum(jax.nn.softmax(hgrn_lb_logits.astype(F32), axis=0), axis=0)[0].reshape(1, D_GROUP)
    ga, gb = row(g_norm_a[0]), row(g_norm_b[0])
    ln_e = (row(ln_emb_g), row(ln_emb_b))
    out_p = (w_out[0].astype(BF16), row(b_out[0]), row(ln1_g[0]), row(ln1_b[0]))
    ffn_p = (w_up[0].astype(BF16), row(b_up[0]), w_conv[0].astype(F32), row(b_conv[0]),
             w_down[0].astype(BF16), row(b_down[0]), row(ln2_g[0]), row(ln2_b[0]))

    c_m, n_m, m_m, s_m, conv_m = _meta_state(
        meta_tokens.astype(F32), *ln_e, w_in_p, b_in_p, bf_row, ga, gb, lb, *out_p, ffn_p[0], ffn_p[1])

    x1_p, c_p, n_p, m_p, s_p = _prompt_mixer(
        x_prompt.astype(F32), *ln_e, w_in_p, b_in_p, c_m, n_m, m_m, s_m, bf_row, ga, gb, lb, *out_p,
        n_chunks=seq // PROMPT_CHUNK, t_len=PROMPT_CHUNK, n_seq_blk=PROMPT_SEQS_PER_STEP)
    y_p, conv_p = _ffn(x1_p.reshape(n_prompt * seq, D_MODEL), conv_m, *ffn_p, n_seq=n_prompt,
                       seq_len=seq, n_seq_blk=1, t_len=512, shared_init=True)

    sample_state = (state_mlstm_C[0].astype(F32), state_mlstm_n[0].astype(F32),
                    state_mlstm_m[0].astype(F32).reshape(n_sample, 1, N_HEADS),
                    state_hgrn_S[0].astype(F32))
    xs_rows = x_sample.reshape(n_sample * dec_seq, D_MODEL).astype(F32)
    proj_s = _in_proj(xs_rows, *ln_e, w_in_p, b_in_p, tm=256)
    mix_s, c_s, n_s, m_s, s_s = _mixer(
        proj_s, *sample_state, bf_row, ga, gb, lb, n_seq=n_sample, n_chunks=1, t_len=dec_seq,
        n_seq_blk=8, shared_init=False)
    y_s, conv_s = _ffn(xs_rows, state_ffn_conv[0].astype(F32), *ffn_p, n_seq=n_sample,
                       seq_len=dec_seq, n_seq_blk=32, t_len=dec_seq, shared_init=False,
                       out_proj=(mix_s, *ln_e, *out_p))

    lead = lambda v: v[None]
    return (y_p.reshape(n_prompt, seq, D_MODEL), y_s.reshape(n_sample, dec_seq, D_MODEL),
            lead(c_p), lead(n_p), lead(m_p.reshape(n_prompt, N_HEADS)), lead(s_p), lead(conv_p),
            lead(c_s), lead(n_s), lead(m_s.reshape(n_sample, N_HEADS)), lead(s_s), lead(conv_s))
```

```python
import functools

import jax
import jax.numpy as jnp
from jax import lax
from jax.experimental import pallas as pl
from jax.experimental.pallas import tpu as pltpu

D_MODEL = 1024
N_META = 16
N_HEADS = 4
D_HEAD = 128
D_GROUP = N_HEADS * D_HEAD
D_FF = 2816
CONV_W = 3
DEPTH = 1
ALPHA = (2.0 * DEPTH) ** 0.25
LN_EPS = 1e-5
RMS_EPS = 1e-6
NEG_LOG2_E = -1.4426950408889634

LANES = 128
SUBLANES = 8
GATE_COL = 8 * D_GROUP
D_PROJ = GATE_COL + LANES
KB_GROUP, F_GROUP = 8, 9
D_ACT = D_PROJ + 2 * D_GROUP
N_EARLY_GROUPS = 4
IN_PROJ_STAGE_COLS = 256
SIDE_STAGES_PER_ROUND = 3
PROMPT_SEQS_PER_STEP = 2
PROMPT_CHUNK = 128
VMEM_LIMIT = 56 * 1024 * 1024

F32 = jnp.float32
BF16 = jnp.bfloat16
NT_DIMS = (((1,), (1,)), ((), ()))
TN_DIMS = (((0,), (0,)), ((), ()))


def _layer_norm(x, g, b):
    mu = jnp.mean(x, axis=-1, keepdims=True)
    xc = x - mu
    var = jnp.mean(xc * xc, axis=-1, keepdims=True)
    return xc * lax.rsqrt(var + LN_EPS) * g + b


def _exp_neg(x):
    return jnp.exp2(x * NEG_LOG2_E)


def _sigmoid(x):
    return 1.0 / (1.0 + _exp_neg(x))


def _resident(block_shape, index_map):
    return pl.BlockSpec(block_shape, index_map, pipeline_mode=pl.Buffered(1))


def _dot(a, b):
    return jnp.dot(a, b, preferred_element_type=F32)


def _dot_nt(a, b):
    return lax.dot_general(a, b, NT_DIMS, preferred_element_type=F32)


def _dot_tn(a, b):
    return lax.dot_general(a, b, TN_DIMS, preferred_element_type=F32)


def _regroup_kernel(wt_ref, o_ref):
    gate0 = 4 * D_GROUP
    gate1 = gate0 + 2 * N_HEADS
    for j in range(GATE_COL // LANES):
        src = j * LANES if j * LANES < gate0 else j * LANES + (gate1 - gate0)
        o_ref[:, j * LANES:(j + 1) * LANES] = wt_ref[src:src + LANES, :].T.astype(o_ref.dtype)
    gates = wt_ref[gate0:gate1, :].T.astype(o_ref.dtype)
    o_ref[:, GATE_COL:] = jnp.concatenate(
        [gates, jnp.zeros((gates.shape[0], LANES - gates.shape[1]), o_ref.dtype)], axis=1)


def _regroup_in_proj_weight(w_t, *, tm):
    cols, n = w_t.shape
    return pl.pallas_call(
        _regroup_kernel,
        grid=(n // tm,),
        in_specs=[pl.BlockSpec((cols, tm), lambda i: (0, i))],
        out_specs=pl.BlockSpec((tm, D_PROJ), lambda i: (i, 0)),
        out_shape=jax.ShapeDtypeStruct((n, D_PROJ), BF16),
        compiler_params=pltpu.CompilerParams(dimension_semantics=("arbitrary",)),
        name="regroup_w_in",
    )(w_t)


def _in_proj_kernel(x_ref, g_ref, b_ref, w_ref, bias_ref, o_ref):
    xn = _layer_norm(x_ref[...], g_ref[...], b_ref[...])
    o_ref[...] = _dot(xn.astype(BF16), w_ref[...]) + bias_ref[...]


def _in_proj(x, ln_g, ln_b, w, bias, *, tm):
    n = x.shape[0]
    const = lambda i: (0, 0)
    return pl.pallas_call(
        _in_proj_kernel,
        grid=(n // tm,),
        in_specs=[
            pl.BlockSpec((tm, D_MODEL), lambda i: (i, 0)),
            pl.BlockSpec((1, D_MODEL), const),
            pl.BlockSpec((1, D_MODEL), const),
            pl.BlockSpec((D_MODEL, D_PROJ), const),
            pl.BlockSpec((1, D_PROJ), const),
        ],
        out_specs=pl.BlockSpec((tm, D_PROJ), lambda i: (i, 0)),
        out_shape=jax.ShapeDtypeStruct((n, D_PROJ), F32),
        compiler_params=pltpu.CompilerParams(
            dimension_semantics=("arbitrary",), vmem_limit_bytes=VMEM_LIMIT),
        name="in_proj",
    )(x, ln_g, ln_b, w, bias)


def _block_rows(x, level, t_len, row_in_block):
    size = 2 << level
    if size > SUBLANES:
        pieces = [jnp.broadcast_to(x[j * size + row_in_block:j * size + row_in_block + 1, :],
                                   (size, x.shape[1])) for j in range(t_len // size)]
        return pieces[0] if len(pieces) == 1 else jnp.concatenate(pieces, axis=0)
    x3 = x.reshape(t_len // SUBLANES, SUBLANES, x.shape[1])
    sub = lax.broadcasted_iota(jnp.int32, x3.shape, 1)
    out = None
    for j in range(SUBLANES // size):
        row = jnp.broadcast_to(x3[:, j * size + row_in_block:j * size + row_in_block + 1, :], x3.shape)
        out = row if out is None else jnp.where(sub >= j * size, row, out)
    return out.reshape(x.shape)


def _interleave_halves(lower, upper, level, t_len):
    half = 1 << level
    if half >= SUBLANES:
        pieces = []
        for j in range(t_len // (2 * half)):
            pieces.append(lower[2 * half * j:2 * half * j + half])
            pieces.append(upper[2 * half * j + half:2 * half * (j + 1)])
        return jnp.concatenate(pieces, axis=0)
    rows = lax.broadcasted_iota(jnp.int32, lower.shape, 0)
    return jnp.where((rows & half) != 0, upper, lower)


def _run_interleaved(gens, stages_per_round):
    results = [None] * len(gens)
    live = [True] * len(gens)
    while any(live):
        for g, steps in enumerate(stages_per_round):
            for _ in range(steps):
                if live[g]:
                    try:
                        next(gens[g])
                    except StopIteration as stop:
                        results[g], live[g] = stop.value, False
    return results


def _cumsum_rows(tril16, x):
    hi = x.astype(BF16)
    rest = x - hi.astype(F32)
    mid = rest.astype(BF16)
    lo = (rest - mid.astype(F32)).astype(BF16)
    return _dot(tril16, hi) + _dot(tril16, mid) + _dot(tril16, lo)


def _activate(group, x, lb=None):
    if group == 1:
        return {1: x * (D_HEAD ** -0.5)}
    if group in (3, 7):
        return {group: _sigmoid(x)}
    if group == 4:
        return {4: x * _sigmoid(x)}
    if group == 5:
        f = lb + (1.0 - lb) * _sigmoid(x)
        return {5: jnp.log(f), KB_GROUP: (1.0 - lb) / (1.0 + jnp.exp(x)), F_GROUP: f}
    return {group: x}


def _mlstm_units(*, q, k, v, gate, z, zt, m0_row, c_old, n_old, g_norm, causal, lane, t_len):
    idx = range(len(q))
    seq = [i // N_HEADS for i in idx]
    b_lane = [N_HEADS + i % N_HEADS for i in idx]
    q16 = [q[i].astype(BF16) for i in idx]
    k16 = [k[i].astype(BF16) for i in idx]
    qk = [_dot_nt(q16[i], k16[i]) for i in idx]
    qc = [_dot(q16[i], c_old[i].astype(BF16)) for i in idx]
    bs_row = [zt[seq[i]][b_lane[i]:b_lane[i] + 1, :] - zt[seq[i]][i % N_HEADS:i % N_HEADS + 1, :]
              for i in idx]
    yield
    col = lambda per_seq, i: per_seq[seq[i]][:, b_lane[i]:b_lane[i] + 1]
    d = [jnp.where(causal, col(z, i) - bs_row[i], -jnp.inf) for i in idx]
    row_max = [jnp.max(d[i], axis=1, keepdims=True) for i in idx]
    last = slice(t_len - 1, t_len)
    m_t_seq, dec_seq, floor_seq, w_last_seq = [], [], [], []
    for s in range(len(z)):
        r = z[s] + m0_row[s]
        d_max = jnp.full(r.shape, -jnp.inf, F32)
        for h in range(N_HEADS):
            d_max = jnp.where(lane == N_HEADS + h, row_max[s * N_HEADS + h], d_max)
        m_t = jnp.maximum(r, d_max)
        i_gate = pltpu.roll(z[s], N_HEADS, axis=1)
        m_t_seq.append(m_t)
        dec_seq.append(jnp.exp(r - m_t))
        floor_seq.append(_exp_neg(m_t))
        w_last_seq.append(jnp.exp(z[s][last] - z[s] + i_gate - m_t[last]))
    m_t = [col(m_t_seq, i) for i in idx]
    dec = [col(dec_seq, i) for i in idx]
    w_last = [col(w_last_seq, i) for i in idx]
    sw = [jnp.exp(d[i] - m_t[i]) * qk[i] for i in idx]
    yield
    swv = [_dot(sw[i].astype(BF16), v[i].astype(BF16)) for i in idx]
    kv = [_dot_tn(k16[i], (w_last[i] * v[i]).astype(BF16)) for i in idx]
    yield
    c_new = [dec[i][last] * c_old[i] + kv[i] for i in idx]
    n_new = [dec[i][last] * n_old[i] + jnp.sum(w_last[i] * k[i], axis=0, keepdims=True) for i in idx]
    m_new = [m_t[i][last] for i in idx]
    den = [dec[i] * jnp.sum(q[i] * n_old[i], axis=1, keepdims=True)
           + jnp.sum(sw[i], axis=1, keepdims=True) for i in idx]
    hid = [(dec[i] * qc[i] + swv[i]) / jnp.maximum(jnp.abs(den[i]), col(floor_seq, i)) for i in idx]
    yield
    rms = [lax.rsqrt(jnp.mean(hid[i] * hid[i], axis=1, keepdims=True) + RMS_EPS) for i in idx]
    out = [gate[i] * (hid[i] * rms[i] * g_norm[i]) for i in idx]
    return out, c_new, n_new, m_new


def _hgrn_units(*, qb, log_f, f, kb, iv, gate, s_old_t, g_norm, tril, level_of, t_len):
    idx = range(len(qb))
    n_levels = t_len.bit_length() - 1
    a = [_cumsum_rows(tril, log_f[i]) for i in idx]
    yield
    iv16 = [iv[i].astype(BF16) for i in idx]
    diag = [_dot_nt(qb[i].astype(BF16), kb[i].astype(BF16)) for i in idx]
    scores = [jnp.where(level_of == -2, diag[i], 0.0) for i in idx]
    for level in range(n_levels):
        yield
        x16 = []
        for i in idx:
            base = _interleave_halves(kb[i], qb[i], level, t_len)
            if level == 0:
                x = base * _interleave_halves(jnp.ones_like(f[i]), f[i], 0, t_len)
            else:
                ref = _block_rows(a[i], level, t_len, (1 << level) - 1)
                x = base * _exp_neg(jnp.abs(a[i] - ref))
            x16.append(x.astype(BF16))
        part = [_dot_nt(x16[i], x16[i]) for i in idx]
        scores = [jnp.where(level_of == level, part[i], scores[i]) for i in idx]
    yield
    last = slice(t_len - 1, t_len)
    q_in = [(qb[i] * jnp.exp(a[i])).astype(BF16) for i in idx]
    k_out = [(kb[i] * jnp.exp(a[i][last] - a[i])).astype(BF16) for i in idx]
    inter = [_dot_nt(q_in[i], s_old_t[i].astype(BF16)) for i in idx]
    intra = [_dot(scores[i].astype(BF16), iv16[i]) for i in idx]
    kv = [_dot_tn(iv16[i], k_out[i]) for i in idx]
    yield
    s_new_t = [jnp.exp(a[i][last]) * s_old_t[i] + kv[i] for i in idx]
    o = [inter[i] + intra[i] for i in idx]
    rms = [lax.rsqrt(jnp.mean(o[i] * o[i], axis=1, keepdims=True) + RMS_EPS) for i in idx]
    out = [gate[i] * (o[i] * rms[i] * g_norm[i]) for i in idx]
    return out, s_new_t


def _load_state(c0_ref, n0_ref, m0_ref, s0_ref, c_ref, n_ref, m_ref, s_ref, n_seq_blk):
    shared = c0_ref.shape[0] == 1 and n_seq_blk > 1
    for s in range(n_seq_blk):
        src = 0 if shared else s
        c_ref[s] = c0_ref[src]
        n_ref[s] = n0_ref[src]
        m_ref[s] = m0_ref[src]
        for h in range(N_HEADS):
            s_ref[s, h] = s0_ref[src, h].T


def _finish_state(s_ref, n_seq_blk):
    for s in range(n_seq_blk):
        for h in range(N_HEADS):
            s_ref[s, h] = s_ref[s, h].T


def _act_cols(group, h):
    base = group * D_GROUP if group < KB_GROUP else D_PROJ + (group - KB_GROUP) * D_GROUP
    return slice(base + h * D_HEAD, base + (h + 1) * D_HEAD)


def _mixer_body(p_ref, bf_ref, ga_ref, gb_ref, lb_ref, c_ref, n_ref, m_ref, s_ref,
                *, t_len, n_seq_blk, activated, side_stages=None, late_proj=None):
    rows = lax.broadcasted_iota(jnp.int32, (t_len, t_len), 0)
    cols = lax.broadcasted_iota(jnp.int32, (t_len, t_len), 1)
    causal = cols <= rows
    tril = causal.astype(BF16)
    level_of = jnp.where(rows > cols, 31 - lax.clz(rows ^ cols), jnp.where(rows == cols, -2, -1))
    lane = lax.broadcasted_iota(jnp.int32, (t_len, LANES), 1)
    is_f = (lane >= N_HEADS) & (lane < 2 * N_HEADS)
    hd = lambda j, h: slice(j * D_GROUP + h * D_HEAD, j * D_GROUP + (h + 1) * D_HEAD)

    units = [(s, h) for s in range(n_seq_blk) for h in range(N_HEADS)]
    rs = lambda s: slice(s * t_len, (s + 1) * t_len)

    z_seq, zt_seq, n_seq, m_seq = [], [], [], []
    for s in range(n_seq_blk):
        gates = p_ref[rs(s), GATE_COL:GATE_COL + LANES]
        log_f = jnp.where(is_f, jax.nn.log_sigmoid(gates + bf_ref[...]), 0.0)
        cum_f = _cumsum_rows(tril, log_f)
        z_seq.append(jnp.where(is_f, cum_f, jnp.where(lane < N_HEADS, gates, 0.0)))
        zt_seq.append(z_seq[s].T)
        n_seq.append(n_ref[s])
        m_seq.append(m_ref[s])
    computed = {}

    def act(group):
        source = 5 if group >= KB_GROUP else group
        if late_proj is not None and source >= N_EARLY_GROUPS:
            if source not in computed:
                x16_ref, w_ref, bias_ref = late_proj
                cols = slice(source * D_GROUP, (source + 1) * D_GROUP)
                computed[source] = _activate(
                    source, _dot(x16_ref[...], w_ref[:, cols]) + bias_ref[:, cols], lb_ref[...])
            return [computed[source][group][rs(s), h * D_HEAD:(h + 1) * D_HEAD] for s, h in units]
        if activated:
            return [p_ref[rs(s), _act_cols(group, h)] for s, h in units]
        for s, h in units:
            if (source, s, h) not in computed:
                computed[source, s, h] = _activate(source, p_ref[rs(s), hd(source, h)],
                                                   lb_ref[:, hd(0, h)])
        return [computed[source, s, h][group] for s, h in units]

    lane_row = lane[:1, :]
    m0_rows = []
    for s in range(n_seq_blk):
        m0_row = jnp.zeros((1, LANES), F32)
        for h in range(N_HEADS):
            m0_row = jnp.where(lane_row == N_HEADS + h, m_seq[s][:, h:h + 1], m0_row)
        m0_rows.append(m0_row)
    mlstm = _mlstm_units(
        q=act(0), k=act(1), v=act(2), gate=act(3), z=z_seq, zt=zt_seq, m0_row=m0_rows,
        c_old=[c_ref[s, h] for s, h in units], n_old=[n_seq[s][h:h + 1, :] for s, h in units],
        g_norm=[ga_ref[:, hd(0, h)] for s, h in units], causal=causal, lane=lane, t_len=t_len)
    hgrn = _hgrn_units(
        qb=act(4), log_f=act(5), f=act(F_GROUP), kb=act(KB_GROUP), iv=act(6), gate=act(7),
        s_old_t=[s_ref[s, h] for s, h in units],
        g_norm=[gb_ref[:, hd(0, h)] for s, h in units], tril=tril, level_of=level_of, t_len=t_len)
    gens, per_round = [mlstm, hgrn], [1, 2]
    if side_stages is not None:
        gens, per_round = gens + [side_stages], per_round + [SIDE_STAGES_PER_ROUND]
    results = _run_interleaved(gens, per_round)
    (outs, c_new, n_new, m_new), (outs_b, s_new_t) = results[:2]

    for i, (s, h) in enumerate(units):
        c_ref[s, h] = c_new[i]
        s_ref[s, h] = s_new_t[i]
    head_lane = lax.broadcasted_iota(jnp.int32, (1, N_HEADS), 1)
    for s in range(n_seq_blk):
        n_ref[s] = jnp.concatenate(n_new[s * N_HEADS:(s + 1) * N_HEADS], axis=0)
        m_row = m_seq[s]
        for h in range(N_HEADS):
            m_row = jnp.where(head_lane == h, m_new[s * N_HEADS + h], m_row)
        m_ref[s] = m_row
    return outs, outs_b


def _mixer_kernel(p_ref, c0_ref, n0_ref, m0_ref, s0_ref, bf_ref, ga_ref, gb_ref, lb_ref,
                  mix_ref, c_ref, n_ref, m_ref, s_ref, *, t_len, n_chunks, n_seq_blk):
    chunk = pl.program_id(1)

    @pl.when(chunk == 0)
    def _():
        _load_state(c0_ref, n0_ref, m0_ref, s0_ref, c_ref, n_ref, m_ref, s_ref, n_seq_blk)

    outs_a, outs_b = _mixer_body(p_ref, bf_ref, ga_ref, gb_ref, lb_ref, c_ref, n_ref, m_ref, s_ref,
                                 t_len=t_len, n_seq_blk=n_seq_blk, activated=False)
    for i in range(n_seq_blk * N_HEADS):
        s, h = divmod(i, N_HEADS)
        rows = slice(s * t_len, (s + 1) * t_len)
        mix_ref[rows, h * D_HEAD:(h + 1) * D_HEAD] = outs_a[i].astype(mix_ref.dtype)
        mix_ref[rows, D_GROUP + h * D_HEAD:D_GROUP + (h + 1) * D_HEAD] = outs_b[i].astype(mix_ref.dtype)

    @pl.when(chunk == n_chunks - 1)
    def _():
        _finish_state(s_ref, n_seq_blk)


def _mixer(proj, c0, n0, m0, s0, bf_row, ga, gb, lb, *, n_seq, n_chunks, t_len, n_seq_blk,
           shared_init):
    assert not shared_init or n_seq_blk == 1
    assert n_chunks == 1 or n_seq_blk == 1
    nb = n_seq_blk
    init = (lambda b, c: (0, 0, 0, 0)) if shared_init else (lambda b, c: (b, 0, 0, 0))
    init3 = (lambda b, c: (0, 0, 0)) if shared_init else (lambda b, c: (b, 0, 0))
    const = lambda b, c: (0, 0)
    state4 = pl.BlockSpec((nb, N_HEADS, D_HEAD, D_HEAD), lambda b, c: (b, 0, 0, 0))
    return pl.pallas_call(
        functools.partial(_mixer_kernel, t_len=t_len, n_chunks=n_chunks, n_seq_blk=nb),
        grid=(n_seq // nb, n_chunks),
        in_specs=[
            pl.BlockSpec((nb * t_len, D_PROJ), lambda b, c: (b * n_chunks + c, 0)),
            pl.BlockSpec((nb, N_HEADS, D_HEAD, D_HEAD), init),
            pl.BlockSpec((nb, N_HEADS, D_HEAD), init3),
            pl.BlockSpec((nb, 1, N_HEADS), init3),
            pl.BlockSpec((nb, N_HEADS, D_HEAD, D_HEAD), init),
            pl.BlockSpec((1, LANES), const),
            pl.BlockSpec((1, D_GROUP), const),
            pl.BlockSpec((1, D_GROUP), const),
            pl.BlockSpec((1, D_GROUP), const),
        ],
        out_specs=[
            pl.BlockSpec((nb * t_len, D_MODEL), lambda b, c: (b * n_chunks + c, 0)),
            state4,
            pl.BlockSpec((nb, N_HEADS, D_HEAD), lambda b, c: (b, 0, 0)),
            pl.BlockSpec((nb, 1, N_HEADS), lambda b, c: (b, 0, 0)),
            state4,
        ],
        out_shape=[
            jax.ShapeDtypeStruct((n_seq * n_chunks * t_len, D_MODEL), BF16),
            jax.ShapeDtypeStruct((n_seq, N_HEADS, D_HEAD, D_HEAD), F32),
            jax.ShapeDtypeStruct((n_seq, N_HEADS, D_HEAD), F32),
            jax.ShapeDtypeStruct((n_seq, 1, N_HEADS), F32),
            jax.ShapeDtypeStruct((n_seq, N_HEADS, D_HEAD, D_HEAD), F32),
        ],
        compiler_params=pltpu.CompilerParams(
            dimension_semantics=("arbitrary", "arbitrary"), vmem_limit_bytes=VMEM_LIMIT),
        name=f"mixer_t{t_len}",
    )(proj, c0, n0, m0, s0, bf_row, ga, gb, lb)


def _meta_kernel(x_ref, ge_ref, be_ref, win_ref, bin_ref, bf_ref, ga_ref, gb_ref, lb_ref,
                 wout_ref, bout_ref, g1_ref, b1_ref, wu_ref, bu_ref,
                 c_ref, n_ref, m_ref, s_ref, conv_ref, proj_scr):
    t_len = x_ref.shape[0]
    xn = _layer_norm(x_ref[...], ge_ref[...], be_ref[...])
    proj_scr[...] = _dot(xn.astype(BF16), win_ref[...]) + bin_ref[...]
    for ref in (c_ref, n_ref, m_ref, s_ref):
        ref[...] = jnp.zeros(ref.shape, ref.dtype)
    outs_a, outs_b = _mixer_body(proj_scr, bf_ref, ga_ref, gb_ref, lb_ref, c_ref, n_ref, m_ref, s_ref,
                                 t_len=t_len, n_seq_blk=1, activated=False)
    _finish_state(s_ref, 1)
    mix = jnp.concatenate([o.astype(BF16) for o in outs_a + outs_b], axis=1)
    y = _dot(mix, wout_ref[...]) + bout_ref[...]
    x1 = _layer_norm(ALPHA * xn + y, g1_ref[...], b1_ref[...])
    u = _dot(x1.astype(BF16), wu_ref[...]) + bu_ref[...]
    conv_ref[0] = u[t_len - (CONV_W - 1):, :]


def _meta_state(x, ln_e_g, ln_e_b, w_in, b_in, bf_row, ga, gb, lb, w_out, b_out, ln_g, ln_b,
                w_up, b_up):
    t_len = x.shape[0]
    full = lambda shape: _resident(shape, lambda i: (0,) * len(shape))
    vec, grp = full((1, D_MODEL)), full((1, D_GROUP))
    state4 = (1, N_HEADS, D_HEAD, D_HEAD)
    return pl.pallas_call(
        _meta_kernel,
        grid=(1,),
        in_specs=[full((t_len, D_MODEL)), vec, vec, full((D_MODEL, D_PROJ)), full((1, D_PROJ)),
                  full((1, LANES)), grp, grp, grp, full((D_MODEL, D_MODEL)), vec, vec, vec,
                  full((D_MODEL, D_FF)), full((1, D_FF))],
        out_specs=[pl.BlockSpec(shape, lambda i, rank=len(shape): (0,) * rank)
                   for shape in (state4, (1, N_HEADS, D_HEAD), (1, 1, N_HEADS), state4,
                                 (1, CONV_W - 1, D_FF))],
        out_shape=[
            jax.ShapeDtypeStruct(state4, F32),
            jax.ShapeDtypeStruct((1, N_HEADS, D_HEAD), F32),
            jax.ShapeDtypeStruct((1, 1, N_HEADS), F32),
            jax.ShapeDtypeStruct(state4, F32),
            jax.ShapeDtypeStruct((1, CONV_W - 1, D_FF), F32),
        ],
        scratch_shapes=[pltpu.VMEM((t_len, D_PROJ), F32)],
        compiler_params=pltpu.CompilerParams(
            dimension_semantics=("arbitrary",), vmem_limit_bytes=VMEM_LIMIT),
        name="meta_state",
    )(x, ln_e_g, ln_e_b, w_in, b_in, bf_row, ga, gb, lb, w_out, b_out, ln_g, ln_b, w_up, b_up)


def _in_proj_stages(x_ref, g_ref, b_ref, w_ref, bias_ref, lb_ref, act_ref, xn_ref, x16_ref):
    assert D_GROUP % IN_PROJ_STAGE_COLS == 0
    xn = _layer_norm(x_ref[...].reshape(xn_ref.shape), g_ref[...], b_ref[...])
    xn_ref[...] = xn
    x16 = xn.astype(BF16)
    x16_ref[...] = x16
    starts = list(range(0, N_EARLY_GROUPS * D_GROUP, IN_PROJ_STAGE_COLS)) + [GATE_COL]
    for lo in starts:
        hi = min(lo + IN_PROJ_STAGE_COLS, D_PROJ)
        yield
        block = _dot(x16, w_ref[:, lo:hi]) + bias_ref[:, lo:hi]
        group, off = divmod(lo, D_GROUP)
        if lo >= GATE_COL:
            act_ref[:, lo:hi] = block
            continue
        for dst, val in _activate(group, block, lb_ref[:, off:off + hi - lo]).items():
            base = _act_cols(dst, 0).start + off
            act_ref[:, base:base + hi - lo] = val


def _prompt_kernel(x0_ref, xnext_ref, ge_ref, be_ref, win_ref, bin_ref, c0_ref, n0_ref, m0_ref, s0_ref,
                   bf_ref, ga_ref, gb_ref, lb_ref, wout_ref, bout_ref, g1_ref, b1_ref,
                   x1_ref, c_ref, n_ref, m_ref, s_ref,
                   proj_scr, xn_scr, x16_scr, proj_alt, xn_alt, x16_alt,
                   *, t_len, n_chunks, n_seq_blk):
    chunk = pl.program_id(1)
    step = pl.program_id(0) * n_chunks + chunk

    @pl.when(step == 0)
    def _():
        first = _in_proj_stages(x0_ref, ge_ref, be_ref, win_ref, bin_ref, lb_ref,
                                proj_scr, xn_scr, x16_scr)
        _run_interleaved([first], [1])

    @pl.when(chunk == 0)
    def _():
        _load_state(c0_ref, n0_ref, m0_ref, s0_ref, c_ref, n_ref, m_ref, s_ref, n_seq_blk)

    def tile(proj_cur, xn_cur, x16_cur, proj_next, xn_next, x16_next):
        next_proj = _in_proj_stages(xnext_ref, ge_ref, be_ref, win_ref, bin_ref, lb_ref,
                                    proj_next, xn_next, x16_next)
        outs_a, outs_b = _mixer_body(
            proj_cur, bf_ref, ga_ref, gb_ref, lb_ref, c_ref, n_ref, m_ref, s_ref,
            t_len=t_len, n_seq_blk=n_seq_blk, activated=True, side_stages=next_proj,
            late_proj=(x16_cur, win_ref, bin_ref))
        mix = jnp.concatenate(
            [jnp.concatenate([o.astype(BF16) for o in outs_a[s * N_HEADS:(s + 1) * N_HEADS]
                              + outs_b[s * N_HEADS:(s + 1) * N_HEADS]], axis=1)
             for s in range(n_seq_blk)], axis=0)
        y = _dot(mix, wout_ref[...]) + bout_ref[...]
        x1 = _layer_norm(ALPHA * xn_cur[...] + y, g1_ref[...], b1_ref[...])
        x1_ref[...] = x1.reshape(x1_ref.shape)

    @pl.when(step % 2 == 0)
    def _():
        tile(proj_scr, xn_scr, x16_scr, proj_alt, xn_alt, x16_alt)

    @pl.when(step % 2 == 1)
    def _():
        tile(proj_alt, xn_alt, x16_alt, proj_scr, xn_scr, x16_scr)

    @pl.when(chunk == n_chunks - 1)
    def _():
        _finish_state(s_ref, n_seq_blk)


def _prompt_mixer(x, ln_e_g, ln_e_b, w_in, b_in, c0, n0, m0, s0, bf_row, ga, gb, lb,
                  w_out, b_out, ln_g, ln_b, *, n_chunks, t_len, n_seq_blk):
    n_seq = x.shape[0]
    nb = n_seq_blk
    n_tiles = (n_seq // nb) * n_chunks
    const = lambda b, c: (0, 0)
    init4 = lambda b, c: (0, 0, 0, 0)
    init3 = lambda b, c: (0, 0, 0)
    vec = pl.BlockSpec((1, D_MODEL), const)
    grp = pl.BlockSpec((1, D_GROUP), const)
    state4 = pl.BlockSpec((nb, N_HEADS, D_HEAD, D_HEAD), lambda b, c: (b, 0, 0, 0))

    def next_tile(b, c):
        nxt = jnp.minimum(b * n_chunks + c + 1, n_tiles - 1)
        return (nxt // n_chunks, nxt % n_chunks, 0)

    return pl.pallas_call(
        functools.partial(_prompt_kernel, t_len=t_len, n_chunks=n_chunks, n_seq_blk=nb),
        grid=(n_seq // nb, n_chunks),
        in_specs=[
            _resident((nb, t_len, D_MODEL), init3),
            pl.BlockSpec((nb, t_len, D_MODEL), next_tile),
            vec, vec,
            _resident((D_MODEL, D_PROJ), const),
            pl.BlockSpec((1, D_PROJ), const),
            pl.BlockSpec((1, N_HEADS, D_HEAD, D_HEAD), init4),
            pl.BlockSpec((1, N_HEADS, D_HEAD), init3),
            pl.BlockSpec((1, 1, N_HEADS), init3),
            pl.BlockSpec((1, N_HEADS, D_HEAD, D_HEAD), init4),
            pl.BlockSpec((1, LANES), const),
            grp, grp, grp,
            _resident((D_MODEL, D_MODEL), const),
            vec, vec, vec,
        ],
        out_specs=[
            pl.BlockSpec((nb, t_len, D_MODEL), lambda b, c: (b, c, 0)),
            state4,
            pl.BlockSpec((nb, N_HEADS, D_HEAD), lambda b, c: (b, 0, 0)),
            pl.BlockSpec((nb, 1, N_HEADS), lambda b, c: (b, 0, 0)),
            state4,
        ],
        out_shape=[
            jax.ShapeDtypeStruct(x.shape, F32),
            jax.ShapeDtypeStruct((n_seq, N_HEADS, D_HEAD, D_HEAD), F32),
            jax.ShapeDtypeStruct((n_seq, N_HEADS, D_HEAD), F32),
            jax.ShapeDtypeStruct((n_seq, 1, N_HEADS), F32),
            jax.ShapeDtypeStruct((n_seq, N_HEADS, D_HEAD, D_HEAD), F32),
        ],
        scratch_shapes=2 * [pltpu.VMEM((nb * t_len, D_ACT), F32), pltpu.VMEM((nb * t_len, D_MODEL), F32),
                            pltpu.VMEM((nb * t_len, D_MODEL), BF16)],
        compiler_params=pltpu.CompilerParams(
            dimension_semantics=("arbitrary", "arbitrary"), vmem_limit_bytes=VMEM_LIMIT),
        name="prompt_mixer",
    )(x, x, ln_e_g, ln_e_b, w_in, b_in, c0, n0, m0, s0, bf_row, ga, gb, lb, w_out, b_out, ln_g, ln_b)


def _ffn_kernel(x_ref, *refs, n_seq_blk, t_len):
    _ffn_tile(x_ref[...], *refs, n_seq_blk=n_seq_blk, t_len=t_len)


def _out_proj_ffn_kernel(x_ref, mix_ref, ge_ref, be_ref, wo_ref, bo_ref, g1_ref, b1_ref, *refs,
                         n_seq_blk, t_len):
    xn = _layer_norm(x_ref[...], ge_ref[...], be_ref[...])
    y = _dot(mix_ref[...], wo_ref[...]) + bo_ref[...]
    x1 = _layer_norm(ALPHA * xn + y, g1_ref[...], b1_ref[...])
    _ffn_tile(x1, *refs, n_seq_blk=n_seq_blk, t_len=t_len)


def _ffn_tile(x, cs_ref, wu_ref, bu_ref, wc_ref, bc_ref, wd_ref, bd_ref, g_ref, b_ref,
              y_ref, nc_ref, full_ref, *, n_seq_blk, t_len):
    hist = SUBLANES - (CONV_W - 1)

    @pl.when(pl.program_id(1) == 0)
    def _():
        full_ref[:, hist:SUBLANES, :] = cs_ref[...]

    up = _dot(x.astype(BF16), wu_ref[...]) + bu_ref[...]
    u = up[:, :D_FF].reshape(n_seq_blk, t_len, D_FF)
    gate = up[:, D_FF:].reshape(n_seq_blk, t_len, D_FF)
    full_ref[:, SUBLANES:SUBLANES + t_len, :] = u
    conv = bc_ref[...] + u * wc_ref[CONV_W - 1:CONV_W, :]
    for j in range(CONV_W - 1):
        conv = conv + full_ref[:, hist + j:hist + j + t_len, :] * wc_ref[j:j + 1, :]
    last = full_ref[:, hist + t_len:SUBLANES + t_len, :]
    nc_ref[...] = last
    full_ref[:, hist:SUBLANES, :] = last
    act = (conv * _sigmoid(conv) * gate).reshape(n_seq_blk * t_len, D_FF)
    ffn = _dot(act.astype(BF16), wd_ref[...]) + bd_ref[...]
    y_ref[...] = _layer_norm(ALPHA * x + ffn, g_ref[...], b_ref[...])


def _ffn(x, conv_state, w_up, b_up, w_conv, b_conv, w_down, b_down, ln_g, ln_b,
         *, n_seq, seq_len, n_seq_blk, t_len, shared_init, out_proj=None):
    n_t = seq_len // t_len
    rows = n_seq_blk * t_len
    const = lambda s, t: (0, 0)
    cs_map = (lambda s, t: (0, 0, 0)) if shared_init else (lambda s, t: (s, 0, 0))
    row = pl.BlockSpec((rows, D_MODEL), lambda s, t: (s * n_t + t, 0))
    vec = pl.BlockSpec((1, D_MODEL), const)
    body, lead_specs, lead_args = _ffn_kernel, [row], (x,)
    if out_proj is not None:
        body = _out_proj_ffn_kernel
        lead_specs = [row, row, vec, vec, _resident((D_MODEL, D_MODEL), const), vec, vec, vec]
        lead_args = (x,) + tuple(out_proj)
    return pl.pallas_call(
        functools.partial(body, n_seq_blk=n_seq_blk, t_len=t_len),
        grid=(n_seq // n_seq_blk, n_t),
        in_specs=lead_specs + [
            pl.BlockSpec((n_seq_blk, CONV_W - 1, D_FF), cs_map),
            _resident((D_MODEL, 2 * D_FF), const),
            pl.BlockSpec((1, 2 * D_FF), const),
            pl.BlockSpec((CONV_W, D_FF), const),
            pl.BlockSpec((1, D_FF), const),
            _resident((D_FF, D_MODEL), const),
            vec, vec, vec,
        ],
        out_specs=[row, pl.BlockSpec((n_seq_blk, CONV_W - 1, D_FF), lambda s, t: (s, 0, 0))],
        out_shape=[
            jax.ShapeDtypeStruct((n_seq * seq_len, D_MODEL), F32),
            jax.ShapeDtypeStruct((n_seq, CONV_W - 1, D_FF), F32),
        ],
        scratch_shapes=[pltpu.VMEM((n_seq_blk, SUBLANES + t_len, D_FF), F32)],
        compiler_params=pltpu.CompilerParams(
            dimension_semantics=("arbitrary", "arbitrary"), vmem_limit_bytes=VMEM_LIMIT),
        name=f"ffn_t{t_len}",
    )(*lead_args, conv_state, w_up, b_up, w_conv, b_conv, w_down, b_down, ln_g, ln_b)


def kernel(x_prompt, x_sample, state_mlstm_C, state_mlstm_n, state_mlstm_m, state_hgrn_S, state_ffn_conv, meta_tokens, ln_emb_g, ln_emb_b, w_in, b_in, b_fgate_a, g_norm_a, g_norm_b, hgrn_lb_logits, w_out, b_out, ln1_g, ln1_b, w_up, b_up, w_conv, b_conv, w_down, b_down, ln2_g, ln2_b):
    assert w_in.shape[0] == DEPTH == 1
    n_prompt, seq, _ = x_prompt.shape
    n_sample, dec_seq, _ = x_sample.shape
    row = lambda v: v.reshape(1, -1).astype(F32)

    gate0 = 4 * D_GROUP
    gate1 = gate0 + 2 * N_HEADS
    pad = D_PROJ - w_in.shape[2]
    w_in_p = _regroup_in_proj_weight(w_in[0].T, tm=256)
    b_in_p = jnp.concatenate(
        [b_in[0][:gate0], b_in[0][gate1:], b_in[0][gate0:gate1], jnp.zeros((pad,), b_in.dtype)]
    ).reshape(1, D_PROJ).astype(F32)
    bf_row = jnp.zeros((1, LANES), F32).at[0, N_HEADS:2 * N_HEADS].set(b_fgate_a[0].astype(F32))
    lb = jnp.cumsum(jax.nn.softmax(hgrn_lb_logits.astype(F32), axis=0), axis=0)[0].reshape(1, D_GROUP)
    ga, gb = row(g_norm_a[0]), row(g_norm_b[0])
    ln_e = (row(ln_emb_g), row(ln_emb_b))
    out_p = (w_out[0].astype(BF16), row(b_out[0]), row(ln1_g[0]), row(ln1_b[0]))
    ffn_p = (w_up[0].astype(BF16), row(b_up[0]), w_conv[0].astype(F32), row(b_conv[0]),
             w_down[0].astype(BF16), row(b_down[0]), row(ln2_g[0]), row(ln2_b[0]))

    c_m, n_m, m_m, s_m, conv_m = _meta_state(
        meta_tokens.astype(F32), *ln_e, w_in_p, b_in_p, bf_row, ga, gb, lb, *out_p, ffn_p[0], ffn_p[1])

    x1_p, c_p, n_p, m_p, s_p = _prompt_mixer(
        x_prompt.astype(F32), *ln_e, w_in_p, b_in_p, c_m, n_m, m_m, s_m, bf_row, ga, gb, lb, *out_p,
        n_chunks=seq // PROMPT_CHUNK, t_len=PROMPT_CHUNK, n_seq_blk=PROMPT_SEQS_PER_STEP)
    y_p, conv_p = _ffn(x1_p.reshape(n_prompt * seq, D_MODEL), conv_m, *ffn_p, n_seq=n_prompt,
                       seq_len=seq, n_seq_blk=1, t_len=512, shared_init=True)

    sample_state = (state_mlstm_C[0].astype(F32), state_mlstm_n[0].astype(F32),
                    state_mlstm_m[0].astype(F32).reshape(n_sample, 1, N_HEADS),
                    state_hgrn_S[0].astype(F32))
    xs_rows = x_sample.reshape(n_sample * dec_seq, D_MODEL).astype(F32)
    proj_s = _in_proj(xs_rows, *ln_e, w_in_p, b_in_p, tm=256)
    mix_s, c_s, n_s, m_s, s_s = _mixer(
        proj_s, *sample_state, bf_row, ga, gb, lb, n_seq=n_sample, n_chunks=1, t_len=dec_seq,
        n_seq_blk=8, shared_init=False)
    y_s, conv_s = _ffn(xs_rows, state_ffn_conv[0].astype(F32), *ffn_p, n_seq=n_sample,
                       seq_len=dec_seq, n_seq_blk=32, t_len=dec_seq, shared_init=False,
                       out_proj=(mix_s, *ln_e, *out_p))

    lead = lambda v: v[None]
    return (y_p.reshape(n_prompt, seq, D_MODEL), y_s.reshape(n_sample, dec_seq, D_MODEL),
            lead(c_p), lead(n_p), lead(m_p.reshape(n_prompt, N_HEADS)), lead(s_p), lead(conv_p),
            lead(c_s), lead(n_s), lead(m_s.reshape(n_sample, N_HEADS)), lead(s_s), lead(conv_s))
```

```python
import functools

import jax
import jax.numpy as jnp
from jax import lax
from jax.experimental import pallas as pl
from jax.experimental.pallas import tpu as pltpu

D_MODEL = 1024
N_META = 16
N_HEADS = 4
D_HEAD = 128
D_GROUP = N_HEADS * D_HEAD
D_FF = 2816
CONV_W = 3
DEPTH = 1
ALPHA = (2.0 * DEPTH) ** 0.25
LN_EPS = 1e-5
RMS_EPS = 1e-6
NEG_LOG2_E = -1.4426950408889634

LANES = 128
SUBLANES = 8
GATE_COL = 8 * D_GROUP
D_PROJ = GATE_COL + LANES
KB_GROUP, F_GROUP = 8, 9
D_ACT = D_PROJ + 2 * D_GROUP
LATE_GROUPS = (3, 6, 7)
LATE_STAGES_PER_ROUND = 2
IN_PROJ_STAGE_COLS = 256
SIDE_STAGES_PER_ROUND = 3
PROMPT_SEQS_PER_STEP = 2
PROMPT_CHUNK = 128
VMEM_LIMIT = 56 * 1024 * 1024

F32 = jnp.float32
BF16 = jnp.bfloat16
NT_DIMS = (((1,), (1,)), ((), ()))
TN_DIMS = (((0,), (0,)), ((), ()))


def _layer_norm(x, g, b):
    mu = jnp.mean(x, axis=-1, keepdims=True)
    xc = x - mu
    var = jnp.mean(xc * xc, axis=-1, keepdims=True)
    return xc * lax.rsqrt(var + LN_EPS) * g + b


def _exp_neg(x):
    return jnp.exp2(x * NEG_LOG2_E)


def _sigmoid(x):
    return 1.0 / (1.0 + _exp_neg(x))


def _resident(block_shape, index_map):
    return pl.BlockSpec(block_shape, index_map, pipeline_mode=pl.Buffered(1))


def _dot(a, b):
    return jnp.dot(a, b, preferred_element_type=F32)


def _dot_nt(a, b):
    return lax.dot_general(a, b, NT_DIMS, preferred_element_type=F32)


def _dot_tn(a, b):
    return lax.dot_general(a, b, TN_DIMS, preferred_element_type=F32)


def _regroup_kernel(wt_ref, o_ref):
    gate0 = 4 * D_GROUP
    gate1 = gate0 + 2 * N_HEADS
    for j in range(GATE_COL // LANES):
        src = j * LANES if j * LANES < gate0 else j * LANES + (gate1 - gate0)
        o_ref[:, j * LANES:(j + 1) * LANES] = wt_ref[src:src + LANES, :].T.astype(o_ref.dtype)
    gates = wt_ref[gate0:gate1, :].T.astype(o_ref.dtype)
    o_ref[:, GATE_COL:] = jnp.concatenate(
        [gates, jnp.zeros((gates.shape[0], LANES - gates.shape[1]), o_ref.dtype)], axis=1)


def _regroup_in_proj_weight(w_t, *, tm):
    cols, n = w_t.shape
    return pl.pallas_call(
        _regroup_kernel,
        grid=(n // tm,),
        in_specs=[pl.BlockSpec((cols, tm), lambda i: (0, i))],
        out_specs=pl.BlockSpec((tm, D_PROJ), lambda i: (i, 0)),
        out_shape=jax.ShapeDtypeStruct((n, D_PROJ), BF16),
        compiler_params=pltpu.CompilerParams(dimension_semantics=("arbitrary",)),
        name="regroup_w_in",
    )(w_t)


def _in_proj_kernel(x_ref, g_ref, b_ref, w_ref, bias_ref, o_ref):
    xn = _layer_norm(x_ref[...], g_ref[...], b_ref[...])
    o_ref[...] = _dot(xn.astype(BF16), w_ref[...]) + bias_ref[...]


def _in_proj(x, ln_g, ln_b, w, bias, *, tm):
    n = x.shape[0]
    const = lambda i: (0, 0)
    return pl.pallas_call(
        _in_proj_kernel,
        grid=(n // tm,),
        in_specs=[
            pl.BlockSpec((tm, D_MODEL), lambda i: (i, 0)),
            pl.BlockSpec((1, D_MODEL), const),
            pl.BlockSpec((1, D_MODEL), const),
            pl.BlockSpec((D_MODEL, D_PROJ), const),
            pl.BlockSpec((1, D_PROJ), const),
        ],
        out_specs=pl.BlockSpec((tm, D_PROJ), lambda i: (i, 0)),
        out_shape=jax.ShapeDtypeStruct((n, D_PROJ), F32),
        compiler_params=pltpu.CompilerParams(
            dimension_semantics=("arbitrary",), vmem_limit_bytes=VMEM_LIMIT),
        name="in_proj",
    )(x, ln_g, ln_b, w, bias)


def _block_rows(x, level, t_len, row_in_block):
    size = 2 << level
    if size > SUBLANES:
        pieces = [jnp.broadcast_to(x[j * size + row_in_block:j * size + row_in_block + 1, :],
                                   (size, x.shape[1])) for j in range(t_len // size)]
        return pieces[0] if len(pieces) == 1 else jnp.concatenate(pieces, axis=0)
    x3 = x.reshape(t_len // SUBLANES, SUBLANES, x.shape[1])
    sub = lax.broadcasted_iota(jnp.int32, x3.shape, 1)
    out = None
    for j in range(SUBLANES // size):
        row = jnp.broadcast_to(x3[:, j * size + row_in_block:j * size + row_in_block + 1, :], x3.shape)
        out = row if out is None else jnp.where(sub >= j * size, row, out)
    return out.reshape(x.shape)


def _interleave_halves(lower, upper, level, t_len):
    half = 1 << level
    if half >= SUBLANES:
        pieces = []
        for j in range(t_len // (2 * half)):
            pieces.append(lower[2 * half * j:2 * half * j + half])
            pieces.append(upper[2 * half * j + half:2 * half * (j + 1)])
        return jnp.concatenate(pieces, axis=0)
    rows = lax.broadcasted_iota(jnp.int32, lower.shape, 0)
    return jnp.where((rows & half) != 0, upper, lower)


def _run_interleaved(gens, stages_per_round):
    results = [None] * len(gens)
    live = [True] * len(gens)
    while any(live):
        for g, steps in enumerate(stages_per_round):
            for _ in range(steps):
                if live[g]:
                    try:
                        next(gens[g])
                    except StopIteration as stop:
                        results[g], live[g] = stop.value, False
    return results


def _cumsum_rows(tril16, x):
    hi = x.astype(BF16)
    rest = x - hi.astype(F32)
    mid = rest.astype(BF16)
    lo = (rest - mid.astype(F32)).astype(BF16)
    return _dot(tril16, hi) + _dot(tril16, mid) + _dot(tril16, lo)


def _activate(group, x, lb=None):
    if group == 1:
        return {1: x * (D_HEAD ** -0.5)}
    if group in (3, 7):
        return {group: _sigmoid(x)}
    if group == 4:
        return {4: x * _sigmoid(x)}
    if group == 5:
        f = lb + (1.0 - lb) * _sigmoid(x)
        return {5: jnp.log(f), KB_GROUP: (1.0 - lb) / (1.0 + jnp.exp(x)), F_GROUP: f}
    return {group: x}


def _now(value):
    return value() if callable(value) else value


def _late_proj_stages(x16_ref, w_ref, bias_ref, lb_ref, out):
    for group in LATE_GROUPS:
        for off in range(0, D_GROUP, IN_PROJ_STAGE_COLS):
            yield
            cols = slice(group * D_GROUP + off, group * D_GROUP + off + IN_PROJ_STAGE_COLS)
            block = _dot(x16_ref[...], w_ref[:, cols]) + bias_ref[:, cols]
            out[group, off] = _activate(group, block, lb_ref[:, off:off + IN_PROJ_STAGE_COLS])[group]


def _mlstm_units(*, q, k, v, gate, z, zt, m0_row, c_old, n_old, g_norm, causal, lane, t_len):
    idx = range(len(q))
    seq = [i // N_HEADS for i in idx]
    b_lane = [N_HEADS + i % N_HEADS for i in idx]
    q16 = [q[i].astype(BF16) for i in idx]
    k16 = [k[i].astype(BF16) for i in idx]
    qk = [_dot_nt(q16[i], k16[i]) for i in idx]
    qc = [_dot(q16[i], c_old[i].astype(BF16)) for i in idx]
    bs_row = [zt[seq[i]][b_lane[i]:b_lane[i] + 1, :] - zt[seq[i]][i % N_HEADS:i % N_HEADS + 1, :]
              for i in idx]
    yield
    col = lambda per_seq, i: per_seq[seq[i]][:, b_lane[i]:b_lane[i] + 1]
    d = [jnp.where(causal, col(z, i) - bs_row[i], -jnp.inf) for i in idx]
    row_max = [jnp.max(d[i], axis=1, keepdims=True) for i in idx]
    last = slice(t_len - 1, t_len)
    m_t_seq, dec_seq, floor_seq, w_last_seq = [], [], [], []
    for s in range(len(z)):
        r = z[s] + m0_row[s]
        d_max = jnp.full(r.shape, -jnp.inf, F32)
        for h in range(N_HEADS):
            d_max = jnp.where(lane == N_HEADS + h, row_max[s * N_HEADS + h], d_max)
        m_t = jnp.maximum(r, d_max)
        i_gate = pltpu.roll(z[s], N_HEADS, axis=1)
        m_t_seq.append(m_t)
        dec_seq.append(jnp.exp(r - m_t))
        floor_seq.append(_exp_neg(m_t))
        w_last_seq.append(jnp.exp(z[s][last] - z[s] + i_gate - m_t[last]))
    m_t = [col(m_t_seq, i) for i in idx]
    dec = [col(dec_seq, i) for i in idx]
    w_last = [col(w_last_seq, i) for i in idx]
    sw = [jnp.exp(d[i] - m_t[i]) * qk[i] for i in idx]
    yield
    swv = [_dot(sw[i].astype(BF16), v[i].astype(BF16)) for i in idx]
    kv = [_dot_tn(k16[i], (w_last[i] * v[i]).astype(BF16)) for i in idx]
    yield
    c_new = [dec[i][last] * c_old[i] + kv[i] for i in idx]
    n_new = [dec[i][last] * n_old[i] + jnp.sum(w_last[i] * k[i], axis=0, keepdims=True) for i in idx]
    m_new = [m_t[i][last] for i in idx]
    den = [dec[i] * jnp.sum(q[i] * n_old[i], axis=1, keepdims=True)
           + jnp.sum(sw[i], axis=1, keepdims=True) for i in idx]
    hid = [(dec[i] * qc[i] + swv[i]) / jnp.maximum(jnp.abs(den[i]), col(floor_seq, i)) for i in idx]
    yield
    rms = [lax.rsqrt(jnp.mean(hid[i] * hid[i], axis=1, keepdims=True) + RMS_EPS) for i in idx]
    out = [_now(gate[i]) * (hid[i] * rms[i] * g_norm[i]) for i in idx]
    return out, c_new, n_new, m_new


def _hgrn_units(*, qb, log_f, f, kb, iv, gate, s_old_t, g_norm, tril, level_of, t_len):
    idx = range(len(qb))
    n_levels = t_len.bit_length() - 1
    a = [_cumsum_rows(tril, log_f[i]) for i in idx]
    yield
    diag =[_dot_nt(qb[i].astype(BF16), kb[i].astype(BF16)) for i in idx]
    scores = [jnp.where(level_of == -2, diag[i], 0.0) for i in idx]
    for level in range(n_levels):
        yield
        x16 = []
        for i in idx:
            base = _interleave_halves(kb[i], qb[i], level, t_len)
            if level == 0:
                x = base * _interleave_halves(jnp.ones_like(f[i]), f[i], 0, t_len)
            else:
                ref = _block_rows(a[i], level, t_len, (1 << level) - 1)
                x = base * _exp_neg(jnp.abs(a[i] - ref))
            x16.append(x.astype(BF16))
        part = [_dot_nt(x16[i], x16[i]) for i in idx]
        scores = [jnp.where(level_of == level, part[i], scores[i]) for i in idx]
    yield
    last = slice(t_len - 1, t_len)
    q_in = [(qb[i] * jnp.exp(a[i])).astype(BF16) for i in idx]
    k_out = [(kb[i] * jnp.exp(a[i][last] - a[i])).astype(BF16) for i in idx]
    inter = [_dot_nt(q_in[i], s_old_t[i].astype(BF16)) for i in idx]
    iv16 = [_now(iv[i]).astype(BF16) for i in idx]
    intra = [_dot(scores[i].astype(BF16), iv16[i]) for i in idx]
    kv = [_dot_tn(iv16[i], k_out[i]) for i in idx]
    yield
    s_new_t = [jnp.exp(a[i][last]) * s_old_t[i] + kv[i] for i in idx]
    o = [inter[i] + intra[i] for i in idx]
    rms = [lax.rsqrt(jnp.mean(o[i] * o[i], axis=1, keepdims=True) + RMS_EPS) for i in idx]
    out = [_now(gate[i]) * (o[i] * rms[i] * g_norm[i]) for i in idx]
    return out, s_new_t


def _load_state(c0_ref, n0_ref, m0_ref, s0_ref, c_ref, n_ref, m_ref, s_ref, n_seq_blk):
    shared = c0_ref.shape[0] == 1 and n_seq_blk > 1
    for s in range(n_seq_blk):
        src = 0 if shared else s
        c_ref[s] = c0_ref[src]
        n_ref[s] = n0_ref[src]
        m_ref[s] = m0_ref[src]
        for h in range(N_HEADS):
            s_ref[s, h] = s0_ref[src, h].T


def _finish_state(s_ref, n_seq_blk):
    for s in range(n_seq_blk):
        for h in range(N_HEADS):
            s_ref[s, h] = s_ref[s, h].T


def _act_cols(group, h):
    base = group * D_GROUP if group < KB_GROUP else D_PROJ + (group - KB_GROUP) * D_GROUP
    return slice(base + h * D_HEAD, base + (h + 1) * D_HEAD)


def _mixer_body(p_ref, bf_ref, ga_ref, gb_ref, lb_ref, c_ref, n_ref, m_ref, s_ref,
                *, t_len, n_seq_blk, activated, side_stages=None, late_proj=None):
    rows = lax.broadcasted_iota(jnp.int32, (t_len, t_len), 0)
    cols = lax.broadcasted_iota(jnp.int32, (t_len, t_len), 1)
    causal = cols <= rows
    tril = causal.astype(BF16)
    level_of = jnp.where(rows > cols, 31 - lax.clz(rows ^ cols), jnp.where(rows == cols, -2, -1))
    lane = lax.broadcasted_iota(jnp.int32, (t_len, LANES), 1)
    is_f = (lane >= N_HEADS) & (lane < 2 * N_HEADS)
    hd = lambda j, h: slice(j * D_GROUP + h * D_HEAD, j * D_GROUP + (h + 1) * D_HEAD)

    units = [(s, h) for s in range(n_seq_blk) for h in range(N_HEADS)]
    rs = lambda s: slice(s * t_len, (s + 1) * t_len)

    z_seq, zt_seq, n_seq, m_seq = [], [], [], []
    for s in range(n_seq_blk):
        gates = p_ref[rs(s), GATE_COL:GATE_COL + LANES]
        log_f = jnp.where(is_f, jax.nn.log_sigmoid(gates + bf_ref[...]), 0.0)
        cum_f = _cumsum_rows(tril, log_f)
        z_seq.append(jnp.where(is_f, cum_f, jnp.where(lane < N_HEADS, gates, 0.0)))
        zt_seq.append(z_seq[s].T)
        n_seq.append(n_ref[s])
        m_seq.append(m_ref[s])
    computed, late_vals = {}, {}

    def act(group):
        source = 5 if group >= KB_GROUP else group
        if late_proj is not None and source in LATE_GROUPS:
            def thunk(s, h):
                off, col = divmod(h * D_HEAD, IN_PROJ_STAGE_COLS)
                return lambda: late_vals[source, off * IN_PROJ_STAGE_COLS][rs(s), col:col + D_HEAD]
            return [thunk(s, h) for s, h in units]
        if activated:
            return [p_ref[rs(s), _act_cols(group, h)] for s, h in units]
        for s, h in units:
            if (source, s, h) not in computed:
                computed[source, s, h] = _activate(source, p_ref[rs(s), hd(source, h)],
                                                   lb_ref[:, hd(0, h)])
        return [computed[source, s, h][group] for s, h in units]

    lane_row = lane[:1, :]
    m0_rows = []
    for s in range(n_seq_blk):
        m0_row = jnp.zeros((1, LANES), F32)
        for h in range(N_HEADS):
            m0_row = jnp.where(lane_row == N_HEADS + h, m_seq[s][:, h:h + 1], m0_row)
        m0_rows.append(m0_row)
    mlstm = _mlstm_units(
        q=act(0), k=act(1), v=act(2), gate=act(3), z=z_seq, zt=zt_seq, m0_row=m0_rows,
        c_old=[c_ref[s, h] for s, h in units], n_old=[n_seq[s][h:h + 1, :] for s, h in units],
        g_norm=[ga_ref[:, hd(0, h)] for s, h in units], causal=causal, lane=lane, t_len=t_len)
    hgrn = _hgrn_units(
        qb=act(4), log_f=act(5), f=act(F_GROUP), kb=act(KB_GROUP), iv=act(6), gate=act(7),
        s_old_t=[s_ref[s, h] for s, h in units],
        g_norm=[gb_ref[:, hd(0, h)] for s, h in units], tril=tril, level_of=level_of, t_len=t_len)
    gens, per_round = [mlstm, hgrn], [1, 2]
    if late_proj is not None:
        late = _late_proj_stages(*late_proj, lb_ref, late_vals)
        gens, per_round = [late] + gens, [LATE_STAGES_PER_ROUND] + per_round
    if side_stages is not None:
        gens, per_round = gens + [side_stages], per_round + [SIDE_STAGES_PER_ROUND]
    results = _run_interleaved(gens, per_round)
    outs, c_new, n_new, m_new = results[gens.index(mlstm)]
    outs_b, s_new_t = results[gens.index(hgrn)]

    for i, (s, h) in enumerate(units):
        c_ref[s, h] = c_new[i]
        s_ref[s, h] = s_new_t[i]
    head_lane = lax.broadcasted_iota(jnp.int32, (1, N_HEADS), 1)
    for s in range(n_seq_blk):
        n_ref[s] = jnp.concatenate(n_new[s * N_HEADS:(s + 1) * N_HEADS], axis=0)
        m_row = m_seq[s]
        for h in range(N_HEADS):
            m_row = jnp.where(head_lane == h, m_new[s * N_HEADS + h], m_row)
        m_ref[s] = m_row
    return outs, outs_b


def _mixer_kernel(p_ref, c0_ref, n0_ref, m0_ref, s0_ref, bf_ref, ga_ref, gb_ref, lb_ref,
                  mix_ref, c_ref, n_ref, m_ref, s_ref, *, t_len, n_chunks, n_seq_blk):
    chunk = pl.program_id(1)

    @pl.when(chunk == 0)
    def _():
        _load_state(c0_ref, n0_ref, m0_ref, s0_ref, c_ref, n_ref, m_ref, s_ref, n_seq_blk)

    outs_a, outs_b = _mixer_body(p_ref, bf_ref, ga_ref, gb_ref, lb_ref, c_ref, n_ref, m_ref, s_ref,
                                 t_len=t_len, n_seq_blk=n_seq_blk, activated=False)
    for i in range(n_seq_blk * N_HEADS):
        s, h = divmod(i, N_HEADS)
        rows = slice(s * t_len, (s + 1) * t_len)
        mix_ref[rows, h * D_HEAD:(h + 1) * D_HEAD] = outs_a[i].astype(mix_ref.dtype)
        mix_ref[rows, D_GROUP + h * D_HEAD:D_GROUP + (h + 1) * D_HEAD] = outs_b[i].astype(mix_ref.dtype)

    @pl.when(chunk == n_chunks - 1)
    def _():
        _finish_state(s_ref, n_seq_blk)


def _mixer(proj, c0, n0, m0, s0, bf_row, ga, gb, lb, *, n_seq, n_chunks, t_len, n_seq_blk,
           shared_init):
    assert not shared_init or n_seq_blk == 1
    assert n_chunks == 1 or n_seq_blk == 1
    nb = n_seq_blk
    init = (lambda b, c: (0, 0, 0, 0)) if shared_init else (lambda b, c: (b, 0, 0, 0))
    init3 = (lambda b, c: (0, 0, 0)) if shared_init else (lambda b, c: (b, 0, 0))
    const = lambda b, c: (0, 0)
    state4 = pl.BlockSpec((nb, N_HEADS, D_HEAD, D_HEAD), lambda b, c: (b, 0, 0, 0))
    return pl.pallas_call(
        functools.partial(_mixer_kernel, t_len=t_len, n_chunks=n_chunks, n_seq_blk=nb),
        grid=(n_seq // nb, n_chunks),
        in_specs=[
            pl.BlockSpec((nb * t_len, D_PROJ), lambda b, c: (b * n_chunks + c, 0)),
            pl.BlockSpec((nb, N_HEADS, D_HEAD, D_HEAD), init),
            pl.BlockSpec((nb, N_HEADS, D_HEAD), init3),
            pl.BlockSpec((nb, 1, N_HEADS), init3),
            pl.BlockSpec((nb, N_HEADS, D_HEAD, D_HEAD), init),
            pl.BlockSpec((1, LANES), const),
            pl.BlockSpec((1, D_GROUP), const),
            pl.BlockSpec((1, D_GROUP), const),
            pl.BlockSpec((1, D_GROUP), const),
        ],
        out_specs=[
            pl.BlockSpec((nb * t_len, D_MODEL), lambda b, c: (b * n_chunks + c, 0)),
            state4,
            pl.BlockSpec((nb, N_HEADS, D_HEAD), lambda b, c: (b, 0, 0)),
            pl.BlockSpec((nb, 1, N_HEADS), lambda b, c: (b, 0, 0)),
            state4,
        ],
        out_shape=[
            jax.ShapeDtypeStruct((n_seq * n_chunks * t_len, D_MODEL), BF16),
            jax.ShapeDtypeStruct((n_seq, N_HEADS, D_HEAD, D_HEAD), F32),
            jax.ShapeDtypeStruct((n_seq, N_HEADS, D_HEAD), F32),
            jax.ShapeDtypeStruct((n_seq, 1, N_HEADS), F32),
            jax.ShapeDtypeStruct((n_seq, N_HEADS, D_HEAD, D_HEAD), F32),
        ],
        compiler_params=pltpu.CompilerParams(
            dimension_semantics=("arbitrary", "arbitrary"), vmem_limit_bytes=VMEM_LIMIT),
        name=f"mixer_t{t_len}",
    )(proj, c0, n0, m0, s0, bf_row, ga, gb, lb)


def _meta_kernel(x_ref, ge_ref, be_ref, win_ref, bin_ref, bf_ref, ga_ref, gb_ref, lb_ref,
                 wout_ref, bout_ref, g1_ref, b1_ref, wu_ref, bu_ref,
                 c_ref, n_ref, m_ref, s_ref, conv_ref, proj_scr):
    t_len = x_ref.shape[0]
    xn = _layer_norm(x_ref[...], ge_ref[...], be_ref[...])
    proj_scr[...] = _dot(xn.astype(BF16), win_ref[...]) + bin_ref[...]
    for ref in (c_ref, n_ref, m_ref, s_ref):
        ref[...] = jnp.zeros(ref.shape, ref.dtype)
    outs_a, outs_b = _mixer_body(proj_scr, bf_ref, ga_ref, gb_ref, lb_ref, c_ref, n_ref, m_ref, s_ref,
                                 t_len=t_len, n_seq_blk=1, activated=False)
    _finish_state(s_ref, 1)
    mix = jnp.concatenate([o.astype(BF16) for o in outs_a + outs_b], axis=1)
    y = _dot(mix, wout_ref[...]) + bout_ref[...]
    x1 = _layer_norm(ALPHA * xn + y, g1_ref[...], b1_ref[...])
    u = _dot(x1.astype(BF16), wu_ref[...]) + bu_ref[...]
    conv_ref[0] = u[t_len - (CONV_W - 1):, :]


def _meta_state(x, ln_e_g, ln_e_b, w_in, b_in, bf_row, ga, gb, lb, w_out, b_out, ln_g, ln_b,
                w_up, b_up):
    t_len = x.shape[0]
    full = lambda shape: _resident(shape, lambda i: (0,) * len(shape))
    vec, grp = full((1, D_MODEL)), full((1, D_GROUP))
    state4 = (1, N_HEADS, D_HEAD, D_HEAD)
    return pl.pallas_call(
        _meta_kernel,
        grid=(1,),
        in_specs=[full((t_len, D_MODEL)), vec, vec, full((D_MODEL, D_PROJ)), full((1, D_PROJ)),
                  full((1, LANES)), grp, grp, grp, full((D_MODEL, D_MODEL)), vec, vec, vec,
                  full((D_MODEL, D_FF)), full((1, D_FF))],
        out_specs=[pl.BlockSpec(shape, lambda i, rank=len(shape): (0,) * rank)
                   for shape in (state4, (1, N_HEADS, D_HEAD), (1, 1, N_HEADS), state4,
                                 (1, CONV_W - 1, D_FF))],
        out_shape=[
            jax.ShapeDtypeStruct(state4, F32),
            jax.ShapeDtypeStruct((1, N_HEADS, D_HEAD), F32),
            jax.ShapeDtypeStruct((1, 1, N_HEADS), F32),
            jax.ShapeDtypeStruct(state4, F32),
            jax.ShapeDtypeStruct((1, CONV_W - 1, D_FF), F32),
        ],
        scratch_shapes=[pltpu.VMEM((t_len, D_PROJ), F32)],
        compiler_params=pltpu.CompilerParams(
            dimension_semantics=("arbitrary",), vmem_limit_bytes=VMEM_LIMIT),
        name="meta_state",
    )(x, ln_e_g, ln_e_b, w_in, b_in, bf_row, ga, gb, lb, w_out, b_out, ln_g, ln_b, w_up, b_up)


def _in_proj_stages(x_ref, g_ref, b_ref, w_ref, bias_ref, lb_ref, act_ref, xn_ref, x16_ref):
    assert D_GROUP % IN_PROJ_STAGE_COLS == 0
    xn = _layer_norm(x_ref[...].reshape(xn_ref.shape), g_ref[...], b_ref[...])
    xn_ref[...] = xn
    x16 = xn.astype(BF16)
    x16_ref[...] = x16
    starts = [lo for lo in range(0, GATE_COL, IN_PROJ_STAGE_COLS) if lo // D_GROUP not in LATE_GROUPS]
    for lo in starts + [GATE_COL]:
        hi = min(lo + IN_PROJ_STAGE_COLS, D_PROJ)
        yield
        block = _dot(x16, w_ref[:, lo:hi]) + bias_ref[:, lo:hi]
        group, off = divmod(lo, D_GROUP)
        if lo >= GATE_COL:
            act_ref[:, lo:hi] = block
            continue
        for dst, val in _activate(group, block, lb_ref[:, off:off + hi - lo]).items():
            base = _act_cols(dst, 0).start + off
            act_ref[:, base:base + hi - lo] = val


def _prompt_kernel(x0_ref, xnext_ref, ge_ref, be_ref, win_ref, bin_ref, c0_ref, n0_ref, m0_ref, s0_ref,
                   bf_ref, ga_ref, gb_ref, lb_ref, wout_ref, bout_ref, g1_ref, b1_ref,
                   x1_ref, c_ref, n_ref, m_ref, s_ref,
                   proj_scr, xn_scr, x16_scr, proj_alt, xn_alt, x16_alt,
                   *, t_len, n_chunks, n_seq_blk):
    chunk = pl.program_id(1)
    step = pl.program_id(0) * n_chunks + chunk

    @pl.when(step == 0)
    def _():
        first = _in_proj_stages(x0_ref, ge_ref, be_ref, win_ref, bin_ref, lb_ref,
                                proj_scr, xn_scr, x16_scr)
        _run_interleaved([first], [1])

    @pl.when(chunk == 0)
    def _():
        _load_state(c0_ref, n0_ref, m0_ref, s0_ref, c_ref, n_ref, m_ref, s_ref, n_seq_blk)

    def tile(proj_cur, xn_cur, x16_cur, proj_next, xn_next, x16_next):
        next_proj = _in_proj_stages(xnext_ref, ge_ref, be_ref, win_ref, bin_ref, lb_ref,
                                    proj_next, xn_next, x16_next)
        outs_a, outs_b = _mixer_body(
            proj_cur, bf_ref, ga_ref, gb_ref, lb_ref, c_ref, n_ref, m_ref, s_ref,
            t_len=t_len, n_seq_blk=n_seq_blk, activated=True, side_stages=next_proj,
            late_proj=(x16_cur, win_ref, bin_ref))
        mix = jnp.concatenate(
            [jnp.concatenate([o.astype(BF16) for o in outs_a[s * N_HEADS:(s + 1) * N_HEADS]
                              + outs_b[s * N_HEADS:(s + 1) * N_HEADS]], axis=1)
             for s in range(n_seq_blk)], axis=0)
        y = _dot(mix, wout_ref[...]) + bout_ref[...]
        x1 = _layer_norm(ALPHA * xn_cur[...] + y, g1_ref[...], b1_ref[...])
        x1_ref[...] = x1.reshape(x1_ref.shape)

    @pl.when(step % 2 == 0)
    def _():
        tile(proj_scr, xn_scr, x16_scr, proj_alt, xn_alt, x16_alt)

    @pl.when(step % 2 == 1)
    def _():
        tile(proj_alt, xn_alt, x16_alt, proj_scr, xn_scr, x16_scr)

    @pl.when(chunk == n_chunks - 1)
    def _():
        _finish_state(s_ref, n_seq_blk)


def _prompt_mixer(x, ln_e_g, ln_e_b, w_in, b_in, c0, n0, m0, s0, bf_row, ga, gb, lb,
                  w_out, b_out, ln_g, ln_b, *, n_chunks, t_len, n_seq_blk):
    n_seq = x.shape[0]
    nb = n_seq_blk
    n_tiles = (n_seq // nb) * n_chunks
    const = lambda b, c: (0, 0)
    init4 = lambda b, c: (0, 0, 0, 0)
    init3 = lambda b, c: (0, 0, 0)
    vec = pl.BlockSpec((1, D_MODEL), const)
    grp = pl.BlockSpec((1, D_GROUP), const)
    state4 = pl.BlockSpec((nb, N_HEADS, D_HEAD, D_HEAD), lambda b, c: (b, 0, 0, 0))

    def next_tile(b, c):
        nxt = jnp.minimum(b * n_chunks + c + 1, n_tiles - 1)
        return (nxt // n_chunks, nxt % n_chunks, 0)

    return pl.pallas_call(
        functools.partial(_prompt_kernel, t_len=t_len, n_chunks=n_chunks, n_seq_blk=nb),
        grid=(n_seq // nb, n_chunks),
        in_specs=[
            _resident((nb, t_len, D_MODEL), init3),
            pl.BlockSpec((nb, t_len, D_MODEL), next_tile),
            vec, vec,
            _resident((D_MODEL, D_PROJ), const),
            pl.BlockSpec((1, D_PROJ), const),
            pl.BlockSpec((1, N_HEADS, D_HEAD, D_HEAD), init4),
            pl.BlockSpec((1, N_HEADS, D_HEAD), init3),
            pl.BlockSpec((1, 1, N_HEADS), init3),
            pl.BlockSpec((1, N_HEADS, D_HEAD, D_HEAD), init4),
            pl.BlockSpec((1, LANES), const),
            grp, grp, grp,
            _resident((D_MODEL, D_MODEL), const),
            vec, vec, vec,
        ],
        out_specs=[
            pl.BlockSpec((nb, t_len, D_MODEL), lambda b, c: (b, c, 0)),
            state4,
            pl.BlockSpec((nb, N_HEADS, D_HEAD), lambda b, c: (b, 0, 0)),
            pl.BlockSpec((nb, 1, N_HEADS), lambda b, c: (b, 0, 0)),
            state4,
        ],
        out_shape=[
            jax.ShapeDtypeStruct(x.shape, F32),
            jax.ShapeDtypeStruct((n_seq, N_HEADS, D_HEAD, D_HEAD), F32),
            jax.ShapeDtypeStruct((n_seq, N_HEADS, D_HEAD), F32),
            jax.ShapeDtypeStruct((n_seq, 1, N_HEADS), F32),
            jax.ShapeDtypeStruct((n_seq, N_HEADS, D_HEAD, D_HEAD), F32),
        ],
        scratch_shapes=2 * [pltpu.VMEM((nb * t_len, D_ACT), F32), pltpu.VMEM((nb * t_len, D_MODEL), F32),
                            pltpu.VMEM((nb * t_len, D_MODEL), BF16)],
        compiler_params=pltpu.CompilerParams(
            dimension_semantics=("arbitrary", "arbitrary"), vmem_limit_bytes=VMEM_LIMIT),
        name="prompt_mixer",
    )(x, x, ln_e_g, ln_e_b, w_in, b_in, c0, n0, m0, s0, bf_row, ga, gb, lb, w_out, b_out, ln_g, ln_b)


def _ffn_kernel(x_ref, *refs, n_seq_blk, t_len):
    _ffn_tile(x_ref[...], *refs, n_seq_blk=n_seq_blk, t_len=t_len)


def _out_proj_ffn_kernel(x_ref, mix_ref, ge_ref, be_ref, wo_ref, bo_ref, g1_ref, b1_ref, *refs,
                         n_seq_blk, t_len):
    xn = _layer_norm(x_ref[...], ge_ref[...], be_ref[...])
    y = _dot(mix_ref[...], wo_ref[...]) + bo_ref[...]
    x1 = _layer_norm(ALPHA * xn + y, g1_ref[...], b1_ref[...])
    _ffn_tile(x1, *refs, n_seq_blk=n_seq_blk, t_len=t_len)


def _ffn_tile(x, cs_ref, wu_ref, bu_ref, wc_ref, bc_ref, wd_ref, bd_ref, g_ref, b_ref,
              y_ref, nc_ref, full_ref, *, n_seq_blk, t_len):
    hist = SUBLANES - (CONV_W - 1)

    @pl.when(pl.program_id(1) == 0)
    def _():
        full_ref[:, hist:SUBLANES, :] = cs_ref[...]

    up = _dot(x.astype(BF16), wu_ref[...]) + bu_ref[...]
    u = up[:, :D_FF].reshape(n_seq_blk, t_len, D_FF)
    gate = up[:, D_FF:].reshape(n_seq_blk, t_len, D_FF)
    full_ref[:, SUBLANES:SUBLANES + t_len, :] = u
    conv = bc_ref[...] + u * wc_ref[CONV_W - 1:CONV_W, :]
    for j in range(CONV_W - 1):
        conv = conv + full_ref[:, hist + j:hist + j + t_len, :] * wc_ref[j:j + 1, :]
    last = full_ref[:, hist + t_len:SUBLANES + t_len, :]
    nc_ref[...] = last
    full_ref[:, hist:SUBLANES, :] = last
    act = (conv * _sigmoid(conv) * gate).reshape(n_seq_blk * t_len, D_FF)
    ffn = _dot(act.astype(BF16), wd_ref[...]) + bd_ref[...]
    y_ref[...] = _layer_norm(ALPHA * x + ffn, g_ref[...], b_ref[...])


def _ffn(x, conv_state, w_up, b_up, w_conv, b_conv, w_down, b_down, ln_g, ln_b,
         *, n_seq, seq_len, n_seq_blk, t_len, shared_init, out_proj=None):
    n_t = seq_len // t_len
    rows = n_seq_blk * t_len
    const = lambda s, t: (0, 0)
    cs_map = (lambda s, t: (0, 0, 0)) if shared_init else (lambda s, t: (s, 0, 0))
    row = pl.BlockSpec((rows, D_MODEL), lambda s, t: (s * n_t + t, 0))
    vec = pl.BlockSpec((1, D_MODEL), const)
    body, lead_specs, lead_args = _ffn_kernel, [row], (x,)
    if out_proj is not None:
        body = _out_proj_ffn_kernel
        lead_specs = [row, row, vec, vec, _resident((D_MODEL, D_MODEL), const), vec, vec, vec]
        lead_args = (x,) + tuple(out_proj)
    return pl.pallas_call(
        functools.partial(body, n_seq_blk=n_seq_blk, t_len=t_len),
        grid=(n_seq // n_seq_blk, n_t),
        in_specs=lead_specs + [
            pl.BlockSpec((n_seq_blk, CONV_W - 1, D_FF), cs_map),
            _resident((D_MODEL, 2 * D_FF), const),
            pl.BlockSpec((1, 2 * D_FF), const),
            pl.BlockSpec((CONV_W, D_FF), const),
            pl.BlockSpec((1, D_FF), const),
            _resident((D_FF, D_MODEL), const),
            vec, vec, vec,
        ],
        out_specs=[row, pl.BlockSpec((n_seq_blk, CONV_W - 1, D_FF), lambda s, t: (s, 0, 0))],
        out_shape=[
            jax.ShapeDtypeStruct((n_seq * seq_len, D_MODEL), F32),
            jax.ShapeDtypeStruct((n_seq, CONV_W - 1, D_FF), F32),
        ],
        scratch_shapes=[pltpu.VMEM((n_seq_blk, SUBLANES + t_len, D_FF), F32)],
        compiler_params=pltpu.CompilerParams(
            dimension_semantics=("arbitrary", "arbitrary"), vmem_limit_bytes=VMEM_LIMIT),
        name=f"ffn_t{t_len}",
    )(*lead_args, conv_state, w_up, b_up, w_conv, b_conv, w_down, b_down, ln_g, ln_b)


def kernel(x_prompt, x_sample, state_mlstm_C, state_mlstm_n, state_mlstm_m, state_hgrn_S, state_ffn_conv, meta_tokens, ln_emb_g, ln_emb_b, w_in, b_in, b_fgate_a, g_norm_a, g_norm_b, hgrn_lb_logits, w_out, b_out, ln1_g, ln1_b, w_up, b_up, w_conv, b_conv, w_down, b_down, ln2_g, ln2_b):
    assert w_in.shape[0] == DEPTH == 1
    n_prompt, seq, _ = x_prompt.shape
    n_sample, dec_seq, _ = x_sample.shape
    row = lambda v: v.reshape(1, -1).astype(F32)

    gate0 = 4 * D_GROUP
    gate1 = gate0 + 2 * N_HEADS
    pad = D_PROJ - w_in.shape[2]
    w_in_p = _regroup_in_proj_weight(w_in[0].T, tm=256)
    b_in_p = jnp.concatenate(
        [b_in[0][:gate0], b_in[0][gate1:], b_in[0][gate0:gate1], jnp.zeros((pad,), b_in.dtype)]
    ).reshape(1, D_PROJ).astype(F32)
    bf_row = jnp.zeros((1, LANES), F32).at[0, N_HEADS:2 * N_HEADS].set(b_fgate_a[0].astype(F32))
    lb = jnp.cumsum(jax.nn.softmax(hgrn_lb_logits.astype(F32), axis=0), axis=0)[0].reshape(1, D_GROUP)
    ga, gb = row(g_norm_a[0]), row(g_norm_b[0])
    ln_e = (row(ln_emb_g), row(ln_emb_b))
    out_p = (w_out[0].astype(BF16), row(b_out[0]), row(ln1_g[0]), row(ln1_b[0]))
    ffn_p = (w_up[0].astype(BF16), row(b_up[0]), w_conv[0].astype(F32), row(b_conv[0]),
             w_down[0].astype(BF16), row(b_down[0]), row(ln2_g[0]), row(ln2_b[0]))

    c_m, n_m, m_m, s_m, conv_m = _meta_state(
        meta_tokens.astype(F32), *ln_e, w_in_p, b_in_p, bf_row, ga, gb, lb, *out_p, ffn_p[0], ffn_p[1])

    x1_p, c_p, n_p, m_p, s_p = _prompt_mixer(
        x_prompt.astype(F32), *ln_e, w_in_p, b_in_p, c_m, n_m, m_m, s_m, bf_row, ga, gb, lb, *out_p,
        n_chunks=seq // PROMPT_CHUNK, t_len=PROMPT_CHUNK, n_seq_blk=PROMPT_SEQS_PER_STEP)
    y_p, conv_p = _ffn(x1_p.reshape(n_prompt * seq, D_MODEL), conv_m, *ffn_p, n_seq=n_prompt,
                       seq_len=seq, n_seq_blk=1, t_len=512, shared_init=True)

    sample_state = (state_mlstm_C[0].astype(F32), state_mlstm_n[0].astype(F32),
                    state_mlstm_m[0].astype(F32).reshape(n_sample, 1, N_HEADS),
                    state_hgrn_S[0].astype(F32))
    xs_rows = x_sample.reshape(n_sample * dec_seq, D_MODEL).astype(F32)
    proj_s = _in_proj(xs_rows, *ln_e, w_in_p, b_in_p, tm=256)
    mix_s, c_s, n_s, m_s, s_s = _mixer(
        proj_s, *sample_state, bf_row, ga, gb, lb, n_seq=n_sample, n_chunks=1, t_len=dec_seq,
        n_seq_blk=8, shared_init=False)
    y_s, conv_s = _ffn(xs_rows, state_ffn_conv[0].astype(F32), *ffn_p, n_seq=n_sample,
                       seq_len=dec_seq, n_seq_blk=32, t_len=dec_seq, shared_init=False,
                       out_proj=(mix_s, *ln_e, *out_p))

    lead = lambda v: v[None]
    return (y_p.reshape(n_prompt, seq, D_MODEL), y_s.reshape(n_sample, dec_seq, D_MODEL),
            lead(c_p), lead(n_p), lead(m_p.reshape(n_prompt, N_HEADS)), lead(s_p), lead(conv_p),
            lead(c_s), lead(n_s), lead(m_s.reshape(n_sample, N_HEADS)), lead(s_s), lead(conv_s))
```

```python
import functools

import jax
import jax.numpy as jnp
from jax import lax
from jax.experimental import pallas as pl
from jax.experimental.pallas import tpu as pltpu

D_MODEL = 1024
N_META = 16
N_HEADS = 4
D_HEAD = 128
D_GROUP = N_HEADS * D_HEAD
D_FF = 2816
CONV_W = 3
DEPTH = 1
ALPHA = (2.0 * DEPTH) ** 0.25
LN_EPS = 1e-5
RMS_EPS = 1e-6
NEG_LOG2_E = -1.4426950408889634

LANES = 128
SUBLANES = 8
GATE_COL = 8 * D_GROUP
D_PROJ = GATE_COL + LANES
KB_GROUP, F_GROUP = 8, 9
D_ACT = D_PROJ + 2 * D_GROUP
LATE_GROUPS = (2, 3, 6, 7)
LATE_STAGES_PER_ROUND = 2
IN_PROJ_STAGE_COLS = 256
SIDE_STAGES_PER_ROUND = 3
PROMPT_SEQS_PER_STEP = 2
PROMPT_CHUNK = 128
VMEM_LIMIT = 56 * 1024 * 1024

F32 = jnp.float32
BF16 = jnp.bfloat16
NT_DIMS = (((1,), (1,)), ((), ()))
TN_DIMS = (((0,), (0,)), ((), ()))


def _layer_norm(x, g, b):
    mu = jnp.mean(x, axis=-1, keepdims=True)
    xc = x - mu
    var = jnp.mean(xc * xc, axis=-1, keepdims=True)
    return xc * lax.rsqrt(var + LN_EPS) * g + b


def _exp_neg(x):
    return jnp.exp2(x * NEG_LOG2_E)


def _sigmoid(x):
    return 1.0 / (1.0 + _exp_neg(x))


def _resident(block_shape, index_map):
    return pl.BlockSpec(block_shape, index_map, pipeline_mode=pl.Buffered(1))


def _dot(a, b):
    return jnp.dot(a, b, preferred_element_type=F32)


def _dot_nt(a, b):
    return lax.dot_general(a, b, NT_DIMS, preferred_element_type=F32)


def _dot_tn(a, b):
    return lax.dot_general(a, b, TN_DIMS, preferred_element_type=F32)


def _regroup_kernel(wt_ref, o_ref):
    gate0 = 4 * D_GROUP
    gate1 = gate0 + 2 * N_HEADS
    for j in range(GATE_COL // LANES):
        src = j * LANES if j * LANES < gate0 else j * LANES + (gate1 - gate0)
        o_ref[:, j * LANES:(j + 1) * LANES] = wt_ref[src:src + LANES, :].T.astype(o_ref.dtype)
    gates = wt_ref[gate0:gate1, :].T.astype(o_ref.dtype)
    o_ref[:, GATE_COL:] = jnp.concatenate(
        [gates, jnp.zeros((gates.shape[0], LANES - gates.shape[1]), o_ref.dtype)], axis=1)


def _regroup_in_proj_weight(w_t, *, tm):
    cols, n = w_t.shape
    return pl.pallas_call(
        _regroup_kernel,
        grid=(n // tm,),
        in_specs=[pl.BlockSpec((cols, tm), lambda i: (0, i))],
        out_specs=pl.BlockSpec((tm, D_PROJ), lambda i: (i, 0)),
        out_shape=jax.ShapeDtypeStruct((n, D_PROJ), BF16),
        compiler_params=pltpu.CompilerParams(dimension_semantics=("arbitrary",)),
        name="regroup_w_in",
    )(w_t)


def _in_proj_kernel(x_ref, g_ref, b_ref, w_ref, bias_ref, o_ref):
    xn = _layer_norm(x_ref[...], g_ref[...], b_ref[...])
    o_ref[...] = _dot(xn.astype(BF16), w_ref[...]) + bias_ref[...]


def _in_proj(x, ln_g, ln_b, w, bias, *, tm):
    n = x.shape[0]
    const = lambda i: (0, 0)
    return pl.pallas_call(
        _in_proj_kernel,
        grid=(n // tm,),
        in_specs=[
            pl.BlockSpec((tm, D_MODEL), lambda i: (i, 0)),
            pl.BlockSpec((1, D_MODEL), const),
            pl.BlockSpec((1, D_MODEL), const),
            pl.BlockSpec((D_MODEL, D_PROJ), const),
            pl.BlockSpec((1, D_PROJ), const),
        ],
        out_specs=pl.BlockSpec((tm, D_PROJ), lambda i: (i, 0)),
        out_shape=jax.ShapeDtypeStruct((n, D_PROJ), F32),
        compiler_params=pltpu.CompilerParams(
            dimension_semantics=("arbitrary",), vmem_limit_bytes=VMEM_LIMIT),
        name="in_proj",
    )(x, ln_g, ln_b, w, bias)


def _block_rows(x, level, t_len, row_in_block):
    size = 2 << level
    if size > SUBLANES:
        pieces = [jnp.broadcast_to(x[j * size + row_in_block:j * size + row_in_block + 1, :],
                                   (size, x.shape[1])) for j in range(t_len // size)]
        return pieces[0] if len(pieces) == 1 else jnp.concatenate(pieces, axis=0)
    x3 = x.reshape(t_len // SUBLANES, SUBLANES, x.shape[1])
    sub = lax.broadcasted_iota(jnp.int32, x3.shape, 1)
    out = None
    for j in range(SUBLANES // size):
        row = jnp.broadcast_to(x3[:, j * size + row_in_block:j * size + row_in_block + 1, :], x3.shape)
        out = row if out is None else jnp.where(sub >= j * size, row, out)
    return out.reshape(x.shape)


def _interleave_halves(lower, upper, level, t_len):
    half = 1 << level
    if half >= SUBLANES:
        pieces = []
        for j in range(t_len // (2 * half)):
            pieces.append(lower[2 * half * j:2 * half * j + half])
            pieces.append(upper[2 * half * j + half:2 * half * (j + 1)])
        return jnp.concatenate(pieces, axis=0)
    rows = lax.broadcasted_iota(jnp.int32, lower.shape, 0)
    return jnp.where((rows & half) != 0, upper, lower)


def _run_interleaved(gens, stages_per_round):
    results = [None] * len(gens)
    live = [True] * len(gens)
    while any(live):
        for g, steps in enumerate(stages_per_round):
            for _ in range(steps):
                if live[g]:
                    try:
                        next(gens[g])
                    except StopIteration as stop:
                        results[g], live[g] = stop.value, False
    return results


def _cumsum_rows(tril16, x):
    hi = x.astype(BF16)
    rest = x - hi.astype(F32)
    mid = rest.astype(BF16)
    lo = (rest - mid.astype(F32)).astype(BF16)
    return _dot(tril16, hi) + _dot(tril16, mid) + _dot(tril16, lo)


def _activate(group, x, lb=None):
    if group == 1:
        return {1: x * (D_HEAD ** -0.5)}
    if group in (3, 7):
        return {group: _sigmoid(x)}
    if group == 4:
        return {4: x * _sigmoid(x)}
    if group == 5:
        f = lb + (1.0 - lb) * _sigmoid(x)
        return {5: jnp.log(f), KB_GROUP: (1.0 - lb) / (1.0 + jnp.exp(x)), F_GROUP: f}
    return {group: x}


def _now(value):
    return value() if callable(value) else value


def _late_proj_stages(x16_ref, w_ref, bias_ref, lb_ref, out):
    for group in LATE_GROUPS:
        for off in range(0, D_GROUP, IN_PROJ_STAGE_COLS):
            yield
            cols = slice(group * D_GROUP + off, group * D_GROUP + off + IN_PROJ_STAGE_COLS)
            block = _dot(x16_ref[...], w_ref[:, cols]) + bias_ref[:, cols]
            out[group, off] = _activate(group, block, lb_ref[:, off:off + IN_PROJ_STAGE_COLS])[group]


def _mlstm_units(*, q, k, v, gate, z, zt, m0_row, c_old, n_old, g_norm, causal, lane, t_len):
    idx = range(len(q))
    seq = [i // N_HEADS for i in idx]
    b_lane = [N_HEADS + i % N_HEADS for i in idx]
    q16 = [q[i].astype(BF16) for i in idx]
    k16 = [k[i].astype(BF16) for i in idx]
    qk = [_dot_nt(q16[i], k16[i]) for i in idx]
    qc = [_dot(q16[i], c_old[i].astype(BF16)) for i in idx]
    bs_row = [zt[seq[i]][b_lane[i]:b_lane[i] + 1, :] - zt[seq[i]][i % N_HEADS:i % N_HEADS + 1, :]
              for i in idx]
    yield
    col = lambda per_seq, i: per_seq[seq[i]][:, b_lane[i]:b_lane[i] + 1]
    d = [jnp.where(causal, col(z, i) - bs_row[i], -jnp.inf) for i in idx]
    row_max = [jnp.max(d[i], axis=1, keepdims=True) for i in idx]
    last = slice(t_len - 1, t_len)
    m_t_seq, dec_seq, floor_seq, w_last_seq = [], [], [], []
    for s in range(len(z)):
        r = z[s] + m0_row[s]
        d_max = jnp.full(r.shape, -jnp.inf, F32)
        for h in range(N_HEADS):
            d_max = jnp.where(lane == N_HEADS + h, row_max[s * N_HEADS + h], d_max)
        m_t = jnp.maximum(r, d_max)
        i_gate = pltpu.roll(z[s], N_HEADS, axis=1)
        m_t_seq.append(m_t)
        dec_seq.append(jnp.exp(r - m_t))
        floor_seq.append(_exp_neg(m_t))
        w_last_seq.append(jnp.exp(z[s][last] - z[s] + i_gate - m_t[last]))
    m_t = [col(m_t_seq, i) for i in idx]
    dec = [col(dec_seq, i) for i in idx]
    w_last = [col(w_last_seq, i) for i in idx]
    sw = [jnp.exp(d[i] - m_t[i]) * qk[i] for i in idx]
    yield
    v = [_now(v[i]) for i in idx]
    swv = [_dot(sw[i].astype(BF16), v[i].astype(BF16)) for i in idx]
    kv = [_dot_tn(k16[i], (w_last[i] * v[i]).astype(BF16)) for i in idx]
    yield
    c_new = [dec[i][last] * c_old[i] + kv[i] for i in idx]
    n_new = [dec[i][last] * n_old[i] + jnp.sum(w_last[i] * k[i], axis=0, keepdims=True) for i in idx]
    m_new = [m_t[i][last] for i in idx]
    den = [dec[i] * jnp.sum(q[i] * n_old[i], axis=1, keepdims=True)
           + jnp.sum(sw[i], axis=1, keepdims=True) for i in idx]
    hid = [(dec[i] * qc[i] + swv[i]) / jnp.maximum(jnp.abs(den[i]), col(floor_seq, i)) for i in idx]
    yield
    rms = [lax.rsqrt(jnp.mean(hid[i] * hid[i], axis=1, keepdims=True) + RMS_EPS) for i in idx]
    out = [_now(gate[i]) * (hid[i] * rms[i] * g_norm[i]) for i in idx]
    return out, c_new, n_new, m_new


def _hgrn_units(*, qb, log_f, f, kb, iv, gate, s_old_t, g_norm, tril, level_of, t_len):
    idx = range(len(qb))
    n_levels = t_len.bit_length() - 1
    a = [_cumsum_rows(tril, log_f[i]) for i in idx]
    yield
    diag =[_dot_nt(qb[i].astype(BF16), kb[i].astype(BF16)) for i in idx]
    scores = [jnp.where(level_of == -2, diag[i], 0.0) for i in idx]
    for level in range(n_levels):
        yield
        x16 = []
        for i in idx:
            base = _interleave_halves(kb[i], qb[i], level, t_len)
            if level == 0:
                x = base * _interleave_halves(jnp.ones_like(f[i]), f[i], 0, t_len)
            else:
                ref = _block_rows(a[i], level, t_len, (1 << level) - 1)
                x = base * _exp_neg(jnp.abs(a[i] - ref))
            x16.append(x.astype(BF16))
        part = [_dot_nt(x16[i], x16[i]) for i in idx]
        scores = [jnp.where(level_of == level, part[i], scores[i]) for i in idx]
    yield
    last = slice(t_len - 1, t_len)
    q_in = [(qb[i] * jnp.exp(a[i])).astype(BF16) for i in idx]
    k_out = [(kb[i] * jnp.exp(a[i][last] - a[i])).astype(BF16) for i in idx]
    inter = [_dot_nt(q_in[i], s_old_t[i].astype(BF16)) for i in idx]
    iv16 = [_now(iv[i]).astype(BF16) for i in idx]
    intra = [_dot(scores[i].astype(BF16), iv16[i]) for i in idx]
    kv = [_dot_tn(iv16[i], k_out[i]) for i in idx]
    yield
    s_new_t = [jnp.exp(a[i][last]) * s_old_t[i] + kv[i] for i in idx]
    o = [inter[i] + intra[i] for i in idx]
    rms = [lax.rsqrt(jnp.mean(o[i] * o[i], axis=1, keepdims=True) + RMS_EPS) for i in idx]
    out = [_now(gate[i]) * (o[i] * rms[i] * g_norm[i]) for i in idx]
    return out, s_new_t


def _load_state(c0_ref, n0_ref, m0_ref, s0_ref, c_ref, n_ref, m_ref, s_ref, n_seq_blk):
    shared = c0_ref.shape[0] == 1 and n_seq_blk > 1
    for s in range(n_seq_blk):
        src = 0 if shared else s
        c_ref[s] = c0_ref[src]
        n_ref[s] = n0_ref[src]
        m_ref[s] = m0_ref[src]
        for h in range(N_HEADS):
            s_ref[s, h] = s0_ref[src, h].T


def _finish_state(s_ref, n_seq_blk):
    for s in range(n_seq_blk):
        for h in range(N_HEADS):
            s_ref[s, h] = s_ref[s, h].T


def _act_cols(group, h):
    base = group * D_GROUP if group < KB_GROUP else D_PROJ + (group - KB_GROUP) * D_GROUP
    return slice(base + h * D_HEAD, base + (h + 1) * D_HEAD)


def _mixer_body(p_ref, bf_ref, ga_ref, gb_ref, lb_ref, c_ref, n_ref, m_ref, s_ref,
                *, t_len, n_seq_blk, activated, side_stages=None, late_proj=None):
    rows = lax.broadcasted_iota(jnp.int32, (t_len, t_len), 0)
    cols = lax.broadcasted_iota(jnp.int32, (t_len, t_len), 1)
    causal = cols <= rows
    tril = causal.astype(BF16)
    level_of = jnp.where(rows > cols, 31 - lax.clz(rows ^ cols), jnp.where(rows == cols, -2, -1))
    lane = lax.broadcasted_iota(jnp.int32, (t_len, LANES), 1)
    is_f = (lane >= N_HEADS) & (lane < 2 * N_HEADS)
    hd = lambda j, h: slice(j * D_GROUP + h * D_HEAD, j * D_GROUP + (h + 1) * D_HEAD)

    units = [(s, h) for s in range(n_seq_blk) for h in range(N_HEADS)]
    rs = lambda s: slice(s * t_len, (s + 1) * t_len)

    z_seq, zt_seq, n_seq, m_seq = [], [], [], []
    for s in range(n_seq_blk):
        gates = p_ref[rs(s), GATE_COL:GATE_COL + LANES]
        log_f = jnp.where(is_f, jax.nn.log_sigmoid(gates + bf_ref[...]), 0.0)
        cum_f = _cumsum_rows(tril, log_f)
        z_seq.append(jnp.where(is_f, cum_f, jnp.where(lane < N_HEADS, gates, 0.0)))
        zt_seq.append(z_seq[s].T)
        n_seq.append(n_ref[s])
        m_seq.append(m_ref[s])
    computed, late_vals = {}, {}

    def act(group):
        source = 5 if group >= KB_GROUP else group
        if late_proj is not None and source in LATE_GROUPS:
            def thunk(s, h):
                off, col = divmod(h * D_HEAD, IN_PROJ_STAGE_COLS)
                return lambda: late_vals[source, off * IN_PROJ_STAGE_COLS][rs(s), col:col + D_HEAD]
            return [thunk(s, h) for s, h in units]
        if activated:
            return [p_ref[rs(s), _act_cols(group, h)] for s, h in units]
        for s, h in units:
            if (source, s, h) not in computed:
                computed[source, s, h] = _activate(source, p_ref[rs(s), hd(source, h)],
                                                   lb_ref[:, hd(0, h)])
        return [computed[source, s, h][group] for s, h in units]

    lane_row = lane[:1, :]
    m0_rows = []
    for s in range(n_seq_blk):
        m0_row = jnp.zeros((1, LANES), F32)
        for h in range(N_HEADS):
            m0_row = jnp.where(lane_row == N_HEADS + h, m_seq[s][:, h:h + 1], m0_row)
        m0_rows.append(m0_row)
    mlstm = _mlstm_units(
        q=act(0), k=act(1), v=act(2), gate=act(3), z=z_seq, zt=zt_seq, m0_row=m0_rows,
        c_old=[c_ref[s, h] for s, h in units], n_old=[n_seq[s][h:h + 1, :] for s, h in units],
        g_norm=[ga_ref[:, hd(0, h)] for s, h in units], causal=causal, lane=lane, t_len=t_len)
    hgrn = _hgrn_units(
        qb=act(4), log_f=act(5), f=act(F_GROUP), kb=act(KB_GROUP), iv=act(6), gate=act(7),
        s_old_t=[s_ref[s, h] for s, h in units],
        g_norm=[gb_ref[:, hd(0, h)] for s, h in units], tril=tril, level_of=level_of, t_len=t_len)
    gens, per_round = [mlstm, hgrn], [1, 2]
    if late_proj is not None:
        late = _late_proj_stages(*late_proj, lb_ref, late_vals)
        gens, per_round = [late] + gens, [LATE_STAGES_PER_ROUND] + per_round
    if side_stages is not None:
        gens, per_round = gens + [side_stages], per_round + [SIDE_STAGES_PER_ROUND]
    results = _run_interleaved(gens, per_round)
    outs, c_new, n_new, m_new = results[gens.index(mlstm)]
    outs_b, s_new_t = results[gens.index(hgrn)]

    for i, (s, h) in enumerate(units):
        c_ref[s, h] = c_new[i]
        s_ref[s, h] = s_new_t[i]
    head_lane = lax.broadcasted_iota(jnp.int32, (1, N_HEADS), 1)
    for s in range(n_seq_blk):
        n_ref[s] = jnp.concatenate(n_new[s * N_HEADS:(s + 1) * N_HEADS], axis=0)
        m_row = m_seq[s]
        for h in range(N_HEADS):
            m_row = jnp.where(head_lane == h, m_new[s * N_HEADS + h], m_row)
        m_ref[s] = m_row
    return outs, outs_b


def _mixer_kernel(p_ref, c0_ref, n0_ref, m0_ref, s0_ref, bf_ref, ga_ref, gb_ref, lb_ref,
                  mix_ref, c_ref, n_ref, m_ref, s_ref, *, t_len, n_chunks, n_seq_blk):
    chunk = pl.program_id(1)

    @pl.when(chunk == 0)
    def _():
        _load_state(c0_ref, n0_ref, m0_ref, s0_ref, c_ref, n_ref, m_ref, s_ref, n_seq_blk)

    outs_a, outs_b = _mixer_body(p_ref, bf_ref, ga_ref, gb_ref, lb_ref, c_ref, n_ref, m_ref, s_ref,
                                 t_len=t_len, n_seq_blk=n_seq_blk, activated=False)
    for i in range(n_seq_blk * N_HEADS):
        s, h = divmod(i, N_HEADS)
        rows = slice(s * t_len, (s + 1) * t_len)
        mix_ref[rows, h * D_HEAD:(h + 1) * D_HEAD] = outs_a[i].astype(mix_ref.dtype)
        mix_ref[rows, D_GROUP + h * D_HEAD:D_GROUP + (h + 1) * D_HEAD] = outs_b[i].astype(mix_ref.dtype)

    @pl.when(chunk == n_chunks - 1)
    def _():
        _finish_state(s_ref, n_seq_blk)


def _mixer(proj, c0, n0, m0, s0, bf_row, ga, gb, lb, *, n_seq, n_chunks, t_len, n_seq_blk,
           shared_init):
    assert not shared_init or n_seq_blk == 1
    assert n_chunks == 1 or n_seq_blk == 1
    nb = n_seq_blk
    init = (lambda b, c: (0, 0, 0, 0)) if shared_init else (lambda b, c: (b, 0, 0, 0))
    init3 = (lambda b, c: (0, 0, 0)) if shared_init else (lambda b, c: (b, 0, 0))
    const = lambda b, c: (0, 0)
    state4 = pl.BlockSpec((nb, N_HEADS, D_HEAD, D_HEAD), lambda b, c: (b, 0, 0, 0))
    return pl.pallas_call(
        functools.partial(_mixer_kernel, t_len=t_len, n_chunks=n_chunks, n_seq_blk=nb),
        grid=(n_seq // nb, n_chunks),
        in_specs=[
            pl.BlockSpec((nb * t_len, D_PROJ), lambda b, c: (b * n_chunks + c, 0)),
            pl.BlockSpec((nb, N_HEADS, D_HEAD, D_HEAD), init),
            pl.BlockSpec((nb, N_HEADS, D_HEAD), init3),
            pl.BlockSpec((nb, 1, N_HEADS), init3),
            pl.BlockSpec((nb, N_HEADS, D_HEAD, D_HEAD), init),
            pl.BlockSpec((1, LANES), const),
            pl.BlockSpec((1, D_GROUP), const),
            pl.BlockSpec((1, D_GROUP), const),
            pl.BlockSpec((1, D_GROUP), const),
        ],
        out_specs=[
            pl.BlockSpec((nb * t_len, D_MODEL), lambda b, c: (b * n_chunks + c, 0)),
            state4,
            pl.BlockSpec((nb, N_HEADS, D_HEAD), lambda b, c: (b, 0, 0)),
            pl.BlockSpec((nb, 1, N_HEADS), lambda b, c: (b, 0, 0)),
            state4,
        ],
        out_shape=[
            jax.ShapeDtypeStruct((n_seq * n_chunks * t_len, D_MODEL), BF16),
            jax.ShapeDtypeStruct((n_seq, N_HEADS, D_HEAD, D_HEAD), F32),
            jax.ShapeDtypeStruct((n_seq, N_HEADS, D_HEAD), F32),
            jax.ShapeDtypeStruct((n_seq, 1, N_HEADS), F32),
            jax.ShapeDtypeStruct((n_seq, N_HEADS, D_HEAD, D_HEAD), F32),
        ],
        compiler_params=pltpu.CompilerParams(
            dimension_semantics=("arbitrary", "arbitrary"), vmem_limit_bytes=VMEM_LIMIT),
        name=f"mixer_t{t_len}",
    )(proj, c0, n0, m0, s0, bf_row, ga, gb, lb)


def _meta_kernel(x_ref, ge_ref, be_ref, win_ref, bin_ref, bf_ref, ga_ref, gb_ref, lb_ref,
                 wout_ref, bout_ref, g1_ref, b1_ref, wu_ref, bu_ref,
                 c_ref, n_ref, m_ref, s_ref, conv_ref, proj_scr):
    t_len = x_ref.shape[0]
    xn = _layer_norm(x_ref[...], ge_ref[...], be_ref[...])
    proj_scr[...] = _dot(xn.astype(BF16), win_ref[...]) + bin_ref[...]
    for ref in (c_ref, n_ref, m_ref, s_ref):
        ref[...] = jnp.zeros(ref.shape, ref.dtype)
    outs_a, outs_b = _mixer_body(proj_scr, bf_ref, ga_ref, gb_ref, lb_ref, c_ref, n_ref, m_ref, s_ref,
                                 t_len=t_len, n_seq_blk=1, activated=False)
    _finish_state(s_ref, 1)
    mix = jnp.concatenate([o.astype(BF16) for o in outs_a + outs_b], axis=1)
    y = _dot(mix, wout_ref[...]) + bout_ref[...]
    x1 = _layer_norm(ALPHA * xn + y, g1_ref[...], b1_ref[...])
    u = _dot(x1.astype(BF16), wu_ref[...]) + bu_ref[...]
    conv_ref[0] = u[t_len - (CONV_W - 1):, :]


def _meta_state(x, ln_e_g, ln_e_b, w_in, b_in, bf_row, ga, gb, lb, w_out, b_out, ln_g, ln_b,
                w_up, b_up):
    t_len = x.shape[0]
    full = lambda shape: _resident(shape, lambda i: (0,) * len(shape))
    vec, grp = full((1, D_MODEL)), full((1, D_GROUP))
    state4 = (1, N_HEADS, D_HEAD, D_HEAD)
    return pl.pallas_call(
        _meta_kernel,
        grid=(1,),
        in_specs=[full((t_len, D_MODEL)), vec, vec, full((D_MODEL, D_PROJ)), full((1, D_PROJ)),
                  full((1, LANES)), grp, grp, grp, full((D_MODEL, D_MODEL)), vec, vec, vec,
                  full((D_MODEL, D_FF)), full((1, D_FF))],
        out_specs=[pl.BlockSpec(shape, lambda i, rank=len(shape): (0,) * rank)
                   for shape in (state4, (1, N_HEADS, D_HEAD), (1, 1, N_HEADS), state4,
                                 (1, CONV_W - 1, D_FF))],
        out_shape=[
            jax.ShapeDtypeStruct(state4, F32),
            jax.ShapeDtypeStruct((1, N_HEADS, D_HEAD), F32),
            jax.ShapeDtypeStruct((1, 1, N_HEADS), F32),
            jax.ShapeDtypeStruct(state4, F32),
            jax.ShapeDtypeStruct((1, CONV_W - 1, D_FF), F32),
        ],
        scratch_shapes=[pltpu.VMEM((t_len, D_PROJ), F32)],
        compiler_params=pltpu.CompilerParams(
            dimension_semantics=("arbitrary",), vmem_limit_bytes=VMEM_LIMIT),
        name="meta_state",
    )(x, ln_e_g, ln_e_b, w_in, b_in, bf_row, ga, gb, lb, w_out, b_out, ln_g, ln_b, w_up, b_up)


def _in_proj_stages(x_ref, g_ref, b_ref, w_ref, bias_ref, lb_ref, act_ref, xn_ref, x16_ref):
    assert D_GROUP % IN_PROJ_STAGE_COLS == 0
    xn = _layer_norm(x_ref[...].reshape(xn_ref.shape), g_ref[...], b_ref[...])
    xn_ref[...] = xn
    x16 = xn.astype(BF16)
    x16_ref[...] = x16
    starts = [lo for lo in range(0, GATE_COL, IN_PROJ_STAGE_COLS) if lo // D_GROUP not in LATE_GROUPS]
    for lo in starts + [GATE_COL]:
        hi = min(lo + IN_PROJ_STAGE_COLS, D_PROJ)
        yield
        block = _dot(x16, w_ref[:, lo:hi]) + bias_ref[:, lo:hi]
        group, off = divmod(lo, D_GROUP)
        if lo >= GATE_COL:
            act_ref[:, lo:hi] = block
            continue
        for dst, val in _activate(group, block, lb_ref[:, off:off + hi - lo]).items():
            base = _act_cols(dst, 0).start + off
            act_ref[:, base:base + hi - lo] = val


def _prompt_kernel(x0_ref, xnext_ref, ge_ref, be_ref, win_ref, bin_ref, c0_ref, n0_ref, m0_ref, s0_ref,
                   bf_ref, ga_ref, gb_ref, lb_ref, wout_ref, bout_ref, g1_ref, b1_ref,
                   x1_ref, c_ref, n_ref, m_ref, s_ref,
                   proj_scr, xn_scr, x16_scr, proj_alt, xn_alt, x16_alt,
                   *, t_len, n_chunks, n_seq_blk):
    chunk = pl.program_id(1)
    step = pl.program_id(0) * n_chunks + chunk

    @pl.when(step == 0)
    def _():
        first = _in_proj_stages(x0_ref, ge_ref, be_ref, win_ref, bin_ref, lb_ref,
                                proj_scr, xn_scr, x16_scr)
        _run_interleaved([first], [1])

    @pl.when(chunk == 0)
    def _():
        _load_state(c0_ref, n0_ref, m0_ref, s0_ref, c_ref, n_ref, m_ref, s_ref, n_seq_blk)

    def tile(proj_cur, xn_cur, x16_cur, proj_next, xn_next, x16_next):
        next_proj = _in_proj_stages(xnext_ref, ge_ref, be_ref, win_ref, bin_ref, lb_ref,
                                    proj_next, xn_next, x16_next)
        outs_a, outs_b = _mixer_body(
            proj_cur, bf_ref, ga_ref, gb_ref, lb_ref, c_ref, n_ref, m_ref, s_ref,
            t_len=t_len, n_seq_blk=n_seq_blk, activated=True, side_stages=next_proj,
            late_proj=(x16_cur, win_ref, bin_ref))
        mix = jnp.concatenate(
            [jnp.concatenate([o.astype(BF16) for o in outs_a[s * N_HEADS:(s + 1) * N_HEADS]
                              + outs_b[s * N_HEADS:(s + 1) * N_HEADS]], axis=1)
             for s in range(n_seq_blk)], axis=0)
        y = _dot(mix, wout_ref[...]) + bout_ref[...]
        x1 = _layer_norm(ALPHA * xn_cur[...] + y, g1_ref[...], b1_ref[...])
        x1_ref[...] = x1.reshape(x1_ref.shape)

    @pl.when(step % 2 == 0)
    def _():
        tile(proj_scr, xn_scr, x16_scr, proj_alt, xn_alt, x16_alt)

    @pl.when(step % 2 == 1)
    def _():
        tile(proj_alt, xn_alt, x16_alt, proj_scr, xn_scr, x16_scr)

    @pl.when(chunk == n_chunks - 1)
    def _():
        _finish_state(s_ref, n_seq_blk)


def _prompt_mixer(x, ln_e_g, ln_e_b, w_in, b_in, c0, n0, m0, s0, bf_row, ga, gb, lb,
                  w_out, b_out, ln_g, ln_b, *, n_chunks, t_len, n_seq_blk):
    n_seq = x.shape[0]
    nb = n_seq_blk
    n_tiles = (n_seq // nb) * n_chunks
    const = lambda b, c: (0, 0)
    init4 = lambda b, c: (0, 0, 0, 0)
    init3 = lambda b, c: (0, 0, 0)
    vec = pl.BlockSpec((1, D_MODEL), const)
    grp = pl.BlockSpec((1, D_GROUP), const)
    state4 = pl.BlockSpec((nb, N_HEADS, D_HEAD, D_HEAD), lambda b, c: (b, 0, 0, 0))

    def next_tile(b, c):
        nxt = jnp.minimum(b * n_chunks + c + 1, n_tiles - 1)
        return (nxt // n_chunks, nxt % n_chunks, 0)

    return pl.pallas_call(
        functools.partial(_prompt_kernel, t_len=t_len, n_chunks=n_chunks, n_seq_blk=nb),
        grid=(n_seq // nb, n_chunks),
        in_specs=[
            _resident((nb, t_len, D_MODEL), init3),
            pl.BlockSpec((nb, t_len, D_MODEL), next_tile),
            vec, vec,
            _resident((D_MODEL, D_PROJ), const),
            pl.BlockSpec((1, D_PROJ), const),
            pl.BlockSpec((1, N_HEADS, D_HEAD, D_HEAD), init4),
            pl.BlockSpec((1, N_HEADS, D_HEAD), init3),
            pl.BlockSpec((1, 1, N_HEADS), init3),
            pl.BlockSpec((1, N_HEADS, D_HEAD, D_HEAD), init4),
            pl.BlockSpec((1, LANES), const),
            grp, grp, grp,
            _resident((D_MODEL, D_MODEL), const),
            vec, vec, vec,
        ],
        out_specs=[
            pl.BlockSpec((nb, t_len, D_MODEL), lambda b, c: (b, c, 0)),
            state4,
            pl.BlockSpec((nb, N_HEADS, D_HEAD), lambda b, c: (b, 0, 0)),
            pl.BlockSpec((nb, 1, N_HEADS), lambda b, c: (b, 0, 0)),
            state4,
        ],
        out_shape=[
            jax.ShapeDtypeStruct(x.shape, F32),
            jax.ShapeDtypeStruct((n_seq, N_HEADS, D_HEAD, D_HEAD), F32),
            jax.ShapeDtypeStruct((n_seq, N_HEADS, D_HEAD), F32),
            jax.ShapeDtypeStruct((n_seq, 1, N_HEADS), F32),
            jax.ShapeDtypeStruct((n_seq, N_HEADS, D_HEAD, D_HEAD), F32),
        ],
        scratch_shapes=2 * [pltpu.VMEM((nb * t_len, D_ACT), F32), pltpu.VMEM((nb * t_len, D_MODEL), F32),
                            pltpu.VMEM((nb * t_len, D_MODEL), BF16)],
        compiler_params=pltpu.CompilerParams(
            dimension_semantics=("arbitrary", "arbitrary"), vmem_limit_bytes=VMEM_LIMIT),
        name="prompt_mixer",
    )(x, x, ln_e_g, ln_e_b, w_in, b_in, c0, n0, m0, s0, bf_row, ga, gb, lb, w_out, b_out, ln_g, ln_b)


def _ffn_kernel(x_ref, *refs, n_seq_blk, t_len):
    _ffn_tile(x_ref[...], *refs, n_seq_blk=n_seq_blk, t_len=t_len)


def _out_proj_ffn_kernel(x_ref, mix_ref, ge_ref, be_ref, wo_ref, bo_ref, g1_ref, b1_ref, *refs,
                         n_seq_blk, t_len):
    xn = _layer_norm(x_ref[...], ge_ref[...], be_ref[...])
    y = _dot(mix_ref[...], wo_ref[...]) + bo_ref[...]
    x1 = _layer_norm(ALPHA * xn + y, g1_ref[...], b1_ref[...])
    _ffn_tile(x1, *refs, n_seq_blk=n_seq_blk, t_len=t_len)


def _ffn_tile(x, cs_ref, wu_ref, bu_ref, wc_ref, bc_ref, wd_ref, bd_ref, g_ref, b_ref,
              y_ref, nc_ref, full_ref, *, n_seq_blk, t_len):
    hist = SUBLANES - (CONV_W - 1)

    @pl.when(pl.program_id(1) == 0)
    def _():
        full_ref[:, hist:SUBLANES, :] = cs_ref[...]

    up = _dot(x.astype(BF16), wu_ref[...]) + bu_ref[...]
    u = up[:, :D_FF].reshape(n_seq_blk, t_len, D_FF)
    gate = up[:, D_FF:].reshape(n_seq_blk, t_len, D_FF)
    full_ref[:, SUBLANES:SUBLANES + t_len, :] = u
    conv = bc_ref[...] + u * wc_ref[CONV_W - 1:CONV_W, :]
    for j in range(CONV_W - 1):
        conv = conv + full_ref[:, hist + j:hist + j + t_len, :] * wc_ref[j:j + 1, :]
    last = full_ref[:, hist + t_len:SUBLANES + t_len, :]
    nc_ref[...] = last
    full_ref[:, hist:SUBLANES, :] = last
    act = (conv * _sigmoid(conv) * gate).reshape(n_seq_blk * t_len, D_FF)
    ffn = _dot(act.astype(BF16), wd_ref[...]) + bd_ref[...]
    y_ref[...] = _layer_norm(ALPHA * x + ffn, g_ref[...], b_ref[...])


def _ffn(x, conv_state, w_up, b_up, w_conv, b_conv, w_down, b_down, ln_g, ln_b,
         *, n_seq, seq_len, n_seq_blk, t_len, shared_init, out_proj=None):
    n_t = seq_len // t_len
    rows = n_seq_blk * t_len
    const = lambda s, t: (0, 0)
    cs_map = (lambda s, t: (0, 0, 0)) if shared_init else (lambda s, t: (s, 0, 0))
    row = pl.BlockSpec((rows, D_MODEL), lambda s, t: (s * n_t + t, 0))
    vec = pl.BlockSpec((1, D_MODEL), const)
    body, lead_specs, lead_args = _ffn_kernel, [row], (x,)
    if out_proj is not None:
        body = _out_proj_ffn_kernel
        lead_specs = [row, row, vec, vec, _resident((D_MODEL, D_MODEL), const), vec, vec, vec]
        lead_args = (x,) + tuple(out_proj)
    return pl.pallas_call(
        functools.partial(body, n_seq_blk=n_seq_blk, t_len=t_len),
        grid=(n_seq // n_seq_blk, n_t),
        in_specs=lead_specs + [
            pl.BlockSpec((n_seq_blk, CONV_W - 1, D_FF), cs_map),
            _resident((D_MODEL, 2 * D_FF), const),
            pl.BlockSpec((1, 2 * D_FF), const),
            pl.BlockSpec((CONV_W, D_FF), const),
            pl.BlockSpec((1, D_FF), const),
            _resident((D_FF, D_MODEL), const),
            vec, vec, vec,
        ],
        out_specs=[row, pl.BlockSpec((n_seq_blk, CONV_W - 1, D_FF), lambda s, t: (s, 0, 0))],
        out_shape=[
            jax.ShapeDtypeStruct((n_seq * seq_len, D_MODEL), F32),
            jax.ShapeDtypeStruct((n_seq, CONV_W - 1, D_FF), F32),
        ],
        scratch_shapes=[pltpu.VMEM((n_seq_blk, SUBLANES + t_len, D_FF), F32)],
        compiler_params=pltpu.CompilerParams(
            dimension_semantics=("arbitrary", "arbitrary"), vmem_limit_bytes=VMEM_LIMIT),
        name=f"ffn_t{t_len}",
    )(*lead_args, conv_state, w_up, b_up, w_conv, b_conv, w_down, b_down, ln_g, ln_b)


def kernel(x_prompt, x_sample, state_mlstm_C, state_mlstm_n, state_mlstm_m, state_hgrn_S, state_ffn_conv, meta_tokens, ln_emb_g, ln_emb_b, w_in, b_in, b_fgate_a, g_norm_a, g_norm_b, hgrn_lb_logits, w_out, b_out, ln1_g, ln1_b, w_up, b_up, w_conv, b_conv, w_down, b_down, ln2_g, ln2_b):
    assert w_in.shape[0] == DEPTH == 1
    n_prompt, seq, _ = x_prompt.shape
    n_sample, dec_seq, _ = x_sample.shape
    row = lambda v: v.reshape(1, -1).astype(F32)

    gate0 = 4 * D_GROUP
    gate1 = gate0 + 2 * N_HEADS
    pad = D_PROJ - w_in.shape[2]
    w_in_p = _regroup_in_proj_weight(w_in[0].T, tm=256)
    b_in_p = jnp.concatenate(
        [b_in[0][:gate0], b_in[0][gate1:], b_in[0][gate0:gate1], jnp.zeros((pad,), b_in.dtype)]
    ).reshape(1, D_PROJ).astype(F32)
    bf_row = jnp.zeros((1, LANES), F32).at[0, N_HEADS:2 * N_HEADS].set(b_fgate_a[0].astype(F32))
    lb = jnp.cumsum(jax.nn.softmax(hgrn_lb_logits.astype(F32), axis=0), axis=0)[0].reshape(1, D_GROUP)
    ga, gb = row(g_norm_a[0]), row(g_norm_b[0])
    ln_e = (row(ln_emb_g), row(ln_emb_b))
    out_p = (w_out[0].astype(BF16), row(b_out[0]), row(ln1_g[0]), row(ln1_b[0]))
    ffn_p = (w_up[0].astype(BF16), row(b_up[0]), w_conv[0].astype(F32), row(b_conv[0]),
             w_down[0].astype(BF16), row(b_down[0]), row(ln2_g[0]), row(ln2_b[0]))

    c_m, n_m, m_m, s_m, conv_m = _meta_state(
        meta_tokens.astype(F32), *ln_e, w_in_p, b_in_p, bf_row, ga, gb, lb, *out_p, ffn_p[0], ffn_p[1])

    x1_p, c_p, n_p, m_p, s_p = _prompt_mixer(
        x_prompt.astype(F32), *ln_e, w_in_p, b_in_p, c_m, n_m, m_m, s_m, bf_row, ga, gb, lb, *out_p,
        n_chunks=seq // PROMPT_CHUNK, t_len=PROMPT_CHUNK, n_seq_blk=PROMPT_SEQS_PER_STEP)
    y_p, conv_p = _ffn(x1_p.reshape(n_prompt * seq, D_MODEL), conv_m, *ffn_p, n_seq=n_prompt,
                       seq_len=seq, n_seq_blk=1, t_len=512, shared_init=True)

    sample_state = (state_mlstm_C[0].astype(F32), state_mlstm_n[0].astype(F32),
                    state_mlstm_m[0].astype(F32).reshape(n_sample, 1, N_HEADS),
                    state_hgrn_S[0].astype(F32))
    xs_rows = x_sample.reshape(n_sample * dec_seq, D_MODEL).astype(F32)
    proj_s = _in_proj(xs_rows, *ln_e, w_in_p, b_in_p, tm=256)
    mix_s, c_s, n_s, m_s, s_s = _mixer(
        proj_s, *sample_state, bf_row, ga, gb, lb, n_seq=n_sample, n_chunks=1, t_len=dec_seq,
        n_seq_blk=8, shared_init=False)
    y_s, conv_s = _ffn(xs_rows, state_ffn_conv[0].astype(F32), *ffn_p, n_seq=n_sample,
                       seq_len=dec_seq, n_seq_blk=32, t_len=dec_seq, shared_init=False,
                       out_proj=(mix_s, *ln_e, *out_p))

    lead = lambda v: v[None]
    return (y_p.reshape(n_prompt, seq, D_MODEL), y_s.reshape(n_sample, dec_seq, D_MODEL),
            lead(c_p), lead(n_p), lead(m_p.reshape(n_prompt, N_HEADS)), lead(s_p), lead(conv_p),
            lead(c_s), lead(n_s), lead(m_s.reshape(n_sample, N_HEADS)), lead(s_s), lead(conv_s))
```

```python
import functools

import jax
import jax.numpy as jnp
from jax import lax
from jax.experimental import pallas as pl
from jax.experimental.pallas import tpu as pltpu

D_MODEL = 1024
N_META = 16
N_HEADS = 4
D_HEAD = 128
D_GROUP = N_HEADS * D_HEAD
D_FF = 2816
CONV_W = 3
DEPTH = 1
ALPHA = (2.0 * DEPTH) ** 0.25
LN_EPS = 1e-5
RMS_EPS = 1e-6
NEG_LOG2_E = -1.4426950408889634

LANES = 128
SUBLANES = 8
GATE_COL = 8 * D_GROUP
D_PROJ = GATE_COL + LANES
KB_GROUP, F_GROUP = 8, 9
D_ACT = D_PROJ + 2 * D_GROUP
LATE_GROUPS = (3, 6, 7)
LATE_STAGE_COLS = 256
LATE_STAGES_PER_ROUND = 2
IN_PROJ_STAGE_COLS = 256
SIDE_STAGES_PER_ROUND = 3
PROMPT_SEQS_PER_STEP = 2
PROMPT_CHUNK = 128
VMEM_LIMIT = 56 * 1024 * 1024

F32 = jnp.float32
BF16 = jnp.bfloat16
NT_DIMS = (((1,), (1,)), ((), ()))
TN_DIMS = (((0,), (0,)), ((), ()))


def _layer_norm(x, g, b):
    mu = jnp.mean(x, axis=-1, keepdims=True)
    xc = x - mu
    var = jnp.mean(xc * xc, axis=-1, keepdims=True)
    return xc * lax.rsqrt(var + LN_EPS) * g + b


def _exp_neg(x):
    return jnp.exp2(x * NEG_LOG2_E)


def _sigmoid(x):
    return 1.0 / (1.0 + _exp_neg(x))


def _resident(block_shape, index_map):
    return pl.BlockSpec(block_shape, index_map, pipeline_mode=pl.Buffered(1))


def _dot(a, b):
    return jnp.dot(a, b, preferred_element_type=F32)


def _dot_nt(a, b):
    return lax.dot_general(a, b, NT_DIMS, preferred_element_type=F32)


def _dot_tn(a, b):
    return lax.dot_general(a, b, TN_DIMS, preferred_element_type=F32)


def _regroup_kernel(wt_ref, o_ref):
    gate0 = 4 * D_GROUP
    gate1 = gate0 + 2 * N_HEADS
    for j in range(GATE_COL // LANES):
        src = j * LANES if j * LANES < gate0 else j * LANES + (gate1 - gate0)
        o_ref[:, j * LANES:(j + 1) * LANES] = wt_ref[src:src + LANES, :].T.astype(o_ref.dtype)
    gates = wt_ref[gate0:gate1, :].T.astype(o_ref.dtype)
    o_ref[:, GATE_COL:] = jnp.concatenate(
        [gates, jnp.zeros((gates.shape[0], LANES - gates.shape[1]), o_ref.dtype)], axis=1)


def _regroup_in_proj_weight(w_t, *, tm):
    cols, n = w_t.shape
    return pl.pallas_call(
        _regroup_kernel,
        grid=(n // tm,),
        in_specs=[pl.BlockSpec((cols, tm), lambda i: (0, i))],
        out_specs=pl.BlockSpec((tm, D_PROJ), lambda i: (i, 0)),
        out_shape=jax.ShapeDtypeStruct((n, D_PROJ), BF16),
        compiler_params=pltpu.CompilerParams(dimension_semantics=("arbitrary",)),
        name="regroup_w_in",
    )(w_t)


def _in_proj_kernel(x_ref, g_ref, b_ref, w_ref, bias_ref, o_ref):
    xn = _layer_norm(x_ref[...], g_ref[...], b_ref[...])
    o_ref[...] = _dot(xn.astype(BF16), w_ref[...]) + bias_ref[...]


def _in_proj(x, ln_g, ln_b, w, bias, *, tm):
    n = x.shape[0]
    const = lambda i: (0, 0)
    return pl.pallas_call(
        _in_proj_kernel,
        grid=(n // tm,),
        in_specs=[
            pl.BlockSpec((tm, D_MODEL), lambda i: (i, 0)),
            pl.BlockSpec((1, D_MODEL), const),
            pl.BlockSpec((1, D_MODEL), const),
            pl.BlockSpec((D_MODEL, D_PROJ), const),
            pl.BlockSpec((1, D_PROJ), const),
        ],
        out_specs=pl.BlockSpec((tm, D_PROJ), lambda i: (i, 0)),
        out_shape=jax.ShapeDtypeStruct((n, D_PROJ), F32),
        compiler_params=pltpu.CompilerParams(
            dimension_semantics=("arbitrary",), vmem_limit_bytes=VMEM_LIMIT),
        name="in_proj",
    )(x, ln_g, ln_b, w, bias)


def _block_rows(x, level, t_len, row_in_block):
    size = 2 << level
    if size > SUBLANES:
        pieces = [jnp.broadcast_to(x[j * size + row_in_block:j * size + row_in_block + 1, :],
                                   (size, x.shape[1])) for j in range(t_len // size)]
        return pieces[0] if len(pieces) == 1 else jnp.concatenate(pieces, axis=0)
    x3 = x.reshape(t_len // SUBLANES, SUBLANES, x.shape[1])
    sub = lax.broadcasted_iota(jnp.int32, x3.shape, 1)
    out = None
    for j in range(SUBLANES // size):
        row = jnp.broadcast_to(x3[:, j * size + row_in_block:j * size + row_in_block + 1, :], x3.shape)
        out = row if out is None else jnp.where(sub >= j * size, row, out)
    return out.reshape(x.shape)


def _interleave_halves(lower, upper, level, t_len):
    half = 1 << level
    if half >= SUBLANES:
        pieces = []
        for j in range(t_len // (2 * half)):
            pieces.append(lower[2 * half * j:2 * half * j + half])
            pieces.append(upper[2 * half * j + half:2 * half * (j + 1)])
        return jnp.concatenate(pieces, axis=0)
    rows = lax.broadcasted_iota(jnp.int32, lower.shape, 0)
    return jnp.where((rows & half) != 0, upper, lower)


def _run_interleaved(gens, stages_per_round):
    results = [None] * len(gens)
    live = [True] * len(gens)
    while any(live):
        for g, steps in enumerate(stages_per_round):
            for _ in range(steps):
                if live[g]:
                    try:
                        next(gens[g])
                    except StopIteration as stop:
                        results[g], live[g] = stop.value, False
    return results


def _cumsum_rows(tril16, x):
    hi = x.astype(BF16)
    rest = x - hi.astype(F32)
    mid = rest.astype(BF16)
    lo = (rest - mid.astype(F32)).astype(BF16)
    return _dot(tril16, hi) + _dot(tril16, mid) + _dot(tril16, lo)


def _activate(group, x, lb=None):
    if group == 1:
        return {1: x * (D_HEAD ** -0.5)}
    if group in (3, 7):
        return {group: _sigmoid(x)}
    if group == 4:
        return {4: x * _sigmoid(x)}
    if group == 5:
        f = lb + (1.0 - lb) * _sigmoid(x)
        return {5: jnp.log(f), KB_GROUP: (1.0 - lb) / (1.0 + jnp.exp(x)), F_GROUP: f}
    return {group: x}


def _now(value):
    return value() if callable(value) else value


def _late_proj_stages(x16_ref, w_ref, bias_ref, lb_ref, out):
    for group in LATE_GROUPS:
        for off in range(0, D_GROUP, LATE_STAGE_COLS):
            yield
            cols = slice(group * D_GROUP + off, group * D_GROUP + off + LATE_STAGE_COLS)
            block = _dot(x16_ref[...], w_ref[:, cols]) + bias_ref[:, cols]
            out[group, off] = _activate(group, block, lb_ref[:, off:off + LATE_STAGE_COLS])[group]


def _mlstm_units(*, q, k, v, gate, z, zt, m0_row, c_old, n_old, g_norm, causal, lane, t_len):
    idx = range(len(q))
    seq = [i // N_HEADS for i in idx]
    b_lane = [N_HEADS + i % N_HEADS for i in idx]
    q16 = [q[i].astype(BF16) for i in idx]
    k16 = [k[i].astype(BF16) for i in idx]
    qk = [_dot_nt(q16[i], k16[i]) for i in idx]
    qc = [_dot(q16[i], c_old[i].astype(BF16)) for i in idx]
    bs_row = [zt[seq[i]][b_lane[i]:b_lane[i] + 1, :] - zt[seq[i]][i % N_HEADS:i % N_HEADS + 1, :]
              for i in idx]
    yield
    col = lambda per_seq, i: per_seq[seq[i]][:, b_lane[i]:b_lane[i] + 1]
    d = [jnp.where(causal, col(z, i) - bs_row[i], -jnp.inf) for i in idx]
    row_max = [jnp.max(d[i], axis=1, keepdims=True) for i in idx]
    last = slice(t_len - 1, t_len)
    m_t_seq, dec_seq, floor_seq, w_last_seq = [], [], [], []
    for s in range(len(z)):
        r = z[s] + m0_row[s]
        d_max = jnp.full(r.shape, -jnp.inf, F32)
        for h in range(N_HEADS):
            d_max = jnp.where(lane == N_HEADS + h, row_max[s * N_HEADS + h], d_max)
        m_t = jnp.maximum(r, d_max)
        i_gate = pltpu.roll(z[s], N_HEADS, axis=1)
        m_t_seq.append(m_t)
        dec_seq.append(jnp.exp(r - m_t))
        floor_seq.append(_exp_neg(m_t))
        w_last_seq.append(jnp.exp(z[s][last] - z[s] + i_gate - m_t[last]))
    m_t = [col(m_t_seq, i) for i in idx]
    dec = [col(dec_seq, i) for i in idx]
    w_last = [col(w_last_seq, i) for i in idx]
    sw = [jnp.exp(d[i] - m_t[i]) * qk[i] for i in idx]
    yield
    swv = [_dot(sw[i].astype(BF16), v[i].astype(BF16)) for i in idx]
    kv = [_dot_tn(k16[i], (w_last[i] * v[i]).astype(BF16)) for i in idx]
    yield
    c_new = [dec[i][last] * c_old[i] + kv[i] for i in idx]
    n_new = [dec[i][last] * n_old[i] + jnp.sum(w_last[i] * k[i], axis=0, keepdims=True) for i in idx]
    m_new = [m_t[i][last] for i in idx]
    den = [dec[i] * jnp.sum(q[i] * n_old[i], axis=1, keepdims=True)
           + jnp.sum(sw[i], axis=1, keepdims=True) for i in idx]
    hid = [(dec[i] * qc[i] + swv[i]) / jnp.maximum(jnp.abs(den[i]), col(floor_seq, i)) for i in idx]
    yield
    rms = [lax.rsqrt(jnp.mean(hid[i] * hid[i], axis=1, keepdims=True) + RMS_EPS) for i in idx]
    out = [_now(gate[i]) * (hid[i] * rms[i] * g_norm[i]) for i in idx]
    return out, c_new, n_new, m_new


def _hgrn_units(*, qb, log_f, f, kb, iv, gate, s_old_t, g_norm, tril, level_of, t_len):
    idx = range(len(qb))
    n_levels = t_len.bit_length() - 1
    a = [_cumsum_rows(tril, log_f[i]) for i in idx]
    yield
    diag =[_dot_nt(qb[i].astype(BF16), kb[i].astype(BF16)) for i in idx]
    scores = [jnp.where(level_of == -2, diag[i], 0.0) for i in idx]
    for level in range(n_levels):
        yield
        x16 = []
        for i in idx:
            base = _interleave_halves(kb[i], qb[i], level, t_len)
            if level == 0:
                x = base * _interleave_halves(jnp.ones_like(f[i]), f[i], 0, t_len)
            else:
                ref = _block_rows(a[i], level, t_len, (1 << level) - 1)
                x = base * _exp_neg(jnp.abs(a[i] - ref))
            x16.append(x.astype(BF16))
        part = [_dot_nt(x16[i], x16[i]) for i in idx]
        scores = [jnp.where(level_of == level, part[i], scores[i]) for i in idx]
    yield
    last = slice(t_len - 1, t_len)
    q_in = [(qb[i] * jnp.exp(a[i])).astype(BF16) for i in idx]
    k_out = [(kb[i] * jnp.exp(a[i][last] - a[i])).astype(BF16) for i in idx]
    inter = [_dot_nt(q_in[i], s_old_t[i].astype(BF16)) for i in idx]
    iv16 = [_now(iv[i]).astype(BF16) for i in idx]
    intra = [_dot(scores[i].astype(BF16), iv16[i]) for i in idx]
    kv = [_dot_tn(iv16[i], k_out[i]) for i in idx]
    yield
    s_new_t = [jnp.exp(a[i][last]) * s_old_t[i] + kv[i] for i in idx]
    o = [inter[i] + intra[i] for i in idx]
    rms = [lax.rsqrt(jnp.mean(o[i] * o[i], axis=1, keepdims=True) + RMS_EPS) for i in idx]
    out = [_now(gate[i]) * (o[i] * rms[i] * g_norm[i]) for i in idx]
    return out, s_new_t


def _load_state(c0_ref, n0_ref, m0_ref, s0_ref, c_ref, n_ref, m_ref, s_ref, n_seq_blk):
    shared = c0_ref.shape[0] == 1 and n_seq_blk > 1
    for s in range(n_seq_blk):
        src = 0 if shared else s
        c_ref[s] = c0_ref[src]
        n_ref[s] = n0_ref[src]
        m_ref[s] = m0_ref[src]
        for h in range(N_HEADS):
            s_ref[s, h] = s0_ref[src, h].T


def _finish_state(s_ref, n_seq_blk):
    for s in range(n_seq_blk):
        for h in range(N_HEADS):
            s_ref[s, h] = s_ref[s, h].T


def _act_cols(group, h):
    base = group * D_GROUP if group < KB_GROUP else D_PROJ + (group - KB_GROUP) * D_GROUP
    return slice(base + h * D_HEAD, base + (h + 1) * D_HEAD)


def _mixer_body(p_ref, bf_ref, ga_ref, gb_ref, lb_ref, c_ref, n_ref, m_ref, s_ref,
                *, t_len, n_seq_blk, activated, side_stages=None, late_proj=None):
    rows = lax.broadcasted_iota(jnp.int32, (t_len, t_len), 0)
    cols = lax.broadcasted_iota(jnp.int32, (t_len, t_len), 1)
    causal = cols <= rows
    tril = causal.astype(BF16)
    level_of = jnp.where(rows > cols, 31 - lax.clz(rows ^ cols), jnp.where(rows == cols, -2, -1))
    lane = lax.broadcasted_iota(jnp.int32, (t_len, LANES), 1)
    is_f = (lane >= N_HEADS) & (lane < 2 * N_HEADS)
    hd = lambda j, h: slice(j * D_GROUP + h * D_HEAD, j * D_GROUP + (h + 1) * D_HEAD)

    units = [(s, h) for s in range(n_seq_blk) for h in range(N_HEADS)]
    rs = lambda s: slice(s * t_len, (s + 1) * t_len)

    z_seq, zt_seq, n_seq, m_seq = [], [], [], []
    for s in range(n_seq_blk):
        gates = p_ref[rs(s), GATE_COL:GATE_COL + LANES]
        log_f = jnp.where(is_f, jax.nn.log_sigmoid(gates + bf_ref[...]), 0.0)
        cum_f = _cumsum_rows(tril, log_f)
        z_seq.append(jnp.where(is_f, cum_f, jnp.where(lane < N_HEADS, gates, 0.0)))
        zt_seq.append(z_seq[s].T)
        n_seq.append(n_ref[s])
        m_seq.append(m_ref[s])
    computed, late_vals = {}, {}

    def act(group):
        source = 5 if group >= KB_GROUP else group
        if late_proj is not None and source in LATE_GROUPS:
            def thunk(s, h):
                off, col = divmod(h * D_HEAD, LATE_STAGE_COLS)
                return lambda: late_vals[source, off * LATE_STAGE_COLS][rs(s), col:col + D_HEAD]
            return [thunk(s, h) for s, h in units]
        if activated:
            return [p_ref[rs(s), _act_cols(group, h)] for s, h in units]
        for s, h in units:
            if (source, s, h) not in computed:
                computed[source, s, h] = _activate(source, p_ref[rs(s), hd(source, h)],
                                                   lb_ref[:, hd(0, h)])
        return [computed[source, s, h][group] for s, h in units]

    lane_row = lane[:1, :]
    m0_rows = []
    for s in range(n_seq_blk):
        m0_row = jnp.zeros((1, LANES), F32)
        for h in range(N_HEADS):
            m0_row = jnp.where(lane_row == N_HEADS + h, m_seq[s][:, h:h + 1], m0_row)
        m0_rows.append(m0_row)
    mlstm = _mlstm_units(
        q=act(0), k=act(1), v=act(2), gate=act(3), z=z_seq, zt=zt_seq, m0_row=m0_rows,
        c_old=[c_ref[s, h] for s, h in units], n_old=[n_seq[s][h:h + 1, :] for s, h in units],
        g_norm=[ga_ref[:, hd(0, h)] for s, h in units], causal=causal, lane=lane, t_len=t_len)
    hgrn = _hgrn_units(
        qb=act(4), log_f=act(5), f=act(F_GROUP), kb=act(KB_GROUP), iv=act(6), gate=act(7),
        s_old_t=[s_ref[s, h] for s, h in units],
        g_norm=[gb_ref[:, hd(0, h)] for s, h in units], tril=tril, level_of=level_of, t_len=t_len)
    gens, per_round = [mlstm, hgrn], [1, 2]
    if late_proj is not None:
        late = _late_proj_stages(*late_proj, lb_ref, late_vals)
        gens, per_round = [late] + gens, [LATE_STAGES_PER_ROUND] + per_round
    if side_stages is not None:
        gens, per_round = gens + [side_stages], per_round + [SIDE_STAGES_PER_ROUND]
    results = _run_interleaved(gens, per_round)
    outs, c_new, n_new, m_new = results[gens.index(mlstm)]
    outs_b, s_new_t = results[gens.index(hgrn)]

    for i, (s, h) in enumerate(units):
        c_ref[s, h] = c_new[i]
        s_ref[s, h] = s_new_t[i]
    head_lane = lax.broadcasted_iota(jnp.int32, (1, N_HEADS), 1)
    for s in range(n_seq_blk):
        n_ref[s] = jnp.concatenate(n_new[s * N_HEADS:(s + 1) * N_HEADS], axis=0)
        m_row = m_seq[s]
        for h in range(N_HEADS):
            m_row = jnp.where(head_lane == h, m_new[s * N_HEADS + h], m_row)
        m_ref[s] = m_row
    return outs, outs_b


def _mixer_kernel(p_ref, c0_ref, n0_ref, m0_ref, s0_ref, bf_ref, ga_ref, gb_ref, lb_ref,
                  mix_ref, c_ref, n_ref, m_ref, s_ref, *, t_len, n_chunks, n_seq_blk):
    chunk = pl.program_id(1)

    @pl.when(chunk == 0)
    def _():
        _load_state(c0_ref, n0_ref, m0_ref, s0_ref, c_ref, n_ref, m_ref, s_ref, n_seq_blk)

    outs_a, outs_b = _mixer_body(p_ref, bf_ref, ga_ref, gb_ref, lb_ref, c_ref, n_ref, m_ref, s_ref,
                                 t_len=t_len, n_seq_blk=n_seq_blk, activated=False)
    for i in range(n_seq_blk * N_HEADS):
        s, h = divmod(i, N_HEADS)
        rows = slice(s * t_len, (s + 1) * t_len)
        mix_ref[rows, h * D_HEAD:(h + 1) * D_HEAD] = outs_a[i].astype(mix_ref.dtype)
        mix_ref[rows, D_GROUP + h * D_HEAD:D_GROUP + (h + 1) * D_HEAD] = outs_b[i].astype(mix_ref.dtype)

    @pl.when(chunk == n_chunks - 1)
    def _():
        _finish_state(s_ref, n_seq_blk)


def _mixer(proj, c0, n0, m0, s0, bf_row, ga, gb, lb, *, n_seq, n_chunks, t_len, n_seq_blk,
           shared_init):
    assert not shared_init or n_seq_blk == 1
    assert n_chunks == 1 or n_seq_blk == 1
    nb = n_seq_blk
    init = (lambda b, c: (0, 0, 0, 0)) if shared_init else (lambda b, c: (b, 0, 0, 0))
    init3 = (lambda b, c: (0, 0, 0)) if shared_init else (lambda b, c: (b, 0, 0))
    const = lambda b, c: (0, 0)
    state4 = pl.BlockSpec((nb, N_HEADS, D_HEAD, D_HEAD), lambda b, c: (b, 0, 0, 0))
    return pl.pallas_call(
        functools.partial(_mixer_kernel, t_len=t_len, n_chunks=n_chunks, n_seq_blk=nb),
        grid=(n_seq // nb, n_chunks),
        in_specs=[
            pl.BlockSpec((nb * t_len, D_PROJ), lambda b, c: (b * n_chunks + c, 0)),
            pl.BlockSpec((nb, N_HEADS, D_HEAD, D_HEAD), init),
            pl.BlockSpec((nb, N_HEADS, D_HEAD), init3),
            pl.BlockSpec((nb, 1, N_HEADS), init3),
            pl.BlockSpec((nb, N_HEADS, D_HEAD, D_HEAD), init),
            pl.BlockSpec((1, LANES), const),
            pl.BlockSpec((1, D_GROUP), const),
            pl.BlockSpec((1, D_GROUP), const),
            pl.BlockSpec((1, D_GROUP), const),
        ],
        out_specs=[
            pl.BlockSpec((nb * t_len, D_MODEL), lambda b, c: (b * n_chunks + c, 0)),
            state4,
            pl.BlockSpec((nb, N_HEADS, D_HEAD), lambda b, c: (b, 0, 0)),
            pl.BlockSpec((nb, 1, N_HEADS), lambda b, c: (b, 0, 0)),
            state4,
        ],
        out_shape=[
            jax.ShapeDtypeStruct((n_seq * n_chunks * t_len, D_MODEL), BF16),
            jax.ShapeDtypeStruct((n_seq, N_HEADS, D_HEAD, D_HEAD), F32),
            jax.ShapeDtypeStruct((n_seq, N_HEADS, D_HEAD), F32),
            jax.ShapeDtypeStruct((n_seq, 1, N_HEADS), F32),
            jax.ShapeDtypeStruct((n_seq, N_HEADS, D_HEAD, D_HEAD), F32),
        ],
        compiler_params=pltpu.CompilerParams(
            dimension_semantics=("arbitrary", "arbitrary"), vmem_limit_bytes=VMEM_LIMIT),
        name=f"mixer_t{t_len}",
    )(proj, c0, n0, m0, s0, bf_row, ga, gb, lb)


def _meta_kernel(x_ref, ge_ref, be_ref, win_ref, bin_ref, bf_ref, ga_ref, gb_ref, lb_ref,
                 wout_ref, bout_ref, g1_ref, b1_ref, wu_ref, bu_ref,
                 c_ref, n_ref, m_ref, s_ref, conv_ref, proj_scr):
    t_len = x_ref.shape[0]
    xn = _layer_norm(x_ref[...], ge_ref[...], be_ref[...])
    proj_scr[...] = _dot(xn.astype(BF16), win_ref[...]) + bin_ref[...]
    for ref in (c_ref, n_ref, m_ref, s_ref):
        ref[...] = jnp.zeros(ref.shape, ref.dtype)
    outs_a, outs_b = _mixer_body(proj_scr, bf_ref, ga_ref, gb_ref, lb_ref, c_ref, n_ref, m_ref, s_ref,
                                 t_len=t_len, n_seq_blk=1, activated=False)
    _finish_state(s_ref, 1)
    mix = jnp.concatenate([o.astype(BF16) for o in outs_a + outs_b], axis=1)
    y = _dot(mix, wout_ref[...]) + bout_ref[...]
    x1 = _layer_norm(ALPHA * xn + y, g1_ref[...], b1_ref[...])
    u = _dot(x1.astype(BF16), wu_ref[...]) + bu_ref[...]
    conv_ref[0] = u[t_len - (CONV_W - 1):, :]


def _meta_state(x, ln_e_g, ln_e_b, w_in, b_in, bf_row, ga, gb, lb, w_out, b_out, ln_g, ln_b,
                w_up, b_up):
    t_len = x.shape[0]
    full = lambda shape: _resident(shape, lambda i: (0,) * len(shape))
    vec, grp = full((1, D_MODEL)), full((1, D_GROUP))
    state4 = (1, N_HEADS, D_HEAD, D_HEAD)
    return pl.pallas_call(
        _meta_kernel,
        grid=(1,),
        in_specs=[full((t_len, D_MODEL)), vec, vec, full((D_MODEL, D_PROJ)), full((1, D_PROJ)),
                  full((1, LANES)), grp, grp, grp, full((D_MODEL, D_MODEL)), vec, vec, vec,
                  full((D_MODEL, D_FF)), full((1, D_FF))],
        out_specs=[pl.BlockSpec(shape, lambda i, rank=len(shape): (0,) * rank)
                   for shape in (state4, (1, N_HEADS, D_HEAD), (1, 1, N_HEADS), state4,
                                 (1, CONV_W - 1, D_FF))],
        out_shape=[
            jax.ShapeDtypeStruct(state4, F32),
            jax.ShapeDtypeStruct((1, N_HEADS, D_HEAD), F32),
            jax.ShapeDtypeStruct((1, 1, N_HEADS), F32),
            jax.ShapeDtypeStruct(state4, F32),
            jax.ShapeDtypeStruct((1, CONV_W - 1, D_FF), F32),
        ],
        scratch_shapes=[pltpu.VMEM((t_len, D_PROJ), F32)],
        compiler_params=pltpu.CompilerParams(
            dimension_semantics=("arbitrary",), vmem_limit_bytes=VMEM_LIMIT),
        name="meta_state",
    )(x, ln_e_g, ln_e_b, w_in, b_in, bf_row, ga, gb, lb, w_out, b_out, ln_g, ln_b, w_up, b_up)


def _in_proj_stages(x_ref, g_ref, b_ref, w_ref, bias_ref, lb_ref, act_ref, xn_ref, x16_ref):
    assert D_GROUP % IN_PROJ_STAGE_COLS == 0
    xn = _layer_norm(x_ref[...].reshape(xn_ref.shape), g_ref[...], b_ref[...])
    xn_ref[...] = xn
    x16 = xn.astype(BF16)
    x16_ref[...] = x16
    starts = [lo for lo in range(0, GATE_COL, IN_PROJ_STAGE_COLS) if lo // D_GROUP not in LATE_GROUPS]
    for lo in starts + [GATE_COL]:
        hi = min(lo + IN_PROJ_STAGE_COLS, D_PROJ)
        yield
        block = _dot(x16, w_ref[:, lo:hi]) + bias_ref[:, lo:hi]
        group, off = divmod(lo, D_GROUP)
        if lo >= GATE_COL:
            act_ref[:, lo:hi] = block
            continue
        for dst, val in _activate(group, block, lb_ref[:, off:off + hi - lo]).items():
            base = _act_cols(dst, 0).start + off
            act_ref[:, base:base + hi - lo] = val


def _prompt_kernel(x0_ref, xnext_ref, ge_ref, be_ref, win_ref, bin_ref, c0_ref, n0_ref, m0_ref, s0_ref,
                   bf_ref, ga_ref, gb_ref, lb_ref, wout_ref, bout_ref, g1_ref, b1_ref,
                   x1_ref, c_ref, n_ref, m_ref, s_ref,
                   proj_scr, xn_scr, x16_scr, proj_alt, xn_alt, x16_alt,
                   *, t_len, n_chunks, n_seq_blk):
    chunk = pl.program_id(1)
    step = pl.program_id(0) * n_chunks + chunk

    @pl.when(step == 0)
    def _():
        first = _in_proj_stages(x0_ref, ge_ref, be_ref, win_ref, bin_ref, lb_ref,
                                proj_scr, xn_scr, x16_scr)
        _run_interleaved([first], [1])

    @pl.when(chunk == 0)
    def _():
        _load_state(c0_ref, n0_ref, m0_ref, s0_ref, c_ref, n_ref, m_ref, s_ref, n_seq_blk)

    def tile(proj_cur, xn_cur, x16_cur, proj_next, xn_next, x16_next):
        next_proj = _in_proj_stages(xnext_ref, ge_ref, be_ref, win_ref, bin_ref, lb_ref,
                                    proj_next, xn_next, x16_next)
        outs_a, outs_b = _mixer_body(
            proj_cur, bf_ref, ga_ref, gb_ref, lb_ref, c_ref, n_ref, m_ref, s_ref,
            t_len=t_len, n_seq_blk=n_seq_blk, activated=True, side_stages=next_proj,
            late_proj=(x16_cur, win_ref, bin_ref))
        mix = jnp.concatenate(
            [jnp.concatenate([o.astype(BF16) for o in outs_a[s * N_HEADS:(s + 1) * N_HEADS]
                              + outs_b[s * N_HEADS:(s + 1) * N_HEADS]], axis=1)
             for s in range(n_seq_blk)], axis=0)
        y = _dot(mix, wout_ref[...]) + bout_ref[...]
        x1 = _layer_norm(ALPHA * xn_cur[...] + y, g1_ref[...], b1_ref[...])
        x1_ref[...] = x1.reshape(x1_ref.shape)

    @pl.when(step % 2 == 0)
    def _():
        tile(proj_scr, xn_scr, x16_scr, proj_alt, xn_alt, x16_alt)

    @pl.when(step % 2 == 1)
    def _():
        tile(proj_alt, xn_alt, x16_alt, proj_scr, xn_scr, x16_scr)

    @pl.when(chunk == n_chunks - 1)
    def _():
        _finish_state(s_ref, n_seq_blk)


def _prompt_mixer(x, ln_e_g, ln_e_b, w_in, b_in, c0, n0, m0, s0, bf_row, ga, gb, lb,
                  w_out, b_out, ln_g, ln_b, *, n_chunks, t_len, n_seq_blk):
    n_seq = x.shape[0]
    nb = n_seq_blk
    n_tiles = (n_seq // nb) * n_chunks
    const = lambda b, c: (0, 0)
    init4 = lambda b, c: (0, 0, 0, 0)
    init3 = lambda b, c: (0, 0, 0)
    vec = pl.BlockSpec((1, D_MODEL), const)
    grp = pl.BlockSpec((1, D_GROUP), const)
    state4 = pl.BlockSpec((nb, N_HEADS, D_HEAD, D_HEAD), lambda b, c: (b, 0, 0, 0))

    def next_tile(b, c):
        nxt = jnp.minimum(b * n_chunks + c + 1, n_tiles - 1)
        return (nxt // n_chunks, nxt % n_chunks, 0)

    return pl.pallas_call(
        functools.partial(_prompt_kernel, t_len=t_len, n_chunks=n_chunks, n_seq_blk=nb),
        grid=(n_seq // nb, n_chunks),
        in_specs=[
            _resident((nb, t_len, D_MODEL), init3),
            pl.BlockSpec((nb, t_len, D_MODEL), next_tile),
            vec, vec,
            _resident((D_MODEL, D_PROJ), const),
            pl.BlockSpec((1, D_PROJ), const),
            pl.BlockSpec((1, N_HEADS, D_HEAD, D_HEAD), init4),
            pl.BlockSpec((1, N_HEADS, D_HEAD), init3),
            pl.BlockSpec((1, 1, N_HEADS), init3),
            pl.BlockSpec((1, N_HEADS, D_HEAD, D_HEAD), init4),
            pl.BlockSpec((1, LANES), const),
            grp, grp, grp,
            _resident((D_MODEL, D_MODEL), const),
            vec, vec, vec,
        ],
        out_specs=[
            pl.BlockSpec((nb, t_len, D_MODEL), lambda b, c: (b, c, 0)),
            state4,
            pl.BlockSpec((nb, N_HEADS, D_HEAD), lambda b, c: (b, 0, 0)),
            pl.BlockSpec((nb, 1, N_HEADS), lambda b, c: (b, 0, 0)),
            state4,
        ],
        out_shape=[
            jax.ShapeDtypeStruct(x.shape, F32),
            jax.ShapeDtypeStruct((n_seq, N_HEADS, D_HEAD, D_HEAD), F32),
            jax.ShapeDtypeStruct((n_seq, N_HEADS, D_HEAD), F32),
            jax.ShapeDtypeStruct((n_seq, 1, N_HEADS), F32),
            jax.ShapeDtypeStruct((n_seq, N_HEADS, D_HEAD, D_HEAD), F32),
        ],
        scratch_shapes=2 * [pltpu.VMEM((nb * t_len, D_ACT), F32), pltpu.VMEM((nb * t_len, D_MODEL), F32),
                            pltpu.VMEM((nb * t_len, D_MODEL), BF16)],
        compiler_params=pltpu.CompilerParams(
            dimension_semantics=("arbitrary", "arbitrary"), vmem_limit_bytes=VMEM_LIMIT),
        name="prompt_mixer",
    )(x, x, ln_e_g, ln_e_b, w_in, b_in, c0, n0, m0, s0, bf_row, ga, gb, lb, w_out, b_out, ln_g, ln_b)


def _ffn_kernel(x_ref, *refs, n_seq_blk, t_len):
    _ffn_tile(x_ref[...], *refs, n_seq_blk=n_seq_blk, t_len=t_len)


def _out_proj_ffn_kernel(x_ref, mix_ref, ge_ref, be_ref, wo_ref, bo_ref, g1_ref, b1_ref, *refs,
                         n_seq_blk, t_len):
    xn = _layer_norm(x_ref[...], ge_ref[...], be_ref[...])
    y = _dot(mix_ref[...], wo_ref[...]) + bo_ref[...]
    x1 = _layer_norm(ALPHA * xn + y, g1_ref[...], b1_ref[...])
    _ffn_tile(x1, *refs, n_seq_blk=n_seq_blk, t_len=t_len)


def _ffn_tile(x, cs_ref, wu_ref, bu_ref, wc_ref, bc_ref, wd_ref, bd_ref, g_ref, b_ref,
              y_ref, nc_ref, full_ref, *, n_seq_blk, t_len):
    hist = SUBLANES - (CONV_W - 1)

    @pl.when(pl.program_id(1) == 0)
    def _():
        full_ref[:, hist:SUBLANES, :] = cs_ref[...]

    up = _dot(x.astype(BF16), wu_ref[...]) + bu_ref[...]
    u = up[:, :D_FF].reshape(n_seq_blk, t_len, D_FF)
    gate = up[:, D_FF:].reshape(n_seq_blk, t_len, D_FF)
    full_ref[:, SUBLANES:SUBLANES + t_len, :] = u
    conv = bc_ref[...] + u * wc_ref[CONV_W - 1:CONV_W, :]
    for j in range(CONV_W - 1):
        conv = conv + full_ref[:, hist + j:hist + j + t_len, :] * wc_ref[j:j + 1, :]
    last = full_ref[:, hist + t_len:SUBLANES + t_len, :]
    nc_ref[...] = last
    full_ref[:, hist:SUBLANES, :] = last
    act = (conv * _sigmoid(conv) * gate).reshape(n_seq_blk * t_len, D_FF)
    ffn = _dot(act.astype(BF16), wd_ref[...]) + bd_ref[...]
    y_ref[...] = _layer_norm(ALPHA * x + ffn, g_ref[...], b_ref[...])


def _ffn(x, conv_state, w_up, b_up, w_conv, b_conv, w_down, b_down, ln_g, ln_b,
         *, n_seq, seq_len, n_seq_blk, t_len, shared_init, out_proj=None):
    n_t = seq_len // t_len
    rows = n_seq_blk * t_len
    const = lambda s, t: (0, 0)
    cs_map = (lambda s, t: (0, 0, 0)) if shared_init else (lambda s, t: (s, 0, 0))
    row = pl.BlockSpec((rows, D_MODEL), lambda s, t: (s * n_t + t, 0))
    vec = pl.BlockSpec((1, D_MODEL), const)
    body, lead_specs, lead_args = _ffn_kernel, [row], (x,)
    if out_proj is not None:
        body = _out_proj_ffn_kernel
        lead_specs = [row, row, vec, vec, _resident((D_MODEL, D_MODEL), const), vec, vec, vec]
        lead_args = (x,) + tuple(out_proj)
    return pl.pallas_call(
        functools.partial(body, n_seq_blk=n_seq_blk, t_len=t_len),
        grid=(n_seq // n_seq_blk, n_t),
        in_specs=lead_specs + [
            pl.BlockSpec((n_seq_blk, CONV_W - 1, D_FF), cs_map),
            _resident((D_MODEL, 2 * D_FF), const),
            pl.BlockSpec((1, 2 * D_FF), const),
            pl.BlockSpec((CONV_W, D_FF), const),
            pl.BlockSpec((1, D_FF), const),
            _resident((D_FF, D_MODEL), const),
            vec, vec, vec,
        ],
        out_specs=[row, pl.BlockSpec((n_seq_blk, CONV_W - 1, D_FF), lambda s, t: (s, 0, 0))],
        out_shape=[
            jax.ShapeDtypeStruct((n_seq * seq_len, D_MODEL), F32),
            jax.ShapeDtypeStruct((n_seq, CONV_W - 1, D_FF), F32),
        ],
        scratch_shapes=[pltpu.VMEM((n_seq_blk, SUBLANES + t_len, D_FF), F32)],
        compiler_params=pltpu.CompilerParams(
            dimension_semantics=("arbitrary", "arbitrary"), vmem_limit_bytes=VMEM_LIMIT),
        name=f"ffn_t{t_len}",
    )(*lead_args, conv_state, w_up, b_up, w_conv, b_conv, w_down, b_down, ln_g, ln_b)


def kernel(x_prompt, x_sample, state_mlstm_C, state_mlstm_n, state_mlstm_m, state_hgrn_S, state_ffn_conv, meta_tokens, ln_emb_g, ln_emb_b, w_in, b_in, b_fgate_a, g_norm_a, g_norm_b, hgrn_lb_logits, w_out, b_out, ln1_g, ln1_b, w_up, b_up, w_conv, b_conv, w_down, b_down, ln2_g, ln2_b):
    assert w_in.shape[0] == DEPTH == 1
    n_prompt, seq, _ = x_prompt.shape
    n_sample, dec_seq, _ = x_sample.shape
    row = lambda v: v.reshape(1, -1).astype(F32)

    gate0 = 4 * D_GROUP
    gate1 = gate0 + 2 * N_HEADS
    pad = D_PROJ - w_in.shape[2]
    w_in_p = _regroup_in_proj_weight(w_in[0].T, tm=256)
    b_in_p = jnp.concatenate(
        [b_in[0][:gate0], b_in[0][gate1:], b_in[0][gate0:gate1], jnp.zeros((pad,), b_in.dtype)]
    ).reshape(1, D_PROJ).astype(F32)
    bf_row = jnp.zeros((1, LANES), F32).at[0, N_HEADS:2 * N_HEADS].set(b_fgate_a[0].astype(F32))
    lb = jnp.cumsum(jax.nn.softmax(hgrn_lb_logits.astype(F32), axis=0), axis=0)[0].reshape(1, D_GROUP)
    ga, gb = row(g_norm_a[0]), row(g_norm_b[0])
    ln_e = (row(ln_emb_g), row(ln_emb_b))
    out_p = (w_out[0].astype(BF16), row(b_out[0]), row(ln1_g[0]), row(ln1_b[0]))
    ffn_p = (w_up[0].astype(BF16), row(b_up[0]), w_conv[0].astype(F32), row(b_conv[0]),
             w_down[0].astype(BF16), row(b_down[0]), row(ln2_g[0]), row(ln2_b[0]))

    c_m, n_m, m_m, s_m, conv_m = _meta_state(
        meta_tokens.astype(F32), *ln_e, w_in_p, b_in_p, bf_row, ga, gb, lb, *out_p, ffn_p[0], ffn_p[1])

    x1_p, c_p, n_p, m_p, s_p = _prompt_mixer(
        x_prompt.astype(F32), *ln_e, w_in_p, b_in_p, c_m, n_m, m_m, s_m, bf_row, ga, gb, lb, *out_p,
        n_chunks=seq // PROMPT_CHUNK, t_len=PROMPT_CHUNK, n_seq_blk=PROMPT_SEQS_PER_STEP)
    y_p, conv_p = _ffn(x1_p.reshape(n_prompt * seq, D_MODEL), conv_m, *ffn_p, n_seq=n_prompt,
                       seq_len=seq, n_seq_blk=1, t_len=512, shared_init=True)

    sample_state = (state_mlstm_C[0].astype(F32), state_mlstm_n[0].astype(F32),
                    state_mlstm_m[0].astype(F32).reshape(n_sample, 1, N_HEADS),
                    state_hgrn_S[0].astype(F32))
    xs_rows = x_sample.reshape(n_sample * dec_seq, D_MODEL).astype(F32)
    proj_s = _in_proj(xs_rows, *ln_e, w_in_p, b_in_p, tm=256)
    mix_s, c_s, n_s, m_s, s_s = _mixer(
        proj_s, *sample_state, bf_row, ga, gb, lb, n_seq=n_sample, n_chunks=1, t_len=dec_seq,
        n_seq_blk=8, shared_init=False)
    y_s, conv_s = _ffn(xs_rows, state_ffn_conv[0].astype(F32), *ffn_p, n_seq=n_sample,
                       seq_len=dec_seq, n_seq_blk=32, t_len=dec_seq, shared_init=False,
                       out_proj=(mix_s, *ln_e, *out_p))

    lead = lambda v: v[None]
    return (y_p.reshape(n_prompt, seq, D_MODEL), y_s.reshape(n_sample, dec_seq, D_MODEL),
            lead(c_p), lead(n_p), lead(m_p.reshape(n_prompt, N_HEADS)), lead(s_p), lead(conv_p),
            lead(c_s), lead(n_s), lead(m_s.reshape(n_sample, N_HEADS)), lead(s_s), lead(conv_s))
```

```python
import functools

import jax
import jax.numpy as jnp
from jax import lax
from jax.experimental import pallas as pl
from jax.experimental.pallas import tpu as pltpu

D_MODEL = 1024
N_META = 16
N_HEADS = 4
D_HEAD = 128
D_GROUP = N_HEADS * D_HEAD
D_FF = 2816
CONV_W = 3
DEPTH = 1
ALPHA = (2.0 * DEPTH) ** 0.25
LN_EPS = 1e-5
RMS_EPS = 1e-6
NEG_LOG2_E = -1.4426950408889634

LANES = 128
SUBLANES = 8
GATE_COL = 8 * D_GROUP
D_PROJ = GATE_COL + LANES
KB_GROUP, F_GROUP = 8, 9
D_ACT = D_PROJ + 2 * D_GROUP
LATE_GROUPS = (6, 7)
LATE_STAGE_COLS = 256
LATE_STAGES_PER_ROUND = 2
IN_PROJ_STAGE_COLS = 256
SIDE_STAGES_PER_ROUND = 3
PROMPT_SEQS_PER_STEP = 2
PROMPT_CHUNK = 128
VMEM_LIMIT = 56 * 1024 * 1024

F32 = jnp.float32
BF16 = jnp.bfloat16
NT_DIMS = (((1,), (1,)), ((), ()))
TN_DIMS = (((0,), (0,)), ((), ()))


def _layer_norm(x, g, b):
    mu = jnp.mean(x, axis=-1, keepdims=True)
    xc = x - mu
    var = jnp.mean(xc * xc, axis=-1, keepdims=True)
    return xc * lax.rsqrt(var + LN_EPS) * g + b


def _exp_neg(x):
    return jnp.exp2(x * NEG_LOG2_E)


def _sigmoid(x):
    return 1.0 / (1.0 + _exp_neg(x))


def _resident(block_shape, index_map):
    return pl.BlockSpec(block_shape, index_map, pipeline_mode=pl.Buffered(1))


def _dot(a, b):
    return jnp.dot(a, b, preferred_element_type=F32)


def _dot_nt(a, b):
    return lax.dot_general(a, b, NT_DIMS, preferred_element_type=F32)


def _dot_tn(a, b):
    return lax.dot_general(a, b, TN_DIMS, preferred_element_type=F32)


def _regroup_kernel(wt_ref, o_ref):
    gate0 = 4 * D_GROUP
    gate1 = gate0 + 2 * N_HEADS
    for j in range(GATE_COL // LANES):
        src = j * LANES if j * LANES < gate0 else j * LANES + (gate1 - gate0)
        o_ref[:, j * LANES:(j + 1) * LANES] = wt_ref[src:src + LANES, :].T.astype(o_ref.dtype)
    gates = wt_ref[gate0:gate1, :].T.astype(o_ref.dtype)
    o_ref[:, GATE_COL:] = jnp.concatenate(
        [gates, jnp.zeros((gates.shape[0], LANES - gates.shape[1]), o_ref.dtype)], axis=1)


def _regroup_in_proj_weight(w_t, *, tm):
    cols, n = w_t.shape
    return pl.pallas_call(
        _regroup_kernel,
        grid=(n // tm,),
        in_specs=[pl.BlockSpec((cols, tm), lambda i: (0, i))],
        out_specs=pl.BlockSpec((tm, D_PROJ), lambda i: (i, 0)),
        out_shape=jax.ShapeDtypeStruct((n, D_PROJ), BF16),
        compiler_params=pltpu.CompilerParams(dimension_semantics=("arbitrary",)),
        name="regroup_w_in",
    )(w_t)


def _in_proj_kernel(x_ref, g_ref, b_ref, w_ref, bias_ref, o_ref):
    xn = _layer_norm(x_ref[...], g_ref[...], b_ref[...])
    o_ref[...] = _dot(xn.astype(BF16), w_ref[...]) + bias_ref[...]


def _in_proj(x, ln_g, ln_b, w, bias, *, tm):
    n = x.shape[0]
    const = lambda i: (0, 0)
    return pl.pallas_call(
        _in_proj_kernel,
        grid=(n // tm,),
        in_specs=[
            pl.BlockSpec((tm, D_MODEL), lambda i: (i, 0)),
            pl.BlockSpec((1, D_MODEL), const),
            pl.BlockSpec((1, D_MODEL), const),
            pl.BlockSpec((D_MODEL, D_PROJ), const),
            pl.BlockSpec((1, D_PROJ), const),
        ],
        out_specs=pl.BlockSpec((tm, D_PROJ), lambda i: (i, 0)),
        out_shape=jax.ShapeDtypeStruct((n, D_PROJ), F32),
        compiler_params=pltpu.CompilerParams(
            dimension_semantics=("arbitrary",), vmem_limit_bytes=VMEM_LIMIT),
        name="in_proj",
    )(x, ln_g, ln_b, w, bias)


def _block_rows(x, level, t_len, row_in_block):
    size = 2 << level
    if size > SUBLANES:
        pieces = [jnp.broadcast_to(x[j * size + row_in_block:j * size + row_in_block + 1, :],
                                   (size, x.shape[1])) for j in range(t_len // size)]
        return pieces[0] if len(pieces) == 1 else jnp.concatenate(pieces, axis=0)
    x3 = x.reshape(t_len // SUBLANES, SUBLANES, x.shape[1])
    sub = lax.broadcasted_iota(jnp.int32, x3.shape, 1)
    out = None
    for j in range(SUBLANES // size):
        row = jnp.broadcast_to(x3[:, j * size + row_in_block:j * size + row_in_block + 1, :], x3.shape)
        out = row if out is None else jnp.where(sub >= j * size, row, out)
    return out.reshape(x.shape)


def _interleave_halves(lower, upper, level, t_len):
    half = 1 << level
    if half >= SUBLANES:
        pieces = []
        for j in range(t_len // (2 * half)):
            pieces.append(lower[2 * half * j:2 * half * j + half])
            pieces.append(upper[2 * half * j + half:2 * half * (j + 1)])
        return jnp.concatenate(pieces, axis=0)
    rows = lax.broadcasted_iota(jnp.int32, lower.shape, 0)
    return jnp.where((rows & half) != 0, upper, lower)


def _run_interleaved(gens, stages_per_round):
    results = [None] * len(gens)
    live = [True] * len(gens)
    while any(live):
        for g, steps in enumerate(stages_per_round):
            for _ in range(steps):
                if live[g]:
                    try:
                        next(gens[g])
                    except StopIteration as stop:
                        results[g], live[g] = stop.value, False
    return results


def _cumsum_rows(tril16, x):
    hi = x.astype(BF16)
    rest = x - hi.astype(F32)
    mid = rest.astype(BF16)
    lo = (rest - mid.astype(F32)).astype(BF16)
    return _dot(tril16, hi) + _dot(tril16, mid) + _dot(tril16, lo)


def _activate(group, x, lb=None):
    if group == 1:
        return {1: x * (D_HEAD ** -0.5)}
    if group in (3, 7):
        return {group: _sigmoid(x)}
    if group == 4:
        return {4: x * _sigmoid(x)}
    if group == 5:
        f = lb + (1.0 - lb) * _sigmoid(x)
        return {5: jnp.log(f), KB_GROUP: (1.0 - lb) / (1.0 + jnp.exp(x)), F_GROUP: f}
    return {group: x}


def _now(value):
    return value() if callable(value) else value


def _late_proj_stages(x16_ref, w_ref, bias_ref, lb_ref, out):
    for group in LATE_GROUPS:
        for off in range(0, D_GROUP, LATE_STAGE_COLS):
            yield
            cols = slice(group * D_GROUP + off, group * D_GROUP + off + LATE_STAGE_COLS)
            block = _dot(x16_ref[...], w_ref[:, cols]) + bias_ref[:, cols]
            out[group, off] = _activate(group, block, lb_ref[:, off:off + LATE_STAGE_COLS])[group]


def _mlstm_units(*, q, k, v, gate, z, zt, m0_row, c_old, n_old, g_norm, causal, lane, t_len):
    idx = range(len(q))
    seq = [i // N_HEADS for i in idx]
    b_lane = [N_HEADS + i % N_HEADS for i in idx]
    q16 = [q[i].astype(BF16) for i in idx]
    k16 = [k[i].astype(BF16) for i in idx]
    qk = [_dot_nt(q16[i], k16[i]) for i in idx]
    qc = [_dot(q16[i], c_old[i].astype(BF16)) for i in idx]
    bs_row = [zt[seq[i]][b_lane[i]:b_lane[i] + 1, :] - zt[seq[i]][i % N_HEADS:i % N_HEADS + 1, :]
              for i in idx]
    yield
    col = lambda per_seq, i: per_seq[seq[i]][:, b_lane[i]:b_lane[i] + 1]
    d = [jnp.where(causal, col(z, i) - bs_row[i], -jnp.inf) for i in idx]
    row_max = [jnp.max(d[i], axis=1, keepdims=True) for i in idx]
    last = slice(t_len - 1, t_len)
    m_t_seq, dec_seq, floor_seq, w_last_seq = [], [], [], []
    for s in range(len(z)):
        r = z[s] + m0_row[s]
        d_max = jnp.full(r.shape, -jnp.inf, F32)
        for h in range(N_HEADS):
            d_max = jnp.where(lane == N_HEADS + h, row_max[s * N_HEADS + h], d_max)
        m_t = jnp.maximum(r, d_max)
        i_gate = pltpu.roll(z[s], N_HEADS, axis=1)
        m_t_seq.append(m_t)
        dec_seq.append(jnp.exp(r - m_t))
        floor_seq.append(_exp_neg(m_t))
        w_last_seq.append(jnp.exp(z[s][last] - z[s] + i_gate - m_t[last]))
    m_t = [col(m_t_seq, i) for i in idx]
    dec = [col(dec_seq, i) for i in idx]
    w_last = [col(w_last_seq, i) for i in idx]
    sw = [jnp.exp(d[i] - m_t[i]) * qk[i] for i in idx]
    yield
    swv = [_dot(sw[i].astype(BF16), v[i].astype(BF16)) for i in idx]
    kv = [_dot_tn(k16[i], (w_last[i] * v[i]).astype(BF16)) for i in idx]
    yield
    c_new = [dec[i][last] * c_old[i] + kv[i] for i in idx]
    n_new = [dec[i][last] * n_old[i] + jnp.sum(w_last[i] * k[i], axis=0, keepdims=True) for i in idx]
    m_new = [m_t[i][last] for i in idx]
    den = [dec[i] * jnp.sum(q[i] * n_old[i], axis=1, keepdims=True)
           + jnp.sum(sw[i], axis=1, keepdims=True) for i in idx]
    hid = [(dec[i] * qc[i] + swv[i]) / jnp.maximum(jnp.abs(den[i]), col(floor_seq, i)) for i in idx]
    yield
    rms = [lax.rsqrt(jnp.mean(hid[i] * hid[i], axis=1, keepdims=True) + RMS_EPS) for i in idx]
    out = [_now(gate[i]) * (hid[i] * rms[i] * g_norm[i]) for i in idx]
    return out, c_new, n_new, m_new


def _hgrn_units(*, qb, log_f, f, kb, iv, gate, s_old_t, g_norm, tril, level_of, t_len):
    idx = range(len(qb))
    n_levels = t_len.bit_length() - 1
    a = [_cumsum_rows(tril, log_f[i]) for i in idx]
    yield
    diag =[_dot_nt(qb[i].astype(BF16), kb[i].astype(BF16)) for i in idx]
    scores = [jnp.where(level_of == -2, diag[i], 0.0) for i in idx]
    for level in range(n_levels):
        yield
        x16 = []
        for i in idx:
            base = _interleave_halves(kb[i], qb[i], level, t_len)
            if level == 0:
                x = base * _interleave_halves(jnp.ones_like(f[i]), f[i], 0, t_len)
            else:
                ref = _block_rows(a[i], level, t_len, (1 << level) - 1)
                x = base * _exp_neg(jnp.abs(a[i] - ref))
            x16.append(x.astype(BF16))
        part = [_dot_nt(x16[i], x16[i]) for i in idx]
        scores = [jnp.where(level_of == level, part[i], scores[i]) for i in idx]
    yield
    last = slice(t_len - 1, t_len)
    q_in = [(qb[i] * jnp.exp(a[i])).astype(BF16) for i in idx]
    k_out = [(kb[i] * jnp.exp(a[i][last] - a[i])).astype(BF16) for i in idx]
    inter = [_dot_nt(q_in[i], s_old_t[i].astype(BF16)) for i in idx]
    iv16 = [_now(iv[i]).astype(BF16) for i in idx]
    intra = [_dot(scores[i].astype(BF16), iv16[i]) for i in idx]
    kv = [_dot_tn(iv16[i], k_out[i]) for i in idx]
    yield
    s_new_t = [jnp.exp(a[i][last]) * s_old_t[i] + kv[i] for i in idx]
    o = [inter[i] + intra[i] for i in idx]
    rms = [lax.rsqrt(jnp.mean(o[i] * o[i], axis=1, keepdims=True) + RMS_EPS) for i in idx]
    out = [_now(gate[i]) * (o[i] * rms[i] * g_norm[i]) for i in idx]
    return out, s_new_t


def _load_state(c0_ref, n0_ref, m0_ref, s0_ref, c_ref, n_ref, m_ref, s_ref, n_seq_blk):
    shared = c0_ref.shape[0] == 1 and n_seq_blk > 1
    for s in range(n_seq_blk):
        src = 0 if shared else s
        c_ref[s] = c0_ref[src]
        n_ref[s] = n0_ref[src]
        m_ref[s] = m0_ref[src]
        for h in range(N_HEADS):
            s_ref[s, h] = s0_ref[src, h].T


def _finish_state(s_ref, n_seq_blk):
    for s in range(n_seq_blk):
        for h in range(N_HEADS):
            s_ref[s, h] = s_ref[s, h].T


def _act_cols(group, h):
    base = group * D_GROUP if group < KB_GROUP else D_PROJ + (group - KB_GROUP) * D_GROUP
    return slice(base + h * D_HEAD, base + (h + 1) * D_HEAD)


def _mixer_body(p_ref, bf_ref, ga_ref, gb_ref, lb_ref, c_ref, n_ref, m_ref, s_ref,
                *, t_len, n_seq_blk, activated, side_stages=None, late_proj=None):
    rows = lax.broadcasted_iota(jnp.int32, (t_len, t_len), 0)
    cols = lax.broadcasted_iota(jnp.int32, (t_len, t_len), 1)
    causal = cols <= rows
    tril = causal.astype(BF16)
    level_of = jnp.where(rows > cols, 31 - lax.clz(rows ^ cols), jnp.where(rows == cols, -2, -1))
    lane = lax.broadcasted_iota(jnp.int32, (t_len, LANES), 1)
    is_f = (lane >= N_HEADS) & (lane < 2 * N_HEADS)
    hd = lambda j, h: slice(j * D_GROUP + h * D_HEAD, j * D_GROUP + (h + 1) * D_HEAD)

    units = [(s, h) for s in range(n_seq_blk) for h in range(N_HEADS)]
    rs = lambda s: slice(s * t_len, (s + 1) * t_len)

    z_seq, zt_seq, n_seq, m_seq = [], [], [], []
    for s in range(n_seq_blk):
        gates = p_ref[rs(s), GATE_COL:GATE_COL + LANES]
        log_f = jnp.where(is_f, jax.nn.log_sigmoid(gates + bf_ref[...]), 0.0)
        cum_f = _cumsum_rows(tril, log_f)
        z_seq.append(jnp.where(is_f, cum_f, jnp.where(lane < N_HEADS, gates, 0.0)))
        zt_seq.append(z_seq[s].T)
        n_seq.append(n_ref[s])
        m_seq.append(m_ref[s])
    computed, late_vals = {}, {}

    def act(group):
        source = 5 if group >= KB_GROUP else group
        if late_proj is not None and source in LATE_GROUPS:
            def thunk(s, h):
                off, col = divmod(h * D_HEAD, LATE_STAGE_COLS)
                return lambda: late_vals[source, off * LATE_STAGE_COLS][rs(s), col:col + D_HEAD]
            return [thunk(s, h) for s, h in units]
        if activated:
            return [p_ref[rs(s), _act_cols(group, h)] for s, h in units]
        for s, h in units:
            if (source, s, h) not in computed:
                computed[source, s, h] = _activate(source, p_ref[rs(s), hd(source, h)],
                                                   lb_ref[:, hd(0, h)])
        return [computed[source, s, h][group] for s, h in units]

    lane_row = lane[:1, :]
    m0_rows = []
    for s in range(n_seq_blk):
        m0_row = jnp.zeros((1, LANES), F32)
        for h in range(N_HEADS):
            m0_row = jnp.where(lane_row == N_HEADS + h, m_seq[s][:, h:h + 1], m0_row)
        m0_rows.append(m0_row)
    mlstm = _mlstm_units(
        q=act(0), k=act(1), v=act(2), gate=act(3), z=z_seq, zt=zt_seq, m0_row=m0_rows,
        c_old=[c_ref[s, h] for s, h in units], n_old=[n_seq[s][h:h + 1, :] for s, h in units],
        g_norm=[ga_ref[:, hd(0, h)] for s, h in units], causal=causal, lane=lane, t_len=t_len)
    hgrn = _hgrn_units(
        qb=act(4), log_f=act(5), f=act(F_GROUP), kb=act(KB_GROUP), iv=act(6), gate=act(7),
        s_old_t=[s_ref[s, h] for s, h in units],
        g_norm=[gb_ref[:, hd(0, h)] for s, h in units], tril=tril, level_of=level_of, t_len=t_len)
    gens, per_round = [mlstm, hgrn], [1, 2]
    if late_proj is not None:
        late = _late_proj_stages(*late_proj, lb_ref, late_vals)
        gens, per_round = [late] + gens, [LATE_STAGES_PER_ROUND] + per_round
    if side_stages is not None:
        gens, per_round = gens + [side_stages], per_round + [SIDE_STAGES_PER_ROUND]
    results = _run_interleaved(gens, per_round)
    outs, c_new, n_new, m_new = results[gens.index(mlstm)]
    outs_b, s_new_t = results[gens.index(hgrn)]

    for i, (s, h) in enumerate(units):
        c_ref[s, h] = c_new[i]
        s_ref[s, h] = s_new_t[i]
    head_lane = lax.broadcasted_iota(jnp.int32, (1, N_HEADS), 1)
    for s in range(n_seq_blk):
        n_ref[s] = jnp.concatenate(n_new[s * N_HEADS:(s + 1) * N_HEADS], axis=0)
        m_row = m_seq[s]
        for h in range(N_HEADS):
            m_row = jnp.where(head_lane == h, m_new[s * N_HEADS + h], m_row)
        m_ref[s] = m_row
    return outs, outs_b


def _mixer_kernel(p_ref, c0_ref, n0_ref, m0_ref, s0_ref, bf_ref, ga_ref, gb_ref, lb_ref,
                  mix_ref, c_ref, n_ref, m_ref, s_ref, *, t_len, n_chunks, n_seq_blk):
    chunk = pl.program_id(1)

    @pl.when(chunk == 0)
    def _():
        _load_state(c0_ref, n0_ref, m0_ref, s0_ref, c_ref, n_ref, m_ref, s_ref, n_seq_blk)

    outs_a, outs_b = _mixer_body(p_ref, bf_ref, ga_ref, gb_ref, lb_ref, c_ref, n_ref, m_ref, s_ref,
                                 t_len=t_len, n_seq_blk=n_seq_blk, activated=False)
    for i in range(n_seq_blk * N_HEADS):
        s, h = divmod(i, N_HEADS)
        rows = slice(s * t_len, (s + 1) * t_len)
        mix_ref[rows, h * D_HEAD:(h + 1) * D_HEAD] = outs_a[i].astype(mix_ref.dtype)
        mix_ref[rows, D_GROUP + h * D_HEAD:D_GROUP + (h + 1) * D_HEAD] = outs_b[i].astype(mix_ref.dtype)

    @pl.when(chunk == n_chunks - 1)
    def _():
        _finish_state(s_ref, n_seq_blk)


def _mixer(proj, c0, n0, m0, s0, bf_row, ga, gb, lb, *, n_seq, n_chunks, t_len, n_seq_blk,
           shared_init):
    assert not shared_init or n_seq_blk == 1
    assert n_chunks == 1 or n_seq_blk == 1
    nb = n_seq_blk
    init = (lambda b, c: (0, 0, 0, 0)) if shared_init else (lambda b, c: (b, 0, 0, 0))
    init3 = (lambda b, c: (0, 0, 0)) if shared_init else (lambda b, c: (b, 0, 0))
    const = lambda b, c: (0, 0)
    state4 = pl.BlockSpec((nb, N_HEADS, D_HEAD, D_HEAD), lambda b, c: (b, 0, 0, 0))
    return pl.pallas_call(
        functools.partial(_mixer_kernel, t_len=t_len, n_chunks=n_chunks, n_seq_blk=nb),
        grid=(n_seq // nb, n_chunks),
        in_specs=[
            pl.BlockSpec((nb * t_len, D_PROJ), lambda b, c: (b * n_chunks + c, 0)),
            pl.BlockSpec((nb, N_HEADS, D_HEAD, D_HEAD), init),
            pl.BlockSpec((nb, N_HEADS, D_HEAD), init3),
            pl.BlockSpec((nb, 1, N_HEADS), init3),
            pl.BlockSpec((nb, N_HEADS, D_HEAD, D_HEAD), init),
            pl.BlockSpec((1, LANES), const),
            pl.BlockSpec((1, D_GROUP), const),
            pl.BlockSpec((1, D_GROUP), const),
            pl.BlockSpec((1, D_GROUP), const),
        ],
        out_specs=[
            pl.BlockSpec((nb * t_len, D_MODEL), lambda b, c: (b * n_chunks + c, 0)),
            state4,
            pl.BlockSpec((nb, N_HEADS, D_HEAD), lambda b, c: (b, 0, 0)),
            pl.BlockSpec((nb, 1, N_HEADS), lambda b, c: (b, 0, 0)),
            state4,
        ],
        out_shape=[
            jax.ShapeDtypeStruct((n_seq * n_chunks * t_len, D_MODEL), BF16),
            jax.ShapeDtypeStruct((n_seq, N_HEADS, D_HEAD, D_HEAD), F32),
            jax.ShapeDtypeStruct((n_seq, N_HEADS, D_HEAD), F32),
            jax.ShapeDtypeStruct((n_seq, 1, N_HEADS), F32),
            jax.ShapeDtypeStruct((n_seq, N_HEADS, D_HEAD, D_HEAD), F32),
        ],
        compiler_params=pltpu.CompilerParams(
            dimension_semantics=("arbitrary", "arbitrary"), vmem_limit_bytes=VMEM_LIMIT),
        name=f"mixer_t{t_len}",
    )(proj, c0, n0, m0, s0, bf_row, ga, gb, lb)


def _meta_kernel(x_ref, ge_ref, be_ref, win_ref, bin_ref, bf_ref, ga_ref, gb_ref, lb_ref,
                 wout_ref, bout_ref, g1_ref, b1_ref, wu_ref, bu_ref,
                 c_ref, n_ref, m_ref, s_ref, conv_ref, proj_scr):
    t_len = x_ref.shape[0]
    xn = _layer_norm(x_ref[...], ge_ref[...], be_ref[...])
    proj_scr[...] = _dot(xn.astype(BF16), win_ref[...]) + bin_ref[...]
    for ref in (c_ref, n_ref, m_ref, s_ref):
        ref[...] = jnp.zeros(ref.shape, ref.dtype)
    outs_a, outs_b = _mixer_body(proj_scr, bf_ref, ga_ref, gb_ref, lb_ref, c_ref, n_ref, m_ref, s_ref,
                                 t_len=t_len, n_seq_blk=1, activated=False)
    _finish_state(s_ref, 1)
    mix = jnp.concatenate([o.astype(BF16) for o in outs_a + outs_b], axis=1)
    y = _dot(mix, wout_ref[...]) + bout_ref[...]
    x1 = _layer_norm(ALPHA * xn + y, g1_ref[...], b1_ref[...])
    u = _dot(x1.astype(BF16), wu_ref[...]) + bu_ref[...]
    conv_ref[0] = u[t_len - (CONV_W - 1):, :]


def _meta_state(x, ln_e_g, ln_e_b, w_in, b_in, bf_row, ga, gb, lb, w_out, b_out, ln_g, ln_b,
                w_up, b_up):
    t_len = x.shape[0]
    full = lambda shape: _resident(shape, lambda i: (0,) * len(shape))
    vec, grp = full((1, D_MODEL)), full((1, D_GROUP))
    state4 = (1, N_HEADS, D_HEAD, D_HEAD)
    return pl.pallas_call(
        _meta_kernel,
        grid=(1,),
        in_specs=[full((t_len, D_MODEL)), vec, vec, full((D_MODEL, D_PROJ)), full((1, D_PROJ)),
                  full((1, LANES)), grp, grp, grp, full((D_MODEL, D_MODEL)), vec, vec, vec,
                  full((D_MODEL, D_FF)), full((1, D_FF))],
        out_specs=[pl.BlockSpec(shape, lambda i, rank=len(shape): (0,) * rank)
                   for shape in (state4, (1, N_HEADS, D_HEAD), (1, 1, N_HEADS), state4,
                                 (1, CONV_W - 1, D_FF))],
        out_shape=[
            jax.ShapeDtypeStruct(state4, F32),
            jax.ShapeDtypeStruct((1, N_HEADS, D_HEAD), F32),
            jax.ShapeDtypeStruct((1, 1, N_HEADS), F32),
            jax.ShapeDtypeStruct(state4, F32),
            jax.ShapeDtypeStruct((1, CONV_W - 1, D_FF), F32),
        ],
        scratch_shapes=[pltpu.VMEM((t_len, D_PROJ), F32)],
        compiler_params=pltpu.CompilerParams(
            dimension_semantics=("arbitrary",), vmem_limit_bytes=VMEM_LIMIT),
        name="meta_state",
    )(x, ln_e_g, ln_e_b, w_in, b_in, bf_row, ga, gb, lb, w_out, b_out, ln_g, ln_b, w_up, b_up)


def _in_proj_stages(x_ref, g_ref, b_ref, w_ref, bias_ref, lb_ref, act_ref, xn_ref, x16_ref):
    assert D_GROUP % IN_PROJ_STAGE_COLS == 0
    xn = _layer_norm(x_ref[...].reshape(xn_ref.shape), g_ref[...], b_ref[...])
    xn_ref[...] = xn
    x16 = xn.astype(BF16)
    x16_ref[...] = x16
    starts = [lo for lo in range(0, GATE_COL, IN_PROJ_STAGE_COLS) if lo // D_GROUP not in LATE_GROUPS]
    for lo in starts + [GATE_COL]:
        hi = min(lo + IN_PROJ_STAGE_COLS, D_PROJ)
        yield
        block = _dot(x16, w_ref[:, lo:hi]) + bias_ref[:, lo:hi]
        group, off = divmod(lo, D_GROUP)
        if lo >= GATE_COL:
            act_ref[:, lo:hi] = block
            continue
        for dst, val in _activate(group, block, lb_ref[:, off:off + hi - lo]).items():
            base = _act_cols(dst, 0).start + off
            act_ref[:, base:base + hi - lo] = val


def _prompt_kernel(x0_ref, xnext_ref, ge_ref, be_ref, win_ref, bin_ref, c0_ref, n0_ref, m0_ref, s0_ref,
                   bf_ref, ga_ref, gb_ref, lb_ref, wout_ref, bout_ref, g1_ref, b1_ref,
                   x1_ref, c_ref, n_ref, m_ref, s_ref,
                   proj_scr, xn_scr, x16_scr, proj_alt, xn_alt, x16_alt,
                   *, t_len, n_chunks, n_seq_blk):
    chunk = pl.program_id(1)
    step = pl.program_id(0) * n_chunks + chunk

    @pl.when(step == 0)
    def _():
        first = _in_proj_stages(x0_ref, ge_ref, be_ref, win_ref, bin_ref, lb_ref,
                                proj_scr, xn_scr, x16_scr)
        _run_interleaved([first], [1])

    @pl.when(chunk == 0)
    def _():
        _load_state(c0_ref, n0_ref, m0_ref, s0_ref, c_ref, n_ref, m_ref, s_ref, n_seq_blk)

    def tile(proj_cur, xn_cur, x16_cur, proj_next, xn_next, x16_next):
        next_proj = _in_proj_stages(xnext_ref, ge_ref, be_ref, win_ref, bin_ref, lb_ref,
                                    proj_next, xn_next, x16_next)
        outs_a, outs_b = _mixer_body(
            proj_cur, bf_ref, ga_ref, gb_ref, lb_ref, c_ref, n_ref, m_ref, s_ref,
            t_len=t_len, n_seq_blk=n_seq_blk, activated=True, side_stages=next_proj,
            late_proj=(x16_cur, win_ref, bin_ref))
        mix = jnp.concatenate(
            [jnp.concatenate([o.astype(BF16) for o in outs_a[s * N_HEADS:(s + 1) * N_HEADS]
                              + outs_b[s * N_HEADS:(s + 1) * N_HEADS]], axis=1)
             for s in range(n_seq_blk)], axis=0)
        y = _dot(mix, wout_ref[...]) + bout_ref[...]
        x1 = _layer_norm(ALPHA * xn_cur[...] + y, g1_ref[...], b1_ref[...])
        x1_ref[...] = x1.reshape(x1_ref.shape)

    @pl.when(step % 2 == 0)
    def _():
        tile(proj_scr, xn_scr, x16_scr, proj_alt, xn_alt, x16_alt)

    @pl.when(step % 2 == 1)
    def _():
        tile(proj_alt, xn_alt, x16_alt, proj_scr, xn_scr, x16_scr)

    @pl.when(chunk == n_chunks - 1)
    def _():
        _finish_state(s_ref, n_seq_blk)


def _prompt_mixer(x, ln_e_g, ln_e_b, w_in, b_in, c0, n0, m0, s0, bf_row, ga, gb, lb,
                  w_out, b_out, ln_g, ln_b, *, n_chunks, t_len, n_seq_blk):
    n_seq = x.shape[0]
    nb = n_seq_blk
    n_tiles = (n_seq // nb) * n_chunks
    const = lambda b, c: (0, 0)
    init4 = lambda b, c: (0, 0, 0, 0)
    init3 = lambda b, c: (0, 0, 0)
    vec = pl.BlockSpec((1, D_MODEL), const)
    grp = pl.BlockSpec((1, D_GROUP), const)
    state4 = pl.BlockSpec((nb, N_HEADS, D_HEAD, D_HEAD), lambda b, c: (b, 0, 0, 0))

    def next_tile(b, c):
        nxt = jnp.minimum(b * n_chunks + c + 1, n_tiles - 1)
        return (nxt // n_chunks, nxt % n_chunks, 0)

    return pl.pallas_call(
        functools.partial(_prompt_kernel, t_len=t_len, n_chunks=n_chunks, n_seq_blk=nb),
        grid=(n_seq // nb, n_chunks),
        in_specs=[
            _resident((nb, t_len, D_MODEL), init3),
            pl.BlockSpec((nb, t_len, D_MODEL), next_tile),
            vec, vec,
            _resident((D_MODEL, D_PROJ), const),
            pl.BlockSpec((1, D_PROJ), const),
            pl.BlockSpec((1, N_HEADS, D_HEAD, D_HEAD), init4),
            pl.BlockSpec((1, N_HEADS, D_HEAD), init3),
            pl.BlockSpec((1, 1, N_HEADS), init3),
            pl.BlockSpec((1, N_HEADS, D_HEAD, D_HEAD), init4),
            pl.BlockSpec((1, LANES), const),
            grp, grp, grp,
            _resident((D_MODEL, D_MODEL), const),
            vec, vec, vec,
        ],
        out_specs=[
            pl.BlockSpec((nb, t_len, D_MODEL), lambda b, c: (b, c, 0)),
            state4,
            pl.BlockSpec((nb, N_HEADS, D_HEAD), lambda b, c: (b, 0, 0)),
            pl.BlockSpec((nb, 1, N_HEADS), lambda b, c: (b, 0, 0)),
            state4,
        ],
        out_shape=[
            jax.ShapeDtypeStruct(x.shape, F32),
            jax.ShapeDtypeStruct((n_seq, N_HEADS, D_HEAD, D_HEAD), F32),
            jax.ShapeDtypeStruct((n_seq, N_HEADS, D_HEAD), F32),
            jax.ShapeDtypeStruct((n_seq, 1, N_HEADS), F32),
            jax.ShapeDtypeStruct((n_seq, N_HEADS, D_HEAD, D_HEAD), F32),
        ],
        scratch_shapes=2 * [pltpu.VMEM((nb * t_len, D_ACT), F32), pltpu.VMEM((nb * t_len, D_MODEL), F32),
                            pltpu.VMEM((nb * t_len, D_MODEL), BF16)],
        compiler_params=pltpu.CompilerParams(
            dimension_semantics=("arbitrary", "arbitrary"), vmem_limit_bytes=VMEM_LIMIT),
        name="prompt_mixer",
    )(x, x, ln_e_g, ln_e_b, w_in, b_in, c0, n0, m0, s0, bf_row, ga, gb, lb, w_out, b_out, ln_g, ln_b)


def _ffn_kernel(x_ref, *refs, n_seq_blk, t_len):
    _ffn_tile(x_ref[...], *refs, n_seq_blk=n_seq_blk, t_len=t_len)


def _out_proj_ffn_kernel(x_ref, mix_ref, ge_ref, be_ref, wo_ref, bo_ref, g1_ref, b1_ref, *refs,
                         n_seq_blk, t_len):
    xn = _layer_norm(x_ref[...], ge_ref[...], be_ref[...])
    y = _dot(mix_ref[...], wo_ref[...]) + bo_ref[...]
    x1 = _layer_norm(ALPHA * xn + y, g1_ref[...], b1_ref[...])
    _ffn_tile(x1, *refs, n_seq_blk=n_seq_blk, t_len=t_len)


def _ffn_tile(x, cs_ref, wu_ref, bu_ref, wc_ref, bc_ref, wd_ref, bd_ref, g_ref, b_ref,
              y_ref, nc_ref, full_ref, *, n_seq_blk, t_len):
    hist = SUBLANES - (CONV_W - 1)

    @pl.when(pl.program_id(1) == 0)
    def _():
        full_ref[:, hist:SUBLANES, :] = cs_ref[...]

    up = _dot(x.astype(BF16), wu_ref[...]) + bu_ref[...]
    u = up[:, :D_FF].reshape(n_seq_blk, t_len, D_FF)
    gate = up[:, D_FF:].reshape(n_seq_blk, t_len, D_FF)
    full_ref[:, SUBLANES:SUBLANES + t_len, :] = u
    conv = bc_ref[...] + u * wc_ref[CONV_W - 1:CONV_W, :]
    for j in range(CONV_W - 1):
        conv = conv + full_ref[:, hist + j:hist + j + t_len, :] * wc_ref[j:j + 1, :]
    last = full_ref[:, hist + t_len:SUBLANES + t_len, :]
    nc_ref[...] = last
    full_ref[:, hist:SUBLANES, :] = last
    act = (conv * _sigmoid(conv) * gate).reshape(n_seq_blk * t_len, D_FF)
    ffn = _dot(act.astype(BF16), wd_ref[...]) + bd_ref[...]
    y_ref[...] = _layer_norm(ALPHA * x + ffn, g_ref[...], b_ref[...])


def _ffn(x, conv_state, w_up, b_up, w_conv, b_conv, w_down, b_down, ln_g, ln_b,
         *, n_seq, seq_len, n_seq_blk, t_len, shared_init, out_proj=None):
    n_t = seq_len // t_len
    rows = n_seq_blk * t_len
    const = lambda s, t: (0, 0)
    cs_map = (lambda s, t: (0, 0, 0)) if shared_init else (lambda s, t: (s, 0, 0))
    row = pl.BlockSpec((rows, D_MODEL), lambda s, t: (s * n_t + t, 0))
    vec = pl.BlockSpec((1, D_MODEL), const)
    body, lead_specs, lead_args = _ffn_kernel, [row], (x,)
    if out_proj is not None:
        body = _out_proj_ffn_kernel
        lead_specs = [row, row, vec, vec, _resident((D_MODEL, D_MODEL), const), vec, vec, vec]
        lead_args = (x,) + tuple(out_proj)
    return pl.pallas_call(
        functools.partial(body, n_seq_blk=n_seq_blk, t_len=t_len),
        grid=(n_seq // n_seq_blk, n_t),
        in_specs=lead_specs + [
            pl.BlockSpec((n_seq_blk, CONV_W - 1, D_FF), cs_map),
            _resident((D_MODEL, 2 * D_FF), const),
            pl.BlockSpec((1, 2 * D_FF), const),
            pl.BlockSpec((CONV_W, D_FF), const),
            pl.BlockSpec((1, D_FF), const),
            _resident((D_FF, D_MODEL), const),
            vec, vec, vec,
        ],
        out_specs=[row, pl.BlockSpec((n_seq_blk, CONV_W - 1, D_FF), lambda s, t: (s, 0, 0))],
        out_shape=[
            jax.ShapeDtypeStruct((n_seq * seq_len, D_MODEL), F32),
            jax.ShapeDtypeStruct((n_seq, CONV_W - 1, D_FF), F32),
        ],
        scratch_shapes=[pltpu.VMEM((n_seq_blk, SUBLANES + t_len, D_FF), F32)],
        compiler_params=pltpu.CompilerParams(
            dimension_semantics=("arbitrary", "arbitrary"), vmem_limit_bytes=VMEM_LIMIT),
        name=f"ffn_t{t_len}",
    )(*lead_args, conv_state, w_up, b_up, w_conv, b_conv, w_down, b_down, ln_g, ln_b)


def kernel(x_prompt, x_sample, state_mlstm_C, state_mlstm_n, state_mlstm_m, state_hgrn_S, state_ffn_conv, meta_tokens, ln_emb_g, ln_emb_b, w_in, b_in, b_fgate_a, g_norm_a, g_norm_b, hgrn_lb_logits, w_out, b_out, ln1_g, ln1_b, w_up, b_up, w_conv, b_conv, w_down, b_down, ln2_g, ln2_b):
    assert w_in.shape[0] == DEPTH == 1
    n_prompt, seq, _ = x_prompt.shape
    n_sample, dec_seq, _ = x_sample.shape
    row = lambda v: v.reshape(1, -1).astype(F32)

    gate0 = 4 * D_GROUP
    gate1 = gate0 + 2 * N_HEADS
    pad = D_PROJ - w_in.shape[2]
    w_in_p = _regroup_in_proj_weight(w_in[0].T, tm=256)
    b_in_p = jnp.concatenate(
        [b_in[0][:gate0], b_in[0][gate1:], b_in[0][gate0:gate1], jnp.zeros((pad,), b_in.dtype)]
    ).reshape(1, D_PROJ).astype(F32)
    bf_row = jnp.zeros((1, LANES), F32).at[0, N_HEADS:2 * N_HEADS].set(b_fgate_a[0].astype(F32))
    lb = jnp.cumsum(jax.nn.softmax(hgrn_lb_logits.astype(F32), axis=0), axis=0)[0].reshape(1, D_GROUP)
    ga, gb = row(g_norm_a[0]), row(g_norm_b[0])
    ln_e = (row(ln_emb_g), row(ln_emb_b))
    out_p = (w_out[0].astype(BF16), row(b_out[0]), row(ln1_g[0]), row(ln1_b[0]))
    ffn_p = (w_up[0].astype(BF16), row(b_up[0]), w_conv[0].astype(F32), row(b_conv[0]),
             w_down[0].astype(BF16), row(b_down[0]), row(ln2_g[0]), row(ln2_b[0]))

    c_m, n_m, m_m, s_m, conv_m = _meta_state(
        meta_tokens.astype(F32), *ln_e, w_in_p, b_in_p, bf_row, ga, gb, lb, *out_p, ffn_p[0], ffn_p[1])

    x1_p, c_p, n_p, m_p, s_p = _prompt_mixer(
        x_prompt.astype(F32), *ln_e, w_in_p, b_in_p, c_m, n_m, m_m, s_m, bf_row, ga, gb, lb, *out_p,
        n_chunks=seq // PROMPT_CHUNK, t_len=PROMPT_CHUNK, n_seq_blk=PROMPT_SEQS_PER_STEP)
    y_p, conv_p = _ffn(x1_p.reshape(n_prompt * seq, D_MODEL), conv_m, *ffn_p, n_seq=n_prompt,
                       seq_len=seq, n_seq_blk=1, t_len=512, shared_init=True)

    sample_state = (state_mlstm_C[0].astype(F32), state_mlstm_n[0].astype(F32),
                    state_mlstm_m[0].astype(F32).reshape(n_sample, 1, N_HEADS),
                    state_hgrn_S[0].astype(F32))
    xs_rows = x_sample.reshape(n_sample * dec_seq, D_MODEL).astype(F32)
    proj_s = _in_proj(xs_rows, *ln_e, w_in_p, b_in_p, tm=256)
    mix_s, c_s, n_s, m_s, s_s = _mixer(
        proj_s, *sample_state, bf_row, ga, gb, lb, n_seq=n_sample, n_chunks=1, t_len=dec_seq,
        n_seq_blk=8, shared_init=False)
    y_s, conv_s = _ffn(xs_rows, state_ffn_conv[0].astype(F32), *ffn_p, n_seq=n_sample,
                       seq_len=dec_seq, n_seq_blk=32, t_len=dec_seq, shared_init=False,
                       out_proj=(mix_s, *ln_e, *out_p))

    lead = lambda v: v[None]
    return (y_p.reshape(n_prompt, seq, D_MODEL), y_s.reshape(n_sample, dec_seq, D_MODEL),
            lead(c_p), lead(n_p), lead(m_p.reshape(n_prompt, N_HEADS)), lead(s_p), lead(conv_p),
            lead(c_s), lead(n_s), lead(m_s.reshape(n_sample, N_HEADS)), lead(s_s), lead(conv_s))
```

```python
import functools

import jax
import jax.numpy as jnp
from jax import lax
from jax.experimental import pallas as pl
from jax.experimental.pallas import tpu as pltpu

D_MODEL = 1024
N_META = 16
N_HEADS = 4
D_HEAD = 128
D_GROUP = N_HEADS * D_HEAD
D_FF = 2816
CONV_W = 3
DEPTH = 1
ALPHA = (2.0 * DEPTH) ** 0.25
LN_EPS = 1e-5
RMS_EPS = 1e-6
NEG_LOG2_E = -1.4426950408889634

LANES = 128
SUBLANES = 8
GATE_COL = 8 * D_GROUP
D_PROJ = GATE_COL + LANES
KB_GROUP, F_GROUP = 8, 9
D_ACT = D_PROJ + 2 * D_GROUP
LATE_GROUPS = (3, 6, 7)
LATE_STAGE_COLS = 512
LATE_STAGES_PER_ROUND = 1
IN_PROJ_STAGE_COLS = 256
SIDE_STAGES_PER_ROUND = 3
PROMPT_SEQS_PER_STEP = 2
PROMPT_CHUNK = 128
VMEM_LIMIT = 56 * 1024 * 1024

F32 = jnp.float32
BF16 = jnp.bfloat16
NT_DIMS = (((1,), (1,)), ((), ()))
TN_DIMS = (((0,), (0,)), ((), ()))


def _layer_norm(x, g, b):
    mu = jnp.mean(x, axis=-1, keepdims=True)
    xc = x - mu
    var = jnp.mean(xc * xc, axis=-1, keepdims=True)
    return xc * lax.rsqrt(var + LN_EPS) * g + b


def _exp_neg(x):
    return jnp.exp2(x * NEG_LOG2_E)


def _sigmoid(x):
    return 1.0 / (1.0 + _exp_neg(x))


def _resident(block_shape, index_map):
    return pl.BlockSpec(block_shape, index_map, pipeline_mode=pl.Buffered(1))


def _dot(a, b):
    return jnp.dot(a, b, preferred_element_type=F32)


def _dot_nt(a, b):
    return lax.dot_general(a, b, NT_DIMS, preferred_element_type=F32)


def _dot_tn(a, b):
    return lax.dot_general(a, b, TN_DIMS, preferred_element_type=F32)


def _regroup_kernel(wt_ref, o_ref):
    gate0 = 4 * D_GROUP
    gate1 = gate0 + 2 * N_HEADS
    for j in range(GATE_COL // LANES):
        src = j * LANES if j * LANES < gate0 else j * LANES + (gate1 - gate0)
        o_ref[:, j * LANES:(j + 1) * LANES] = wt_ref[src:src + LANES, :].T.astype(o_ref.dtype)
    gates = wt_ref[gate0:gate1, :].T.astype(o_ref.dtype)
    o_ref[:, GATE_COL:] = jnp.concatenate(
        [gates, jnp.zeros((gates.shape[0], LANES - gates.shape[1]), o_ref.dtype)], axis=1)


def _regroup_in_proj_weight(w_t, *, tm):
    cols, n = w_t.shape
    return pl.pallas_call(
        _regroup_kernel,
        grid=(n // tm,),
        in_specs=[pl.BlockSpec((cols, tm), lambda i: (0, i))],
        out_specs=pl.BlockSpec((tm, D_PROJ), lambda i: (i, 0)),
        out_shape=jax.ShapeDtypeStruct((n, D_PROJ), BF16),
        compiler_params=pltpu.CompilerParams(dimension_semantics=("arbitrary",)),
        name="regroup_w_in",
    )(w_t)


def _in_proj_kernel(x_ref, g_ref, b_ref, w_ref, bias_ref, o_ref):
    xn = _layer_norm(x_ref[...], g_ref[...], b_ref[...])
    o_ref[...] = _dot(xn.astype(BF16), w_ref[...]) + bias_ref[...]


def _in_proj(x, ln_g, ln_b, w, bias, *, tm):
    n = x.shape[0]
    const = lambda i: (0, 0)
    return pl.pallas_call(
        _in_proj_kernel,
        grid=(n // tm,),
        in_specs=[
            pl.BlockSpec((tm, D_MODEL), lambda i: (i, 0)),
            pl.BlockSpec((1, D_MODEL), const),
            pl.BlockSpec((1, D_MODEL), const),
            pl.BlockSpec((D_MODEL, D_PROJ), const),
            pl.BlockSpec((1, D_PROJ), const),
        ],
        out_specs=pl.BlockSpec((tm, D_PROJ), lambda i: (i, 0)),
        out_shape=jax.ShapeDtypeStruct((n, D_PROJ), F32),
        compiler_params=pltpu.CompilerParams(
            dimension_semantics=("arbitrary",), vmem_limit_bytes=VMEM_LIMIT),
        name="in_proj",
    )(x, ln_g, ln_b, w, bias)


def _block_rows(x, level, t_len, row_in_block):
    size = 2 << level
    if size > SUBLANES:
        pieces = [jnp.broadcast_to(x[j * size + row_in_block:j * size + row_in_block + 1, :],
                                   (size, x.shape[1])) for j in range(t_len // size)]
        return pieces[0] if len(pieces) == 1 else jnp.concatenate(pieces, axis=0)
    x3 = x.reshape(t_len // SUBLANES, SUBLANES, x.shape[1])
    sub = lax.broadcasted_iota(jnp.int32, x3.shape, 1)
    out = None
    for j in range(SUBLANES // size):
        row = jnp.broadcast_to(x3[:, j * size + row_in_block:j * size + row_in_block + 1, :], x3.shape)
        out = row if out is None else jnp.where(sub >= j * size, row, out)
    return out.reshape(x.shape)


def _interleave_halves(lower, upper, level, t_len):
    half = 1 << level
    if half >= SUBLANES:
        pieces = []
        for j in range(t_len // (2 * half)):
            pieces.append(lower[2 * half * j:2 * half * j + half])
            pieces.append(upper[2 * half * j + half:2 * half * (j + 1)])
        return jnp.concatenate(pieces, axis=0)
    rows = lax.broadcasted_iota(jnp.int32, lower.shape, 0)
    return jnp.where((rows & half) != 0, upper, lower)


def _run_interleaved(gens, stages_per_round):
    results = [None] * len(gens)
    live = [True] * len(gens)
    while any(live):
        for g, steps in enumerate(stages_per_round):
            for _ in range(steps):
                if live[g]:
                    try:
                        next(gens[g])
                    except StopIteration as stop:
                        results[g], live[g] = stop.value, False
    return results


def _cumsum_rows(tril16, x):
    hi = x.astype(BF16)
    rest = x - hi.astype(F32)
    mid = rest.astype(BF16)
    lo = (rest - mid.astype(F32)).astype(BF16)
    return _dot(tril16, hi) + _dot(tril16, mid) + _dot(tril16, lo)


def _activate(group, x, lb=None):
    if group == 1:
        return {1: x * (D_HEAD ** -0.5)}
    if group in (3, 7):
        return {group: _sigmoid(x)}
    if group == 4:
        return {4: x * _sigmoid(x)}
    if group == 5:
        f = lb + (1.0 - lb) * _sigmoid(x)
        return {5: jnp.log(f), KB_GROUP: (1.0 - lb) / (1.0 + jnp.exp(x)), F_GROUP: f}
    return {group: x}


def _now(value):
    return value() if callable(value) else value


def _late_proj_stages(x16_ref, w_ref, bias_ref, lb_ref, out):
    for group in LATE_GROUPS:
        for off in range(0, D_GROUP, LATE_STAGE_COLS):
            yield
            cols = slice(group * D_GROUP + off, group * D_GROUP + off + LATE_STAGE_COLS)
            block = _dot(x16_ref[...], w_ref[:, cols]) + bias_ref[:, cols]
            out[group, off] = _activate(group, block, lb_ref[:, off:off + LATE_STAGE_COLS])[group]


def _mlstm_units(*, q, k, v, gate, z, zt, m0_row, c_old, n_old, g_norm, causal, lane, t_len):
    idx = range(len(q))
    seq = [i // N_HEADS for i in idx]
    b_lane = [N_HEADS + i % N_HEADS for i in idx]
    q16 = [q[i].astype(BF16) for i in idx]
    k16 = [k[i].astype(BF16) for i in idx]
    qk = [_dot_nt(q16[i], k16[i]) for i in idx]
    qc = [_dot(q16[i], c_old[i].astype(BF16)) for i in idx]
    bs_row = [zt[seq[i]][b_lane[i]:b_lane[i] + 1, :] - zt[seq[i]][i % N_HEADS:i % N_HEADS + 1, :]
              for i in idx]
    yield
    col = lambda per_seq, i: per_seq[seq[i]][:, b_lane[i]:b_lane[i] + 1]
    d = [jnp.where(causal, col(z, i) - bs_row[i], -jnp.inf) for i in idx]
    row_max = [jnp.max(d[i], axis=1, keepdims=True) for i in idx]
    last = slice(t_len - 1, t_len)
    m_t_seq, dec_seq, floor_seq, w_last_seq = [], [], [], []
    for s in range(len(z)):
        r = z[s] + m0_row[s]
        d_max = jnp.full(r.shape, -jnp.inf, F32)
        for h in range(N_HEADS):
            d_max = jnp.where(lane == N_HEADS + h, row_max[s * N_HEADS + h], d_max)
        m_t = jnp.maximum(r, d_max)
        i_gate = pltpu.roll(z[s], N_HEADS, axis=1)
        m_t_seq.append(m_t)
        dec_seq.append(jnp.exp(r - m_t))
        floor_seq.append(_exp_neg(m_t))
        w_last_seq.append(jnp.exp(z[s][last] - z[s] + i_gate - m_t[last]))
    m_t = [col(m_t_seq, i) for i in idx]
    dec = [col(dec_seq, i) for i in idx]
    w_last = [col(w_last_seq, i) for i in idx]
    sw = [jnp.exp(d[i] - m_t[i]) * qk[i] for i in idx]
    yield
    swv = [_dot(sw[i].astype(BF16), v[i].astype(BF16)) for i in idx]
    kv = [_dot_tn(k16[i], (w_last[i] * v[i]).astype(BF16)) for i in idx]
    yield
    c_new = [dec[i][last] * c_old[i] + kv[i] for i in idx]
    n_new = [dec[i][last] * n_old[i] + jnp.sum(w_last[i] * k[i], axis=0, keepdims=True) for i in idx]
    m_new = [m_t[i][last] for i in idx]
    den = [dec[i] * jnp.sum(q[i] * n_old[i], axis=1, keepdims=True)
           + jnp.sum(sw[i], axis=1, keepdims=True) for i in idx]
    hid = [(dec[i] * qc[i] + swv[i]) / jnp.maximum(jnp.abs(den[i]), col(floor_seq, i)) for i in idx]
    yield
    rms = [lax.rsqrt(jnp.mean(hid[i] * hid[i], axis=1, keepdims=True) + RMS_EPS) for i in idx]
    out = [_now(gate[i]) * (hid[i] * rms[i] * g_norm[i]) for i in idx]
    return out, c_new, n_new, m_new


def _hgrn_units(*, qb, log_f, f, kb, iv, gate, s_old_t, g_norm, tril, level_of, t_len):
    idx = range(len(qb))
    n_levels = t_len.bit_length() - 1
    a = [_cumsum_rows(tril, log_f[i]) for i in idx]
    yield
    diag =[_dot_nt(qb[i].astype(BF16), kb[i].astype(BF16)) for i in idx]
    scores = [jnp.where(level_of == -2, diag[i], 0.0) for i in idx]
    for level in range(n_levels):
        yield
        x16 = []
        for i in idx:
            base = _interleave_halves(kb[i], qb[i], level, t_len)
            if level == 0:
                x = base * _interleave_halves(jnp.ones_like(f[i]), f[i], 0, t_len)
            else:
                ref = _block_rows(a[i], level, t_len, (1 << level) - 1)
                x = base * _exp_neg(jnp.abs(a[i] - ref))
            x16.append(x.astype(BF16))
        part = [_dot_nt(x16[i], x16[i]) for i in idx]
        scores = [jnp.where(level_of == level, part[i], scores[i]) for i in idx]
    yield
    last = slice(t_len - 1, t_len)
    q_in = [(qb[i] * jnp.exp(a[i])).astype(BF16) for i in idx]
    k_out = [(kb[i] * jnp.exp(a[i][last] - a[i])).astype(BF16) for i in idx]
    inter = [_dot_nt(q_in[i], s_old_t[i].astype(BF16)) for i in idx]
    iv16 = [_now(iv[i]).astype(BF16) for i in idx]
    intra = [_dot(scores[i].astype(BF16), iv16[i]) for i in idx]
    kv = [_dot_tn(iv16[i], k_out[i]) for i in idx]
    yield
    s_new_t = [jnp.exp(a[i][last]) * s_old_t[i] + kv[i] for i in idx]
    o = [inter[i] + intra[i] for i in idx]
    rms = [lax.rsqrt(jnp.mean(o[i] * o[i], axis=1, keepdims=True) + RMS_EPS) for i in idx]
    out = [_now(gate[i]) * (o[i] * rms[i] * g_norm[i]) for i in idx]
    return out, s_new_t


def _load_state(c0_ref, n0_ref, m0_ref, s0_ref, c_ref, n_ref, m_ref, s_ref, n_seq_blk):
    shared = c0_ref.shape[0] == 1 and n_seq_blk > 1
    for s in range(n_seq_blk):
        src = 0 if shared else s
        c_ref[s] = c0_ref[src]
        n_ref[s] = n0_ref[src]
        m_ref[s] = m0_ref[src]
        for h in range(N_HEADS):
            s_ref[s, h] = s0_ref[src, h].T


def _finish_state(s_ref, n_seq_blk):
    for s in range(n_seq_blk):
        for h in range(N_HEADS):
            s_ref[s, h] = s_ref[s, h].T


def _act_cols(group, h):
    base = group * D_GROUP if group < KB_GROUP else D_PROJ + (group - KB_GROUP) * D_GROUP
    return slice(base + h * D_HEAD, base + (h + 1) * D_HEAD)


def _mixer_body(p_ref, bf_ref, ga_ref, gb_ref, lb_ref, c_ref, n_ref, m_ref, s_ref,
                *, t_len, n_seq_blk, activated, side_stages=None, late_proj=None):
    rows = lax.broadcasted_iota(jnp.int32, (t_len, t_len), 0)
    cols = lax.broadcasted_iota(jnp.int32, (t_len, t_len), 1)
    causal = cols <= rows
    tril = causal.astype(BF16)
    level_of = jnp.where(rows > cols, 31 - lax.clz(rows ^ cols), jnp.where(rows == cols, -2, -1))
    lane = lax.broadcasted_iota(jnp.int32, (t_len, LANES), 1)
    is_f = (lane >= N_HEADS) & (lane < 2 * N_HEADS)
    hd = lambda j, h: slice(j * D_GROUP + h * D_HEAD, j * D_GROUP + (h + 1) * D_HEAD)

    units = [(s, h) for s in range(n_seq_blk) for h in range(N_HEADS)]
    rs = lambda s: slice(s * t_len, (s + 1) * t_len)

    z_seq, zt_seq, n_seq, m_seq = [], [], [], []
    for s in range(n_seq_blk):
        gates = p_ref[rs(s), GATE_COL:GATE_COL + LANES]
        log_f = jnp.where(is_f, jax.nn.log_sigmoid(gates + bf_ref[...]), 0.0)
        cum_f = _cumsum_rows(tril, log_f)
        z_seq.append(jnp.where(is_f, cum_f, jnp.where(lane < N_HEADS, gates, 0.0)))
        zt_seq.append(z_seq[s].T)
        n_seq.append(n_ref[s])
        m_seq.append(m_ref[s])
    computed, late_vals = {}, {}

    def act(group):
        source = 5 if group >= KB_GROUP else group
        if late_proj is not None and source in LATE_GROUPS:
            def thunk(s, h):
                off, col = divmod(h * D_HEAD, LATE_STAGE_COLS)
                return lambda: late_vals[source, off * LATE_STAGE_COLS][rs(s), col:col + D_HEAD]
            return [thunk(s, h) for s, h in units]
        if activated:
            return [p_ref[rs(s), _act_cols(group, h)] for s, h in units]
        for s, h in units:
            if (source, s, h) not in computed:
                computed[source, s, h] = _activate(source, p_ref[rs(s), hd(source, h)],
                                                   lb_ref[:, hd(0, h)])
        return [computed[source, s, h][group] for s, h in units]

    lane_row = lane[:1, :]
    m0_rows = []
    for s in range(n_seq_blk):
        m0_row = jnp.zeros((1, LANES), F32)
        for h in range(N_HEADS):
            m0_row = jnp.where(lane_row == N_HEADS + h, m_seq[s][:, h:h + 1], m0_row)
        m0_rows.append(m0_row)
    mlstm = _mlstm_units(
        q=act(0), k=act(1), v=act(2), gate=act(3), z=z_seq, zt=zt_seq, m0_row=m0_rows,
        c_old=[c_ref[s, h] for s, h in units], n_old=[n_seq[s][h:h + 1, :] for s, h in units],
        g_norm=[ga_ref[:, hd(0, h)] for s, h in units], causal=causal, lane=lane, t_len=t_len)
    hgrn = _hgrn_units(
        qb=act(4), log_f=act(5), f=act(F_GROUP), kb=act(KB_GROUP), iv=act(6), gate=act(7),
        s_old_t=[s_ref[s, h] for s, h in units],
        g_norm=[gb_ref[:, hd(0, h)] for s, h in units], tril=tril, level_of=level_of, t_len=t_len)
    gens, per_round = [mlstm, hgrn], [1, 2]
    if late_proj is not None:
        late = _late_proj_stages(*late_proj, lb_ref, late_vals)
        gens, per_round = [late] + gens, [LATE_STAGES_PER_ROUND] + per_round
    if side_stages is not None:
        gens, per_round = gens + [side_stages], per_round + [SIDE_STAGES_PER_ROUND]
    results = _run_interleaved(gens, per_round)
    outs, c_new, n_new, m_new = results[gens.index(mlstm)]
    outs_b, s_new_t = results[gens.index(hgrn)]

    for i, (s, h) in enumerate(units):
        c_ref[s, h] = c_new[i]
        s_ref[s, h] = s_new_t[i]
    head_lane = lax.broadcasted_iota(jnp.int32, (1, N_HEADS), 1)
    for s in range(n_seq_blk):
        n_ref[s] = jnp.concatenate(n_new[s * N_HEADS:(s + 1) * N_HEADS], axis=0)
        m_row = m_seq[s]
        for h in range(N_HEADS):
            m_row = jnp.where(head_lane == h, m_new[s * N_HEADS + h], m_row)
        m_ref[s] = m_row
    return outs, outs_b


def _mixer_kernel(p_ref, c0_ref, n0_ref, m0_ref, s0_ref, bf_ref, ga_ref, gb_ref, lb_ref,
                  mix_ref, c_ref, n_ref, m_ref, s_ref, *, t_len, n_chunks, n_seq_blk):
    chunk = pl.program_id(1)

    @pl.when(chunk == 0)
    def _():
        _load_state(c0_ref, n0_ref, m0_ref, s0_ref, c_ref, n_ref, m_ref, s_ref, n_seq_blk)

    outs_a, outs_b = _mixer_body(p_ref, bf_ref, ga_ref, gb_ref, lb_ref, c_ref, n_ref, m_ref, s_ref,
                                 t_len=t_len, n_seq_blk=n_seq_blk, activated=False)
    for i in range(n_seq_blk * N_HEADS):
        s, h = divmod(i, N_HEADS)
        rows = slice(s * t_len, (s + 1) * t_len)
        mix_ref[rows, h * D_HEAD:(h + 1) * D_HEAD] = outs_a[i].astype(mix_ref.dtype)
        mix_ref[rows, D_GROUP + h * D_HEAD:D_GROUP + (h + 1) * D_HEAD] = outs_b[i].astype(mix_ref.dtype)

    @pl.when(chunk == n_chunks - 1)
    def _():
        _finish_state(s_ref, n_seq_blk)


def _mixer(proj, c0, n0, m0, s0, bf_row, ga, gb, lb, *, n_seq, n_chunks, t_len, n_seq_blk,
           shared_init):
    assert not shared_init or n_seq_blk == 1
    assert n_chunks == 1 or n_seq_blk == 1
    nb = n_seq_blk
    init = (lambda b, c: (0, 0, 0, 0)) if shared_init else (lambda b, c: (b, 0, 0, 0))
    init3 = (lambda b, c: (0, 0, 0)) if shared_init else (lambda b, c: (b, 0, 0))
    const = lambda b, c: (0, 0)
    state4 = pl.BlockSpec((nb, N_HEADS, D_HEAD, D_HEAD), lambda b, c: (b, 0, 0, 0))
    return pl.pallas_call(
        functools.partial(_mixer_kernel, t_len=t_len, n_chunks=n_chunks, n_seq_blk=nb),
        grid=(n_seq // nb, n_chunks),
        in_specs=[
            pl.BlockSpec((nb * t_len, D_PROJ), lambda b, c: (b * n_chunks + c, 0)),
            pl.BlockSpec((nb, N_HEADS, D_HEAD, D_HEAD), init),
            pl.BlockSpec((nb, N_HEADS, D_HEAD), init3),
            pl.BlockSpec((nb, 1, N_HEADS), init3),
            pl.BlockSpec((nb, N_HEADS, D_HEAD, D_HEAD), init),
            pl.BlockSpec((1, LANES), const),
            pl.BlockSpec((1, D_GROUP), const),
            pl.BlockSpec((1, D_GROUP), const),
            pl.BlockSpec((1, D_GROUP), const),
        ],
        out_specs=[
            pl.BlockSpec((nb * t_len, D_MODEL), lambda b, c: (b * n_chunks + c, 0)),
            state4,
            pl.BlockSpec((nb, N_HEADS, D_HEAD), lambda b, c: (b, 0, 0)),
            pl.BlockSpec((nb, 1, N_HEADS), lambda b, c: (b, 0, 0)),
            state4,
        ],
        out_shape=[
            jax.ShapeDtypeStruct((n_seq * n_chunks * t_len, D_MODEL), BF16),
            jax.ShapeDtypeStruct((n_seq, N_HEADS, D_HEAD, D_HEAD), F32),
            jax.ShapeDtypeStruct((n_seq, N_HEADS, D_HEAD), F32),
            jax.ShapeDtypeStruct((n_seq, 1, N_HEADS), F32),
            jax.ShapeDtypeStruct((n_seq, N_HEADS, D_HEAD, D_HEAD), F32),
        ],
        compiler_params=pltpu.CompilerParams(
            dimension_semantics=("arbitrary", "arbitrary"), vmem_limit_bytes=VMEM_LIMIT),
        name=f"mixer_t{t_len}",
    )(proj, c0, n0, m0, s0, bf_row, ga, gb, lb)


def _meta_kernel(x_ref, ge_ref, be_ref, win_ref, bin_ref, bf_ref, ga_ref, gb_ref, lb_ref,
                 wout_ref, bout_ref, g1_ref, b1_ref, wu_ref, bu_ref,
                 c_ref, n_ref, m_ref, s_ref, conv_ref, proj_scr):
    t_len = x_ref.shape[0]
    xn = _layer_norm(x_ref[...], ge_ref[...], be_ref[...])
    proj_scr[...] = _dot(xn.astype(BF16), win_ref[...]) + bin_ref[...]
    for ref in (c_ref, n_ref, m_ref, s_ref):
        ref[...] = jnp.zeros(ref.shape, ref.dtype)
    outs_a, outs_b = _mixer_body(proj_scr, bf_ref, ga_ref, gb_ref, lb_ref, c_ref, n_ref, m_ref, s_ref,
                                 t_len=t_len, n_seq_blk=1, activated=False)
    _finish_state(s_ref, 1)
    mix = jnp.concatenate([o.astype(BF16) for o in outs_a + outs_b], axis=1)
    y = _dot(mix, wout_ref[...]) + bout_ref[...]
    x1 = _layer_norm(ALPHA * xn + y, g1_ref[...], b1_ref[...])
    u = _dot(x1.astype(BF16), wu_ref[...]) + bu_ref[...]
    conv_ref[0] = u[t_len - (CONV_W - 1):, :]


def _meta_state(x, ln_e_g, ln_e_b, w_in, b_in, bf_row, ga, gb, lb, w_out, b_out, ln_g, ln_b,
                w_up, b_up):
    t_len = x.shape[0]
    full = lambda shape: _resident(shape, lambda i: (0,) * len(shape))
    vec, grp = full((1, D_MODEL)), full((1, D_GROUP))
    state4 = (1, N_HEADS, D_HEAD, D_HEAD)
    return pl.pallas_call(
        _meta_kernel,
        grid=(1,),
        in_specs=[full((t_len, D_MODEL)), vec, vec, full((D_MODEL, D_PROJ)), full((1, D_PROJ)),
                  full((1, LANES)), grp, grp, grp, full((D_MODEL, D_MODEL)), vec, vec, vec,
                  full((D_MODEL, D_FF)), full((1, D_FF))],
        out_specs=[pl.BlockSpec(shape, lambda i, rank=len(shape): (0,) * rank)
                   for shape in (state4, (1, N_HEADS, D_HEAD), (1, 1, N_HEADS), state4,
                                 (1, CONV_W - 1, D_FF))],
        out_shape=[
            jax.ShapeDtypeStruct(state4, F32),
            jax.ShapeDtypeStruct((1, N_HEADS, D_HEAD), F32),
            jax.ShapeDtypeStruct((1, 1, N_HEADS), F32),
            jax.ShapeDtypeStruct(state4, F32),
            jax.ShapeDtypeStruct((1, CONV_W - 1, D_FF), F32),
        ],
        scratch_shapes=[pltpu.VMEM((t_len, D_PROJ), F32)],
        compiler_params=pltpu.CompilerParams(
            dimension_semantics=("arbitrary",), vmem_limit_bytes=VMEM_LIMIT),
        name="meta_state",
    )(x, ln_e_g, ln_e_b, w_in, b_in, bf_row, ga, gb, lb, w_out, b_out, ln_g, ln_b, w_up, b_up)


def _in_proj_stages(x_ref, g_ref, b_ref, w_ref, bias_ref, lb_ref, act_ref, xn_ref, x16_ref):
    assert D_GROUP % IN_PROJ_STAGE_COLS == 0
    xn = _layer_norm(x_ref[...].reshape(xn_ref.shape), g_ref[...], b_ref[...])
    xn_ref[...] = xn
    x16 = xn.astype(BF16)
    x16_ref[...] = x16
    starts = [lo for lo in range(0, GATE_COL, IN_PROJ_STAGE_COLS) if lo // D_GROUP not in LATE_GROUPS]
    for lo in starts + [GATE_COL]:
        hi = min(lo + IN_PROJ_STAGE_COLS, D_PROJ)
        yield
        block = _dot(x16, w_ref[:, lo:hi]) + bias_ref[:, lo:hi]
        group, off = divmod(lo, D_GROUP)
        if lo >= GATE_COL:
            act_ref[:, lo:hi] = block
            continue
        for dst, val in _activate(group, block, lb_ref[:, off:off + hi - lo]).items():
            base = _act_cols(dst, 0).start + off
            act_ref[:, base:base + hi - lo] = val


def _prompt_kernel(x0_ref, xnext_ref, ge_ref, be_ref, win_ref, bin_ref, c0_ref, n0_ref, m0_ref, s0_ref,
                   bf_ref, ga_ref, gb_ref, lb_ref, wout_ref, bout_ref, g1_ref, b1_ref,
                   x1_ref, c_ref, n_ref, m_ref, s_ref,
                   proj_scr, xn_scr, x16_scr, proj_alt, xn_alt, x16_alt,
                   *, t_len, n_chunks, n_seq_blk):
    chunk = pl.program_id(1)
    step = pl.program_id(0) * n_chunks + chunk

    @pl.when(step == 0)
    def _():
        first = _in_proj_stages(x0_ref, ge_ref, be_ref, win_ref, bin_ref, lb_ref,
                                proj_scr, xn_scr, x16_scr)
        _run_interleaved([first], [1])

    @pl.when(chunk == 0)
    def _():
        _load_state(c0_ref, n0_ref, m0_ref, s0_ref, c_ref, n_ref, m_ref, s_ref, n_seq_blk)

    def tile(proj_cur, xn_cur, x16_cur, proj_next, xn_next, x16_next):
        next_proj = _in_proj_stages(xnext_ref, ge_ref, be_ref, win_ref, bin_ref, lb_ref,
                                    proj_next, xn_next, x16_next)
        outs_a, outs_b = _mixer_body(
            proj_cur, bf_ref, ga_ref, gb_ref, lb_ref, c_ref, n_ref, m_ref, s_ref,
            t_len=t_len, n_seq_blk=n_seq_blk, activated=True, side_stages=next_proj,
            late_proj=(x16_cur, win_ref, bin_ref))
        mix = jnp.concatenate(
            [jnp.concatenate([o.astype(BF16) for o in outs_a[s * N_HEADS:(s + 1) * N_HEADS]
                              + outs_b[s * N_HEADS:(s + 1) * N_HEADS]], axis=1)
             for s in range(n_seq_blk)], axis=0)
        y = _dot(mix, wout_ref[...]) + bout_ref[...]
        x1 = _layer_norm(ALPHA * xn_cur[...] + y, g1_ref[...], b1_ref[...])
        x1_ref[...] = x1.reshape(x1_ref.shape)

    @pl.when(step % 2 == 0)
    def _():
        tile(proj_scr, xn_scr, x16_scr, proj_alt, xn_alt, x16_alt)

    @pl.when(step % 2 == 1)
    def _():
        tile(proj_alt, xn_alt, x16_alt, proj_scr, xn_scr, x16_scr)

    @pl.when(chunk == n_chunks - 1)
    def _():
        _finish_state(s_ref, n_seq_blk)


def _prompt_mixer(x, ln_e_g, ln_e_b, w_in, b_in, c0, n0, m0, s0, bf_row, ga, gb, lb,
                  w_out, b_out, ln_g, ln_b, *, n_chunks, t_len, n_seq_blk):
    n_seq = x.shape[0]
    nb = n_seq_blk
    n_tiles = (n_seq // nb) * n_chunks
    const = lambda b, c: (0, 0)
    init4 = lambda b, c: (0, 0, 0, 0)
    init3 = lambda b, c: (0, 0, 0)
    vec = pl.BlockSpec((1, D_MODEL), const)
    grp = pl.BlockSpec((1, D_GROUP), const)
    state4 = pl.BlockSpec((nb, N_HEADS, D_HEAD, D_HEAD), lambda b, c: (b, 0, 0, 0))

    def next_tile(b, c):
        nxt = jnp.minimum(b * n_chunks + c + 1, n_tiles - 1)
        return (nxt // n_chunks, nxt % n_chunks, 0)

    return pl.pallas_call(
        functools.partial(_prompt_kernel, t_len=t_len, n_chunks=n_chunks, n_seq_blk=nb),
        grid=(n_seq // nb, n_chunks),
        in_specs=[
            _resident((nb, t_len, D_MODEL), init3),
            pl.BlockSpec((nb, t_len, D_MODEL), next_tile),
            vec, vec,
            _resident((D_MODEL, D_PROJ), const),
            pl.BlockSpec((1, D_PROJ), const),
            pl.BlockSpec((1, N_HEADS, D_HEAD, D_HEAD), init4),
            pl.BlockSpec((1, N_HEADS, D_HEAD), init3),
            pl.BlockSpec((1, 1, N_HEADS), init3),
            pl.BlockSpec((1, N_HEADS, D_HEAD, D_HEAD), init4),
            pl.BlockSpec((1, LANES), const),
            grp, grp, grp,
            _resident((D_MODEL, D_MODEL), const),
            vec, vec, vec,
        ],
        out_specs=[
            pl.BlockSpec((nb, t_len, D_MODEL), lambda b, c: (b, c, 0)),
            state4,
            pl.BlockSpec((nb, N_HEADS, D_HEAD), lambda b, c: (b, 0, 0)),
            pl.BlockSpec((nb, 1, N_HEADS), lambda b, c: (b, 0, 0)),
            state4,
        ],
        out_shape=[
            jax.ShapeDtypeStruct(x.shape, F32),
            jax.ShapeDtypeStruct((n_seq, N_HEADS, D_HEAD, D_HEAD), F32),
            jax.ShapeDtypeStruct((n_seq, N_HEADS, D_HEAD), F32),
            jax.ShapeDtypeStruct((n_seq, 1, N_HEADS), F32),
            jax.ShapeDtypeStruct((n_seq, N_HEADS, D_HEAD, D_HEAD), F32),
        ],
        scratch_shapes=2 * [pltpu.VMEM((nb * t_len, D_ACT), F32), pltpu.VMEM((nb * t_len, D_MODEL), F32),
                            pltpu.VMEM((nb * t_len, D_MODEL), BF16)],
        compiler_params=pltpu.CompilerParams(
            dimension_semantics=("arbitrary", "arbitrary"), vmem_limit_bytes=VMEM_LIMIT),
        name="prompt_mixer",
    )(x, x, ln_e_g, ln_e_b, w_in, b_in, c0, n0, m0, s0, bf_row, ga, gb, lb, w_out, b_out, ln_g, ln_b)


def _ffn_kernel(x_ref, *refs, n_seq_blk, t_len):
    _ffn_tile(x_ref[...], *refs, n_seq_blk=n_seq_blk, t_len=t_len)


def _out_proj_ffn_kernel(x_ref, mix_ref, ge_ref, be_ref, wo_ref, bo_ref, g1_ref, b1_ref, *refs,
                         n_seq_blk, t_len):
    xn = _layer_norm(x_ref[...], ge_ref[...], be_ref[...])
    y = _dot(mix_ref[...], wo_ref[...]) + bo_ref[...]
    x1 = _layer_norm(ALPHA * xn + y, g1_ref[...], b1_ref[...])
    _ffn_tile(x1, *refs, n_seq_blk=n_seq_blk, t_len=t_len)


def _ffn_tile(x, cs_ref, wu_ref, bu_ref, wc_ref, bc_ref, wd_ref, bd_ref, g_ref, b_ref,
              y_ref, nc_ref, full_ref, *, n_seq_blk, t_len):
    hist = SUBLANES - (CONV_W - 1)

    @pl.when(pl.program_id(1) == 0)
    def _():
        full_ref[:, hist:SUBLANES, :] = cs_ref[...]

    up = _dot(x.astype(BF16), wu_ref[...]) + bu_ref[...]
    u = up[:, :D_FF].reshape(n_seq_blk, t_len, D_FF)
    gate = up[:, D_FF:].reshape(n_seq_blk, t_len, D_FF)
    full_ref[:, SUBLANES:SUBLANES + t_len, :] = u
    conv = bc_ref[...] + u * wc_ref[CONV_W - 1:CONV_W, :]
    for j in range(CONV_W - 1):
        conv = conv + full_ref[:, hist + j:hist + j + t_len, :] * wc_ref[j:j + 1, :]
    last = full_ref[:, hist + t_len:SUBLANES + t_len, :]
    nc_ref[...] = last
    full_ref[:, hist:SUBLANES, :] = last
    act = (conv * _sigmoid(conv) * gate).reshape(n_seq_blk * t_len, D_FF)
    ffn = _dot(act.astype(BF16), wd_ref[...]) + bd_ref[...]
    y_ref[...] = _layer_norm(ALPHA * x + ffn, g_ref[...], b_ref[...])


def _ffn(x, conv_state, w_up, b_up, w_conv, b_conv, w_down, b_down, ln_g, ln_b,
         *, n_seq, seq_len, n_seq_blk, t_len, shared_init, out_proj=None):
    n_t = seq_len // t_len
    rows = n_seq_blk * t_len
    const = lambda s, t: (0, 0)
    cs_map = (lambda s, t: (0, 0, 0)) if shared_init else (lambda s, t: (s, 0, 0))
    row = pl.BlockSpec((rows, D_MODEL), lambda s, t: (s * n_t + t, 0))
    vec = pl.BlockSpec((1, D_MODEL), const)
    body, lead_specs, lead_args = _ffn_kernel, [row], (x,)
    if out_proj is not None:
        body = _out_proj_ffn_kernel
        lead_specs = [row, row, vec, vec, _resident((D_MODEL, D_MODEL), const), vec, vec, vec]
        lead_args = (x,) + tuple(out_proj)
    return pl.pallas_call(
        functools.partial(body, n_seq_blk=n_seq_blk, t_len=t_len),
        grid=(n_seq // n_seq_blk, n_t),
        in_specs=lead_specs + [
            pl.BlockSpec((n_seq_blk, CONV_W - 1, D_FF), cs_map),
            _resident((D_MODEL, 2 * D_FF), const),
            pl.BlockSpec((1, 2 * D_FF), const),
            pl.BlockSpec((CONV_W, D_FF), const),
            pl.BlockSpec((1, D_FF), const),
            _resident((D_FF, D_MODEL), const),
            vec, vec, vec,
        ],
        out_specs=[row, pl.BlockSpec((n_seq_blk, CONV_W - 1, D_FF), lambda s, t: (s, 0, 0))],
        out_shape=[
            jax.ShapeDtypeStruct((n_seq * seq_len, D_MODEL), F32),
            jax.ShapeDtypeStruct((n_seq, CONV_W - 1, D_FF), F32),
        ],
        scratch_shapes=[pltpu.VMEM((n_seq_blk, SUBLANES + t_len, D_FF), F32)],
        compiler_params=pltpu.CompilerParams(
            dimension_semantics=("arbitrary", "arbitrary"), vmem_limit_bytes=VMEM_LIMIT),
        name=f"ffn_t{t_len}",
    )(*lead_args, conv_state, w_up, b_up, w_conv, b_conv, w_down, b_down, ln_g, ln_b)


def kernel(x_prompt, x_sample, state_mlstm_C, state_mlstm_n, state_mlstm_m, state_hgrn_S, state_ffn_conv, meta_tokens, ln_emb_g, ln_emb_b, w_in, b_in, b_fgate_a, g_norm_a, g_norm_b, hgrn_lb_logits, w_out, b_out, ln1_g, ln1_b, w_up, b_up, w_conv, b_conv, w_down, b_down, ln2_g, ln2_b):
    assert w_in.shape[0] == DEPTH == 1
    n_prompt, seq, _ = x_prompt.shape
    n_sample, dec_seq, _ = x_sample.shape
    row = lambda v: v.reshape(1, -1).astype(F32)

    gate0 = 4 * D_GROUP
    gate1 = gate0 + 2 * N_HEADS
    pad = D_PROJ - w_in.shape[2]
    w_in_p = _regroup_in_proj_weight(w_in[0].T, tm=256)
    b_in_p = jnp.concatenate(
        [b_in[0][:gate0], b_in[0][gate1:], b_in[0][gate0:gate1], jnp.zeros((pad,), b_in.dtype)]
    ).reshape(1, D_PROJ).astype(F32)
    bf_row = jnp.zeros((1, LANES), F32).at[0, N_HEADS:2 * N_HEADS].set(b_fgate_a[0].astype(F32))
    lb = jnp.cumsum(jax.nn.softmax(hgrn_lb_logits.astype(F32), axis=0), axis=0)[0].reshape(1, D_GROUP)
    ga, gb = row(g_norm_a[0]), row(g_norm_b[0])
    ln_e = (row(ln_emb_g), row(ln_emb_b))
    out_p = (w_out[0].astype(BF16), row(b_out[0]), row(ln1_g[0]), row(ln1_b[0]))
    ffn_p = (w_up[0].astype(BF16), row(b_up[0]), w_conv[0].astype(F32), row(b_conv[0]),
             w_down[0].astype(BF16), row(b_down[0]), row(ln2_g[0]), row(ln2_b[0]))

    c_m, n_m, m_m, s_m, conv_m = _meta_state(
        meta_tokens.astype(F32), *ln_e, w_in_p, b_in_p, bf_row, ga, gb, lb, *out_p, ffn_p[0], ffn_p[1])

    x1_p, c_p, n_p, m_p, s_p = _prompt_mixer(
        x_prompt.astype(F32), *ln_e, w_in_p, b_in_p, c_m, n_m, m_m, s_m, bf_row, ga, gb, lb, *out_p,
        n_chunks=seq // PROMPT_CHUNK, t_len=PROMPT_CHUNK, n_seq_blk=PROMPT_SEQS_PER_STEP)
    y_p, conv_p = _ffn(x1_p.reshape(n_prompt * seq, D_MODEL), conv_m, *ffn_p, n_seq=n_prompt,
                       seq_len=seq, n_seq_blk=1, t_len=512, shared_init=True)

    sample_state = (state_mlstm_C[0].astype(F32), state_mlstm_n[0].astype(F32),
                    state_mlstm_m[0].astype(F32).reshape(n_sample, 1, N_HEADS),
                    state_hgrn_S[0].astype(F32))
    xs_rows = x_sample.reshape(n_sample * dec_seq, D_MODEL).astype(F32)
    proj_s = _in_proj(xs_rows, *ln_e, w_in_p, b_in_p, tm=256)
    mix_s, c_s, n_s, m_s, s_s = _mixer(
        proj_s, *sample_state, bf_row, ga, gb, lb, n_seq=n_sample, n_chunks=1, t_len=dec_seq,
        n_seq_blk=8, shared_init=False)
    y_s, conv_s = _ffn(xs_rows, state_ffn_conv[0].astype(F32), *ffn_p, n_seq=n_sample,
                       seq_len=dec_seq, n_seq_blk=32, t_len=dec_seq, shared_init=False,
                       out_proj=(mix_s, *ln_e, *out_p))

    lead = lambda v: v[None]
    return (y_p.reshape(n_prompt, seq, D_MODEL), y_s.reshape(n_sample, dec_seq, D_MODEL),
            lead(c_p), lead(n_p), lead(m_p.reshape(n_prompt, N_HEADS)), lead(s_p), lead(conv_p),
            lead(c_s), lead(n_s), lead(m_s.reshape(n_sample, N_HEADS)), lead(s_s), lead(conv_s))
```

```python
import functools

import jax
import jax.numpy as jnp
from jax import lax
from jax.experimental import pallas as pl
from jax.experimental.pallas import tpu as pltpu

D_MODEL = 1024
N_META = 16
N_HEADS = 4
D_HEAD = 128
D_GROUP = N_HEADS * D_HEAD
D_FF = 2816
CONV_W = 3
DEPTH = 1
ALPHA = (2.0 * DEPTH) ** 0.25
LN_EPS = 1e-5
RMS_EPS = 1e-6
NEG_LOG2_E = -1.4426950408889634

LANES = 128
SUBLANES = 8
GATE_COL = 8 * D_GROUP
D_PROJ = GATE_COL + LANES
KB_GROUP, F_GROUP = 8, 9
D_ACT = D_PROJ + 2 * D_GROUP
LATE_GROUPS = (3, 6, 7)
LATE_STAGE_COLS = 256
LATE_STAGES_PER_ROUND = 2
IN_PROJ_STAGE_COLS = 256
SIDE_STAGES_PER_ROUND = 3
PROMPT_SEQS_PER_STEP = 2
PROMPT_CHUNK = 128
VMEM_LIMIT = 56 * 1024 * 1024

F32 = jnp.float32
BF16 = jnp.bfloat16
NT_DIMS = (((1,), (1,)), ((), ()))
TN_DIMS = (((0,), (0,)), ((), ()))


def _layer_norm(x, g, b):
    mu = jnp.mean(x, axis=-1, keepdims=True)
    xc = x - mu
    var = jnp.mean(xc * xc, axis=-1, keepdims=True)
    return xc * lax.rsqrt(var + LN_EPS) * g + b


def _exp_neg(x):
    return jnp.exp2(x * NEG_LOG2_E)


def _sigmoid(x):
    return 1.0 / (1.0 + _exp_neg(x))


def _resident(block_shape, index_map):
    return pl.BlockSpec(block_shape, index_map, pipeline_mode=pl.Buffered(1))


def _dot(a, b):
    return jnp.dot(a, b, preferred_element_type=F32)


def _dot_nt(a, b):
    return lax.dot_general(a, b, NT_DIMS, preferred_element_type=F32)


def _dot_tn(a, b):
    return lax.dot_general(a, b, TN_DIMS, preferred_element_type=F32)


def _regroup_kernel(wt_ref, o_ref):
    gate0 = 4 * D_GROUP
    gate1 = gate0 + 2 * N_HEADS
    for j in range(GATE_COL // LANES):
        src = j * LANES if j * LANES < gate0 else j * LANES + (gate1 - gate0)
        o_ref[:, j * LANES:(j + 1) * LANES] = wt_ref[src:src + LANES, :].T.astype(o_ref.dtype)
    gates = wt_ref[gate0:gate1, :].T.astype(o_ref.dtype)
    o_ref[:, GATE_COL:] = jnp.concatenate(
        [gates, jnp.zeros((gates.shape[0], LANES - gates.shape[1]), o_ref.dtype)], axis=1)


def _regroup_in_proj_weight(w_t, *, tm):
    cols, n = w_t.shape
    return pl.pallas_call(
        _regroup_kernel,
        grid=(n // tm,),
        in_specs=[pl.BlockSpec((cols, tm), lambda i: (0, i))],
        out_specs=pl.BlockSpec((tm, D_PROJ), lambda i: (i, 0)),
        out_shape=jax.ShapeDtypeStruct((n, D_PROJ), BF16),
        compiler_params=pltpu.CompilerParams(dimension_semantics=("arbitrary",)),
        name="regroup_w_in",
    )(w_t)


def _in_proj_kernel(x_ref, g_ref, b_ref, w_ref, bias_ref, o_ref):
    xn = _layer_norm(x_ref[...], g_ref[...], b_ref[...])
    o_ref[...] = _dot(xn.astype(BF16), w_ref[...]) + bias_ref[...]


def _in_proj(x, ln_g, ln_b, w, bias, *, tm):
    n = x.shape[0]
    const = lambda i: (0, 0)
    return pl.pallas_call(
        _in_proj_kernel,
        grid=(n // tm,),
        in_specs=[
            pl.BlockSpec((tm, D_MODEL), lambda i: (i, 0)),
            pl.BlockSpec((1, D_MODEL), const),
            pl.BlockSpec((1, D_MODEL), const),
            pl.BlockSpec((D_MODEL, D_PROJ), const),
            pl.BlockSpec((1, D_PROJ), const),
        ],
        out_specs=pl.BlockSpec((tm, D_PROJ), lambda i: (i, 0)),
        out_shape=jax.ShapeDtypeStruct((n, D_PROJ), F32),
        compiler_params=pltpu.CompilerParams(
            dimension_semantics=("arbitrary",), vmem_limit_bytes=VMEM_LIMIT),
        name="in_proj",
    )(x, ln_g, ln_b, w, bias)


def _block_rows(x, level, t_len, row_in_block):
    size = 2 << level
    if size > SUBLANES:
        pieces = [jnp.broadcast_to(x[j * size + row_in_block:j * size + row_in_block + 1, :],
                                   (size, x.shape[1])) for j in range(t_len // size)]
        return pieces[0] if len(pieces) == 1 else jnp.concatenate(pieces, axis=0)
    x3 = x.reshape(t_len // SUBLANES, SUBLANES, x.shape[1])
    sub = lax.broadcasted_iota(jnp.int32, x3.shape, 1)
    out = None
    for j in range(SUBLANES // size):
        row = jnp.broadcast_to(x3[:, j * size + row_in_block:j * size + row_in_block + 1, :], x3.shape)
        out = row if out is None else jnp.where(sub >= j * size, row, out)
    return out.reshape(x.shape)


def _interleave_halves(lower, upper, level, t_len):
    half = 1 << level
    if half >= SUBLANES:
        pieces = []
        for j in range(t_len // (2 * half)):
            pieces.append(lower[2 * half * j:2 * half * j + half])
            pieces.append(upper[2 * half * j + half:2 * half * (j + 1)])
        return jnp.concatenate(pieces, axis=0)
    rows = lax.broadcasted_iota(jnp.int32, lower.shape, 0)
    return jnp.where((rows & half) != 0, upper, lower)


def _run_interleaved(gens, stages_per_round):
    results = [None] * len(gens)
    live = [True] * len(gens)
    while any(live):
        for g, steps in enumerate(stages_per_round):
            for _ in range(steps):
                if live[g]:
                    try:
                        next(gens[g])
                    except StopIteration as stop:
                        results[g], live[g] = stop.value, False
    return results


def _cumsum_rows(tril16, x):
    hi = x.astype(BF16)
    rest = x - hi.astype(F32)
    mid = rest.astype(BF16)
    lo = (rest - mid.astype(F32)).astype(BF16)
    return _dot(tril16, hi) + _dot(tril16, mid) + _dot(tril16, lo)


def _activate(group, x, lb=None):
    if group == 1:
        return {1: x * (D_HEAD ** -0.5)}
    if group in (3, 7):
        return {group: _sigmoid(x)}
    if group == 4:
        return {4: x * _sigmoid(x)}
    if group == 5:
        f = lb + (1.0 - lb) * _sigmoid(x)
        return {5: jnp.log(f), KB_GROUP: (1.0 - lb) / (1.0 + jnp.exp(x)), F_GROUP: f}
    return {group: x}


def _now(value):
    return value() if callable(value) else value


def _late_proj_stages(x16_ref, w_ref, bias_ref, lb_ref, out):
    for group in LATE_GROUPS:
        for off in range(0, D_GROUP, LATE_STAGE_COLS):
            yield
            cols = slice(group * D_GROUP + off, group * D_GROUP + off + LATE_STAGE_COLS)
            block = _dot(x16_ref[...], w_ref[:, cols]) + bias_ref[:, cols]
            out[group, off] = _activate(group, block, lb_ref[:, off:off + LATE_STAGE_COLS])[group]


def _mlstm_units(*, q, k, v, gate, z, zt, m0_row, c_old, n_old, g_norm, causal, lane, t_len):
    idx = range(len(q))
    seq = [i // N_HEADS for i in idx]
    b_lane = [N_HEADS + i % N_HEADS for i in idx]
    q16 = [_now(q[i]).astype(BF16) for i in idx]
    k16 = [_now(k[i]).astype(BF16) for i in idx]
    qk = [_dot_nt(q16[i], k16[i]) for i in idx]
    qc = [_dot(q16[i], c_old[i].astype(BF16)) for i in idx]
    bs_row = [zt[seq[i]][b_lane[i]:b_lane[i] + 1, :] - zt[seq[i]][i % N_HEADS:i % N_HEADS + 1, :]
              for i in idx]
    yield
    col = lambda per_seq, i: per_seq[seq[i]][:, b_lane[i]:b_lane[i] + 1]
    d = [jnp.where(causal, col(z, i) - bs_row[i], -jnp.inf) for i in idx]
    row_max = [jnp.max(d[i], axis=1, keepdims=True) for i in idx]
    last = slice(t_len - 1, t_len)
    m_t_seq, dec_seq, floor_seq, w_last_seq = [], [], [], []
    for s in range(len(z)):
        r = z[s] + m0_row[s]
        d_max = jnp.full(r.shape, -jnp.inf, F32)
        for h in range(N_HEADS):
            d_max = jnp.where(lane == N_HEADS + h, row_max[s * N_HEADS + h], d_max)
        m_t = jnp.maximum(r, d_max)
        i_gate = pltpu.roll(z[s], N_HEADS, axis=1)
        m_t_seq.append(m_t)
        dec_seq.append(jnp.exp(r - m_t))
        floor_seq.append(_exp_neg(m_t))
        w_last_seq.append(jnp.exp(z[s][last] - z[s] + i_gate - m_t[last]))
    m_t = [col(m_t_seq, i) for i in idx]
    dec = [col(dec_seq, i) for i in idx]
    w_last = [col(w_last_seq, i) for i in idx]
    sw = [jnp.exp(d[i] - m_t[i]) * qk[i] for i in idx]
    yield
    swv = [_dot(sw[i].astype(BF16), _now(v[i]).astype(BF16)) for i in idx]
    kv = [_dot_tn(k16[i], (w_last[i] * _now(v[i])).astype(BF16)) for i in idx]
    yield
    c_new = [dec[i][last] * c_old[i] + kv[i] for i in idx]
    n_new = [dec[i][last] * n_old[i] + jnp.sum(w_last[i] * _now(k[i]), axis=0, keepdims=True) for i in idx]
    m_new = [m_t[i][last] for i in idx]
    den = [dec[i] * jnp.sum(_now(q[i]) * n_old[i], axis=1, keepdims=True)
           + jnp.sum(sw[i], axis=1, keepdims=True) for i in idx]
    hid = [(dec[i] * qc[i] + swv[i]) / jnp.maximum(jnp.abs(den[i]), col(floor_seq, i)) for i in idx]
    yield
    rms = [lax.rsqrt(jnp.mean(hid[i] * hid[i], axis=1, keepdims=True) + RMS_EPS) for i in idx]
    out = [_now(gate[i]) * (hid[i] * rms[i] * g_norm[i]) for i in idx]
    return out, c_new, n_new, m_new


def _hgrn_units(*, qb, log_f, f, kb, iv, gate, s_old_t, g_norm, tril, level_of, t_len):
    idx = range(len(qb))
    n_levels = t_len.bit_length() - 1
    a = [_cumsum_rows(tril, _now(log_f[i])) for i in idx]
    yield
    diag = [_dot_nt(_now(qb[i]).astype(BF16), _now(kb[i]).astype(BF16)) for i in idx]
    scores = [jnp.where(level_of == -2, diag[i], 0.0) for i in idx]
    for level in range(n_levels):
        yield
        x16 = []
        for i in idx:
            base = _interleave_halves(_now(kb[i]), _now(qb[i]), level, t_len)
            if level == 0:
                f_i = _now(f[i])
                x = base * _interleave_halves(jnp.ones_like(f_i), f_i, 0, t_len)
            else:
                ref = _block_rows(a[i], level, t_len, (1 << level) - 1)
                x = base * _exp_neg(jnp.abs(a[i] - ref))
            x16.append(x.astype(BF16))
        part = [_dot_nt(x16[i], x16[i]) for i in idx]
        scores = [jnp.where(level_of == level, part[i], scores[i]) for i in idx]
    yield
    last = slice(t_len - 1, t_len)
    q_in = [(_now(qb[i]) * jnp.exp(a[i])).astype(BF16) for i in idx]
    k_out = [(_now(kb[i]) * jnp.exp(a[i][last] - a[i])).astype(BF16) for i in idx]
    inter = [_dot_nt(q_in[i], s_old_t[i].astype(BF16)) for i in idx]
    iv16 = [_now(iv[i]).astype(BF16) for i in idx]
    intra = [_dot(scores[i].astype(BF16), iv16[i]) for i in idx]
    kv = [_dot_tn(iv16[i], k_out[i]) for i in idx]
    yield
    s_new_t = [jnp.exp(a[i][last]) * s_old_t[i] + kv[i] for i in idx]
    o = [inter[i] + intra[i] for i in idx]
    rms = [lax.rsqrt(jnp.mean(o[i] * o[i], axis=1, keepdims=True) + RMS_EPS) for i in idx]
    out = [_now(gate[i]) * (o[i] * rms[i] * g_norm[i]) for i in idx]
    return out, s_new_t


def _load_state(c0_ref, n0_ref, m0_ref, s0_ref, c_ref, n_ref, m_ref, s_ref, n_seq_blk):
    shared = c0_ref.shape[0] == 1 and n_seq_blk > 1
    for s in range(n_seq_blk):
        src = 0 if shared else s
        c_ref[s] = c0_ref[src]
        n_ref[s] = n0_ref[src]
        m_ref[s] = m0_ref[src]
        for h in range(N_HEADS):
            s_ref[s, h] = s0_ref[src, h].T


def _finish_state(s_ref, n_seq_blk):
    for s in range(n_seq_blk):
        for h in range(N_HEADS):
            s_ref[s, h] = s_ref[s, h].T


def _act_cols(group, h):
    base = group * D_GROUP if group < KB_GROUP else D_PROJ + (group - KB_GROUP) * D_GROUP
    return slice(base + h * D_HEAD, base + (h + 1) * D_HEAD)


def _mixer_body(p_ref, bf_ref, ga_ref, gb_ref, lb_ref, c_ref, n_ref, m_ref, s_ref,
                *, t_len, n_seq_blk, activated, side_stages=None, late_proj=None):
    rows = lax.broadcasted_iota(jnp.int32, (t_len, t_len), 0)
    cols = lax.broadcasted_iota(jnp.int32, (t_len, t_len), 1)
    causal = cols <= rows
    tril = causal.astype(BF16)
    level_of = jnp.where(rows > cols, 31 - lax.clz(rows ^ cols), jnp.where(rows == cols, -2, -1))
    lane = lax.broadcasted_iota(jnp.int32, (t_len, LANES), 1)
    is_f = (lane >= N_HEADS) & (lane < 2 * N_HEADS)
    hd = lambda j, h: slice(j * D_GROUP + h * D_HEAD, j * D_GROUP + (h + 1) * D_HEAD)

    units = [(s, h) for s in range(n_seq_blk) for h in range(N_HEADS)]
    rs = lambda s: slice(s * t_len, (s + 1) * t_len)

    z_seq, zt_seq, n_seq, m_seq = [], [], [], []
    for s in range(n_seq_blk):
        gates = p_ref[rs(s), GATE_COL:GATE_COL + LANES]
        log_f = jnp.where(is_f, jax.nn.log_sigmoid(gates + bf_ref[...]), 0.0)
        cum_f = _cumsum_rows(tril, log_f)
        z_seq.append(jnp.where(is_f, cum_f, jnp.where(lane < N_HEADS, gates, 0.0)))
        zt_seq.append(z_seq[s].T)
        n_seq.append(n_ref[s])
        m_seq.append(m_ref[s])
    computed, late_vals = {}, {}

    def act(group):
        source = 5 if group >= KB_GROUP else group
        if late_proj is not None and source in LATE_GROUPS:
            def thunk(s, h):
                off, col = divmod(h * D_HEAD, LATE_STAGE_COLS)
                return lambda: late_vals[source, off * LATE_STAGE_COLS][rs(s), col:col + D_HEAD]
            return [thunk(s, h) for s, h in units]
        if activated:
            return [lambda s=s, h=h: p_ref[rs(s), _act_cols(group, h)] for s, h in units]
        for s, h in units:
            if (source, s, h) not in computed:
                computed[source, s, h] = _activate(source, p_ref[rs(s), hd(source, h)],
                                                   lb_ref[:, hd(0, h)])
        return [computed[source, s, h][group] for s, h in units]

    lane_row = lane[:1, :]
    m0_rows = []
    for s in range(n_seq_blk):
        m0_row = jnp.zeros((1, LANES), F32)
        for h in range(N_HEADS):
            m0_row = jnp.where(lane_row == N_HEADS + h, m_seq[s][:, h:h + 1], m0_row)
        m0_rows.append(m0_row)
    mlstm = _mlstm_units(
        q=act(0), k=act(1), v=act(2), gate=act(3), z=z_seq, zt=zt_seq, m0_row=m0_rows,
        c_old=[c_ref[s, h] for s, h in units], n_old=[n_seq[s][h:h + 1, :] for s, h in units],
        g_norm=[ga_ref[:, hd(0, h)] for s, h in units], causal=causal, lane=lane, t_len=t_len)
    hgrn = _hgrn_units(
        qb=act(4), log_f=act(5), f=act(F_GROUP), kb=act(KB_GROUP), iv=act(6), gate=act(7),
        s_old_t=[s_ref[s, h] for s, h in units],
        g_norm=[gb_ref[:, hd(0, h)] for s, h in units], tril=tril, level_of=level_of, t_len=t_len)
    gens, per_round = [mlstm, hgrn], [1, 2]
    if late_proj is not None:
        late = _late_proj_stages(*late_proj, lb_ref, late_vals)
        gens, per_round = [late] + gens, [LATE_STAGES_PER_ROUND] + per_round
    if side_stages is not None:
        gens, per_round = gens + [side_stages], per_round + [SIDE_STAGES_PER_ROUND]
    results = _run_interleaved(gens, per_round)
    outs, c_new, n_new, m_new = results[gens.index(mlstm)]
    outs_b, s_new_t = results[gens.index(hgrn)]

    for i, (s, h) in enumerate(units):
        c_ref[s, h] = c_new[i]
        s_ref[s, h] = s_new_t[i]
    head_lane = lax.broadcasted_iota(jnp.int32, (1, N_HEADS), 1)
    for s in range(n_seq_blk):
        n_ref[s] = jnp.concatenate(n_new[s * N_HEADS:(s + 1) * N_HEADS], axis=0)
        m_row = m_seq[s]
        for h in range(N_HEADS):
            m_row = jnp.where(head_lane == h, m_new[s * N_HEADS + h], m_row)
        m_ref[s] = m_row
    return outs, outs_b


def _mixer_kernel(p_ref, c0_ref, n0_ref, m0_ref, s0_ref, bf_ref, ga_ref, gb_ref, lb_ref,
                  mix_ref, c_ref, n_ref, m_ref, s_ref, *, t_len, n_chunks, n_seq_blk):
    chunk = pl.program_id(1)

    @pl.when(chunk == 0)
    def _():
        _load_state(c0_ref, n0_ref, m0_ref, s0_ref, c_ref, n_ref, m_ref, s_ref, n_seq_blk)

    outs_a, outs_b = _mixer_body(p_ref, bf_ref, ga_ref, gb_ref, lb_ref, c_ref, n_ref, m_ref, s_ref,
                                 t_len=t_len, n_seq_blk=n_seq_blk, activated=False)
    for i in range(n_seq_blk * N_HEADS):
        s, h = divmod(i, N_HEADS)
        rows = slice(s * t_len, (s + 1) * t_len)
        mix_ref[rows, h * D_HEAD:(h + 1) * D_HEAD] = outs_a[i].astype(mix_ref.dtype)
        mix_ref[rows, D_GROUP + h * D_HEAD:D_GROUP + (h + 1) * D_HEAD] = outs_b[i].astype(mix_ref.dtype)

    @pl.when(chunk == n_chunks - 1)
    def _():
        _finish_state(s_ref, n_seq_blk)


def _mixer(proj, c0, n0, m0, s0, bf_row, ga, gb, lb, *, n_seq, n_chunks, t_len, n_seq_blk,
           shared_init):
    assert not shared_init or n_seq_blk == 1
    assert n_chunks == 1 or n_seq_blk == 1
    nb = n_seq_blk
    init = (lambda b, c: (0, 0, 0, 0)) if shared_init else (lambda b, c: (b, 0, 0, 0))
    init3 = (lambda b, c: (0, 0, 0)) if shared_init else (lambda b, c: (b, 0, 0))
    const = lambda b, c: (0, 0)
    state4 = pl.BlockSpec((nb, N_HEADS, D_HEAD, D_HEAD), lambda b, c: (b, 0, 0, 0))
    return pl.pallas_call(
        functools.partial(_mixer_kernel, t_len=t_len, n_chunks=n_chunks, n_seq_blk=nb),
        grid=(n_seq // nb, n_chunks),
        in_specs=[
            pl.BlockSpec((nb * t_len, D_PROJ), lambda b, c: (b * n_chunks + c, 0)),
            pl.BlockSpec((nb, N_HEADS, D_HEAD, D_HEAD), init),
            pl.BlockSpec((nb, N_HEADS, D_HEAD), init3),
            pl.BlockSpec((nb, 1, N_HEADS), init3),
            pl.BlockSpec((nb, N_HEADS, D_HEAD, D_HEAD), init),
            pl.BlockSpec((1, LANES), const),
            pl.BlockSpec((1, D_GROUP), const),
            pl.BlockSpec((1, D_GROUP), const),
            pl.BlockSpec((1, D_GROUP), const),
        ],
        out_specs=[
            pl.BlockSpec((nb * t_len, D_MODEL), lambda b, c: (b * n_chunks + c, 0)),
            state4,
            pl.BlockSpec((nb, N_HEADS, D_HEAD), lambda b, c: (b, 0, 0)),
            pl.BlockSpec((nb, 1, N_HEADS), lambda b, c: (b, 0, 0)),
            state4,
        ],
        out_shape=[
            jax.ShapeDtypeStruct((n_seq * n_chunks * t_len, D_MODEL), BF16),
            jax.ShapeDtypeStruct((n_seq, N_HEADS, D_HEAD, D_HEAD), F32),
            jax.ShapeDtypeStruct((n_seq, N_HEADS, D_HEAD), F32),
            jax.ShapeDtypeStruct((n_seq, 1, N_HEADS), F32),
            jax.ShapeDtypeStruct((n_seq, N_HEADS, D_HEAD, D_HEAD), F32),
        ],
        compiler_params=pltpu.CompilerParams(
            dimension_semantics=("arbitrary", "arbitrary"), vmem_limit_bytes=VMEM_LIMIT),
        name=f"mixer_t{t_len}",
    )(proj, c0, n0, m0, s0, bf_row, ga, gb, lb)


def _meta_kernel(x_ref, ge_ref, be_ref, win_ref, bin_ref, bf_ref, ga_ref, gb_ref, lb_ref,
                 wout_ref, bout_ref, g1_ref, b1_ref, wu_ref, bu_ref,
                 c_ref, n_ref, m_ref, s_ref, conv_ref, proj_scr):
    t_len = x_ref.shape[0]
    xn = _layer_norm(x_ref[...], ge_ref[...], be_ref[...])
    proj_scr[...] = _dot(xn.astype(BF16), win_ref[...]) + bin_ref[...]
    for ref in (c_ref, n_ref, m_ref, s_ref):
        ref[...] = jnp.zeros(ref.shape, ref.dtype)
    outs_a, outs_b = _mixer_body(proj_scr, bf_ref, ga_ref, gb_ref, lb_ref, c_ref, n_ref, m_ref, s_ref,
                                 t_len=t_len, n_seq_blk=1, activated=False)
    _finish_state(s_ref, 1)
    mix = jnp.concatenate([o.astype(BF16) for o in outs_a + outs_b], axis=1)
    y = _dot(mix, wout_ref[...]) + bout_ref[...]
    x1 = _layer_norm(ALPHA * xn + y, g1_ref[...], b1_ref[...])
    u = _dot(x1.astype(BF16), wu_ref[...]) + bu_ref[...]
    conv_ref[0] = u[t_len - (CONV_W - 1):, :]


def _meta_state(x, ln_e_g, ln_e_b, w_in, b_in, bf_row, ga, gb, lb, w_out, b_out, ln_g, ln_b,
                w_up, b_up):
    t_len = x.shape[0]
    full = lambda shape: _resident(shape, lambda i: (0,) * len(shape))
    vec, grp = full((1, D_MODEL)), full((1, D_GROUP))
    state4 = (1, N_HEADS, D_HEAD, D_HEAD)
    return pl.pallas_call(
        _meta_kernel,
        grid=(1,),
        in_specs=[full((t_len, D_MODEL)), vec, vec, full((D_MODEL, D_PROJ)), full((1, D_PROJ)),
                  full((1, LANES)), grp, grp, grp, full((D_MODEL, D_MODEL)), vec, vec, vec,
                  full((D_MODEL, D_FF)), full((1, D_FF))],
        out_specs=[pl.BlockSpec(shape, lambda i, rank=len(shape): (0,) * rank)
                   for shape in (state4, (1, N_HEADS, D_HEAD), (1, 1, N_HEADS), state4,
                                 (1, CONV_W - 1, D_FF))],
        out_shape=[
            jax.ShapeDtypeStruct(state4, F32),
            jax.ShapeDtypeStruct((1, N_HEADS, D_HEAD), F32),
            jax.ShapeDtypeStruct((1, 1, N_HEADS), F32),
            jax.ShapeDtypeStruct(state4, F32),
            jax.ShapeDtypeStruct((1, CONV_W - 1, D_FF), F32),
        ],
        scratch_shapes=[pltpu.VMEM((t_len, D_PROJ), F32)],
        compiler_params=pltpu.CompilerParams(
            dimension_semantics=("arbitrary",), vmem_limit_bytes=VMEM_LIMIT),
        name="meta_state",
    )(x, ln_e_g, ln_e_b, w_in, b_in, bf_row, ga, gb, lb, w_out, b_out, ln_g, ln_b, w_up, b_up)


def _in_proj_stages(x_ref, g_ref, b_ref, w_ref, bias_ref, lb_ref, act_ref, xn_ref, x16_ref):
    assert D_GROUP % IN_PROJ_STAGE_COLS == 0
    xn = _layer_norm(x_ref[...].reshape(xn_ref.shape), g_ref[...], b_ref[...])
    xn_ref[...] = xn
    x16 = xn.astype(BF16)
    x16_ref[...] = x16
    starts = [lo for lo in range(0, GATE_COL, IN_PROJ_STAGE_COLS) if lo // D_GROUP not in LATE_GROUPS]
    for lo in starts + [GATE_COL]:
        hi = min(lo + IN_PROJ_STAGE_COLS, D_PROJ)
        yield
        block = _dot(x16, w_ref[:, lo:hi]) + bias_ref[:, lo:hi]
        group, off = divmod(lo, D_GROUP)
        if lo >= GATE_COL:
            act_ref[:, lo:hi] = block
            continue
        for dst, val in _activate(group, block, lb_ref[:, off:off + hi - lo]).items():
            base = _act_cols(dst, 0).start + off
            act_ref[:, base:base + hi - lo] = val


def _prompt_kernel(x0_ref, xnext_ref, ge_ref, be_ref, win_ref, bin_ref, c0_ref, n0_ref, m0_ref, s0_ref,
                   bf_ref, ga_ref, gb_ref, lb_ref, wout_ref, bout_ref, g1_ref, b1_ref,
                   x1_ref, c_ref, n_ref, m_ref, s_ref,
                   proj_scr, xn_scr, x16_scr, proj_alt, xn_alt, x16_alt,
                   *, t_len, n_chunks, n_seq_blk):
    chunk = pl.program_id(1)
    step = pl.program_id(0) * n_chunks + chunk

    @pl.when(step == 0)
    def _():
        first = _in_proj_stages(x0_ref, ge_ref, be_ref, win_ref, bin_ref, lb_ref,
                                proj_scr, xn_scr, x16_scr)
        _run_interleaved([first], [1])

    @pl.when(chunk == 0)
    def _():
        _load_state(c0_ref, n0_ref, m0_ref, s0_ref, c_ref, n_ref, m_ref, s_ref, n_seq_blk)

    def tile(proj_cur, xn_cur, x16_cur, proj_next, xn_next, x16_next):
        next_proj = _in_proj_stages(xnext_ref, ge_ref, be_ref, win_ref, bin_ref, lb_ref,
                                    proj_next, xn_next, x16_next)
        outs_a, outs_b = _mixer_body(
            proj_cur, bf_ref, ga_ref, gb_ref, lb_ref, c_ref, n_ref, m_ref, s_ref,
            t_len=t_len, n_seq_blk=n_seq_blk, activated=True, side_stages=next_proj,
            late_proj=(x16_cur, win_ref, bin_ref))
        mix = jnp.concatenate(
            [jnp.concatenate([o.astype(BF16) for o in outs_a[s * N_HEADS:(s + 1) * N_HEADS]
                              + outs_b[s * N_HEADS:(s + 1) * N_HEADS]], axis=1)
             for s in range(n_seq_blk)], axis=0)
        y = _dot(mix, wout_ref[...]) + bout_ref[...]
        x1 = _layer_norm(ALPHA * xn_cur[...] + y, g1_ref[...], b1_ref[...])
        x1_ref[...] = x1.reshape(x1_ref.shape)

    @pl.when(step % 2 == 0)
    def _():
        tile(proj_scr, xn_scr, x16_scr, proj_alt, xn_alt, x16_alt)

    @pl.when(step % 2 == 1)
    def _():
        tile(proj_alt, xn_alt, x16_alt, proj_scr, xn_scr, x16_scr)

    @pl.when(chunk == n_chunks - 1)
    def _():
        _finish_state(s_ref, n_seq_blk)


def _prompt_mixer(x, ln_e_g, ln_e_b, w_in, b_in, c0, n0, m0, s0, bf_row, ga, gb, lb,
                  w_out, b_out, ln_g, ln_b, *, n_chunks, t_len, n_seq_blk):
    n_seq = x.shape[0]
    nb = n_seq_blk
    n_tiles = (n_seq // nb) * n_chunks
    const = lambda b, c: (0, 0)
    init4 = lambda b, c: (0, 0, 0, 0)
    init3 = lambda b, c: (0, 0, 0)
    vec = pl.BlockSpec((1, D_MODEL), const)
    grp = pl.BlockSpec((1, D_GROUP), const)
    state4 = pl.BlockSpec((nb, N_HEADS, D_HEAD, D_HEAD), lambda b, c: (b, 0, 0, 0))

    def next_tile(b, c):
        nxt = jnp.minimum(b * n_chunks + c + 1, n_tiles - 1)
        return (nxt // n_chunks, nxt % n_chunks, 0)

    return pl.pallas_call(
        functools.partial(_prompt_kernel, t_len=t_len, n_chunks=n_chunks, n_seq_blk=nb),
        grid=(n_seq // nb, n_chunks),
        in_specs=[
            _resident((nb, t_len, D_MODEL), init3),
            pl.BlockSpec((nb, t_len, D_MODEL), next_tile),
            vec, vec,
            _resident((D_MODEL, D_PROJ), const),
            pl.BlockSpec((1, D_PROJ), const),
            pl.BlockSpec((1, N_HEADS, D_HEAD, D_HEAD), init4),
            pl.BlockSpec((1, N_HEADS, D_HEAD), init3),
            pl.BlockSpec((1, 1, N_HEADS), init3),
            pl.BlockSpec((1, N_HEADS, D_HEAD, D_HEAD), init4),
            pl.BlockSpec((1, LANES), const),
            grp, grp, grp,
            _resident((D_MODEL, D_MODEL), const),
            vec, vec, vec,
        ],
        out_specs=[
            pl.BlockSpec((nb, t_len, D_MODEL), lambda b, c: (b, c, 0)),
            state4,
            pl.BlockSpec((nb, N_HEADS, D_HEAD), lambda b, c: (b, 0, 0)),
            pl.BlockSpec((nb, 1, N_HEADS), lambda b, c: (b, 0, 0)),
            state4,
        ],
        out_shape=[
            jax.ShapeDtypeStruct(x.shape, F32),
            jax.ShapeDtypeStruct((n_seq, N_HEADS, D_HEAD, D_HEAD), F32),
            jax.ShapeDtypeStruct((n_seq, N_HEADS, D_HEAD), F32),
            jax.ShapeDtypeStruct((n_seq, 1, N_HEADS), F32),
            jax.ShapeDtypeStruct((n_seq, N_HEADS, D_HEAD, D_HEAD), F32),
        ],
        scratch_shapes=2 * [pltpu.VMEM((nb * t_len, D_ACT), F32), pltpu.VMEM((nb * t_len, D_MODEL), F32),
                            pltpu.VMEM((nb * t_len, D_MODEL), BF16)],
        compiler_params=pltpu.CompilerParams(
            dimension_semantics=("arbitrary", "arbitrary"), vmem_limit_bytes=VMEM_LIMIT),
        name="prompt_mixer",
    )(x, x, ln_e_g, ln_e_b, w_in, b_in, c0, n0, m0, s0, bf_row, ga, gb, lb, w_out, b_out, ln_g, ln_b)


def _ffn_kernel(x_ref, *refs, n_seq_blk, t_len):
    _ffn_tile(x_ref[...], *refs, n_seq_blk=n_seq_blk, t_len=t_len)


def _out_proj_ffn_kernel(x_ref, mix_ref, ge_ref, be_ref, wo_ref, bo_ref, g1_ref, b1_ref, *refs,
                         n_seq_blk, t_len):
    xn = _layer_norm(x_ref[...], ge_ref[...], be_ref[...])
    y = _dot(mix_ref[...], wo_ref[...]) + bo_ref[...]
    x1 = _layer_norm(ALPHA * xn + y, g1_ref[...], b1_ref[...])
    _ffn_tile(x1, *refs, n_seq_blk=n_seq_blk, t_len=t_len)


def _ffn_tile(x, cs_ref, wu_ref, bu_ref, wc_ref, bc_ref, wd_ref, bd_ref, g_ref, b_ref,
              y_ref, nc_ref, full_ref, *, n_seq_blk, t_len):
    hist = SUBLANES - (CONV_W - 1)

    @pl.when(pl.program_id(1) == 0)
    def _():
        full_ref[:, hist:SUBLANES, :] = cs_ref[...]

    up = _dot(x.astype(BF16), wu_ref[...]) + bu_ref[...]
    u = up[:, :D_FF].reshape(n_seq_blk, t_len, D_FF)
    gate = up[:, D_FF:].reshape(n_seq_blk, t_len, D_FF)
    full_ref[:, SUBLANES:SUBLANES + t_len, :] = u
    conv = bc_ref[...] + u * wc_ref[CONV_W - 1:CONV_W, :]
    for j in range(CONV_W - 1):
        conv = conv + full_ref[:, hist + j:hist + j + t_len, :] * wc_ref[j:j + 1, :]
    last = full_ref[:, hist + t_len:SUBLANES + t_len, :]
    nc_ref[...] = last
    full_ref[:, hist:SUBLANES, :] = last
    act = (conv * _sigmoid(conv) * gate).reshape(n_seq_blk * t_len, D_FF)
    ffn = _dot(act.astype(BF16), wd_ref[...]) + bd_ref[...]
    y_ref[...] = _layer_norm(ALPHA * x + ffn, g_ref[...], b_ref[...])


def _ffn(x, conv_state, w_up, b_up, w_conv, b_conv, w_down, b_down, ln_g, ln_b,
         *, n_seq, seq_len, n_seq_blk, t_len, shared_init, out_proj=None):
    n_t = seq_len // t_len
    rows = n_seq_blk * t_len
    const = lambda s, t: (0, 0)
    cs_map = (lambda s, t: (0, 0, 0)) if shared_init else (lambda s, t: (s, 0, 0))
    row = pl.BlockSpec((rows, D_MODEL), lambda s, t: (s * n_t + t, 0))
    vec = pl.BlockSpec((1, D_MODEL), const)
    body, lead_specs, lead_args = _ffn_kernel, [row], (x,)
    if out_proj is not None:
        body = _out_proj_ffn_kernel
        lead_specs = [row, row, vec, vec, _resident((D_MODEL, D_MODEL), const), vec, vec, vec]
        lead_args = (x,) + tuple(out_proj)
    return pl.pallas_call(
        functools.partial(body, n_seq_blk=n_seq_blk, t_len=t_len),
        grid=(n_seq // n_seq_blk, n_t),
        in_specs=lead_specs + [
            pl.BlockSpec((n_seq_blk, CONV_W - 1, D_FF), cs_map),
            _resident((D_MODEL, 2 * D_FF), const),
            pl.BlockSpec((1, 2 * D_FF), const),
            pl.BlockSpec((CONV_W, D_FF), const),
            pl.BlockSpec((1, D_FF), const),
            _resident((D_FF, D_MODEL), const),
            vec, vec, vec,
        ],
        out_specs=[row, pl.BlockSpec((n_seq_blk, CONV_W - 1, D_FF), lambda s, t: (s, 0, 0))],
        out_shape=[
            jax.ShapeDtypeStruct((n_seq * seq_len, D_MODEL), F32),
            jax.ShapeDtypeStruct((n_seq, CONV_W - 1, D_FF), F32),
        ],
        scratch_shapes=[pltpu.VMEM((n_seq_blk, SUBLANES + t_len, D_FF), F32)],
        compiler_params=pltpu.CompilerParams(
            dimension_semantics=("arbitrary", "arbitrary"), vmem_limit_bytes=VMEM_LIMIT),
        name=f"ffn_t{t_len}",
    )(*lead_args, conv_state, w_up, b_up, w_conv, b_conv, w_down, b_down, ln_g, ln_b)


def kernel(x_prompt, x_sample, state_mlstm_C, state_mlstm_n, state_mlstm_m, state_hgrn_S, state_ffn_conv, meta_tokens, ln_emb_g, ln_emb_b, w_in, b_in, b_fgate_a, g_norm_a, g_norm_b, hgrn_lb_logits, w_out, b_out, ln1_g, ln1_b, w_up, b_up, w_conv, b_conv, w_down, b_down, ln2_g, ln2_b):
    assert w_in.shape[0] == DEPTH == 1
    n_prompt, seq, _ = x_prompt.shape
    n_sample, dec_seq, _ = x_sample.shape
    row = lambda v: v.reshape(1, -1).astype(F32)

    gate0 = 4 * D_GROUP
    gate1 = gate0 + 2 * N_HEADS
    pad = D_PROJ - w_in.shape[2]
    w_in_p = _regroup_in_proj_weight(w_in[0].T, tm=256)
    b_in_p = jnp.concatenate(
        [b_in[0][:gate0], b_in[0][gate1:], b_in[0][gate0:gate1], jnp.zeros((pad,), b_in.dtype)]
    ).reshape(1, D_PROJ).astype(F32)
    bf_row = jnp.zeros((1, LANES), F32).at[0, N_HEADS:2 * N_HEADS].set(b_fgate_a[0].astype(F32))
    lb = jnp.cumsum(jax.nn.softmax(hgrn_lb_logits.astype(F32), axis=0), axis=0)[0].reshape(1, D_GROUP)
    ga, gb = row(g_norm_a[0]), row(g_norm_b[0])
    ln_e = (row(ln_emb_g), row(ln_emb_b))
    out_p = (w_out[0].astype(BF16), row(b_out[0]), row(ln1_g[0]), row(ln1_b[0]))
    ffn_p = (w_up[0].astype(BF16), row(b_up[0]), w_conv[0].astype(F32), row(b_conv[0]),
             w_down[0].astype(BF16), row(b_down[0]), row(ln2_g[0]), row(ln2_b[0]))

    c_m, n_m, m_m, s_m, conv_m = _meta_state(
        meta_tokens.astype(F32), *ln_e, w_in_p, b_in_p, bf_row, ga, gb, lb, *out_p, ffn_p[0], ffn_p[1])

    x1_p, c_p, n_p, m_p, s_p = _prompt_mixer(
        x_prompt.astype(F32), *ln_e, w_in_p, b_in_p, c_m, n_m, m_m, s_m, bf_row, ga, gb, lb, *out_p,
        n_chunks=seq // PROMPT_CHUNK, t_len=PROMPT_CHUNK, n_seq_blk=PROMPT_SEQS_PER_STEP)
    y_p, conv_p = _ffn(x1_p.reshape(n_prompt * seq, D_MODEL), conv_m, *ffn_p, n_seq=n_prompt,
                       seq_len=seq, n_seq_blk=1, t_len=512, shared_init=True)

    sample_state = (state_mlstm_C[0].astype(F32), state_mlstm_n[0].astype(F32),
                    state_mlstm_m[0].astype(F32).reshape(n_sample, 1, N_HEADS),
                    state_hgrn_S[0].astype(F32))
    xs_rows = x_sample.reshape(n_sample * dec_seq, D_MODEL).astype(F32)
    proj_s = _in_proj(xs_rows, *ln_e, w_in_p, b_in_p, tm=256)
    mix_s, c_s, n_s, m_s, s_s = _mixer(
        proj_s, *sample_state, bf_row, ga, gb, lb, n_seq=n_sample, n_chunks=1, t_len=dec_seq,
        n_seq_blk=8, shared_init=False)
    y_s, conv_s = _ffn(xs_rows, state_ffn_conv[0].astype(F32), *ffn_p, n_seq=n_sample,
                       seq_len=dec_seq, n_seq_blk=32, t_len=dec_seq, shared_init=False,
                       out_proj=(mix_s, *ln_e, *out_p))

    lead = lambda v: v[None]
    return (y_p.reshape(n_prompt, seq, D_MODEL), y_s.reshape(n_sample, dec_seq, D_MODEL),
            lead(c_p), lead(n_p), lead(m_p.reshape(n_prompt, N_HEADS)), lead(s_p), lead(conv_p),
            lead(c_s), lead(n_s), lead(m_s.reshape(n_sample, N_HEADS)), lead(s_s), lead(conv_s))
```
